```python
import math
import jax, jax.numpy as jnp
from jax import lax
import numpy as np

D_MODEL = 1024
BATCH = 8
SEQ = 2048
DEPTH = 1

HEAD_DIM = 64
N_Q_HEADS = 8
N_KV_HEADS = 2
GROUP = N_Q_HEADS // N_KV_HEADS
WINDOW = 128
BLOCK = 128
ATTN_WIDTH = N_Q_HEADS * HEAD_DIM
KV_WIDTH = N_KV_HEADS * HEAD_DIM
CONV_CHANNELS = 512
CONV_GROUPS = 8
CONV_WIDTH = 31
N_BRANCHES = 2
Q_OFF = 0
K_OFF = Q_OFF + ATTN_WIDTH
V_OFF = K_OFF + KV_WIDTH
GLU_OFF = V_OFF + KV_WIDTH
GATE_OFF = GLU_OFF + 2 * CONV_CHANNELS
IN_WIDTH = GATE_OFF + N_BRANCHES * D_MODEL
D_FF = int(math.ceil(8 * D_MODEL / 3 / 256)) * 256
EPS = 1e-5
NEG = -1e30

kernel_name = "hybrid_swa_sink_conformer_conv_gated"


def rmsnorm(x, g):
    xf = x.astype(jnp.float32)
    y = xf * lax.rsqrt(jnp.mean(xf * xf, axis=-1, keepdims=True) + EPS)
    return (y * g.astype(jnp.float32)).astype(x.dtype)


def layernorm(x, g, b):
    xf = x.astype(jnp.float32)
    mu = jnp.mean(xf, axis=-1, keepdims=True)
    xc = xf - mu
    var = jnp.mean(xc * xc, axis=-1, keepdims=True)
    y = xc * lax.rsqrt(var + EPS) * g.astype(jnp.float32) + b.astype(jnp.float32)
    return y.astype(x.dtype)


def sliding_window_attention(q, k, v, sinks):
    B, S = q.shape[0], q.shape[1]
    nb = S // BLOCK
    qb = q.reshape(B, nb, BLOCK, N_KV_HEADS, GROUP, HEAD_DIM)

    def band(t):
        padded = jnp.pad(t, ((0, 0), (BLOCK, 0), (0, 0), (0, 0)))
        prev = padded[:, :S].reshape(B, nb, BLOCK, N_KV_HEADS, HEAD_DIM)
        cur = t.reshape(B, nb, BLOCK, N_KV_HEADS, HEAD_DIM)
        return jnp.concatenate([prev, cur], axis=2)

    kb = band(k)
    vb = band(v)
    scale = HEAD_DIM ** -0.5
    s = jnp.einsum('bnqhgd,bnkhd->bnhgqk', qb, kb).astype(jnp.float32) * scale
    qi = jnp.arange(BLOCK)[:, None]
    kj = jnp.arange(2 * BLOCK)[None, :]
    diff = qi + BLOCK - kj
    kpos = jnp.arange(nb)[:, None, None] * BLOCK - BLOCK + kj[None]
    valid = (diff >= 0)[None] & (diff < WINDOW)[None] & (kpos >= 0)
    s = jnp.where(valid[None, :, None, None], s, NEG)
    sink_col = jnp.broadcast_to(
        sinks.astype(jnp.float32).reshape(1, 1, N_KV_HEADS, GROUP, 1, 1),
        s.shape[:-1] + (1,))
    p = jax.nn.softmax(jnp.concatenate([s, sink_col], axis=-1), axis=-1)[..., :-1]
    o = jnp.einsum('bnhgqk,bnkhd->bnqhgd', p.astype(v.dtype), vb)
    return o.reshape(B, S, ATTN_WIDTH)


def conformer_conv(u, conv_w, conv_b, ln_g, ln_b):
    a, b = jnp.split(u, 2, axis=-1)
    z = a * jax.nn.sigmoid(b)
    z = lax.conv_general_dilated(
        z, conv_w[:, None, :].astype(z.dtype),
        window_strides=(1,), padding=[(CONV_WIDTH - 1, 0)],
        dimension_numbers=('NWC', 'WIO', 'NWC'),
        feature_group_count=CONV_CHANNELS) + conv_b
    z = layernorm(z, ln_g, ln_b)
    return jax.nn.silu(z)


def _fwd_setup_inputs(seed: int = 0) -> dict:
    key = jax.random.key(seed)
    ks = jax.random.split(key, 20)
    L, D, C = DEPTH, D_MODEL, CONV_CHANNELS
    nrm = lambda k, shape, fan_in: jax.random.normal(k, shape, jnp.float32) * fan_in ** -0.5
    gain = lambda k, shape: 1.0 + 0.01 * jax.random.normal(k, shape, jnp.float32)
    small = lambda k, shape: 0.01 * jax.random.normal(k, shape, jnp.float32)
    return {
        "x": jax.random.normal(ks[0], (BATCH, SEQ, D), jnp.float32),
        "g_mix_norm": gain(ks[1], (L, D)),
        "w_in": nrm(ks[2], (L, D, IN_WIDTH), D),
        "b_in": small(ks[3], (L, IN_WIDTH)),
        "sinks": 0.5 * jax.random.normal(ks[4], (L, N_Q_HEADS), jnp.float32),
        "conv_w": nrm(ks[5], (L, CONV_WIDTH, C), CONV_WIDTH),
        "conv_b": small(ks[6], (L, C)),
        "ln_g": gain(ks[7], (L, C)),
        "ln_b": small(ks[8], (L, C)),
        "w_attn_proj": nrm(ks[9], (L, ATTN_WIDTH, D), ATTN_WIDTH),
        "w_conv_proj": nrm(ks[10], (L, C, D), C),
        "b_conv_proj": small(ks[11], (L, D)),
        "w_out": nrm(ks[12], (L, D, D), D),
        "g_ffn_norm": gain(ks[13], (L, D)),
        "w_ffn_in": nrm(ks[14], (L, D, 2 * D_FF), D),
        "w_ffn_down": nrm(ks[15], (L, D_FF, D), D_FF),
        "g_final": gain(ks[16], (D,)),
    }


def _fwd_reference(x, g_mix_norm, w_in, b_in, sinks, conv_w, conv_b, ln_g, ln_b,
              w_attn_proj, w_conv_proj, b_conv_proj, w_out, g_ffn_norm,
              w_ffn_in, w_ffn_down, g_final):
    B, S, D = x.shape
    for l in range(DEPTH):
        h = rmsnorm(x, g_mix_norm[l])
        proj = h @ w_in[l] + b_in[l]
        q = proj[..., Q_OFF:K_OFF].reshape(B, S, N_Q_HEADS, HEAD_DIM)
        k = proj[..., K_OFF:V_OFF].reshape(B, S, N_KV_HEADS, HEAD_DIM)
        v = proj[..., V_OFF:GLU_OFF].reshape(B, S, N_KV_HEADS, HEAD_DIM)
        glu_in = proj[..., GLU_OFF:GATE_OFF]
        gates = jax.nn.sigmoid(proj[..., GATE_OFF:].reshape(B, S, N_BRANCHES, D))

        y_attn = sliding_window_attention(q, k, v, sinks[l]) @ w_attn_proj[l]
        y_conv = conformer_conv(glu_in, conv_w[l], conv_b[l], ln_g[l], ln_b[l]) @ w_conv_proj[l] + b_conv_proj[l]
        merged = gates[:, :, 0] * y_attn + gates[:, :, 1] * y_conv
        x = x + merged @ w_out[l]

        h2 = rmsnorm(x, g_ffn_norm[l])
        gu = h2 @ w_ffn_in[l]
        gate, up = gu[..., :D_FF], gu[..., D_FF:]
        x = x + (jax.nn.silu(gate) * up) @ w_ffn_down[l]
    return rmsnorm(x, g_final)


import jax as _jax
import jax.numpy as _jnp

TWIN_FORMAT = 'train_step'
FWD_PARAMS = ['x', 'g_mix_norm', 'w_in', 'b_in', 'sinks', 'conv_w', 'conv_b', 'ln_g', 'ln_b', 'w_attn_proj', 'w_conv_proj', 'b_conv_proj', 'w_out', 'g_ffn_norm', 'w_ffn_in', 'w_ffn_down', 'g_final']
TWIN_WEIGHTS = ['g_mix_norm', 'w_in', 'b_in', 'sinks', 'conv_w', 'conv_b', 'ln_g', 'ln_b', 'w_attn_proj', 'w_conv_proj', 'b_conv_proj', 'w_out', 'g_ffn_norm', 'w_ffn_in', 'w_ffn_down', 'g_final']
TWIN_DIFF_INPUT = 'x'
TWIN_INPUTS = ['x', 'g_mix_norm', 'w_in', 'b_in', 'sinks', 'conv_w', 'conv_b', 'ln_g', 'ln_b', 'w_attn_proj', 'w_conv_proj', 'b_conv_proj', 'w_out', 'g_ffn_norm', 'w_ffn_in', 'w_ffn_down', 'g_final', 'loss_target', 'm_g_mix_norm', 'm_w_in', 'm_b_in', 'm_sinks', 'm_conv_w', 'm_conv_b', 'm_ln_g', 'm_ln_b', 'm_w_attn_proj', 'm_w_conv_proj', 'm_b_conv_proj', 'm_w_out', 'm_g_ffn_norm', 'm_w_ffn_in', 'm_w_ffn_down', 'm_g_final', 'v_g_mix_norm', 'v_w_in', 'v_b_in', 'v_sinks', 'v_conv_w', 'v_conv_b', 'v_ln_g', 'v_ln_b', 'v_w_attn_proj', 'v_w_conv_proj', 'v_b_conv_proj', 'v_w_out', 'v_g_ffn_norm', 'v_w_ffn_in', 'v_w_ffn_down', 'v_g_final']
TWIN_OUTPUTS = ['loss', 'grad_x', 'grad_g_mix_norm', 'grad_w_in', 'grad_b_in', 'grad_sinks', 'grad_conv_w', 'grad_conv_b', 'grad_ln_g', 'grad_ln_b', 'grad_w_attn_proj', 'grad_w_conv_proj', 'grad_b_conv_proj', 'grad_w_out', 'grad_g_ffn_norm', 'grad_w_ffn_in', 'grad_w_ffn_down', 'grad_g_final', 'delta_g_mix_norm', 'delta_w_in', 'delta_b_in', 'delta_sinks', 'delta_conv_w', 'delta_conv_b', 'delta_ln_g', 'delta_ln_b', 'delta_w_attn_proj', 'delta_w_conv_proj', 'delta_b_conv_proj', 'delta_w_out', 'delta_g_ffn_norm', 'delta_w_ffn_in', 'delta_w_ffn_down', 'delta_g_final', 'new_m_g_mix_norm', 'new_m_w_in', 'new_m_b_in', 'new_m_sinks', 'new_m_conv_w', 'new_m_conv_b', 'new_m_ln_g', 'new_m_ln_b', 'new_m_w_attn_proj', 'new_m_w_conv_proj', 'new_m_b_conv_proj', 'new_m_w_out', 'new_m_g_ffn_norm', 'new_m_w_ffn_in', 'new_m_w_ffn_down', 'new_m_g_final', 'new_v_g_mix_norm', 'new_v_w_in', 'new_v_b_in', 'new_v_sinks', 'new_v_conv_w', 'new_v_conv_b', 'new_v_ln_g', 'new_v_ln_b', 'new_v_w_attn_proj', 'new_v_w_conv_proj', 'new_v_b_conv_proj', 'new_v_w_out', 'new_v_g_ffn_norm', 'new_v_w_ffn_in', 'new_v_w_ffn_down', 'new_v_g_final']
TWIN_LEAF_KINDS = {'loss': 'loss', 'grad_x': 'grad_x', 'grad_g_mix_norm': 'grad_w', 'grad_w_in': 'grad_w', 'grad_b_in': 'grad_w', 'grad_sinks': 'grad_w', 'grad_conv_w': 'grad_w', 'grad_conv_b': 'grad_w', 'grad_ln_g': 'grad_w', 'grad_ln_b': 'grad_w', 'grad_w_attn_proj': 'grad_w', 'grad_w_conv_proj': 'grad_w', 'grad_b_conv_proj': 'grad_w', 'grad_w_out': 'grad_w', 'grad_g_ffn_norm': 'grad_w', 'grad_w_ffn_in': 'grad_w', 'grad_w_ffn_down': 'grad_w', 'grad_g_final': 'grad_w', 'delta_g_mix_norm': 'delta_w', 'delta_w_in': 'delta_w', 'delta_b_in': 'delta_w', 'delta_sinks': 'delta_w', 'delta_conv_w': 'delta_w', 'delta_conv_b': 'delta_w', 'delta_ln_g': 'delta_w', 'delta_ln_b': 'delta_w', 'delta_w_attn_proj': 'delta_w', 'delta_w_conv_proj': 'delta_w', 'delta_b_conv_proj': 'delta_w', 'delta_w_out': 'delta_w', 'delta_g_ffn_norm': 'delta_w', 'delta_w_ffn_in': 'delta_w', 'delta_w_ffn_down': 'delta_w', 'delta_g_final': 'delta_w', 'new_m_g_mix_norm': 'new_m', 'new_m_w_in': 'new_m', 'new_m_b_in': 'new_m', 'new_m_sinks': 'new_m', 'new_m_conv_w': 'new_m', 'new_m_conv_b': 'new_m', 'new_m_ln_g': 'new_m', 'new_m_ln_b': 'new_m', 'new_m_w_attn_proj': 'new_m', 'new_m_w_conv_proj': 'new_m', 'new_m_b_conv_proj': 'new_m', 'new_m_w_out': 'new_m', 'new_m_g_ffn_norm': 'new_m', 'new_m_w_ffn_in': 'new_m', 'new_m_w_ffn_down': 'new_m', 'new_m_g_final': 'new_m', 'new_v_g_mix_norm': 'new_v', 'new_v_w_in': 'new_v', 'new_v_b_in': 'new_v', 'new_v_sinks': 'new_v', 'new_v_conv_w': 'new_v', 'new_v_conv_b': 'new_v', 'new_v_ln_g': 'new_v', 'new_v_ln_b': 'new_v', 'new_v_w_attn_proj': 'new_v', 'new_v_w_conv_proj': 'new_v', 'new_v_b_conv_proj': 'new_v', 'new_v_w_out': 'new_v', 'new_v_g_ffn_norm': 'new_v', 'new_v_w_ffn_in': 'new_v', 'new_v_w_ffn_down': 'new_v', 'new_v_g_final': 'new_v'}


def _forward(args):
    return _fwd_reference(*[args[k] for k in FWD_PARAMS])


def _output_shape():
    out = _jax.eval_shape(lambda: _forward(_fwd_setup_inputs(0)))
    return out.shape, out.dtype

N_MICROBATCH = 1
ADAM_LR = 0.001
ADAM_B1 = 0.9
ADAM_B2 = 0.999
ADAM_EPS = 1e-08
ADAM_WD = 0.01
ADAM_STEP = 10
PER_EXAMPLE_BATCH_AXIS = {'x': 0, 'loss_target': 0}
SHARED_INPUTS = []
_WEIGHT_DTYPES = {'g_mix_norm': _jnp.float32, 'w_in': _jnp.float32, 'b_in': _jnp.float32, 'sinks': _jnp.float32, 'conv_w': _jnp.float32, 'conv_b': _jnp.float32, 'ln_g': _jnp.float32, 'ln_b': _jnp.float32, 'w_attn_proj': _jnp.float32, 'w_conv_proj': _jnp.float32, 'b_conv_proj': _jnp.float32, 'w_out': _jnp.float32, 'g_ffn_norm': _jnp.float32, 'w_ffn_in': _jnp.float32, 'w_ffn_down': _jnp.float32, 'g_final': _jnp.float32}
MOMENT_SCALE = {'g_mix_norm': 5.606357e-02, 'w_in': 2.909792e-02, 'b_in': 4.706168e-02, 'sinks': 2.131187e-02, 'conv_w': 6.321364e-02, 'conv_b': 1.326396e-01, 'ln_g': 7.522839e-02, 'ln_b': 6.322226e-02, 'w_attn_proj': 1.456594e-02, 'w_conv_proj': 4.313489e-02, 'b_conv_proj': 7.075303e-02, 'w_out': 4.559078e-02, 'g_ffn_norm': 8.957119e-02, 'w_ffn_in': 3.775048e-02, 'w_ffn_down': 6.155858e-02, 'g_final': 1.598420e+01}


def _to_microbatches(a, axis):
    t = _jnp.moveaxis(a, axis, 0)
    t = t.reshape((N_MICROBATCH, t.shape[0] // N_MICROBATCH) + t.shape[1:])
    return _jnp.moveaxis(t, 1, axis + 1)


def setup_inputs(seed: int = 0) -> dict:
    inp = _fwd_setup_inputs(seed)
    key = _jax.random.fold_in(_jax.random.key(seed), 7919)
    shape, _ = _output_shape()
    out = dict(inp)
    out["loss_target"] = _jax.random.normal(_jax.random.fold_in(key, 0), shape, _jnp.float32)
    for i, name in enumerate(TWIN_WEIGHTS):
        w = inp[name].astype(_jnp.float32)
        if MOMENT_SCALE is None:
            s = _jnp.sqrt(_jnp.mean(_jnp.square(w)) + 1e-30)
        else:
            s = MOMENT_SCALE[name]
        km, kv = _jax.random.split(_jax.random.fold_in(key, i + 1))
        out[name] = w
        out["m_" + name] = s * _jax.random.normal(km, w.shape, _jnp.float32)
        out["v_" + name] = (s * s) * _jax.random.uniform(kv, w.shape, _jnp.float32, 0.5, 1.5)
    if N_MICROBATCH > 1:
        for name, axis in PER_EXAMPLE_BATCH_AXIS.items():
            out[name] = _to_microbatches(out[name], axis)
    return {'x': out['x'], 'g_mix_norm': out['g_mix_norm'], 'w_in': out['w_in'], 'b_in': out['b_in'], 'sinks': out['sinks'], 'conv_w': out['conv_w'], 'conv_b': out['conv_b'], 'ln_g': out['ln_g'], 'ln_b': out['ln_b'], 'w_attn_proj': out['w_attn_proj'], 'w_conv_proj': out['w_conv_proj'], 'b_conv_proj': out['b_conv_proj'], 'w_out': out['w_out'], 'g_ffn_norm': out['g_ffn_norm'], 'w_ffn_in': out['w_ffn_in'], 'w_ffn_down': out['w_ffn_down'], 'g_final': out['g_final'], 'loss_target': out['loss_target'], 'm_g_mix_norm': out['m_g_mix_norm'], 'm_w_in': out['m_w_in'], 'm_b_in': out['m_b_in'], 'm_sinks': out['m_sinks'], 'm_conv_w': out['m_conv_w'], 'm_conv_b': out['m_conv_b'], 'm_ln_g': out['m_ln_g'], 'm_ln_b': out['m_ln_b'], 'm_w_attn_proj': out['m_w_attn_proj'], 'm_w_conv_proj': out['m_w_conv_proj'], 'm_b_conv_proj': out['m_b_conv_proj'], 'm_w_out': out['m_w_out'], 'm_g_ffn_norm': out['m_g_ffn_norm'], 'm_w_ffn_in': out['m_w_ffn_in'], 'm_w_ffn_down': out['m_w_ffn_down'], 'm_g_final': out['m_g_final'], 'v_g_mix_norm': out['v_g_mix_norm'], 'v_w_in': out['v_w_in'], 'v_b_in': out['v_b_in'], 'v_sinks': out['v_sinks'], 'v_conv_w': out['v_conv_w'], 'v_conv_b': out['v_conv_b'], 'v_ln_g': out['v_ln_g'], 'v_ln_b': out['v_ln_b'], 'v_w_attn_proj': out['v_w_attn_proj'], 'v_w_conv_proj': out['v_w_conv_proj'], 'v_b_conv_proj': out['v_b_conv_proj'], 'v_w_out': out['v_w_out'], 'v_g_ffn_norm': out['v_g_ffn_norm'], 'v_w_ffn_in': out['v_w_ffn_in'], 'v_w_ffn_down': out['v_w_ffn_down'], 'v_g_final': out['v_g_final']}


def _loss(weights, diff, rest, loss_target):
    with _jax.named_scope("forward"):
        args = {**rest, TWIN_DIFF_INPUT: diff, **{k: w.astype(_WEIGHT_DTYPES[k]) for k, w in weights.items()}}
        y = _forward(args)
    with _jax.named_scope("loss_head"):
        err = _jnp.square(y.astype(_jnp.float32) - loss_target)
        return 0.5 * _jnp.sum(_jnp.mean(err, axis=-1)) if err.ndim else 0.5 * err


def _adamw(w, g, m, v):
    m = ADAM_B1 * m + (1.0 - ADAM_B1) * g
    v = ADAM_B2 * v + (1.0 - ADAM_B2) * _jnp.square(g)
    m_hat = m / (1.0 - ADAM_B1 ** ADAM_STEP)
    v_hat = v / (1.0 - ADAM_B2 ** ADAM_STEP)
    delta = -ADAM_LR * (m_hat / (_jnp.sqrt(v_hat) + ADAM_EPS) + ADAM_WD * w)
    return delta, m, v


def reference(x, g_mix_norm, w_in, b_in, sinks, conv_w, conv_b, ln_g, ln_b, w_attn_proj, w_conv_proj, b_conv_proj, w_out, g_ffn_norm, w_ffn_in, w_ffn_down, g_final, loss_target, m_g_mix_norm, m_w_in, m_b_in, m_sinks, m_conv_w, m_conv_b, m_ln_g, m_ln_b, m_w_attn_proj, m_w_conv_proj, m_b_conv_proj, m_w_out, m_g_ffn_norm, m_w_ffn_in, m_w_ffn_down, m_g_final, v_g_mix_norm, v_w_in, v_b_in, v_sinks, v_conv_w, v_conv_b, v_ln_g, v_ln_b, v_w_attn_proj, v_w_conv_proj, v_b_conv_proj, v_w_out, v_g_ffn_norm, v_w_ffn_in, v_w_ffn_down, v_g_final):
    given = dict(x=x, g_mix_norm=g_mix_norm, w_in=w_in, b_in=b_in, sinks=sinks, conv_w=conv_w, conv_b=conv_b, ln_g=ln_g, ln_b=ln_b, w_attn_proj=w_attn_proj, w_conv_proj=w_conv_proj, b_conv_proj=b_conv_proj, w_out=w_out, g_ffn_norm=g_ffn_norm, w_ffn_in=w_ffn_in, w_ffn_down=w_ffn_down, g_final=g_final, loss_target=loss_target, m_g_mix_norm=m_g_mix_norm, m_w_in=m_w_in, m_b_in=m_b_in, m_sinks=m_sinks, m_conv_w=m_conv_w, m_conv_b=m_conv_b, m_ln_g=m_ln_g, m_ln_b=m_ln_b, m_w_attn_proj=m_w_attn_proj, m_w_conv_proj=m_w_conv_proj, m_b_conv_proj=m_b_conv_proj, m_w_out=m_w_out, m_g_ffn_norm=m_g_ffn_norm, m_w_ffn_in=m_w_ffn_in, m_w_ffn_down=m_w_ffn_down, m_g_final=m_g_final, v_g_mix_norm=v_g_mix_norm, v_w_in=v_w_in, v_b_in=v_b_in, v_sinks=v_sinks, v_conv_w=v_conv_w, v_conv_b=v_conv_b, v_ln_g=v_ln_g, v_ln_b=v_ln_b, v_w_attn_proj=v_w_attn_proj, v_w_conv_proj=v_w_conv_proj, v_b_conv_proj=v_b_conv_proj, v_w_out=v_w_out, v_g_ffn_norm=v_g_ffn_norm, v_w_ffn_in=v_w_ffn_in, v_w_ffn_down=v_w_ffn_down, v_g_final=v_g_final)
    weights = {n: given[n] for n in TWIN_WEIGHTS}
    shared = {n: given[n] for n in SHARED_INPUTS}
    per_example = {n: given[n] for n in ['x']}
    grad_fn = _jax.value_and_grad(_loss, argnums=(0, 1))

    def one_microbatch(ex, loss_target):
        ex = dict(ex)
        diff = ex.pop(TWIN_DIFF_INPUT)
        return grad_fn(weights, diff, {**shared, **ex}, loss_target)

    if N_MICROBATCH == 1:
        loss, (grad_w, grad_x) = one_microbatch(per_example, given["loss_target"])
    else:
        def body(carry, xs):
            loss_sum, grad_sum = carry
            l_k, (gw_k, gx_k) = one_microbatch(xs[0], xs[1])
            with _jax.named_scope("update"):
                return (loss_sum + l_k, _jax.tree.map(_jnp.add, grad_sum, gw_k)), gx_k

        init = (_jnp.zeros((), _jnp.float32), _jax.tree.map(_jnp.zeros_like, weights))
        (loss, grad_w), grad_x = _jax.lax.scan(body, init, (per_example, given["loss_target"]))
    with _jax.named_scope("update"):
        delta_w, new_m, new_v = {}, {}, {}
        for n in TWIN_WEIGHTS:
            delta_w[n], new_m[n], new_v[n] = _adamw(weights[n], grad_w[n], given["m_" + n], given["v_" + n])
    return (loss, grad_x, *[grad_w[n] for n in TWIN_WEIGHTS], *[delta_w[n] for n in TWIN_WEIGHTS],
            *[new_m[n] for n in TWIN_WEIGHTS], *[new_v[n] for n in TWIN_WEIGHTS])
```

```python
import functools

import jax
import jax.numpy as jnp
from jax import lax
from jax.experimental import pallas as pl
from jax.experimental.pallas import tpu as pltpu

F32 = jnp.float32
BF16 = jnp.bfloat16

T = 2048
D = 1024
HD = 64
NQ = 8
NKV = 2
GROUP = NQ // NKV
BLK = 128
AW = NQ * HD
KVW = NKV * HD
C = 512
KW = 31
QKVW = AW + 2 * KVW
GLU_OFF = QKVW
GATE_OFF = GLU_OFF + 2 * C
INW = GATE_OFF + 2 * D
DFF = 2816
EPS = 1e-5
NEG = -1e30
SCALE = HD ** -0.5
HALO = 32

ADAM_LR = 0.001
ADAM_B1 = 0.9
ADAM_B2 = 0.999
ADAM_EPS = 1e-08
ADAM_WD = 0.01
ADAM_STEP = 10

VMEM_LIMIT = 56 * 1024 * 1024
MESH = pl.DeviceIdType.MESH


def _params(*sem):
    return pltpu.CompilerParams(dimension_semantics=sem, vmem_limit_bytes=VMEM_LIMIT)


def _dot(a, b):
    return jnp.dot(a, b, preferred_element_type=F32)


def _dot_nt(a, b):
    return lax.dot_general(a, b, (((1,), (1,)), ((), ())), preferred_element_type=F32)


def _dot_tn(a, b):
    return lax.dot_general(a, b, (((0,), (0,)), ((), ())), preferred_element_type=F32)


def _sigmoid(v):
    return 1.0 / (1.0 + jnp.exp(-v))


def _rows(tm, n):
    return pl.BlockSpec((tm, n), lambda i: (i, 0))


def _whole(shape):
    return pl.BlockSpec(shape, lambda i: tuple(0 for _ in shape))


def _in_proj(x, g_mix, w_in, b_in):
    tm = 256

    def body(x_ref, g_ref, w_ref, b_ref, h_ref, qkv_ref, glu_ref, gl_ref):
        xv = x_ref[...]
        r = lax.rsqrt(jnp.mean(xv * xv, axis=-1, keepdims=True) + EPS)
        h = (xv * r * g_ref[...]).astype(BF16)
        h_ref[...] = h
        qkv_ref[...] = (_dot(h, w_ref[:, 0:GLU_OFF]) + b_ref[:, 0:GLU_OFF]).astype(BF16)
        glu_ref[...] = (_dot(h, w_ref[:, GLU_OFF:GATE_OFF]) + b_ref[:, GLU_OFF:GATE_OFF]).astype(BF16)
        gl_ref[...] = (_dot(h, w_ref[:, GATE_OFF:INW]) + b_ref[:, GATE_OFF:INW]).astype(BF16)

    return pl.pallas_call(
        body, name="in_proj", grid=(T // tm,),
        in_specs=[_rows(tm, D), _whole((1, D)), _whole((D, INW)), _whole((1, INW))],
        out_specs=[_rows(tm, D), _rows(tm, QKVW), _rows(tm, 2 * C), _rows(tm, 2 * D)],
        out_shape=[jax.ShapeDtypeStruct((T, D), BF16), jax.ShapeDtypeStruct((T, QKVW), BF16),
                   jax.ShapeDtypeStruct((T, 2 * C), BF16), jax.ShapeDtypeStruct((T, 2 * D), BF16)],
        compiler_params=_params("parallel"),
    )(x, g_mix, w_in, b_in)


def _attn_masks(i):
    qi = lax.broadcasted_iota(jnp.int32, (BLK, BLK), 0)
    kj = lax.broadcasted_iota(jnp.int32, (BLK, BLK), 1)
    mask_cur = kj <= qi
    mask_prev = jnp.logical_and(kj > qi, i > 0)
    return mask_cur, mask_prev


def _attn_fwd(qkv, sinks):
    def body(sink_ref, qkv_ref, o_ref, lse_ref):
        i = pl.program_id(0)
        r0 = pl.multiple_of(i * BLK, BLK)
        rp = pl.multiple_of(jnp.maximum(i - 1, 0) * BLK, BLK)
        mask_cur, mask_prev = _attn_masks(i)
        for h in range(NQ):
            kv = h // GROUP
            q = qkv_ref[pl.ds(r0, BLK), h * HD:(h + 1) * HD]
            kc = qkv_ref[pl.ds(r0, BLK), AW + kv * HD:AW + (kv + 1) * HD]
            kp = qkv_ref[pl.ds(rp, BLK), AW + kv * HD:AW + (kv + 1) * HD]
            vc = qkv_ref[pl.ds(r0, BLK), AW + KVW + kv * HD:AW + KVW + (kv + 1) * HD]
            vp = qkv_ref[pl.ds(rp, BLK), AW + KVW + kv * HD:AW + KVW + (kv + 1) * HD]
            sc = jnp.where(mask_cur, _dot_nt(q, kc) * SCALE, NEG)
            sp = jnp.where(mask_prev, _dot_nt(q, kp) * SCALE, NEG)
            sink = sink_ref[0, h]
            m = jnp.maximum(jnp.maximum(jnp.max(sc, axis=-1, keepdims=True),
                                        jnp.max(sp, axis=-1, keepdims=True)), sink)
            pc = jnp.exp(sc - m)
            pp = jnp.exp(sp - m)
            den = (jnp.sum(pc, axis=-1, keepdims=True) + jnp.sum(pp, axis=-1, keepdims=True)
                   + jnp.exp(sink - m))
            inv = 1.0 / den
            o = _dot((pc * inv).astype(BF16), vc) + _dot((pp * inv).astype(BF16), vp)
            o_ref[:, h * HD:(h + 1) * HD] = o.astype(BF16)
            lse_ref[:, h:h + 1] = m + jnp.log(den)

    return pl.pallas_call(
        body, name="attn_fwd", grid=(T // BLK,),
        in_specs=[pl.BlockSpec(memory_space=pltpu.SMEM), _whole((T, QKVW))],
        out_specs=[_rows(BLK, AW), _rows(BLK, NQ)],
        out_shape=[jax.ShapeDtypeStruct((T, AW), BF16), jax.ShapeDtypeStruct((T, NQ), F32)],
        compiler_params=_params("parallel"),
    )(sinks, qkv)


CONV_TM = 256
CONV_SUB = 32


def _glu(ab):
    a = ab[:, 0:C].astype(F32)
    b = ab[:, C:2 * C].astype(F32)
    return a * _sigmoid(b)


def _conv_fwd(glu, conv_w, conv_b, ln_g, ln_b):
    tm = CONV_TM

    def body(cur_ref, prev_ref, w_ref, cb_ref, g_ref, b_ref, u_ref, c_ref, zp_ref):
        i = pl.program_id(0)
        zprev = _glu(prev_ref[tm - HALO:tm, :])
        zp_ref[0:HALO, :] = jnp.where(i > 0, zprev, 0.0)
        zp_ref[HALO:HALO + tm, :] = _glu(cur_ref[...])
        for s in range(tm // CONV_SUB):
            base = HALO + s * CONV_SUB - (KW - 1)
            acc = jnp.broadcast_to(cb_ref[...], (CONV_SUB, C))
            for j in range(KW):
                acc = acc + w_ref[j:j + 1, :] * zp_ref[base + j:base + j + CONV_SUB, :]
            rows = slice(s * CONV_SUB, (s + 1) * CONV_SUB)
            u_ref[rows, :] = acc
            mu = jnp.mean(acc, axis=-1, keepdims=True)
            xc = acc - mu
            var = jnp.mean(xc * xc, axis=-1, keepdims=True)
            y = xc * lax.rsqrt(var + EPS) * g_ref[...] + b_ref[...]
            c_ref[rows, :] = (y * _sigmoid(y)).astype(BF16)

    return pl.pallas_call(
        body, name="conv_fwd", grid=(T // tm,),
        in_specs=[_rows(tm, 2 * C),
                  pl.BlockSpec((tm, 2 * C), lambda i: (jnp.maximum(i - 1, 0), 0)),
                  _whole((KW, C)), _whole((1, C)), _whole((1, C)), _whole((1, C))],
        out_specs=[_rows(tm, C), _rows(tm, C)],
        out_shape=[jax.ShapeDtypeStruct((T, C), F32), jax.ShapeDtypeStruct((T, C), BF16)],
        scratch_shapes=[pltpu.VMEM((HALO + tm, C), F32)],
        compiler_params=_params("parallel"),
    )(glu, glu, conv_w, conv_b, ln_g, ln_b)


def _mix_out(x, o, cact, gl, w_ap, w_cp, b_cp, w_out):
    tm = 256

    def body(x_ref, o_ref, c_ref, gl_ref, wap_ref, wcp_ref, bcp_ref, wo_ref,
             ya_ref, yc_ref, mg_ref, x1_ref):
        ya = _dot(o_ref[...], wap_ref[...])
        yc = _dot(c_ref[...], wcp_ref[...]) + bcp_ref[...]
        g0 = _sigmoid(gl_ref[:, 0:D].astype(F32))
        g1 = _sigmoid(gl_ref[:, D:2 * D].astype(F32))
        mg = (g0 * ya + g1 * yc).astype(BF16)
        ya_ref[...] = ya.astype(BF16)
        yc_ref[...] = yc.astype(BF16)
        mg_ref[...] = mg
        x1_ref[...] = x_ref[...] + _dot(mg, wo_ref[...])

    return pl.pallas_call(
        body, name="mix_out", grid=(T // tm,),
        in_specs=[_rows(tm, D), _rows(tm, AW), _rows(tm, C), _rows(tm, 2 * D),
                  _whole((AW, D)), _whole((C, D)), _whole((1, D)), _whole((D, D))],
        out_specs=[_rows(tm, D), _rows(tm, D), _rows(tm, D), _rows(tm, D)],
        out_shape=[jax.ShapeDtypeStruct((T, D), BF16), jax.ShapeDtypeStruct((T, D), BF16),
                   jax.ShapeDtypeStruct((T, D), BF16), jax.ShapeDtypeStruct((T, D), F32)],
        compiler_params=_params("parallel"),
    )(x, o, cact, gl, w_ap, w_cp, b_cp, w_out)


FFN_TN = 256


def _ffn_in(x1, g_ffn, w_fi):
    tm = 256

    def body(x_ref, g_ref, w_ref, h_ref, gu_ref, act_ref):
        xv = x_ref[...]
        r = lax.rsqrt(jnp.mean(xv * xv, axis=-1, keepdims=True) + EPS)
        h = (xv * r * g_ref[...]).astype(BF16)
        h_ref[...] = h
        for n in range(DFF // FFN_TN):
            c0 = n * FFN_TN
            gate = _dot(h, w_ref[:, c0:c0 + FFN_TN])
            up = _dot(h, w_ref[:, DFF + c0:DFF + c0 + FFN_TN])
            gu_ref[:, c0:c0 + FFN_TN] = gate.astype(BF16)
            gu_ref[:, DFF + c0:DFF + c0 + FFN_TN] = up.astype(BF16)
            act_ref[:, c0:c0 + FFN_TN] = (gate * _sigmoid(gate) * up).astype(BF16)

    return pl.pallas_call(
        body, name="ffn_in", grid=(T // tm,),
        in_specs=[_rows(tm, D), _whole((1, D)), _whole((D, 2 * DFF))],
        out_specs=[_rows(tm, D), _rows(tm, 2 * DFF), _rows(tm, DFF)],
        out_shape=[jax.ShapeDtypeStruct((T, D), BF16), jax.ShapeDtypeStruct((T, 2 * DFF), BF16),
                   jax.ShapeDtypeStruct((T, DFF), BF16)],
        compiler_params=_params("parallel"),
    )(x1, g_ffn, w_fi)


def _ffn_out_loss(x1, act, w_dn, g_final, target):
    tm = 256

    def body(x_ref, a_ref, w_ref, g_ref, t_ref, dx_ref, dxb_ref, dg_ref, loss_ref):
        i = pl.program_id(0)
        x2 = x_ref[...] + _dot(a_ref[...], w_ref[...])
        r = lax.rsqrt(jnp.mean(x2 * x2, axis=-1, keepdims=True) + EPS)
        xh = x2 * r
        g = g_ref[...]
        err = xh * g - t_ref[...]
        dy = err * (1.0 / D)
        dyg = dy * g
        dx = r * (dyg - xh * jnp.mean(dyg * xh, axis=-1, keepdims=True))
        dx_ref[...] = dx
        dxb_ref[...] = dx.astype(BF16)
        part = 0.5 * jnp.sum(jnp.mean(err * err, axis=-1, keepdims=True), axis=0, keepdims=True)

        @pl.when(i == 0)
        def _():
            dg_ref[...] = jnp.zeros_like(dg_ref)
            loss_ref[...] = jnp.zeros_like(loss_ref)

        dg_ref[...] += jnp.sum(dy * xh, axis=0, keepdims=True)
        loss_ref[...] += jnp.broadcast_to(part, loss_ref.shape)

    return pl.pallas_call(
        body, name="ffn_out_loss", grid=(T // tm,),
        in_specs=[_rows(tm, D), _rows(tm, DFF), _whole((DFF, D)), _whole((1, D)), _rows(tm, D)],
        out_specs=[_rows(tm, D), _rows(tm, D), _whole((1, D)), _whole((1, 128))],
        out_shape=[jax.ShapeDtypeStruct((T, D), F32), jax.ShapeDtypeStruct((T, D), BF16),
                   jax.ShapeDtypeStruct((1, D), F32), jax.ShapeDtypeStruct((1, 128), F32)],
        compiler_params=_params("arbitrary"),
    )(x1, act, w_dn, g_final, target)


def _const(shape):
    return pl.BlockSpec(shape, lambda i: tuple(0 for _ in shape), pipeline_mode=pl.Buffered(1))


def _ffn_bwd(dx2, dx2b, gu, x1, g_ffn, w_dn_t, w_fi_t):
    tm = 256

    def body(dx_ref, dxb_ref, gu_ref, x_ref, g_ref, wdn_ref, wfi_ref,
             dgu_ref, dx1_ref, dx1b_ref, dg_ref):
        i = pl.program_id(0)
        dxb = dxb_ref[...]
        dh = jnp.zeros((tm, D), F32)
        for n in range(DFF // FFN_TN):
            c0 = n * FFN_TN
            dact = _dot(dxb, wdn_ref[:, c0:c0 + FFN_TN])
            gate = gu_ref[:, c0:c0 + FFN_TN].astype(F32)
            up = gu_ref[:, DFF + c0:DFF + c0 + FFN_TN].astype(F32)
            s = _sigmoid(gate)
            dup = (dact * gate * s).astype(BF16)
            dgate = (dact * up * s * (1.0 + gate * (1.0 - s))).astype(BF16)
            dgu_ref[:, c0:c0 + FFN_TN] = dgate
            dgu_ref[:, DFF + c0:DFF + c0 + FFN_TN] = dup
            dh = dh + _dot(dgate, wfi_ref[c0:c0 + FFN_TN, :]) + _dot(dup, wfi_ref[DFF + c0:DFF + c0 + FFN_TN, :])
        xv = x_ref[...]
        r = lax.rsqrt(jnp.mean(xv * xv, axis=-1, keepdims=True) + EPS)
        xh = xv * r
        dhg = dh * g_ref[...]
        dx1 = dx_ref[...] + r * (dhg - xh * jnp.mean(dhg * xh, axis=-1, keepdims=True))
        dx1_ref[...] = dx1
        dx1b_ref[...] = dx1.astype(BF16)

        @pl.when(i == 0)
        def _():
            dg_ref[...] = jnp.zeros_like(dg_ref)

        dg_ref[...] += jnp.sum(dh * xh, axis=0, keepdims=True)

    return pl.pallas_call(
        body, name="ffn_bwd", grid=(T // tm,),
        in_specs=[_rows(tm, D), _rows(tm, D), _rows(tm, 2 * DFF), _rows(tm, D), _whole((1, D)),
                  _const((D, DFF)), _const((2 * DFF, D))],
        out_specs=[_rows(tm, 2 * DFF), _rows(tm, D), _rows(tm, D), _whole((1, D))],
        out_shape=[jax.ShapeDtypeStruct((T, 2 * DFF), BF16), jax.ShapeDtypeStruct((T, D), F32),
                   jax.ShapeDtypeStruct((T, D), BF16), jax.ShapeDtypeStruct((1, D), F32)],
        compiler_params=_params("arbitrary"),
    )(dx2, dx2b, gu, x1, g_ffn, w_dn_t, w_fi_t)


def _mix_bwd(dx1b, gl, ya, yc, w_out_t, w_ap_t, w_cp_t):
    tm = 256

    def body(dx_ref, gl_ref, ya_ref, yc_ref, wo_ref, wap_ref, wcp_ref,
             dya_ref, dyc_ref, dgl_ref, do_ref, dc_ref, db_ref):
        i = pl.program_id(0)
        dm = _dot(dx_ref[...], wo_ref[...])
        g0 = _sigmoid(gl_ref[:, 0:D].astype(F32))
        g1 = _sigmoid(gl_ref[:, D:2 * D].astype(F32))
        dya = dm * g0
        dyc = dm * g1
        dgl_ref[:, 0:D] = (dya * ya_ref[...].astype(F32) * (1.0 - g0)).astype(BF16)
        dgl_ref[:, D:2 * D] = (dyc * yc_ref[...].astype(F32) * (1.0 - g1)).astype(BF16)
        dyab = dya.astype(BF16)
        dycb = dyc.astype(BF16)
        dya_ref[...] = dyab
        dyc_ref[...] = dycb
        do_ref[...] = _dot(dyab, wap_ref[...]).astype(BF16)
        dc_ref[...] = _dot(dycb, wcp_ref[...])

        @pl.when(i == 0)
        def _():
            db_ref[...] = jnp.zeros_like(db_ref)

        db_ref[...] += jnp.sum(dyc, axis=0, keepdims=True)

    return pl.pallas_call(
        body, name="mix_bwd", grid=(T // tm,),
        in_specs=[_rows(tm, D), _rows(tm, 2 * D), _rows(tm, D), _rows(tm, D),
                  _whole((D, D)), _whole((D, AW)), _whole((D, C))],
        out_specs=[_rows(tm, D), _rows(tm, D), _rows(tm, 2 * D), _rows(tm, AW), _rows(tm, C),
                   _whole((1, D))],
        out_shape=[jax.ShapeDtypeStruct((T, D), BF16), jax.ShapeDtypeStruct((T, D), BF16),
                   jax.ShapeDtypeStruct((T, 2 * D), BF16), jax.ShapeDtypeStruct((T, AW), BF16),
                   jax.ShapeDtypeStruct((T, C), F32), jax.ShapeDtypeStruct((1, D), F32)],
        compiler_params=_params("arbitrary"),
    )(dx1b, gl, ya, yc, w_out_t, w_ap_t, w_cp_t)


def _conv_bwd(glu, u, dc, conv_w, ln_g, ln_b):
    tm = CONV_TM
    nblk = T // tm

    def du_of(uv, dcv, g_ref, b_ref):
        mu = jnp.mean(uv, axis=-1, keepdims=True)
        xc = uv - mu
        var = jnp.mean(xc * xc, axis=-1, keepdims=True)
        rstd = lax.rsqrt(var + EPS)
        xh = xc * rstd
        y = xh * g_ref[...] + b_ref[...]
        sg = _sigmoid(y)
        dy = dcv * (sg * (1.0 + y * (1.0 - sg)))
        dxh = dy * g_ref[...]
        du = rstd * (dxh - jnp.mean(dxh, axis=-1, keepdims=True)
                     - xh * jnp.mean(dxh * xh, axis=-1, keepdims=True))
        return du, dy, xh

    def body(cur_ref, prev_ref, u_ref, un_ref, dc_ref, dcn_ref, w_ref, g_ref, b_ref,
             dglu_ref, dw_ref, dcb_ref, dg_ref, db_ref, zp_ref, du_ref):
        i = pl.program_id(0)

        @pl.when(i == 0)
        def _():
            dw_ref[...] = jnp.zeros_like(dw_ref)
            dcb_ref[...] = jnp.zeros_like(dcb_ref)
            dg_ref[...] = jnp.zeros_like(dg_ref)
            db_ref[...] = jnp.zeros_like(db_ref)

        zprev = _glu(prev_ref[tm - HALO:tm, :])
        zp_ref[0:HALO, :] = jnp.where(i > 0, zprev, 0.0)
        zp_ref[HALO:HALO + tm, :] = _glu(cur_ref[...])

        dun, _, _ = du_of(un_ref[0:HALO, :], dcn_ref[0:HALO, :], g_ref, b_ref)
        du_ref[tm:tm + HALO, :] = jnp.where(i < nblk - 1, dun, 0.0)
        dg_acc = jnp.zeros((1, C), F32)
        db_acc = jnp.zeros((1, C), F32)
        dcb_acc = jnp.zeros((1, C), F32)
        for s in range(tm // CONV_SUB):
            rows = slice(s * CONV_SUB, (s + 1) * CONV_SUB)
            du, dy, xh = du_of(u_ref[rows, :], dc_ref[rows, :], g_ref, b_ref)
            du_ref[rows, :] = du
            dg_acc = dg_acc + jnp.sum(dy * xh, axis=0, keepdims=True)
            db_acc = db_acc + jnp.sum(dy, axis=0, keepdims=True)
            dcb_acc = dcb_acc + jnp.sum(du, axis=0, keepdims=True)
        dg_ref[...] += dg_acc
        db_ref[...] += db_acc
        dcb_ref[...] += dcb_acc

        for j in range(KW):
            acc = jnp.zeros((CONV_SUB, C), F32)
            for s in range(tm // CONV_SUB):
                base = HALO + s * CONV_SUB - (KW - 1) + j
                acc = acc + du_ref[s * CONV_SUB:(s + 1) * CONV_SUB, :] * zp_ref[base:base + CONV_SUB, :]
            dw_ref[j:j + 1, :] += jnp.sum(acc, axis=0, keepdims=True)

        for s in range(tm // CONV_SUB):
            rows = slice(s * CONV_SUB, (s + 1) * CONV_SUB)
            dz = jnp.zeros((CONV_SUB, C), F32)
            for j in range(KW):
                o = s * CONV_SUB + (KW - 1) - j
                dz = dz + w_ref[j:j + 1, :] * du_ref[o:o + CONV_SUB, :]
            a = cur_ref[rows, 0:C].astype(F32)
            sb = _sigmoid(cur_ref[rows, C:2 * C].astype(F32))
            dglu_ref[rows, 0:C] = (dz * sb).astype(BF16)
            dglu_ref[rows, C:2 * C] = (dz * a * sb * (1.0 - sb)).astype(BF16)

    nxt = lambda i: (jnp.minimum(i + 1, nblk - 1), 0)
    return pl.pallas_call(
        body, name="conv_bwd", grid=(nblk,),
        in_specs=[_rows(tm, 2 * C),
                  pl.BlockSpec((tm, 2 * C), lambda i: (jnp.maximum(i - 1, 0), 0)),
                  _rows(tm, C), pl.BlockSpec((tm, C), nxt),
                  _rows(tm, C), pl.BlockSpec((tm, C), nxt),
                  _whole((KW, C)), _whole((1, C)), _whole((1, C))],
        out_specs=[_rows(tm, 2 * C), _whole((KW, C)), _whole((1, C)), _whole((1, C)), _whole((1, C))],
        out_shape=[jax.ShapeDtypeStruct((T, 2 * C), BF16), jax.ShapeDtypeStruct((KW, C), F32),
                   jax.ShapeDtypeStruct((1, C), F32), jax.ShapeDtypeStruct((1, C), F32),
                   jax.ShapeDtypeStruct((1, C), F32)],
        scratch_shapes=[pltpu.VMEM((HALO + tm, C), F32), pltpu.VMEM((tm + HALO, C), F32)],
        compiler_params=_params("arbitrary"),
    )(glu, glu, u, u, dc, dc, conv_w, ln_g, ln_b)


def _attn_bwd(qkv, o, do, lse, sinks):
    def body(sink_ref, qkv_ref, o_ref, do_ref, lse_ref, dq_ref, dkv_ref, ds_ref):
        i = pl.program_id(0)

        @pl.when(i == 0)
        def _():
            dkv_ref[...] = jnp.zeros_like(dkv_ref)
            ds_ref[...] = jnp.zeros_like(ds_ref)

        r0 = pl.multiple_of(i * BLK, BLK)
        rp = pl.multiple_of(jnp.maximum(i - 1, 0) * BLK, BLK)
        mask_cur, mask_prev = _attn_masks(i)
        for h in range(NQ):
            kv = h // GROUP
            kcol = slice(AW + kv * HD, AW + (kv + 1) * HD)
            vcol = slice(AW + KVW + kv * HD, AW + KVW + (kv + 1) * HD)
            hcol = slice(h * HD, (h + 1) * HD)
            q = qkv_ref[pl.ds(r0, BLK), hcol]
            kc = qkv_ref[pl.ds(r0, BLK), kcol]
            kp = qkv_ref[pl.ds(rp, BLK), kcol]
            vc = qkv_ref[pl.ds(r0, BLK), vcol]
            vp = qkv_ref[pl.ds(rp, BLK), vcol]
            doh = do_ref[:, hcol]
            lse_h = lse_ref[:, h:h + 1]
            dl = jnp.sum(doh.astype(F32) * o_ref[:, hcol].astype(F32), axis=-1, keepdims=True)
            pc = jnp.where(mask_cur, jnp.exp(_dot_nt(q, kc) * SCALE - lse_h), 0.0)
            pp = jnp.where(mask_prev, jnp.exp(_dot_nt(q, kp) * SCALE - lse_h), 0.0)
            dsc = (pc * (_dot_nt(doh, vc) - dl)).astype(BF16)
            dsp = (pp * (_dot_nt(doh, vp) - dl)).astype(BF16)
            dq_ref[:, hcol] = ((_dot(dsc, kc) + _dot(dsp, kp)) * SCALE).astype(BF16)
            dkc = slice(kv * HD, (kv + 1) * HD)
            dvc = slice(KVW + kv * HD, KVW + (kv + 1) * HD)
            dkv_ref[pl.ds(r0, BLK), dkc] += _dot_tn(dsc, q) * SCALE
            dkv_ref[pl.ds(rp, BLK), dkc] += _dot_tn(dsp, q) * SCALE
            dkv_ref[pl.ds(r0, BLK), dvc] += _dot_tn(pc.astype(BF16), doh)
            dkv_ref[pl.ds(rp, BLK), dvc] += _dot_tn(pp.astype(BF16), doh)
            psink = jnp.exp(sink_ref[0, h] - lse_h)
            dsink = -jnp.sum(psink * dl, axis=0, keepdims=True)
            ds_ref[h:h + 1, :] += jnp.broadcast_to(dsink, (1, 128))

    return pl.pallas_call(
        body, name="attn_bwd", grid=(T // BLK,),
        in_specs=[pl.BlockSpec(memory_space=pltpu.SMEM), _whole((T, QKVW)),
                  _rows(BLK, AW), _rows(BLK, AW), _rows(BLK, NQ)],
        out_specs=[_rows(BLK, AW), _whole((T, 2 * KVW)), _whole((NQ, 128))],
        out_shape=[jax.ShapeDtypeStruct((T, AW), BF16), jax.ShapeDtypeStruct((T, 2 * KVW), F32),
                   jax.ShapeDtypeStruct((NQ, 128), F32)],
        compiler_params=_params("arbitrary"),
    )(sinks, qkv, o, do, lse)


def _in_proj_bwd(dproj, x, dx1, g_mix, w_in_t):
    tm = 256

    def body(dp_ref, x_ref, dx1_ref, g_ref, w_ref, gx_ref, dg_ref, db_ref):
        i = pl.program_id(0)
        dp = dp_ref[...]
        dh = _dot(dp, w_ref[...])
        xv = x_ref[...]
        r = lax.rsqrt(jnp.mean(xv * xv, axis=-1, keepdims=True) + EPS)
        xh = xv * r
        dhg = dh * g_ref[...]
        gx_ref[...] = dx1_ref[...] + r * (dhg - xh * jnp.mean(dhg * xh, axis=-1, keepdims=True))

        @pl.when(i == 0)
        def _():
            dg_ref[...] = jnp.zeros_like(dg_ref)
            db_ref[...] = jnp.zeros_like(db_ref)

        dg_ref[...] += jnp.sum(dh * xh, axis=0, keepdims=True)
        db_ref[...] += jnp.sum(dp.astype(F32), axis=0, keepdims=True)

    return pl.pallas_call(
        body, name="in_proj_bwd", grid=(T // tm,),
        in_specs=[_rows(tm, INW), _rows(tm, D), _rows(tm, D), _whole((1, D)), _const((INW, D))],
        out_specs=[_rows(tm, D), _whole((1, D)), _whole((1, INW))],
        out_shape=[jax.ShapeDtypeStruct((T, D), F32), jax.ShapeDtypeStruct((1, D), F32),
                   jax.ShapeDtypeStruct((1, INW), F32)],
        compiler_params=_params("arbitrary"),
    )(dproj, x, dx1, g_mix, w_in_t)


def _grad_w(a, b, name, tk, tn):
    k, n = a.shape[1], b.shape[1]

    def body(a_ref, b_ref, o_ref):
        o_ref[...] = _dot_tn(a_ref[...], b_ref[...]).astype(BF16)

    return pl.pallas_call(
        body, name=name, grid=(k // tk, n // tn),
        in_specs=[pl.BlockSpec((T, tk), lambda i, j: (0, i)), pl.BlockSpec((T, tn), lambda i, j: (0, j))],
        out_specs=pl.BlockSpec((tk, tn), lambda i, j: (i, j)),
        out_shape=jax.ShapeDtypeStruct((k, n), BF16),
        compiler_params=_params("parallel", "parallel"),
    )(a, b)


def _local_step(x, target, small, w_in, w_ap, w_cp, w_out, w_fi, w_dn):
    h, qkv, glu, gl = _in_proj(x, small["g_mix_norm"], w_in, small["b_in"])
    o, lse = _attn_fwd(qkv, small["sinks"])
    u, cact = _conv_fwd(glu, small["conv_w"], small["conv_b"], small["ln_g"], small["ln_b"])
    ya, yc, mg, x1 = _mix_out(x, o, cact, gl, w_ap, w_cp, small["b_conv_proj"], w_out)
    h2, gu, act = _ffn_in(x1, small["g_ffn_norm"], w_fi)
    dx2, dx2b, dg_final, loss = _ffn_out_loss(x1, act, w_dn, small["g_final"], target)

    dgu, dx1, dx1b, dg_ffn = _ffn_bwd(dx2, dx2b, gu, x1, small["g_ffn_norm"], w_dn.T, w_fi.T)
    dya, dyc, dgl, do, dc, db_cp = _mix_bwd(dx1b, gl, ya, yc, w_out.T, w_ap.T, w_cp.T)
    dglu, dconv_w, dconv_b, dln_g, dln_b = _conv_bwd(glu, u, dc, small["conv_w"], small["ln_g"], small["ln_b"])
    dq, dkv, dsinks = _attn_bwd(qkv, o, do, lse, small["sinks"])
    dproj = jnp.concatenate([dq, dkv.astype(BF16), dglu, dgl], axis=1)
    grad_x, dg_mix, db_in = _in_proj_bwd(dproj, x, dx1, small["g_mix_norm"], w_in.T)

    gw = {
        "w_in": _grad_w(h, dproj, "grad_w_in", 512, 768),
        "w_attn_proj": _grad_w(o, dya, "grad_w_attn_proj", 512, 512),
        "w_conv_proj": _grad_w(cact, dyc, "grad_w_conv_proj", 512, 512),
        "w_out": _grad_w(mg, dx1b, "grad_w_out", 512, 512),
        "w_ffn_in": _grad_w(h2, dgu, "grad_w_ffn_in", 512, 512),
        "w_ffn_down": _grad_w(act, dx2b, "grad_w_ffn_down", 256, 512),
    }
    gs = {
        "g_mix_norm": dg_mix, "b_in": db_in, "sinks": dsinks[:, 0].reshape(1, NQ),
        "conv_w": dconv_w, "conv_b": dconv_b, "ln_g": dln_g, "ln_b": dln_b,
        "b_conv_proj": db_cp, "g_ffn_norm": dg_ffn, "g_final": dg_final,
    }
    return loss[0, 0], grad_x, gw, gs


HBM_SPEC = pl.BlockSpec(memory_space=pltpu.HBM)
N_CHIPS = 4


def _place():
    x, y, c = lax.axis_index("x"), lax.axis_index("y"), lax.axis_index("c")
    chips = [(1 - x, y), (x, 1 - y), (1 - x, 1 - y)]
    return x, y, c, chips


def _gather_weights(shards):
    n = len(shards)

    def body(*refs):
        ins, outs = refs[:n], refs[n:2 * n]
        send_sems, recv_sems, local_sems = refs[2 * n:]
        x, y, c, chips = _place()
        me, sibling = (x, y, c), (x, y, 1 - c)
        s_me = 2 * x + y

        def half(ref, s, cc):
            kh = ref.shape[1] // 2
            return ref.at[s, pl.ds(cc * kh, kh)]

        def remote(w, k, src, dst, to):
            return pltpu.make_async_remote_copy(
                src_ref=src, dst_ref=dst, send_sem=send_sems.at[6 * w + k],
                recv_sem=recv_sems.at[6 * w + k], device_id=to, device_id_type=MESH)

        started = []
        for w in range(n):
            own = pltpu.make_async_copy(ins[w], outs[w].at[s_me], local_sems.at[w])
            own.start()
            started.append(own)
        sends = []
        for w in range(n):
            kh = ins[w].shape[0] // 2
            for j, chip in enumerate(chips):
                cp = remote(w, j, ins[w].at[pl.ds(c * kh, kh)], half(outs[w], s_me, c), (*chip, c))
                cp.start()
                sends.append(cp)
        for w in range(n):
            for j, chip in enumerate(chips):
                blk = half(outs[w], 2 * chip[0] + chip[1], c)
                remote(w, j, blk, blk, me).wait_recv()
                cp = remote(w, 3 + j, blk, blk, sibling)
                cp.start()
                sends.append(cp)
        for w in range(n):
            for j, chip in enumerate(chips):
                blk = half(outs[w], 2 * chip[0] + chip[1], 1 - c)
                remote(w, 3 + j, blk, blk, me).wait_recv()
        for cp in sends:
            cp.wait_send()
        for own in started:
            own.wait()

    return pl.pallas_call(
        body, name="gather_weights",
        in_specs=[HBM_SPEC] * n, out_specs=[HBM_SPEC] * n,
        out_shape=[jax.ShapeDtypeStruct((N_CHIPS,) + s.shape, s.dtype) for s in shards],
        scratch_shapes=[pltpu.SemaphoreType.DMA((6 * n,)), pltpu.SemaphoreType.DMA((6 * n,)),
                        pltpu.SemaphoreType.DMA((n,))],
    )(*shards)


def _pair_exchange(grads):
    n = len(grads)

    def body(*refs):
        ins, outs = refs[:n], refs[n:2 * n]
        send_sems, recv_sems = refs[2 * n:]
        x, y, c, _ = _place()
        copies = []
        for w in range(n):
            kh = ins[w].shape[1] // 2
            cp = pltpu.make_async_remote_copy(
                src_ref=ins[w].at[:, pl.ds((1 - c) * kh, kh)], dst_ref=outs[w],
                send_sem=send_sems.at[w], recv_sem=recv_sems.at[w],
                device_id=(x, y, 1 - c), device_id_type=MESH)
            cp.start()
            copies.append(cp)
        for cp in copies:
            cp.wait()

    return pl.pallas_call(
        body, name="pair_exchange",
        in_specs=[HBM_SPEC] * n, out_specs=[HBM_SPEC] * n,
        out_shape=[jax.ShapeDtypeStruct((g.shape[0], g.shape[1] // 2, g.shape[2]), g.dtype) for g in grads],
        scratch_shapes=[pltpu.SemaphoreType.DMA((n,)), pltpu.SemaphoreType.DMA((n,))],
    )(*grads)


def _row_tile(k):
    for t in (256, 128, 176, 64, 32, 16):
        if k % t == 0:
            return t
    raise ValueError(k)


def _pair_sum(c_idx, g, got, name):
    _, k, n = g.shape
    kh = k // 2
    tm = _row_tile(kh)
    nb = kh // tm

    def body(c_ref, g_ref, r_ref, o_ref):
        o_ref[...] = (g_ref[...].astype(F32) + r_ref[...].astype(F32)).astype(BF16)

    return pl.pallas_call(
        body, name=name,
        grid_spec=pltpu.PrefetchScalarGridSpec(
            num_scalar_prefetch=1, grid=(N_CHIPS, nb),
            in_specs=[pl.BlockSpec((1, tm, n), lambda s, i, c_ref: (s, c_ref[0] * nb + i, 0)),
                      pl.BlockSpec((1, tm, n), lambda s, i, c_ref: (s, i, 0))],
            out_specs=pl.BlockSpec((1, tm, n), lambda s, i, c_ref: (s, i, 0))),
        out_shape=jax.ShapeDtypeStruct((N_CHIPS, kh, n), BF16),
        compiler_params=_params("parallel", "parallel"),
    )(c_idx, g, got)


def _chip_exchange(sums):
    n = len(sums)

    def body(*refs):
        ins, outs = refs[:n], refs[n:2 * n]
        send_sems, recv_sems = refs[2 * n:]
        x, y, c, chips = _place()
        copies = []
        for w in range(n):
            for j, chip in enumerate(chips):
                cp = pltpu.make_async_remote_copy(
                    src_ref=ins[w].at[2 * chip[0] + chip[1]], dst_ref=outs[w].at[j],
                    send_sem=send_sems.at[3 * w + j], recv_sem=recv_sems.at[3 * w + j],
                    device_id=(*chip, c), device_id_type=MESH)
                cp.start()
                copies.append(cp)
        for cp in copies:
            cp.wait()

    return pl.pallas_call(
        body, name="chip_exchange",
        in_specs=[HBM_SPEC] * n, out_specs=[HBM_SPEC] * n,
        out_shape=[jax.ShapeDtypeStruct((3,) + s.shape[1:], s.dtype) for s in sums],
        scratch_shapes=[pltpu.SemaphoreType.DMA((3 * n,)), pltpu.SemaphoreType.DMA((3 * n,))],
    )(*sums)


def _chip_sum(s_idx, mine, got, name):
    _, kh, n = mine.shape
    tm = _row_tile(kh)

    def body(s_ref, m_ref, r_ref, o_ref):
        acc = m_ref[0].astype(F32)
        for j in range(3):
            acc = acc + r_ref[j].astype(F32)
        o_ref[...] = acc

    return pl.pallas_call(
        body, name=name,
        grid_spec=pltpu.PrefetchScalarGridSpec(
            num_scalar_prefetch=1, grid=(kh // tm,),
            in_specs=[pl.BlockSpec((1, tm, n), lambda i, s_ref: (s_ref[0], i, 0)),
                      pl.BlockSpec((3, tm, n), lambda i, s_ref: (0, i, 0))],
            out_specs=pl.BlockSpec((tm, n), lambda i, s_ref: (i, 0))),
        out_shape=jax.ShapeDtypeStruct((kh, n), F32),
        compiler_params=_params("parallel"),
    )(s_idx, mine, got)


def _pair_share(halves):
    n = len(halves)

    def body(*refs):
        ins, outs = refs[:n], refs[n:2 * n]
        send_sems, recv_sems, local_sems = refs[2 * n:]
        x, y, c, _ = _place()
        copies, owns = [], []
        for w in range(n):
            own = pltpu.make_async_copy(ins[w], outs[w].at[c], local_sems.at[w])
            own.start()
            owns.append(own)
            cp = pltpu.make_async_remote_copy(
                src_ref=ins[w], dst_ref=outs[w].at[c],
                send_sem=send_sems.at[w], recv_sem=recv_sems.at[w],
                device_id=(x, y, 1 - c), device_id_type=MESH)
            cp.start()
            copies.append(cp)
        for w in range(n):
            copies[w].wait_send()
            pltpu.make_async_remote_copy(
                src_ref=ins[w], dst_ref=outs[w].at[1 - c],
                send_sem=send_sems.at[w], recv_sem=recv_sems.at[w],
                device_id=(x, y, c), device_id_type=MESH).wait_recv()
            owns[w].wait()

    return pl.pallas_call(
        body, name="pair_share",
        in_specs=[HBM_SPEC] * n, out_specs=[HBM_SPEC] * n,
        out_shape=[jax.ShapeDtypeStruct((2,) + h.shape, h.dtype) for h in halves],
        scratch_shapes=[pltpu.SemaphoreType.DMA((n,)), pltpu.SemaphoreType.DMA((n,)),
                        pltpu.SemaphoreType.DMA((n,))],
    )(*halves)


def _gather_small(blocks):
    n = len(blocks)

    def body(*refs):
        ins, outs = refs[:n], refs[n:2 * n]
        send_sems, recv_sems, local_sems = refs[2 * n:]
        x, y, c, chips = _place()
        me, sibling = (x, y, c), (x, y, 1 - c)

        def rows(w, px, py, pc):
            m = ins[w].shape[0]
            return outs[w].at[pl.ds((4 * px + 2 * py + pc) * m, m), :]

        def copy(w, k, block, to, src=None):
            return pltpu.make_async_remote_copy(
                src_ref=rows(w, *block) if src is None else src, dst_ref=rows(w, *block),
                send_sem=send_sems.at[7 * w + k], recv_sem=recv_sems.at[7 * w + k],
                device_id=to, device_id_type=MESH)

        started, owns = [], []
        for w in range(n):
            own = pltpu.make_async_copy(ins[w], rows(w, *me), local_sems.at[w])
            own.start()
            owns.append(own)
            first = [copy(w, 0, me, sibling, src=ins[w])]
            first += [copy(w, 1 + j, me, (*chip, c), src=ins[w]) for j, chip in enumerate(chips)]
            for cp in first:
                cp.start()
            started += first
        for w in range(n):
            for j, chip in enumerate(chips):
                copy(w, 1 + j, (*chip, c), me).wait_recv()
                cp = copy(w, 4 + j, (*chip, c), sibling)
                cp.start()
                started.append(cp)
        for w in range(n):
            copy(w, 0, sibling, me).wait_recv()
            for j, chip in enumerate(chips):
                copy(w, 4 + j, (*chip, 1 - c), me).wait_recv()
        for cp in started:
            cp.wait_send()
        for own in owns:
            own.wait()

    vmem = pl.BlockSpec(memory_space=pltpu.VMEM)
    return pl.pallas_call(
        body, name="gather_small",
        in_specs=[vmem] * n, out_specs=[vmem] * n,
        out_shape=[jax.ShapeDtypeStruct((8 * b.shape[0], b.shape[1]), b.dtype) for b in blocks],
        scratch_shapes=[pltpu.SemaphoreType.DMA((7 * n,)), pltpu.SemaphoreType.DMA((7 * n,)),
                        pltpu.SemaphoreType.DMA((n,))],
    )(*blocks)


def _adamw_math(w, g, m, v):
    m = ADAM_B1 * m + (1.0 - ADAM_B1) * g
    v = ADAM_B2 * v + (1.0 - ADAM_B2) * (g * g)
    m_hat = m / (1.0 - ADAM_B1 ** ADAM_STEP)
    v_hat = v / (1.0 - ADAM_B2 ** ADAM_STEP)
    delta = -ADAM_LR * (m_hat / (jnp.sqrt(v_hat) + ADAM_EPS) + ADAM_WD * w)
    return delta, m, v


def _adamw(w, g, m, v, name):
    k, n = w.shape
    tm = k // 4

    def body(w_ref, g_ref, m_ref, v_ref, d_ref, mo_ref, vo_ref):
        d, mm, vv = _adamw_math(w_ref[...], g_ref[...], m_ref[...], v_ref[...])
        d_ref[...] = d
        mo_ref[...] = mm
        vo_ref[...] = vv

    spec = pl.BlockSpec((tm, n), lambda i: (i, 0))
    shp = jax.ShapeDtypeStruct((k, n), F32)
    return pl.pallas_call(
        body, name=name, grid=(4,), in_specs=[spec] * 4, out_specs=[spec] * 3,
        out_shape=[shp] * 3, compiler_params=_params("parallel"),
    )(w, g, m, v)


VEC_SLOTS = {
    "g_mix_norm": (0, 0, D), "b_conv_proj": (0, D, D), "g_ffn_norm": (0, 2 * D, D),
    "g_final": (0, 3 * D, D), "b_in": (1, 0, INW), "conv_b": (2, 0, C), "ln_g": (2, C, C),
    "ln_b": (2, 2 * C, C), "sinks": (2, 3 * C, NQ), "loss": (2, 3 * C + 128, 1),
}
VEC_ROWS, VEC_COLS = 8, 4 * D
CW_ROWS = 32
SMALL_NAMES = ["g_mix_norm", "b_in", "sinks", "conv_w", "conv_b", "ln_g", "ln_b",
               "b_conv_proj", "g_ffn_norm", "g_final"]
CW_LANES = C // N_CHIPS


def _pack_small(gs, loss):
    row0 = jnp.concatenate([gs["g_mix_norm"], gs["b_conv_proj"], gs["g_ffn_norm"], gs["g_final"]], axis=1)
    row1 = jnp.pad(gs["b_in"], ((0, 0), (0, VEC_COLS - INW)))
    row2 = jnp.concatenate([gs["conv_b"], gs["ln_g"], gs["ln_b"],
                            jnp.pad(gs["sinks"], ((0, 0), (0, 128 - NQ))),
                            jnp.pad(loss.reshape(1, 1), ((0, 0), (0, VEC_COLS - 3 * C - 129)))], axis=1)
    vec = jnp.concatenate([row0, row1, row2, jnp.zeros((VEC_ROWS - 3, VEC_COLS), F32)], axis=0)
    cw = jnp.pad(gs["conv_w"], ((0, CW_ROWS - KW), (0, 0)))
    return vec, cw


def _small_update(s_idx, vec_all, cw_all, wmv):
    nsm = len(SMALL_NAMES)

    def body(s_ref, vec_ref, cw_ref, *refs):
        ins = refs[:3 * nsm]
        outs = refs[3 * nsm:7 * nsm]
        loss_ref = refs[7 * nsm]

        def total(slot):
            row, lane, width = slot
            acc = vec_ref[row:row + 1, lane:lane + width]
            for k in range(1, 8):
                acc = acc + vec_ref[k * VEC_ROWS + row:k * VEC_ROWS + row + 1, lane:lane + width]
            return acc

        loss_ref[...] = jnp.broadcast_to(total(VEC_SLOTS["loss"]), loss_ref.shape)
        for p, name in enumerate(SMALL_NAMES):
            w_ref, m_ref, v_ref = ins[3 * p:3 * p + 3]
            g_ref, d_ref, mo_ref, vo_ref = outs[4 * p:4 * p + 4]
            if name == "conv_w":
                g = jnp.zeros((KW, CW_LANES), F32)
                for s in range(N_CHIPS):
                    cand = cw_ref[0:KW, s * CW_LANES:(s + 1) * CW_LANES]
                    for k in range(1, 8):
                        cand = cand + cw_ref[k * CW_ROWS:k * CW_ROWS + KW, s * CW_LANES:(s + 1) * CW_LANES]
                    g = jnp.where(s_ref[0] == s, cand, g)
            else:
                g = total(VEC_SLOTS[name])
            d, mm, vv = _adamw_math(w_ref[...], g, m_ref[...], v_ref[...])
            g_ref[...] = g
            d_ref[...] = d
            mo_ref[...] = mm
            vo_ref[...] = vv

    vmem = pl.BlockSpec(memory_space=pltpu.VMEM)
    flat = [a for t in wmv for a in t]
    out_shape = []
    for w, _, _ in wmv:
        out_shape += [jax.ShapeDtypeStruct(w.shape, F32)] * 4
    out_shape.append(jax.ShapeDtypeStruct((1, 128), F32))
    res = pl.pallas_call(
        body, name="small_update",
        in_specs=[pl.BlockSpec(memory_space=pltpu.SMEM)] + [vmem] * (2 + len(flat)),
        out_specs=[vmem] * len(out_shape), out_shape=out_shape,
    )(s_idx, vec_all, cw_all, *flat)
    return [tuple(res[4 * p:4 * p + 4]) for p in range(nsm)], res[4 * nsm]


BIG_NAMES = ["w_in", "w_attn_proj", "w_conv_proj", "w_out", "w_ffn_in", "w_ffn_down"]
COL_SHARDED = {"w_in": True, "w_attn_proj": True, "w_conv_proj": True, "w_out": False,
               "w_ffn_in": True, "w_ffn_down": False}
WEIGHT_ORDER = ["g_mix_norm", "w_in", "b_in", "sinks", "conv_w", "conv_b", "ln_g", "ln_b",
                "w_attn_proj", "w_conv_proj", "b_conv_proj", "w_out", "g_ffn_norm", "w_ffn_in",
                "w_ffn_down", "g_final"]


def _assemble(name, gathered):
    s, k, n = gathered.shape
    if COL_SHARDED[name]:
        return gathered.transpose(1, 0, 2).reshape(k, s * n)
    return gathered.reshape(s * k, n)


def _split(name, full):
    k, n = full.shape
    if COL_SHARDED[name]:
        return full.reshape(k, N_CHIPS, n // N_CHIPS).transpose(1, 0, 2)
    return full.reshape(N_CHIPS, k // N_CHIPS, n)


def kernel(x, g_mix_norm, w_in, b_in, sinks, conv_w, conv_b, ln_g, ln_b, w_attn_proj, w_conv_proj, b_conv_proj, w_out, g_ffn_norm, w_ffn_in, w_ffn_down, g_final, loss_target, m_g_mix_norm, m_w_in, m_b_in, m_sinks, m_conv_w, m_conv_b, m_ln_g, m_ln_b, m_w_attn_proj, m_w_conv_proj, m_b_conv_proj, m_w_out, m_g_ffn_norm, m_w_ffn_in, m_w_ffn_down, m_g_final, v_g_mix_norm, v_w_in, v_b_in, v_sinks, v_conv_w, v_conv_b, v_ln_g, v_ln_b, v_w_attn_proj, v_w_conv_proj, v_b_conv_proj, v_w_out, v_g_ffn_norm, v_w_ffn_in, v_w_ffn_down, v_g_final):
    w = dict(g_mix_norm=g_mix_norm, w_in=w_in, b_in=b_in, sinks=sinks, conv_w=conv_w, conv_b=conv_b,
             ln_g=ln_g, ln_b=ln_b, w_attn_proj=w_attn_proj, w_conv_proj=w_conv_proj,
             b_conv_proj=b_conv_proj, w_out=w_out, g_ffn_norm=g_ffn_norm, w_ffn_in=w_ffn_in,
             w_ffn_down=w_ffn_down, g_final=g_final)
    m = dict(g_mix_norm=m_g_mix_norm, w_in=m_w_in, b_in=m_b_in, sinks=m_sinks, conv_w=m_conv_w,
             conv_b=m_conv_b, ln_g=m_ln_g, ln_b=m_ln_b, w_attn_proj=m_w_attn_proj,
             w_conv_proj=m_w_conv_proj, b_conv_proj=m_b_conv_proj, w_out=m_w_out,
             g_ffn_norm=m_g_ffn_norm, w_ffn_in=m_w_ffn_in, w_ffn_down=m_w_ffn_down, g_final=m_g_final)
    v = dict(g_mix_norm=v_g_mix_norm, w_in=v_w_in, b_in=v_b_in, sinks=v_sinks, conv_w=v_conv_w,
             conv_b=v_conv_b, ln_g=v_ln_g, ln_b=v_ln_b, w_attn_proj=v_w_attn_proj,
             w_conv_proj=v_w_conv_proj, b_conv_proj=v_b_conv_proj, w_out=v_w_out,
             g_ffn_norm=v_g_ffn_norm, w_ffn_in=v_w_ffn_in, w_ffn_down=v_w_ffn_down, g_final=v_g_final)

    c_idx = lax.axis_index("c").astype(jnp.int32).reshape(1)
    s_idx = (2 * lax.axis_index("x") + lax.axis_index("y")).astype(jnp.int32).reshape(1)

    shards = [w[name][0].astype(BF16) for name in BIG_NAMES]
    shards.append(jnp.pad(conv_w[0], ((0, CW_ROWS - KW), (0, 0))))
    gathered = _gather_weights(shards)
    full = {name: _assemble(name, g) for name, g in zip(BIG_NAMES, gathered)}
    conv_w_full = gathered[-1].transpose(1, 0, 2).reshape(CW_ROWS, C)[:KW]

    small = dict(g_mix_norm=g_mix_norm, b_in=b_in, sinks=sinks, conv_w=conv_w_full, conv_b=conv_b,
                 ln_g=ln_g, ln_b=ln_b, b_conv_proj=b_conv_proj, g_ffn_norm=g_ffn_norm,
                 g_final=g_final.reshape(1, D))
    loss_part, grad_x, gw, gs = _local_step(
        x[0], loss_target[0], small, full["w_in"], full["w_attn_proj"], full["w_conv_proj"],
        full["w_out"], full["w_ffn_in"], full["w_ffn_down"])

    grads = [_split(name, gw[name]) for name in BIG_NAMES]
    from_sibling = _pair_exchange(grads)
    pair = [_pair_sum(c_idx, g, r, "pair_sum_" + name) for name, g, r in zip(BIG_NAMES, grads, from_sibling)]
    from_chips = _chip_exchange(pair)
    halves = [_chip_sum(s_idx, p, r, "chip_sum_" + name) for name, p, r in zip(BIG_NAMES, pair, from_chips)]
    shared = _pair_share(halves)

    out_g, out_d, out_m, out_v = {}, {}, {}, {}
    for name, both in zip(BIG_NAMES, shared):
        g = both.reshape(w[name].shape[1:])
        d, mm, vv = _adamw(w[name][0], g, m[name][0], v[name][0], "adamw_" + name)
        out_g[name], out_d[name], out_m[name], out_v[name] = g[None], d[None], mm[None], vv[None]

    vec, cw = _pack_small(gs, loss_part)
    vec_all, cw_all = _gather_small([vec, cw])

    def view(a, name):
        if name == "conv_w":
            return a[0]
        if name == "g_final":
            return a.reshape(1, D)
        return a

    wmv = [(view(w[name], name), view(m[name], name), view(v[name], name)) for name in SMALL_NAMES]
    small_out, loss_row = _small_update(s_idx, vec_all, cw_all, wmv)
    for name, (g, d, mm, vv) in zip(SMALL_NAMES, small_out):
        shape = w[name].shape
        out_g[name], out_d[name], out_m[name], out_v[name] = (
            g.reshape(shape), d.reshape(shape), mm.reshape(shape), vv.reshape(shape))

    loss = loss_row[0, 0]
    return (loss, grad_x[None], *[out_g[k] for k in WEIGHT_ORDER], *[out_d[k] for k in WEIGHT_ORDER],
            *[out_m[k] for k in WEIGHT_ORDER], *[out_v[k] for k in WEIGHT_ORDER])
```

```python
import functools

import jax
import jax.numpy as jnp
from jax import lax
from jax.experimental import pallas as pl
from jax.experimental.pallas import tpu as pltpu

F32 = jnp.float32
BF16 = jnp.bfloat16

T = 2048
D = 1024
HD = 64
NQ = 8
NKV = 2
GROUP = NQ // NKV
BLK = 128
AW = NQ * HD
KVW = NKV * HD
C = 512
KW = 31
QKVW = AW + 2 * KVW
GLU_OFF = QKVW
GATE_OFF = GLU_OFF + 2 * C
INW = GATE_OFF + 2 * D
DFF = 2816
EPS = 1e-5
NEG = -1e30
SCALE = HD ** -0.5
HALO = 32

ADAM_LR = 0.001
ADAM_B1 = 0.9
ADAM_B2 = 0.999
ADAM_EPS = 1e-08
ADAM_WD = 0.01
ADAM_STEP = 10

VMEM_LIMIT = 56 * 1024 * 1024
MESH = pl.DeviceIdType.MESH


def _params(*sem):
    return pltpu.CompilerParams(dimension_semantics=sem, vmem_limit_bytes=VMEM_LIMIT)


def _dot(a, b):
    return jnp.dot(a, b, preferred_element_type=F32)


def _dot_nt(a, b):
    return lax.dot_general(a, b, (((1,), (1,)), ((), ())), preferred_element_type=F32)


def _dot_tn(a, b):
    return lax.dot_general(a, b, (((0,), (0,)), ((), ())), preferred_element_type=F32)


def _sigmoid(v):
    return 1.0 / (1.0 + jnp.exp(-v))


def _rows(tm, n):
    return pl.BlockSpec((tm, n), lambda i: (i, 0))


def _whole(shape):
    return pl.BlockSpec(shape, lambda i: tuple(0 for _ in shape))


def _in_proj(x, g_mix, w_in, b_in):
    tm = 256

    def body(x_ref, g_ref, w_ref, b_ref, h_ref, qkv_ref, glu_ref, gl_ref):
        xv = x_ref[...]
        r = lax.rsqrt(jnp.mean(xv * xv, axis=-1, keepdims=True) + EPS)
        h = (xv * r * g_ref[...]).astype(BF16)
        h_ref[...] = h
        qkv_ref[...] = (_dot(h, w_ref[:, 0:GLU_OFF]) + b_ref[:, 0:GLU_OFF]).astype(BF16)
        glu_ref[...] = (_dot(h, w_ref[:, GLU_OFF:GATE_OFF]) + b_ref[:, GLU_OFF:GATE_OFF]).astype(BF16)
        gl_ref[...] = (_dot(h, w_ref[:, GATE_OFF:INW]) + b_ref[:, GATE_OFF:INW]).astype(BF16)

    return pl.pallas_call(
        body, name="in_proj", grid=(T // tm,),
        in_specs=[_rows(tm, D), _whole((1, D)), _whole((D, INW)), _whole((1, INW))],
        out_specs=[_rows(tm, D), _rows(tm, QKVW), _rows(tm, 2 * C), _rows(tm, 2 * D)],
        out_shape=[jax.ShapeDtypeStruct((T, D), BF16), jax.ShapeDtypeStruct((T, QKVW), BF16),
                   jax.ShapeDtypeStruct((T, 2 * C), BF16), jax.ShapeDtypeStruct((T, 2 * D), BF16)],
        compiler_params=_params("parallel"),
    )(x, g_mix, w_in, b_in)


def _attn_masks(i):
    qi = lax.broadcasted_iota(jnp.int32, (BLK, BLK), 0)
    kj = lax.broadcasted_iota(jnp.int32, (BLK, BLK), 1)
    mask_cur = kj <= qi
    mask_prev = jnp.logical_and(kj > qi, i > 0)
    return mask_cur, mask_prev


def _attn_fwd(qkv, sinks):
    def body(sink_ref, qkv_ref, o_ref, lse_ref):
        i = pl.program_id(0)
        r0 = pl.multiple_of(i * BLK, BLK)
        rp = pl.multiple_of(jnp.maximum(i - 1, 0) * BLK, BLK)
        mask_cur, mask_prev = _attn_masks(i)
        for h in range(NQ):
            kv = h // GROUP
            q = qkv_ref[pl.ds(r0, BLK), h * HD:(h + 1) * HD]
            kc = qkv_ref[pl.ds(r0, BLK), AW + kv * HD:AW + (kv + 1) * HD]
            kp = qkv_ref[pl.ds(rp, BLK), AW + kv * HD:AW + (kv + 1) * HD]
            vc = qkv_ref[pl.ds(r0, BLK), AW + KVW + kv * HD:AW + KVW + (kv + 1) * HD]
            vp = qkv_ref[pl.ds(rp, BLK), AW + KVW + kv * HD:AW + KVW + (kv + 1) * HD]
            sc = jnp.where(mask_cur, _dot_nt(q, kc) * SCALE, NEG)
            sp = jnp.where(mask_prev, _dot_nt(q, kp) * SCALE, NEG)
            sink = sink_ref[0, h]
            m = jnp.maximum(jnp.maximum(jnp.max(sc, axis=-1, keepdims=True),
                                        jnp.max(sp, axis=-1, keepdims=True)), sink)
            pc = jnp.exp(sc - m)
            pp = jnp.exp(sp - m)
            den = (jnp.sum(pc, axis=-1, keepdims=True) + jnp.sum(pp, axis=-1, keepdims=True)
                   + jnp.exp(sink - m))
            inv = 1.0 / den
            o = _dot((pc * inv).astype(BF16), vc) + _dot((pp * inv).astype(BF16), vp)
            o_ref[:, h * HD:(h + 1) * HD] = o.astype(BF16)
            lse_ref[:, h:h + 1] = m + jnp.log(den)

    return pl.pallas_call(
        body, name="attn_fwd", grid=(T // BLK,),
        in_specs=[pl.BlockSpec(memory_space=pltpu.SMEM), _whole((T, QKVW))],
        out_specs=[_rows(BLK, AW), _rows(BLK, NQ)],
        out_shape=[jax.ShapeDtypeStruct((T, AW), BF16), jax.ShapeDtypeStruct((T, NQ), F32)],
        compiler_params=_params("parallel"),
    )(sinks, qkv)


CONV_TM = 256
CONV_SUB = 32


def _glu(ab):
    a = ab[:, 0:C].astype(F32)
    b = ab[:, C:2 * C].astype(F32)
    return a * _sigmoid(b)


def _conv_fwd(glu, conv_w, conv_b, ln_g, ln_b):
    tm = CONV_TM

    def body(cur_ref, prev_ref, w_ref, cb_ref, g_ref, b_ref, u_ref, c_ref, zp_ref):
        i = pl.program_id(0)
        zprev = _glu(prev_ref[tm - HALO:tm, :])
        zp_ref[0:HALO, :] = jnp.where(i > 0, zprev, 0.0)
        zp_ref[HALO:HALO + tm, :] = _glu(cur_ref[...])
        for s in range(tm // CONV_SUB):
            base = HALO + s * CONV_SUB - (KW - 1)
            acc = jnp.broadcast_to(cb_ref[...], (CONV_SUB, C))
            for j in range(KW):
                acc = acc + w_ref[j:j + 1, :] * zp_ref[base + j:base + j + CONV_SUB, :]
            rows = slice(s * CONV_SUB, (s + 1) * CONV_SUB)
            u_ref[rows, :] = acc
            mu = jnp.mean(acc, axis=-1, keepdims=True)
            xc = acc - mu
            var = jnp.mean(xc * xc, axis=-1, keepdims=True)
            y = xc * lax.rsqrt(var + EPS) * g_ref[...] + b_ref[...]
            c_ref[rows, :] = (y * _sigmoid(y)).astype(BF16)

    return pl.pallas_call(
        body, name="conv_fwd", grid=(T // tm,),
        in_specs=[_rows(tm, 2 * C),
                  pl.BlockSpec((tm, 2 * C), lambda i: (jnp.maximum(i - 1, 0), 0)),
                  _whole((KW, C)), _whole((1, C)), _whole((1, C)), _whole((1, C))],
        out_specs=[_rows(tm, C), _rows(tm, C)],
        out_shape=[jax.ShapeDtypeStruct((T, C), F32), jax.ShapeDtypeStruct((T, C), BF16)],
        scratch_shapes=[pltpu.VMEM((HALO + tm, C), F32)],
        compiler_params=_params("parallel"),
    )(glu, glu, conv_w, conv_b, ln_g, ln_b)


def _mix_out(x, o, cact, gl, w_ap, w_cp, b_cp, w_out):
    tm = 256

    def body(x_ref, o_ref, c_ref, gl_ref, wap_ref, wcp_ref, bcp_ref, wo_ref,
             ya_ref, yc_ref, mg_ref, x1_ref):
        ya = _dot(o_ref[...], wap_ref[...])
        yc = _dot(c_ref[...], wcp_ref[...]) + bcp_ref[...]
        g0 = _sigmoid(gl_ref[:, 0:D].astype(F32))
        g1 = _sigmoid(gl_ref[:, D:2 * D].astype(F32))
        mg = (g0 * ya + g1 * yc).astype(BF16)
        ya_ref[...] = ya.astype(BF16)
        yc_ref[...] = yc.astype(BF16)
        mg_ref[...] = mg
        x1_ref[...] = x_ref[...] + _dot(mg, wo_ref[...])

    return pl.pallas_call(
        body, name="mix_out", grid=(T // tm,),
        in_specs=[_rows(tm, D), _rows(tm, AW), _rows(tm, C), _rows(tm, 2 * D),
                  _whole((AW, D)), _whole((C, D)), _whole((1, D)), _whole((D, D))],
        out_specs=[_rows(tm, D), _rows(tm, D), _rows(tm, D), _rows(tm, D)],
        out_shape=[jax.ShapeDtypeStruct((T, D), BF16), jax.ShapeDtypeStruct((T, D), BF16),
                   jax.ShapeDtypeStruct((T, D), BF16), jax.ShapeDtypeStruct((T, D), F32)],
        compiler_params=_params("parallel"),
    )(x, o, cact, gl, w_ap, w_cp, b_cp, w_out)


FFN_TN = 256


def _ffn_in(x1, g_ffn, w_fi):
    tm = 256

    def body(x_ref, g_ref, w_ref, h_ref, gu_ref, act_ref):
        xv = x_ref[...]
        r = lax.rsqrt(jnp.mean(xv * xv, axis=-1, keepdims=True) + EPS)
        h = (xv * r * g_ref[...]).astype(BF16)
        h_ref[...] = h
        for n in range(DFF // FFN_TN):
            c0 = n * FFN_TN
            gate = _dot(h, w_ref[:, c0:c0 + FFN_TN])
            up = _dot(h, w_ref[:, DFF + c0:DFF + c0 + FFN_TN])
            gu_ref[:, c0:c0 + FFN_TN] = gate.astype(BF16)
            gu_ref[:, DFF + c0:DFF + c0 + FFN_TN] = up.astype(BF16)
            act_ref[:, c0:c0 + FFN_TN] = (gate * _sigmoid(gate) * up).astype(BF16)

    return pl.pallas_call(
        body, name="ffn_in", grid=(T // tm,),
        in_specs=[_rows(tm, D), _whole((1, D)), _whole((D, 2 * DFF))],
        out_specs=[_rows(tm, D), _rows(tm, 2 * DFF), _rows(tm, DFF)],
        out_shape=[jax.ShapeDtypeStruct((T, D), BF16), jax.ShapeDtypeStruct((T, 2 * DFF), BF16),
                   jax.ShapeDtypeStruct((T, DFF), BF16)],
        compiler_params=_params("parallel"),
    )(x1, g_ffn, w_fi)


def _ffn_out_loss(x1, act, w_dn, g_final, target):
    tm = 256

    def body(x_ref, a_ref, w_ref, g_ref, t_ref, dx_ref, dxb_ref, dg_ref, loss_ref):
        i = pl.program_id(0)
        x2 = x_ref[...] + _dot(a_ref[...], w_ref[...])
        r = lax.rsqrt(jnp.mean(x2 * x2, axis=-1, keepdims=True) + EPS)
        xh = x2 * r
        g = g_ref[...]
        err = xh * g - t_ref[...]
        dy = err * (1.0 / D)
        dyg = dy * g
        dx = r * (dyg - xh * jnp.mean(dyg * xh, axis=-1, keepdims=True))
        dx_ref[...] = dx
        dxb_ref[...] = dx.astype(BF16)
        part = 0.5 * jnp.sum(jnp.mean(err * err, axis=-1, keepdims=True), axis=0, keepdims=True)

        @pl.when(i == 0)
        def _():
            dg_ref[...] = jnp.zeros_like(dg_ref)
            loss_ref[...] = jnp.zeros_like(loss_ref)

        dg_ref[...] += jnp.sum(dy * xh, axis=0, keepdims=True)
        loss_ref[...] += jnp.broadcast_to(part, loss_ref.shape)

    return pl.pallas_call(
        body, name="ffn_out_loss", grid=(T // tm,),
        in_specs=[_rows(tm, D), _rows(tm, DFF), _whole((DFF, D)), _whole((1, D)), _rows(tm, D)],
        out_specs=[_rows(tm, D), _rows(tm, D), _whole((1, D)), _whole((1, 128))],
        out_shape=[jax.ShapeDtypeStruct((T, D), F32), jax.ShapeDtypeStruct((T, D), BF16),
                   jax.ShapeDtypeStruct((1, D), F32), jax.ShapeDtypeStruct((1, 128), F32)],
        compiler_params=_params("arbitrary"),
    )(x1, act, w_dn, g_final, target)


def _const(shape):
    return pl.BlockSpec(shape, lambda i: tuple(0 for _ in shape), pipeline_mode=pl.Buffered(1))


def _ffn_bwd(dx2, dx2b, gu, x1, g_ffn, w_dn_t, w_fi_t):
    tm = 256

    def body(dx_ref, dxb_ref, gu_ref, x_ref, g_ref, wdn_ref, wfi_ref,
             dgu_ref, dx1_ref, dx1b_ref, dg_ref):
        i = pl.program_id(0)
        dxb = dxb_ref[...]
        dh = jnp.zeros((tm, D), F32)
        for n in range(DFF // FFN_TN):
            c0 = n * FFN_TN
            dact = _dot(dxb, wdn_ref[:, c0:c0 + FFN_TN])
            gate = gu_ref[:, c0:c0 + FFN_TN].astype(F32)
            up = gu_ref[:, DFF + c0:DFF + c0 + FFN_TN].astype(F32)
            s = _sigmoid(gate)
            dup = (dact * gate * s).astype(BF16)
            dgate = (dact * up * s * (1.0 + gate * (1.0 - s))).astype(BF16)
            dgu_ref[:, c0:c0 + FFN_TN] = dgate
            dgu_ref[:, DFF + c0:DFF + c0 + FFN_TN] = dup
            dh = dh + _dot(dgate, wfi_ref[c0:c0 + FFN_TN, :]) + _dot(dup, wfi_ref[DFF + c0:DFF + c0 + FFN_TN, :])
        xv = x_ref[...]
        r = lax.rsqrt(jnp.mean(xv * xv, axis=-1, keepdims=True) + EPS)
        xh = xv * r
        dhg = dh * g_ref[...]
        dx1 = dx_ref[...] + r * (dhg - xh * jnp.mean(dhg * xh, axis=-1, keepdims=True))
        dx1_ref[...] = dx1
        dx1b_ref[...] = dx1.astype(BF16)

        @pl.when(i == 0)
        def _():
            dg_ref[...] = jnp.zeros_like(dg_ref)

        dg_ref[...] += jnp.sum(dh * xh, axis=0, keepdims=True)

    return pl.pallas_call(
        body, name="ffn_bwd", grid=(T // tm,),
        in_specs=[_rows(tm, D), _rows(tm, D), _rows(tm, 2 * DFF), _rows(tm, D), _whole((1, D)),
                  _const((D, DFF)), _const((2 * DFF, D))],
        out_specs=[_rows(tm, 2 * DFF), _rows(tm, D), _rows(tm, D), _whole((1, D))],
        out_shape=[jax.ShapeDtypeStruct((T, 2 * DFF), BF16), jax.ShapeDtypeStruct((T, D), F32),
                   jax.ShapeDtypeStruct((T, D), BF16), jax.ShapeDtypeStruct((1, D), F32)],
        compiler_params=_params("arbitrary"),
    )(dx2, dx2b, gu, x1, g_ffn, w_dn_t, w_fi_t)


def _mix_bwd(dx1b, gl, ya, yc, w_out_t, w_ap_t, w_cp_t):
    tm = 256

    def body(dx_ref, gl_ref, ya_ref, yc_ref, wo_ref, wap_ref, wcp_ref,
             dya_ref, dyc_ref, dgl_ref, do_ref, dc_ref, db_ref):
        i = pl.program_id(0)
        dm = _dot(dx_ref[...], wo_ref[...])
        g0 = _sigmoid(gl_ref[:, 0:D].astype(F32))
        g1 = _sigmoid(gl_ref[:, D:2 * D].astype(F32))
        dya = dm * g0
        dyc = dm * g1
        dgl_ref[:, 0:D] = (dya * ya_ref[...].astype(F32) * (1.0 - g0)).astype(BF16)
        dgl_ref[:, D:2 * D] = (dyc * yc_ref[...].astype(F32) * (1.0 - g1)).astype(BF16)
        dyab = dya.astype(BF16)
        dycb = dyc.astype(BF16)
        dya_ref[...] = dyab
        dyc_ref[...] = dycb
        do_ref[...] = _dot(dyab, wap_ref[...]).astype(BF16)
        dc_ref[...] = _dot(dycb, wcp_ref[...])

        @pl.when(i == 0)
        def _():
            db_ref[...] = jnp.zeros_like(db_ref)

        db_ref[...] += jnp.sum(dyc, axis=0, keepdims=True)

    return pl.pallas_call(
        body, name="mix_bwd", grid=(T // tm,),
        in_specs=[_rows(tm, D), _rows(tm, 2 * D), _rows(tm, D), _rows(tm, D),
                  _whole((D, D)), _whole((D, AW)), _whole((D, C))],
        out_specs=[_rows(tm, D), _rows(tm, D), _rows(tm, 2 * D), _rows(tm, AW), _rows(tm, C),
                   _whole((1, D))],
        out_shape=[jax.ShapeDtypeStruct((T, D), BF16), jax.ShapeDtypeStruct((T, D), BF16),
                   jax.ShapeDtypeStruct((T, 2 * D), BF16), jax.ShapeDtypeStruct((T, AW), BF16),
                   jax.ShapeDtypeStruct((T, C), F32), jax.ShapeDtypeStruct((1, D), F32)],
        compiler_params=_params("arbitrary"),
    )(dx1b, gl, ya, yc, w_out_t, w_ap_t, w_cp_t)


def _conv_bwd(glu, u, dc, conv_w, ln_g, ln_b):
    tm = CONV_TM
    nblk = T // tm

    def du_of(uv, dcv, g_ref, b_ref):
        mu = jnp.mean(uv, axis=-1, keepdims=True)
        xc = uv - mu
        var = jnp.mean(xc * xc, axis=-1, keepdims=True)
        rstd = lax.rsqrt(var + EPS)
        xh = xc * rstd
        y = xh * g_ref[...] + b_ref[...]
        sg = _sigmoid(y)
        dy = dcv * (sg * (1.0 + y * (1.0 - sg)))
        dxh = dy * g_ref[...]
        du = rstd * (dxh - jnp.mean(dxh, axis=-1, keepdims=True)
                     - xh * jnp.mean(dxh * xh, axis=-1, keepdims=True))
        return du, dy, xh

    def body(cur_ref, prev_ref, u_ref, un_ref, dc_ref, dcn_ref, w_ref, g_ref, b_ref,
             dglu_ref, dw_ref, dcb_ref, dg_ref, db_ref, zp_ref, du_ref):
        i = pl.program_id(0)

        @pl.when(i == 0)
        def _():
            dw_ref[...] = jnp.zeros_like(dw_ref)
            dcb_ref[...] = jnp.zeros_like(dcb_ref)
            dg_ref[...] = jnp.zeros_like(dg_ref)
            db_ref[...] = jnp.zeros_like(db_ref)

        zprev = _glu(prev_ref[tm - HALO:tm, :])
        zp_ref[0:HALO, :] = jnp.where(i > 0, zprev, 0.0)
        zp_ref[HALO:HALO + tm, :] = _glu(cur_ref[...])

        dun, _, _ = du_of(un_ref[0:HALO, :], dcn_ref[0:HALO, :], g_ref, b_ref)
        du_ref[tm:tm + HALO, :] = jnp.where(i < nblk - 1, dun, 0.0)
        dg_acc = jnp.zeros((1, C), F32)
        db_acc = jnp.zeros((1, C), F32)
        dcb_acc = jnp.zeros((1, C), F32)
        for s in range(tm // CONV_SUB):
            rows = slice(s * CONV_SUB, (s + 1) * CONV_SUB)
            du, dy, xh = du_of(u_ref[rows, :], dc_ref[rows, :], g_ref, b_ref)
            du_ref[rows, :] = du
            dg_acc = dg_acc + jnp.sum(dy * xh, axis=0, keepdims=True)
            db_acc = db_acc + jnp.sum(dy, axis=0, keepdims=True)
            dcb_acc = dcb_acc + jnp.sum(du, axis=0, keepdims=True)
        dg_ref[...] += dg_acc
        db_ref[...] += db_acc
        dcb_ref[...] += dcb_acc

        for j in range(KW):
            acc = jnp.zeros((CONV_SUB, C), F32)
            for s in range(tm // CONV_SUB):
                base = HALO + s * CONV_SUB - (KW - 1) + j
                acc = acc + du_ref[s * CONV_SUB:(s + 1) * CONV_SUB, :] * zp_ref[base:base + CONV_SUB, :]
            dw_ref[j:j + 1, :] += jnp.sum(acc, axis=0, keepdims=True)

        for s in range(tm // CONV_SUB):
            rows = slice(s * CONV_SUB, (s + 1) * CONV_SUB)
            dz = jnp.zeros((CONV_SUB, C), F32)
            for j in range(KW):
                o = s * CONV_SUB + (KW - 1) - j
                dz = dz + w_ref[j:j + 1, :] * du_ref[o:o + CONV_SUB, :]
            a = cur_ref[rows, 0:C].astype(F32)
            sb = _sigmoid(cur_ref[rows, C:2 * C].astype(F32))
            dglu_ref[rows, 0:C] = (dz * sb).astype(BF16)
            dglu_ref[rows, C:2 * C] = (dz * a * sb * (1.0 - sb)).astype(BF16)

    nxt = lambda i: (jnp.minimum(i + 1, nblk - 1), 0)
    return pl.pallas_call(
        body, name="conv_bwd", grid=(nblk,),
        in_specs=[_rows(tm, 2 * C),
                  pl.BlockSpec((tm, 2 * C), lambda i: (jnp.maximum(i - 1, 0), 0)),
                  _rows(tm, C), pl.BlockSpec((tm, C), nxt),
                  _rows(tm, C), pl.BlockSpec((tm, C), nxt),
                  _whole((KW, C)), _whole((1, C)), _whole((1, C))],
        out_specs=[_rows(tm, 2 * C), _whole((KW, C)), _whole((1, C)), _whole((1, C)), _whole((1, C))],
        out_shape=[jax.ShapeDtypeStruct((T, 2 * C), BF16), jax.ShapeDtypeStruct((KW, C), F32),
                   jax.ShapeDtypeStruct((1, C), F32), jax.ShapeDtypeStruct((1, C), F32),
                   jax.ShapeDtypeStruct((1, C), F32)],
        scratch_shapes=[pltpu.VMEM((HALO + tm, C), F32), pltpu.VMEM((tm + HALO, C), F32)],
        compiler_params=_params("arbitrary"),
    )(glu, glu, u, u, dc, dc, conv_w, ln_g, ln_b)


def _attn_bwd(qkv, o, do, lse, sinks):
    def body(sink_ref, qkv_ref, o_ref, do_ref, lse_ref, dq_ref, dkv_ref, ds_ref):
        i = pl.program_id(0)

        @pl.when(i == 0)
        def _():
            dkv_ref[...] = jnp.zeros_like(dkv_ref)
            ds_ref[...] = jnp.zeros_like(ds_ref)

        r0 = pl.multiple_of(i * BLK, BLK)
        rp = pl.multiple_of(jnp.maximum(i - 1, 0) * BLK, BLK)
        mask_cur, mask_prev = _attn_masks(i)
        for h in range(NQ):
            kv = h // GROUP
            kcol = slice(AW + kv * HD, AW + (kv + 1) * HD)
            vcol = slice(AW + KVW + kv * HD, AW + KVW + (kv + 1) * HD)
            hcol = slice(h * HD, (h + 1) * HD)
            q = qkv_ref[pl.ds(r0, BLK), hcol]
            kc = qkv_ref[pl.ds(r0, BLK), kcol]
            kp = qkv_ref[pl.ds(rp, BLK), kcol]
            vc = qkv_ref[pl.ds(r0, BLK), vcol]
            vp = qkv_ref[pl.ds(rp, BLK), vcol]
            doh = do_ref[:, hcol]
            lse_h = lse_ref[:, h:h + 1]
            dl = jnp.sum(doh.astype(F32) * o_ref[:, hcol].astype(F32), axis=-1, keepdims=True)
            pc = jnp.where(mask_cur, jnp.exp(_dot_nt(q, kc) * SCALE - lse_h), 0.0)
            pp = jnp.where(mask_prev, jnp.exp(_dot_nt(q, kp) * SCALE - lse_h), 0.0)
            dsc = (pc * (_dot_nt(doh, vc) - dl)).astype(BF16)
            dsp = (pp * (_dot_nt(doh, vp) - dl)).astype(BF16)
            dq_ref[:, hcol] = ((_dot(dsc, kc) + _dot(dsp, kp)) * SCALE).astype(BF16)
            dkc = slice(kv * HD, (kv + 1) * HD)
            dvc = slice(KVW + kv * HD, KVW + (kv + 1) * HD)
            dkv_ref[pl.ds(r0, BLK), dkc] += _dot_tn(dsc, q) * SCALE
            dkv_ref[pl.ds(rp, BLK), dkc] += _dot_tn(dsp, q) * SCALE
            dkv_ref[pl.ds(r0, BLK), dvc] += _dot_tn(pc.astype(BF16), doh)
            dkv_ref[pl.ds(rp, BLK), dvc] += _dot_tn(pp.astype(BF16), doh)
            psink = jnp.exp(sink_ref[0, h] - lse_h)
            dsink = -jnp.sum(psink * dl, axis=0, keepdims=True)
            ds_ref[h:h + 1, :] += jnp.broadcast_to(dsink, (1, 128))

    return pl.pallas_call(
        body, name="attn_bwd", grid=(T // BLK,),
        in_specs=[pl.BlockSpec(memory_space=pltpu.SMEM), _whole((T, QKVW)),
                  _rows(BLK, AW), _rows(BLK, AW), _rows(BLK, NQ)],
        out_specs=[_rows(BLK, AW), _whole((T, 2 * KVW)), _whole((NQ, 128))],
        out_shape=[jax.ShapeDtypeStruct((T, AW), BF16), jax.ShapeDtypeStruct((T, 2 * KVW), F32),
                   jax.ShapeDtypeStruct((NQ, 128), F32)],
        compiler_params=_params("arbitrary"),
    )(sinks, qkv, o, do, lse)


def _in_proj_bwd(dproj, x, dx1, g_mix, w_in_t):
    tm = 256

    def body(dp_ref, x_ref, dx1_ref, g_ref, w_ref, gx_ref, dg_ref, db_ref):
        i = pl.program_id(0)
        dp = dp_ref[...]
        dh = _dot(dp, w_ref[...])
        xv = x_ref[...]
        r = lax.rsqrt(jnp.mean(xv * xv, axis=-1, keepdims=True) + EPS)
        xh = xv * r
        dhg = dh * g_ref[...]
        gx_ref[...] = dx1_ref[...] + r * (dhg - xh * jnp.mean(dhg * xh, axis=-1, keepdims=True))

        @pl.when(i == 0)
        def _():
            dg_ref[...] = jnp.zeros_like(dg_ref)
            db_ref[...] = jnp.zeros_like(db_ref)

        dg_ref[...] += jnp.sum(dh * xh, axis=0, keepdims=True)
        db_ref[...] += jnp.sum(dp.astype(F32), axis=0, keepdims=True)

    return pl.pallas_call(
        body, name="in_proj_bwd", grid=(T // tm,),
        in_specs=[_rows(tm, INW), _rows(tm, D), _rows(tm, D), _whole((1, D)), _const((INW, D))],
        out_specs=[_rows(tm, D), _whole((1, D)), _whole((1, INW))],
        out_shape=[jax.ShapeDtypeStruct((T, D), F32), jax.ShapeDtypeStruct((1, D), F32),
                   jax.ShapeDtypeStruct((1, INW), F32)],
        compiler_params=_params("arbitrary"),
    )(dproj, x, dx1, g_mix, w_in_t)


def _grad_w(a, b, name, tk, tn):
    k, n = a.shape[1], b.shape[1]

    def body(a_ref, b_ref, o_ref):
        o_ref[...] = _dot_tn(a_ref[...], b_ref[...]).astype(BF16)

    return pl.pallas_call(
        body, name=name, grid=(k // tk, n // tn),
        in_specs=[pl.BlockSpec((T, tk), lambda i, j: (0, i)), pl.BlockSpec((T, tn), lambda i, j: (0, j))],
        out_specs=pl.BlockSpec((tk, tn), lambda i, j: (i, j)),
        out_shape=jax.ShapeDtypeStruct((k, n), BF16),
        compiler_params=_params("parallel", "parallel"),
    )(a, b)


def _local_step(x, target, small, w_in, w_ap, w_cp, w_out, w_fi, w_dn):
    h, qkv, glu, gl = _in_proj(x, small["g_mix_norm"], w_in, small["b_in"])
    o, lse = _attn_fwd(qkv, small["sinks"])
    u, cact = _conv_fwd(glu, small["conv_w"], small["conv_b"], small["ln_g"], small["ln_b"])
    ya, yc, mg, x1 = _mix_out(x, o, cact, gl, w_ap, w_cp, small["b_conv_proj"], w_out)
    h2, gu, act = _ffn_in(x1, small["g_ffn_norm"], w_fi)
    dx2, dx2b, dg_final, loss = _ffn_out_loss(x1, act, w_dn, small["g_final"], target)

    dgu, dx1, dx1b, dg_ffn = _ffn_bwd(dx2, dx2b, gu, x1, small["g_ffn_norm"], w_dn.T, w_fi.T)
    dya, dyc, dgl, do, dc, db_cp = _mix_bwd(dx1b, gl, ya, yc, w_out.T, w_ap.T, w_cp.T)
    dglu, dconv_w, dconv_b, dln_g, dln_b = _conv_bwd(glu, u, dc, small["conv_w"], small["ln_g"], small["ln_b"])
    dq, dkv, dsinks = _attn_bwd(qkv, o, do, lse, small["sinks"])
    dproj = jnp.concatenate([dq, dkv.astype(BF16), dglu, dgl], axis=1)
    grad_x, dg_mix, db_in = _in_proj_bwd(dproj, x, dx1, small["g_mix_norm"], w_in.T)

    gw = {
        "w_in": _grad_w(h, dproj, "grad_w_in", 512, 768),
        "w_attn_proj": _grad_w(o, dya, "grad_w_attn_proj", 512, 512),
        "w_conv_proj": _grad_w(cact, dyc, "grad_w_conv_proj", 512, 512),
        "w_out": _grad_w(mg, dx1b, "grad_w_out", 512, 512),
        "w_ffn_in": _grad_w(h2, dgu, "grad_w_ffn_in", 512, 512),
        "w_ffn_down": _grad_w(act, dx2b, "grad_w_ffn_down", 256, 512),
    }
    gs = {
        "g_mix_norm": dg_mix, "b_in": db_in, "sinks": dsinks[:, 0].reshape(1, NQ),
        "conv_w": dconv_w, "conv_b": dconv_b, "ln_g": dln_g, "ln_b": dln_b,
        "b_conv_proj": db_cp, "g_ffn_norm": dg_ffn, "g_final": dg_final,
    }
    return loss[0, 0], grad_x, gw, gs


HBM_SPEC = pl.BlockSpec(memory_space=pltpu.HBM)
N_CHIPS = 4


def _place():
    x, y, c = lax.axis_index("x"), lax.axis_index("y"), lax.axis_index("c")
    chips = [(1 - x, y), (x, 1 - y), (1 - x, 1 - y)]
    return x, y, c, chips


def _gather_weights(shards):
    n = len(shards)

    def body(*refs):
        ins, outs = refs[:n], refs[n:2 * n]
        send_sems, recv_sems = refs[2 * n:]
        x, y, c, chips = _place()
        me, sibling = (x, y, c), (x, y, 1 - c)
        s_me = 2 * x + y

        def half(ref, s, cc):
            kh = ref.shape[1] // 2
            return ref.at[s, pl.ds(cc * kh, kh)]

        def remote(w, k, src, dst, to):
            return pltpu.make_async_remote_copy(
                src_ref=src, dst_ref=dst, send_sem=send_sems.at[6 * w + k],
                recv_sem=recv_sems.at[6 * w + k], device_id=to, device_id_type=MESH)

        sends = []
        for w in range(n):
            kh = ins[w].shape[0] // 2
            for j, chip in enumerate(chips):
                cp = remote(w, j, ins[w].at[pl.ds(c * kh, kh)], half(outs[w], s_me, c), (*chip, c))
                cp.start()
                sends.append(cp)
        for w in range(n):
            for j, chip in enumerate(chips):
                blk = half(outs[w], 2 * chip[0] + chip[1], c)
                remote(w, j, blk, blk, me).wait_recv()
                cp = remote(w, 3 + j, blk, blk, sibling)
                cp.start()
                sends.append(cp)
        for w in range(n):
            for j, chip in enumerate(chips):
                blk = half(outs[w], 2 * chip[0] + chip[1], 1 - c)
                remote(w, 3 + j, blk, blk, me).wait_recv()
        for cp in sends:
            cp.wait_send()

    return pl.pallas_call(
        body, name="gather_weights",
        in_specs=[HBM_SPEC] * n, out_specs=[HBM_SPEC] * n,
        out_shape=[jax.ShapeDtypeStruct((N_CHIPS,) + s.shape, s.dtype) for s in shards],
        scratch_shapes=[pltpu.SemaphoreType.DMA((6 * n,)), pltpu.SemaphoreType.DMA((6 * n,))],
    )(*shards)


def _pair_exchange(grads):
    n = len(grads)

    def body(*refs):
        ins, outs = refs[:n], refs[n:2 * n]
        send_sems, recv_sems = refs[2 * n:]
        x, y, c, _ = _place()
        copies = []
        for w in range(n):
            kh = ins[w].shape[1] // 2
            cp = pltpu.make_async_remote_copy(
                src_ref=ins[w].at[:, pl.ds((1 - c) * kh, kh)], dst_ref=outs[w],
                send_sem=send_sems.at[w], recv_sem=recv_sems.at[w],
                device_id=(x, y, 1 - c), device_id_type=MESH)
            cp.start()
            copies.append(cp)
        for cp in copies:
            cp.wait()

    return pl.pallas_call(
        body, name="pair_exchange",
        in_specs=[HBM_SPEC] * n, out_specs=[HBM_SPEC] * n,
        out_shape=[jax.ShapeDtypeStruct((g.shape[0], g.shape[1] // 2, g.shape[2]), g.dtype) for g in grads],
        scratch_shapes=[pltpu.SemaphoreType.DMA((n,)), pltpu.SemaphoreType.DMA((n,))],
    )(*grads)


def _row_tile(k):
    for t in (256, 128, 176, 64, 32, 16):
        if k % t == 0:
            return t
    raise ValueError(k)


def _pair_sum(c_idx, g, got, name):
    _, k, n = g.shape
    kh = k // 2
    tm = _row_tile(kh)
    nb = kh // tm

    def body(c_ref, g_ref, r_ref, o_ref):
        o_ref[...] = (g_ref[...].astype(F32) + r_ref[...].astype(F32)).astype(BF16)

    return pl.pallas_call(
        body, name=name,
        grid_spec=pltpu.PrefetchScalarGridSpec(
            num_scalar_prefetch=1, grid=(N_CHIPS, nb),
            in_specs=[pl.BlockSpec((1, tm, n), lambda s, i, c_ref: (s, c_ref[0] * nb + i, 0)),
                      pl.BlockSpec((1, tm, n), lambda s, i, c_ref: (s, i, 0))],
            out_specs=pl.BlockSpec((1, tm, n), lambda s, i, c_ref: (s, i, 0))),
        out_shape=jax.ShapeDtypeStruct((N_CHIPS, kh, n), BF16),
        compiler_params=_params("parallel", "parallel"),
    )(c_idx, g, got)


def _chip_exchange(sums):
    n = len(sums)

    def body(*refs):
        ins, outs = refs[:n], refs[n:2 * n]
        send_sems, recv_sems = refs[2 * n:]
        x, y, c, chips = _place()
        copies = []
        for w in range(n):
            for j, chip in enumerate(chips):
                cp = pltpu.make_async_remote_copy(
                    src_ref=ins[w].at[2 * chip[0] + chip[1]], dst_ref=outs[w].at[j],
                    send_sem=send_sems.at[3 * w + j], recv_sem=recv_sems.at[3 * w + j],
                    device_id=(*chip, c), device_id_type=MESH)
                cp.start()
                copies.append(cp)
        for cp in copies:
            cp.wait()

    return pl.pallas_call(
        body, name="chip_exchange",
        in_specs=[HBM_SPEC] * n, out_specs=[HBM_SPEC] * n,
        out_shape=[jax.ShapeDtypeStruct((3,) + s.shape[1:], s.dtype) for s in sums],
        scratch_shapes=[pltpu.SemaphoreType.DMA((3 * n,)), pltpu.SemaphoreType.DMA((3 * n,))],
    )(*sums)


def _chip_sum(sc_idx, mine, got, name):
    _, kh, n = mine.shape
    tm = _row_tile(kh)

    def body(sc_ref, m_ref, r_ref, o_ref):
        acc = m_ref[0].astype(F32)
        for j in range(3):
            acc = acc + r_ref[j].astype(F32)
        o_ref[0] = acc

    return pl.pallas_call(
        body, name=name,
        grid_spec=pltpu.PrefetchScalarGridSpec(
            num_scalar_prefetch=1, grid=(kh // tm,),
            in_specs=[pl.BlockSpec((1, tm, n), lambda i, sc_ref: (sc_ref[0], i, 0)),
                      pl.BlockSpec((3, tm, n), lambda i, sc_ref: (0, i, 0))],
            out_specs=pl.BlockSpec((1, tm, n), lambda i, sc_ref: (sc_ref[1], i, 0))),
        out_shape=jax.ShapeDtypeStruct((2, kh, n), F32),
        compiler_params=_params("parallel"),
    )(sc_idx, mine, got)


def _pair_share(halves):
    n = len(halves)

    def body(*refs):
        bufs = refs[n:2 * n]
        send_sems, recv_sems = refs[2 * n:]
        x, y, c, _ = _place()
        copies = []
        for w in range(n):
            cp = pltpu.make_async_remote_copy(
                src_ref=bufs[w].at[c], dst_ref=bufs[w].at[c],
                send_sem=send_sems.at[w], recv_sem=recv_sems.at[w],
                device_id=(x, y, 1 - c), device_id_type=MESH)
            cp.start()
            copies.append(cp)
        for w in range(n):
            copies[w].wait_send()
            pltpu.make_async_remote_copy(
                src_ref=bufs[w].at[1 - c], dst_ref=bufs[w].at[1 - c],
                send_sem=send_sems.at[w], recv_sem=recv_sems.at[w],
                device_id=(x, y, c), device_id_type=MESH).wait_recv()

    return pl.pallas_call(
        body, name="pair_share",
        in_specs=[HBM_SPEC] * n, out_specs=[HBM_SPEC] * n,
        out_shape=[jax.ShapeDtypeStruct(h.shape, h.dtype) for h in halves],
        input_output_aliases={w: w for w in range(n)},
        scratch_shapes=[pltpu.SemaphoreType.DMA((n,)), pltpu.SemaphoreType.DMA((n,))],
    )(*halves)


def _gather_small(blocks):
    n = len(blocks)

    def body(*refs):
        ins, outs = refs[:n], refs[n:2 * n]
        send_sems, recv_sems, local_sems = refs[2 * n:]
        x, y, c, chips = _place()
        me, sibling = (x, y, c), (x, y, 1 - c)

        def rows(w, px, py, pc):
            m = ins[w].shape[0]
            return outs[w].at[pl.ds((4 * px + 2 * py + pc) * m, m), :]

        def copy(w, k, block, to, src=None):
            return pltpu.make_async_remote_copy(
                src_ref=rows(w, *block) if src is None else src, dst_ref=rows(w, *block),
                send_sem=send_sems.at[7 * w + k], recv_sem=recv_sems.at[7 * w + k],
                device_id=to, device_id_type=MESH)

        started, owns = [], []
        for w in range(n):
            own = pltpu.make_async_copy(ins[w], rows(w, *me), local_sems.at[w])
            own.start()
            owns.append(own)
            first = [copy(w, 0, me, sibling, src=ins[w])]
            first += [copy(w, 1 + j, me, (*chip, c), src=ins[w]) for j, chip in enumerate(chips)]
            for cp in first:
                cp.start()
            started += first
        for w in range(n):
            for j, chip in enumerate(chips):
                copy(w, 1 + j, (*chip, c), me).wait_recv()
                cp = copy(w, 4 + j, (*chip, c), sibling)
                cp.start()
                started.append(cp)
        for w in range(n):
            copy(w, 0, sibling, me).wait_recv()
            for j, chip in enumerate(chips):
                copy(w, 4 + j, (*chip, 1 - c), me).wait_recv()
        for cp in started:
            cp.wait_send()
        for own in owns:
            own.wait()

    vmem = pl.BlockSpec(memory_space=pltpu.VMEM)
    return pl.pallas_call(
        body, name="gather_small",
        in_specs=[vmem] * n, out_specs=[vmem] * n,
        out_shape=[jax.ShapeDtypeStruct((8 * b.shape[0], b.shape[1]), b.dtype) for b in blocks],
        scratch_shapes=[pltpu.SemaphoreType.DMA((7 * n,)), pltpu.SemaphoreType.DMA((7 * n,)),
                        pltpu.SemaphoreType.DMA((n,))],
    )(*blocks)


def _adamw_math(w, g, m, v):
    m = ADAM_B1 * m + (1.0 - ADAM_B1) * g
    v = ADAM_B2 * v + (1.0 - ADAM_B2) * (g * g)
    m_hat = m / (1.0 - ADAM_B1 ** ADAM_STEP)
    v_hat = v / (1.0 - ADAM_B2 ** ADAM_STEP)
    delta = -ADAM_LR * (m_hat / (jnp.sqrt(v_hat) + ADAM_EPS) + ADAM_WD * w)
    return delta, m, v


def _adamw(w, g, m, v, name):
    k, n = w.shape
    tm = k // 4

    def body(w_ref, g_ref, m_ref, v_ref, d_ref, mo_ref, vo_ref):
        d, mm, vv = _adamw_math(w_ref[...], g_ref[...], m_ref[...], v_ref[...])
        d_ref[...] = d
        mo_ref[...] = mm
        vo_ref[...] = vv

    spec = pl.BlockSpec((tm, n), lambda i: (i, 0))
    shp = jax.ShapeDtypeStruct((k, n), F32)
    return pl.pallas_call(
        body, name=name, grid=(4,), in_specs=[spec] * 4, out_specs=[spec] * 3,
        out_shape=[shp] * 3, compiler_params=_params("parallel"),
    )(w, g, m, v)


VEC_SLOTS = {
    "g_mix_norm": (0, 0, D), "b_conv_proj": (0, D, D), "g_ffn_norm": (0, 2 * D, D),
    "g_final": (0, 3 * D, D), "b_in": (1, 0, INW), "conv_b": (2, 0, C), "ln_g": (2, C, C),
    "ln_b": (2, 2 * C, C), "sinks": (2, 3 * C, NQ), "loss": (2, 3 * C + 128, 1),
}
VEC_ROWS, VEC_COLS = 8, 4 * D
CW_ROWS = 32
SMALL_NAMES = ["g_mix_norm", "b_in", "sinks", "conv_w", "conv_b", "ln_g", "ln_b",
               "b_conv_proj", "g_ffn_norm", "g_final"]
CW_LANES = C // N_CHIPS


def _pack_small(gs, loss):
    row0 = jnp.concatenate([gs["g_mix_norm"], gs["b_conv_proj"], gs["g_ffn_norm"], gs["g_final"]], axis=1)
    row1 = jnp.pad(gs["b_in"], ((0, 0), (0, VEC_COLS - INW)))
    row2 = jnp.concatenate([gs["conv_b"], gs["ln_g"], gs["ln_b"],
                            jnp.pad(gs["sinks"], ((0, 0), (0, 128 - NQ))),
                            jnp.pad(loss.reshape(1, 1), ((0, 0), (0, VEC_COLS - 3 * C - 129)))], axis=1)
    vec = jnp.concatenate([row0, row1, row2, jnp.zeros((VEC_ROWS - 3, VEC_COLS), F32)], axis=0)
    cw = jnp.pad(gs["conv_w"], ((0, CW_ROWS - KW), (0, 0)))
    return vec, cw


def _small_update(s_idx, vec_all, cw_all, wmv):
    nsm = len(SMALL_NAMES)

    def body(s_ref, vec_ref, cw_ref, *refs):
        ins = refs[:3 * nsm]
        outs = refs[3 * nsm:7 * nsm]
        loss_ref = refs[7 * nsm]

        def total(slot):
            row, lane, width = slot
            acc = vec_ref[row:row + 1, lane:lane + width]
            for k in range(1, 8):
                acc = acc + vec_ref[k * VEC_ROWS + row:k * VEC_ROWS + row + 1, lane:lane + width]
            return acc

        loss_ref[...] = jnp.broadcast_to(total(VEC_SLOTS["loss"]), loss_ref.shape)
        for p, name in enumerate(SMALL_NAMES):
            w_ref, m_ref, v_ref = ins[3 * p:3 * p + 3]
            g_ref, d_ref, mo_ref, vo_ref = outs[4 * p:4 * p + 4]
            if name == "conv_w":
                g = jnp.zeros((KW, CW_LANES), F32)
                for s in range(N_CHIPS):
                    cand = cw_ref[0:KW, s * CW_LANES:(s + 1) * CW_LANES]
                    for k in range(1, 8):
                        cand = cand + cw_ref[k * CW_ROWS:k * CW_ROWS + KW, s * CW_LANES:(s + 1) * CW_LANES]
                    g = jnp.where(s_ref[0] == s, cand, g)
            else:
                g = total(VEC_SLOTS[name])
            d, mm, vv = _adamw_math(w_ref[...], g, m_ref[...], v_ref[...])
            g_ref[...] = g
            d_ref[...] = d
            mo_ref[...] = mm
            vo_ref[...] = vv

    vmem = pl.BlockSpec(memory_space=pltpu.VMEM)
    flat = [a for t in wmv for a in t]
    out_shape = []
    for w, _, _ in wmv:
        out_shape += [jax.ShapeDtypeStruct(w.shape, F32)] * 4
    out_shape.append(jax.ShapeDtypeStruct((1, 128), F32))
    res = pl.pallas_call(
        body, name="small_update",
        in_specs=[pl.BlockSpec(memory_space=pltpu.SMEM)] + [vmem] * (2 + len(flat)),
        out_specs=[vmem] * len(out_shape), out_shape=out_shape,
    )(s_idx, vec_all, cw_all, *flat)
    return [tuple(res[4 * p:4 * p + 4]) for p in range(nsm)], res[4 * nsm]


BIG_NAMES = ["w_in", "w_attn_proj", "w_conv_proj", "w_out", "w_ffn_in", "w_ffn_down"]
COL_SHARDED = {"w_in": True, "w_attn_proj": True, "w_conv_proj": True, "w_out": False,
               "w_ffn_in": True, "w_ffn_down": False}
WEIGHT_ORDER = ["g_mix_norm", "w_in", "b_in", "sinks", "conv_w", "conv_b", "ln_g", "ln_b",
                "w_attn_proj", "w_conv_proj", "b_conv_proj", "w_out", "g_ffn_norm", "w_ffn_in",
                "w_ffn_down", "g_final"]


def _assemble(name, gathered):
    s, k, n = gathered.shape
    if COL_SHARDED[name]:
        return gathered.transpose(1, 0, 2).reshape(k, s * n)
    return gathered.reshape(s * k, n)


def _split(name, full):
    k, n = full.shape
    if COL_SHARDED[name]:
        return full.reshape(k, N_CHIPS, n // N_CHIPS).transpose(1, 0, 2)
    return full.reshape(N_CHIPS, k // N_CHIPS, n)


def kernel(x, g_mix_norm, w_in, b_in, sinks, conv_w, conv_b, ln_g, ln_b, w_attn_proj, w_conv_proj, b_conv_proj, w_out, g_ffn_norm, w_ffn_in, w_ffn_down, g_final, loss_target, m_g_mix_norm, m_w_in, m_b_in, m_sinks, m_conv_w, m_conv_b, m_ln_g, m_ln_b, m_w_attn_proj, m_w_conv_proj, m_b_conv_proj, m_w_out, m_g_ffn_norm, m_w_ffn_in, m_w_ffn_down, m_g_final, v_g_mix_norm, v_w_in, v_b_in, v_sinks, v_conv_w, v_conv_b, v_ln_g, v_ln_b, v_w_attn_proj, v_w_conv_proj, v_b_conv_proj, v_w_out, v_g_ffn_norm, v_w_ffn_in, v_w_ffn_down, v_g_final):
    w = dict(g_mix_norm=g_mix_norm, w_in=w_in, b_in=b_in, sinks=sinks, conv_w=conv_w, conv_b=conv_b,
             ln_g=ln_g, ln_b=ln_b, w_attn_proj=w_attn_proj, w_conv_proj=w_conv_proj,
             b_conv_proj=b_conv_proj, w_out=w_out, g_ffn_norm=g_ffn_norm, w_ffn_in=w_ffn_in,
             w_ffn_down=w_ffn_down, g_final=g_final)
    m = dict(g_mix_norm=m_g_mix_norm, w_in=m_w_in, b_in=m_b_in, sinks=m_sinks, conv_w=m_conv_w,
             conv_b=m_conv_b, ln_g=m_ln_g, ln_b=m_ln_b, w_attn_proj=m_w_attn_proj,
             w_conv_proj=m_w_conv_proj, b_conv_proj=m_b_conv_proj, w_out=m_w_out,
             g_ffn_norm=m_g_ffn_norm, w_ffn_in=m_w_ffn_in, w_ffn_down=m_w_ffn_down, g_final=m_g_final)
    v = dict(g_mix_norm=v_g_mix_norm, w_in=v_w_in, b_in=v_b_in, sinks=v_sinks, conv_w=v_conv_w,
             conv_b=v_conv_b, ln_g=v_ln_g, ln_b=v_ln_b, w_attn_proj=v_w_attn_proj,
             w_conv_proj=v_w_conv_proj, b_conv_proj=v_b_conv_proj, w_out=v_w_out,
             g_ffn_norm=v_g_ffn_norm, w_ffn_in=v_w_ffn_in, w_ffn_down=v_w_ffn_down, g_final=v_g_final)

    c_idx = lax.axis_index("c").astype(jnp.int32).reshape(1)
    s_idx = (2 * lax.axis_index("x") + lax.axis_index("y")).astype(jnp.int32).reshape(1)

    shards = [w[name][0].astype(BF16) for name in BIG_NAMES]
    shards.append(jnp.pad(conv_w[0], ((0, CW_ROWS - KW), (0, 0))))
    chip_iota = lax.broadcasted_iota(jnp.int32, (N_CHIPS, 1, 1), 0)
    gathered = [jnp.where(chip_iota == s_idx[0], own[None], got)
                for own, got in zip(shards, _gather_weights(shards))]
    full ={name: _assemble(name, g) for name, g in zip(BIG_NAMES, gathered)}
    conv_w_full = gathered[-1].transpose(1, 0, 2).reshape(CW_ROWS, C)[:KW]

    small = dict(g_mix_norm=g_mix_norm, b_in=b_in, sinks=sinks, conv_w=conv_w_full, conv_b=conv_b,
                 ln_g=ln_g, ln_b=ln_b, b_conv_proj=b_conv_proj, g_ffn_norm=g_ffn_norm,
                 g_final=g_final.reshape(1, D))
    loss_part, grad_x, gw, gs = _local_step(
        x[0], loss_target[0], small, full["w_in"], full["w_attn_proj"], full["w_conv_proj"],
        full["w_out"], full["w_ffn_in"], full["w_ffn_down"])

    grads = [_split(name, gw[name]) for name in BIG_NAMES]
    from_sibling = _pair_exchange(grads)
    pair = [_pair_sum(c_idx, g, r, "pair_sum_" + name) for name, g, r in zip(BIG_NAMES, grads, from_sibling)]
    from_chips = _chip_exchange(pair)
    sc_idx = jnp.concatenate([s_idx, c_idx])
    halves = [_chip_sum(sc_idx, p, r, "chip_sum_" + name) for name, p, r in zip(BIG_NAMES, pair, from_chips)]
    shared = _pair_share(halves)

    out_g, out_d, out_m, out_v = {}, {}, {}, {}
    for name, both in zip(BIG_NAMES, shared):
        g = both.reshape(w[name].shape[1:])
        d, mm, vv = _adamw(w[name][0], g, m[name][0], v[name][0], "adamw_" + name)
        out_g[name], out_d[name], out_m[name], out_v[name] = g[None], d[None], mm[None], vv[None]

    vec, cw = _pack_small(gs, loss_part)
    vec_all, cw_all = _gather_small([vec, cw])

    def view(a, name):
        if name == "conv_w":
            return a[0]
        if name == "g_final":
            return a.reshape(1, D)
        return a

    wmv = [(view(w[name], name), view(m[name], name), view(v[name], name)) for name in SMALL_NAMES]
    small_out, loss_row = _small_update(s_idx, vec_all, cw_all, wmv)
    for name, (g, d, mm, vv) in zip(SMALL_NAMES, small_out):
        shape = w[name].shape
        out_g[name], out_d[name], out_m[name], out_v[name] = (
            g.reshape(shape), d.reshape(shape), mm.reshape(shape), vv.reshape(shape))

    loss = loss_row[0, 0]
    return (loss, grad_x[None], *[out_g[k] for k in WEIGHT_ORDER], *[out_d[k] for k in WEIGHT_ORDER],
            *[out_m[k] for k in WEIGHT_ORDER], *[out_v[k] for k in WEIGHT_ORDER])
```

```python
import functools

import jax
import jax.numpy as jnp
from jax import lax
from jax.experimental import pallas as pl
from jax.experimental.pallas import tpu as pltpu

F32 = jnp.float32
BF16 = jnp.bfloat16

T = 2048
D = 1024
HD = 64
NQ = 8
NKV = 2
GROUP = NQ // NKV
BLK = 128
AW = NQ * HD
KVW = NKV * HD
C = 512
KW = 31
QKVW = AW + 2 * KVW
GLU_OFF = QKVW
GATE_OFF = GLU_OFF + 2 * C
INW = GATE_OFF + 2 * D
DFF = 2816
EPS = 1e-5
NEG = -1e30
SCALE = HD ** -0.5
HALO = 32

ADAM_LR = 0.001
ADAM_B1 = 0.9
ADAM_B2 = 0.999
ADAM_EPS = 1e-08
ADAM_WD = 0.01
ADAM_STEP = 10

VMEM_LIMIT = 56 * 1024 * 1024
MESH = pl.DeviceIdType.MESH


def _params(*sem):
    return pltpu.CompilerParams(dimension_semantics=sem, vmem_limit_bytes=VMEM_LIMIT)


def _dot(a, b):
    return jnp.dot(a, b, preferred_element_type=F32)


def _dot_nt(a, b):
    return lax.dot_general(a, b, (((1,), (1,)), ((), ())), preferred_element_type=F32)


def _dot_tn(a, b):
    return lax.dot_general(a, b, (((0,), (0,)), ((), ())), preferred_element_type=F32)


def _sigmoid(v):
    return 1.0 / (1.0 + jnp.exp(-v))


def _rows(tm, n):
    return pl.BlockSpec((tm, n), lambda i: (i, 0))


def _whole(shape):
    return pl.BlockSpec(shape, lambda i: tuple(0 for _ in shape))


def _in_proj(x, g_mix, w_in, b_in):
    tm = 256

    def body(x_ref, g_ref, w_ref, b_ref, h_ref, qkv_ref, glu_ref, gl_ref):
        xv = x_ref[...]
        r = lax.rsqrt(jnp.mean(xv * xv, axis=-1, keepdims=True) + EPS)
        h = (xv * r * g_ref[...]).astype(BF16)
        h_ref[...] = h
        qkv_ref[...] = (_dot(h, w_ref[:, 0:GLU_OFF]) + b_ref[:, 0:GLU_OFF]).astype(BF16)
        glu_ref[...] = (_dot(h, w_ref[:, GLU_OFF:GATE_OFF]) + b_ref[:, GLU_OFF:GATE_OFF]).astype(BF16)
        gl_ref[...] = (_dot(h, w_ref[:, GATE_OFF:INW]) + b_ref[:, GATE_OFF:INW]).astype(BF16)

    return pl.pallas_call(
        body, name="in_proj", grid=(T // tm,),
        in_specs=[_rows(tm, D), _whole((1, D)), _whole((D, INW)), _whole((1, INW))],
        out_specs=[_rows(tm, D), _rows(tm, QKVW), _rows(tm, 2 * C), _rows(tm, 2 * D)],
        out_shape=[jax.ShapeDtypeStruct((T, D), BF16), jax.ShapeDtypeStruct((T, QKVW), BF16),
                   jax.ShapeDtypeStruct((T, 2 * C), BF16), jax.ShapeDtypeStruct((T, 2 * D), BF16)],
        compiler_params=_params("parallel"),
    )(x, g_mix, w_in, b_in)


def _attn_masks(i):
    qi = lax.broadcasted_iota(jnp.int32, (BLK, BLK), 0)
    kj = lax.broadcasted_iota(jnp.int32, (BLK, BLK), 1)
    mask_cur = kj <= qi
    mask_prev = jnp.logical_and(kj > qi, i > 0)
    return mask_cur, mask_prev


def _attn_fwd(qkv, sinks):
    def body(sink_ref, qkv_ref, o_ref, lse_ref):
        i = pl.program_id(0)
        r0 = pl.multiple_of(i * BLK, BLK)
        rp = pl.multiple_of(jnp.maximum(i - 1, 0) * BLK, BLK)
        mask_cur, mask_prev = _attn_masks(i)
        for h in range(NQ):
            kv = h // GROUP
            q = qkv_ref[pl.ds(r0, BLK), h * HD:(h + 1) * HD]
            kc = qkv_ref[pl.ds(r0, BLK), AW + kv * HD:AW + (kv + 1) * HD]
            kp = qkv_ref[pl.ds(rp, BLK), AW + kv * HD:AW + (kv + 1) * HD]
            vc = qkv_ref[pl.ds(r0, BLK), AW + KVW + kv * HD:AW + KVW + (kv + 1) * HD]
            vp = qkv_ref[pl.ds(rp, BLK), AW + KVW + kv * HD:AW + KVW + (kv + 1) * HD]
            sc = jnp.where(mask_cur, _dot_nt(q, kc) * SCALE, NEG)
            sp = jnp.where(mask_prev, _dot_nt(q, kp) * SCALE, NEG)
            sink = sink_ref[0, h]
            m = jnp.maximum(jnp.maximum(jnp.max(sc, axis=-1, keepdims=True),
                                        jnp.max(sp, axis=-1, keepdims=True)), sink)
            pc = jnp.exp(sc - m)
            pp = jnp.exp(sp - m)
            den = (jnp.sum(pc, axis=-1, keepdims=True) + jnp.sum(pp, axis=-1, keepdims=True)
                   + jnp.exp(sink - m))
            inv = 1.0 / den
            o = _dot((pc * inv).astype(BF16), vc) + _dot((pp * inv).astype(BF16), vp)
            o_ref[:, h * HD:(h + 1) * HD] = o.astype(BF16)
            lse_ref[:, h:h + 1] = m + jnp.log(den)

    return pl.pallas_call(
        body, name="attn_fwd", grid=(T // BLK,),
        in_specs=[pl.BlockSpec(memory_space=pltpu.SMEM), _whole((T, QKVW))],
        out_specs=[_rows(BLK, AW), _rows(BLK, NQ)],
        out_shape=[jax.ShapeDtypeStruct((T, AW), BF16), jax.ShapeDtypeStruct((T, NQ), F32)],
        compiler_params=_params("parallel"),
    )(sinks, qkv)


CONV_TM = 256
CONV_SUB = 32


def _glu(ab):
    a = ab[:, 0:C].astype(F32)
    b = ab[:, C:2 * C].astype(F32)
    return a * _sigmoid(b)


def _conv_fwd(glu, conv_w, conv_b, ln_g, ln_b):
    tm = CONV_TM

    def body(cur_ref, prev_ref, w_ref, cb_ref, g_ref, b_ref, u_ref, c_ref, zp_ref):
        i = pl.program_id(0)
        zprev = _glu(prev_ref[tm - HALO:tm, :])
        zp_ref[0:HALO, :] = jnp.where(i > 0, zprev, 0.0)
        zp_ref[HALO:HALO + tm, :] = _glu(cur_ref[...])
        for s in range(tm // CONV_SUB):
            base = HALO + s * CONV_SUB - (KW - 1)
            acc = jnp.broadcast_to(cb_ref[...], (CONV_SUB, C))
            for j in range(KW):
                acc = acc + w_ref[j:j + 1, :] * zp_ref[base + j:base + j + CONV_SUB, :]
            rows = slice(s * CONV_SUB, (s + 1) * CONV_SUB)
            u_ref[rows, :] = acc
            mu = jnp.mean(acc, axis=-1, keepdims=True)
            xc = acc - mu
            var = jnp.mean(xc * xc, axis=-1, keepdims=True)
            y = xc * lax.rsqrt(var + EPS) * g_ref[...] + b_ref[...]
            c_ref[rows, :] = (y * _sigmoid(y)).astype(BF16)

    return pl.pallas_call(
        body, name="conv_fwd", grid=(T // tm,),
        in_specs=[_rows(tm, 2 * C),
                  pl.BlockSpec((tm, 2 * C), lambda i: (jnp.maximum(i - 1, 0), 0)),
                  _whole((KW, C)), _whole((1, C)), _whole((1, C)), _whole((1, C))],
        out_specs=[_rows(tm, C), _rows(tm, C)],
        out_shape=[jax.ShapeDtypeStruct((T, C), F32), jax.ShapeDtypeStruct((T, C), BF16)],
        scratch_shapes=[pltpu.VMEM((HALO + tm, C), F32)],
        compiler_params=_params("parallel"),
    )(glu, glu, conv_w, conv_b, ln_g, ln_b)


def _mix_out(x, o, cact, gl, w_ap, w_cp, b_cp, w_out):
    tm = 256

    def body(x_ref, o_ref, c_ref, gl_ref, wap_ref, wcp_ref, bcp_ref, wo_ref,
             ya_ref, yc_ref, mg_ref, x1_ref):
        ya = _dot(o_ref[...], wap_ref[...])
        yc = _dot(c_ref[...], wcp_ref[...]) + bcp_ref[...]
        g0 = _sigmoid(gl_ref[:, 0:D].astype(F32))
        g1 = _sigmoid(gl_ref[:, D:2 * D].astype(F32))
        mg = (g0 * ya + g1 * yc).astype(BF16)
        ya_ref[...] = ya.astype(BF16)
        yc_ref[...] = yc.astype(BF16)
        mg_ref[...] = mg
        x1_ref[...] = x_ref[...] + _dot(mg, wo_ref[...])

    return pl.pallas_call(
        body, name="mix_out", grid=(T // tm,),
        in_specs=[_rows(tm, D), _rows(tm, AW), _rows(tm, C), _rows(tm, 2 * D),
                  _whole((AW, D)), _whole((C, D)), _whole((1, D)), _whole((D, D))],
        out_specs=[_rows(tm, D), _rows(tm, D), _rows(tm, D), _rows(tm, D)],
        out_shape=[jax.ShapeDtypeStruct((T, D), BF16), jax.ShapeDtypeStruct((T, D), BF16),
                   jax.ShapeDtypeStruct((T, D), BF16), jax.ShapeDtypeStruct((T, D), F32)],
        compiler_params=_params("parallel"),
    )(x, o, cact, gl, w_ap, w_cp, b_cp, w_out)


FFN_TN = 256


def _ffn_in(x1, g_ffn, w_fi):
    tm = 256

    def body(x_ref, g_ref, w_ref, h_ref, gu_ref, act_ref):
        xv = x_ref[...]
        r = lax.rsqrt(jnp.mean(xv * xv, axis=-1, keepdims=True) + EPS)
        h = (xv * r * g_ref[...]).astype(BF16)
        h_ref[...] = h
        for n in range(DFF // FFN_TN):
            c0 = n * FFN_TN
            gate = _dot(h, w_ref[:, c0:c0 + FFN_TN])
            up = _dot(h, w_ref[:, DFF + c0:DFF + c0 + FFN_TN])
            gu_ref[:, c0:c0 + FFN_TN] = gate.astype(BF16)
            gu_ref[:, DFF + c0:DFF + c0 + FFN_TN] = up.astype(BF16)
            act_ref[:, c0:c0 + FFN_TN] = (gate * _sigmoid(gate) * up).astype(BF16)

    return pl.pallas_call(
        body, name="ffn_in", grid=(T // tm,),
        in_specs=[_rows(tm, D), _whole((1, D)), _whole((D, 2 * DFF))],
        out_specs=[_rows(tm, D), _rows(tm, 2 * DFF), _rows(tm, DFF)],
        out_shape=[jax.ShapeDtypeStruct((T, D), BF16), jax.ShapeDtypeStruct((T, 2 * DFF), BF16),
                   jax.ShapeDtypeStruct((T, DFF), BF16)],
        compiler_params=_params("parallel"),
    )(x1, g_ffn, w_fi)


def _ffn_out_loss(x1, act, w_dn, g_final, target):
    tm = 256

    def body(x_ref, a_ref, w_ref, g_ref, t_ref, dx_ref, dxb_ref, dg_ref, loss_ref):
        i = pl.program_id(0)
        x2 = x_ref[...] + _dot(a_ref[...], w_ref[...])
        r = lax.rsqrt(jnp.mean(x2 * x2, axis=-1, keepdims=True) + EPS)
        xh = x2 * r
        g = g_ref[...]
        err = xh * g - t_ref[...]
        dy = err * (1.0 / D)
        dyg = dy * g
        dx = r * (dyg - xh * jnp.mean(dyg * xh, axis=-1, keepdims=True))
        dx_ref[...] = dx
        dxb_ref[...] = dx.astype(BF16)
        part = 0.5 * jnp.sum(jnp.mean(err * err, axis=-1, keepdims=True), axis=0, keepdims=True)

        @pl.when(i == 0)
        def _():
            dg_ref[...] = jnp.zeros_like(dg_ref)
            loss_ref[...] = jnp.zeros_like(loss_ref)

        dg_ref[...] += jnp.sum(dy * xh, axis=0, keepdims=True)
        loss_ref[...] += jnp.broadcast_to(part, loss_ref.shape)

    return pl.pallas_call(
        body, name="ffn_out_loss", grid=(T // tm,),
        in_specs=[_rows(tm, D), _rows(tm, DFF), _whole((DFF, D)), _whole((1, D)), _rows(tm, D)],
        out_specs=[_rows(tm, D), _rows(tm, D), _whole((1, D)), _whole((1, 128))],
        out_shape=[jax.ShapeDtypeStruct((T, D), F32), jax.ShapeDtypeStruct((T, D), BF16),
                   jax.ShapeDtypeStruct((1, D), F32), jax.ShapeDtypeStruct((1, 128), F32)],
        compiler_params=_params("arbitrary"),
    )(x1, act, w_dn, g_final, target)


def _const(shape):
    return pl.BlockSpec(shape, lambda i: tuple(0 for _ in shape), pipeline_mode=pl.Buffered(1))


def _ffn_bwd(dx2, dx2b, gu, x1, g_ffn, w_dn_t, w_fi_t):
    tm = 256

    def body(dx_ref, dxb_ref, gu_ref, x_ref, g_ref, wdn_ref, wfi_ref,
             dgu_ref, dx1_ref, dx1b_ref, dg_ref):
        i = pl.program_id(0)
        dxb = dxb_ref[...]
        dh = jnp.zeros((tm, D), F32)
        for n in range(DFF // FFN_TN):
            c0 = n * FFN_TN
            dact = _dot(dxb, wdn_ref[:, c0:c0 + FFN_TN])
            gate = gu_ref[:, c0:c0 + FFN_TN].astype(F32)
            up = gu_ref[:, DFF + c0:DFF + c0 + FFN_TN].astype(F32)
            s = _sigmoid(gate)
            dup = (dact * gate * s).astype(BF16)
            dgate = (dact * up * s * (1.0 + gate * (1.0 - s))).astype(BF16)
            dgu_ref[:, c0:c0 + FFN_TN] = dgate
            dgu_ref[:, DFF + c0:DFF + c0 + FFN_TN] = dup
            dh = dh + _dot(dgate, wfi_ref[c0:c0 + FFN_TN, :]) + _dot(dup, wfi_ref[DFF + c0:DFF + c0 + FFN_TN, :])
        xv = x_ref[...]
        r = lax.rsqrt(jnp.mean(xv * xv, axis=-1, keepdims=True) + EPS)
        xh = xv * r
        dhg = dh * g_ref[...]
        dx1 = dx_ref[...] + r * (dhg - xh * jnp.mean(dhg * xh, axis=-1, keepdims=True))
        dx1_ref[...] = dx1
        dx1b_ref[...] = dx1.astype(BF16)

        @pl.when(i == 0)
        def _():
            dg_ref[...] = jnp.zeros_like(dg_ref)

        dg_ref[...] += jnp.sum(dh * xh, axis=0, keepdims=True)

    return pl.pallas_call(
        body, name="ffn_bwd", grid=(T // tm,),
        in_specs=[_rows(tm, D), _rows(tm, D), _rows(tm, 2 * DFF), _rows(tm, D), _whole((1, D)),
                  _const((D, DFF)), _const((2 * DFF, D))],
        out_specs=[_rows(tm, 2 * DFF), _rows(tm, D), _rows(tm, D), _whole((1, D))],
        out_shape=[jax.ShapeDtypeStruct((T, 2 * DFF), BF16), jax.ShapeDtypeStruct((T, D), F32),
                   jax.ShapeDtypeStruct((T, D), BF16), jax.ShapeDtypeStruct((1, D), F32)],
        compiler_params=_params("arbitrary"),
    )(dx2, dx2b, gu, x1, g_ffn, w_dn_t, w_fi_t)


def _mix_bwd(dx1b, gl, ya, yc, w_out_t, w_ap_t, w_cp_t, dep):
    tm = 256

    def body(dx_ref, gl_ref, ya_ref, yc_ref, wo_ref, wap_ref, wcp_ref, dep_ref,
             dya_ref, dyc_ref, dgl_ref, do_ref, dc_ref, db_ref):
        i = pl.program_id(0)
        dm = _dot(dx_ref[...], wo_ref[...])
        g0 = _sigmoid(gl_ref[:, 0:D].astype(F32))
        g1 = _sigmoid(gl_ref[:, D:2 * D].astype(F32))
        dya = dm * g0
        dyc = dm * g1
        dgl_ref[:, 0:D] = (dya * ya_ref[...].astype(F32) * (1.0 - g0)).astype(BF16)
        dgl_ref[:, D:2 * D] = (dyc * yc_ref[...].astype(F32) * (1.0 - g1)).astype(BF16)
        dyab = dya.astype(BF16)
        dycb = dyc.astype(BF16)
        dya_ref[...] = dyab
        dyc_ref[...] = dycb
        do_ref[...] = _dot(dyab, wap_ref[...]).astype(BF16)
        dc_ref[...] = _dot(dycb, wcp_ref[...])

        @pl.when(i == 0)
        def _():
            db_ref[...] = jnp.zeros_like(db_ref)

        db_ref[...] += jnp.sum(dyc, axis=0, keepdims=True)

    return pl.pallas_call(
        body, name="mix_bwd", grid=(T // tm,),
        in_specs=[_rows(tm, D), _rows(tm, 2 * D), _rows(tm, D), _rows(tm, D),
                  _whole((D, D)), _whole((D, AW)), _whole((D, C)), _whole((8, 128))],
        out_specs=[_rows(tm, D), _rows(tm, D), _rows(tm, 2 * D), _rows(tm, AW), _rows(tm, C),
                   _whole((1, D))],
        out_shape=[jax.ShapeDtypeStruct((T, D), BF16), jax.ShapeDtypeStruct((T, D), BF16),
                   jax.ShapeDtypeStruct((T, 2 * D), BF16), jax.ShapeDtypeStruct((T, AW), BF16),
                   jax.ShapeDtypeStruct((T, C), F32), jax.ShapeDtypeStruct((1, D), F32)],
        compiler_params=_params("arbitrary"),
    )(dx1b, gl, ya, yc, w_out_t, w_ap_t, w_cp_t, dep)


def _conv_bwd(glu, u, dc, conv_w, ln_g, ln_b, dep):
    tm = CONV_TM
    nblk = T // tm

    def du_of(uv, dcv, g_ref, b_ref):
        mu = jnp.mean(uv, axis=-1, keepdims=True)
        xc = uv - mu
        var = jnp.mean(xc * xc, axis=-1, keepdims=True)
        rstd = lax.rsqrt(var + EPS)
        xh = xc * rstd
        y = xh * g_ref[...] + b_ref[...]
        sg = _sigmoid(y)
        dy = dcv * (sg * (1.0 + y * (1.0 - sg)))
        dxh = dy * g_ref[...]
        du = rstd * (dxh - jnp.mean(dxh, axis=-1, keepdims=True)
                     - xh * jnp.mean(dxh * xh, axis=-1, keepdims=True))
        return du, dy, xh

    def body(cur_ref, prev_ref, u_ref, un_ref, dc_ref, dcn_ref, w_ref, g_ref, b_ref, dep_ref,
             dglu_ref, dw_ref, dcb_ref, dg_ref, db_ref, zp_ref, du_ref):
        i = pl.program_id(0)

        @pl.when(i == 0)
        def _():
            dw_ref[...] = jnp.zeros_like(dw_ref)
            dcb_ref[...] = jnp.zeros_like(dcb_ref)
            dg_ref[...] = jnp.zeros_like(dg_ref)
            db_ref[...] = jnp.zeros_like(db_ref)

        zprev = _glu(prev_ref[tm - HALO:tm, :])
        zp_ref[0:HALO, :] = jnp.where(i > 0, zprev, 0.0)
        zp_ref[HALO:HALO + tm, :] = _glu(cur_ref[...])

        dun, _, _ = du_of(un_ref[0:HALO, :], dcn_ref[0:HALO, :], g_ref, b_ref)
        du_ref[tm:tm + HALO, :] = jnp.where(i < nblk - 1, dun, 0.0)
        dg_acc = jnp.zeros((1, C), F32)
        db_acc = jnp.zeros((1, C), F32)
        dcb_acc = jnp.zeros((1, C), F32)
        for s in range(tm // CONV_SUB):
            rows = slice(s * CONV_SUB, (s + 1) * CONV_SUB)
            du, dy, xh = du_of(u_ref[rows, :], dc_ref[rows, :], g_ref, b_ref)
            du_ref[rows, :] = du
            dg_acc = dg_acc + jnp.sum(dy * xh, axis=0, keepdims=True)
            db_acc = db_acc + jnp.sum(dy, axis=0, keepdims=True)
            dcb_acc = dcb_acc + jnp.sum(du, axis=0, keepdims=True)
        dg_ref[...] += dg_acc
        db_ref[...] += db_acc
        dcb_ref[...] += dcb_acc

        for j in range(KW):
            acc = jnp.zeros((CONV_SUB, C), F32)
            for s in range(tm // CONV_SUB):
                base = HALO + s * CONV_SUB - (KW - 1) + j
                acc = acc + du_ref[s * CONV_SUB:(s + 1) * CONV_SUB, :] * zp_ref[base:base + CONV_SUB, :]
            dw_ref[j:j + 1, :] += jnp.sum(acc, axis=0, keepdims=True)

        for s in range(tm // CONV_SUB):
            rows = slice(s * CONV_SUB, (s + 1) * CONV_SUB)
            dz = jnp.zeros((CONV_SUB, C), F32)
            for j in range(KW):
                o = s * CONV_SUB + (KW - 1) - j
                dz = dz + w_ref[j:j + 1, :] * du_ref[o:o + CONV_SUB, :]
            a = cur_ref[rows, 0:C].astype(F32)
            sb = _sigmoid(cur_ref[rows, C:2 * C].astype(F32))
            dglu_ref[rows, 0:C] = (dz * sb).astype(BF16)
            dglu_ref[rows, C:2 * C] = (dz * a * sb * (1.0 - sb)).astype(BF16)

    nxt = lambda i: (jnp.minimum(i + 1, nblk - 1), 0)
    return pl.pallas_call(
        body, name="conv_bwd", grid=(nblk,),
        in_specs=[_rows(tm, 2 * C),
                  pl.BlockSpec((tm, 2 * C), lambda i: (jnp.maximum(i - 1, 0), 0)),
                  _rows(tm, C), pl.BlockSpec((tm, C), nxt),
                  _rows(tm, C), pl.BlockSpec((tm, C), nxt),
                  _whole((KW, C)), _whole((1, C)), _whole((1, C)), _whole((8, 128))],
        out_specs=[_rows(tm, 2 * C), _whole((KW, C)), _whole((1, C)), _whole((1, C)), _whole((1, C))],
        out_shape=[jax.ShapeDtypeStruct((T, 2 * C), BF16), jax.ShapeDtypeStruct((KW, C), F32),
                   jax.ShapeDtypeStruct((1, C), F32), jax.ShapeDtypeStruct((1, C), F32),
                   jax.ShapeDtypeStruct((1, C), F32)],
        scratch_shapes=[pltpu.VMEM((HALO + tm, C), F32), pltpu.VMEM((tm + HALO, C), F32)],
        compiler_params=_params("arbitrary"),
    )(glu, glu, u, u, dc, dc, conv_w, ln_g, ln_b, dep)


def _attn_bwd(qkv, o, do, lse, sinks):
    def body(sink_ref, qkv_ref, o_ref, do_ref, lse_ref, dq_ref, dkv_ref, ds_ref):
        i = pl.program_id(0)

        @pl.when(i == 0)
        def _():
            dkv_ref[...] = jnp.zeros_like(dkv_ref)
            ds_ref[...] = jnp.zeros_like(ds_ref)

        r0 = pl.multiple_of(i * BLK, BLK)
        rp = pl.multiple_of(jnp.maximum(i - 1, 0) * BLK, BLK)
        mask_cur, mask_prev = _attn_masks(i)
        for h in range(NQ):
            kv = h // GROUP
            kcol = slice(AW + kv * HD, AW + (kv + 1) * HD)
            vcol = slice(AW + KVW + kv * HD, AW + KVW + (kv + 1) * HD)
            hcol = slice(h * HD, (h + 1) * HD)
            q = qkv_ref[pl.ds(r0, BLK), hcol]
            kc = qkv_ref[pl.ds(r0, BLK), kcol]
            kp = qkv_ref[pl.ds(rp, BLK), kcol]
            vc = qkv_ref[pl.ds(r0, BLK), vcol]
            vp = qkv_ref[pl.ds(rp, BLK), vcol]
            doh = do_ref[:, hcol]
            lse_h = lse_ref[:, h:h + 1]
            dl = jnp.sum(doh.astype(F32) * o_ref[:, hcol].astype(F32), axis=-1, keepdims=True)
            pc = jnp.where(mask_cur, jnp.exp(_dot_nt(q, kc) * SCALE - lse_h), 0.0)
            pp = jnp.where(mask_prev, jnp.exp(_dot_nt(q, kp) * SCALE - lse_h), 0.0)
            dsc = (pc * (_dot_nt(doh, vc) - dl)).astype(BF16)
            dsp = (pp * (_dot_nt(doh, vp) - dl)).astype(BF16)
            dq_ref[:, hcol] = ((_dot(dsc, kc) + _dot(dsp, kp)) * SCALE).astype(BF16)
            dkc = slice(kv * HD, (kv + 1) * HD)
            dvc = slice(KVW + kv * HD, KVW + (kv + 1) * HD)
            dkv_ref[pl.ds(r0, BLK), dkc] += _dot_tn(dsc, q) * SCALE
            dkv_ref[pl.ds(rp, BLK), dkc] += _dot_tn(dsp, q) * SCALE
            dkv_ref[pl.ds(r0, BLK), dvc] += _dot_tn(pc.astype(BF16), doh)
            dkv_ref[pl.ds(rp, BLK), dvc] += _dot_tn(pp.astype(BF16), doh)
            psink = jnp.exp(sink_ref[0, h] - lse_h)
            dsink = -jnp.sum(psink * dl, axis=0, keepdims=True)
            ds_ref[h:h + 1, :] += jnp.broadcast_to(dsink, (1, 128))

    return pl.pallas_call(
        body, name="attn_bwd", grid=(T // BLK,),
        in_specs=[pl.BlockSpec(memory_space=pltpu.SMEM), _whole((T, QKVW)),
                  _rows(BLK, AW), _rows(BLK, AW), _rows(BLK, NQ)],
        out_specs=[_rows(BLK, AW), _whole((T, 2 * KVW)), _whole((NQ, 128))],
        out_shape=[jax.ShapeDtypeStruct((T, AW), BF16), jax.ShapeDtypeStruct((T, 2 * KVW), F32),
                   jax.ShapeDtypeStruct((NQ, 128), F32)],
        compiler_params=_params("arbitrary"),
    )(sinks, qkv, o, do, lse)


def _in_proj_bwd(dproj, x, dx1, g_mix, w_in_t):
    tm = 256

    def body(dp_ref, x_ref, dx1_ref, g_ref, w_ref, gx_ref, dg_ref, db_ref):
        i = pl.program_id(0)
        dp = dp_ref[...]
        dh = _dot(dp, w_ref[...])
        xv = x_ref[...]
        r = lax.rsqrt(jnp.mean(xv * xv, axis=-1, keepdims=True) + EPS)
        xh = xv * r
        dhg = dh * g_ref[...]
        gx_ref[...] = dx1_ref[...] + r * (dhg - xh * jnp.mean(dhg * xh, axis=-1, keepdims=True))

        @pl.when(i == 0)
        def _():
            dg_ref[...] = jnp.zeros_like(dg_ref)
            db_ref[...] = jnp.zeros_like(db_ref)

        dg_ref[...] += jnp.sum(dh * xh, axis=0, keepdims=True)
        db_ref[...] += jnp.sum(dp.astype(F32), axis=0, keepdims=True)

    return pl.pallas_call(
        body, name="in_proj_bwd", grid=(T // tm,),
        in_specs=[_rows(tm, INW), _rows(tm, D), _rows(tm, D), _whole((1, D)), _const((INW, D))],
        out_specs=[_rows(tm, D), _whole((1, D)), _whole((1, INW))],
        out_shape=[jax.ShapeDtypeStruct((T, D), F32), jax.ShapeDtypeStruct((1, D), F32),
                   jax.ShapeDtypeStruct((1, INW), F32)],
        compiler_params=_params("arbitrary"),
    )(dproj, x, dx1, g_mix, w_in_t)


def _grad_w(a, b, name, tk, tn):
    k, n = a.shape[1], b.shape[1]

    def body(a_ref, b_ref, o_ref):
        o_ref[...] = _dot_tn(a_ref[...], b_ref[...]).astype(BF16)

    return pl.pallas_call(
        body, name=name, grid=(k // tk, n // tn),
        in_specs=[pl.BlockSpec((T, tk), lambda i, j: (0, i)), pl.BlockSpec((T, tn), lambda i, j: (0, j))],
        out_specs=pl.BlockSpec((tk, tn), lambda i, j: (i, j)),
        out_shape=jax.ShapeDtypeStruct((k, n), BF16),
        compiler_params=_params("parallel", "parallel"),
    )(a, b)


HBM_SPEC = pl.BlockSpec(memory_space=pltpu.HBM)
N_CHIPS = 4


def _place():
    x, y, c = lax.axis_index("x"), lax.axis_index("y"), lax.axis_index("c")
    chips = [(1 - x, y), (x, 1 - y), (1 - x, 1 - y)]
    return x, y, c, chips


SEM_SPEC = pl.BlockSpec(memory_space=pltpu.SEMAPHORE)
ANY_SPEC = pl.BlockSpec(memory_space=pl.ANY)
VMEM_SPEC = pl.BlockSpec(memory_space=pltpu.VMEM)
EFFECT = pltpu.SideEffectType.DATAFLOW_SIDE_EFFECTING


def _gather_ends(src, land, j, chip, x, y, c):
    kh = src.shape[0] // 2
    return src.at[pl.ds(c * kh, kh)], land.at[2 * x + y, pl.ds(c * kh, kh)]


def _reduce_ends(src, land, j, chip, x, y, c):
    return src.at[2 * chip[0] + chip[1]], land.at[j]


def _chip_copies(ends, srcs, lands, send_sems, recv_sems):
    x, y, c, chips = _place()
    copies = []
    for w, (src, land) in enumerate(zip(srcs, lands)):
        for j, chip in enumerate(chips):
            s, d = ends(src, land, j, chip, x, y, c)
            copies.append(pltpu.make_async_remote_copy(
                src_ref=s, dst_ref=d, send_sem=send_sems.at[3 * w + j],
                recv_sem=recv_sems.at[3 * w + j], device_id=(*chip, c), device_id_type=MESH))
    return copies


def _chip_start(name, ends, srcs, lands):
    n = len(srcs)

    def body(*refs):
        copies = _chip_copies(ends, refs[:n], refs[n:2 * n], refs[2 * n], refs[2 * n + 1])
        for cp in copies:
            cp.start()
        token = refs[-1]
        token[...] = jnp.zeros_like(token)

    hbm = lambda a: pltpu.HBM(a.shape, a.dtype)
    res = pl.pallas_call(
        body, name=name,
        out_shape=(pltpu.SemaphoreType.DMA((3 * n,)), pltpu.SemaphoreType.DMA((3 * n,)),
                   *[hbm(a) for a in srcs], *[hbm(a) for a in lands],
                   jax.ShapeDtypeStruct((8, 128), F32)),
        in_specs=[HBM_SPEC] * (2 * n),
        out_specs=(SEM_SPEC, SEM_SPEC, *[HBM_SPEC] * (2 * n), VMEM_SPEC),
        input_output_aliases={i: 2 + i for i in range(2 * n)},
        compiler_params=pltpu.CompilerParams(has_side_effects=EFFECT),
    )(*[pltpu.with_memory_space_constraint(a, pltpu.HBM) for a in (*srcs, *lands)])
    return res[0], res[1], list(res[2:2 + n]), list(res[2 + n:2 + 2 * n]), res[-1]


def _chip_wait(name, ends, send_sems, recv_sems, srcs, lands, after):
    n, na = len(srcs), len(after)

    def body(*refs):
        copies = _chip_copies(ends, refs[:n], refs[n:2 * n], refs[2 * n], refs[2 * n + 1])
        for cp in copies:
            cp.wait_send()
            cp.wait_recv()

    hbm = lambda a: pltpu.HBM(a.shape, a.dtype)
    res = pl.pallas_call(
        body, name=name,
        out_shape=tuple(hbm(a) for a in (*srcs, *lands)),
        in_specs=[HBM_SPEC] * (2 * n) + [SEM_SPEC, SEM_SPEC] + [ANY_SPEC] * na,
        out_specs=tuple([HBM_SPEC] * (2 * n)),
        input_output_aliases={i: i for i in range(2 * n)},
        compiler_params=pltpu.CompilerParams(has_side_effects=EFFECT),
    )(*srcs, *lands, send_sems, recv_sems, *after)
    return list(res[:n]), list(res[n:])


def _pair_forward(name, lands):
    n = len(lands)

    def body(*refs):
        bufs = refs[n:2 * n]
        send_sems, recv_sems = refs[2 * n:]
        x, y, c, chips = _place()

        def copy(w, j, cc, to):
            kh = bufs[w].shape[1] // 2
            blk = bufs[w].at[2 * chips[j][0] + chips[j][1], pl.ds(cc * kh, kh)]
            return pltpu.make_async_remote_copy(
                src_ref=blk, dst_ref=blk, send_sem=send_sems.at[3 * w + j],
                recv_sem=recv_sems.at[3 * w + j], device_id=to, device_id_type=MESH)

        sends = [copy(w, j, c, (x, y, 1 - c)) for w in range(n) for j in range(3)]
        for cp in sends:
            cp.start()
        for w in range(n):
            for j in range(3):
                copy(w, j, 1 - c, (x, y, c)).wait_recv()
        for cp in sends:
            cp.wait_send()

    return pl.pallas_call(
        body, name=name,
        in_specs=[HBM_SPEC] * n, out_specs=[HBM_SPEC] * n,
        out_shape=[jax.ShapeDtypeStruct(a.shape, a.dtype) for a in lands],
        input_output_aliases={w: w for w in range(n)},
        scratch_shapes=[pltpu.SemaphoreType.DMA((3 * n,)), pltpu.SemaphoreType.DMA((3 * n,))],
    )(*lands)


def _pair_exchange(name, grads):
    n = len(grads)

    def body(*refs):
        ins, outs = refs[:n], refs[n:2 * n]
        send_sems, recv_sems = refs[2 * n:]
        x, y, c, _ = _place()
        copies = []
        for w in range(n):
            kh = ins[w].shape[1] // 2
            cp = pltpu.make_async_remote_copy(
                src_ref=ins[w].at[:, pl.ds((1 - c) * kh, kh)], dst_ref=outs[w],
                send_sem=send_sems.at[w], recv_sem=recv_sems.at[w],
                device_id=(x, y, 1 - c), device_id_type=MESH)
            cp.start()
            copies.append(cp)
        for cp in copies:
            cp.wait()

    return pl.pallas_call(
        body, name=name,
        in_specs=[HBM_SPEC] * n, out_specs=[HBM_SPEC] * n,
        out_shape=[jax.ShapeDtypeStruct((g.shape[0], g.shape[1] // 2, g.shape[2]), g.dtype) for g in grads],
        scratch_shapes=[pltpu.SemaphoreType.DMA((n,)), pltpu.SemaphoreType.DMA((n,))],
    )(*grads)


def _row_tile(k):
    for t in (256, 128, 176, 64, 32, 16):
        if k % t == 0:
            return t
    raise ValueError(k)


def _pair_sum(c_idx, g, got, name):
    _, k, n = g.shape
    kh = k // 2
    tm = _row_tile(kh)
    nb = kh // tm

    def body(c_ref, g_ref, r_ref, o_ref):
        o_ref[...] = (g_ref[...].astype(F32) + r_ref[...].astype(F32)).astype(BF16)

    return pl.pallas_call(
        body, name=name,
        grid_spec=pltpu.PrefetchScalarGridSpec(
            num_scalar_prefetch=1, grid=(N_CHIPS, nb),
            in_specs=[pl.BlockSpec((1, tm, n), lambda s, i, c_ref: (s, c_ref[0] * nb + i, 0)),
                      pl.BlockSpec((1, tm, n), lambda s, i, c_ref: (s, i, 0))],
            out_specs=pl.BlockSpec((1, tm, n), lambda s, i, c_ref: (s, i, 0))),
        out_shape=jax.ShapeDtypeStruct((N_CHIPS, kh, n), BF16),
        compiler_params=_params("parallel", "parallel"),
    )(c_idx, g, got)


def _chip_sum(sc_idx, mine, got, name):
    _, kh, n = mine.shape
    tm = _row_tile(kh)

    def body(sc_ref, m_ref, r_ref, o_ref):
        acc = m_ref[0].astype(F32)
        for j in range(3):
            acc = acc + r_ref[j].astype(F32)
        o_ref[0] = acc

    return pl.pallas_call(
        body, name=name,
        grid_spec=pltpu.PrefetchScalarGridSpec(
            num_scalar_prefetch=1, grid=(kh // tm,),
            in_specs=[pl.BlockSpec((1, tm, n), lambda i, sc_ref: (sc_ref[0], i, 0)),
                      pl.BlockSpec((3, tm, n), lambda i, sc_ref: (0, i, 0))],
            out_specs=pl.BlockSpec((1, tm, n), lambda i, sc_ref: (sc_ref[1], i, 0))),
        out_shape=jax.ShapeDtypeStruct((2, kh, n), F32),
        compiler_params=_params("parallel"),
    )(sc_idx, mine, got)


def _pair_share(name, halves):
    n = len(halves)

    def body(*refs):
        bufs = refs[n:2 * n]
        send_sems, recv_sems = refs[2 * n:]
        x, y, c, _ = _place()
        copies = []
        for w in range(n):
            cp = pltpu.make_async_remote_copy(
                src_ref=bufs[w].at[c], dst_ref=bufs[w].at[c],
                send_sem=send_sems.at[w], recv_sem=recv_sems.at[w],
                device_id=(x, y, 1 - c), device_id_type=MESH)
            cp.start()
            copies.append(cp)
        for w in range(n):
            copies[w].wait_send()
            pltpu.make_async_remote_copy(
                src_ref=bufs[w].at[1 - c], dst_ref=bufs[w].at[1 - c],
                send_sem=send_sems.at[w], recv_sem=recv_sems.at[w],
                device_id=(x, y, c), device_id_type=MESH).wait_recv()

    return pl.pallas_call(
        body, name=name,
        in_specs=[HBM_SPEC] * n, out_specs=[HBM_SPEC] * n,
        out_shape=[jax.ShapeDtypeStruct(h.shape, h.dtype) for h in halves],
        input_output_aliases={w: w for w in range(n)},
        scratch_shapes=[pltpu.SemaphoreType.DMA((n,)), pltpu.SemaphoreType.DMA((n,))],
    )(*halves)


def _gather_small(blocks):
    n = len(blocks)

    def body(*refs):
        ins, outs = refs[:n], refs[n:2 * n]
        send_sems, recv_sems, local_sems = refs[2 * n:]
        x, y, c, chips = _place()
        me, sibling = (x, y, c), (x, y, 1 - c)

        def rows(w, px, py, pc):
            m = ins[w].shape[0]
            return outs[w].at[pl.ds((4 * px + 2 * py + pc) * m, m), :]

        def copy(w, k, block, to, src=None):
            return pltpu.make_async_remote_copy(
                src_ref=rows(w, *block) if src is None else src, dst_ref=rows(w, *block),
                send_sem=send_sems.at[7 * w + k], recv_sem=recv_sems.at[7 * w + k],
                device_id=to, device_id_type=MESH)

        started, owns = [], []
        for w in range(n):
            own = pltpu.make_async_copy(ins[w], rows(w, *me), local_sems.at[w])
            own.start()
            owns.append(own)
            first = [copy(w, 0, me, sibling, src=ins[w])]
            first += [copy(w, 1 + j, me, (*chip, c), src=ins[w]) for j, chip in enumerate(chips)]
            for cp in first:
                cp.start()
            started += first
        for w in range(n):
            for j, chip in enumerate(chips):
                copy(w, 1 + j, (*chip, c), me).wait_recv()
                cp = copy(w, 4 + j, (*chip, c), sibling)
                cp.start()
                started.append(cp)
        for w in range(n):
            copy(w, 0, sibling, me).wait_recv()
            for j, chip in enumerate(chips):
                copy(w, 4 + j, (*chip, 1 - c), me).wait_recv()
        for cp in started:
            cp.wait_send()
        for own in owns:
            own.wait()

    vmem = pl.BlockSpec(memory_space=pltpu.VMEM)
    return pl.pallas_call(
        body, name="gather_small",
        in_specs=[vmem] * n, out_specs=[vmem] * n,
        out_shape=[jax.ShapeDtypeStruct((8 * b.shape[0], b.shape[1]), b.dtype) for b in blocks],
        scratch_shapes=[pltpu.SemaphoreType.DMA((7 * n,)), pltpu.SemaphoreType.DMA((7 * n,)),
                        pltpu.SemaphoreType.DMA((n,))],
    )(*blocks)


def _adamw_math(w, g, m, v):
    m = ADAM_B1 * m + (1.0 - ADAM_B1) * g
    v = ADAM_B2 * v + (1.0 - ADAM_B2) * (g * g)
    m_hat = m / (1.0 - ADAM_B1 ** ADAM_STEP)
    v_hat = v / (1.0 - ADAM_B2 ** ADAM_STEP)
    delta = -ADAM_LR * (m_hat / (jnp.sqrt(v_hat) + ADAM_EPS) + ADAM_WD * w)
    return delta, m, v


def _adamw(w, g, m, v, name):
    k, n = w.shape
    tm = k // 4

    def body(w_ref, g_ref, m_ref, v_ref, d_ref, mo_ref, vo_ref):
        d, mm, vv = _adamw_math(w_ref[...], g_ref[...], m_ref[...], v_ref[...])
        d_ref[...] = d
        mo_ref[...] = mm
        vo_ref[...] = vv

    spec = pl.BlockSpec((tm, n), lambda i: (i, 0))
    shp = jax.ShapeDtypeStruct((k, n), F32)
    return pl.pallas_call(
        body, name=name, grid=(4,), in_specs=[spec] * 4, out_specs=[spec] * 3,
        out_shape=[shp] * 3, compiler_params=_params("parallel"),
    )(w, g, m, v)


VEC_SLOTS = {
    "g_mix_norm": (0, 0, D), "b_conv_proj": (0, D, D), "g_ffn_norm": (0, 2 * D, D),
    "g_final": (0, 3 * D, D), "b_in": (1, 0, INW), "conv_b": (2, 0, C), "ln_g": (2, C, C),
    "ln_b": (2, 2 * C, C), "sinks": (2, 3 * C, NQ), "loss": (2, 3 * C + 128, 1),
}
VEC_ROWS, VEC_COLS = 8, 4 * D
CW_ROWS = 32
SMALL_NAMES = ["g_mix_norm", "b_in", "sinks", "conv_w", "conv_b", "ln_g", "ln_b",
               "b_conv_proj", "g_ffn_norm", "g_final"]
CW_LANES = C // N_CHIPS


def _pack_small(gs, loss):
    row0 = jnp.concatenate([gs["g_mix_norm"], gs["b_conv_proj"], gs["g_ffn_norm"], gs["g_final"]], axis=1)
    row1 = jnp.pad(gs["b_in"], ((0, 0), (0, VEC_COLS - INW)))
    row2 = jnp.concatenate([gs["conv_b"], gs["ln_g"], gs["ln_b"],
                            jnp.pad(gs["sinks"], ((0, 0), (0, 128 - NQ))),
                            jnp.pad(loss.reshape(1, 1), ((0, 0), (0, VEC_COLS - 3 * C - 129)))], axis=1)
    vec = jnp.concatenate([row0, row1, row2, jnp.zeros((VEC_ROWS - 3, VEC_COLS), F32)], axis=0)
    cw = jnp.pad(gs["conv_w"], ((0, CW_ROWS - KW), (0, 0)))
    return vec, cw


def _small_update(s_idx, vec_all, cw_all, wmv):
    nsm = len(SMALL_NAMES)

    def body(s_ref, vec_ref, cw_ref, *refs):
        ins = refs[:3 * nsm]
        outs = refs[3 * nsm:7 * nsm]
        loss_ref = refs[7 * nsm]

        def total(slot):
            row, lane, width = slot
            acc = vec_ref[row:row + 1, lane:lane + width]
            for k in range(1, 8):
                acc = acc + vec_ref[k * VEC_ROWS + row:k * VEC_ROWS + row + 1, lane:lane + width]
            return acc

        loss_ref[...] = jnp.broadcast_to(total(VEC_SLOTS["loss"]), loss_ref.shape)
        for p, name in enumerate(SMALL_NAMES):
            w_ref, m_ref, v_ref = ins[3 * p:3 * p + 3]
            g_ref, d_ref, mo_ref, vo_ref = outs[4 * p:4 * p + 4]
            if name == "conv_w":
                g = jnp.zeros((KW, CW_LANES), F32)
                for s in range(N_CHIPS):
                    cand = cw_ref[0:KW, s * CW_LANES:(s + 1) * CW_LANES]
                    for k in range(1, 8):
                        cand = cand + cw_ref[k * CW_ROWS:k * CW_ROWS + KW, s * CW_LANES:(s + 1) * CW_LANES]
                    g = jnp.where(s_ref[0] == s, cand, g)
            else:
                g = total(VEC_SLOTS[name])
            d, mm, vv = _adamw_math(w_ref[...], g, m_ref[...], v_ref[...])
            g_ref[...] = g
            d_ref[...] = d
            mo_ref[...] = mm
            vo_ref[...] = vv

    vmem = pl.BlockSpec(memory_space=pltpu.VMEM)
    flat = [a for t in wmv for a in t]
    out_shape = []
    for w, _, _ in wmv:
        out_shape += [jax.ShapeDtypeStruct(w.shape, F32)] * 4
    out_shape.append(jax.ShapeDtypeStruct((1, 128), F32))
    res = pl.pallas_call(
        body, name="small_update",
        in_specs=[pl.BlockSpec(memory_space=pltpu.SMEM)] + [vmem] * (2 + len(flat)),
        out_specs=[vmem] * len(out_shape), out_shape=out_shape,
    )(s_idx, vec_all, cw_all, *flat)
    return [tuple(res[4 * p:4 * p + 4]) for p in range(nsm)], res[4 * nsm]


BIG_NAMES = ["w_in", "w_attn_proj", "w_conv_proj", "w_out", "w_ffn_in", "w_ffn_down"]
COL_SHARDED = {"w_in": True, "w_attn_proj": True, "w_conv_proj": True, "w_out": False,
               "w_ffn_in": True, "w_ffn_down": False}
WEIGHT_ORDER = ["g_mix_norm", "w_in", "b_in", "sinks", "conv_w", "conv_b", "ln_g", "ln_b",
                "w_attn_proj", "w_conv_proj", "b_conv_proj", "w_out", "g_ffn_norm", "w_ffn_in",
                "w_ffn_down", "g_final"]


def _assemble(name, gathered):
    s, k, n = gathered.shape
    if COL_SHARDED[name]:
        return gathered.transpose(1, 0, 2).reshape(k, s * n)
    return gathered.reshape(s * k, n)


def _split(name, full):
    k, n = full.shape
    if COL_SHARDED[name]:
        return full.reshape(k, N_CHIPS, n // N_CHIPS).transpose(1, 0, 2)
    return full.reshape(N_CHIPS, k // N_CHIPS, n)


def kernel(x, g_mix_norm, w_in, b_in, sinks, conv_w, conv_b, ln_g, ln_b, w_attn_proj, w_conv_proj, b_conv_proj, w_out, g_ffn_norm, w_ffn_in, w_ffn_down, g_final, loss_target, m_g_mix_norm, m_w_in, m_b_in, m_sinks, m_conv_w, m_conv_b, m_ln_g, m_ln_b, m_w_attn_proj, m_w_conv_proj, m_b_conv_proj, m_w_out, m_g_ffn_norm, m_w_ffn_in, m_w_ffn_down, m_g_final, v_g_mix_norm, v_w_in, v_b_in, v_sinks, v_conv_w, v_conv_b, v_ln_g, v_ln_b, v_w_attn_proj, v_w_conv_proj, v_b_conv_proj, v_w_out, v_g_ffn_norm, v_w_ffn_in, v_w_ffn_down, v_g_final):
    w = dict(g_mix_norm=g_mix_norm, w_in=w_in, b_in=b_in, sinks=sinks, conv_w=conv_w, conv_b=conv_b,
             ln_g=ln_g, ln_b=ln_b, w_attn_proj=w_attn_proj, w_conv_proj=w_conv_proj,
             b_conv_proj=b_conv_proj, w_out=w_out, g_ffn_norm=g_ffn_norm, w_ffn_in=w_ffn_in,
             w_ffn_down=w_ffn_down, g_final=g_final)
    m = dict(g_mix_norm=m_g_mix_norm, w_in=m_w_in, b_in=m_b_in, sinks=m_sinks, conv_w=m_conv_w,
             conv_b=m_conv_b, ln_g=m_ln_g, ln_b=m_ln_b, w_attn_proj=m_w_attn_proj,
             w_conv_proj=m_w_conv_proj, b_conv_proj=m_b_conv_proj, w_out=m_w_out,
             g_ffn_norm=m_g_ffn_norm, w_ffn_in=m_w_ffn_in, w_ffn_down=m_w_ffn_down, g_final=m_g_final)
    v = dict(g_mix_norm=v_g_mix_norm, w_in=v_w_in, b_in=v_b_in, sinks=v_sinks, conv_w=v_conv_w,
             conv_b=v_conv_b, ln_g=v_ln_g, ln_b=v_ln_b, w_attn_proj=v_w_attn_proj,
             w_conv_proj=v_w_conv_proj, b_conv_proj=v_b_conv_proj, w_out=v_w_out,
             g_ffn_norm=v_g_ffn_norm, w_ffn_in=v_w_ffn_in, w_ffn_down=v_w_ffn_down, g_final=v_g_final)

    c_idx = lax.axis_index("c").astype(jnp.int32).reshape(1)
    s_idx = (2 * lax.axis_index("x") + lax.axis_index("y")).astype(jnp.int32).reshape(1)

    sc_idx = jnp.concatenate([s_idx, c_idx])
    chip_iota = lax.broadcasted_iota(jnp.int32, (N_CHIPS, 1, 1), 0)
    out_g, out_d, out_m, out_v = {}, {}, {}, {}

    def gather_start(tag, shards):
        lands = [lax.empty((N_CHIPS,) + s.shape, s.dtype) for s in shards]
        return _chip_start("gather_start_" + tag, _gather_ends, shards, lands)

    def gather_finish(tag, state, after):
        send_sems, recv_sems, shards, lands, _ = state
        shards, lands = _chip_wait("gather_wait_" + tag, _gather_ends, send_sems, recv_sems, shards, lands, after)
        lands = _pair_forward("pair_forward_" + tag, lands)
        return [jnp.where(chip_iota == s_idx[0], own[None], got) for own, got in zip(shards, lands)]

    names_a, names_b = ["w_in"], ["w_attn_proj", "w_conv_proj", "w_out", "w_ffn_in", "w_ffn_down"]
    state_a = gather_start("a", [w["w_in"][0].astype(BF16), jnp.pad(conv_w[0], ((0, CW_ROWS - KW), (0, 0)))])
    state_b = gather_start("b", [w[name][0].astype(BF16) for name in names_b])
    got_a = gather_finish("a", state_a, [state_b[4]])
    full = {"w_in": _assemble("w_in", got_a[0])}
    conv_w_full = got_a[1].transpose(1, 0, 2).reshape(CW_ROWS, C)[:KW]

    xs, target = x[0], loss_target[0]
    g_final2 = g_final.reshape(1, D)
    h, qkv, glu, gl = _in_proj(xs, g_mix_norm, full["w_in"], b_in)
    o, lse = _attn_fwd(qkv, sinks)
    u, cact = _conv_fwd(glu, conv_w_full, conv_b, ln_g, ln_b)
    got_b = gather_finish("b", state_b, [o, cact])
    full.update({name: _assemble(name, g) for name, g in zip(names_b, got_b)})
    ya, yc, mg, x1 = _mix_out(xs, o, cact, gl, full["w_attn_proj"], full["w_conv_proj"], b_conv_proj,
                              full["w_out"])
    h2, gu, act = _ffn_in(x1, g_ffn_norm, full["w_ffn_in"])
    dx2, dx2b, dg_final, loss_part = _ffn_out_loss(x1, act, full["w_ffn_down"], g_final2, target)

    def reduce_start(tag, names, gws):
        grads = [_split(name, g) for name, g in zip(names, gws)]
        from_sibling = _pair_exchange("pair_exchange_" + tag, grads)
        pair = [_pair_sum(c_idx, g, r, "pair_sum_" + name) for name, g, r in zip(names, grads, from_sibling)]
        lands = [lax.empty((3,) + p.shape[1:], p.dtype) for p in pair]
        return _chip_start("chip_start_" + tag, _reduce_ends, pair, lands)

    def reduce_finish(tag, names, state, after):
        send_sems, recv_sems, pair, lands, _ = state
        pair, lands = _chip_wait("chip_wait_" + tag, _reduce_ends, send_sems, recv_sems, pair, lands, after)
        halves = [_chip_sum(sc_idx, p, r, "chip_sum_" + name) for name, p, r in zip(names, pair, lands)]
        for name, both in zip(names, _pair_share("pair_share_" + tag, halves)):
            g = both.reshape(w[name].shape[1:])
            d, mm, vv = _adamw(w[name][0], g, m[name][0], v[name][0], "adamw_" + name)
            out_g[name], out_d[name], out_m[name], out_v[name] = g[None], d[None], mm[None], vv[None]

    dgu, dx1, dx1b, dg_ffn = _ffn_bwd(dx2, dx2b, gu, x1, g_ffn_norm, full["w_ffn_down"].T, full["w_ffn_in"].T)
    names_1 = ["w_ffn_in", "w_ffn_down"]
    state_1 = reduce_start("1", names_1, [_grad_w(h2, dgu, "grad_w_ffn_in", 512, 512),
                                          _grad_w(act, dx2b, "grad_w_ffn_down", 256, 512)])
    dya, dyc, dgl, do, dc, db_cp = _mix_bwd(dx1b, gl, ya, yc, full["w_out"].T, full["w_attn_proj"].T,
                                            full["w_conv_proj"].T, state_1[4])
    names_2 = ["w_out", "w_attn_proj", "w_conv_proj"]
    state_2 = reduce_start("2", names_2, [_grad_w(mg, dx1b, "grad_w_out", 512, 512),
                                          _grad_w(o, dya, "grad_w_attn_proj", 512, 512),
                                          _grad_w(cact, dyc, "grad_w_conv_proj", 512, 512)])
    dglu, dconv_w, dconv_b, dln_g, dln_b = _conv_bwd(glu, u, dc, conv_w_full, ln_g, ln_b, state_2[4])
    dq, dkv, dsinks = _attn_bwd(qkv, o, do, lse, sinks)
    dproj = jnp.concatenate([dq, dkv.astype(BF16), dglu, dgl], axis=1)
    grad_x, dg_mix, db_in = _in_proj_bwd(dproj, xs, dx1, g_mix_norm, full["w_in"].T)
    names_3 = ["w_in"]
    state_3 = reduce_start("3", names_3, [_grad_w(h, dproj, "grad_w_in", 512, 768)])

    reduce_finish("1", names_1, state_1, [state_3[4]])
    reduce_finish("2", names_2, state_2, [out_d["w_ffn_down"]])

    gs = {"g_mix_norm": dg_mix, "b_in": db_in, "sinks": dsinks[:, 0].reshape(1, NQ),
          "conv_w": dconv_w, "conv_b": dconv_b, "ln_g": dln_g, "ln_b": dln_b,
          "b_conv_proj": db_cp, "g_ffn_norm": dg_ffn, "g_final": dg_final}
    vec, cw = _pack_small(gs, loss_part[0, 0])
    vec_all, cw_all = _gather_small([vec, cw])

    def view(a, name):
        if name == "conv_w":
            return a[0]
        if name == "g_final":
            return a.reshape(1, D)
        return a

    wmv = [(view(w[name], name), view(m[name], name), view(v[name], name)) for name in SMALL_NAMES]
    small_out, loss_row = _small_update(s_idx, vec_all, cw_all, wmv)
    for name, (g, d, mm, vv) in zip(SMALL_NAMES, small_out):
        shape = w[name].shape
        out_g[name], out_d[name], out_m[name], out_v[name] = (
            g.reshape(shape), d.reshape(shape), mm.reshape(shape), vv.reshape(shape))

    reduce_finish("3", names_3, state_3, [loss_row, out_d["w_conv_proj"]])

    loss = loss_row[0, 0]
    return (loss, grad_x[None], *[out_g[k] for k in WEIGHT_ORDER], *[out_d[k] for k in WEIGHT_ORDER],
            *[out_m[k] for k in WEIGHT_ORDER], *[out_v[k] for k in WEIGHT_ORDER])
```

```python
import functools

import jax
import jax.numpy as jnp
from jax import lax
from jax.experimental import pallas as pl
from jax.experimental.pallas import tpu as pltpu

F32 = jnp.float32
BF16 = jnp.bfloat16

T = 2048
D = 1024
HD = 64
NQ = 8
NKV = 2
GROUP = NQ // NKV
BLK = 128
AW = NQ * HD
KVW = NKV * HD
C = 512
KW = 31
QKVW = AW + 2 * KVW
GLU_OFF = QKVW
GATE_OFF = GLU_OFF + 2 * C
INW = GATE_OFF + 2 * D
DFF = 2816
EPS = 1e-5
NEG = -1e30
SCALE = HD ** -0.5
HALO = 32
N_CHIPS = 4
FSH = 2 * DFF // N_CHIPS

ADAM_LR = 0.001
ADAM_B1 = 0.9
ADAM_B2 = 0.999
ADAM_EPS = 1e-08
ADAM_WD = 0.01
ADAM_STEP = 10

VMEM_LIMIT = 56 * 1024 * 1024
MESH = pl.DeviceIdType.MESH


def _params(*sem):
    return pltpu.CompilerParams(dimension_semantics=sem, vmem_limit_bytes=VMEM_LIMIT)


def _dot(a, b):
    return jnp.dot(a, b, preferred_element_type=F32)


def _dot_nt(a, b):
    return lax.dot_general(a, b, (((1,), (1,)), ((), ())), preferred_element_type=F32)


def _dot_tn(a, b):
    return lax.dot_general(a, b, (((0,), (0,)), ((), ())), preferred_element_type=F32)


def _sigmoid(v):
    return 1.0 / (1.0 + jnp.exp(-v))


def _rows(tm, n):
    return pl.BlockSpec((tm, n), lambda i: (i, 0))


def _whole(shape):
    return pl.BlockSpec(shape, lambda i: tuple(0 for _ in shape))


def _in_proj(x, g_mix, w_in, b_in):
    tm = 256

    def body(x_ref, g_ref, w_ref, b_ref, h_ref, qkv_ref, glu_ref, gl_ref):
        xv = x_ref[...]
        r = lax.rsqrt(jnp.mean(xv * xv, axis=-1, keepdims=True) + EPS)
        h = (xv * r * g_ref[...]).astype(BF16)
        h_ref[...] = h
        qkv_ref[...] = (_dot(h, w_ref[:, 0:GLU_OFF]) + b_ref[:, 0:GLU_OFF]).astype(BF16)
        glu_ref[...] = (_dot(h, w_ref[:, GLU_OFF:GATE_OFF]) + b_ref[:, GLU_OFF:GATE_OFF]).astype(BF16)
        gl_ref[...] = (_dot(h, w_ref[:, GATE_OFF:INW]) + b_ref[:, GATE_OFF:INW]).astype(BF16)

    return pl.pallas_call(
        body, name="in_proj", grid=(T // tm,),
        in_specs=[_rows(tm, D), _whole((1, D)), _whole((D, INW)), _whole((1, INW))],
        out_specs=[_rows(tm, D), _rows(tm, QKVW), _rows(tm, 2 * C), _rows(tm, 2 * D)],
        out_shape=[jax.ShapeDtypeStruct((T, D), BF16), jax.ShapeDtypeStruct((T, QKVW), BF16),
                   jax.ShapeDtypeStruct((T, 2 * C), BF16), jax.ShapeDtypeStruct((T, 2 * D), BF16)],
        compiler_params=_params("parallel"),
    )(x, g_mix, w_in, b_in)


def _attn_masks(i):
    qi = lax.broadcasted_iota(jnp.int32, (BLK, BLK), 0)
    kj = lax.broadcasted_iota(jnp.int32, (BLK, BLK), 1)
    mask_cur = kj <= qi
    mask_prev = jnp.logical_and(kj > qi, i > 0)
    return mask_cur, mask_prev


def _attn_fwd(qkv, sinks):
    def body(sink_ref, qkv_ref, o_ref, lse_ref):
        i = pl.program_id(0)
        r0 = pl.multiple_of(i * BLK, BLK)
        rp = pl.multiple_of(jnp.maximum(i - 1, 0) * BLK, BLK)
        mask_cur, mask_prev = _attn_masks(i)
        for h in range(NQ):
            kv = h // GROUP
            q = qkv_ref[pl.ds(r0, BLK), h * HD:(h + 1) * HD]
            kc = qkv_ref[pl.ds(r0, BLK), AW + kv * HD:AW + (kv + 1) * HD]
            kp = qkv_ref[pl.ds(rp, BLK), AW + kv * HD:AW + (kv + 1) * HD]
            vc = qkv_ref[pl.ds(r0, BLK), AW + KVW + kv * HD:AW + KVW + (kv + 1) * HD]
            vp = qkv_ref[pl.ds(rp, BLK), AW + KVW + kv * HD:AW + KVW + (kv + 1) * HD]
            sc = jnp.where(mask_cur, _dot_nt(q, kc) * SCALE, NEG)
            sp = jnp.where(mask_prev, _dot_nt(q, kp) * SCALE, NEG)
            sink = sink_ref[0, h]
            m = jnp.maximum(jnp.maximum(jnp.max(sc, axis=-1, keepdims=True),
                                        jnp.max(sp, axis=-1, keepdims=True)), sink)
            pc = jnp.exp(sc - m)
            pp = jnp.exp(sp - m)
            den = (jnp.sum(pc, axis=-1, keepdims=True) + jnp.sum(pp, axis=-1, keepdims=True)
                   + jnp.exp(sink - m))
            inv = 1.0 / den
            o = _dot((pc * inv).astype(BF16), vc) + _dot((pp * inv).astype(BF16), vp)
            o_ref[:, h * HD:(h + 1) * HD] = o.astype(BF16)
            lse_ref[:, h:h + 1] = m + jnp.log(den)

    return pl.pallas_call(
        body, name="attn_fwd", grid=(T // BLK,),
        in_specs=[pl.BlockSpec(memory_space=pltpu.SMEM), _whole((T, QKVW))],
        out_specs=[_rows(BLK, AW), _rows(BLK, NQ)],
        out_shape=[jax.ShapeDtypeStruct((T, AW), BF16), jax.ShapeDtypeStruct((T, NQ), F32)],
        compiler_params=_params("parallel"),
    )(sinks, qkv)


CONV_TM = 256
CONV_SUB = 32


def _glu(ab):
    a = ab[:, 0:C].astype(F32)
    b = ab[:, C:2 * C].astype(F32)
    return a * _sigmoid(b)


def _conv_fwd(glu, conv_w, conv_b, ln_g, ln_b):
    tm = CONV_TM

    def body(cur_ref, prev_ref, w_ref, cb_ref, g_ref, b_ref, u_ref, c_ref, zp_ref):
        i = pl.program_id(0)
        zprev = _glu(prev_ref[tm - HALO:tm, :])
        zp_ref[0:HALO, :] = jnp.where(i > 0, zprev, 0.0)
        zp_ref[HALO:HALO + tm, :] = _glu(cur_ref[...])
        for s in range(tm // CONV_SUB):
            base = HALO + s * CONV_SUB - (KW - 1)
            acc = jnp.broadcast_to(cb_ref[...], (CONV_SUB, C))
            for j in range(KW):
                acc = acc + w_ref[j:j + 1, :] * zp_ref[base + j:base + j + CONV_SUB, :]
            rows = slice(s * CONV_SUB, (s + 1) * CONV_SUB)
            u_ref[rows, :] = acc
            mu = jnp.mean(acc, axis=-1, keepdims=True)
            xc = acc - mu
            var = jnp.mean(xc * xc, axis=-1, keepdims=True)
            y = xc * lax.rsqrt(var + EPS) * g_ref[...] + b_ref[...]
            c_ref[rows, :] = (y * _sigmoid(y)).astype(BF16)

    return pl.pallas_call(
        body, name="conv_fwd", grid=(T // tm,),
        in_specs=[_rows(tm, 2 * C),
                  pl.BlockSpec((tm, 2 * C), lambda i: (jnp.maximum(i - 1, 0), 0)),
                  _whole((KW, C)), _whole((1, C)), _whole((1, C)), _whole((1, C))],
        out_specs=[_rows(tm, C), _rows(tm, C)],
        out_shape=[jax.ShapeDtypeStruct((T, C), F32), jax.ShapeDtypeStruct((T, C), BF16)],
        scratch_shapes=[pltpu.VMEM((HALO + tm, C), F32)],
        compiler_params=_params("parallel"),
    )(glu, glu, conv_w, conv_b, ln_g, ln_b)


def _mix_out(x, o, cact, gl, w_ap, w_cp, b_cp, w_out):
    tm = 256

    def body(x_ref, o_ref, c_ref, gl_ref, wap_ref, wcp_ref, bcp_ref, wo_ref,
             ya_ref, yc_ref, mg_ref, x1_ref):
        ov, cv = o_ref[...], c_ref[...]
        ya = jnp.concatenate([_dot(ov, wap_ref[s]) for s in range(N_CHIPS)], axis=1)
        yc = jnp.concatenate([_dot(cv, wcp_ref[s]) for s in range(N_CHIPS)], axis=1) + bcp_ref[...]
        g0 = _sigmoid(gl_ref[:, 0:D].astype(F32))
        g1 = _sigmoid(gl_ref[:, D:2 * D].astype(F32))
        mg = (g0 * ya + g1 * yc).astype(BF16)
        ya_ref[...] = ya.astype(BF16)
        yc_ref[...] = yc.astype(BF16)
        mg_ref[...] = mg
        x1_ref[...] = x_ref[...] + _dot(mg, wo_ref[...])

    return pl.pallas_call(
        body, name="mix_out", grid=(T // tm,),
        in_specs=[_rows(tm, D), _rows(tm, AW), _rows(tm, C), _rows(tm, 2 * D),
                  _whole((N_CHIPS, AW, D // N_CHIPS)), _whole((N_CHIPS, C, D // N_CHIPS)), _whole((1, D)),
                  _whole((D, D))],
        out_specs=[_rows(tm, D), _rows(tm, D), _rows(tm, D), _rows(tm, D)],
        out_shape=[jax.ShapeDtypeStruct((T, D), BF16), jax.ShapeDtypeStruct((T, D), BF16),
                   jax.ShapeDtypeStruct((T, D), BF16), jax.ShapeDtypeStruct((T, D), F32)],
        compiler_params=_params("parallel"),
    )(x, o, cact, gl, w_ap, w_cp, b_cp, w_out)


def _ffn_in(x1, g_ffn, w_fi):
    tm = 256

    def body(x_ref, g_ref, w_ref, h_ref, gu_ref, act_ref):
        xv = x_ref[...]
        r = lax.rsqrt(jnp.mean(xv * xv, axis=-1, keepdims=True) + EPS)
        h = (xv * r * g_ref[...]).astype(BF16)
        h_ref[...] = h
        for s in range(N_CHIPS // 2):
            c0 = s * FSH
            gate = _dot(h, w_ref[s])
            up = _dot(h, w_ref[s + N_CHIPS // 2])
            gu_ref[:, c0:c0 + FSH] = gate.astype(BF16)
            gu_ref[:, DFF + c0:DFF + c0 + FSH] = up.astype(BF16)
            act_ref[:, c0:c0 + FSH] = (gate * _sigmoid(gate) * up).astype(BF16)

    return pl.pallas_call(
        body, name="ffn_in", grid=(T // tm,),
        in_specs=[_rows(tm, D), _whole((1, D)), _whole((N_CHIPS, D, FSH))],
        out_specs=[_rows(tm, D), _rows(tm, 2 * DFF), _rows(tm, DFF)],
        out_shape=[jax.ShapeDtypeStruct((T, D), BF16), jax.ShapeDtypeStruct((T, 2 * DFF), BF16),
                   jax.ShapeDtypeStruct((T, DFF), BF16)],
        compiler_params=_params("parallel"),
    )(x1, g_ffn, w_fi)


def _ffn_out_loss(x1, act, w_dn, g_final, target):
    tm = 256

    def body(x_ref, a_ref, w_ref, g_ref, t_ref, dx_ref, dxb_ref, dg_ref, loss_ref):
        i = pl.program_id(0)
        x2 = x_ref[...] + _dot(a_ref[...], w_ref[...])
        r = lax.rsqrt(jnp.mean(x2 * x2, axis=-1, keepdims=True) + EPS)
        xh = x2 * r
        g = g_ref[...]
        err = xh * g - t_ref[...]
        dy = err * (1.0 / D)
        dyg = dy * g
        dx = r * (dyg - xh * jnp.mean(dyg * xh, axis=-1, keepdims=True))
        dx_ref[...] = dx
        dxb_ref[...] = dx.astype(BF16)
        part = 0.5 * jnp.sum(jnp.mean(err * err, axis=-1, keepdims=True), axis=0, keepdims=True)

        @pl.when(i == 0)
        def _():
            dg_ref[...] = jnp.zeros_like(dg_ref)
            loss_ref[...] = jnp.zeros_like(loss_ref)

        dg_ref[...] += jnp.sum(dy * xh, axis=0, keepdims=True)
        loss_ref[...] += jnp.broadcast_to(part, loss_ref.shape)

    return pl.pallas_call(
        body, name="ffn_out_loss", grid=(T // tm,),
        in_specs=[_rows(tm, D), _rows(tm, DFF), _whole((DFF, D)), _whole((1, D)), _rows(tm, D)],
        out_specs=[_rows(tm, D), _rows(tm, D), _whole((1, D)), _whole((1, 128))],
        out_shape=[jax.ShapeDtypeStruct((T, D), F32), jax.ShapeDtypeStruct((T, D), BF16),
                   jax.ShapeDtypeStruct((1, D), F32), jax.ShapeDtypeStruct((1, 128), F32)],
        compiler_params=_params("arbitrary"),
    )(x1, act, w_dn, g_final, target)


def _const(shape):
    return pl.BlockSpec(shape, lambda i: tuple(0 for _ in shape), pipeline_mode=pl.Buffered(1))


def _ffn_bwd(dx2, dx2b, gu, x1, g_ffn, w_dn_t, w_fi_t):
    tm = 256

    def body(dx_ref, dxb_ref, gu_ref, x_ref, g_ref, wdn_ref, wfi_ref,
             dgu_ref, dx1_ref, dx1b_ref, dg_ref):
        i = pl.program_id(0)
        dxb = dxb_ref[...]
        dh = jnp.zeros((tm, D), F32)
        for k in range(N_CHIPS // 2):
            c0 = k * FSH
            dact = _dot_nt(dxb, wdn_ref[c0:c0 + FSH, :])
            gate = gu_ref[:, c0:c0 + FSH].astype(F32)
            up = gu_ref[:, DFF + c0:DFF + c0 + FSH].astype(F32)
            s = _sigmoid(gate)
            dup = (dact * gate * s).astype(BF16)
            dgate = (dact * up * s * (1.0 + gate * (1.0 - s))).astype(BF16)
            dgu_ref[:, c0:c0 + FSH] = dgate
            dgu_ref[:, DFF + c0:DFF + c0 + FSH] = dup
            dh = dh + _dot_nt(dgate, wfi_ref[k]) + _dot_nt(dup, wfi_ref[k + N_CHIPS // 2])
        xv = x_ref[...]
        r = lax.rsqrt(jnp.mean(xv * xv, axis=-1, keepdims=True) + EPS)
        xh = xv * r
        dhg = dh * g_ref[...]
        dx1 = dx_ref[...] + r * (dhg - xh * jnp.mean(dhg * xh, axis=-1, keepdims=True))
        dx1_ref[...] = dx1
        dx1b_ref[...] = dx1.astype(BF16)

        @pl.when(i == 0)
        def _():
            dg_ref[...] = jnp.zeros_like(dg_ref)

        dg_ref[...] += jnp.sum(dh * xh, axis=0, keepdims=True)

    return pl.pallas_call(
        body, name="ffn_bwd", grid=(T // tm,),
        in_specs=[_rows(tm, D), _rows(tm, D), _rows(tm, 2 * DFF), _rows(tm, D), _whole((1, D)),
                  _const((DFF, D)), _const((N_CHIPS, D, FSH))],
        out_specs=[_rows(tm, 2 * DFF), _rows(tm, D), _rows(tm, D), _whole((1, D))],
        out_shape=[jax.ShapeDtypeStruct((T, 2 * DFF), BF16), jax.ShapeDtypeStruct((T, D), F32),
                   jax.ShapeDtypeStruct((T, D), BF16), jax.ShapeDtypeStruct((1, D), F32)],
        compiler_params=_params("arbitrary"),
    )(dx2, dx2b, gu, x1, g_ffn, w_dn_t, w_fi_t)


def _mix_bwd(dx1b, gl, ya, yc, w_out_t, w_ap_t, w_cp_t, dep):
    tm = 256

    def body(dx_ref, gl_ref, ya_ref, yc_ref, wo_ref, wap_ref, wcp_ref, dep_ref,
             dya_ref, dyc_ref, dgl_ref, do_ref, dc_ref, db_ref):
        i = pl.program_id(0)
        dm = _dot_nt(dx_ref[...], wo_ref[...])
        g0 = _sigmoid(gl_ref[:, 0:D].astype(F32))
        g1 = _sigmoid(gl_ref[:, D:2 * D].astype(F32))
        dya = dm * g0
        dyc = dm * g1
        dgl_ref[:, 0:D] = (dya * ya_ref[...].astype(F32) * (1.0 - g0)).astype(BF16)
        dgl_ref[:, D:2 * D] = (dyc * yc_ref[...].astype(F32) * (1.0 - g1)).astype(BF16)
        dyab = dya.astype(BF16)
        dycb = dyc.astype(BF16)
        dya_ref[...] = dyab
        dyc_ref[...] = dycb
        sw = D // N_CHIPS
        do = jnp.zeros((tm, AW), F32)
        dcv = jnp.zeros((tm, C), F32)
        for s in range(N_CHIPS):
            do = do + _dot_nt(dyab[:, s * sw:(s + 1) * sw], wap_ref[s])
            dcv = dcv + _dot_nt(dycb[:, s * sw:(s + 1) * sw], wcp_ref[s])
        do_ref[...] = do.astype(BF16)
        dc_ref[...] = dcv

        @pl.when(i == 0)
        def _():
            db_ref[...] = jnp.zeros_like(db_ref)

        db_ref[...] += jnp.sum(dyc, axis=0, keepdims=True)

    return pl.pallas_call(
        body, name="mix_bwd", grid=(T // tm,),
        in_specs=[_rows(tm, D), _rows(tm, 2 * D), _rows(tm, D), _rows(tm, D),
                  _whole((D, D)), _whole((N_CHIPS, AW, D // N_CHIPS)), _whole((N_CHIPS, C, D // N_CHIPS)),
                  _whole((8, 128))],
        out_specs=[_rows(tm, D), _rows(tm, D), _rows(tm, 2 * D), _rows(tm, AW), _rows(tm, C),
                   _whole((1, D))],
        out_shape=[jax.ShapeDtypeStruct((T, D), BF16), jax.ShapeDtypeStruct((T, D), BF16),
                   jax.ShapeDtypeStruct((T, 2 * D), BF16), jax.ShapeDtypeStruct((T, AW), BF16),
                   jax.ShapeDtypeStruct((T, C), F32), jax.ShapeDtypeStruct((1, D), F32)],
        compiler_params=_params("arbitrary"),
    )(dx1b, gl, ya, yc, w_out_t, w_ap_t, w_cp_t, dep)


def _conv_bwd(glu, u, dc, conv_w, ln_g, ln_b, dep):
    tm = CONV_TM
    nblk = T // tm

    def du_of(uv, dcv, g_ref, b_ref):
        mu = jnp.mean(uv, axis=-1, keepdims=True)
        xc = uv - mu
        var = jnp.mean(xc * xc, axis=-1, keepdims=True)
        rstd = lax.rsqrt(var + EPS)
        xh = xc * rstd
        y = xh * g_ref[...] + b_ref[...]
        sg = _sigmoid(y)
        dy = dcv * (sg * (1.0 + y * (1.0 - sg)))
        dxh = dy * g_ref[...]
        du = rstd * (dxh - jnp.mean(dxh, axis=-1, keepdims=True)
                     - xh * jnp.mean(dxh * xh, axis=-1, keepdims=True))
        return du, dy, xh

    def body(cur_ref, prev_ref, u_ref, un_ref, dc_ref, dcn_ref, w_ref, g_ref, b_ref, dep_ref,
             dglu_ref, dw_ref, dcb_ref, dg_ref, db_ref, zp_ref, du_ref):
        i = pl.program_id(0)

        @pl.when(i == 0)
        def _():
            dw_ref[...] = jnp.zeros_like(dw_ref)
            dcb_ref[...] = jnp.zeros_like(dcb_ref)
            dg_ref[...] = jnp.zeros_like(dg_ref)
            db_ref[...] = jnp.zeros_like(db_ref)

        zprev = _glu(prev_ref[tm - HALO:tm, :])
        zp_ref[0:HALO, :] = jnp.where(i > 0, zprev, 0.0)
        zp_ref[HALO:HALO + tm, :] = _glu(cur_ref[...])

        dun, _, _ = du_of(un_ref[0:HALO, :], dcn_ref[0:HALO, :], g_ref, b_ref)
        du_ref[tm:tm + HALO, :] = jnp.where(i < nblk - 1, dun, 0.0)
        dg_acc = jnp.zeros((1, C), F32)
        db_acc = jnp.zeros((1, C), F32)
        dcb_acc = jnp.zeros((1, C), F32)
        for s in range(tm // CONV_SUB):
            rows = slice(s * CONV_SUB, (s + 1) * CONV_SUB)
            du, dy, xh = du_of(u_ref[rows, :], dc_ref[rows, :], g_ref, b_ref)
            du_ref[rows, :] = du
            dg_acc = dg_acc + jnp.sum(dy * xh, axis=0, keepdims=True)
            db_acc = db_acc + jnp.sum(dy, axis=0, keepdims=True)
            dcb_acc = dcb_acc + jnp.sum(du, axis=0, keepdims=True)
        dg_ref[...] += dg_acc
        db_ref[...] += db_acc
        dcb_ref[...] += dcb_acc

        for j in range(KW):
            acc = jnp.zeros((CONV_SUB, C), F32)
            for s in range(tm // CONV_SUB):
                base = HALO + s * CONV_SUB - (KW - 1) + j
                acc = acc + du_ref[s * CONV_SUB:(s + 1) * CONV_SUB, :] * zp_ref[base:base + CONV_SUB, :]
            dw_ref[j:j + 1, :] += jnp.sum(acc, axis=0, keepdims=True)

        for s in range(tm // CONV_SUB):
            rows = slice(s * CONV_SUB, (s + 1) * CONV_SUB)
            dz = jnp.zeros((CONV_SUB, C), F32)
            for j in range(KW):
                o = s * CONV_SUB + (KW - 1) - j
                dz = dz + w_ref[j:j + 1, :] * du_ref[o:o + CONV_SUB, :]
            a = cur_ref[rows, 0:C].astype(F32)
            sb = _sigmoid(cur_ref[rows, C:2 * C].astype(F32))
            dglu_ref[rows, 0:C] = (dz * sb).astype(BF16)
            dglu_ref[rows, C:2 * C] = (dz * a * sb * (1.0 - sb)).astype(BF16)

    nxt = lambda i: (jnp.minimum(i + 1, nblk - 1), 0)
    return pl.pallas_call(
        body, name="conv_bwd", grid=(nblk,),
        in_specs=[_rows(tm, 2 * C),
                  pl.BlockSpec((tm, 2 * C), lambda i: (jnp.maximum(i - 1, 0), 0)),
                  _rows(tm, C), pl.BlockSpec((tm, C), nxt),
                  _rows(tm, C), pl.BlockSpec((tm, C), nxt),
                  _whole((KW, C)), _whole((1, C)), _whole((1, C)), _whole((8, 128))],
        out_specs=[_rows(tm, 2 * C), _whole((KW, C)), _whole((1, C)), _whole((1, C)), _whole((1, C))],
        out_shape=[jax.ShapeDtypeStruct((T, 2 * C), BF16), jax.ShapeDtypeStruct((KW, C), F32),
                   jax.ShapeDtypeStruct((1, C), F32), jax.ShapeDtypeStruct((1, C), F32),
                   jax.ShapeDtypeStruct((1, C), F32)],
        scratch_shapes=[pltpu.VMEM((HALO + tm, C), F32), pltpu.VMEM((tm + HALO, C), F32)],
        compiler_params=_params("arbitrary"),
    )(glu, glu, u, u, dc, dc, conv_w, ln_g, ln_b, dep)


def _attn_bwd(qkv, o, do, lse, sinks):
    def body(sink_ref, qkv_ref, o_ref, do_ref, lse_ref, dq_ref, dkv_ref, ds_ref):
        i = pl.program_id(0)

        @pl.when(i == 0)
        def _():
            dkv_ref[...] = jnp.zeros_like(dkv_ref)
            ds_ref[...] = jnp.zeros_like(ds_ref)

        r0 = pl.multiple_of(i * BLK, BLK)
        rp = pl.multiple_of(jnp.maximum(i - 1, 0) * BLK, BLK)
        mask_cur, mask_prev = _attn_masks(i)
        for h in range(NQ):
            kv = h // GROUP
            kcol = slice(AW + kv * HD, AW + (kv + 1) * HD)
            vcol = slice(AW + KVW + kv * HD, AW + KVW + (kv + 1) * HD)
            hcol = slice(h * HD, (h + 1) * HD)
            q = qkv_ref[pl.ds(r0, BLK), hcol]
            kc = qkv_ref[pl.ds(r0, BLK), kcol]
            kp = qkv_ref[pl.ds(rp, BLK), kcol]
            vc = qkv_ref[pl.ds(r0, BLK), vcol]
            vp = qkv_ref[pl.ds(rp, BLK), vcol]
            doh = do_ref[:, hcol]
            lse_h = lse_ref[:, h:h + 1]
            dl = jnp.sum(doh.astype(F32) * o_ref[:, hcol].astype(F32), axis=-1, keepdims=True)
            pc = jnp.where(mask_cur, jnp.exp(_dot_nt(q, kc) * SCALE - lse_h), 0.0)
            pp = jnp.where(mask_prev, jnp.exp(_dot_nt(q, kp) * SCALE - lse_h), 0.0)
            dsc = (pc * (_dot_nt(doh, vc) - dl)).astype(BF16)
            dsp = (pp * (_dot_nt(doh, vp) - dl)).astype(BF16)
            dq_ref[:, hcol] = ((_dot(dsc, kc) + _dot(dsp, kp)) * SCALE).astype(BF16)
            dkc = slice(kv * HD, (kv + 1) * HD)
            dvc = slice(KVW + kv * HD, KVW + (kv + 1) * HD)
            dkv_ref[pl.ds(r0, BLK), dkc] += _dot_tn(dsc, q) * SCALE
            dkv_ref[pl.ds(rp, BLK), dkc] += _dot_tn(dsp, q) * SCALE
            dkv_ref[pl.ds(r0, BLK), dvc] += _dot_tn(pc.astype(BF16), doh)
            dkv_ref[pl.ds(rp, BLK), dvc] += _dot_tn(pp.astype(BF16), doh)
            psink = jnp.exp(sink_ref[0, h] - lse_h)
            dsink = -jnp.sum(psink * dl, axis=0, keepdims=True)
            ds_ref[h:h + 1, :] += jnp.broadcast_to(dsink, (1, 128))

    return pl.pallas_call(
        body, name="attn_bwd", grid=(T // BLK,),
        in_specs=[pl.BlockSpec(memory_space=pltpu.SMEM), _whole((T, QKVW)),
                  _rows(BLK, AW), _rows(BLK, AW), _rows(BLK, NQ)],
        out_specs=[_rows(BLK, AW), _whole((T, 2 * KVW)), _whole((NQ, 128))],
        out_shape=[jax.ShapeDtypeStruct((T, AW), BF16), jax.ShapeDtypeStruct((T, 2 * KVW), F32),
                   jax.ShapeDtypeStruct((NQ, 128), F32)],
        compiler_params=_params("arbitrary"),
    )(sinks, qkv, o, do, lse)


PROJ_PARTS = [(0, AW), (AW, QKVW), (GLU_OFF, GATE_OFF), (GATE_OFF, INW)]


def _in_proj_bwd(dq, dkv, dglu, dgl, x, dx1, g_mix, w_in):
    tm = 256

    def body(dq_ref, dkv_ref, dglu_ref, dgl_ref, x_ref, dx1_ref, g_ref, w_ref, gx_ref, dg_ref, db_ref):
        i = pl.program_id(0)

        @pl.when(i == 0)
        def _():
            dg_ref[...] = jnp.zeros_like(dg_ref)
            db_ref[...] = jnp.zeros_like(db_ref)

        dh = jnp.zeros((tm, D), F32)
        for part_ref, (lo, hi) in zip((dq_ref, dkv_ref, dglu_ref, dgl_ref), PROJ_PARTS):
            part = part_ref[...]
            dh = dh + _dot_nt(part.astype(BF16), w_ref[:, lo:hi])
            db_ref[:, lo:hi] += jnp.sum(part.astype(F32), axis=0, keepdims=True)
        xv = x_ref[...]
        r = lax.rsqrt(jnp.mean(xv * xv, axis=-1, keepdims=True) + EPS)
        xh = xv * r
        dhg = dh * g_ref[...]
        gx_ref[...] = dx1_ref[...] + r * (dhg - xh * jnp.mean(dhg * xh, axis=-1, keepdims=True))
        dg_ref[...] += jnp.sum(dh * xh, axis=0, keepdims=True)

    return pl.pallas_call(
        body, name="in_proj_bwd", grid=(T // tm,),
        in_specs=[_rows(tm, AW), _rows(tm, 2 * KVW), _rows(tm, 2 * C), _rows(tm, 2 * D),
                  _rows(tm, D), _rows(tm, D), _whole((1, D)), _const((D, INW))],
        out_specs=[_rows(tm, D), _whole((1, D)), _whole((1, INW))],
        out_shape=[jax.ShapeDtypeStruct((T, D), F32), jax.ShapeDtypeStruct((1, D), F32),
                   jax.ShapeDtypeStruct((1, INW), F32)],
        compiler_params=_params("arbitrary"),
    )(dq, dkv, dglu, dgl, x, dx1, g_mix, w_in)


def _grad_w_in(h, dq, dkv, dglu, dgl):
    tk = 512

    def body(h_ref, dq_ref, dkv_ref, dglu_ref, dgl_ref, o_ref):
        hv = h_ref[...]
        for part_ref, (lo, hi) in zip((dq_ref, dkv_ref, dglu_ref, dgl_ref), PROJ_PARTS):
            o_ref[:, lo:hi] = _dot_tn(hv, part_ref[...].astype(BF16)).astype(BF16)

    return pl.pallas_call(
        body, name="grad_w_in", grid=(D // tk,),
        in_specs=[pl.BlockSpec((T, tk), lambda i: (0, i)), _whole((T, AW)), _whole((T, 2 * KVW)),
                  _whole((T, 2 * C)), _whole((T, 2 * D))],
        out_specs=pl.BlockSpec((tk, INW), lambda i: (i, 0)),
        out_shape=jax.ShapeDtypeStruct((D, INW), BF16),
        compiler_params=_params("parallel"),
    )(h, dq, dkv, dglu, dgl)


def _grad_w(a, b, name, tk, tn, col_sharded):
    k, n = a.shape[1], b.shape[1]

    def body(a_ref, b_ref, o_ref):
        o_ref[...] = _dot_tn(a_ref[...], b_ref[...]).astype(BF16)

    if col_sharded:
        per = n // N_CHIPS // tn
        shape = (N_CHIPS, k, n // N_CHIPS)
        out_map = lambda i, j: (j // per, i, j % per)
    else:
        shape = (1, k, n)
        out_map = lambda i, j: (0, i, j)
    out = pl.pallas_call(
        body, name=name, grid=(k // tk, n // tn),
        in_specs=[pl.BlockSpec((T, tk), lambda i, j: (0, i)), pl.BlockSpec((T, tn), lambda i, j: (0, j))],
        out_specs=pl.BlockSpec((None, tk, tn), out_map),
        out_shape=jax.ShapeDtypeStruct(shape, BF16),
        compiler_params=_params("parallel", "parallel"),
    )(a, b)
    return out if col_sharded else out.reshape(N_CHIPS, k // N_CHIPS, n)


HBM_SPEC = pl.BlockSpec(memory_space=pltpu.HBM)


def _place():
    x, y, c = lax.axis_index("x"), lax.axis_index("y"), lax.axis_index("c")
    chips = [(1 - x, y), (x, 1 - y), (1 - x, 1 - y)]
    return x, y, c, chips


SEM_SPEC = pl.BlockSpec(memory_space=pltpu.SEMAPHORE)
ANY_SPEC = pl.BlockSpec(memory_space=pl.ANY)
VMEM_SPEC = pl.BlockSpec(memory_space=pltpu.VMEM)
EFFECT = pltpu.SideEffectType.DATAFLOW_SIDE_EFFECTING


def _gather_ends(src, land, x, y, c, chips):
    kh = src.shape[0] // 2
    s_me = 2 * x + y
    ends = [(src.at[pl.ds(c * kh, kh)], land.at[s_me, pl.ds(c * kh, kh)], (*chip, c)) for chip in chips]
    return ends + [(src, land.at[s_me], (x, y, 1 - c))]


def _reduce_ends(src, land, x, y, c, chips):
    return [(src.at[2 * chip[0] + chip[1]], land.at[j], (*chip, c)) for j, chip in enumerate(chips)]


def _chip_copies(ends, srcs, lands, send_sems, recv_sems):
    x, y, c, chips = _place()
    copies = []
    for src, land in zip(srcs, lands):
        for s, d, to in ends(src, land, x, y, c, chips):
            k = len(copies)
            copies.append(pltpu.make_async_remote_copy(
                src_ref=s, dst_ref=d, send_sem=send_sems.at[k], recv_sem=recv_sems.at[k],
                device_id=to, device_id_type=MESH))
    return copies


GATHER_PEERS, REDUCE_PEERS = 4, 3


def _chip_start(name, ends, peers, srcs, lands):
    n = len(srcs)

    def body(*refs):
        copies = _chip_copies(ends, refs[:n], refs[n:2 * n], refs[2 * n], refs[2 * n + 1])
        for cp in copies:
            cp.start()
        token = refs[-1]
        token[...] = jnp.zeros_like(token)

    hbm = lambda a: pltpu.HBM(a.shape, a.dtype)
    res = pl.pallas_call(
        body, name=name,
        out_shape=(pltpu.SemaphoreType.DMA((peers * n,)), pltpu.SemaphoreType.DMA((peers * n,)),
                   *[hbm(a) for a in srcs], *[hbm(a) for a in lands],
                   jax.ShapeDtypeStruct((8, 128), F32)),
        in_specs=[HBM_SPEC] * (2 * n),
        out_specs=(SEM_SPEC, SEM_SPEC, *[HBM_SPEC] * (2 * n), VMEM_SPEC),
        input_output_aliases={i: 2 + i for i in range(2 * n)},
        compiler_params=pltpu.CompilerParams(has_side_effects=EFFECT),
    )(*[pltpu.with_memory_space_constraint(a, pltpu.HBM) for a in (*srcs, *lands)])
    return res[0], res[1], list(res[2:2 + n]), list(res[2 + n:2 + 2 * n]), res[-1]


def _chip_wait(name, ends, send_sems, recv_sems, srcs, lands, after):
    n, na = len(srcs), len(after)

    def body(*refs):
        copies = _chip_copies(ends, refs[:n], refs[n:2 * n], refs[2 * n], refs[2 * n + 1])
        for cp in copies:
            cp.wait_send()
            cp.wait_recv()

    hbm = lambda a: pltpu.HBM(a.shape, a.dtype)
    res = pl.pallas_call(
        body, name=name,
        out_shape=tuple(hbm(a) for a in (*srcs, *lands)),
        in_specs=[HBM_SPEC] * (2 * n) + [SEM_SPEC, SEM_SPEC] + [ANY_SPEC] * na,
        out_specs=tuple([HBM_SPEC] * (2 * n)),
        input_output_aliases={i: i for i in range(2 * n)},
        compiler_params=pltpu.CompilerParams(has_side_effects=EFFECT),
    )(*srcs, *lands, send_sems, recv_sems, *after)
    return list(res[:n]), list(res[n:])


def _pair_forward(name, lands):
    n = len(lands)

    def body(*refs):
        bufs = refs[n:2 * n]
        send_sems, recv_sems = refs[2 * n:]
        x, y, c, chips = _place()

        def copy(w, j, cc, to):
            kh = bufs[w].shape[1] // 2
            blk = bufs[w].at[2 * chips[j][0] + chips[j][1], pl.ds(cc * kh, kh)]
            return pltpu.make_async_remote_copy(
                src_ref=blk, dst_ref=blk, send_sem=send_sems.at[3 * w + j],
                recv_sem=recv_sems.at[3 * w + j], device_id=to, device_id_type=MESH)

        sends = [copy(w, j, c, (x, y, 1 - c)) for w in range(n) for j in range(3)]
        for cp in sends:
            cp.start()
        for w in range(n):
            for j in range(3):
                copy(w, j, 1 - c, (x, y, c)).wait_recv()
        for cp in sends:
            cp.wait_send()

    return pl.pallas_call(
        body, name=name,
        in_specs=[HBM_SPEC] * n, out_specs=[HBM_SPEC] * n,
        out_shape=[jax.ShapeDtypeStruct(a.shape, a.dtype) for a in lands],
        input_output_aliases={w: w for w in range(n)},
        scratch_shapes=[pltpu.SemaphoreType.DMA((3 * n,)), pltpu.SemaphoreType.DMA((3 * n,))],
    )(*lands)


def _pair_exchange(name, grads):
    n = len(grads)

    def body(*refs):
        ins, outs = refs[:n], refs[n:2 * n]
        send_sems, recv_sems = refs[2 * n:]
        x, y, c, _ = _place()
        copies = []
        for w in range(n):
            kh = ins[w].shape[1] // 2
            cp = pltpu.make_async_remote_copy(
                src_ref=ins[w].at[:, pl.ds((1 - c) * kh, kh)], dst_ref=outs[w],
                send_sem=send_sems.at[w], recv_sem=recv_sems.at[w],
                device_id=(x, y, 1 - c), device_id_type=MESH)
            cp.start()
            copies.append(cp)
        for cp in copies:
            cp.wait()

    return pl.pallas_call(
        body, name=name,
        in_specs=[HBM_SPEC] * n, out_specs=[HBM_SPEC] * n,
        out_shape=[jax.ShapeDtypeStruct((g.shape[0], g.shape[1] // 2, g.shape[2]), g.dtype) for g in grads],
        scratch_shapes=[pltpu.SemaphoreType.DMA((n,)), pltpu.SemaphoreType.DMA((n,))],
    )(*grads)


def _row_tile(k):
    for t in (256, 128, 176, 64, 32, 16):
        if k % t == 0:
            return t
    raise ValueError(k)


def _pair_sum(c_idx, g, got, name):
    _, k, n = g.shape
    kh = k // 2
    tm = _row_tile(kh)
    nb = kh // tm

    def body(c_ref, g_ref, r_ref, o_ref):
        o_ref[...] = (g_ref[...].astype(F32) + r_ref[...].astype(F32)).astype(BF16)

    return pl.pallas_call(
        body, name=name,
        grid_spec=pltpu.PrefetchScalarGridSpec(
            num_scalar_prefetch=1, grid=(N_CHIPS, nb),
            in_specs=[pl.BlockSpec((1, tm, n), lambda s, i, c_ref: (s, c_ref[0] * nb + i, 0)),
                      pl.BlockSpec((1, tm, n), lambda s, i, c_ref: (s, i, 0))],
            out_specs=pl.BlockSpec((1, tm, n), lambda s, i, c_ref: (s, i, 0))),
        out_shape=jax.ShapeDtypeStruct((N_CHIPS, kh, n), BF16),
        compiler_params=_params("parallel", "parallel"),
    )(c_idx, g, got)


def _chip_sum(sc_idx, mine, got, name):
    _, kh, n = mine.shape
    tm = _row_tile(kh)

    def body(sc_ref, m_ref, r_ref, o_ref):
        acc = m_ref[0].astype(F32)
        for j in range(3):
            acc = acc + r_ref[j].astype(F32)
        o_ref[0] = acc

    return pl.pallas_call(
        body, name=name,
        grid_spec=pltpu.PrefetchScalarGridSpec(
            num_scalar_prefetch=1, grid=(kh // tm,),
            in_specs=[pl.BlockSpec((1, tm, n), lambda i, sc_ref: (sc_ref[0], i, 0)),
                      pl.BlockSpec((3, tm, n), lambda i, sc_ref: (0, i, 0))],
            out_specs=pl.BlockSpec((1, tm, n), lambda i, sc_ref: (sc_ref[1], i, 0))),
        out_shape=jax.ShapeDtypeStruct((2, kh, n), F32),
        compiler_params=_params("parallel"),
    )(sc_idx, mine, got)


def _pair_share(name, halves):
    n = len(halves)

    def body(*refs):
        bufs = refs[n:2 * n]
        send_sems, recv_sems = refs[2 * n:]
        x, y, c, _ = _place()
        copies = []
        for w in range(n):
            cp = pltpu.make_async_remote_copy(
                src_ref=bufs[w].at[c], dst_ref=bufs[w].at[c],
                send_sem=send_sems.at[w], recv_sem=recv_sems.at[w],
                device_id=(x, y, 1 - c), device_id_type=MESH)
            cp.start()
            copies.append(cp)
        for w in range(n):
            copies[w].wait_send()
            pltpu.make_async_remote_copy(
                src_ref=bufs[w].at[1 - c], dst_ref=bufs[w].at[1 - c],
                send_sem=send_sems.at[w], recv_sem=recv_sems.at[w],
                device_id=(x, y, c), device_id_type=MESH).wait_recv()

    return pl.pallas_call(
        body, name=name,
        in_specs=[HBM_SPEC] * n, out_specs=[HBM_SPEC] * n,
        out_shape=[jax.ShapeDtypeStruct(h.shape, h.dtype) for h in halves],
        input_output_aliases={w: w for w in range(n)},
        scratch_shapes=[pltpu.SemaphoreType.DMA((n,)), pltpu.SemaphoreType.DMA((n,))],
    )(*halves)


def _gather_small(blocks):
    n = len(blocks)

    def body(*refs):
        ins, outs = refs[:n], refs[n:2 * n]
        send_sems, recv_sems, local_sems = refs[2 * n:]
        x, y, c, chips = _place()
        me, sibling = (x, y, c), (x, y, 1 - c)

        def rows(w, px, py, pc):
            m = ins[w].shape[0]
            return outs[w].at[pl.ds((4 * px + 2 * py + pc) * m, m), :]

        def copy(w, k, block, to, src=None):
            return pltpu.make_async_remote_copy(
                src_ref=rows(w, *block) if src is None else src, dst_ref=rows(w, *block),
                send_sem=send_sems.at[7 * w + k], recv_sem=recv_sems.at[7 * w + k],
                device_id=to, device_id_type=MESH)

        started, owns = [], []
        for w in range(n):
            own = pltpu.make_async_copy(ins[w], rows(w, *me), local_sems.at[w])
            own.start()
            owns.append(own)
            first = [copy(w, 0, me, sibling, src=ins[w])]
            first += [copy(w, 1 + j, me, (*chip, c), src=ins[w]) for j, chip in enumerate(chips)]
            for cp in first:
                cp.start()
            started += first
        for w in range(n):
            for j, chip in enumerate(chips):
                copy(w, 1 + j, (*chip, c), me).wait_recv()
                cp = copy(w, 4 + j, (*chip, c), sibling)
                cp.start()
                started.append(cp)
        for w in range(n):
            copy(w, 0, sibling, me).wait_recv()
            for j, chip in enumerate(chips):
                copy(w, 4 + j, (*chip, 1 - c), me).wait_recv()
        for cp in started:
            cp.wait_send()
        for own in owns:
            own.wait()

    vmem = pl.BlockSpec(memory_space=pltpu.VMEM)
    return pl.pallas_call(
        body, name="gather_small",
        in_specs=[vmem] * n, out_specs=[vmem] * n,
        out_shape=[jax.ShapeDtypeStruct((8 * b.shape[0], b.shape[1]), b.dtype) for b in blocks],
        scratch_shapes=[pltpu.SemaphoreType.DMA((7 * n,)), pltpu.SemaphoreType.DMA((7 * n,)),
                        pltpu.SemaphoreType.DMA((n,))],
    )(*blocks)


def _adamw_math(w, g, m, v):
    m = ADAM_B1 * m + (1.0 - ADAM_B1) * g
    v = ADAM_B2 * v + (1.0 - ADAM_B2) * (g * g)
    m_hat = m / (1.0 - ADAM_B1 ** ADAM_STEP)
    v_hat = v / (1.0 - ADAM_B2 ** ADAM_STEP)
    delta = -ADAM_LR * (m_hat / (jnp.sqrt(v_hat) + ADAM_EPS) + ADAM_WD * w)
    return delta, m, v


def _adamw(w, g, m, v, name):
    k, n = w.shape
    tm = k // 4

    def body(w_ref, g_ref, m_ref, v_ref, d_ref, mo_ref, vo_ref):
        d, mm, vv = _adamw_math(w_ref[...], g_ref[...], m_ref[...], v_ref[...])
        d_ref[...] = d
        mo_ref[...] = mm
        vo_ref[...] = vv

    spec = pl.BlockSpec((tm, n), lambda i: (i, 0))
    shp = jax.ShapeDtypeStruct((k, n), F32)
    return pl.pallas_call(
        body, name=name, grid=(4,), in_specs=[spec] * 4, out_specs=[spec] * 3,
        out_shape=[shp] * 3, compiler_params=_params("parallel"),
    )(w, g, m, v)


VEC_SLOTS = {
    "g_mix_norm": (0, 0, D), "b_conv_proj": (0, D, D), "g_ffn_norm": (0, 2 * D, D),
    "g_final": (0, 3 * D, D), "b_in": (1, 0, INW), "conv_b": (2, 0, C), "ln_g": (2, C, C),
    "ln_b": (2, 2 * C, C), "sinks": (2, 3 * C, NQ), "loss": (2, 3 * C + 128, 1),
}
VEC_ROWS, VEC_COLS = 8, 4 * D
CW_ROWS = 32
SMALL_NAMES = ["g_mix_norm", "b_in", "sinks", "conv_w", "conv_b", "ln_g", "ln_b",
               "b_conv_proj", "g_ffn_norm", "g_final"]
CW_LANES = C // N_CHIPS


def _pack_small(gs, loss):
    row0 = jnp.concatenate([gs["g_mix_norm"], gs["b_conv_proj"], gs["g_ffn_norm"], gs["g_final"]], axis=1)
    row1 = jnp.pad(gs["b_in"], ((0, 0), (0, VEC_COLS - INW)))
    row2 = jnp.concatenate([gs["conv_b"], gs["ln_g"], gs["ln_b"],
                            jnp.pad(gs["sinks"], ((0, 0), (0, 128 - NQ))),
                            jnp.pad(loss.reshape(1, 1), ((0, 0), (0, VEC_COLS - 3 * C - 129)))], axis=1)
    vec = jnp.concatenate([row0, row1, row2, jnp.zeros((VEC_ROWS - 3, VEC_COLS), F32)], axis=0)
    cw = jnp.pad(gs["conv_w"], ((0, CW_ROWS - KW), (0, 0)))
    return vec, cw


def _small_update(s_idx, vec_all, cw_all, wmv):
    nsm = len(SMALL_NAMES)

    def body(s_ref, vec_ref, cw_ref, *refs):
        ins = refs[:3 * nsm]
        outs = refs[3 * nsm:7 * nsm]
        loss_ref = refs[7 * nsm]

        def total(slot):
            row, lane, width = slot
            acc = vec_ref[row:row + 1, lane:lane + width]
            for k in range(1, 8):
                acc = acc + vec_ref[k * VEC_ROWS + row:k * VEC_ROWS + row + 1, lane:lane + width]
            return acc

        loss_ref[...] = jnp.broadcast_to(total(VEC_SLOTS["loss"]), loss_ref.shape)
        for p, name in enumerate(SMALL_NAMES):
            w_ref, m_ref, v_ref = ins[3 * p:3 * p + 3]
            g_ref, d_ref, mo_ref, vo_ref = outs[4 * p:4 * p + 4]
            if name == "conv_w":
                g = jnp.zeros((KW, CW_LANES), F32)
                for s in range(N_CHIPS):
                    cand = cw_ref[0:KW, s * CW_LANES:(s + 1) * CW_LANES]
                    for k in range(1, 8):
                        cand = cand + cw_ref[k * CW_ROWS:k * CW_ROWS + KW, s * CW_LANES:(s + 1) * CW_LANES]
                    g = jnp.where(s_ref[0] == s, cand, g)
            else:
                g = total(VEC_SLOTS[name])
            d, mm, vv = _adamw_math(w_ref[...], g, m_ref[...], v_ref[...])
            g_ref[...] = g
            d_ref[...] = d
            mo_ref[...] = mm
            vo_ref[...] = vv

    vmem = pl.BlockSpec(memory_space=pltpu.VMEM)
    flat = [a for t in wmv for a in t]
    out_shape = []
    for w, _, _ in wmv:
        out_shape += [jax.ShapeDtypeStruct(w.shape, F32)] * 4
    out_shape.append(jax.ShapeDtypeStruct((1, 128), F32))
    res = pl.pallas_call(
        body, name="small_update",
        in_specs=[pl.BlockSpec(memory_space=pltpu.SMEM)] + [vmem] * (2 + len(flat)),
        out_specs=[vmem] * len(out_shape), out_shape=out_shape,
    )(s_idx, vec_all, cw_all, *flat)
    return [tuple(res[4 * p:4 * p + 4]) for p in range(nsm)], res[4 * nsm]


WEIGHT_ORDER = ["g_mix_norm", "w_in", "b_in", "sinks", "conv_w", "conv_b", "ln_g", "ln_b",
                "w_attn_proj", "w_conv_proj", "b_conv_proj", "w_out", "g_ffn_norm", "w_ffn_in",
                "w_ffn_down", "g_final"]


def kernel(x, g_mix_norm, w_in, b_in, sinks, conv_w, conv_b, ln_g, ln_b, w_attn_proj, w_conv_proj, b_conv_proj, w_out, g_ffn_norm, w_ffn_in, w_ffn_down, g_final, loss_target, m_g_mix_norm, m_w_in, m_b_in, m_sinks, m_conv_w, m_conv_b, m_ln_g, m_ln_b, m_w_attn_proj, m_w_conv_proj, m_b_conv_proj, m_w_out, m_g_ffn_norm, m_w_ffn_in, m_w_ffn_down, m_g_final, v_g_mix_norm, v_w_in, v_b_in, v_sinks, v_conv_w, v_conv_b, v_ln_g, v_ln_b, v_w_attn_proj, v_w_conv_proj, v_b_conv_proj, v_w_out, v_g_ffn_norm, v_w_ffn_in, v_w_ffn_down, v_g_final):
    w = dict(g_mix_norm=g_mix_norm, w_in=w_in, b_in=b_in, sinks=sinks, conv_w=conv_w, conv_b=conv_b,
             ln_g=ln_g, ln_b=ln_b, w_attn_proj=w_attn_proj, w_conv_proj=w_conv_proj,
             b_conv_proj=b_conv_proj, w_out=w_out, g_ffn_norm=g_ffn_norm, w_ffn_in=w_ffn_in,
             w_ffn_down=w_ffn_down, g_final=g_final)
    m = dict(g_mix_norm=m_g_mix_norm, w_in=m_w_in, b_in=m_b_in, sinks=m_sinks, conv_w=m_conv_w,
             conv_b=m_conv_b, ln_g=m_ln_g, ln_b=m_ln_b, w_attn_proj=m_w_attn_proj,
             w_conv_proj=m_w_conv_proj, b_conv_proj=m_b_conv_proj, w_out=m_w_out,
             g_ffn_norm=m_g_ffn_norm, w_ffn_in=m_w_ffn_in, w_ffn_down=m_w_ffn_down, g_final=m_g_final)
    v = dict(g_mix_norm=v_g_mix_norm, w_in=v_w_in, b_in=v_b_in, sinks=v_sinks, conv_w=v_conv_w,
             conv_b=v_conv_b, ln_g=v_ln_g, ln_b=v_ln_b, w_attn_proj=v_w_attn_proj,
             w_conv_proj=v_w_conv_proj, b_conv_proj=v_b_conv_proj, w_out=v_w_out,
             g_ffn_norm=v_g_ffn_norm, w_ffn_in=v_w_ffn_in, w_ffn_down=v_w_ffn_down, g_final=v_g_final)

    c_idx = lax.axis_index("c").astype(jnp.int32).reshape(1)
    s_idx = (2 * lax.axis_index("x") + lax.axis_index("y")).astype(jnp.int32).reshape(1)

    sc_idx = jnp.concatenate([s_idx, c_idx])
    out_g, out_d, out_m, out_v = {}, {}, {}, {}

    def gather_start(tag, shards):
        lands = [lax.empty((N_CHIPS,) + s.shape, s.dtype) for s in shards]
        return _chip_start("gather_start_" + tag, _gather_ends, GATHER_PEERS, shards, lands)

    def gather_finish(tag, state, after):
        send_sems, recv_sems, shards, lands, _ = state
        shards, lands = _chip_wait("gather_wait_" + tag, _gather_ends, send_sems, recv_sems, shards, lands, after)
        return _pair_forward("pair_forward_" + tag, lands)

    names_b = ["w_attn_proj", "w_conv_proj", "w_out", "w_ffn_in", "w_ffn_down"]
    state_a = gather_start("a", [w["w_in"][0].astype(BF16), jnp.pad(conv_w[0], ((0, CW_ROWS - KW), (0, 0)))])
    state_b = gather_start("b", [w[name][0].astype(BF16) for name in names_b])
    got_a = gather_finish("a", state_a, [state_b[4]])
    w_in_full = got_a[0].transpose(1, 0, 2).reshape(D, INW)
    conv_w_full = got_a[1].transpose(1, 0, 2).reshape(CW_ROWS, C)[:KW]

    xs, target = x[0], loss_target[0]
    g_final2 = g_final.reshape(1, D)
    h, qkv, glu, gl = _in_proj(xs, g_mix_norm, w_in_full, b_in)
    o, lse = _attn_fwd(qkv, sinks)
    u, cact = _conv_fwd(glu, conv_w_full, conv_b, ln_g, ln_b)
    w_ap4, w_cp4, w_out4, w_fi4, w_dn4 = gather_finish("b", state_b, [o, cact])
    w_out_full, w_dn_full = w_out4.reshape(D, D), w_dn4.reshape(DFF, D)
    ya, yc, mg, x1 = _mix_out(xs, o, cact, gl, w_ap4, w_cp4, b_conv_proj, w_out_full)
    h2, gu, act = _ffn_in(x1, g_ffn_norm, w_fi4)
    dx2, dx2b, dg_final, loss_part = _ffn_out_loss(x1, act, w_dn_full, g_final2, target)

    def reduce_start(tag, names, grads):
        from_sibling = _pair_exchange("pair_exchange_" + tag, grads)
        pair = [_pair_sum(c_idx, g, r, "pair_sum_" + name) for name, g, r in zip(names, grads, from_sibling)]
        lands = [lax.empty((3,) + p.shape[1:], p.dtype) for p in pair]
        return _chip_start("chip_start_" + tag, _reduce_ends, REDUCE_PEERS, pair, lands)

    def reduce_finish(tag, names, state, after):
        send_sems, recv_sems, pair, lands, _ = state
        pair, lands = _chip_wait("chip_wait_" + tag, _reduce_ends, send_sems, recv_sems, pair, lands, after)
        halves = [_chip_sum(sc_idx, p, r, "chip_sum_" + name) for name, p, r in zip(names, pair, lands)]
        for name, both in zip(names, _pair_share("pair_share_" + tag, halves)):
            g = both.reshape(w[name].shape[1:])
            d, mm, vv = _adamw(w[name][0], g, m[name][0], v[name][0], "adamw_" + name)
            out_g[name], out_d[name], out_m[name], out_v[name] = g[None], d[None], mm[None], vv[None]

    dgu, dx1, dx1b, dg_ffn = _ffn_bwd(dx2, dx2b, gu, x1, g_ffn_norm, w_dn_full, w_fi4)
    names_1 = ["w_ffn_in", "w_ffn_down"]
    state_1 = reduce_start("1", names_1, [_grad_w(h2, dgu, "grad_w_ffn_in", 512, FSH, True),
                                          _grad_w(act, dx2b, "grad_w_ffn_down", 256, 512, False)])
    dya, dyc, dgl, do, dc, db_cp = _mix_bwd(dx1b, gl, ya, yc, w_out_full, w_ap4, w_cp4, state_1[4])
    names_2 = ["w_out", "w_attn_proj", "w_conv_proj"]
    state_2 = reduce_start("2", names_2, [_grad_w(mg, dx1b, "grad_w_out", 512, 512, False),
                                          _grad_w(o, dya, "grad_w_attn_proj", 512, 256, True),
                                          _grad_w(cact, dyc, "grad_w_conv_proj", 512, 256, True)])
    dglu, dconv_w, dconv_b, dln_g, dln_b = _conv_bwd(glu, u, dc, conv_w_full, ln_g, ln_b, state_2[4])
    dq, dkv, dsinks = _attn_bwd(qkv, o, do, lse, sinks)
    grad_x, dg_mix, db_in = _in_proj_bwd(dq, dkv, dglu, dgl, xs, dx1, g_mix_norm, w_in_full)
    names_3 = ["w_in"]
    gw_in = _grad_w_in(h, dq, dkv, dglu, dgl)
    state_3 = reduce_start("3", names_3, [gw_in.reshape(D, N_CHIPS, INW // N_CHIPS).transpose(1, 0, 2)])

    reduce_finish("1", names_1, state_1, [state_3[4]])
    reduce_finish("2", names_2, state_2, [out_d["w_ffn_down"]])

    gs = {"g_mix_norm": dg_mix, "b_in": db_in, "sinks": dsinks[:, 0].reshape(1, NQ),
          "conv_w": dconv_w, "conv_b": dconv_b, "ln_g": dln_g, "ln_b": dln_b,
          "b_conv_proj": db_cp, "g_ffn_norm": dg_ffn, "g_final": dg_final}
    vec, cw = _pack_small(gs, loss_part[0, 0])
    vec_all, cw_all = _gather_small([vec, cw])

    def view(a, name):
        if name == "conv_w":
            return a[0]
        if name == "g_final":
            return a.reshape(1, D)
        return a

    wmv = [(view(w[name], name), view(m[name], name), view(v[name], name)) for name in SMALL_NAMES]
    small_out, loss_row = _small_update(s_idx, vec_all, cw_all, wmv)
    for name, (g, d, mm, vv) in zip(SMALL_NAMES, small_out):
        shape = w[name].shape
        out_g[name], out_d[name], out_m[name], out_v[name] = (
            g.reshape(shape), d.reshape(shape), mm.reshape(shape), vv.reshape(shape))

    reduce_finish("3", names_3, state_3, [loss_row, out_d["w_conv_proj"]])

    loss = loss_row[0, 0]
    return (loss, grad_x[None], *[out_g[k] for k in WEIGHT_ORDER], *[out_d[k] for k in WEIGHT_ORDER],
            *[out_m[k] for k in WEIGHT_ORDER], *[out_v[k] for k in WEIGHT_ORDER])
```

```python
import functools

import jax
import jax.numpy as jnp
from jax import lax
from jax.experimental import pallas as pl
from jax.experimental.pallas import tpu as pltpu

F32 = jnp.float32
BF16 = jnp.bfloat16

T = 2048
D = 1024
HD = 64
NQ = 8
NKV = 2
GROUP = NQ // NKV
BLK = 128
AW = NQ * HD
KVW = NKV * HD
C = 512
KW = 31
QKVW = AW + 2 * KVW
GLU_OFF = QKVW
GATE_OFF = GLU_OFF + 2 * C
INW = GATE_OFF + 2 * D
DFF = 2816
EPS = 1e-5
NEG = -1e30
SCALE = HD ** -0.5
HALO = 32
N_CHIPS = 4
FSH = 2 * DFF // N_CHIPS

ADAM_LR = 0.001
ADAM_B1 = 0.9
ADAM_B2 = 0.999
ADAM_EPS = 1e-08
ADAM_WD = 0.01
ADAM_STEP = 10

VMEM_LIMIT = 56 * 1024 * 1024
MESH = pl.DeviceIdType.MESH


def _params(*sem):
    return pltpu.CompilerParams(dimension_semantics=sem, vmem_limit_bytes=VMEM_LIMIT)


def _dot(a, b):
    return jnp.dot(a, b, preferred_element_type=F32)


def _dot_nt(a, b):
    return lax.dot_general(a, b, (((1,), (1,)), ((), ())), preferred_element_type=F32)


def _dot_tn(a, b):
    return lax.dot_general(a, b, (((0,), (0,)), ((), ())), preferred_element_type=F32)


def _sigmoid(v):
    return 1.0 / (1.0 + jnp.exp(-v))


def _rows(tm, n):
    return pl.BlockSpec((tm, n), lambda i: (i, 0))


def _whole(shape):
    return pl.BlockSpec(shape, lambda i: tuple(0 for _ in shape))


def _in_proj(x, g_mix, w_in, b_in):
    tm = 256

    def body(x_ref, g_ref, w_ref, b_ref, h_ref, qkv_ref, glu_ref, gl_ref):
        xv = x_ref[...]
        r = lax.rsqrt(jnp.mean(xv * xv, axis=-1, keepdims=True) + EPS)
        h = (xv * r * g_ref[...]).astype(BF16)
        h_ref[...] = h
        qkv_ref[...] = (_dot(h, w_ref[:, 0:GLU_OFF]) + b_ref[:, 0:GLU_OFF]).astype(BF16)
        glu_ref[...] = (_dot(h, w_ref[:, GLU_OFF:GATE_OFF]) + b_ref[:, GLU_OFF:GATE_OFF]).astype(BF16)
        gl_ref[...] = (_dot(h, w_ref[:, GATE_OFF:INW]) + b_ref[:, GATE_OFF:INW]).astype(BF16)

    return pl.pallas_call(
        body, name="in_proj", grid=(T // tm,),
        in_specs=[_rows(tm, D), _whole((1, D)), _whole((D, INW)), _whole((1, INW))],
        out_specs=[_rows(tm, D), _rows(tm, QKVW), _rows(tm, 2 * C), _rows(tm, 2 * D)],
        out_shape=[jax.ShapeDtypeStruct((T, D), BF16), jax.ShapeDtypeStruct((T, QKVW), BF16),
                   jax.ShapeDtypeStruct((T, 2 * C), BF16), jax.ShapeDtypeStruct((T, 2 * D), BF16)],
        compiler_params=_params("parallel"),
    )(x, g_mix, w_in, b_in)


GROWS = GROUP * BLK
BAND = 2 * BLK


def _band(i):
    rb = pl.multiple_of(jnp.maximum(i - 1, 0) * BLK, BLK)
    row = lax.broadcasted_iota(jnp.int32, (GROWS, BAND), 0)
    kpos = rb + lax.broadcasted_iota(jnp.int32, (GROWS, BAND), 1)
    qpos = i * BLK + jnp.bitwise_and(row, BLK - 1)
    return rb, jnp.logical_and(kpos <= qpos, kpos > qpos - BLK)


def _sink_column(sink_ref, g):
    head = lax.shift_right_logical(lax.broadcasted_iota(jnp.int32, (GROWS, 1), 0), 7)
    col = jnp.full((GROWS, 1), sink_ref[0, g * GROUP], F32)
    for hh in range(1, GROUP):
        col = jnp.where(head == hh, sink_ref[0, g * GROUP + hh], col)
    return col


def _attn_fwd(qkv, sinks):
    def body(sink_ref, qkv_ref, o_ref, lse_ref, s_ref, p_ref):
        i = pl.program_id(0)
        r0 = pl.multiple_of(i * BLK, BLK)
        rb, valid = _band(i)
        for g in range(NKV):
            kband = qkv_ref[pl.ds(rb, BAND), AW + g * HD:AW + (g + 1) * HD]
            vband = qkv_ref[pl.ds(rb, BAND), AW + KVW + g * HD:AW + KVW + (g + 1) * HD]
            for hh in range(GROUP):
                h = g * GROUP + hh
                s_ref[hh * BLK:(hh + 1) * BLK, :] = _dot_nt(qkv_ref[pl.ds(r0, BLK), h * HD:(h + 1) * HD], kband)
            s = jnp.where(valid, s_ref[...] * SCALE, NEG)
            sink = _sink_column(sink_ref, g)
            m = jnp.maximum(jnp.max(s, axis=-1, keepdims=True), sink)
            p = jnp.exp(s - m)
            den = jnp.sum(p, axis=-1, keepdims=True) + jnp.exp(sink - m)
            p_ref[...] = (p * (1.0 / den)).astype(BF16)
            lse = m + jnp.log(den)
            for hh in range(GROUP):
                h = g * GROUP + hh
                o_ref[:, h * HD:(h + 1) * HD] = _dot(p_ref[hh * BLK:(hh + 1) * BLK, :], vband).astype(BF16)
                lse_ref[:, h:h + 1] = lse[hh * BLK:(hh + 1) * BLK]

    return pl.pallas_call(
        body, name="attn_fwd", grid=(T // BLK,),
        in_specs=[pl.BlockSpec(memory_space=pltpu.SMEM), _whole((T, QKVW))],
        out_specs=[_rows(BLK, AW), _rows(BLK, NQ)],
        out_shape=[jax.ShapeDtypeStruct((T, AW), BF16), jax.ShapeDtypeStruct((T, NQ), F32)],
        scratch_shapes=[pltpu.VMEM((GROWS, BAND), F32), pltpu.VMEM((GROWS, BAND), BF16)],
        compiler_params=_params("parallel"),
    )(sinks, qkv)


CONV_TM = 256
CONV_SUB = 32


def _glu(ab):
    a = ab[:, 0:C].astype(F32)
    b = ab[:, C:2 * C].astype(F32)
    return a * _sigmoid(b)


SUBLANES = 8


def _shifted_copies(ref):
    rows = ref.shape[1] - SUBLANES
    for r in range(1, SUBLANES):
        ref[r, 0:rows, :] = ref[0, r:r + rows, :]


def _shifted_rows(ref, start, size):
    r = start % SUBLANES
    return ref[r, start - r:start - r + size, :]


def _conv_fwd(glu, conv_w, conv_b, ln_g, ln_b):
    tm = CONV_TM

    def body(cur_ref, prev_ref, w_ref, cb_ref, g_ref, b_ref, u_ref, c_ref, zs_ref):
        i = pl.program_id(0)
        zprev = _glu(prev_ref[tm - HALO:tm, :])
        zs_ref[0, 0:HALO, :] = jnp.where(i > 0, zprev, 0.0)
        zs_ref[0, HALO:HALO + tm, :] = _glu(cur_ref[...])
        _shifted_copies(zs_ref)
        for s in range(tm // CONV_SUB):
            base = HALO + s * CONV_SUB - (KW - 1)
            acc = jnp.broadcast_to(cb_ref[...], (CONV_SUB, C))
            for j in range(KW):
                acc = acc + w_ref[j:j + 1, :] * _shifted_rows(zs_ref, base + j, CONV_SUB)
            rows = slice(s * CONV_SUB, (s + 1) * CONV_SUB)
            u_ref[rows, :] = acc
            mu = jnp.mean(acc, axis=-1, keepdims=True)
            xc = acc - mu
            var = jnp.mean(xc * xc, axis=-1, keepdims=True)
            y = xc * lax.rsqrt(var + EPS) * g_ref[...] + b_ref[...]
            c_ref[rows, :] = (y * _sigmoid(y)).astype(BF16)

    return pl.pallas_call(
        body, name="conv_fwd", grid=(T // tm,),
        in_specs=[_rows(tm, 2 * C),
                  pl.BlockSpec((tm, 2 * C), lambda i: (jnp.maximum(i - 1, 0), 0)),
                  _whole((KW, C)), _whole((1, C)), _whole((1, C)), _whole((1, C))],
        out_specs=[_rows(tm, C), _rows(tm, C)],
        out_shape=[jax.ShapeDtypeStruct((T, C), F32), jax.ShapeDtypeStruct((T, C), BF16)],
        scratch_shapes=[pltpu.VMEM((SUBLANES, HALO + tm, C), F32)],
        compiler_params=_params("parallel"),
    )(glu, glu, conv_w, conv_b, ln_g, ln_b)


def _mix_out(x, o, cact, gl, w_ap, w_cp, b_cp, w_out):
    tm = 256

    def body(x_ref, o_ref, c_ref, gl_ref, wap_ref, wcp_ref, bcp_ref, wo_ref,
             ya_ref, yc_ref, mg_ref, x1_ref):
        ov, cv = o_ref[...], c_ref[...]
        ya = jnp.concatenate([_dot(ov, wap_ref[s]) for s in range(N_CHIPS)], axis=1)
        yc = jnp.concatenate([_dot(cv, wcp_ref[s]) for s in range(N_CHIPS)], axis=1) + bcp_ref[...]
        g0 = _sigmoid(gl_ref[:, 0:D].astype(F32))
        g1 = _sigmoid(gl_ref[:, D:2 * D].astype(F32))
        mg = (g0 * ya + g1 * yc).astype(BF16)
        ya_ref[...] = ya.astype(BF16)
        yc_ref[...] = yc.astype(BF16)
        mg_ref[...] = mg
        x1_ref[...] = x_ref[...] + _dot(mg, wo_ref[...])

    return pl.pallas_call(
        body, name="mix_out", grid=(T // tm,),
        in_specs=[_rows(tm, D), _rows(tm, AW), _rows(tm, C), _rows(tm, 2 * D),
                  _whole((N_CHIPS, AW, D // N_CHIPS)), _whole((N_CHIPS, C, D // N_CHIPS)), _whole((1, D)),
                  _whole((D, D))],
        out_specs=[_rows(tm, D), _rows(tm, D), _rows(tm, D), _rows(tm, D)],
        out_shape=[jax.ShapeDtypeStruct((T, D), BF16), jax.ShapeDtypeStruct((T, D), BF16),
                   jax.ShapeDtypeStruct((T, D), BF16), jax.ShapeDtypeStruct((T, D), F32)],
        compiler_params=_params("parallel"),
    )(x, o, cact, gl, w_ap, w_cp, b_cp, w_out)


def _ffn_in(x1, g_ffn, w_fi):
    tm = 256

    def body(x_ref, g_ref, w_ref, h_ref, gu_ref, act_ref):
        xv = x_ref[...]
        r = lax.rsqrt(jnp.mean(xv * xv, axis=-1, keepdims=True) + EPS)
        h = (xv * r * g_ref[...]).astype(BF16)
        h_ref[...] = h
        for s in range(N_CHIPS // 2):
            c0 = s * FSH
            gate = _dot(h, w_ref[s])
            up = _dot(h, w_ref[s + N_CHIPS // 2])
            gu_ref[:, c0:c0 + FSH] = gate.astype(BF16)
            gu_ref[:, DFF + c0:DFF + c0 + FSH] = up.astype(BF16)
            act_ref[:, c0:c0 + FSH] = (gate * _sigmoid(gate) * up).astype(BF16)

    return pl.pallas_call(
        body, name="ffn_in", grid=(T // tm,),
        in_specs=[_rows(tm, D), _whole((1, D)), _whole((N_CHIPS, D, FSH))],
        out_specs=[_rows(tm, D), _rows(tm, 2 * DFF), _rows(tm, DFF)],
        out_shape=[jax.ShapeDtypeStruct((T, D), BF16), jax.ShapeDtypeStruct((T, 2 * DFF), BF16),
                   jax.ShapeDtypeStruct((T, DFF), BF16)],
        compiler_params=_params("parallel"),
    )(x1, g_ffn, w_fi)


def _ffn_out_loss(x1, act, w_dn, g_final, target):
    tm = 256

    def body(x_ref, a_ref, w_ref, g_ref, t_ref, dx_ref, dxb_ref, dg_ref, loss_ref):
        i = pl.program_id(0)
        x2 = x_ref[...] + _dot(a_ref[...], w_ref[...])
        r = lax.rsqrt(jnp.mean(x2 * x2, axis=-1, keepdims=True) + EPS)
        xh = x2 * r
        g = g_ref[...]
        err = xh * g - t_ref[...]
        dy = err * (1.0 / D)
        dyg = dy * g
        dx = r * (dyg - xh * jnp.mean(dyg * xh, axis=-1, keepdims=True))
        dx_ref[...] = dx
        dxb_ref[...] = dx.astype(BF16)
        part = 0.5 * jnp.sum(jnp.mean(err * err, axis=-1, keepdims=True), axis=0, keepdims=True)

        @pl.when(i == 0)
        def _():
            dg_ref[...] = jnp.zeros_like(dg_ref)
            loss_ref[...] = jnp.zeros_like(loss_ref)

        dg_ref[...] += jnp.sum(dy * xh, axis=0, keepdims=True)
        loss_ref[...] += jnp.broadcast_to(part, loss_ref.shape)

    return pl.pallas_call(
        body, name="ffn_out_loss", grid=(T // tm,),
        in_specs=[_rows(tm, D), _rows(tm, DFF), _whole((DFF, D)), _whole((1, D)), _rows(tm, D)],
        out_specs=[_rows(tm, D), _rows(tm, D), _whole((1, D)), _whole((1, 128))],
        out_shape=[jax.ShapeDtypeStruct((T, D), F32), jax.ShapeDtypeStruct((T, D), BF16),
                   jax.ShapeDtypeStruct((1, D), F32), jax.ShapeDtypeStruct((1, 128), F32)],
        compiler_params=_params("arbitrary"),
    )(x1, act, w_dn, g_final, target)


def _const(shape):
    return pl.BlockSpec(shape, lambda i: tuple(0 for _ in shape), pipeline_mode=pl.Buffered(1))


def _ffn_bwd(dx2, dx2b, gu, x1, g_ffn, w_dn_t, w_fi_t):
    tm = 256

    def body(dx_ref, dxb_ref, gu_ref, x_ref, g_ref, wdn_ref, wfi_ref,
             dgu_ref, dx1_ref, dx1b_ref, dg_ref):
        i = pl.program_id(0)
        dxb = dxb_ref[...]
        dh = jnp.zeros((tm, D), F32)
        for k in range(N_CHIPS // 2):
            c0 = k * FSH
            dact = _dot_nt(dxb, wdn_ref[c0:c0 + FSH, :])
            gate = gu_ref[:, c0:c0 + FSH].astype(F32)
            up = gu_ref[:, DFF + c0:DFF + c0 + FSH].astype(F32)
            s = _sigmoid(gate)
            dup = (dact * gate * s).astype(BF16)
            dgate = (dact * up * s * (1.0 + gate * (1.0 - s))).astype(BF16)
            dgu_ref[:, c0:c0 + FSH] = dgate
            dgu_ref[:, DFF + c0:DFF + c0 + FSH] = dup
            dh = dh + _dot_nt(dgate, wfi_ref[k]) + _dot_nt(dup, wfi_ref[k + N_CHIPS // 2])
        xv = x_ref[...]
        r = lax.rsqrt(jnp.mean(xv * xv, axis=-1, keepdims=True) + EPS)
        xh = xv * r
        dhg = dh * g_ref[...]
        dx1 = dx_ref[...] + r * (dhg - xh * jnp.mean(dhg * xh, axis=-1, keepdims=True))
        dx1_ref[...] = dx1
        dx1b_ref[...] = dx1.astype(BF16)

        @pl.when(i == 0)
        def _():
            dg_ref[...] = jnp.zeros_like(dg_ref)

        dg_ref[...] += jnp.sum(dh * xh, axis=0, keepdims=True)

    return pl.pallas_call(
        body, name="ffn_bwd", grid=(T // tm,),
        in_specs=[_rows(tm, D), _rows(tm, D), _rows(tm, 2 * DFF), _rows(tm, D), _whole((1, D)),
                  _const((DFF, D)), _const((N_CHIPS, D, FSH))],
        out_specs=[_rows(tm, 2 * DFF), _rows(tm, D), _rows(tm, D), _whole((1, D))],
        out_shape=[jax.ShapeDtypeStruct((T, 2 * DFF), BF16), jax.ShapeDtypeStruct((T, D), F32),
                   jax.ShapeDtypeStruct((T, D), BF16), jax.ShapeDtypeStruct((1, D), F32)],
        compiler_params=_params("arbitrary"),
    )(dx2, dx2b, gu, x1, g_ffn, w_dn_t, w_fi_t)


def _mix_bwd(dx1b, gl, ya, yc, w_out_t, w_ap_t, w_cp_t, dep):
    tm = 256

    def body(dx_ref, gl_ref, ya_ref, yc_ref, wo_ref, wap_ref, wcp_ref, dep_ref,
             dya_ref, dyc_ref, dgl_ref, do_ref, dc_ref, db_ref):
        i = pl.program_id(0)
        dm = _dot_nt(dx_ref[...], wo_ref[...])
        g0 = _sigmoid(gl_ref[:, 0:D].astype(F32))
        g1 = _sigmoid(gl_ref[:, D:2 * D].astype(F32))
        dya = dm * g0
        dyc = dm * g1
        dgl_ref[:, 0:D] = (dya * ya_ref[...].astype(F32) * (1.0 - g0)).astype(BF16)
        dgl_ref[:, D:2 * D] = (dyc * yc_ref[...].astype(F32) * (1.0 - g1)).astype(BF16)
        dyab = dya.astype(BF16)
        dycb = dyc.astype(BF16)
        dya_ref[...] = dyab
        dyc_ref[...] = dycb
        sw = D // N_CHIPS
        do = jnp.zeros((tm, AW), F32)
        dcv = jnp.zeros((tm, C), F32)
        for s in range(N_CHIPS):
            do = do + _dot_nt(dyab[:, s * sw:(s + 1) * sw], wap_ref[s])
            dcv = dcv + _dot_nt(dycb[:, s * sw:(s + 1) * sw], wcp_ref[s])
        do_ref[...] = do.astype(BF16)
        dc_ref[...] = dcv

        @pl.when(i == 0)
        def _():
            db_ref[...] = jnp.zeros_like(db_ref)

        db_ref[...] += jnp.sum(dyc, axis=0, keepdims=True)

    return pl.pallas_call(
        body, name="mix_bwd", grid=(T // tm,),
        in_specs=[_rows(tm, D), _rows(tm, 2 * D), _rows(tm, D), _rows(tm, D),
                  _whole((D, D)), _whole((N_CHIPS, AW, D // N_CHIPS)), _whole((N_CHIPS, C, D // N_CHIPS)),
                  _whole((8, 128))],
        out_specs=[_rows(tm, D), _rows(tm, D), _rows(tm, 2 * D), _rows(tm, AW), _rows(tm, C),
                   _whole((1, D))],
        out_shape=[jax.ShapeDtypeStruct((T, D), BF16), jax.ShapeDtypeStruct((T, D), BF16),
                   jax.ShapeDtypeStruct((T, 2 * D), BF16), jax.ShapeDtypeStruct((T, AW), BF16),
                   jax.ShapeDtypeStruct((T, C), F32), jax.ShapeDtypeStruct((1, D), F32)],
        compiler_params=_params("arbitrary"),
    )(dx1b, gl, ya, yc, w_out_t, w_ap_t, w_cp_t, dep)


def _conv_bwd(glu, u, dc, conv_w, ln_g, ln_b, dep):
    tm = CONV_TM
    nblk = T // tm

    def du_of(uv, dcv, g_ref, b_ref):
        mu = jnp.mean(uv, axis=-1, keepdims=True)
        xc = uv - mu
        var = jnp.mean(xc * xc, axis=-1, keepdims=True)
        rstd = lax.rsqrt(var + EPS)
        xh = xc * rstd
        y = xh * g_ref[...] + b_ref[...]
        sg = _sigmoid(y)
        dy = dcv * (sg * (1.0 + y * (1.0 - sg)))
        dxh = dy * g_ref[...]
        du = rstd * (dxh - jnp.mean(dxh, axis=-1, keepdims=True)
                     - xh * jnp.mean(dxh * xh, axis=-1, keepdims=True))
        return du, dy, xh

    def body(cur_ref, prev_ref, u_ref, un_ref, dc_ref, dcn_ref, w_ref, g_ref, b_ref, dep_ref,
             dglu_ref, dw_ref, dcb_ref, dg_ref, db_ref, zs_ref, dus_ref):
        i = pl.program_id(0)

        @pl.when(i == 0)
        def _():
            dw_ref[...] = jnp.zeros_like(dw_ref)
            dcb_ref[...] = jnp.zeros_like(dcb_ref)
            dg_ref[...] = jnp.zeros_like(dg_ref)
            db_ref[...] = jnp.zeros_like(db_ref)

        zprev = _glu(prev_ref[tm - HALO:tm, :])
        zs_ref[0, 0:HALO, :] = jnp.where(i > 0, zprev, 0.0)
        zs_ref[0, HALO:HALO + tm, :] = _glu(cur_ref[...])
        _shifted_copies(zs_ref)

        dun, _, _ = du_of(un_ref[0:HALO, :], dcn_ref[0:HALO, :], g_ref, b_ref)
        dus_ref[0, tm:tm + HALO, :] = jnp.where(i < nblk - 1, dun, 0.0)
        dg_acc = jnp.zeros((1, C), F32)
        db_acc = jnp.zeros((1, C), F32)
        dcb_acc = jnp.zeros((1, C), F32)
        for s in range(tm // CONV_SUB):
            rows = slice(s * CONV_SUB, (s + 1) * CONV_SUB)
            du, dy, xh = du_of(u_ref[rows, :], dc_ref[rows, :], g_ref, b_ref)
            dus_ref[0, rows, :] = du
            dg_acc = dg_acc + jnp.sum(dy * xh, axis=0, keepdims=True)
            db_acc = db_acc + jnp.sum(dy, axis=0, keepdims=True)
            dcb_acc = dcb_acc + jnp.sum(du, axis=0, keepdims=True)
        dg_ref[...] += dg_acc
        db_ref[...] += db_acc
        dcb_ref[...] += dcb_acc
        _shifted_copies(dus_ref)

        for j in range(KW):
            acc = jnp.zeros((CONV_SUB, C), F32)
            for s in range(tm // CONV_SUB):
                base = HALO + s * CONV_SUB - (KW - 1) + j
                acc = acc + dus_ref[0, s * CONV_SUB:(s + 1) * CONV_SUB, :] * _shifted_rows(zs_ref, base, CONV_SUB)
            dw_ref[j:j + 1, :] += jnp.sum(acc, axis=0, keepdims=True)

        for s in range(tm // CONV_SUB):
            rows = slice(s * CONV_SUB, (s + 1) * CONV_SUB)
            dz = jnp.zeros((CONV_SUB, C), F32)
            for j in range(KW):
                dz = dz + w_ref[j:j + 1, :] * _shifted_rows(dus_ref, s * CONV_SUB + (KW - 1) - j, CONV_SUB)
            a = cur_ref[rows, 0:C].astype(F32)
            sb = _sigmoid(cur_ref[rows, C:2 * C].astype(F32))
            dglu_ref[rows, 0:C] = (dz * sb).astype(BF16)
            dglu_ref[rows, C:2 * C] = (dz * a * sb * (1.0 - sb)).astype(BF16)

    nxt = lambda i: (jnp.minimum(i + 1, nblk - 1), 0)
    return pl.pallas_call(
        body, name="conv_bwd", grid=(nblk,),
        in_specs=[_rows(tm, 2 * C),
                  pl.BlockSpec((tm, 2 * C), lambda i: (jnp.maximum(i - 1, 0), 0)),
                  _rows(tm, C), pl.BlockSpec((tm, C), nxt),
                  _rows(tm, C), pl.BlockSpec((tm, C), nxt),
                  _whole((KW, C)), _whole((1, C)), _whole((1, C)), _whole((8, 128))],
        out_specs=[_rows(tm, 2 * C), _whole((KW, C)), _whole((1, C)), _whole((1, C)), _whole((1, C))],
        out_shape=[jax.ShapeDtypeStruct((T, 2 * C), BF16), jax.ShapeDtypeStruct((KW, C), F32),
                   jax.ShapeDtypeStruct((1, C), F32), jax.ShapeDtypeStruct((1, C), F32),
                   jax.ShapeDtypeStruct((1, C), F32)],
        scratch_shapes=[pltpu.VMEM((SUBLANES, HALO + tm, C), F32), pltpu.VMEM((SUBLANES, tm + HALO, C), F32)],
        compiler_params=_params("arbitrary"),
    )(glu, glu, u, u, dc, dc, conv_w, ln_g, ln_b, dep)


def _attn_bwd(qkv, o, do, lse, sinks):
    def body(sink_ref, qkv_ref, o_ref, do_ref, lse_ref, dq_ref, dkv_ref, ds_ref,
             s_ref, dp_ref, p_ref, dsb_ref):
        i = pl.program_id(0)

        @pl.when(i == 0)
        def _():
            dkv_ref[...] = jnp.zeros_like(dkv_ref)
            ds_ref[...] = jnp.zeros_like(ds_ref)

        r0 = pl.multiple_of(i * BLK, BLK)
        rb, valid = _band(i)
        for g in range(NKV):
            kband = qkv_ref[pl.ds(rb, BAND), AW + g * HD:AW + (g + 1) * HD]
            vband = qkv_ref[pl.ds(rb, BAND), AW + KVW + g * HD:AW + KVW + (g + 1) * HD]
            lse_parts, dl_parts = [], []
            for hh in range(GROUP):
                h = g * GROUP + hh
                hcol = slice(h * HD, (h + 1) * HD)
                doh = do_ref[:, hcol]
                s_ref[hh * BLK:(hh + 1) * BLK, :] = _dot_nt(qkv_ref[pl.ds(r0, BLK), hcol], kband)
                dp_ref[hh * BLK:(hh + 1) * BLK, :] = _dot_nt(doh, vband)
                lse_parts.append(lse_ref[:, h:h + 1])
                dl_parts.append(jnp.sum(doh.astype(F32) * o_ref[:, hcol].astype(F32), axis=-1, keepdims=True))
            lse = jnp.concatenate(lse_parts, axis=0)
            dl = jnp.concatenate(dl_parts, axis=0)
            p = jnp.where(valid, jnp.exp(s_ref[...] * SCALE - lse), 0.0)
            p_ref[...] = p.astype(BF16)
            dsb_ref[...] = (p * (dp_ref[...] - dl)).astype(BF16)
            dsink = -(jnp.exp(_sink_column(sink_ref, g) - lse) * dl)
            dk = jnp.zeros((BAND, HD), F32)
            dv = jnp.zeros((BAND, HD), F32)
            for hh in range(GROUP):
                h = g * GROUP + hh
                hcol = slice(h * HD, (h + 1) * HD)
                rows = slice(hh * BLK, (hh + 1) * BLK)
                dq_ref[:, hcol] = (_dot(dsb_ref[rows, :], kband) * SCALE).astype(BF16)
                dk = dk + _dot_tn(dsb_ref[rows, :], qkv_ref[pl.ds(r0, BLK), hcol])
                dv = dv + _dot_tn(p_ref[rows, :], do_ref[:, hcol])
                ds_ref[h:h + 1, :] += jnp.broadcast_to(jnp.sum(dsink[rows], axis=0, keepdims=True), (1, 128))
            dkv_ref[pl.ds(rb, BAND), g * HD:(g + 1) * HD] += dk * SCALE
            dkv_ref[pl.ds(rb, BAND), KVW + g * HD:KVW + (g + 1) * HD] += dv

    return pl.pallas_call(
        body, name="attn_bwd", grid=(T // BLK,),
        in_specs=[pl.BlockSpec(memory_space=pltpu.SMEM), _whole((T, QKVW)),
                  _rows(BLK, AW), _rows(BLK, AW), _rows(BLK, NQ)],
        out_specs=[_rows(BLK, AW), _whole((T, 2 * KVW)), _whole((NQ, 128))],
        out_shape=[jax.ShapeDtypeStruct((T, AW), BF16), jax.ShapeDtypeStruct((T, 2 * KVW), F32),
                   jax.ShapeDtypeStruct((NQ, 128), F32)],
        scratch_shapes=[pltpu.VMEM((GROWS, BAND), F32), pltpu.VMEM((GROWS, BAND), F32),
                        pltpu.VMEM((GROWS, BAND), BF16), pltpu.VMEM((GROWS, BAND), BF16)],
        compiler_params=_params("arbitrary"),
    )(sinks, qkv, o, do, lse)


PROJ_PARTS = [(0, AW), (AW, QKVW), (GLU_OFF, GATE_OFF), (GATE_OFF, INW)]


def _in_proj_bwd(dq, dkv, dglu, dgl, x, dx1, g_mix, w_in):
    tm = 256

    def body(dq_ref, dkv_ref, dglu_ref, dgl_ref, x_ref, dx1_ref, g_ref, w_ref, gx_ref, dg_ref, db_ref):
        i = pl.program_id(0)

        @pl.when(i == 0)
        def _():
            dg_ref[...] = jnp.zeros_like(dg_ref)
            db_ref[...] = jnp.zeros_like(db_ref)

        dh = jnp.zeros((tm, D), F32)
        for part_ref, (lo, hi) in zip((dq_ref, dkv_ref, dglu_ref, dgl_ref), PROJ_PARTS):
            part = part_ref[...]
            dh = dh + _dot_nt(part.astype(BF16), w_ref[:, lo:hi])
            db_ref[:, lo:hi] += jnp.sum(part.astype(F32), axis=0, keepdims=True)
        xv = x_ref[...]
        r = lax.rsqrt(jnp.mean(xv * xv, axis=-1, keepdims=True) + EPS)
        xh = xv * r
        dhg = dh * g_ref[...]
        gx_ref[...] = dx1_ref[...] + r * (dhg - xh * jnp.mean(dhg * xh, axis=-1, keepdims=True))
        dg_ref[...] += jnp.sum(dh * xh, axis=0, keepdims=True)

    return pl.pallas_call(
        body, name="in_proj_bwd", grid=(T // tm,),
        in_specs=[_rows(tm, AW), _rows(tm, 2 * KVW), _rows(tm, 2 * C), _rows(tm, 2 * D),
                  _rows(tm, D), _rows(tm, D), _whole((1, D)), _const((D, INW))],
        out_specs=[_rows(tm, D), _whole((1, D)), _whole((1, INW))],
        out_shape=[jax.ShapeDtypeStruct((T, D), F32), jax.ShapeDtypeStruct((1, D), F32),
                   jax.ShapeDtypeStruct((1, INW), F32)],
        compiler_params=_params("arbitrary"),
    )(dq, dkv, dglu, dgl, x, dx1, g_mix, w_in)


def _grad_w_in(h, dq, dkv, dglu, dgl):
    tk = 512

    def body(h_ref, dq_ref, dkv_ref, dglu_ref, dgl_ref, o_ref):
        ht = h_ref[...].T
        for part_ref, (lo, hi) in zip((dq_ref, dkv_ref, dglu_ref, dgl_ref), PROJ_PARTS):
            o_ref[:, lo:hi] = _dot(ht, part_ref[...].astype(BF16)).astype(BF16)

    return pl.pallas_call(
        body, name="grad_w_in", grid=(D // tk,),
        in_specs=[pl.BlockSpec((T, tk), lambda i: (0, i)), _whole((T, AW)), _whole((T, 2 * KVW)),
                  _whole((T, 2 * C)), _whole((T, 2 * D))],
        out_specs=pl.BlockSpec((tk, INW), lambda i: (i, 0)),
        out_shape=jax.ShapeDtypeStruct((D, INW), BF16),
        compiler_params=_params("parallel"),
    )(h, dq, dkv, dglu, dgl)


def _grad_w(a, b, name, tk, tn, col_sharded):
    k, n = a.shape[1], b.shape[1]

    def body(a_ref, b_ref, o_ref, at_ref):
        @pl.when(pl.program_id(1) == 0)
        def _():
            at_ref[...] = a_ref[...].T

        o_ref[...] = _dot(at_ref[...], b_ref[...]).astype(BF16)

    if col_sharded:
        per = n // N_CHIPS // tn
        shape = (N_CHIPS, k, n // N_CHIPS)
        out_map = lambda i, j: (j // per, i, j % per)
    else:
        shape = (1, k, n)
        out_map = lambda i, j: (0, i, j)
    out = pl.pallas_call(
        body, name=name, grid=(k // tk, n // tn),
        in_specs=[pl.BlockSpec((T, tk), lambda i, j: (0, i)), pl.BlockSpec((T, tn), lambda i, j: (0, j))],
        out_specs=pl.BlockSpec((None, tk, tn), out_map),
        out_shape=jax.ShapeDtypeStruct(shape, BF16),
        scratch_shapes=[pltpu.VMEM((tk, T), BF16)],
        compiler_params=_params("parallel", "arbitrary"),
    )(a, b)
    return out if col_sharded else out.reshape(N_CHIPS, k // N_CHIPS, n)


HBM_SPEC = pl.BlockSpec(memory_space=pltpu.HBM)


def _place():
    x, y, c = lax.axis_index("x"), lax.axis_index("y"), lax.axis_index("c")
    chips = [(1 - x, y), (x, 1 - y), (1 - x, 1 - y)]
    return x, y, c, chips


SEM_SPEC = pl.BlockSpec(memory_space=pltpu.SEMAPHORE)
ANY_SPEC = pl.BlockSpec(memory_space=pl.ANY)
VMEM_SPEC = pl.BlockSpec(memory_space=pltpu.VMEM)
EFFECT = pltpu.SideEffectType.DATAFLOW_SIDE_EFFECTING


def _gather_ends(src, land, x, y, c, chips):
    kh = src.shape[0] // 2
    s_me = 2 * x + y
    ends = [(src.at[pl.ds(c * kh, kh)], land.at[s_me, pl.ds(c * kh, kh)], (*chip, c)) for chip in chips]
    return ends + [(src, land.at[s_me], (x, y, 1 - c))]


def _reduce_ends(src, land, x, y, c, chips):
    return [(src.at[2 * chip[0] + chip[1]], land.at[j], (*chip, c)) for j, chip in enumerate(chips)]


def _chip_copies(ends, srcs, lands, send_sems, recv_sems):
    x, y, c, chips = _place()
    copies = []
    for src, land in zip(srcs, lands):
        for s, d, to in ends(src, land, x, y, c, chips):
            k = len(copies)
            copies.append(pltpu.make_async_remote_copy(
                src_ref=s, dst_ref=d, send_sem=send_sems.at[k], recv_sem=recv_sems.at[k],
                device_id=to, device_id_type=MESH))
    return copies


GATHER_PEERS, REDUCE_PEERS = 4, 3


def _chip_start(name, ends, peers, srcs, lands):
    n = len(srcs)

    def body(*refs):
        copies = _chip_copies(ends, refs[:n], refs[n:2 * n], refs[2 * n], refs[2 * n + 1])
        for cp in copies:
            cp.start()
        token = refs[-1]
        token[...] = jnp.zeros_like(token)

    hbm = lambda a: pltpu.HBM(a.shape, a.dtype)
    res = pl.pallas_call(
        body, name=name,
        out_shape=(pltpu.SemaphoreType.DMA((peers * n,)), pltpu.SemaphoreType.DMA((peers * n,)),
                   *[hbm(a) for a in srcs], *[hbm(a) for a in lands],
                   jax.ShapeDtypeStruct((8, 128), F32)),
        in_specs=[HBM_SPEC] * (2 * n),
        out_specs=(SEM_SPEC, SEM_SPEC, *[HBM_SPEC] * (2 * n), VMEM_SPEC),
        input_output_aliases={i: 2 + i for i in range(2 * n)},
        compiler_params=pltpu.CompilerParams(has_side_effects=EFFECT),
    )(*[pltpu.with_memory_space_constraint(a, pltpu.HBM) for a in (*srcs, *lands)])
    return res[0], res[1], list(res[2:2 + n]), list(res[2 + n:2 + 2 * n]), res[-1]


def _chip_wait(name, ends, send_sems, recv_sems, srcs, lands, after):
    n, na = len(srcs), len(after)

    def body(*refs):
        copies = _chip_copies(ends, refs[:n], refs[n:2 * n], refs[2 * n], refs[2 * n + 1])
        for cp in copies:
            cp.wait_send()
            cp.wait_recv()

    hbm = lambda a: pltpu.HBM(a.shape, a.dtype)
    res = pl.pallas_call(
        body, name=name,
        out_shape=tuple(hbm(a) for a in (*srcs, *lands)),
        in_specs=[HBM_SPEC] * (2 * n) + [SEM_SPEC, SEM_SPEC] + [ANY_SPEC] * na,
        out_specs=tuple([HBM_SPEC] * (2 * n)),
        input_output_aliases={i: i for i in range(2 * n)},
        compiler_params=pltpu.CompilerParams(has_side_effects=EFFECT),
    )(*srcs, *lands, send_sems, recv_sems, *after)
    return list(res[:n]), list(res[n:])


def _pair_forward(name, lands):
    n = len(lands)

    def body(*refs):
        bufs = refs[n:2 * n]
        send_sems, recv_sems = refs[2 * n:]
        x, y, c, chips = _place()

        def copy(w, j, cc, to):
            kh = bufs[w].shape[1] // 2
            blk = bufs[w].at[2 * chips[j][0] + chips[j][1], pl.ds(cc * kh, kh)]
            return pltpu.make_async_remote_copy(
                src_ref=blk, dst_ref=blk, send_sem=send_sems.at[3 * w + j],
                recv_sem=recv_sems.at[3 * w + j], device_id=to, device_id_type=MESH)

        sends = [copy(w, j, c, (x, y, 1 - c)) for w in range(n) for j in range(3)]
        for cp in sends:
            cp.start()
        for w in range(n):
            for j in range(3):
                copy(w, j, 1 - c, (x, y, c)).wait_recv()
        for cp in sends:
            cp.wait_send()

    return pl.pallas_call(
        body, name=name,
        in_specs=[HBM_SPEC] * n, out_specs=[HBM_SPEC] * n,
        out_shape=[jax.ShapeDtypeStruct(a.shape, a.dtype) for a in lands],
        input_output_aliases={w: w for w in range(n)},
        scratch_shapes=[pltpu.SemaphoreType.DMA((3 * n,)), pltpu.SemaphoreType.DMA((3 * n,))],
    )(*lands)


def _pair_exchange(name, grads):
    n = len(grads)

    def body(*refs):
        ins, outs = refs[:n], refs[n:2 * n]
        send_sems, recv_sems = refs[2 * n:]
        x, y, c, _ = _place()
        copies = []
        for w in range(n):
            kh = ins[w].shape[1] // 2
            cp = pltpu.make_async_remote_copy(
                src_ref=ins[w].at[:, pl.ds((1 - c) * kh, kh)], dst_ref=outs[w],
                send_sem=send_sems.at[w], recv_sem=recv_sems.at[w],
                device_id=(x, y, 1 - c), device_id_type=MESH)
            cp.start()
            copies.append(cp)
        for cp in copies:
            cp.wait()

    return pl.pallas_call(
        body, name=name,
        in_specs=[HBM_SPEC] * n, out_specs=[HBM_SPEC] * n,
        out_shape=[jax.ShapeDtypeStruct((g.shape[0], g.shape[1] // 2, g.shape[2]), g.dtype) for g in grads],
        scratch_shapes=[pltpu.SemaphoreType.DMA((n,)), pltpu.SemaphoreType.DMA((n,))],
    )(*grads)


def _row_tile(k):
    for t in (256, 128, 176, 64, 32, 16):
        if k % t == 0:
            return t
    raise ValueError(k)


def _pair_sum(c_idx, g, got, name):
    _, k, n = g.shape
    kh = k // 2
    tm = _row_tile(kh)
    nb = kh // tm

    def body(c_ref, g_ref, r_ref, o_ref):
        o_ref[...] = (g_ref[...].astype(F32) + r_ref[...].astype(F32)).astype(BF16)

    return pl.pallas_call(
        body, name=name,
        grid_spec=pltpu.PrefetchScalarGridSpec(
            num_scalar_prefetch=1, grid=(N_CHIPS, nb),
            in_specs=[pl.BlockSpec((1, tm, n), lambda s, i, c_ref: (s, c_ref[0] * nb + i, 0)),
                      pl.BlockSpec((1, tm, n), lambda s, i, c_ref: (s, i, 0))],
            out_specs=pl.BlockSpec((1, tm, n), lambda s, i, c_ref: (s, i, 0))),
        out_shape=jax.ShapeDtypeStruct((N_CHIPS, kh, n), BF16),
        compiler_params=_params("parallel", "parallel"),
    )(c_idx, g, got)


def _chip_sum(sc_idx, mine, got, name):
    _, kh, n = mine.shape
    tm = _row_tile(kh)

    def body(sc_ref, m_ref, r_ref, o_ref):
        acc = m_ref[0].astype(F32)
        for j in range(3):
            acc = acc + r_ref[j].astype(F32)
        o_ref[0] = acc

    return pl.pallas_call(
        body, name=name,
        grid_spec=pltpu.PrefetchScalarGridSpec(
            num_scalar_prefetch=1, grid=(kh // tm,),
            in_specs=[pl.BlockSpec((1, tm, n), lambda i, sc_ref: (sc_ref[0], i, 0)),
                      pl.BlockSpec((3, tm, n), lambda i, sc_ref: (0, i, 0))],
            out_specs=pl.BlockSpec((1, tm, n), lambda i, sc_ref: (sc_ref[1], i, 0))),
        out_shape=jax.ShapeDtypeStruct((2, kh, n), F32),
        compiler_params=_params("parallel"),
    )(sc_idx, mine, got)


def _pair_share(name, halves):
    n = len(halves)

    def body(*refs):
        bufs = refs[n:2 * n]
        send_sems, recv_sems = refs[2 * n:]
        x, y, c, _ = _place()
        copies = []
        for w in range(n):
            cp = pltpu.make_async_remote_copy(
                src_ref=bufs[w].at[c], dst_ref=bufs[w].at[c],
                send_sem=send_sems.at[w], recv_sem=recv_sems.at[w],
                device_id=(x, y, 1 - c), device_id_type=MESH)
            cp.start()
            copies.append(cp)
        for w in range(n):
            copies[w].wait_send()
            pltpu.make_async_remote_copy(
                src_ref=bufs[w].at[1 - c], dst_ref=bufs[w].at[1 - c],
                send_sem=send_sems.at[w], recv_sem=recv_sems.at[w],
                device_id=(x, y, c), device_id_type=MESH).wait_recv()

    return pl.pallas_call(
        body, name=name,
        in_specs=[HBM_SPEC] * n, out_specs=[HBM_SPEC] * n,
        out_shape=[jax.ShapeDtypeStruct(h.shape, h.dtype) for h in halves],
        input_output_aliases={w: w for w in range(n)},
        scratch_shapes=[pltpu.SemaphoreType.DMA((n,)), pltpu.SemaphoreType.DMA((n,))],
    )(*halves)


def _gather_small(blocks):
    n = len(blocks)

    def body(*refs):
        ins, outs = refs[:n], refs[n:2 * n]
        send_sems, recv_sems, local_sems = refs[2 * n:]
        x, y, c, chips = _place()
        me, sibling = (x, y, c), (x, y, 1 - c)

        def rows(w, px, py, pc):
            m = ins[w].shape[0]
            return outs[w].at[pl.ds((4 * px + 2 * py + pc) * m, m), :]

        def copy(w, k, block, to, src=None):
            return pltpu.make_async_remote_copy(
                src_ref=rows(w, *block) if src is None else src, dst_ref=rows(w, *block),
                send_sem=send_sems.at[7 * w + k], recv_sem=recv_sems.at[7 * w + k],
                device_id=to, device_id_type=MESH)

        started, owns = [], []
        for w in range(n):
            own = pltpu.make_async_copy(ins[w], rows(w, *me), local_sems.at[w])
            own.start()
            owns.append(own)
            first = [copy(w, 0, me, sibling, src=ins[w])]
            first += [copy(w, 1 + j, me, (*chip, c), src=ins[w]) for j, chip in enumerate(chips)]
            for cp in first:
                cp.start()
            started += first
        for w in range(n):
            for j, chip in enumerate(chips):
                copy(w, 1 + j, (*chip, c), me).wait_recv()
                cp = copy(w, 4 + j, (*chip, c), sibling)
                cp.start()
                started.append(cp)
        for w in range(n):
            copy(w, 0, sibling, me).wait_recv()
            for j, chip in enumerate(chips):
                copy(w, 4 + j, (*chip, 1 - c), me).wait_recv()
        for cp in started:
            cp.wait_send()
        for own in owns:
            own.wait()

    vmem = pl.BlockSpec(memory_space=pltpu.VMEM)
    return pl.pallas_call(
        body, name="gather_small",
        in_specs=[vmem] * n, out_specs=[vmem] * n,
        out_shape=[jax.ShapeDtypeStruct((8 * b.shape[0], b.shape[1]), b.dtype) for b in blocks],
        scratch_shapes=[pltpu.SemaphoreType.DMA((7 * n,)), pltpu.SemaphoreType.DMA((7 * n,)),
                        pltpu.SemaphoreType.DMA((n,))],
    )(*blocks)


def _adamw_math(w, g, m, v):
    m = ADAM_B1 * m + (1.0 - ADAM_B1) * g
    v = ADAM_B2 * v + (1.0 - ADAM_B2) * (g * g)
    m_hat = m / (1.0 - ADAM_B1 ** ADAM_STEP)
    v_hat = v / (1.0 - ADAM_B2 ** ADAM_STEP)
    delta = -ADAM_LR * (m_hat / (jnp.sqrt(v_hat) + ADAM_EPS) + ADAM_WD * w)
    return delta, m, v


def _adamw(w, g, m, v, name):
    k, n = w.shape
    tm = k // 4

    def body(w_ref, g_ref, m_ref, v_ref, go_ref, d_ref, mo_ref, vo_ref):
        g = g_ref[...]
        d, mm, vv = _adamw_math(w_ref[...], g, m_ref[...], v_ref[...])
        go_ref[...] = g
        d_ref[...] = d
        mo_ref[...] = mm
        vo_ref[...] = vv

    spec = pl.BlockSpec((tm, n), lambda i: (i, 0))
    shp = jax.ShapeDtypeStruct((k, n), F32)
    return pl.pallas_call(
        body, name=name, grid=(4,), in_specs=[spec] * 4, out_specs=[spec] * 4,
        out_shape=[shp] * 4, compiler_params=_params("parallel"),
    )(w, g, m, v)


VEC_SLOTS = {
    "g_mix_norm": (0, 0, D), "b_conv_proj": (0, D, D), "g_ffn_norm": (0, 2 * D, D),
    "g_final": (0, 3 * D, D), "b_in": (1, 0, INW), "conv_b": (2, 0, C), "ln_g": (2, C, C),
    "ln_b": (2, 2 * C, C), "sinks": (2, 3 * C, NQ), "loss": (2, 3 * C + 128, 1),
}
VEC_ROWS, VEC_COLS = 8, 4 * D
CW_ROWS = 32
SMALL_NAMES = ["g_mix_norm", "b_in", "sinks", "conv_w", "conv_b", "ln_g", "ln_b",
               "b_conv_proj", "g_ffn_norm", "g_final"]
CW_LANES = C // N_CHIPS


def _pack_small(gs, loss):
    row0 = jnp.concatenate([gs["g_mix_norm"], gs["b_conv_proj"], gs["g_ffn_norm"], gs["g_final"]], axis=1)
    row1 = jnp.pad(gs["b_in"], ((0, 0), (0, VEC_COLS - INW)))
    row2 = jnp.concatenate([gs["conv_b"], gs["ln_g"], gs["ln_b"],
                            jnp.pad(gs["sinks"], ((0, 0), (0, 128 - NQ))),
                            jnp.pad(loss.reshape(1, 1), ((0, 0), (0, VEC_COLS - 3 * C - 129)))], axis=1)
    vec = jnp.concatenate([row0, row1, row2, jnp.zeros((VEC_ROWS - 3, VEC_COLS), F32)], axis=0)
    cw = jnp.pad(gs["conv_w"], ((0, CW_ROWS - KW), (0, 0)))
    return vec, cw


def _small_update(s_idx, vec_all, cw_all, wmv):
    nsm = len(SMALL_NAMES)

    def body(s_ref, vec_ref, cw_ref, *refs):
        ins = refs[:3 * nsm]
        outs = refs[3 * nsm:7 * nsm]
        loss_ref = refs[7 * nsm]

        def total(slot):
            row, lane, width = slot
            acc = vec_ref[row:row + 1, lane:lane + width]
            for k in range(1, 8):
                acc = acc + vec_ref[k * VEC_ROWS + row:k * VEC_ROWS + row + 1, lane:lane + width]
            return acc

        loss_ref[...] = jnp.broadcast_to(total(VEC_SLOTS["loss"]), loss_ref.shape)
        for p, name in enumerate(SMALL_NAMES):
            w_ref, m_ref, v_ref = ins[3 * p:3 * p + 3]
            g_ref, d_ref, mo_ref, vo_ref = outs[4 * p:4 * p + 4]
            if name == "conv_w":
                g = jnp.zeros((KW, CW_LANES), F32)
                for s in range(N_CHIPS):
                    cand = cw_ref[0:KW, s * CW_LANES:(s + 1) * CW_LANES]
                    for k in range(1, 8):
                        cand = cand + cw_ref[k * CW_ROWS:k * CW_ROWS + KW, s * CW_LANES:(s + 1) * CW_LANES]
                    g = jnp.where(s_ref[0] == s, cand, g)
            else:
                g = total(VEC_SLOTS[name])
            d, mm, vv = _adamw_math(w_ref[...], g, m_ref[...], v_ref[...])
            g_ref[...] = g
            d_ref[...] = d
            mo_ref[...] = mm
            vo_ref[...] = vv

    vmem = pl.BlockSpec(memory_space=pltpu.VMEM)
    flat = [a for t in wmv for a in t]
    out_shape = []
    for w, _, _ in wmv:
        out_shape += [jax.ShapeDtypeStruct(w.shape, F32)] * 4
    out_shape.append(jax.ShapeDtypeStruct((1, 128), F32))
    res = pl.pallas_call(
        body, name="small_update",
        in_specs=[pl.BlockSpec(memory_space=pltpu.SMEM)] + [vmem] * (2 + len(flat)),
        out_specs=[vmem] * len(out_shape), out_shape=out_shape,
    )(s_idx, vec_all, cw_all, *flat)
    return [tuple(res[4 * p:4 * p + 4]) for p in range(nsm)], res[4 * nsm]


WEIGHT_ORDER = ["g_mix_norm", "w_in", "b_in", "sinks", "conv_w", "conv_b", "ln_g", "ln_b",
                "w_attn_proj", "w_conv_proj", "b_conv_proj", "w_out", "g_ffn_norm", "w_ffn_in",
                "w_ffn_down", "g_final"]


def kernel(x, g_mix_norm, w_in, b_in, sinks, conv_w, conv_b, ln_g, ln_b, w_attn_proj, w_conv_proj, b_conv_proj, w_out, g_ffn_norm, w_ffn_in, w_ffn_down, g_final, loss_target, m_g_mix_norm, m_w_in, m_b_in, m_sinks, m_conv_w, m_conv_b, m_ln_g, m_ln_b, m_w_attn_proj, m_w_conv_proj, m_b_conv_proj, m_w_out, m_g_ffn_norm, m_w_ffn_in, m_w_ffn_down, m_g_final, v_g_mix_norm, v_w_in, v_b_in, v_sinks, v_conv_w, v_conv_b, v_ln_g, v_ln_b, v_w_attn_proj, v_w_conv_proj, v_b_conv_proj, v_w_out, v_g_ffn_norm, v_w_ffn_in, v_w_ffn_down, v_g_final):
    w = dict(g_mix_norm=g_mix_norm, w_in=w_in, b_in=b_in, sinks=sinks, conv_w=conv_w, conv_b=conv_b,
             ln_g=ln_g, ln_b=ln_b, w_attn_proj=w_attn_proj, w_conv_proj=w_conv_proj,
             b_conv_proj=b_conv_proj, w_out=w_out, g_ffn_norm=g_ffn_norm, w_ffn_in=w_ffn_in,
             w_ffn_down=w_ffn_down, g_final=g_final)
    m = dict(g_mix_norm=m_g_mix_norm, w_in=m_w_in, b_in=m_b_in, sinks=m_sinks, conv_w=m_conv_w,
             conv_b=m_conv_b, ln_g=m_ln_g, ln_b=m_ln_b, w_attn_proj=m_w_attn_proj,
             w_conv_proj=m_w_conv_proj, b_conv_proj=m_b_conv_proj, w_out=m_w_out,
             g_ffn_norm=m_g_ffn_norm, w_ffn_in=m_w_ffn_in, w_ffn_down=m_w_ffn_down, g_final=m_g_final)
    v = dict(g_mix_norm=v_g_mix_norm, w_in=v_w_in, b_in=v_b_in, sinks=v_sinks, conv_w=v_conv_w,
             conv_b=v_conv_b, ln_g=v_ln_g, ln_b=v_ln_b, w_attn_proj=v_w_attn_proj,
             w_conv_proj=v_w_conv_proj, b_conv_proj=v_b_conv_proj, w_out=v_w_out,
             g_ffn_norm=v_g_ffn_norm, w_ffn_in=v_w_ffn_in, w_ffn_down=v_w_ffn_down, g_final=v_g_final)

    c_idx = lax.axis_index("c").astype(jnp.int32).reshape(1)
    s_idx = (2 * lax.axis_index("x") + lax.axis_index("y")).astype(jnp.int32).reshape(1)

    sc_idx = jnp.concatenate([s_idx, c_idx])
    out_g, out_d, out_m, out_v = {}, {}, {}, {}

    def gather_start(tag, shards):
        lands = [lax.empty((N_CHIPS,) + s.shape, s.dtype) for s in shards]
        return _chip_start("gather_start_" + tag, _gather_ends, GATHER_PEERS, shards, lands)

    def gather_finish(tag, state, after):
        send_sems, recv_sems, shards, lands, _ = state
        shards, lands = _chip_wait("gather_wait_" + tag, _gather_ends, send_sems, recv_sems, shards, lands, after)
        return _pair_forward("pair_forward_" + tag, lands)

    names_b = ["w_attn_proj", "w_conv_proj", "w_out", "w_ffn_in", "w_ffn_down"]
    state_a = gather_start("a", [w["w_in"][0].astype(BF16), jnp.pad(conv_w[0], ((0, CW_ROWS - KW), (0, 0)))])
    state_b = gather_start("b", [w[name][0].astype(BF16) for name in names_b])
    got_a = gather_finish("a", state_a, [state_b[4]])
    w_in_full = got_a[0].transpose(1, 0, 2).reshape(D, INW)
    conv_w_full = got_a[1].transpose(1, 0, 2).reshape(CW_ROWS, C)[:KW]

    xs, target = x[0], loss_target[0]
    g_final2 = g_final.reshape(1, D)
    h, qkv, glu, gl = _in_proj(xs, g_mix_norm, w_in_full, b_in)
    o, lse = _attn_fwd(qkv, sinks)
    u, cact = _conv_fwd(glu, conv_w_full, conv_b, ln_g, ln_b)
    w_ap4, w_cp4, w_out4, w_fi4, w_dn4 = gather_finish("b", state_b, [o, cact])
    w_out_full, w_dn_full = w_out4.reshape(D, D), w_dn4.reshape(DFF, D)
    ya, yc, mg, x1 = _mix_out(xs, o, cact, gl, w_ap4, w_cp4, b_conv_proj, w_out_full)
    h2, gu, act = _ffn_in(x1, g_ffn_norm, w_fi4)
    dx2, dx2b, dg_final, loss_part = _ffn_out_loss(x1, act, w_dn_full, g_final2, target)

    def reduce_start(tag, names, grads):
        from_sibling = _pair_exchange("pair_exchange_" + tag, grads)
        pair = [_pair_sum(c_idx, g, r, "pair_sum_" + name) for name, g, r in zip(names, grads, from_sibling)]
        lands = [lax.empty((3,) + p.shape[1:], p.dtype) for p in pair]
        return _chip_start("chip_start_" + tag, _reduce_ends, REDUCE_PEERS, pair, lands)

    def reduce_finish(tag, names, state, after):
        send_sems, recv_sems, pair, lands, _ = state
        pair, lands = _chip_wait("chip_wait_" + tag, _reduce_ends, send_sems, recv_sems, pair, lands, after)
        halves = [_chip_sum(sc_idx, p, r, "chip_sum_" + name) for name, p, r in zip(names, pair, lands)]
        for name, both in zip(names, _pair_share("pair_share_" + tag, halves)):
            g, d, mm, vv = _adamw(w[name][0], both.reshape(w[name].shape[1:]), m[name][0], v[name][0],
                                  "adamw_" + name)
            out_g[name], out_d[name], out_m[name], out_v[name] = g[None], d[None], mm[None], vv[None]

    dgu, dx1, dx1b, dg_ffn = _ffn_bwd(dx2, dx2b, gu, x1, g_ffn_norm, w_dn_full, w_fi4)
    names_1 = ["w_ffn_in", "w_ffn_down"]
    state_1 = reduce_start("1", names_1, [_grad_w(h2, dgu, "grad_w_ffn_in", 512, FSH, True),
                                          _grad_w(act, dx2b, "grad_w_ffn_down", 256, D, False)])
    dya, dyc, dgl, do, dc, db_cp = _mix_bwd(dx1b, gl, ya, yc, w_out_full, w_ap4, w_cp4, state_1[4])
    names_2 = ["w_out", "w_attn_proj", "w_conv_proj"]
    state_2 = reduce_start("2", names_2, [_grad_w(mg, dx1b, "grad_w_out", 512, D, False),
                                          _grad_w(o, dya, "grad_w_attn_proj", 512, 256, True),
                                          _grad_w(cact, dyc, "grad_w_conv_proj", 512, 256, True)])
    dglu, dconv_w, dconv_b, dln_g, dln_b = _conv_bwd(glu, u, dc, conv_w_full, ln_g, ln_b, state_2[4])
    dq, dkv, dsinks = _attn_bwd(qkv, o, do, lse, sinks)
    grad_x, dg_mix, db_in = _in_proj_bwd(dq, dkv, dglu, dgl, xs, dx1, g_mix_norm, w_in_full)
    names_3 = ["w_in"]
    gw_in = _grad_w_in(h, dq, dkv, dglu, dgl)
    state_3 = reduce_start("3", names_3, [gw_in.reshape(D, N_CHIPS, INW // N_CHIPS).transpose(1, 0, 2)])

    reduce_finish("1", names_1, state_1, [state_3[4]])
    reduce_finish("2", names_2, state_2, [out_d["w_ffn_down"]])

    gs = {"g_mix_norm": dg_mix, "b_in": db_in, "sinks": dsinks[:, 0].reshape(1, NQ),
          "conv_w": dconv_w, "conv_b": dconv_b, "ln_g": dln_g, "ln_b": dln_b,
          "b_conv_proj": db_cp, "g_ffn_norm": dg_ffn, "g_final": dg_final}
    vec, cw = _pack_small(gs, loss_part[0, 0])
    vec_all, cw_all = _gather_small([vec, cw])

    def view(a, name):
        if name == "conv_w":
            return a[0]
        if name == "g_final":
            return a.reshape(1, D)
        return a

    wmv = [(view(w[name], name), view(m[name], name), view(v[name], name)) for name in SMALL_NAMES]
    small_out, loss_row = _small_update(s_idx, vec_all, cw_all, wmv)
    for name, (g, d, mm, vv) in zip(SMALL_NAMES, small_out):
        shape = w[name].shape
        out_g[name], out_d[name], out_m[name], out_v[name] = (
            g.reshape(shape), d.reshape(shape), mm.reshape(shape), vv.reshape(shape))

    reduce_finish("3", names_3, state_3, [loss_row, out_d["w_conv_proj"]])

    loss = loss_row[0, 0]
    return (loss, grad_x[None], *[out_g[k] for k in WEIGHT_ORDER], *[out_d[k] for k in WEIGHT_ORDER],
            *[out_m[k] for k in WEIGHT_ORDER], *[out_v[k] for k in WEIGHT_ORDER])
```

```python
import functools

import jax
import jax.numpy as jnp
from jax import lax
from jax.experimental import pallas as pl
from jax.experimental.pallas import tpu as pltpu

F32 = jnp.float32
BF16 = jnp.bfloat16

T = 2048
D = 1024
HD = 64
NQ = 8
NKV = 2
GROUP = NQ // NKV
BLK = 128
AW = NQ * HD
KVW = NKV * HD
C = 512
KW = 31
QKVW = AW + 2 * KVW
GLU_OFF = QKVW
GATE_OFF = GLU_OFF + 2 * C
INW = GATE_OFF + 2 * D
DFF = 2816
EPS = 1e-5
NEG = -1e30
SCALE = HD ** -0.5
HALO = 32
N_CHIPS = 4
FSH = 2 * DFF // N_CHIPS

ADAM_LR = 0.001
ADAM_B1 = 0.9
ADAM_B2 = 0.999
ADAM_EPS = 1e-08
ADAM_WD = 0.01
ADAM_STEP = 10

VMEM_LIMIT = 56 * 1024 * 1024
MESH = pl.DeviceIdType.MESH


def _params(*sem):
    return pltpu.CompilerParams(dimension_semantics=sem, vmem_limit_bytes=VMEM_LIMIT)


def _dot(a, b):
    return jnp.dot(a, b, preferred_element_type=F32)


def _dot_nt(a, b):
    return lax.dot_general(a, b, (((1,), (1,)), ((), ())), preferred_element_type=F32)


def _dot_tn(a, b):
    return lax.dot_general(a, b, (((0,), (0,)), ((), ())), preferred_element_type=F32)


def _sigmoid(v):
    return 1.0 / (1.0 + jnp.exp(-v))


def _rows(tm, n):
    return pl.BlockSpec((tm, n), lambda i: (i, 0))


def _whole(shape):
    return pl.BlockSpec(shape, lambda i: tuple(0 for _ in shape))


def _in_proj(x, g_mix, w_in_t, b_in):
    tm = 256

    def body(x_ref, g_ref, w_ref, b_ref, h_ref, qkv_ref, glu_ref, gl_ref):
        xv = x_ref[...]
        r = lax.rsqrt(jnp.mean(xv * xv, axis=-1, keepdims=True) + EPS)
        h = (xv * r * g_ref[...]).astype(BF16)
        h_ref[...] = h
        qkv_ref[...] = (_dot_nt(h, w_ref[0:GLU_OFF, :]) + b_ref[:, 0:GLU_OFF]).astype(BF16)
        glu_ref[...] = (_dot_nt(h, w_ref[GLU_OFF:GATE_OFF, :]) + b_ref[:, GLU_OFF:GATE_OFF]).astype(BF16)
        gl_ref[...] = (_dot_nt(h, w_ref[GATE_OFF:INW, :]) + b_ref[:, GATE_OFF:INW]).astype(BF16)

    return pl.pallas_call(
        body, name="in_proj", grid=(T // tm,),
        in_specs=[_rows(tm, D), _whole((1, D)), _whole((INW, D)), _whole((1, INW))],
        out_specs=[_rows(tm, D), _rows(tm, QKVW), _rows(tm, 2 * C), _rows(tm, 2 * D)],
        out_shape=[jax.ShapeDtypeStruct((T, D), BF16), jax.ShapeDtypeStruct((T, QKVW), BF16),
                   jax.ShapeDtypeStruct((T, 2 * C), BF16), jax.ShapeDtypeStruct((T, 2 * D), BF16)],
        compiler_params=_params("parallel"),
    )(x, g_mix, w_in_t, b_in)


GROWS = GROUP * BLK
BAND = 2 * BLK


def _band(i):
    rb = pl.multiple_of(jnp.maximum(i - 1, 0) * BLK, BLK)
    row = lax.broadcasted_iota(jnp.int32, (GROWS, BAND), 0)
    kpos = rb + lax.broadcasted_iota(jnp.int32, (GROWS, BAND), 1)
    qpos = i * BLK + jnp.bitwise_and(row, BLK - 1)
    return rb, jnp.logical_and(kpos <= qpos, kpos > qpos - BLK)


def _sink_column(sink_ref, g):
    head = lax.shift_right_logical(lax.broadcasted_iota(jnp.int32, (GROWS, 1), 0), 7)
    col = jnp.full((GROWS, 1), sink_ref[0, g * GROUP], F32)
    for hh in range(1, GROUP):
        col = jnp.where(head == hh, sink_ref[0, g * GROUP + hh], col)
    return col


def _attn_fwd(qkv, sinks):
    def body(sink_ref, qkv_ref, o_ref, lse_ref, s_ref, p_ref):
        i = pl.program_id(0)
        r0 = pl.multiple_of(i * BLK, BLK)
        rb, valid = _band(i)
        for g in range(NKV):
            kband = qkv_ref[pl.ds(rb, BAND), AW + g * HD:AW + (g + 1) * HD]
            vband = qkv_ref[pl.ds(rb, BAND), AW + KVW + g * HD:AW + KVW + (g + 1) * HD]
            for hh in range(GROUP):
                h = g * GROUP + hh
                s_ref[hh * BLK:(hh + 1) * BLK, :] = _dot_nt(qkv_ref[pl.ds(r0, BLK), h * HD:(h + 1) * HD], kband)
            s = jnp.where(valid, s_ref[...] * SCALE, NEG)
            sink = _sink_column(sink_ref, g)
            m = jnp.maximum(jnp.max(s, axis=-1, keepdims=True), sink)
            p = jnp.exp(s - m)
            den = jnp.sum(p, axis=-1, keepdims=True) + jnp.exp(sink - m)
            p_ref[...] = (p * (1.0 / den)).astype(BF16)
            lse = m + jnp.log(den)
            for hh in range(GROUP):
                h = g * GROUP + hh
                o_ref[:, h * HD:(h + 1) * HD] = _dot(p_ref[hh * BLK:(hh + 1) * BLK, :], vband).astype(BF16)
                lse_ref[:, h:h + 1] = lse[hh * BLK:(hh + 1) * BLK]

    return pl.pallas_call(
        body, name="attn_fwd", grid=(T // BLK,),
        in_specs=[pl.BlockSpec(memory_space=pltpu.SMEM), _whole((T, QKVW))],
        out_specs=[_rows(BLK, AW), _rows(BLK, NQ)],
        out_shape=[jax.ShapeDtypeStruct((T, AW), BF16), jax.ShapeDtypeStruct((T, NQ), F32)],
        scratch_shapes=[pltpu.VMEM((GROWS, BAND), F32), pltpu.VMEM((GROWS, BAND), BF16)],
        compiler_params=_params("parallel"),
    )(sinks, qkv)


CONV_TM = 256
CONV_SUB = 32


def _glu(ab):
    a = ab[:, 0:C].astype(F32)
    b = ab[:, C:2 * C].astype(F32)
    return a * _sigmoid(b)


SUBLANES = 8


def _shifted_copies(ref):
    rows = ref.shape[1] - SUBLANES
    for r in range(1, SUBLANES):
        ref[r, 0:rows, :] = ref[0, r:r + rows, :]


def _shifted_rows(ref, start, size):
    r = start % SUBLANES
    return ref[r, start - r:start - r + size, :]


def _conv_fwd(glu, conv_w, conv_b, ln_g, ln_b):
    tm = CONV_TM

    def body(cur_ref, prev_ref, w_ref, cb_ref, g_ref, b_ref, u_ref, c_ref, zs_ref):
        i = pl.program_id(0)
        zprev = _glu(prev_ref[tm - HALO:tm, :])
        zs_ref[0, 0:HALO, :] = jnp.where(i > 0, zprev, 0.0)
        zs_ref[0, HALO:HALO + tm, :] = _glu(cur_ref[...])
        _shifted_copies(zs_ref)
        for s in range(tm // CONV_SUB):
            base = HALO + s * CONV_SUB - (KW - 1)
            acc = jnp.broadcast_to(cb_ref[...], (CONV_SUB, C))
            for j in range(KW):
                acc = acc + w_ref[j:j + 1, :] * _shifted_rows(zs_ref, base + j, CONV_SUB)
            rows = slice(s * CONV_SUB, (s + 1) * CONV_SUB)
            u_ref[rows, :] = acc
            mu = jnp.mean(acc, axis=-1, keepdims=True)
            xc = acc - mu
            var = jnp.mean(xc * xc, axis=-1, keepdims=True)
            y = xc * lax.rsqrt(var + EPS) * g_ref[...] + b_ref[...]
            c_ref[rows, :] = (y * _sigmoid(y)).astype(BF16)

    return pl.pallas_call(
        body, name="conv_fwd", grid=(T // tm,),
        in_specs=[_rows(tm, 2 * C),
                  pl.BlockSpec((tm, 2 * C), lambda i: (jnp.maximum(i - 1, 0), 0)),
                  _whole((KW, C)), _whole((1, C)), _whole((1, C)), _whole((1, C))],
        out_specs=[_rows(tm, C), _rows(tm, C)],
        out_shape=[jax.ShapeDtypeStruct((T, C), F32), jax.ShapeDtypeStruct((T, C), BF16)],
        scratch_shapes=[pltpu.VMEM((SUBLANES, HALO + tm, C), F32)],
        compiler_params=_params("parallel"),
    )(glu, glu, conv_w, conv_b, ln_g, ln_b)


def _mix_out(x, o, cact, gl, w_ap, w_cp, b_cp, w_out):
    tm = 256

    def body(x_ref, o_ref, c_ref, gl_ref, wap_ref, wcp_ref, bcp_ref, wo_ref,
             ya_ref, yc_ref, mg_ref, x1_ref):
        ov, cv = o_ref[...], c_ref[...]
        ya = jnp.concatenate([_dot(ov, wap_ref[s]) for s in range(N_CHIPS)], axis=1)
        yc = jnp.concatenate([_dot(cv, wcp_ref[s]) for s in range(N_CHIPS)], axis=1) + bcp_ref[...]
        g0 = _sigmoid(gl_ref[:, 0:D].astype(F32))
        g1 = _sigmoid(gl_ref[:, D:2 * D].astype(F32))
        mg = (g0 * ya + g1 * yc).astype(BF16)
        ya_ref[...] = ya.astype(BF16)
        yc_ref[...] = yc.astype(BF16)
        mg_ref[...] = mg
        x1_ref[...] = x_ref[...] + _dot(mg, wo_ref[...])

    return pl.pallas_call(
        body, name="mix_out", grid=(T // tm,),
        in_specs=[_rows(tm, D), _rows(tm, AW), _rows(tm, C), _rows(tm, 2 * D),
                  _whole((N_CHIPS, AW, D // N_CHIPS)), _whole((N_CHIPS, C, D // N_CHIPS)), _whole((1, D)),
                  _whole((D, D))],
        out_specs=[_rows(tm, D), _rows(tm, D), _rows(tm, D), _rows(tm, D)],
        out_shape=[jax.ShapeDtypeStruct((T, D), BF16), jax.ShapeDtypeStruct((T, D), BF16),
                   jax.ShapeDtypeStruct((T, D), BF16), jax.ShapeDtypeStruct((T, D), F32)],
        compiler_params=_params("parallel"),
    )(x, o, cact, gl, w_ap, w_cp, b_cp, w_out)


def _ffn_in(x1, g_ffn, w_fi):
    tm = 256

    def body(x_ref, g_ref, w_ref, h_ref, gu_ref, act_ref):
        xv = x_ref[...]
        r = lax.rsqrt(jnp.mean(xv * xv, axis=-1, keepdims=True) + EPS)
        h = (xv * r * g_ref[...]).astype(BF16)
        h_ref[...] = h
        for s in range(N_CHIPS // 2):
            c0 = s * FSH
            gate = _dot(h, w_ref[s])
            up = _dot(h, w_ref[s + N_CHIPS // 2])
            gu_ref[:, c0:c0 + FSH] = gate.astype(BF16)
            gu_ref[:, DFF + c0:DFF + c0 + FSH] = up.astype(BF16)
            act_ref[:, c0:c0 + FSH] = (gate * _sigmoid(gate) * up).astype(BF16)

    return pl.pallas_call(
        body, name="ffn_in", grid=(T // tm,),
        in_specs=[_rows(tm, D), _whole((1, D)), _whole((N_CHIPS, D, FSH))],
        out_specs=[_rows(tm, D), _rows(tm, 2 * DFF), _rows(tm, DFF)],
        out_shape=[jax.ShapeDtypeStruct((T, D), BF16), jax.ShapeDtypeStruct((T, 2 * DFF), BF16),
                   jax.ShapeDtypeStruct((T, DFF), BF16)],
        compiler_params=_params("parallel"),
    )(x1, g_ffn, w_fi)


def _ffn_out_loss(x1, act, w_dn, g_final, target):
    tm = 256

    def body(x_ref, a_ref, w_ref, g_ref, t_ref, dx_ref, dxb_ref, dg_ref, loss_ref):
        i = pl.program_id(0)
        x2 = x_ref[...] + _dot(a_ref[...], w_ref[...])
        r = lax.rsqrt(jnp.mean(x2 * x2, axis=-1, keepdims=True) + EPS)
        xh = x2 * r
        g = g_ref[...]
        err = xh * g - t_ref[...]
        dy = err * (1.0 / D)
        dyg = dy * g
        dx = r * (dyg - xh * jnp.mean(dyg * xh, axis=-1, keepdims=True))
        dx_ref[...] = dx
        dxb_ref[...] = dx.astype(BF16)
        part = 0.5 * jnp.sum(jnp.mean(err * err, axis=-1, keepdims=True), axis=0, keepdims=True)

        @pl.when(i == 0)
        def _():
            dg_ref[...] = jnp.zeros_like(dg_ref)
            loss_ref[...] = jnp.zeros_like(loss_ref)

        dg_ref[...] += jnp.sum(dy * xh, axis=0, keepdims=True)
        loss_ref[...] += jnp.broadcast_to(part, loss_ref.shape)

    return pl.pallas_call(
        body, name="ffn_out_loss", grid=(T // tm,),
        in_specs=[_rows(tm, D), _rows(tm, DFF), _whole((DFF, D)), _whole((1, D)), _rows(tm, D)],
        out_specs=[_rows(tm, D), _rows(tm, D), _whole((1, D)), _whole((1, 128))],
        out_shape=[jax.ShapeDtypeStruct((T, D), F32), jax.ShapeDtypeStruct((T, D), BF16),
                   jax.ShapeDtypeStruct((1, D), F32), jax.ShapeDtypeStruct((1, 128), F32)],
        compiler_params=_params("arbitrary"),
    )(x1, act, w_dn, g_final, target)


def _const(shape):
    return pl.BlockSpec(shape, lambda i: tuple(0 for _ in shape), pipeline_mode=pl.Buffered(1))


def _ffn_bwd(dx2, dx2b, gu, x1, g_ffn, w_dn_t, w_fi_t):
    tm = 256

    def body(dx_ref, dxb_ref, gu_ref, x_ref, g_ref, wdn_ref, wfi_ref,
             dgu_ref, dx1_ref, dx1b_ref, dg_ref):
        i = pl.program_id(0)
        dxb = dxb_ref[...]
        dh = jnp.zeros((tm, D), F32)
        for k in range(N_CHIPS // 2):
            c0 = k * FSH
            dact = _dot_nt(dxb, wdn_ref[c0:c0 + FSH, :])
            gate = gu_ref[:, c0:c0 + FSH].astype(F32)
            up = gu_ref[:, DFF + c0:DFF + c0 + FSH].astype(F32)
            s = _sigmoid(gate)
            dup = (dact * gate * s).astype(BF16)
            dgate = (dact * up * s * (1.0 + gate * (1.0 - s))).astype(BF16)
            dgu_ref[:, c0:c0 + FSH] = dgate
            dgu_ref[:, DFF + c0:DFF + c0 + FSH] = dup
            dh = dh + _dot_nt(dgate, wfi_ref[k]) + _dot_nt(dup, wfi_ref[k + N_CHIPS // 2])
        xv = x_ref[...]
        r = lax.rsqrt(jnp.mean(xv * xv, axis=-1, keepdims=True) + EPS)
        xh = xv * r
        dhg = dh * g_ref[...]
        dx1 = dx_ref[...] + r * (dhg - xh * jnp.mean(dhg * xh, axis=-1, keepdims=True))
        dx1_ref[...] = dx1
        dx1b_ref[...] = dx1.astype(BF16)

        @pl.when(i == 0)
        def _():
            dg_ref[...] = jnp.zeros_like(dg_ref)

        dg_ref[...] += jnp.sum(dh * xh, axis=0, keepdims=True)

    return pl.pallas_call(
        body, name="ffn_bwd", grid=(T // tm,),
        in_specs=[_rows(tm, D), _rows(tm, D), _rows(tm, 2 * DFF), _rows(tm, D), _whole((1, D)),
                  _const((DFF, D)), _const((N_CHIPS, D, FSH))],
        out_specs=[_rows(tm, 2 * DFF), _rows(tm, D), _rows(tm, D), _whole((1, D))],
        out_shape=[jax.ShapeDtypeStruct((T, 2 * DFF), BF16), jax.ShapeDtypeStruct((T, D), F32),
                   jax.ShapeDtypeStruct((T, D), BF16), jax.ShapeDtypeStruct((1, D), F32)],
        compiler_params=_params("arbitrary"),
    )(dx2, dx2b, gu, x1, g_ffn, w_dn_t, w_fi_t)


def _mix_bwd(dx1b, gl, ya, yc, w_out_t, w_ap_t, w_cp_t, dep):
    tm = 256

    def body(dx_ref, gl_ref, ya_ref, yc_ref, wo_ref, wap_ref, wcp_ref, dep_ref,
             dya_ref, dyc_ref, dgl_ref, do_ref, dc_ref, db_ref):
        i = pl.program_id(0)
        dm = _dot_nt(dx_ref[...], wo_ref[...])
        g0 = _sigmoid(gl_ref[:, 0:D].astype(F32))
        g1 = _sigmoid(gl_ref[:, D:2 * D].astype(F32))
        dya = dm * g0
        dyc = dm * g1
        dgl_ref[:, 0:D] = (dya * ya_ref[...].astype(F32) * (1.0 - g0)).astype(BF16)
        dgl_ref[:, D:2 * D] = (dyc * yc_ref[...].astype(F32) * (1.0 - g1)).astype(BF16)
        dyab = dya.astype(BF16)
        dycb = dyc.astype(BF16)
        dya_ref[...] = dyab
        dyc_ref[...] = dycb
        sw = D // N_CHIPS
        do = jnp.zeros((tm, AW), F32)
        dcv = jnp.zeros((tm, C), F32)
        for s in range(N_CHIPS):
            do = do + _dot_nt(dyab[:, s * sw:(s + 1) * sw], wap_ref[s])
            dcv = dcv + _dot_nt(dycb[:, s * sw:(s + 1) * sw], wcp_ref[s])
        do_ref[...] = do.astype(BF16)
        dc_ref[...] = dcv

        @pl.when(i == 0)
        def _():
            db_ref[...] = jnp.zeros_like(db_ref)

        db_ref[...] += jnp.sum(dyc, axis=0, keepdims=True)

    return pl.pallas_call(
        body, name="mix_bwd", grid=(T // tm,),
        in_specs=[_rows(tm, D), _rows(tm, 2 * D), _rows(tm, D), _rows(tm, D),
                  _whole((D, D)), _whole((N_CHIPS, AW, D // N_CHIPS)), _whole((N_CHIPS, C, D // N_CHIPS)),
                  _whole((8, 128))],
        out_specs=[_rows(tm, D), _rows(tm, D), _rows(tm, 2 * D), _rows(tm, AW), _rows(tm, C),
                   _whole((1, D))],
        out_shape=[jax.ShapeDtypeStruct((T, D), BF16), jax.ShapeDtypeStruct((T, D), BF16),
                   jax.ShapeDtypeStruct((T, 2 * D), BF16), jax.ShapeDtypeStruct((T, AW), BF16),
                   jax.ShapeDtypeStruct((T, C), F32), jax.ShapeDtypeStruct((1, D), F32)],
        compiler_params=_params("arbitrary"),
    )(dx1b, gl, ya, yc, w_out_t, w_ap_t, w_cp_t, dep)


def _conv_bwd(glu, u, dc, conv_w, ln_g, ln_b, dep):
    tm = CONV_TM
    nblk = T // tm

    def du_of(uv, dcv, g_ref, b_ref):
        mu = jnp.mean(uv, axis=-1, keepdims=True)
        xc = uv - mu
        var = jnp.mean(xc * xc, axis=-1, keepdims=True)
        rstd = lax.rsqrt(var + EPS)
        xh = xc * rstd
        y = xh * g_ref[...] + b_ref[...]
        sg = _sigmoid(y)
        dy = dcv * (sg * (1.0 + y * (1.0 - sg)))
        dxh = dy * g_ref[...]
        du = rstd * (dxh - jnp.mean(dxh, axis=-1, keepdims=True)
                     - xh * jnp.mean(dxh * xh, axis=-1, keepdims=True))
        return du, dy, xh

    def body(cur_ref, prev_ref, u_ref, un_ref, dc_ref, dcn_ref, w_ref, g_ref, b_ref, dep_ref,
             dglu_ref, dw_ref, dcb_ref, dg_ref, db_ref, zs_ref, dus_ref):
        i = pl.program_id(0)

        @pl.when(i == 0)
        def _():
            dw_ref[...] = jnp.zeros_like(dw_ref)
            dcb_ref[...] = jnp.zeros_like(dcb_ref)
            dg_ref[...] = jnp.zeros_like(dg_ref)
            db_ref[...] = jnp.zeros_like(db_ref)

        zprev = _glu(prev_ref[tm - HALO:tm, :])
        zs_ref[0, 0:HALO, :] = jnp.where(i > 0, zprev, 0.0)
        zs_ref[0, HALO:HALO + tm, :] = _glu(cur_ref[...])
        _shifted_copies(zs_ref)

        dun, _, _ = du_of(un_ref[0:HALO, :], dcn_ref[0:HALO, :], g_ref, b_ref)
        dus_ref[0, tm:tm + HALO, :] = jnp.where(i < nblk - 1, dun, 0.0)
        dg_acc = jnp.zeros((1, C), F32)
        db_acc = jnp.zeros((1, C), F32)
        dcb_acc = jnp.zeros((1, C), F32)
        for s in range(tm // CONV_SUB):
            rows = slice(s * CONV_SUB, (s + 1) * CONV_SUB)
            du, dy, xh = du_of(u_ref[rows, :], dc_ref[rows, :], g_ref, b_ref)
            dus_ref[0, rows, :] = du
            dg_acc = dg_acc + jnp.sum(dy * xh, axis=0, keepdims=True)
            db_acc = db_acc + jnp.sum(dy, axis=0, keepdims=True)
            dcb_acc = dcb_acc + jnp.sum(du, axis=0, keepdims=True)
        dg_ref[...] += dg_acc
        db_ref[...] += db_acc
        dcb_ref[...] += dcb_acc
        _shifted_copies(dus_ref)

        for j in range(KW):
            acc = jnp.zeros((CONV_SUB, C), F32)
            for s in range(tm // CONV_SUB):
                base = HALO + s * CONV_SUB - (KW - 1) + j
                acc = acc + dus_ref[0, s * CONV_SUB:(s + 1) * CONV_SUB, :] * _shifted_rows(zs_ref, base, CONV_SUB)
            dw_ref[j:j + 1, :] += jnp.sum(acc, axis=0, keepdims=True)

        for s in range(tm // CONV_SUB):
            rows = slice(s * CONV_SUB, (s + 1) * CONV_SUB)
            dz = jnp.zeros((CONV_SUB, C), F32)
            for j in range(KW):
                dz = dz + w_ref[j:j + 1, :] * _shifted_rows(dus_ref, s * CONV_SUB + (KW - 1) - j, CONV_SUB)
            a = cur_ref[rows, 0:C].astype(F32)
            sb = _sigmoid(cur_ref[rows, C:2 * C].astype(F32))
            dglu_ref[rows, 0:C] = (dz * sb).astype(BF16)
            dglu_ref[rows, C:2 * C] = (dz * a * sb * (1.0 - sb)).astype(BF16)

    nxt = lambda i: (jnp.minimum(i + 1, nblk - 1), 0)
    return pl.pallas_call(
        body, name="conv_bwd", grid=(nblk,),
        in_specs=[_rows(tm, 2 * C),
                  pl.BlockSpec((tm, 2 * C), lambda i: (jnp.maximum(i - 1, 0), 0)),
                  _rows(tm, C), pl.BlockSpec((tm, C), nxt),
                  _rows(tm, C), pl.BlockSpec((tm, C), nxt),
                  _whole((KW, C)), _whole((1, C)), _whole((1, C)), _whole((8, 128))],
        out_specs=[_rows(tm, 2 * C), _whole((KW, C)), _whole((1, C)), _whole((1, C)), _whole((1, C))],
        out_shape=[jax.ShapeDtypeStruct((T, 2 * C), BF16), jax.ShapeDtypeStruct((KW, C), F32),
                   jax.ShapeDtypeStruct((1, C), F32), jax.ShapeDtypeStruct((1, C), F32),
                   jax.ShapeDtypeStruct((1, C), F32)],
        scratch_shapes=[pltpu.VMEM((SUBLANES, HALO + tm, C), F32), pltpu.VMEM((SUBLANES, tm + HALO, C), F32)],
        compiler_params=_params("arbitrary"),
    )(glu, glu, u, u, dc, dc, conv_w, ln_g, ln_b, dep)


def _attn_bwd(qkv, o, do, lse, sinks):
    def body(sink_ref, qkv_ref, o_ref, do_ref, lse_ref, dq_ref, dkv_ref, ds_ref,
             s_ref, dp_ref, p_ref, dsb_ref):
        i = pl.program_id(0)

        @pl.when(i == 0)
        def _():
            dkv_ref[...] = jnp.zeros_like(dkv_ref)
            ds_ref[...] = jnp.zeros_like(ds_ref)

        r0 = pl.multiple_of(i * BLK, BLK)
        rb, valid = _band(i)
        for g in range(NKV):
            kband = qkv_ref[pl.ds(rb, BAND), AW + g * HD:AW + (g + 1) * HD]
            vband = qkv_ref[pl.ds(rb, BAND), AW + KVW + g * HD:AW + KVW + (g + 1) * HD]
            lse_parts, dl_parts = [], []
            for hh in range(GROUP):
                h = g * GROUP + hh
                hcol = slice(h * HD, (h + 1) * HD)
                doh = do_ref[:, hcol]
                s_ref[hh * BLK:(hh + 1) * BLK, :] = _dot_nt(qkv_ref[pl.ds(r0, BLK), hcol], kband)
                dp_ref[hh * BLK:(hh + 1) * BLK, :] = _dot_nt(doh, vband)
                lse_parts.append(lse_ref[:, h:h + 1])
                dl_parts.append(jnp.sum(doh.astype(F32) * o_ref[:, hcol].astype(F32), axis=-1, keepdims=True))
            lse = jnp.concatenate(lse_parts, axis=0)
            dl = jnp.concatenate(dl_parts, axis=0)
            p = jnp.where(valid, jnp.exp(s_ref[...] * SCALE - lse), 0.0)
            p_ref[...] = p.astype(BF16)
            dsb_ref[...] = (p * (dp_ref[...] - dl)).astype(BF16)
            dsink = -(jnp.exp(_sink_column(sink_ref, g) - lse) * dl)
            dk = jnp.zeros((BAND, HD), F32)
            dv = jnp.zeros((BAND, HD), F32)
            for hh in range(GROUP):
                h = g * GROUP + hh
                hcol = slice(h * HD, (h + 1) * HD)
                rows = slice(hh * BLK, (hh + 1) * BLK)
                dq_ref[:, hcol] = (_dot(dsb_ref[rows, :], kband) * SCALE).astype(BF16)
                dk = dk + _dot_tn(dsb_ref[rows, :], qkv_ref[pl.ds(r0, BLK), hcol])
                dv = dv + _dot_tn(p_ref[rows, :], do_ref[:, hcol])
                ds_ref[h:h + 1, :] += jnp.broadcast_to(jnp.sum(dsink[rows], axis=0, keepdims=True), (1, 128))
            dkv_ref[pl.ds(rb, BAND), g * HD:(g + 1) * HD] += dk * SCALE
            dkv_ref[pl.ds(rb, BAND), KVW + g * HD:KVW + (g + 1) * HD] += dv

    return pl.pallas_call(
        body, name="attn_bwd", grid=(T // BLK,),
        in_specs=[pl.BlockSpec(memory_space=pltpu.SMEM), _whole((T, QKVW)),
                  _rows(BLK, AW), _rows(BLK, AW), _rows(BLK, NQ)],
        out_specs=[_rows(BLK, AW), _whole((T, 2 * KVW)), _whole((NQ, 128))],
        out_shape=[jax.ShapeDtypeStruct((T, AW), BF16), jax.ShapeDtypeStruct((T, 2 * KVW), F32),
                   jax.ShapeDtypeStruct((NQ, 128), F32)],
        scratch_shapes=[pltpu.VMEM((GROWS, BAND), F32), pltpu.VMEM((GROWS, BAND), F32),
                        pltpu.VMEM((GROWS, BAND), BF16), pltpu.VMEM((GROWS, BAND), BF16)],
        compiler_params=_params("arbitrary"),
    )(sinks, qkv, o, do, lse)


PROJ_PARTS = [(0, AW), (AW, QKVW), (GLU_OFF, GATE_OFF), (GATE_OFF, INW)]


def _in_proj_bwd(dq, dkv, dglu, dgl, x, dx1, g_mix, w_in_t):
    tm = 256

    def body(dq_ref, dkv_ref, dglu_ref, dgl_ref, x_ref, dx1_ref, g_ref, w_ref, gx_ref, dg_ref, db_ref):
        i = pl.program_id(0)

        @pl.when(i == 0)
        def _():
            dg_ref[...] = jnp.zeros_like(dg_ref)
            db_ref[...] = jnp.zeros_like(db_ref)

        dh = jnp.zeros((tm, D), F32)
        for part_ref, (lo, hi) in zip((dq_ref, dkv_ref, dglu_ref, dgl_ref), PROJ_PARTS):
            part = part_ref[...]
            dh = dh + _dot(part.astype(BF16), w_ref[lo:hi, :])
            db_ref[:, lo:hi] += jnp.sum(part.astype(F32), axis=0, keepdims=True)
        xv = x_ref[...]
        r = lax.rsqrt(jnp.mean(xv * xv, axis=-1, keepdims=True) + EPS)
        xh = xv * r
        dhg = dh * g_ref[...]
        gx_ref[...] = dx1_ref[...] + r * (dhg - xh * jnp.mean(dhg * xh, axis=-1, keepdims=True))
        dg_ref[...] += jnp.sum(dh * xh, axis=0, keepdims=True)

    return pl.pallas_call(
        body, name="in_proj_bwd", grid=(T // tm,),
        in_specs=[_rows(tm, AW), _rows(tm, 2 * KVW), _rows(tm, 2 * C), _rows(tm, 2 * D),
                  _rows(tm, D), _rows(tm, D), _whole((1, D)), _const((INW, D))],
        out_specs=[_rows(tm, D), _whole((1, D)), _whole((1, INW))],
        out_shape=[jax.ShapeDtypeStruct((T, D), F32), jax.ShapeDtypeStruct((1, D), F32),
                   jax.ShapeDtypeStruct((1, INW), F32)],
        compiler_params=_params("arbitrary"),
    )(dq, dkv, dglu, dgl, x, dx1, g_mix, w_in_t)


def _grad_w_in_t(h, dq, dkv, dglu, dgl):
    tn, chunk = 512, 256

    def body(h_ref, dq_ref, dkv_ref, dglu_ref, dgl_ref, o_ref, pt_ref):
        @pl.when(pl.program_id(0) == 0)
        def _():
            for part_ref, (lo, hi) in zip((dq_ref, dkv_ref, dglu_ref, dgl_ref), PROJ_PARTS):
                for c0 in range(0, hi - lo, chunk):
                    pt_ref[lo + c0:lo + c0 + chunk, :] = part_ref[:, c0:c0 + chunk].astype(BF16).T

        hv = h_ref[...]
        for r0 in range(0, INW, QKVW):
            o_ref[r0:r0 + QKVW, :] = _dot(pt_ref[r0:r0 + QKVW, :], hv).astype(BF16)

    return pl.pallas_call(
        body, name="grad_w_in", grid=(D // tn,),
        in_specs=[pl.BlockSpec((T, tn), lambda j: (0, j)), _const((T, AW)), _const((T, 2 * KVW)),
                  _const((T, 2 * C)), _const((T, 2 * D))],
        out_specs=pl.BlockSpec((INW, tn), lambda j: (0, j)),
        out_shape=jax.ShapeDtypeStruct((INW, D), BF16),
        scratch_shapes=[pltpu.VMEM((INW, T), BF16)],
        compiler_params=_params("arbitrary"),
    )(h, dq, dkv, dglu, dgl)


def _grad_w(a, b, name, tk, tn, col_sharded):
    k, n = a.shape[1], b.shape[1]

    def body(a_ref, b_ref, o_ref, at_ref):
        @pl.when(pl.program_id(1) == 0)
        def _():
            at_ref[...] = a_ref[...].T

        o_ref[...] = _dot(at_ref[...], b_ref[...]).astype(BF16)

    if col_sharded:
        per = n // N_CHIPS // tn
        shape = (N_CHIPS, k, n // N_CHIPS)
        out_map = lambda i, j: (j // per, i, j % per)
    else:
        shape = (1, k, n)
        out_map = lambda i, j: (0, i, j)
    out = pl.pallas_call(
        body, name=name, grid=(k // tk, n // tn),
        in_specs=[pl.BlockSpec((T, tk), lambda i, j: (0, i)), pl.BlockSpec((T, tn), lambda i, j: (0, j))],
        out_specs=pl.BlockSpec((None, tk, tn), out_map),
        out_shape=jax.ShapeDtypeStruct(shape, BF16),
        scratch_shapes=[pltpu.VMEM((tk, T), BF16)],
        compiler_params=_params("parallel", "arbitrary"),
    )(a, b)
    return out if col_sharded else out.reshape(N_CHIPS, k // N_CHIPS, n)


HBM_SPEC = pl.BlockSpec(memory_space=pltpu.HBM)


def _place():
    x, y, c = lax.axis_index("x"), lax.axis_index("y"), lax.axis_index("c")
    chips = [(1 - x, y), (x, 1 - y), (1 - x, 1 - y)]
    return x, y, c, chips


SEM_SPEC = pl.BlockSpec(memory_space=pltpu.SEMAPHORE)
ANY_SPEC = pl.BlockSpec(memory_space=pl.ANY)
VMEM_SPEC = pl.BlockSpec(memory_space=pltpu.VMEM)
EFFECT = pltpu.SideEffectType.DATAFLOW_SIDE_EFFECTING


def _gather_ends(src, land, x, y, c, chips):
    kh = src.shape[0] // 2
    s_me = 2 * x + y
    ends = [(src.at[pl.ds(c * kh, kh)], land.at[s_me, pl.ds(c * kh, kh)], (*chip, c)) for chip in chips]
    return ends + [(src, land.at[s_me], (x, y, 1 - c))]


def _reduce_ends(src, land, x, y, c, chips):
    return [(src.at[2 * chip[0] + chip[1]], land.at[j], (*chip, c)) for j, chip in enumerate(chips)]


def _chip_copies(ends, srcs, lands, send_sems, recv_sems):
    x, y, c, chips = _place()
    copies = []
    for src, land in zip(srcs, lands):
        for s, d, to in ends(src, land, x, y, c, chips):
            k = len(copies)
            copies.append(pltpu.make_async_remote_copy(
                src_ref=s, dst_ref=d, send_sem=send_sems.at[k], recv_sem=recv_sems.at[k],
                device_id=to, device_id_type=MESH))
    return copies


GATHER_PEERS, REDUCE_PEERS = 4, 3


def _chip_start(name, ends, peers, srcs, lands):
    n = len(srcs)

    def body(*refs):
        copies = _chip_copies(ends, refs[:n], refs[n:2 * n], refs[2 * n], refs[2 * n + 1])
        for cp in copies:
            cp.start()
        token = refs[-1]
        token[...] = jnp.zeros_like(token)

    hbm = lambda a: pltpu.HBM(a.shape, a.dtype)
    res = pl.pallas_call(
        body, name=name,
        out_shape=(pltpu.SemaphoreType.DMA((peers * n,)), pltpu.SemaphoreType.DMA((peers * n,)),
                   *[hbm(a) for a in srcs], *[hbm(a) for a in lands],
                   jax.ShapeDtypeStruct((8, 128), F32)),
        in_specs=[HBM_SPEC] * (2 * n),
        out_specs=(SEM_SPEC, SEM_SPEC, *[HBM_SPEC] * (2 * n), VMEM_SPEC),
        input_output_aliases={i: 2 + i for i in range(2 * n)},
        compiler_params=pltpu.CompilerParams(has_side_effects=EFFECT),
    )(*[pltpu.with_memory_space_constraint(a, pltpu.HBM) for a in (*srcs, *lands)])
    return res[0], res[1], list(res[2:2 + n]), list(res[2 + n:2 + 2 * n]), res[-1]


def _chip_wait(name, ends, send_sems, recv_sems, srcs, lands, after):
    n, na = len(srcs), len(after)

    def body(*refs):
        copies = _chip_copies(ends, refs[:n], refs[n:2 * n], refs[2 * n], refs[2 * n + 1])
        for cp in copies:
            cp.wait_send()
            cp.wait_recv()

    hbm = lambda a: pltpu.HBM(a.shape, a.dtype)
    res = pl.pallas_call(
        body, name=name,
        out_shape=tuple(hbm(a) for a in (*srcs, *lands)),
        in_specs=[HBM_SPEC] * (2 * n) + [SEM_SPEC, SEM_SPEC] + [ANY_SPEC] * na,
        out_specs=tuple([HBM_SPEC] * (2 * n)),
        input_output_aliases={i: i for i in range(2 * n)},
        compiler_params=pltpu.CompilerParams(has_side_effects=EFFECT),
    )(*srcs, *lands, send_sems, recv_sems, *after)
    return list(res[:n]), list(res[n:])


def _pair_forward(name, lands):
    n = len(lands)

    def body(*refs):
        bufs = refs[n:2 * n]
        send_sems, recv_sems = refs[2 * n:]
        x, y, c, chips = _place()

        def copy(w, j, cc, to):
            kh = bufs[w].shape[1] // 2
            blk = bufs[w].at[2 * chips[j][0] + chips[j][1], pl.ds(cc * kh, kh)]
            return pltpu.make_async_remote_copy(
                src_ref=blk, dst_ref=blk, send_sem=send_sems.at[3 * w + j],
                recv_sem=recv_sems.at[3 * w + j], device_id=to, device_id_type=MESH)

        sends = [copy(w, j, c, (x, y, 1 - c)) for w in range(n) for j in range(3)]
        for cp in sends:
            cp.start()
        for w in range(n):
            for j in range(3):
                copy(w, j, 1 - c, (x, y, c)).wait_recv()
        for cp in sends:
            cp.wait_send()

    return pl.pallas_call(
        body, name=name,
        in_specs=[HBM_SPEC] * n, out_specs=[HBM_SPEC] * n,
        out_shape=[jax.ShapeDtypeStruct(a.shape, a.dtype) for a in lands],
        input_output_aliases={w: w for w in range(n)},
        scratch_shapes=[pltpu.SemaphoreType.DMA((3 * n,)), pltpu.SemaphoreType.DMA((3 * n,))],
    )(*lands)


def _pair_exchange(name, grads):
    n = len(grads)

    def body(*refs):
        ins, outs = refs[:n], refs[n:2 * n]
        send_sems, recv_sems = refs[2 * n:]
        x, y, c, _ = _place()
        copies = []
        for w in range(n):
            kh = ins[w].shape[1] // 2
            cp = pltpu.make_async_remote_copy(
                src_ref=ins[w].at[:, pl.ds((1 - c) * kh, kh)], dst_ref=outs[w],
                send_sem=send_sems.at[w], recv_sem=recv_sems.at[w],
                device_id=(x, y, 1 - c), device_id_type=MESH)
            cp.start()
            copies.append(cp)
        for cp in copies:
            cp.wait()

    return pl.pallas_call(
        body, name=name,
        in_specs=[HBM_SPEC] * n, out_specs=[HBM_SPEC] * n,
        out_shape=[jax.ShapeDtypeStruct((g.shape[0], g.shape[1] // 2, g.shape[2]), g.dtype) for g in grads],
        scratch_shapes=[pltpu.SemaphoreType.DMA((n,)), pltpu.SemaphoreType.DMA((n,))],
    )(*grads)


def _row_tile(k):
    for t in (256, 240, 128, 176, 64, 32, 16):
        if k % t == 0:
            return t
    raise ValueError(k)


def _pair_sum(c_idx, g, got, name):
    _, k, n = g.shape
    kh = k // 2
    tm = _row_tile(kh)
    nb = kh // tm

    def body(c_ref, g_ref, r_ref, o_ref):
        o_ref[...] = (g_ref[...].astype(F32) + r_ref[...].astype(F32)).astype(BF16)

    return pl.pallas_call(
        body, name=name,
        grid_spec=pltpu.PrefetchScalarGridSpec(
            num_scalar_prefetch=1, grid=(N_CHIPS, nb),
            in_specs=[pl.BlockSpec((1, tm, n), lambda s, i, c_ref: (s, c_ref[0] * nb + i, 0)),
                      pl.BlockSpec((1, tm, n), lambda s, i, c_ref: (s, i, 0))],
            out_specs=pl.BlockSpec((1, tm, n), lambda s, i, c_ref: (s, i, 0))),
        out_shape=jax.ShapeDtypeStruct((N_CHIPS, kh, n), BF16),
        compiler_params=_params("parallel", "parallel"),
    )(c_idx, g, got)


def _chip_sum(sc_idx, mine, got, name):
    _, kh, n = mine.shape
    tm = _row_tile(kh)

    def body(sc_ref, m_ref, r_ref, o_ref):
        acc = m_ref[0].astype(F32)
        for j in range(3):
            acc = acc + r_ref[j].astype(F32)
        o_ref[0] = acc

    return pl.pallas_call(
        body, name=name,
        grid_spec=pltpu.PrefetchScalarGridSpec(
            num_scalar_prefetch=1, grid=(kh // tm,),
            in_specs=[pl.BlockSpec((1, tm, n), lambda i, sc_ref: (sc_ref[0], i, 0)),
                      pl.BlockSpec((3, tm, n), lambda i, sc_ref: (0, i, 0))],
            out_specs=pl.BlockSpec((1, tm, n), lambda i, sc_ref: (sc_ref[1], i, 0))),
        out_shape=jax.ShapeDtypeStruct((2, kh, n), F32),
        compiler_params=_params("parallel"),
    )(sc_idx, mine, got)


def _pair_share(name, halves):
    n = len(halves)

    def body(*refs):
        bufs = refs[n:2 * n]
        send_sems, recv_sems = refs[2 * n:]
        x, y, c, _ = _place()
        copies = []
        for w in range(n):
            cp = pltpu.make_async_remote_copy(
                src_ref=bufs[w].at[c], dst_ref=bufs[w].at[c],
                send_sem=send_sems.at[w], recv_sem=recv_sems.at[w],
                device_id=(x, y, 1 - c), device_id_type=MESH)
            cp.start()
            copies.append(cp)
        for w in range(n):
            copies[w].wait_send()
            pltpu.make_async_remote_copy(
                src_ref=bufs[w].at[1 - c], dst_ref=bufs[w].at[1 - c],
                send_sem=send_sems.at[w], recv_sem=recv_sems.at[w],
                device_id=(x, y, c), device_id_type=MESH).wait_recv()

    return pl.pallas_call(
        body, name=name,
        in_specs=[HBM_SPEC] * n, out_specs=[HBM_SPEC] * n,
        out_shape=[jax.ShapeDtypeStruct(h.shape, h.dtype) for h in halves],
        input_output_aliases={w: w for w in range(n)},
        scratch_shapes=[pltpu.SemaphoreType.DMA((n,)), pltpu.SemaphoreType.DMA((n,))],
    )(*halves)


def _gather_small(blocks):
    n = len(blocks)

    def body(*refs):
        ins, outs = refs[:n], refs[n:2 * n]
        send_sems, recv_sems, local_sems = refs[2 * n:]
        x, y, c, chips = _place()
        me, sibling = (x, y, c), (x, y, 1 - c)

        def rows(w, px, py, pc):
            m = ins[w].shape[0]
            return outs[w].at[pl.ds((4 * px + 2 * py + pc) * m, m), :]

        def copy(w, k, block, to, src=None):
            return pltpu.make_async_remote_copy(
                src_ref=rows(w, *block) if src is None else src, dst_ref=rows(w, *block),
                send_sem=send_sems.at[7 * w + k], recv_sem=recv_sems.at[7 * w + k],
                device_id=to, device_id_type=MESH)

        started, owns = [], []
        for w in range(n):
            own = pltpu.make_async_copy(ins[w], rows(w, *me), local_sems.at[w])
            own.start()
            owns.append(own)
            first = [copy(w, 0, me, sibling, src=ins[w])]
            first += [copy(w, 1 + j, me, (*chip, c), src=ins[w]) for j, chip in enumerate(chips)]
            for cp in first:
                cp.start()
            started += first
        for w in range(n):
            for j, chip in enumerate(chips):
                copy(w, 1 + j, (*chip, c), me).wait_recv()
                cp = copy(w, 4 + j, (*chip, c), sibling)
                cp.start()
                started.append(cp)
        for w in range(n):
            copy(w, 0, sibling, me).wait_recv()
            for j, chip in enumerate(chips):
                copy(w, 4 + j, (*chip, 1 - c), me).wait_recv()
        for cp in started:
            cp.wait_send()
        for own in owns:
            own.wait()

    vmem = pl.BlockSpec(memory_space=pltpu.VMEM)
    return pl.pallas_call(
        body, name="gather_small",
        in_specs=[vmem] * n, out_specs=[vmem] * n,
        out_shape=[jax.ShapeDtypeStruct((8 * b.shape[0], b.shape[1]), b.dtype) for b in blocks],
        scratch_shapes=[pltpu.SemaphoreType.DMA((7 * n,)), pltpu.SemaphoreType.DMA((7 * n,)),
                        pltpu.SemaphoreType.DMA((n,))],
    )(*blocks)


def _adamw_math(w, g, m, v):
    m = ADAM_B1 * m + (1.0 - ADAM_B1) * g
    v = ADAM_B2 * v + (1.0 - ADAM_B2) * (g * g)
    m_hat = m / (1.0 - ADAM_B1 ** ADAM_STEP)
    v_hat = v / (1.0 - ADAM_B2 ** ADAM_STEP)
    delta = -ADAM_LR * (m_hat / (jnp.sqrt(v_hat) + ADAM_EPS) + ADAM_WD * w)
    return delta, m, v


def _adamw(w, g, m, v, name):
    k, n = w.shape
    tm = k // 4

    def body(w_ref, g_ref, m_ref, v_ref, go_ref, d_ref, mo_ref, vo_ref):
        g = g_ref[...]
        d, mm, vv = _adamw_math(w_ref[...], g, m_ref[...], v_ref[...])
        go_ref[...] = g
        d_ref[...] = d
        mo_ref[...] = mm
        vo_ref[...] = vv

    spec = pl.BlockSpec((tm, n), lambda i: (i, 0))
    shp = jax.ShapeDtypeStruct((k, n), F32)
    return pl.pallas_call(
        body, name=name, grid=(4,), in_specs=[spec] * 4, out_specs=[spec] * 4,
        out_shape=[shp] * 4, compiler_params=_params("parallel"),
    )(w, g, m, v)


VEC_SLOTS = {
    "g_mix_norm": (0, 0, D), "b_conv_proj": (0, D, D), "g_ffn_norm": (0, 2 * D, D),
    "g_final": (0, 3 * D, D), "b_in": (1, 0, INW), "conv_b": (2, 0, C), "ln_g": (2, C, C),
    "ln_b": (2, 2 * C, C), "sinks": (2, 3 * C, NQ), "loss": (2, 3 * C + 128, 1),
}
VEC_ROWS, VEC_COLS = 8, 4 * D
CW_ROWS = 32
SMALL_NAMES = ["g_mix_norm", "b_in", "sinks", "conv_w", "conv_b", "ln_g", "ln_b",
               "b_conv_proj", "g_ffn_norm", "g_final"]
CW_LANES = C // N_CHIPS


def _pack_small(gs, loss):
    row0 = jnp.concatenate([gs["g_mix_norm"], gs["b_conv_proj"], gs["g_ffn_norm"], gs["g_final"]], axis=1)
    row1 = jnp.pad(gs["b_in"], ((0, 0), (0, VEC_COLS - INW)))
    row2 = jnp.concatenate([gs["conv_b"], gs["ln_g"], gs["ln_b"],
                            jnp.pad(gs["sinks"], ((0, 0), (0, 128 - NQ))),
                            jnp.pad(loss.reshape(1, 1), ((0, 0), (0, VEC_COLS - 3 * C - 129)))], axis=1)
    vec = jnp.concatenate([row0, row1, row2, jnp.zeros((VEC_ROWS - 3, VEC_COLS), F32)], axis=0)
    cw = jnp.pad(gs["conv_w"], ((0, CW_ROWS - KW), (0, 0)))
    return vec, cw


def _small_update(s_idx, vec_all, cw_all, wmv):
    nsm = len(SMALL_NAMES)

    def body(s_ref, vec_ref, cw_ref, *refs):
        ins = refs[:3 * nsm]
        outs = refs[3 * nsm:7 * nsm]
        loss_ref = refs[7 * nsm]

        def total(slot):
            row, lane, width = slot
            acc = vec_ref[row:row + 1, lane:lane + width]
            for k in range(1, 8):
                acc = acc + vec_ref[k * VEC_ROWS + row:k * VEC_ROWS + row + 1, lane:lane + width]
            return acc

        loss_ref[...] = jnp.broadcast_to(total(VEC_SLOTS["loss"]), loss_ref.shape)
        for p, name in enumerate(SMALL_NAMES):
            w_ref, m_ref, v_ref = ins[3 * p:3 * p + 3]
            g_ref, d_ref, mo_ref, vo_ref = outs[4 * p:4 * p + 4]
            if name == "conv_w":
                g = jnp.zeros((KW, CW_LANES), F32)
                for s in range(N_CHIPS):
                    cand = cw_ref[0:KW, s * CW_LANES:(s + 1) * CW_LANES]
                    for k in range(1, 8):
                        cand = cand + cw_ref[k * CW_ROWS:k * CW_ROWS + KW, s * CW_LANES:(s + 1) * CW_LANES]
                    g = jnp.where(s_ref[0] == s, cand, g)
            else:
                g = total(VEC_SLOTS[name])
            d, mm, vv = _adamw_math(w_ref[...], g, m_ref[...], v_ref[...])
            g_ref[...] = g
            d_ref[...] = d
            mo_ref[...] = mm
            vo_ref[...] = vv

    vmem = pl.BlockSpec(memory_space=pltpu.VMEM)
    flat = [a for t in wmv for a in t]
    out_shape = []
    for w, _, _ in wmv:
        out_shape += [jax.ShapeDtypeStruct(w.shape, F32)] * 4
    out_shape.append(jax.ShapeDtypeStruct((1, 128), F32))
    res = pl.pallas_call(
        body, name="small_update",
        in_specs=[pl.BlockSpec(memory_space=pltpu.SMEM)] + [vmem] * (2 + len(flat)),
        out_specs=[vmem] * len(out_shape), out_shape=out_shape,
    )(s_idx, vec_all, cw_all, *flat)
    return [tuple(res[4 * p:4 * p + 4]) for p in range(nsm)], res[4 * nsm]


WEIGHT_ORDER = ["g_mix_norm", "w_in", "b_in", "sinks", "conv_w", "conv_b", "ln_g", "ln_b",
                "w_attn_proj", "w_conv_proj", "b_conv_proj", "w_out", "g_ffn_norm", "w_ffn_in",
                "w_ffn_down", "g_final"]


def kernel(x, g_mix_norm, w_in, b_in, sinks, conv_w, conv_b, ln_g, ln_b, w_attn_proj, w_conv_proj, b_conv_proj, w_out, g_ffn_norm, w_ffn_in, w_ffn_down, g_final, loss_target, m_g_mix_norm, m_w_in, m_b_in, m_sinks, m_conv_w, m_conv_b, m_ln_g, m_ln_b, m_w_attn_proj, m_w_conv_proj, m_b_conv_proj, m_w_out, m_g_ffn_norm, m_w_ffn_in, m_w_ffn_down, m_g_final, v_g_mix_norm, v_w_in, v_b_in, v_sinks, v_conv_w, v_conv_b, v_ln_g, v_ln_b, v_w_attn_proj, v_w_conv_proj, v_b_conv_proj, v_w_out, v_g_ffn_norm, v_w_ffn_in, v_w_ffn_down, v_g_final):
    w = dict(g_mix_norm=g_mix_norm, w_in=w_in, b_in=b_in, sinks=sinks, conv_w=conv_w, conv_b=conv_b,
             ln_g=ln_g, ln_b=ln_b, w_attn_proj=w_attn_proj, w_conv_proj=w_conv_proj,
             b_conv_proj=b_conv_proj, w_out=w_out, g_ffn_norm=g_ffn_norm, w_ffn_in=w_ffn_in,
             w_ffn_down=w_ffn_down, g_final=g_final)
    m = dict(g_mix_norm=m_g_mix_norm, w_in=m_w_in, b_in=m_b_in, sinks=m_sinks, conv_w=m_conv_w,
             conv_b=m_conv_b, ln_g=m_ln_g, ln_b=m_ln_b, w_attn_proj=m_w_attn_proj,
             w_conv_proj=m_w_conv_proj, b_conv_proj=m_b_conv_proj, w_out=m_w_out,
             g_ffn_norm=m_g_ffn_norm, w_ffn_in=m_w_ffn_in, w_ffn_down=m_w_ffn_down, g_final=m_g_final)
    v = dict(g_mix_norm=v_g_mix_norm, w_in=v_w_in, b_in=v_b_in, sinks=v_sinks, conv_w=v_conv_w,
             conv_b=v_conv_b, ln_g=v_ln_g, ln_b=v_ln_b, w_attn_proj=v_w_attn_proj,
             w_conv_proj=v_w_conv_proj, b_conv_proj=v_b_conv_proj, w_out=v_w_out,
             g_ffn_norm=v_g_ffn_norm, w_ffn_in=v_w_ffn_in, w_ffn_down=v_w_ffn_down, g_final=v_g_final)

    c_idx = lax.axis_index("c").astype(jnp.int32).reshape(1)
    s_idx = (2 * lax.axis_index("x") + lax.axis_index("y")).astype(jnp.int32).reshape(1)

    sc_idx = jnp.concatenate([s_idx, c_idx])
    out_g, out_d, out_m, out_v = {}, {}, {}, {}

    def gather_start(tag, shards):
        lands = [lax.empty((N_CHIPS,) + s.shape, s.dtype) for s in shards]
        return _chip_start("gather_start_" + tag, _gather_ends, GATHER_PEERS, shards, lands)

    def gather_finish(tag, state, after):
        send_sems, recv_sems, shards, lands, _ = state
        shards, lands = _chip_wait("gather_wait_" + tag, _gather_ends, send_sems, recv_sems, shards, lands, after)
        return _pair_forward("pair_forward_" + tag, lands)

    names_b = ["w_attn_proj", "w_conv_proj", "w_out", "w_ffn_in", "w_ffn_down"]
    big = {name: (w[name][0], m[name][0], v[name][0]) for name in names_b}
    big["w_in"] = (w_in[0].T, m_w_in[0].T, v_w_in[0].T)
    state_a = gather_start("a", [big["w_in"][0].astype(BF16), jnp.pad(conv_w[0], ((0, CW_ROWS - KW), (0, 0)))])
    state_b = gather_start("b", [big[name][0].astype(BF16) for name in names_b])
    got_a = gather_finish("a", state_a, [state_b[4]])
    w_in_t_full = got_a[0].reshape(INW, D)
    conv_w_full = got_a[1].transpose(1, 0, 2).reshape(CW_ROWS, C)[:KW]

    xs, target = x[0], loss_target[0]
    g_final2 = g_final.reshape(1, D)
    h, qkv, glu, gl = _in_proj(xs, g_mix_norm, w_in_t_full, b_in)
    o, lse = _attn_fwd(qkv, sinks)
    u, cact = _conv_fwd(glu, conv_w_full, conv_b, ln_g, ln_b)
    w_ap4, w_cp4, w_out4, w_fi4, w_dn4 = gather_finish("b", state_b, [o, cact])
    w_out_full, w_dn_full = w_out4.reshape(D, D), w_dn4.reshape(DFF, D)
    ya, yc, mg, x1 = _mix_out(xs, o, cact, gl, w_ap4, w_cp4, b_conv_proj, w_out_full)
    h2, gu, act = _ffn_in(x1, g_ffn_norm, w_fi4)
    dx2, dx2b, dg_final, loss_part = _ffn_out_loss(x1, act, w_dn_full, g_final2, target)

    def reduce_start(tag, names, grads):
        from_sibling = _pair_exchange("pair_exchange_" + tag, grads)
        pair = [_pair_sum(c_idx, g, r, "pair_sum_" + name) for name, g, r in zip(names, grads, from_sibling)]
        lands = [lax.empty((3,) + p.shape[1:], p.dtype) for p in pair]
        return _chip_start("chip_start_" + tag, _reduce_ends, REDUCE_PEERS, pair, lands)

    def reduce_finish(tag, names, state, after):
        send_sems, recv_sems, pair, lands, _ = state
        pair, lands = _chip_wait("chip_wait_" + tag, _reduce_ends, send_sems, recv_sems, pair, lands, after)
        halves = [_chip_sum(sc_idx, p, r, "chip_sum_" + name) for name, p, r in zip(names, pair, lands)]
        for name, both in zip(names, _pair_share("pair_share_" + tag, halves)):
            wv, mv, vv = big[name]
            res = _adamw(wv, both.reshape(wv.shape), mv, vv, "adamw_" + name)
            if name == "w_in":
                res = [a.T for a in res]
            out_g[name], out_d[name], out_m[name], out_v[name] = [a[None] for a in res]

    dgu, dx1, dx1b, dg_ffn = _ffn_bwd(dx2, dx2b, gu, x1, g_ffn_norm, w_dn_full, w_fi4)
    names_1 = ["w_ffn_in", "w_ffn_down"]
    state_1 = reduce_start("1", names_1, [_grad_w(h2, dgu, "grad_w_ffn_in", 512, FSH, True),
                                          _grad_w(act, dx2b, "grad_w_ffn_down", 256, D, False)])
    dya, dyc, dgl, do, dc, db_cp = _mix_bwd(dx1b, gl, ya, yc, w_out_full, w_ap4, w_cp4, state_1[4])
    names_2 = ["w_out", "w_attn_proj", "w_conv_proj"]
    state_2 = reduce_start("2", names_2, [_grad_w(mg, dx1b, "grad_w_out", 512, D, False),
                                          _grad_w(o, dya, "grad_w_attn_proj", 512, 256, True),
                                          _grad_w(cact, dyc, "grad_w_conv_proj", 512, 256, True)])
    dglu, dconv_w, dconv_b, dln_g, dln_b = _conv_bwd(glu, u, dc, conv_w_full, ln_g, ln_b, state_2[4])
    dq, dkv, dsinks = _attn_bwd(qkv, o, do, lse, sinks)
    grad_x, dg_mix, db_in = _in_proj_bwd(dq, dkv, dglu, dgl, xs, dx1, g_mix_norm, w_in_t_full)
    names_3 = ["w_in"]
    gw_in_t = _grad_w_in_t(h, dq, dkv, dglu, dgl)
    state_3 = reduce_start("3", names_3, [gw_in_t.reshape(N_CHIPS, INW // N_CHIPS, D)])

    reduce_finish("1", names_1, state_1, [state_3[4]])
    reduce_finish("2", names_2, state_2, [out_d["w_ffn_down"]])

    gs = {"g_mix_norm": dg_mix, "b_in": db_in, "sinks": dsinks[:, 0].reshape(1, NQ),
          "conv_w": dconv_w, "conv_b": dconv_b, "ln_g": dln_g, "ln_b": dln_b,
          "b_conv_proj": db_cp, "g_ffn_norm": dg_ffn, "g_final": dg_final}
    vec, cw = _pack_small(gs, loss_part[0, 0])
    vec_all, cw_all = _gather_small([vec, cw])

    def view(a, name):
        if name == "conv_w":
            return a[0]
        if name == "g_final":
            return a.reshape(1, D)
        return a

    wmv = [(view(w[name], name), view(m[name], name), view(v[name], name)) for name in SMALL_NAMES]
    small_out, loss_row = _small_update(s_idx, vec_all, cw_all, wmv)
    for name, (g, d, mm, vv) in zip(SMALL_NAMES, small_out):
        shape = w[name].shape
        out_g[name], out_d[name], out_m[name], out_v[name] = (
            g.reshape(shape), d.reshape(shape), mm.reshape(shape), vv.reshape(shape))

    reduce_finish("3", names_3, state_3, [loss_row, out_d["w_conv_proj"]])

    loss = loss_row[0, 0]
    return (loss, grad_x[None], *[out_g[k] for k in WEIGHT_ORDER], *[out_d[k] for k in WEIGHT_ORDER],
            *[out_m[k] for k in WEIGHT_ORDER], *[out_v[k] for k in WEIGHT_ORDER])
```

```python
import functools

import jax
import jax.numpy as jnp
from jax import lax
from jax.experimental import pallas as pl
from jax.experimental.pallas import tpu as pltpu

F32 = jnp.float32
BF16 = jnp.bfloat16

T = 2048
D = 1024
HD = 64
NQ = 8
NKV = 2
GROUP = NQ // NKV
BLK = 128
AW = NQ * HD
KVW = NKV * HD
C = 512
KW = 31
QKVW = AW + 2 * KVW
GLU_OFF = QKVW
GATE_OFF = GLU_OFF + 2 * C
INW = GATE_OFF + 2 * D
DFF = 2816
EPS = 1e-5
NEG = -1e30
SCALE = HD ** -0.5
HALO = 32
N_CHIPS = 4
FSH = 2 * DFF // N_CHIPS

ADAM_LR = 0.001
ADAM_B1 = 0.9
ADAM_B2 = 0.999
ADAM_EPS = 1e-08
ADAM_WD = 0.01
ADAM_STEP = 10

VMEM_LIMIT = 56 * 1024 * 1024
MESH = pl.DeviceIdType.MESH


def _params(*sem):
    return pltpu.CompilerParams(dimension_semantics=sem, vmem_limit_bytes=VMEM_LIMIT)


def _dot(a, b):
    return jnp.dot(a, b, preferred_element_type=F32)


def _dot_nt(a, b):
    return lax.dot_general(a, b, (((1,), (1,)), ((), ())), preferred_element_type=F32)


def _dot_tn(a, b):
    return lax.dot_general(a, b, (((0,), (0,)), ((), ())), preferred_element_type=F32)


def _sigmoid(v):
    return 1.0 / (1.0 + jnp.exp(-v))


def _rows(tm, n):
    return pl.BlockSpec((tm, n), lambda i: (i, 0))


def _whole(shape):
    return pl.BlockSpec(shape, lambda i: tuple(0 for _ in shape))


def _in_proj(x, g_mix, w_in_t, b_in):
    tm = 256

    def body(x_ref, g_ref, w_ref, b_ref, h_ref, qkv_ref, glu_ref, gl_ref):
        xv = x_ref[...]
        r = lax.rsqrt(jnp.mean(xv * xv, axis=-1, keepdims=True) + EPS)
        h = (xv * r * g_ref[...]).astype(BF16)
        h_ref[...] = h
        qkv_ref[...] = (_dot_nt(h, w_ref[0:GLU_OFF, :]) + b_ref[:, 0:GLU_OFF]).astype(BF16)
        glu_ref[...] = (_dot_nt(h, w_ref[GLU_OFF:GATE_OFF, :]) + b_ref[:, GLU_OFF:GATE_OFF]).astype(BF16)
        gl_ref[...] = (_dot_nt(h, w_ref[GATE_OFF:INW, :]) + b_ref[:, GATE_OFF:INW]).astype(BF16)

    return pl.pallas_call(
        body, name="in_proj", grid=(T // tm,),
        in_specs=[_rows(tm, D), _whole((1, D)), _whole((INW, D)), _whole((1, INW))],
        out_specs=[_rows(tm, D), _rows(tm, QKVW), _rows(tm, 2 * C), _rows(tm, 2 * D)],
        out_shape=[jax.ShapeDtypeStruct((T, D), BF16), jax.ShapeDtypeStruct((T, QKVW), BF16),
                   jax.ShapeDtypeStruct((T, 2 * C), BF16), jax.ShapeDtypeStruct((T, 2 * D), BF16)],
        compiler_params=_params("parallel"),
    )(x, g_mix, w_in_t, b_in)


GROWS = GROUP * BLK
BAND = 2 * BLK


def _band(i):
    rb = pl.multiple_of(jnp.maximum(i - 1, 0) * BLK, BLK)
    row = lax.broadcasted_iota(jnp.int32, (GROWS, BAND), 0)
    kpos = rb + lax.broadcasted_iota(jnp.int32, (GROWS, BAND), 1)
    qpos = i * BLK + jnp.bitwise_and(row, BLK - 1)
    return rb, jnp.logical_and(kpos <= qpos, kpos > qpos - BLK)


def _sink_column(sink_ref, g):
    head = lax.shift_right_logical(lax.broadcasted_iota(jnp.int32, (GROWS, 1), 0), 7)
    col = jnp.full((GROWS, 1), sink_ref[0, g * GROUP], F32)
    for hh in range(1, GROUP):
        col = jnp.where(head == hh, sink_ref[0, g * GROUP + hh], col)
    return col


def _attn_fwd(qkv, sinks):
    def body(sink_ref, qkv_ref, o_ref, lse_ref, s_ref, p_ref):
        i = pl.program_id(0)
        r0 = pl.multiple_of(i * BLK, BLK)
        rb, valid = _band(i)
        for g in range(NKV):
            kband = qkv_ref[pl.ds(rb, BAND), AW + g * HD:AW + (g + 1) * HD]
            vband = qkv_ref[pl.ds(rb, BAND), AW + KVW + g * HD:AW + KVW + (g + 1) * HD]
            for hh in range(GROUP):
                h = g * GROUP + hh
                s_ref[hh * BLK:(hh + 1) * BLK, :] = _dot_nt(qkv_ref[pl.ds(r0, BLK), h * HD:(h + 1) * HD], kband)
            s = jnp.where(valid, s_ref[...] * SCALE, NEG)
            sink = _sink_column(sink_ref, g)
            m = jnp.maximum(jnp.max(s, axis=-1, keepdims=True), sink)
            p = jnp.exp(s - m)
            den = jnp.sum(p, axis=-1, keepdims=True) + jnp.exp(sink - m)
            p_ref[...] = (p * (1.0 / den)).astype(BF16)
            lse = m + jnp.log(den)
            for hh in range(GROUP):
                h = g * GROUP + hh
                o_ref[:, h * HD:(h + 1) * HD] = _dot(p_ref[hh * BLK:(hh + 1) * BLK, :], vband).astype(BF16)
                lse_ref[:, h:h + 1] = lse[hh * BLK:(hh + 1) * BLK]

    return pl.pallas_call(
        body, name="attn_fwd", grid=(T // BLK,),
        in_specs=[pl.BlockSpec(memory_space=pltpu.SMEM), _whole((T, QKVW))],
        out_specs=[_rows(BLK, AW), _rows(BLK, NQ)],
        out_shape=[jax.ShapeDtypeStruct((T, AW), BF16), jax.ShapeDtypeStruct((T, NQ), F32)],
        scratch_shapes=[pltpu.VMEM((GROWS, BAND), F32), pltpu.VMEM((GROWS, BAND), BF16)],
        compiler_params=_params("parallel"),
    )(sinks, qkv)


CONV_TM = 256
CONV_SUB = 32


def _glu(ab):
    a = ab[:, 0:C].astype(F32)
    b = ab[:, C:2 * C].astype(F32)
    return a * _sigmoid(b)


SUBLANES = 8


def _shifted_copies(ref):
    rows = ref.shape[1] - SUBLANES
    for r in range(1, SUBLANES):
        ref[r, 0:rows, :] = ref[0, r:r + rows, :]


def _shifted_rows(ref, start, size):
    r = start % SUBLANES
    return ref[r, start - r:start - r + size, :]


def _conv_fwd(glu, conv_w, conv_b, ln_g, ln_b):
    tm = CONV_TM

    def body(cur_ref, prev_ref, w_ref, cb_ref, g_ref, b_ref, u_ref, c_ref, zs_ref):
        i = pl.program_id(0)
        zprev = _glu(prev_ref[tm - HALO:tm, :])
        zs_ref[0, 0:HALO, :] = jnp.where(i > 0, zprev, 0.0)
        zs_ref[0, HALO:HALO + tm, :] = _glu(cur_ref[...])
        _shifted_copies(zs_ref)
        for s in range(tm // CONV_SUB):
            base = HALO + s * CONV_SUB - (KW - 1)
            acc = jnp.broadcast_to(cb_ref[...], (CONV_SUB, C))
            for j in range(KW):
                acc = acc + w_ref[j:j + 1, :] * _shifted_rows(zs_ref, base + j, CONV_SUB)
            rows = slice(s * CONV_SUB, (s + 1) * CONV_SUB)
            u_ref[rows, :] = acc
            mu = jnp.mean(acc, axis=-1, keepdims=True)
            xc = acc - mu
            var = jnp.mean(xc * xc, axis=-1, keepdims=True)
            y = xc * lax.rsqrt(var + EPS) * g_ref[...] + b_ref[...]
            c_ref[rows, :] = (y * _sigmoid(y)).astype(BF16)

    return pl.pallas_call(
        body, name="conv_fwd", grid=(T // tm,),
        in_specs=[_rows(tm, 2 * C),
                  pl.BlockSpec((tm, 2 * C), lambda i: (jnp.maximum(i - 1, 0), 0)),
                  _whole((KW, C)), _whole((1, C)), _whole((1, C)), _whole((1, C))],
        out_specs=[_rows(tm, C), _rows(tm, C)],
        out_shape=[jax.ShapeDtypeStruct((T, C), F32), jax.ShapeDtypeStruct((T, C), BF16)],
        scratch_shapes=[pltpu.VMEM((SUBLANES, HALO + tm, C), F32)],
        compiler_params=_params("parallel"),
    )(glu, glu, conv_w, conv_b, ln_g, ln_b)


def _mix_out(x, o, cact, gl, w_ap, w_cp, b_cp, w_out):
    tm = 256

    def body(x_ref, o_ref, c_ref, gl_ref, wap_ref, wcp_ref, bcp_ref, wo_ref,
             ya_ref, yc_ref, mg_ref, x1_ref):
        ov, cv = o_ref[...], c_ref[...]
        ya = jnp.concatenate([_dot(ov, wap_ref[s]) for s in range(N_CHIPS)], axis=1)
        yc = jnp.concatenate([_dot(cv, wcp_ref[s]) for s in range(N_CHIPS)], axis=1) + bcp_ref[...]
        g0 = _sigmoid(gl_ref[:, 0:D].astype(F32))
        g1 = _sigmoid(gl_ref[:, D:2 * D].astype(F32))
        mg = (g0 * ya + g1 * yc).astype(BF16)
        ya_ref[...] = ya.astype(BF16)
        yc_ref[...] = yc.astype(BF16)
        mg_ref[...] = mg
        x1_ref[...] = x_ref[...] + _dot(mg, wo_ref[...])

    return pl.pallas_call(
        body, name="mix_out", grid=(T // tm,),
        in_specs=[_rows(tm, D), _rows(tm, AW), _rows(tm, C), _rows(tm, 2 * D),
                  _whole((N_CHIPS, AW, D // N_CHIPS)), _whole((N_CHIPS, C, D // N_CHIPS)), _whole((1, D)),
                  _whole((D, D))],
        out_specs=[_rows(tm, D), _rows(tm, D), _rows(tm, D), _rows(tm, D)],
        out_shape=[jax.ShapeDtypeStruct((T, D), BF16), jax.ShapeDtypeStruct((T, D), BF16),
                   jax.ShapeDtypeStruct((T, D), BF16), jax.ShapeDtypeStruct((T, D), F32)],
        compiler_params=_params("parallel"),
    )(x, o, cact, gl, w_ap, w_cp, b_cp, w_out)


def _ffn_in(x1, g_ffn, w_fi):
    tm = 256

    def body(x_ref, g_ref, w_ref, h_ref, gu_ref, act_ref):
        xv = x_ref[...]
        r = lax.rsqrt(jnp.mean(xv * xv, axis=-1, keepdims=True) + EPS)
        h = (xv * r * g_ref[...]).astype(BF16)
        h_ref[...] = h
        for s in range(N_CHIPS // 2):
            c0 = s * FSH
            gate = _dot(h, w_ref[s])
            up = _dot(h, w_ref[s + N_CHIPS // 2])
            gu_ref[:, c0:c0 + FSH] = gate.astype(BF16)
            gu_ref[:, DFF + c0:DFF + c0 + FSH] = up.astype(BF16)
            act_ref[:, c0:c0 + FSH] = (gate * _sigmoid(gate) * up).astype(BF16)

    return pl.pallas_call(
        body, name="ffn_in", grid=(T // tm,),
        in_specs=[_rows(tm, D), _whole((1, D)), _whole((N_CHIPS, D, FSH))],
        out_specs=[_rows(tm, D), _rows(tm, 2 * DFF), _rows(tm, DFF)],
        out_shape=[jax.ShapeDtypeStruct((T, D), BF16), jax.ShapeDtypeStruct((T, 2 * DFF), BF16),
                   jax.ShapeDtypeStruct((T, DFF), BF16)],
        compiler_params=_params("parallel"),
    )(x1, g_ffn, w_fi)


def _ffn_out_loss(x1, act, w_dn, g_final, target):
    tm = 256

    def body(x_ref, a_ref, w_ref, g_ref, t_ref, dx_ref, dxb_ref, dg_ref, loss_ref):
        i = pl.program_id(0)
        x2 = x_ref[...] + _dot(a_ref[...], w_ref[...])
        r = lax.rsqrt(jnp.mean(x2 * x2, axis=-1, keepdims=True) + EPS)
        xh = x2 * r
        g = g_ref[...]
        err = xh * g - t_ref[...]
        dy = err * (1.0 / D)
        dyg = dy * g
        dx = r * (dyg - xh * jnp.mean(dyg * xh, axis=-1, keepdims=True))
        dx_ref[...] = dx
        dxb_ref[...] = dx.astype(BF16)
        part = 0.5 * jnp.sum(jnp.mean(err * err, axis=-1, keepdims=True), axis=0, keepdims=True)

        @pl.when(i == 0)
        def _():
            dg_ref[...] = jnp.zeros_like(dg_ref)
            loss_ref[...] = jnp.zeros_like(loss_ref)

        dg_ref[...] += jnp.sum(dy * xh, axis=0, keepdims=True)
        loss_ref[...] += jnp.broadcast_to(part, loss_ref.shape)

    return pl.pallas_call(
        body, name="ffn_out_loss", grid=(T // tm,),
        in_specs=[_rows(tm, D), _rows(tm, DFF), _whole((DFF, D)), _whole((1, D)), _rows(tm, D)],
        out_specs=[_rows(tm, D), _rows(tm, D), _whole((1, D)), _whole((1, 128))],
        out_shape=[jax.ShapeDtypeStruct((T, D), F32), jax.ShapeDtypeStruct((T, D), BF16),
                   jax.ShapeDtypeStruct((1, D), F32), jax.ShapeDtypeStruct((1, 128), F32)],
        compiler_params=_params("arbitrary"),
    )(x1, act, w_dn, g_final, target)


def _const(shape):
    return pl.BlockSpec(shape, lambda i: tuple(0 for _ in shape), pipeline_mode=pl.Buffered(1))


def _ffn_bwd(dx2, dx2b, gu, x1, g_ffn, w_dn_t, w_fi_t):
    tm = 256

    def body(dx_ref, dxb_ref, gu_ref, x_ref, g_ref, wdn_ref, wfi_ref,
             dgu_ref, dx1_ref, dx1b_ref, dg_ref):
        i = pl.program_id(0)
        dxb = dxb_ref[...]
        dh = jnp.zeros((tm, D), F32)
        for k in range(N_CHIPS // 2):
            c0 = k * FSH
            dact = _dot_nt(dxb, wdn_ref[c0:c0 + FSH, :])
            gate = gu_ref[:, c0:c0 + FSH].astype(F32)
            up = gu_ref[:, DFF + c0:DFF + c0 + FSH].astype(F32)
            s = _sigmoid(gate)
            dup = (dact * gate * s).astype(BF16)
            dgate = (dact * up * s * (1.0 + gate * (1.0 - s))).astype(BF16)
            dgu_ref[:, c0:c0 + FSH] = dgate
            dgu_ref[:, DFF + c0:DFF + c0 + FSH] = dup
            dh = dh + _dot_nt(dgate, wfi_ref[k]) + _dot_nt(dup, wfi_ref[k + N_CHIPS // 2])
        xv = x_ref[...]
        r = lax.rsqrt(jnp.mean(xv * xv, axis=-1, keepdims=True) + EPS)
        xh = xv * r
        dhg = dh * g_ref[...]
        dx1 = dx_ref[...] + r * (dhg - xh * jnp.mean(dhg * xh, axis=-1, keepdims=True))
        dx1_ref[...] = dx1
        dx1b_ref[...] = dx1.astype(BF16)

        @pl.when(i == 0)
        def _():
            dg_ref[...] = jnp.zeros_like(dg_ref)

        dg_ref[...] += jnp.sum(dh * xh, axis=0, keepdims=True)

    return pl.pallas_call(
        body, name="ffn_bwd", grid=(T // tm,),
        in_specs=[_rows(tm, D), _rows(tm, D), _rows(tm, 2 * DFF), _rows(tm, D), _whole((1, D)),
                  _const((DFF, D)), _const((N_CHIPS, D, FSH))],
        out_specs=[_rows(tm, 2 * DFF), _rows(tm, D), _rows(tm, D), _whole((1, D))],
        out_shape=[jax.ShapeDtypeStruct((T, 2 * DFF), BF16), jax.ShapeDtypeStruct((T, D), F32),
                   jax.ShapeDtypeStruct((T, D), BF16), jax.ShapeDtypeStruct((1, D), F32)],
        compiler_params=_params("arbitrary"),
    )(dx2, dx2b, gu, x1, g_ffn, w_dn_t, w_fi_t)


def _mix_bwd(dx1b, gl, ya, yc, w_out_t, w_ap_t, w_cp_t, dep):
    tm = 256

    def body(dx_ref, gl_ref, ya_ref, yc_ref, wo_ref, wap_ref, wcp_ref, dep_ref,
             dya_ref, dyc_ref, dgl_ref, do_ref, dc_ref, db_ref):
        i = pl.program_id(0)
        dm = _dot_nt(dx_ref[...], wo_ref[...])
        g0 = _sigmoid(gl_ref[:, 0:D].astype(F32))
        g1 = _sigmoid(gl_ref[:, D:2 * D].astype(F32))
        dya = dm * g0
        dyc = dm * g1
        dgl_ref[:, 0:D] = (dya * ya_ref[...].astype(F32) * (1.0 - g0)).astype(BF16)
        dgl_ref[:, D:2 * D] = (dyc * yc_ref[...].astype(F32) * (1.0 - g1)).astype(BF16)
        dyab = dya.astype(BF16)
        dycb = dyc.astype(BF16)
        dya_ref[...] = dyab
        dyc_ref[...] = dycb
        sw = D // N_CHIPS
        do = jnp.zeros((tm, AW), F32)
        dcv = jnp.zeros((tm, C), F32)
        for s in range(N_CHIPS):
            do = do + _dot_nt(dyab[:, s * sw:(s + 1) * sw], wap_ref[s])
            dcv = dcv + _dot_nt(dycb[:, s * sw:(s + 1) * sw], wcp_ref[s])
        do_ref[...] = do.astype(BF16)
        dc_ref[...] = dcv

        @pl.when(i == 0)
        def _():
            db_ref[...] = jnp.zeros_like(db_ref)

        db_ref[...] += jnp.sum(dyc, axis=0, keepdims=True)

    return pl.pallas_call(
        body, name="mix_bwd", grid=(T // tm,),
        in_specs=[_rows(tm, D), _rows(tm, 2 * D), _rows(tm, D), _rows(tm, D),
                  _whole((D, D)), _whole((N_CHIPS, AW, D // N_CHIPS)), _whole((N_CHIPS, C, D // N_CHIPS)),
                  _whole((8, 128))],
        out_specs=[_rows(tm, D), _rows(tm, D), _rows(tm, 2 * D), _rows(tm, AW), _rows(tm, C),
                   _whole((1, D))],
        out_shape=[jax.ShapeDtypeStruct((T, D), BF16), jax.ShapeDtypeStruct((T, D), BF16),
                   jax.ShapeDtypeStruct((T, 2 * D), BF16), jax.ShapeDtypeStruct((T, AW), BF16),
                   jax.ShapeDtypeStruct((T, C), F32), jax.ShapeDtypeStruct((1, D), F32)],
        compiler_params=_params("arbitrary"),
    )(dx1b, gl, ya, yc, w_out_t, w_ap_t, w_cp_t, dep)


def _conv_bwd(glu, u, dc, conv_w, ln_g, ln_b, dep):
    tm = CONV_TM
    nblk = T // tm

    def du_of(uv, dcv, g_ref, b_ref):
        mu = jnp.mean(uv, axis=-1, keepdims=True)
        xc = uv - mu
        var = jnp.mean(xc * xc, axis=-1, keepdims=True)
        rstd = lax.rsqrt(var + EPS)
        xh = xc * rstd
        y = xh * g_ref[...] + b_ref[...]
        sg = _sigmoid(y)
        dy = dcv * (sg * (1.0 + y * (1.0 - sg)))
        dxh = dy * g_ref[...]
        du = rstd * (dxh - jnp.mean(dxh, axis=-1, keepdims=True)
                     - xh * jnp.mean(dxh * xh, axis=-1, keepdims=True))
        return du, dy, xh

    def body(cur_ref, prev_ref, u_ref, un_ref, dc_ref, dcn_ref, w_ref, g_ref, b_ref, dep_ref,
             dglu_ref, dw_ref, dcb_ref, dg_ref, db_ref, zs_ref, dus_ref):
        i = pl.program_id(0)

        @pl.when(i == 0)
        def _():
            dw_ref[...] = jnp.zeros_like(dw_ref)
            dcb_ref[...] = jnp.zeros_like(dcb_ref)
            dg_ref[...] = jnp.zeros_like(dg_ref)
            db_ref[...] = jnp.zeros_like(db_ref)

        zprev = _glu(prev_ref[tm - HALO:tm, :])
        zs_ref[0, 0:HALO, :] = jnp.where(i > 0, zprev, 0.0)
        zs_ref[0, HALO:HALO + tm, :] = _glu(cur_ref[...])
        _shifted_copies(zs_ref)

        dun, _, _ = du_of(un_ref[0:HALO, :], dcn_ref[0:HALO, :], g_ref, b_ref)
        dus_ref[0, tm:tm + HALO, :] = jnp.where(i < nblk - 1, dun, 0.0)
        dg_acc = jnp.zeros((1, C), F32)
        db_acc = jnp.zeros((1, C), F32)
        dcb_acc = jnp.zeros((1, C), F32)
        for s in range(tm // CONV_SUB):
            rows = slice(s * CONV_SUB, (s + 1) * CONV_SUB)
            du, dy, xh = du_of(u_ref[rows, :], dc_ref[rows, :], g_ref, b_ref)
            dus_ref[0, rows, :] = du
            dg_acc = dg_acc + jnp.sum(dy * xh, axis=0, keepdims=True)
            db_acc = db_acc + jnp.sum(dy, axis=0, keepdims=True)
            dcb_acc = dcb_acc + jnp.sum(du, axis=0, keepdims=True)
        dg_ref[...] += dg_acc
        db_ref[...] += db_acc
        dcb_ref[...] += dcb_acc
        _shifted_copies(dus_ref)

        for j in range(KW):
            acc = jnp.zeros((CONV_SUB, C), F32)
            for s in range(tm // CONV_SUB):
                base = HALO + s * CONV_SUB - (KW - 1) + j
                acc = acc + dus_ref[0, s * CONV_SUB:(s + 1) * CONV_SUB, :] * _shifted_rows(zs_ref, base, CONV_SUB)
            dw_ref[j:j + 1, :] += jnp.sum(acc, axis=0, keepdims=True)

        for s in range(tm // CONV_SUB):
            rows = slice(s * CONV_SUB, (s + 1) * CONV_SUB)
            dz = jnp.zeros((CONV_SUB, C), F32)
            for j in range(KW):
                dz = dz + w_ref[j:j + 1, :] * _shifted_rows(dus_ref, s * CONV_SUB + (KW - 1) - j, CONV_SUB)
            a = cur_ref[rows, 0:C].astype(F32)
            sb = _sigmoid(cur_ref[rows, C:2 * C].astype(F32))
            dglu_ref[rows, 0:C] = (dz * sb).astype(BF16)
            dglu_ref[rows, C:2 * C] = (dz * a * sb * (1.0 - sb)).astype(BF16)

    nxt = lambda i: (jnp.minimum(i + 1, nblk - 1), 0)
    return pl.pallas_call(
        body, name="conv_bwd", grid=(nblk,),
        in_specs=[_rows(tm, 2 * C),
                  pl.BlockSpec((tm, 2 * C), lambda i: (jnp.maximum(i - 1, 0), 0)),
                  _rows(tm, C), pl.BlockSpec((tm, C), nxt),
                  _rows(tm, C), pl.BlockSpec((tm, C), nxt),
                  _whole((KW, C)), _whole((1, C)), _whole((1, C)), _whole((8, 128))],
        out_specs=[_rows(tm, 2 * C), _whole((KW, C)), _whole((1, C)), _whole((1, C)), _whole((1, C))],
        out_shape=[jax.ShapeDtypeStruct((T, 2 * C), BF16), jax.ShapeDtypeStruct((KW, C), F32),
                   jax.ShapeDtypeStruct((1, C), F32), jax.ShapeDtypeStruct((1, C), F32),
                   jax.ShapeDtypeStruct((1, C), F32)],
        scratch_shapes=[pltpu.VMEM((SUBLANES, HALO + tm, C), F32), pltpu.VMEM((SUBLANES, tm + HALO, C), F32)],
        compiler_params=_params("arbitrary"),
    )(glu, glu, u, u, dc, dc, conv_w, ln_g, ln_b, dep)


def _attn_bwd(qkv, o, do, lse, sinks):
    def body(sink_ref, qkv_ref, o_ref, do_ref, lse_ref, dq_ref, dkv_ref, ds_ref,
             s_ref, dp_ref, p_ref, dsb_ref):
        i = pl.program_id(0)

        @pl.when(i == 0)
        def _():
            dkv_ref[...] = jnp.zeros_like(dkv_ref)
            ds_ref[...] = jnp.zeros_like(ds_ref)

        r0 = pl.multiple_of(i * BLK, BLK)
        rb, valid = _band(i)
        for g in range(NKV):
            kband = qkv_ref[pl.ds(rb, BAND), AW + g * HD:AW + (g + 1) * HD]
            vband = qkv_ref[pl.ds(rb, BAND), AW + KVW + g * HD:AW + KVW + (g + 1) * HD]
            lse_parts, dl_parts = [], []
            for hh in range(GROUP):
                h = g * GROUP + hh
                hcol = slice(h * HD, (h + 1) * HD)
                doh = do_ref[:, hcol]
                s_ref[hh * BLK:(hh + 1) * BLK, :] = _dot_nt(qkv_ref[pl.ds(r0, BLK), hcol], kband)
                dp_ref[hh * BLK:(hh + 1) * BLK, :] = _dot_nt(doh, vband)
                lse_parts.append(lse_ref[:, h:h + 1])
                dl_parts.append(jnp.sum(doh.astype(F32) * o_ref[:, hcol].astype(F32), axis=-1, keepdims=True))
            lse = jnp.concatenate(lse_parts, axis=0)
            dl = jnp.concatenate(dl_parts, axis=0)
            p = jnp.where(valid, jnp.exp(s_ref[...] * SCALE - lse), 0.0)
            p_ref[...] = p.astype(BF16)
            dsb_ref[...] = (p * (dp_ref[...] - dl)).astype(BF16)
            dsink = -(jnp.exp(_sink_column(sink_ref, g) - lse) * dl)
            dk = jnp.zeros((BAND, HD), F32)
            dv = jnp.zeros((BAND, HD), F32)
            for hh in range(GROUP):
                h = g * GROUP + hh
                hcol = slice(h * HD, (h + 1) * HD)
                rows = slice(hh * BLK, (hh + 1) * BLK)
                dq_ref[:, hcol] = (_dot(dsb_ref[rows, :], kband) * SCALE).astype(BF16)
                dk = dk + _dot_tn(dsb_ref[rows, :], qkv_ref[pl.ds(r0, BLK), hcol])
                dv = dv + _dot_tn(p_ref[rows, :], do_ref[:, hcol])
                ds_ref[h:h + 1, :] += jnp.broadcast_to(jnp.sum(dsink[rows], axis=0, keepdims=True), (1, 128))
            dkv_ref[pl.ds(rb, BAND), g * HD:(g + 1) * HD] += dk * SCALE
            dkv_ref[pl.ds(rb, BAND), KVW + g * HD:KVW + (g + 1) * HD] += dv

    return pl.pallas_call(
        body, name="attn_bwd", grid=(T // BLK,),
        in_specs=[pl.BlockSpec(memory_space=pltpu.SMEM), _whole((T, QKVW)),
                  _rows(BLK, AW), _rows(BLK, AW), _rows(BLK, NQ)],
        out_specs=[_rows(BLK, AW), _whole((T, 2 * KVW)), _whole((NQ, 128))],
        out_shape=[jax.ShapeDtypeStruct((T, AW), BF16), jax.ShapeDtypeStruct((T, 2 * KVW), F32),
                   jax.ShapeDtypeStruct((NQ, 128), F32)],
        scratch_shapes=[pltpu.VMEM((GROWS, BAND), F32), pltpu.VMEM((GROWS, BAND), F32),
                        pltpu.VMEM((GROWS, BAND), BF16), pltpu.VMEM((GROWS, BAND), BF16)],
        compiler_params=_params("arbitrary"),
    )(sinks, qkv, o, do, lse)


PROJ_PARTS = [(0, AW), (AW, QKVW), (GLU_OFF, GATE_OFF), (GATE_OFF, INW)]


def _in_proj_bwd(dq, dkv, dglu, dgl, x, dx1, g_mix, w_in_t):
    tm = 256

    def body(dq_ref, dkv_ref, dglu_ref, dgl_ref, x_ref, dx1_ref, g_ref, w_ref, gx_ref, dg_ref, db_ref):
        i = pl.program_id(0)

        @pl.when(i == 0)
        def _():
            dg_ref[...] = jnp.zeros_like(dg_ref)
            db_ref[...] = jnp.zeros_like(db_ref)

        dh = jnp.zeros((tm, D), F32)
        for part_ref, (lo, hi) in zip((dq_ref, dkv_ref, dglu_ref, dgl_ref), PROJ_PARTS):
            part = part_ref[...]
            dh = dh + _dot(part.astype(BF16), w_ref[lo:hi, :])
            db_ref[:, lo:hi] += jnp.sum(part.astype(F32), axis=0, keepdims=True)
        xv = x_ref[...]
        r = lax.rsqrt(jnp.mean(xv * xv, axis=-1, keepdims=True) + EPS)
        xh = xv * r
        dhg = dh * g_ref[...]
        gx_ref[...] = dx1_ref[...] + r * (dhg - xh * jnp.mean(dhg * xh, axis=-1, keepdims=True))
        dg_ref[...] += jnp.sum(dh * xh, axis=0, keepdims=True)

    return pl.pallas_call(
        body, name="in_proj_bwd", grid=(T // tm,),
        in_specs=[_rows(tm, AW), _rows(tm, 2 * KVW), _rows(tm, 2 * C), _rows(tm, 2 * D),
                  _rows(tm, D), _rows(tm, D), _whole((1, D)), _const((INW, D))],
        out_specs=[_rows(tm, D), _whole((1, D)), _whole((1, INW))],
        out_shape=[jax.ShapeDtypeStruct((T, D), F32), jax.ShapeDtypeStruct((1, D), F32),
                   jax.ShapeDtypeStruct((1, INW), F32)],
        compiler_params=_params("arbitrary"),
    )(dq, dkv, dglu, dgl, x, dx1, g_mix, w_in_t)


def _grad_w_in_t(h, dq, dkv, dglu, dgl):
    tn, chunk = 512, 256

    def body(h_ref, dq_ref, dkv_ref, dglu_ref, dgl_ref, o_ref, pt_ref):
        @pl.when(pl.program_id(0) == 0)
        def _():
            for part_ref, (lo, hi) in zip((dq_ref, dkv_ref, dglu_ref, dgl_ref), PROJ_PARTS):
                for c0 in range(0, hi - lo, chunk):
                    pt_ref[lo + c0:lo + c0 + chunk, :] = part_ref[:, c0:c0 + chunk].astype(BF16).T

        hv = h_ref[...]
        for r0 in range(0, INW, QKVW):
            o_ref[r0:r0 + QKVW, :] = _dot(pt_ref[r0:r0 + QKVW, :], hv).astype(BF16)

    return pl.pallas_call(
        body, name="grad_w_in", grid=(D // tn,),
        in_specs=[pl.BlockSpec((T, tn), lambda j: (0, j)), _const((T, AW)), _const((T, 2 * KVW)),
                  _const((T, 2 * C)), _const((T, 2 * D))],
        out_specs=pl.BlockSpec((INW, tn), lambda j: (0, j)),
        out_shape=jax.ShapeDtypeStruct((INW, D), BF16),
        scratch_shapes=[pltpu.VMEM((INW, T), BF16)],
        compiler_params=_params("arbitrary"),
    )(h, dq, dkv, dglu, dgl)


def _grad_w(a, b, name, tk, tn, col_sharded):
    k, n = a.shape[1], b.shape[1]

    def body(a_ref, b_ref, o_ref, at_ref):
        @pl.when(pl.program_id(1) == 0)
        def _():
            at_ref[...] = a_ref[...].T

        o_ref[...] = _dot(at_ref[...], b_ref[...]).astype(BF16)

    if col_sharded:
        per = n // N_CHIPS // tn
        shape = (N_CHIPS, k, n // N_CHIPS)
        out_map = lambda i, j: (j // per, i, j % per)
    else:
        shape = (1, k, n)
        out_map = lambda i, j: (0, i, j)
    out = pl.pallas_call(
        body, name=name, grid=(k // tk, n // tn),
        in_specs=[pl.BlockSpec((T, tk), lambda i, j: (0, i)), pl.BlockSpec((T, tn), lambda i, j: (0, j))],
        out_specs=pl.BlockSpec((None, tk, tn), out_map),
        out_shape=jax.ShapeDtypeStruct(shape, BF16),
        scratch_shapes=[pltpu.VMEM((tk, T), BF16)],
        compiler_params=_params("parallel", "arbitrary"),
    )(a, b)
    return out if col_sharded else out.reshape(N_CHIPS, k // N_CHIPS, n)


HBM_SPEC = pl.BlockSpec(memory_space=pltpu.HBM)


def _place():
    x, y, c = lax.axis_index("x"), lax.axis_index("y"), lax.axis_index("c")
    chips = [(1 - x, y), (x, 1 - y), (1 - x, 1 - y)]
    return x, y, c, chips


SEM_SPEC = pl.BlockSpec(memory_space=pltpu.SEMAPHORE)
ANY_SPEC = pl.BlockSpec(memory_space=pl.ANY)
VMEM_SPEC = pl.BlockSpec(memory_space=pltpu.VMEM)
EFFECT = pltpu.SideEffectType.DATAFLOW_SIDE_EFFECTING


def _gather_ends(src, land, x, y, c, chips):
    kh = src.shape[0] // 2
    s_me = 2 * x + y
    ends = [(src.at[pl.ds(c * kh, kh)], land.at[s_me, pl.ds(c * kh, kh)], (*chip, c)) for chip in chips]
    return ends + [(src, land.at[s_me], (x, y, 1 - c))]


def _reduce_ends(src, land, x, y, c, chips):
    return [(src.at[2 * chip[0] + chip[1]], land.at[j], (*chip, c)) for j, chip in enumerate(chips)]


def _chip_copies(ends, srcs, lands, send_sems, recv_sems):
    x, y, c, chips = _place()
    copies = []
    for src, land in zip(srcs, lands):
        for s, d, to in ends(src, land, x, y, c, chips):
            k = len(copies)
            copies.append(pltpu.make_async_remote_copy(
                src_ref=s, dst_ref=d, send_sem=send_sems.at[k], recv_sem=recv_sems.at[k],
                device_id=to, device_id_type=MESH))
    return copies


GATHER_PEERS, REDUCE_PEERS = 4, 3


def _chip_start(name, ends, peers, srcs, lands):
    n = len(srcs)

    def body(*refs):
        copies = _chip_copies(ends, refs[:n], refs[n:2 * n], refs[2 * n], refs[2 * n + 1])
        for cp in copies:
            cp.start()
        token = refs[-1]
        token[...] = jnp.zeros_like(token)

    hbm = lambda a: pltpu.HBM(a.shape, a.dtype)
    res = pl.pallas_call(
        body, name=name,
        out_shape=(pltpu.SemaphoreType.DMA((peers * n,)), pltpu.SemaphoreType.DMA((peers * n,)),
                   *[hbm(a) for a in srcs], *[hbm(a) for a in lands],
                   jax.ShapeDtypeStruct((8, 128), F32)),
        in_specs=[HBM_SPEC] * (2 * n),
        out_specs=(SEM_SPEC, SEM_SPEC, *[HBM_SPEC] * (2 * n), VMEM_SPEC),
        input_output_aliases={i: 2 + i for i in range(2 * n)},
        compiler_params=pltpu.CompilerParams(has_side_effects=EFFECT),
    )(*[pltpu.with_memory_space_constraint(a, pltpu.HBM) for a in (*srcs, *lands)])
    return res[0], res[1], list(res[2:2 + n]), list(res[2 + n:2 + 2 * n]), res[-1]


def _chip_wait(name, ends, send_sems, recv_sems, srcs, lands, after):
    n, na = len(srcs), len(after)

    def body(*refs):
        copies = _chip_copies(ends, refs[:n], refs[n:2 * n], refs[2 * n], refs[2 * n + 1])
        for cp in copies:
            cp.wait_send()
            cp.wait_recv()

    hbm = lambda a: pltpu.HBM(a.shape, a.dtype)
    res = pl.pallas_call(
        body, name=name,
        out_shape=tuple(hbm(a) for a in (*srcs, *lands)),
        in_specs=[HBM_SPEC] * (2 * n) + [SEM_SPEC, SEM_SPEC] + [ANY_SPEC] * na,
        out_specs=tuple([HBM_SPEC] * (2 * n)),
        input_output_aliases={i: i for i in range(2 * n)},
        compiler_params=pltpu.CompilerParams(has_side_effects=EFFECT),
    )(*srcs, *lands, send_sems, recv_sems, *after)
    return list(res[:n]), list(res[n:])


def _pair_forward(name, lands):
    n = len(lands)

    def body(*refs):
        bufs = refs[n:2 * n]
        send_sems, recv_sems = refs[2 * n:]
        x, y, c, chips = _place()

        def copy(w, j, cc, to):
            kh = bufs[w].shape[1] // 2
            blk = bufs[w].at[2 * chips[j][0] + chips[j][1], pl.ds(cc * kh, kh)]
            return pltpu.make_async_remote_copy(
                src_ref=blk, dst_ref=blk, send_sem=send_sems.at[3 * w + j],
                recv_sem=recv_sems.at[3 * w + j], device_id=to, device_id_type=MESH)

        sends = [copy(w, j, c, (x, y, 1 - c)) for w in range(n) for j in range(3)]
        for cp in sends:
            cp.start()
        for w in range(n):
            for j in range(3):
                copy(w, j, 1 - c, (x, y, c)).wait_recv()
        for cp in sends:
            cp.wait_send()

    return pl.pallas_call(
        body, name=name,
        in_specs=[HBM_SPEC] * n, out_specs=[HBM_SPEC] * n,
        out_shape=[jax.ShapeDtypeStruct(a.shape, a.dtype) for a in lands],
        input_output_aliases={w: w for w in range(n)},
        scratch_shapes=[pltpu.SemaphoreType.DMA((3 * n,)), pltpu.SemaphoreType.DMA((3 * n,))],
    )(*lands)


def _pair_exchange(name, grads):
    n = len(grads)

    def body(*refs):
        ins, outs = refs[:n], refs[n:2 * n]
        send_sems, recv_sems = refs[2 * n:]
        x, y, c, _ = _place()
        copies = []
        for w in range(n):
            kh = ins[w].shape[1] // 2
            cp = pltpu.make_async_remote_copy(
                src_ref=ins[w].at[:, pl.ds((1 - c) * kh, kh)], dst_ref=outs[w],
                send_sem=send_sems.at[w], recv_sem=recv_sems.at[w],
                device_id=(x, y, 1 - c), device_id_type=MESH)
            cp.start()
            copies.append(cp)
        for cp in copies:
            cp.wait()

    return pl.pallas_call(
        body, name=name,
        in_specs=[HBM_SPEC] * n, out_specs=[HBM_SPEC] * n,
        out_shape=[jax.ShapeDtypeStruct((g.shape[0], g.shape[1] // 2, g.shape[2]), g.dtype) for g in grads],
        scratch_shapes=[pltpu.SemaphoreType.DMA((n,)), pltpu.SemaphoreType.DMA((n,))],
    )(*grads)


def _row_tile(k):
    for t in (256, 240, 128, 176, 64, 32, 16):
        if k % t == 0:
            return t
    raise ValueError(k)


def _pair_sum(c_idx, g, got, name):
    _, k, n = g.shape
    kh = k // 2
    tm = _row_tile(kh)
    nb = kh // tm

    def body(c_ref, g_ref, r_ref, o_ref):
        o_ref[...] = (g_ref[...].astype(F32) + r_ref[...].astype(F32)).astype(BF16)

    return pl.pallas_call(
        body, name=name,
        grid_spec=pltpu.PrefetchScalarGridSpec(
            num_scalar_prefetch=1, grid=(N_CHIPS, nb),
            in_specs=[pl.BlockSpec((1, tm, n), lambda s, i, c_ref: (s, c_ref[0] * nb + i, 0)),
                      pl.BlockSpec((1, tm, n), lambda s, i, c_ref: (s, i, 0))],
            out_specs=pl.BlockSpec((1, tm, n), lambda s, i, c_ref: (s, i, 0))),
        out_shape=jax.ShapeDtypeStruct((N_CHIPS, kh, n), BF16),
        compiler_params=_params("parallel", "parallel"),
    )(c_idx, g, got)


def _chip_sum(sc_idx, mine, got, name):
    _, kh, n = mine.shape
    tm = _row_tile(kh)

    def body(sc_ref, m_ref, r_ref, o_ref):
        acc = m_ref[0].astype(F32)
        for j in range(3):
            acc = acc + r_ref[j].astype(F32)
        o_ref[0] = acc

    return pl.pallas_call(
        body, name=name,
        grid_spec=pltpu.PrefetchScalarGridSpec(
            num_scalar_prefetch=1, grid=(kh // tm,),
            in_specs=[pl.BlockSpec((1, tm, n), lambda i, sc_ref: (sc_ref[0], i, 0)),
                      pl.BlockSpec((3, tm, n), lambda i, sc_ref: (0, i, 0))],
            out_specs=pl.BlockSpec((1, tm, n), lambda i, sc_ref: (sc_ref[1], i, 0))),
        out_shape=jax.ShapeDtypeStruct((2, kh, n), F32),
        compiler_params=_params("parallel"),
    )(sc_idx, mine, got)


def _pair_share(name, halves):
    n = len(halves)

    def body(*refs):
        bufs = refs[n:2 * n]
        send_sems, recv_sems = refs[2 * n:]
        x, y, c, _ = _place()
        copies = []
        for w in range(n):
            cp = pltpu.make_async_remote_copy(
                src_ref=bufs[w].at[c], dst_ref=bufs[w].at[c],
                send_sem=send_sems.at[w], recv_sem=recv_sems.at[w],
                device_id=(x, y, 1 - c), device_id_type=MESH)
            cp.start()
            copies.append(cp)
        for w in range(n):
            copies[w].wait_send()
            pltpu.make_async_remote_copy(
                src_ref=bufs[w].at[1 - c], dst_ref=bufs[w].at[1 - c],
                send_sem=send_sems.at[w], recv_sem=recv_sems.at[w],
                device_id=(x, y, c), device_id_type=MESH).wait_recv()

    return pl.pallas_call(
        body, name=name,
        in_specs=[HBM_SPEC] * n, out_specs=[HBM_SPEC] * n,
        out_shape=[jax.ShapeDtypeStruct(h.shape, h.dtype) for h in halves],
        input_output_aliases={w: w for w in range(n)},
        scratch_shapes=[pltpu.SemaphoreType.DMA((n,)), pltpu.SemaphoreType.DMA((n,))],
    )(*halves)


def _gather_small(blocks):
    n = len(blocks)

    def body(*refs):
        ins, outs = refs[:n], refs[n:2 * n]
        send_sems, recv_sems, local_sems = refs[2 * n:]
        x, y, c, chips = _place()
        me, sibling = (x, y, c), (x, y, 1 - c)

        def rows(w, px, py, pc):
            m = ins[w].shape[0]
            return outs[w].at[pl.ds((4 * px + 2 * py + pc) * m, m), :]

        def copy(w, k, block, to, src=None):
            return pltpu.make_async_remote_copy(
                src_ref=rows(w, *block) if src is None else src, dst_ref=rows(w, *block),
                send_sem=send_sems.at[7 * w + k], recv_sem=recv_sems.at[7 * w + k],
                device_id=to, device_id_type=MESH)

        started, owns = [], []
        for w in range(n):
            own = pltpu.make_async_copy(ins[w], rows(w, *me), local_sems.at[w])
            own.start()
            owns.append(own)
            first = [copy(w, 0, me, sibling, src=ins[w])]
            first += [copy(w, 1 + j, me, (*chip, c), src=ins[w]) for j, chip in enumerate(chips)]
            for cp in first:
                cp.start()
            started += first
        for w in range(n):
            for j, chip in enumerate(chips):
                copy(w, 1 + j, (*chip, c), me).wait_recv()
                cp = copy(w, 4 + j, (*chip, c), sibling)
                cp.start()
                started.append(cp)
        for w in range(n):
            copy(w, 0, sibling, me).wait_recv()
            for j, chip in enumerate(chips):
                copy(w, 4 + j, (*chip, 1 - c), me).wait_recv()
        for cp in started:
            cp.wait_send()
        for own in owns:
            own.wait()

    vmem = pl.BlockSpec(memory_space=pltpu.VMEM)
    return pl.pallas_call(
        body, name="gather_small",
        in_specs=[vmem] * n, out_specs=[vmem] * n,
        out_shape=[jax.ShapeDtypeStruct((8 * b.shape[0], b.shape[1]), b.dtype) for b in blocks],
        scratch_shapes=[pltpu.SemaphoreType.DMA((7 * n,)), pltpu.SemaphoreType.DMA((7 * n,)),
                        pltpu.SemaphoreType.DMA((n,))],
    )(*blocks)


def _adamw_math(w, g, m, v):
    m = ADAM_B1 * m + (1.0 - ADAM_B1) * g
    v = ADAM_B2 * v + (1.0 - ADAM_B2) * (g * g)
    m_hat = m / (1.0 - ADAM_B1 ** ADAM_STEP)
    v_hat = v / (1.0 - ADAM_B2 ** ADAM_STEP)
    delta = -ADAM_LR * (m_hat / (jnp.sqrt(v_hat) + ADAM_EPS) + ADAM_WD * w)
    return delta, m, v


def _adamw(w, g, m, v, name):
    k, n = w.shape
    tm = k // 4

    def body(w_ref, g_ref, m_ref, v_ref, go_ref, d_ref, mo_ref, vo_ref):
        g = g_ref[...]
        d, mm, vv = _adamw_math(w_ref[...], g, m_ref[...], v_ref[...])
        go_ref[...] = g
        d_ref[...] = d
        mo_ref[...] = mm
        vo_ref[...] = vv

    spec = pl.BlockSpec((tm, n), lambda i: (i, 0))
    shp = jax.ShapeDtypeStruct((k, n), F32)
    return pl.pallas_call(
        body, name=name, grid=(4,), in_specs=[spec] * 4, out_specs=[spec] * 4,
        out_shape=[shp] * 4, compiler_params=_params("parallel"),
    )(w, g, m, v)


VEC_SLOTS = {
    "g_mix_norm": (0, 0, D), "b_conv_proj": (0, D, D), "g_ffn_norm": (0, 2 * D, D),
    "g_final": (0, 3 * D, D), "b_in": (1, 0, INW), "conv_b": (2, 0, C), "ln_g": (2, C, C),
    "ln_b": (2, 2 * C, C), "sinks": (2, 3 * C, NQ), "loss": (2, 3 * C + 128, 1),
}
VEC_ROWS, VEC_COLS = 8, 4 * D
CW_ROWS = 32
SMALL_NAMES = ["g_mix_norm", "b_in", "sinks", "conv_w", "conv_b", "ln_g", "ln_b",
               "b_conv_proj", "g_ffn_norm", "g_final"]
CW_LANES = C // N_CHIPS


def _pack_small(gs, loss):
    row0 = jnp.concatenate([gs["g_mix_norm"], gs["b_conv_proj"], gs["g_ffn_norm"], gs["g_final"]], axis=1)
    row1 = jnp.pad(gs["b_in"], ((0, 0), (0, VEC_COLS - INW)))
    row2 = jnp.concatenate([gs["conv_b"], gs["ln_g"], gs["ln_b"],
                            jnp.pad(gs["sinks"], ((0, 0), (0, 128 - NQ))),
                            jnp.pad(loss.reshape(1, 1), ((0, 0), (0, VEC_COLS - 3 * C - 129)))], axis=1)
    vec = jnp.concatenate([row0, row1, row2, jnp.zeros((VEC_ROWS - 3, VEC_COLS), F32)], axis=0)
    cw = jnp.pad(gs["conv_w"], ((0, CW_ROWS - KW), (0, 0)))
    return vec, cw


def _small_update(s_idx, vec_all, cw_all, wmv):
    nsm = len(SMALL_NAMES)

    def body(s_ref, vec_ref, cw_ref, *refs):
        ins = refs[:3 * nsm]
        outs = refs[3 * nsm:7 * nsm]
        loss_ref = refs[7 * nsm]

        def total(slot):
            row, lane, width = slot
            acc = vec_ref[row:row + 1, lane:lane + width]
            for k in range(1, 8):
                acc = acc + vec_ref[k * VEC_ROWS + row:k * VEC_ROWS + row + 1, lane:lane + width]
            return acc

        loss_ref[...] = jnp.broadcast_to(total(VEC_SLOTS["loss"]), loss_ref.shape)
        for p, name in enumerate(SMALL_NAMES):
            w_ref, m_ref, v_ref = ins[3 * p:3 * p + 3]
            g_ref, d_ref, mo_ref, vo_ref = outs[4 * p:4 * p + 4]
            if name == "conv_w":
                g = jnp.zeros((KW, CW_LANES), F32)
                for s in range(N_CHIPS):
                    cand = cw_ref[0:KW, s * CW_LANES:(s + 1) * CW_LANES]
                    for k in range(1, 8):
                        cand = cand + cw_ref[k * CW_ROWS:k * CW_ROWS + KW, s * CW_LANES:(s + 1) * CW_LANES]
                    g = jnp.where(s_ref[0] == s, cand, g)
            else:
                g = total(VEC_SLOTS[name])
            d, mm, vv = _adamw_math(w_ref[...], g, m_ref[...], v_ref[...])
            g_ref[...] = g
            d_ref[...] = d
            mo_ref[...] = mm
            vo_ref[...] = vv

    vmem = pl.BlockSpec(memory_space=pltpu.VMEM)
    flat = [a for t in wmv for a in t]
    out_shape = []
    for w, _, _ in wmv:
        out_shape += [jax.ShapeDtypeStruct(w.shape, F32)] * 4
    out_shape.append(jax.ShapeDtypeStruct((1, 128), F32))
    res = pl.pallas_call(
        body, name="small_update",
        in_specs=[pl.BlockSpec(memory_space=pltpu.SMEM)] + [vmem] * (2 + len(flat)),
        out_specs=[vmem] * len(out_shape), out_shape=out_shape,
    )(s_idx, vec_all, cw_all, *flat)
    return [tuple(res[4 * p:4 * p + 4]) for p in range(nsm)], res[4 * nsm]


WEIGHT_ORDER = ["g_mix_norm", "w_in", "b_in", "sinks", "conv_w", "conv_b", "ln_g", "ln_b",
                "w_attn_proj", "w_conv_proj", "b_conv_proj", "w_out", "g_ffn_norm", "w_ffn_in",
                "w_ffn_down", "g_final"]


def kernel(x, g_mix_norm, w_in, b_in, sinks, conv_w, conv_b, ln_g, ln_b, w_attn_proj, w_conv_proj, b_conv_proj, w_out, g_ffn_norm, w_ffn_in, w_ffn_down, g_final, loss_target, m_g_mix_norm, m_w_in, m_b_in, m_sinks, m_conv_w, m_conv_b, m_ln_g, m_ln_b, m_w_attn_proj, m_w_conv_proj, m_b_conv_proj, m_w_out, m_g_ffn_norm, m_w_ffn_in, m_w_ffn_down, m_g_final, v_g_mix_norm, v_w_in, v_b_in, v_sinks, v_conv_w, v_conv_b, v_ln_g, v_ln_b, v_w_attn_proj, v_w_conv_proj, v_b_conv_proj, v_w_out, v_g_ffn_norm, v_w_ffn_in, v_w_ffn_down, v_g_final):
    w = dict(g_mix_norm=g_mix_norm, w_in=w_in, b_in=b_in, sinks=sinks, conv_w=conv_w, conv_b=conv_b,
             ln_g=ln_g, ln_b=ln_b, w_attn_proj=w_attn_proj, w_conv_proj=w_conv_proj,
             b_conv_proj=b_conv_proj, w_out=w_out, g_ffn_norm=g_ffn_norm, w_ffn_in=w_ffn_in,
             w_ffn_down=w_ffn_down, g_final=g_final)
    m = dict(g_mix_norm=m_g_mix_norm, w_in=m_w_in, b_in=m_b_in, sinks=m_sinks, conv_w=m_conv_w,
             conv_b=m_conv_b, ln_g=m_ln_g, ln_b=m_ln_b, w_attn_proj=m_w_attn_proj,
             w_conv_proj=m_w_conv_proj, b_conv_proj=m_b_conv_proj, w_out=m_w_out,
             g_ffn_norm=m_g_ffn_norm, w_ffn_in=m_w_ffn_in, w_ffn_down=m_w_ffn_down, g_final=m_g_final)
    v = dict(g_mix_norm=v_g_mix_norm, w_in=v_w_in, b_in=v_b_in, sinks=v_sinks, conv_w=v_conv_w,
             conv_b=v_conv_b, ln_g=v_ln_g, ln_b=v_ln_b, w_attn_proj=v_w_attn_proj,
             w_conv_proj=v_w_conv_proj, b_conv_proj=v_b_conv_proj, w_out=v_w_out,
             g_ffn_norm=v_g_ffn_norm, w_ffn_in=v_w_ffn_in, w_ffn_down=v_w_ffn_down, g_final=v_g_final)

    c_idx = lax.axis_index("c").astype(jnp.int32).reshape(1)
    s_idx = (2 * lax.axis_index("x") + lax.axis_index("y")).astype(jnp.int32).reshape(1)

    sc_idx = jnp.concatenate([s_idx, c_idx])
    out_g, out_d, out_m, out_v = {}, {}, {}, {}

    def gather_start(tag, shards):
        lands = [lax.empty((N_CHIPS,) + s.shape, s.dtype) for s in shards]
        return _chip_start("gather_start_" + tag, _gather_ends, GATHER_PEERS, shards, lands)

    def gather_finish(tag, state, after):
        send_sems, recv_sems, shards, lands, _ = state
        shards, lands = _chip_wait("gather_wait_" + tag, _gather_ends, send_sems, recv_sems, shards, lands, after)
        return _pair_forward("pair_forward_" + tag, lands)

    names_b = ["w_attn_proj", "w_conv_proj", "w_out", "w_ffn_in", "w_ffn_down"]
    big = {name: (w[name][0], m[name][0], v[name][0]) for name in names_b}
    big["w_in"] = (w_in[0].T, m_w_in[0].T, v_w_in[0].T)
    state_a = gather_start("a", [big["w_in"][0].astype(BF16), jnp.pad(conv_w[0], ((0, CW_ROWS - KW), (0, 0)))])
    state_b = gather_start("b", [(big[name][0] + state_a[4][0, 0]).astype(BF16) for name in names_b])
    got_a = gather_finish("a", state_a, [state_b[4]])
    w_in_t_full = got_a[0].reshape(INW, D)
    conv_w_full = got_a[1].transpose(1, 0, 2).reshape(CW_ROWS, C)[:KW]

    xs, target = x[0], loss_target[0]
    g_final2 = g_final.reshape(1, D)
    h, qkv, glu, gl = _in_proj(xs, g_mix_norm, w_in_t_full, b_in)
    o, lse = _attn_fwd(qkv, sinks)
    u, cact = _conv_fwd(glu, conv_w_full, conv_b, ln_g, ln_b)
    w_ap4, w_cp4, w_out4, w_fi4, w_dn4 = gather_finish("b", state_b, [o, cact])
    w_out_full, w_dn_full = w_out4.reshape(D, D), w_dn4.reshape(DFF, D)
    ya, yc, mg, x1 = _mix_out(xs, o, cact, gl, w_ap4, w_cp4, b_conv_proj, w_out_full)
    h2, gu, act = _ffn_in(x1, g_ffn_norm, w_fi4)
    dx2, dx2b, dg_final, loss_part = _ffn_out_loss(x1, act, w_dn_full, g_final2, target)

    def reduce_start(tag, names, grads):
        from_sibling = _pair_exchange("pair_exchange_" + tag, grads)
        pair = [_pair_sum(c_idx, g, r, "pair_sum_" + name) for name, g, r in zip(names, grads, from_sibling)]
        lands = [lax.empty((3,) + p.shape[1:], p.dtype) for p in pair]
        return _chip_start("chip_start_" + tag, _reduce_ends, REDUCE_PEERS, pair, lands)

    def reduce_finish(tag, names, state, after):
        send_sems, recv_sems, pair, lands, _ = state
        pair, lands = _chip_wait("chip_wait_" + tag, _reduce_ends, send_sems, recv_sems, pair, lands, after)
        halves = [_chip_sum(sc_idx, p, r, "chip_sum_" + name) for name, p, r in zip(names, pair, lands)]
        for name, both in zip(names, _pair_share("pair_share_" + tag, halves)):
            wv, mv, vv = big[name]
            res = _adamw(wv, both.reshape(wv.shape), mv, vv, "adamw_" + name)
            if name == "w_in":
                res = [a.T for a in res]
            out_g[name], out_d[name], out_m[name], out_v[name] = [a[None] for a in res]

    dgu, dx1, dx1b, dg_ffn = _ffn_bwd(dx2, dx2b, gu, x1, g_ffn_norm, w_dn_full, w_fi4)
    names_1 = ["w_ffn_in", "w_ffn_down"]
    state_1 = reduce_start("1", names_1, [_grad_w(h2, dgu, "grad_w_ffn_in", 512, FSH, True),
                                          _grad_w(act, dx2b, "grad_w_ffn_down", 256, D, False)])
    dya, dyc, dgl, do, dc, db_cp = _mix_bwd(dx1b, gl, ya, yc, w_out_full, w_ap4, w_cp4, state_1[4])
    names_2 = ["w_out", "w_attn_proj", "w_conv_proj"]
    state_2 = reduce_start("2", names_2, [_grad_w(mg, dx1b, "grad_w_out", 512, D, False),
                                          _grad_w(o, dya, "grad_w_attn_proj", 512, 256, True),
                                          _grad_w(cact, dyc, "grad_w_conv_proj", 512, 256, True)])
    dglu, dconv_w, dconv_b, dln_g, dln_b = _conv_bwd(glu, u, dc, conv_w_full, ln_g, ln_b, state_2[4])
    dq, dkv, dsinks = _attn_bwd(qkv, o, do, lse, sinks)
    grad_x, dg_mix, db_in = _in_proj_bwd(dq, dkv, dglu, dgl, xs, dx1, g_mix_norm, w_in_t_full)
    names_3 = ["w_in"]
    gw_in_t = _grad_w_in_t(h, dq, dkv, dglu, dgl)
    state_3 = reduce_start("3", names_3, [gw_in_t.reshape(N_CHIPS, INW // N_CHIPS, D)])

    reduce_finish("1", names_1, state_1, [state_3[4]])
    reduce_finish("2", names_2, state_2, [out_d["w_ffn_down"]])

    gs = {"g_mix_norm": dg_mix, "b_in": db_in, "sinks": dsinks[:, 0].reshape(1, NQ),
          "conv_w": dconv_w, "conv_b": dconv_b, "ln_g": dln_g, "ln_b": dln_b,
          "b_conv_proj": db_cp, "g_ffn_norm": dg_ffn, "g_final": dg_final}
    vec, cw = _pack_small(gs, loss_part[0, 0])
    vec_all, cw_all = _gather_small([vec, cw])

    def view(a, name):
        if name == "conv_w":
            return a[0]
        if name == "g_final":
            return a.reshape(1, D)
        return a

    wmv = [(view(w[name], name), view(m[name], name), view(v[name], name)) for name in SMALL_NAMES]
    small_out, loss_row = _small_update(s_idx, vec_all, cw_all, wmv)
    for name, (g, d, mm, vv) in zip(SMALL_NAMES, small_out):
        shape = w[name].shape
        out_g[name], out_d[name], out_m[name], out_v[name] = (
            g.reshape(shape), d.reshape(shape), mm.reshape(shape), vv.reshape(shape))

    reduce_finish("3", names_3, state_3, [loss_row, out_d["w_conv_proj"]])

    loss = loss_row[0, 0]
    return (loss, grad_x[None], *[out_g[k] for k in WEIGHT_ORDER], *[out_d[k] for k in WEIGHT_ORDER],
            *[out_m[k] for k in WEIGHT_ORDER], *[out_v[k] for k in WEIGHT_ORDER])
```

```python
import functools

import jax
import jax.numpy as jnp
from jax import lax
from jax.experimental import pallas as pl
from jax.experimental.pallas import tpu as pltpu

F32 = jnp.float32
BF16 = jnp.bfloat16

T = 2048
D = 1024
HD = 64
NQ = 8
NKV = 2
GROUP = NQ // NKV
BLK = 128
AW = NQ * HD
KVW = NKV * HD
C = 512
KW = 31
QKVW = AW + 2 * KVW
GLU_OFF = QKVW
GATE_OFF = GLU_OFF + 2 * C
INW = GATE_OFF + 2 * D
DFF = 2816
EPS = 1e-5
NEG = -1e30
SCALE = HD ** -0.5
HALO = 32
N_CHIPS = 4
FSH = 2 * DFF // N_CHIPS

ADAM_LR = 0.001
ADAM_B1 = 0.9
ADAM_B2 = 0.999
ADAM_EPS = 1e-08
ADAM_WD = 0.01
ADAM_STEP = 10

VMEM_LIMIT = 56 * 1024 * 1024
ROW_TM = 512
MESH = pl.DeviceIdType.MESH


def _params(*sem):
    return pltpu.CompilerParams(dimension_semantics=sem, vmem_limit_bytes=VMEM_LIMIT)


def _dot(a, b):
    return jnp.dot(a, b, preferred_element_type=F32)


def _dot_nt(a, b):
    return lax.dot_general(a, b, (((1,), (1,)), ((), ())), preferred_element_type=F32)


def _dot_tn(a, b):
    return lax.dot_general(a, b, (((0,), (0,)), ((), ())), preferred_element_type=F32)


def _sigmoid(v):
    return 1.0 / (1.0 + jnp.exp(-v))


def _rows(tm, n):
    return pl.BlockSpec((tm, n), lambda i: (i, 0))


def _whole(shape):
    return pl.BlockSpec(shape, lambda i: tuple(0 for _ in shape))


def _in_proj(x, g_mix, w_in_t, b_in):
    tm = ROW_TM

    def body(x_ref, g_ref, w_ref, b_ref, h_ref, qkv_ref, glu_ref, gl_ref):
        xv = x_ref[...]
        r = lax.rsqrt(jnp.mean(xv * xv, axis=-1, keepdims=True) + EPS)
        h = (xv * r * g_ref[...]).astype(BF16)
        h_ref[...] = h
        qkv_ref[...] = (_dot_nt(h, w_ref[0:GLU_OFF, :]) + b_ref[:, 0:GLU_OFF]).astype(BF16)
        glu_ref[...] = (_dot_nt(h, w_ref[GLU_OFF:GATE_OFF, :]) + b_ref[:, GLU_OFF:GATE_OFF]).astype(BF16)
        gl_ref[...] = (_dot_nt(h, w_ref[GATE_OFF:INW, :]) + b_ref[:, GATE_OFF:INW]).astype(BF16)

    return pl.pallas_call(
        body, name="in_proj", grid=(T // tm,),
        in_specs=[_rows(tm, D), _whole((1, D)), _whole((INW, D)), _whole((1, INW))],
        out_specs=[_rows(tm, D), _rows(tm, QKVW), _rows(tm, 2 * C), _rows(tm, 2 * D)],
        out_shape=[jax.ShapeDtypeStruct((T, D), BF16), jax.ShapeDtypeStruct((T, QKVW), BF16),
                   jax.ShapeDtypeStruct((T, 2 * C), BF16), jax.ShapeDtypeStruct((T, 2 * D), BF16)],
        compiler_params=_params("parallel"),
    )(x, g_mix, w_in_t, b_in)


GROWS = GROUP * BLK
BAND = 2 * BLK


def _band(i):
    rb = pl.multiple_of(jnp.maximum(i - 1, 0) * BLK, BLK)
    row = lax.broadcasted_iota(jnp.int32, (GROWS, BAND), 0)
    kpos = rb + lax.broadcasted_iota(jnp.int32, (GROWS, BAND), 1)
    qpos = i * BLK + jnp.bitwise_and(row, BLK - 1)
    return rb, jnp.logical_and(kpos <= qpos, kpos > qpos - BLK)


def _sink_column(sink_ref, g):
    head = lax.shift_right_logical(lax.broadcasted_iota(jnp.int32, (GROWS, 1), 0), 7)
    col = jnp.full((GROWS, 1), sink_ref[0, g * GROUP], F32)
    for hh in range(1, GROUP):
        col = jnp.where(head == hh, sink_ref[0, g * GROUP + hh], col)
    return col


def _attn_fwd(qkv, sinks):
    def body(sink_ref, qkv_ref, o_ref, lse_ref, s_ref, p_ref):
        i = pl.program_id(0)
        r0 = pl.multiple_of(i * BLK, BLK)
        rb, valid = _band(i)
        for g in range(NKV):
            kband = qkv_ref[pl.ds(rb, BAND), AW + g * HD:AW + (g + 1) * HD]
            vband = qkv_ref[pl.ds(rb, BAND), AW + KVW + g * HD:AW + KVW + (g + 1) * HD]
            for hh in range(GROUP):
                h = g * GROUP + hh
                s_ref[hh * BLK:(hh + 1) * BLK, :] = _dot_nt(qkv_ref[pl.ds(r0, BLK), h * HD:(h + 1) * HD], kband)
            s = jnp.where(valid, s_ref[...] * SCALE, NEG)
            sink = _sink_column(sink_ref, g)
            m = jnp.maximum(jnp.max(s, axis=-1, keepdims=True), sink)
            p = jnp.exp(s - m)
            den = jnp.sum(p, axis=-1, keepdims=True) + jnp.exp(sink - m)
            p_ref[...] = (p * (1.0 / den)).astype(BF16)
            lse = m + jnp.log(den)
            for hh in range(GROUP):
                h = g * GROUP + hh
                o_ref[:, h * HD:(h + 1) * HD] = _dot(p_ref[hh * BLK:(hh + 1) * BLK, :], vband).astype(BF16)
                lse_ref[:, h:h + 1] = lse[hh * BLK:(hh + 1) * BLK]

    return pl.pallas_call(
        body, name="attn_fwd", grid=(T // BLK,),
        in_specs=[pl.BlockSpec(memory_space=pltpu.SMEM), _whole((T, QKVW))],
        out_specs=[_rows(BLK, AW), _rows(BLK, NQ)],
        out_shape=[jax.ShapeDtypeStruct((T, AW), BF16), jax.ShapeDtypeStruct((T, NQ), F32)],
        scratch_shapes=[pltpu.VMEM((GROWS, BAND), F32), pltpu.VMEM((GROWS, BAND), BF16)],
        compiler_params=_params("parallel"),
    )(sinks, qkv)


CONV_TM = 256
CONV_SUB = 32


def _glu(ab):
    a = ab[:, 0:C].astype(F32)
    b = ab[:, C:2 * C].astype(F32)
    return a * _sigmoid(b)


SUBLANES = 8


def _shifted_copies(ref):
    rows = ref.shape[1] - SUBLANES
    for r in range(1, SUBLANES):
        ref[r, 0:rows, :] = ref[0, r:r + rows, :]


def _shifted_rows(ref, start, size):
    r = start % SUBLANES
    return ref[r, start - r:start - r + size, :]


def _conv_fwd(glu, conv_w, conv_b, ln_g, ln_b):
    tm = CONV_TM

    def body(cur_ref, prev_ref, w_ref, cb_ref, g_ref, b_ref, u_ref, c_ref, zs_ref):
        i = pl.program_id(0)
        zprev = _glu(prev_ref[tm - HALO:tm, :])
        zs_ref[0, 0:HALO, :] = jnp.where(i > 0, zprev, 0.0)
        zs_ref[0, HALO:HALO + tm, :] = _glu(cur_ref[...])
        _shifted_copies(zs_ref)
        for s in range(tm // CONV_SUB):
            base = HALO + s * CONV_SUB - (KW - 1)
            acc = jnp.broadcast_to(cb_ref[...], (CONV_SUB, C))
            for j in range(KW):
                acc = acc + w_ref[j:j + 1, :] * _shifted_rows(zs_ref, base + j, CONV_SUB)
            rows = slice(s * CONV_SUB, (s + 1) * CONV_SUB)
            u_ref[rows, :] = acc
            mu = jnp.mean(acc, axis=-1, keepdims=True)
            xc = acc - mu
            var = jnp.mean(xc * xc, axis=-1, keepdims=True)
            y = xc * lax.rsqrt(var + EPS) * g_ref[...] + b_ref[...]
            c_ref[rows, :] = (y * _sigmoid(y)).astype(BF16)

    return pl.pallas_call(
        body, name="conv_fwd", grid=(T // tm,),
        in_specs=[_rows(tm, 2 * C),
                  pl.BlockSpec((tm, 2 * C), lambda i: (jnp.maximum(i - 1, 0), 0)),
                  _whole((KW, C)), _whole((1, C)), _whole((1, C)), _whole((1, C))],
        out_specs=[_rows(tm, C), _rows(tm, C)],
        out_shape=[jax.ShapeDtypeStruct((T, C), F32), jax.ShapeDtypeStruct((T, C), BF16)],
        scratch_shapes=[pltpu.VMEM((SUBLANES, HALO + tm, C), F32)],
        compiler_params=_params("parallel"),
    )(glu, glu, conv_w, conv_b, ln_g, ln_b)


def _mix_out(x, o, cact, gl, w_ap, w_cp, b_cp, w_out):
    tm = ROW_TM

    def body(x_ref, o_ref, c_ref, gl_ref, wap_ref, wcp_ref, bcp_ref, wo_ref,
             ya_ref, yc_ref, mg_ref, x1_ref):
        ov, cv = o_ref[...], c_ref[...]
        ya = jnp.concatenate([_dot(ov, wap_ref[s]) for s in range(N_CHIPS)], axis=1)
        yc = jnp.concatenate([_dot(cv, wcp_ref[s]) for s in range(N_CHIPS)], axis=1) + bcp_ref[...]
        g0 = _sigmoid(gl_ref[:, 0:D].astype(F32))
        g1 = _sigmoid(gl_ref[:, D:2 * D].astype(F32))
        mg = (g0 * ya + g1 * yc).astype(BF16)
        ya_ref[...] = ya.astype(BF16)
        yc_ref[...] = yc.astype(BF16)
        mg_ref[...] = mg
        x1_ref[...] = x_ref[...] + _dot(mg, wo_ref[...])

    return pl.pallas_call(
        body, name="mix_out", grid=(T // tm,),
        in_specs=[_rows(tm, D), _rows(tm, AW), _rows(tm, C), _rows(tm, 2 * D),
                  _whole((N_CHIPS, AW, D // N_CHIPS)), _whole((N_CHIPS, C, D // N_CHIPS)), _whole((1, D)),
                  _whole((D, D))],
        out_specs=[_rows(tm, D), _rows(tm, D), _rows(tm, D), _rows(tm, D)],
        out_shape=[jax.ShapeDtypeStruct((T, D), BF16), jax.ShapeDtypeStruct((T, D), BF16),
                   jax.ShapeDtypeStruct((T, D), BF16), jax.ShapeDtypeStruct((T, D), F32)],
        compiler_params=_params("parallel"),
    )(x, o, cact, gl, w_ap, w_cp, b_cp, w_out)


def _ffn_in(x1, g_ffn, w_fi):
    tm = ROW_TM

    def body(x_ref, g_ref, w_ref, h_ref, gu_ref, act_ref):
        xv = x_ref[...]
        r = lax.rsqrt(jnp.mean(xv * xv, axis=-1, keepdims=True) + EPS)
        h = (xv * r * g_ref[...]).astype(BF16)
        h_ref[...] = h
        for s in range(N_CHIPS // 2):
            c0 = s * FSH
            gate = _dot(h, w_ref[s])
            up = _dot(h, w_ref[s + N_CHIPS // 2])
            gu_ref[:, c0:c0 + FSH] = gate.astype(BF16)
            gu_ref[:, DFF + c0:DFF + c0 + FSH] = up.astype(BF16)
            act_ref[:, c0:c0 + FSH] = (gate * _sigmoid(gate) * up).astype(BF16)

    return pl.pallas_call(
        body, name="ffn_in", grid=(T // tm,),
        in_specs=[_rows(tm, D), _whole((1, D)), _const((N_CHIPS, D, FSH))],
        out_specs=[_rows(tm, D), _rows(tm, 2 * DFF), _rows(tm, DFF)],
        out_shape=[jax.ShapeDtypeStruct((T, D), BF16), jax.ShapeDtypeStruct((T, 2 * DFF), BF16),
                   jax.ShapeDtypeStruct((T, DFF), BF16)],
        compiler_params=_params("parallel"),
    )(x1, g_ffn, w_fi)


def _ffn_out_loss(x1, act, w_dn, g_final, target):
    tm = ROW_TM

    def body(x_ref, a_ref, w_ref, g_ref, t_ref, dx_ref, dxb_ref, dg_ref, loss_ref):
        i = pl.program_id(0)
        x2 = x_ref[...] + _dot(a_ref[...], w_ref[...])
        r = lax.rsqrt(jnp.mean(x2 * x2, axis=-1, keepdims=True) + EPS)
        xh = x2 * r
        g = g_ref[...]
        err = xh * g - t_ref[...]
        dy = err * (1.0 / D)
        dyg = dy * g
        dx = r * (dyg - xh * jnp.mean(dyg * xh, axis=-1, keepdims=True))
        dx_ref[...] = dx
        dxb_ref[...] = dx.astype(BF16)
        part = 0.5 * jnp.sum(jnp.mean(err * err, axis=-1, keepdims=True), axis=0, keepdims=True)

        @pl.when(i == 0)
        def _():
            dg_ref[...] = jnp.zeros_like(dg_ref)
            loss_ref[...] = jnp.zeros_like(loss_ref)

        dg_ref[...] += jnp.sum(dy * xh, axis=0, keepdims=True)
        loss_ref[...] += jnp.broadcast_to(part, loss_ref.shape)

    return pl.pallas_call(
        body, name="ffn_out_loss", grid=(T // tm,),
        in_specs=[_rows(tm, D), _rows(tm, DFF), _whole((DFF, D)), _whole((1, D)), _rows(tm, D)],
        out_specs=[_rows(tm, D), _rows(tm, D), _whole((1, D)), _whole((1, 128))],
        out_shape=[jax.ShapeDtypeStruct((T, D), F32), jax.ShapeDtypeStruct((T, D), BF16),
                   jax.ShapeDtypeStruct((1, D), F32), jax.ShapeDtypeStruct((1, 128), F32)],
        compiler_params=_params("arbitrary"),
    )(x1, act, w_dn, g_final, target)


def _const(shape):
    return pl.BlockSpec(shape, lambda i: tuple(0 for _ in shape), pipeline_mode=pl.Buffered(1))


def _ffn_bwd(dx2, dx2b, gu, x1, g_ffn, w_dn_t, w_fi_t):
    tm = ROW_TM // 2

    def body(dx_ref, dxb_ref, gu_ref, x_ref, g_ref, wdn_ref, wfi_ref,
             dgu_ref, dx1_ref, dx1b_ref, dg_ref):
        i = pl.program_id(0)
        dxb = dxb_ref[...]
        dh = jnp.zeros((tm, D), F32)
        for k in range(N_CHIPS // 2):
            c0 = k * FSH
            dact = _dot_nt(dxb, wdn_ref[c0:c0 + FSH, :])
            gate = gu_ref[:, c0:c0 + FSH].astype(F32)
            up = gu_ref[:, DFF + c0:DFF + c0 + FSH].astype(F32)
            s = _sigmoid(gate)
            dup = (dact * gate * s).astype(BF16)
            dgate = (dact * up * s * (1.0 + gate * (1.0 - s))).astype(BF16)
            dgu_ref[:, c0:c0 + FSH] = dgate
            dgu_ref[:, DFF + c0:DFF + c0 + FSH] = dup
            dh = dh + _dot_nt(dgate, wfi_ref[k]) + _dot_nt(dup, wfi_ref[k + N_CHIPS // 2])
        xv = x_ref[...]
        r = lax.rsqrt(jnp.mean(xv * xv, axis=-1, keepdims=True) + EPS)
        xh = xv * r
        dhg = dh * g_ref[...]
        dx1 = dx_ref[...] + r * (dhg - xh * jnp.mean(dhg * xh, axis=-1, keepdims=True))
        dx1_ref[...] = dx1
        dx1b_ref[...] = dx1.astype(BF16)

        @pl.when(i == 0)
        def _():
            dg_ref[...] = jnp.zeros_like(dg_ref)

        dg_ref[...] += jnp.sum(dh * xh, axis=0, keepdims=True)

    return pl.pallas_call(
        body, name="ffn_bwd", grid=(T // tm,),
        in_specs=[_rows(tm, D), _rows(tm, D), _rows(tm, 2 * DFF), _rows(tm, D), _whole((1, D)),
                  _const((DFF, D)), _const((N_CHIPS, D, FSH))],
        out_specs=[_rows(tm, 2 * DFF), _rows(tm, D), _rows(tm, D), _whole((1, D))],
        out_shape=[jax.ShapeDtypeStruct((T, 2 * DFF), BF16), jax.ShapeDtypeStruct((T, D), F32),
                   jax.ShapeDtypeStruct((T, D), BF16), jax.ShapeDtypeStruct((1, D), F32)],
        compiler_params=_params("arbitrary"),
    )(dx2, dx2b, gu, x1, g_ffn, w_dn_t, w_fi_t)


def _mix_bwd(dx1b, gl, ya, yc, w_out_t, w_ap_t, w_cp_t, dep):
    tm = ROW_TM

    def body(dx_ref, gl_ref, ya_ref, yc_ref, wo_ref, wap_ref, wcp_ref, dep_ref,
             dya_ref, dyc_ref, dgl_ref, do_ref, dc_ref, db_ref):
        i = pl.program_id(0)
        dm = _dot_nt(dx_ref[...], wo_ref[...])
        g0 = _sigmoid(gl_ref[:, 0:D].astype(F32))
        g1 = _sigmoid(gl_ref[:, D:2 * D].astype(F32))
        dya = dm * g0
        dyc = dm * g1
        dgl_ref[:, 0:D] = (dya * ya_ref[...].astype(F32) * (1.0 - g0)).astype(BF16)
        dgl_ref[:, D:2 * D] = (dyc * yc_ref[...].astype(F32) * (1.0 - g1)).astype(BF16)
        dyab = dya.astype(BF16)
        dycb = dyc.astype(BF16)
        dya_ref[...] = dyab
        dyc_ref[...] = dycb
        sw = D // N_CHIPS
        do = jnp.zeros((tm, AW), F32)
        dcv = jnp.zeros((tm, C), F32)
        for s in range(N_CHIPS):
            do = do + _dot_nt(dyab[:, s * sw:(s + 1) * sw], wap_ref[s])
            dcv = dcv + _dot_nt(dycb[:, s * sw:(s + 1) * sw], wcp_ref[s])
        do_ref[...] = do.astype(BF16)
        dc_ref[...] = dcv

        @pl.when(i == 0)
        def _():
            db_ref[...] = jnp.zeros_like(db_ref)

        db_ref[...] += jnp.sum(dyc, axis=0, keepdims=True)

    return pl.pallas_call(
        body, name="mix_bwd", grid=(T // tm,),
        in_specs=[_rows(tm, D), _rows(tm, 2 * D), _rows(tm, D), _rows(tm, D),
                  _whole((D, D)), _whole((N_CHIPS, AW, D // N_CHIPS)), _whole((N_CHIPS, C, D // N_CHIPS)),
                  _whole((8, 128))],
        out_specs=[_rows(tm, D), _rows(tm, D), _rows(tm, 2 * D), _rows(tm, AW), _rows(tm, C),
                   _whole((1, D))],
        out_shape=[jax.ShapeDtypeStruct((T, D), BF16), jax.ShapeDtypeStruct((T, D), BF16),
                   jax.ShapeDtypeStruct((T, 2 * D), BF16), jax.ShapeDtypeStruct((T, AW), BF16),
                   jax.ShapeDtypeStruct((T, C), F32), jax.ShapeDtypeStruct((1, D), F32)],
        compiler_params=_params("arbitrary"),
    )(dx1b, gl, ya, yc, w_out_t, w_ap_t, w_cp_t, dep)


def _conv_bwd(glu, u, dc, conv_w, ln_g, ln_b, dep):
    tm = CONV_TM
    nblk = T // tm

    def du_of(uv, dcv, g_ref, b_ref):
        mu = jnp.mean(uv, axis=-1, keepdims=True)
        xc = uv - mu
        var = jnp.mean(xc * xc, axis=-1, keepdims=True)
        rstd = lax.rsqrt(var + EPS)
        xh = xc * rstd
        y = xh * g_ref[...] + b_ref[...]
        sg = _sigmoid(y)
        dy = dcv * (sg * (1.0 + y * (1.0 - sg)))
        dxh = dy * g_ref[...]
        du = rstd * (dxh - jnp.mean(dxh, axis=-1, keepdims=True)
                     - xh * jnp.mean(dxh * xh, axis=-1, keepdims=True))
        return du, dy, xh

    def body(cur_ref, prev_ref, u_ref, un_ref, dc_ref, dcn_ref, w_ref, g_ref, b_ref, dep_ref,
             dglu_ref, dw_ref, dcb_ref, dg_ref, db_ref, zs_ref, dus_ref):
        i = pl.program_id(0)

        @pl.when(i == 0)
        def _():
            dw_ref[...] = jnp.zeros_like(dw_ref)
            dcb_ref[...] = jnp.zeros_like(dcb_ref)
            dg_ref[...] = jnp.zeros_like(dg_ref)
            db_ref[...] = jnp.zeros_like(db_ref)

        zprev = _glu(prev_ref[tm - HALO:tm, :])
        zs_ref[0, 0:HALO, :] = jnp.where(i > 0, zprev, 0.0)
        zs_ref[0, HALO:HALO + tm, :] = _glu(cur_ref[...])
        _shifted_copies(zs_ref)

        dun, _, _ = du_of(un_ref[0:HALO, :], dcn_ref[0:HALO, :], g_ref, b_ref)
        dus_ref[0, tm:tm + HALO, :] = jnp.where(i < nblk - 1, dun, 0.0)
        dg_acc = jnp.zeros((1, C), F32)
        db_acc = jnp.zeros((1, C), F32)
        dcb_acc = jnp.zeros((1, C), F32)
        for s in range(tm // CONV_SUB):
            rows = slice(s * CONV_SUB, (s + 1) * CONV_SUB)
            du, dy, xh = du_of(u_ref[rows, :], dc_ref[rows, :], g_ref, b_ref)
            dus_ref[0, rows, :] = du
            dg_acc = dg_acc + jnp.sum(dy * xh, axis=0, keepdims=True)
            db_acc = db_acc + jnp.sum(dy, axis=0, keepdims=True)
            dcb_acc = dcb_acc + jnp.sum(du, axis=0, keepdims=True)
        dg_ref[...] += dg_acc
        db_ref[...] += db_acc
        dcb_ref[...] += dcb_acc
        _shifted_copies(dus_ref)

        for j in range(KW):
            acc = jnp.zeros((CONV_SUB, C), F32)
            for s in range(tm // CONV_SUB):
                base = HALO + s * CONV_SUB - (KW - 1) + j
                acc = acc + dus_ref[0, s * CONV_SUB:(s + 1) * CONV_SUB, :] * _shifted_rows(zs_ref, base, CONV_SUB)
            dw_ref[j:j + 1, :] += jnp.sum(acc, axis=0, keepdims=True)

        for s in range(tm // CONV_SUB):
            rows = slice(s * CONV_SUB, (s + 1) * CONV_SUB)
            dz = jnp.zeros((CONV_SUB, C), F32)
            for j in range(KW):
                dz = dz + w_ref[j:j + 1, :] * _shifted_rows(dus_ref, s * CONV_SUB + (KW - 1) - j, CONV_SUB)
            a = cur_ref[rows, 0:C].astype(F32)
            sb = _sigmoid(cur_ref[rows, C:2 * C].astype(F32))
            dglu_ref[rows, 0:C] = (dz * sb).astype(BF16)
            dglu_ref[rows, C:2 * C] = (dz * a * sb * (1.0 - sb)).astype(BF16)

    nxt = lambda i: (jnp.minimum(i + 1, nblk - 1), 0)
    return pl.pallas_call(
        body, name="conv_bwd", grid=(nblk,),
        in_specs=[_rows(tm, 2 * C),
                  pl.BlockSpec((tm, 2 * C), lambda i: (jnp.maximum(i - 1, 0), 0)),
                  _rows(tm, C), pl.BlockSpec((tm, C), nxt),
                  _rows(tm, C), pl.BlockSpec((tm, C), nxt),
                  _whole((KW, C)), _whole((1, C)), _whole((1, C)), _whole((8, 128))],
        out_specs=[_rows(tm, 2 * C), _whole((KW, C)), _whole((1, C)), _whole((1, C)), _whole((1, C))],
        out_shape=[jax.ShapeDtypeStruct((T, 2 * C), BF16), jax.ShapeDtypeStruct((KW, C), F32),
                   jax.ShapeDtypeStruct((1, C), F32), jax.ShapeDtypeStruct((1, C), F32),
                   jax.ShapeDtypeStruct((1, C), F32)],
        scratch_shapes=[pltpu.VMEM((SUBLANES, HALO + tm, C), F32), pltpu.VMEM((SUBLANES, tm + HALO, C), F32)],
        compiler_params=_params("arbitrary"),
    )(glu, glu, u, u, dc, dc, conv_w, ln_g, ln_b, dep)


def _attn_bwd(qkv, o, do, lse, sinks):
    def body(sink_ref, qkv_ref, o_ref, do_ref, lse_ref, dq_ref, dkv_ref, ds_ref,
             s_ref, dp_ref, p_ref, dsb_ref):
        i = pl.program_id(0)

        @pl.when(i == 0)
        def _():
            dkv_ref[...] = jnp.zeros_like(dkv_ref)
            ds_ref[...] = jnp.zeros_like(ds_ref)

        r0 = pl.multiple_of(i * BLK, BLK)
        rb, valid = _band(i)
        for g in range(NKV):
            kband = qkv_ref[pl.ds(rb, BAND), AW + g * HD:AW + (g + 1) * HD]
            vband = qkv_ref[pl.ds(rb, BAND), AW + KVW + g * HD:AW + KVW + (g + 1) * HD]
            lse_parts, dl_parts = [], []
            for hh in range(GROUP):
                h = g * GROUP + hh
                hcol = slice(h * HD, (h + 1) * HD)
                doh = do_ref[:, hcol]
                s_ref[hh * BLK:(hh + 1) * BLK, :] = _dot_nt(qkv_ref[pl.ds(r0, BLK), hcol], kband)
                dp_ref[hh * BLK:(hh + 1) * BLK, :] = _dot_nt(doh, vband)
                lse_parts.append(lse_ref[:, h:h + 1])
                dl_parts.append(jnp.sum(doh.astype(F32) * o_ref[:, hcol].astype(F32), axis=-1, keepdims=True))
            lse = jnp.concatenate(lse_parts, axis=0)
            dl = jnp.concatenate(dl_parts, axis=0)
            p = jnp.where(valid, jnp.exp(s_ref[...] * SCALE - lse), 0.0)
            p_ref[...] = p.astype(BF16)
            dsb_ref[...] = (p * (dp_ref[...] - dl)).astype(BF16)
            dsink = -(jnp.exp(_sink_column(sink_ref, g) - lse) * dl)
            dk = jnp.zeros((BAND, HD), F32)
            dv = jnp.zeros((BAND, HD), F32)
            for hh in range(GROUP):
                h = g * GROUP + hh
                hcol = slice(h * HD, (h + 1) * HD)
                rows = slice(hh * BLK, (hh + 1) * BLK)
                dq_ref[:, hcol] = (_dot(dsb_ref[rows, :], kband) * SCALE).astype(BF16)
                dk = dk + _dot_tn(dsb_ref[rows, :], qkv_ref[pl.ds(r0, BLK), hcol])
                dv = dv + _dot_tn(p_ref[rows, :], do_ref[:, hcol])
                ds_ref[h:h + 1, :] += jnp.broadcast_to(jnp.sum(dsink[rows], axis=0, keepdims=True), (1, 128))
            dkv_ref[pl.ds(rb, BAND), g * HD:(g + 1) * HD] += dk * SCALE
            dkv_ref[pl.ds(rb, BAND), KVW + g * HD:KVW + (g + 1) * HD] += dv

    return pl.pallas_call(
        body, name="attn_bwd", grid=(T // BLK,),
        in_specs=[pl.BlockSpec(memory_space=pltpu.SMEM), _whole((T, QKVW)),
                  _rows(BLK, AW), _rows(BLK, AW), _rows(BLK, NQ)],
        out_specs=[_rows(BLK, AW), _whole((T, 2 * KVW)), _whole((NQ, 128))],
        out_shape=[jax.ShapeDtypeStruct((T, AW), BF16), jax.ShapeDtypeStruct((T, 2 * KVW), F32),
                   jax.ShapeDtypeStruct((NQ, 128), F32)],
        scratch_shapes=[pltpu.VMEM((GROWS, BAND), F32), pltpu.VMEM((GROWS, BAND), F32),
                        pltpu.VMEM((GROWS, BAND), BF16), pltpu.VMEM((GROWS, BAND), BF16)],
        compiler_params=_params("arbitrary"),
    )(sinks, qkv, o, do, lse)


PROJ_PARTS = [(0, AW), (AW, QKVW), (GLU_OFF, GATE_OFF), (GATE_OFF, INW)]


def _in_proj_bwd(dq, dkv, dglu, dgl, x, dx1, g_mix, w_in_t):
    tm = ROW_TM

    def body(dq_ref, dkv_ref, dglu_ref, dgl_ref, x_ref, dx1_ref, g_ref, w_ref, gx_ref, dg_ref, db_ref):
        i = pl.program_id(0)

        @pl.when(i == 0)
        def _():
            dg_ref[...] = jnp.zeros_like(dg_ref)
            db_ref[...] = jnp.zeros_like(db_ref)

        dh = jnp.zeros((tm, D), F32)
        for part_ref, (lo, hi) in zip((dq_ref, dkv_ref, dglu_ref, dgl_ref), PROJ_PARTS):
            part = part_ref[...]
            dh = dh + _dot(part.astype(BF16), w_ref[lo:hi, :])
            db_ref[:, lo:hi] += jnp.sum(part.astype(F32), axis=0, keepdims=True)
        xv = x_ref[...]
        r = lax.rsqrt(jnp.mean(xv * xv, axis=-1, keepdims=True) + EPS)
        xh = xv * r
        dhg = dh * g_ref[...]
        gx_ref[...] = dx1_ref[...] + r * (dhg - xh * jnp.mean(dhg * xh, axis=-1, keepdims=True))
        dg_ref[...] += jnp.sum(dh * xh, axis=0, keepdims=True)

    return pl.pallas_call(
        body, name="in_proj_bwd", grid=(T // tm,),
        in_specs=[_rows(tm, AW), _rows(tm, 2 * KVW), _rows(tm, 2 * C), _rows(tm, 2 * D),
                  _rows(tm, D), _rows(tm, D), _whole((1, D)), _const((INW, D))],
        out_specs=[_rows(tm, D), _whole((1, D)), _whole((1, INW))],
        out_shape=[jax.ShapeDtypeStruct((T, D), F32), jax.ShapeDtypeStruct((1, D), F32),
                   jax.ShapeDtypeStruct((1, INW), F32)],
        compiler_params=_params("arbitrary"),
    )(dq, dkv, dglu, dgl, x, dx1, g_mix, w_in_t)


def _grad_w_in_t(h, dq, dkv, dglu, dgl):
    tn, chunk = 512, 256

    def body(h_ref, dq_ref, dkv_ref, dglu_ref, dgl_ref, o_ref, pt_ref):
        @pl.when(pl.program_id(0) == 0)
        def _():
            for part_ref, (lo, hi) in zip((dq_ref, dkv_ref, dglu_ref, dgl_ref), PROJ_PARTS):
                for c0 in range(0, hi - lo, chunk):
                    pt_ref[lo + c0:lo + c0 + chunk, :] = part_ref[:, c0:c0 + chunk].astype(BF16).T

        hv = h_ref[...]
        for r0 in range(0, INW, QKVW):
            o_ref[r0:r0 + QKVW, :] = _dot(pt_ref[r0:r0 + QKVW, :], hv).astype(BF16)

    return pl.pallas_call(
        body, name="grad_w_in", grid=(D // tn,),
        in_specs=[pl.BlockSpec((T, tn), lambda j: (0, j)), _const((T, AW)), _const((T, 2 * KVW)),
                  _const((T, 2 * C)), _const((T, 2 * D))],
        out_specs=pl.BlockSpec((INW, tn), lambda j: (0, j)),
        out_shape=jax.ShapeDtypeStruct((INW, D), BF16),
        scratch_shapes=[pltpu.VMEM((INW, T), BF16)],
        compiler_params=_params("arbitrary"),
    )(h, dq, dkv, dglu, dgl)


def _grad_w(a, b, name, tk, tn, col_sharded):
    k, n = a.shape[1], b.shape[1]

    def body(a_ref, b_ref, o_ref, at_ref):
        @pl.when(pl.program_id(1) == 0)
        def _():
            at_ref[...] = a_ref[...].T

        o_ref[...] = _dot(at_ref[...], b_ref[...]).astype(BF16)

    if col_sharded:
        per = n // N_CHIPS // tn
        shape = (N_CHIPS, k, n // N_CHIPS)
        out_map = lambda i, j: (j // per, i, j % per)
    else:
        shape = (1, k, n)
        out_map = lambda i, j: (0, i, j)
    out = pl.pallas_call(
        body, name=name, grid=(k // tk, n // tn),
        in_specs=[pl.BlockSpec((T, tk), lambda i, j: (0, i)), pl.BlockSpec((T, tn), lambda i, j: (0, j))],
        out_specs=pl.BlockSpec((None, tk, tn), out_map),
        out_shape=jax.ShapeDtypeStruct(shape, BF16),
        scratch_shapes=[pltpu.VMEM((tk, T), BF16)],
        compiler_params=_params("parallel", "arbitrary"),
    )(a, b)
    return out if col_sharded else out.reshape(N_CHIPS, k // N_CHIPS, n)


HBM_SPEC = pl.BlockSpec(memory_space=pltpu.HBM)


def _place():
    x, y, c = lax.axis_index("x"), lax.axis_index("y"), lax.axis_index("c")
    chips = [(1 - x, y), (x, 1 - y), (1 - x, 1 - y)]
    return x, y, c, chips


SEM_SPEC = pl.BlockSpec(memory_space=pltpu.SEMAPHORE)
ANY_SPEC = pl.BlockSpec(memory_space=pl.ANY)
VMEM_SPEC = pl.BlockSpec(memory_space=pltpu.VMEM)
EFFECT = pltpu.SideEffectType.DATAFLOW_SIDE_EFFECTING


def _gather_ends(src, land, x, y, c, chips):
    kh = src.shape[0] // 2
    s_me = 2 * x + y
    ends = [(src.at[pl.ds(c * kh, kh)], land.at[s_me, pl.ds(c * kh, kh)], (*chip, c)) for chip in chips]
    return ends + [(src, land.at[s_me], (x, y, 1 - c))]


def _reduce_ends(src, land, x, y, c, chips):
    return [(src.at[2 * chip[0] + chip[1]], land.at[j], (*chip, c)) for j, chip in enumerate(chips)]


def _chip_copies(ends, srcs, lands, send_sems, recv_sems, first=0):
    x, y, c, chips = _place()
    copies = []
    for src, land in zip(srcs, lands):
        peers = ends(src, land, x, y, c, chips)
        for s, d, to in peers:
            k = first * len(peers) + len(copies)
            copies.append(pltpu.make_async_remote_copy(
                src_ref=s, dst_ref=d, send_sem=send_sems.at[k], recv_sem=recv_sems.at[k],
                device_id=to, device_id_type=MESH))
    return copies


GATHER_PEERS, REDUCE_PEERS = 4, 3


def _chip_start(name, ends, peers, srcs, lands):
    n = len(srcs)

    def body(*refs):
        copies = _chip_copies(ends, refs[:n], refs[n:2 * n], refs[2 * n], refs[2 * n + 1])
        for cp in copies:
            cp.start()
        token = refs[-1]
        token[...] = jnp.zeros_like(token)

    hbm = lambda a: pltpu.HBM(a.shape, a.dtype)
    res = pl.pallas_call(
        body, name=name,
        out_shape=(pltpu.SemaphoreType.DMA((peers * n,)), pltpu.SemaphoreType.DMA((peers * n,)),
                   *[hbm(a) for a in srcs], *[hbm(a) for a in lands],
                   jax.ShapeDtypeStruct((8, 128), F32)),
        in_specs=[HBM_SPEC] * (2 * n),
        out_specs=(SEM_SPEC, SEM_SPEC, *[HBM_SPEC] * (2 * n), VMEM_SPEC),
        input_output_aliases={i: 2 + i for i in range(2 * n)},
        compiler_params=pltpu.CompilerParams(has_side_effects=EFFECT),
    )(*[pltpu.with_memory_space_constraint(a, pltpu.HBM) for a in (*srcs, *lands)])
    return res[0], res[1], list(res[2:2 + n]), list(res[2 + n:2 + 2 * n]), res[-1]


def _chip_wait(name, ends, send_sems, recv_sems, srcs, lands, after, first=0):
    n, na = len(srcs), len(after)

    def body(*refs):
        copies = _chip_copies(ends, refs[:n], refs[n:2 * n], refs[2 * n], refs[2 * n + 1], first)
        for cp in copies:
            cp.wait_send()
            cp.wait_recv()

    hbm = lambda a: pltpu.HBM(a.shape, a.dtype)
    res = pl.pallas_call(
        body, name=name,
        out_shape=tuple(hbm(a) for a in (*srcs, *lands)),
        in_specs=[HBM_SPEC] * (2 * n) + [SEM_SPEC, SEM_SPEC] + [ANY_SPEC] * na,
        out_specs=tuple([HBM_SPEC] * (2 * n)),
        input_output_aliases={i: i for i in range(2 * n)},
        compiler_params=pltpu.CompilerParams(has_side_effects=EFFECT),
    )(*srcs, *lands, send_sems, recv_sems, *after)
    return list(res[:n]), list(res[n:])


def _pair_forward(name, lands):
    n = len(lands)

    def body(*refs):
        bufs = refs[n:2 * n]
        send_sems, recv_sems = refs[2 * n:]
        x, y, c, chips = _place()

        def copy(w, j, cc, to):
            kh = bufs[w].shape[1] // 2
            blk = bufs[w].at[2 * chips[j][0] + chips[j][1], pl.ds(cc * kh, kh)]
            return pltpu.make_async_remote_copy(
                src_ref=blk, dst_ref=blk, send_sem=send_sems.at[3 * w + j],
                recv_sem=recv_sems.at[3 * w + j], device_id=to, device_id_type=MESH)

        sends = [copy(w, j, c, (x, y, 1 - c)) for w in range(n) for j in range(3)]
        for cp in sends:
            cp.start()
        for w in range(n):
            for j in range(3):
                copy(w, j, 1 - c, (x, y, c)).wait_recv()
        for cp in sends:
            cp.wait_send()

    return pl.pallas_call(
        body, name=name,
        in_specs=[HBM_SPEC] * n, out_specs=[HBM_SPEC] * n,
        out_shape=[jax.ShapeDtypeStruct(a.shape, a.dtype) for a in lands],
        input_output_aliases={w: w for w in range(n)},
        scratch_shapes=[pltpu.SemaphoreType.DMA((3 * n,)), pltpu.SemaphoreType.DMA((3 * n,))],
    )(*lands)


def _pair_exchange(name, grads):
    n = len(grads)

    def body(*refs):
        ins, outs = refs[:n], refs[n:2 * n]
        send_sems, recv_sems = refs[2 * n:]
        x, y, c, _ = _place()
        copies = []
        for w in range(n):
            kh = ins[w].shape[1] // 2
            cp = pltpu.make_async_remote_copy(
                src_ref=ins[w].at[:, pl.ds((1 - c) * kh, kh)], dst_ref=outs[w],
                send_sem=send_sems.at[w], recv_sem=recv_sems.at[w],
                device_id=(x, y, 1 - c), device_id_type=MESH)
            cp.start()
            copies.append(cp)
        for cp in copies:
            cp.wait()

    return pl.pallas_call(
        body, name=name,
        in_specs=[HBM_SPEC] * n, out_specs=[HBM_SPEC] * n,
        out_shape=[jax.ShapeDtypeStruct((g.shape[0], g.shape[1] // 2, g.shape[2]), g.dtype) for g in grads],
        scratch_shapes=[pltpu.SemaphoreType.DMA((n,)), pltpu.SemaphoreType.DMA((n,))],
    )(*grads)


def _row_tile(k):
    for t in (256, 240, 128, 176, 64, 32, 16):
        if k % t == 0:
            return t
    raise ValueError(k)


def _pair_sum(c_idx, g, got, name):
    _, k, n = g.shape
    kh = k // 2
    tm = _row_tile(kh)
    nb = kh // tm

    def body(c_ref, g_ref, r_ref, o_ref):
        o_ref[...] = (g_ref[...].astype(F32) + r_ref[...].astype(F32)).astype(BF16)

    return pl.pallas_call(
        body, name=name,
        grid_spec=pltpu.PrefetchScalarGridSpec(
            num_scalar_prefetch=1, grid=(N_CHIPS, nb),
            in_specs=[pl.BlockSpec((1, tm, n), lambda s, i, c_ref: (s, c_ref[0] * nb + i, 0)),
                      pl.BlockSpec((1, tm, n), lambda s, i, c_ref: (s, i, 0))],
            out_specs=pl.BlockSpec((1, tm, n), lambda s, i, c_ref: (s, i, 0))),
        out_shape=jax.ShapeDtypeStruct((N_CHIPS, kh, n), BF16),
        compiler_params=_params("parallel", "parallel"),
    )(c_idx, g, got)


def _chip_sum(sc_idx, mine, got, name):
    _, kh, n = mine.shape
    tm = _row_tile(kh)

    def body(sc_ref, m_ref, r_ref, o_ref):
        acc = m_ref[0].astype(F32)
        for j in range(3):
            acc = acc + r_ref[j].astype(F32)
        o_ref[0] = acc

    return pl.pallas_call(
        body, name=name,
        grid_spec=pltpu.PrefetchScalarGridSpec(
            num_scalar_prefetch=1, grid=(kh // tm,),
            in_specs=[pl.BlockSpec((1, tm, n), lambda i, sc_ref: (sc_ref[0], i, 0)),
                      pl.BlockSpec((3, tm, n), lambda i, sc_ref: (0, i, 0))],
            out_specs=pl.BlockSpec((1, tm, n), lambda i, sc_ref: (sc_ref[1], i, 0))),
        out_shape=jax.ShapeDtypeStruct((2, kh, n), F32),
        compiler_params=_params("parallel"),
    )(sc_idx, mine, got)


def _pair_share(name, halves):
    n = len(halves)

    def body(*refs):
        bufs = refs[n:2 * n]
        send_sems, recv_sems = refs[2 * n:]
        x, y, c, _ = _place()
        copies = []
        for w in range(n):
            cp = pltpu.make_async_remote_copy(
                src_ref=bufs[w].at[c], dst_ref=bufs[w].at[c],
                send_sem=send_sems.at[w], recv_sem=recv_sems.at[w],
                device_id=(x, y, 1 - c), device_id_type=MESH)
            cp.start()
            copies.append(cp)
        for w in range(n):
            copies[w].wait_send()
            pltpu.make_async_remote_copy(
                src_ref=bufs[w].at[1 - c], dst_ref=bufs[w].at[1 - c],
                send_sem=send_sems.at[w], recv_sem=recv_sems.at[w],
                device_id=(x, y, c), device_id_type=MESH).wait_recv()

    return pl.pallas_call(
        body, name=name,
        in_specs=[HBM_SPEC] * n, out_specs=[HBM_SPEC] * n,
        out_shape=[jax.ShapeDtypeStruct(h.shape, h.dtype) for h in halves],
        input_output_aliases={w: w for w in range(n)},
        scratch_shapes=[pltpu.SemaphoreType.DMA((n,)), pltpu.SemaphoreType.DMA((n,))],
    )(*halves)


def _gather_small(blocks):
    n = len(blocks)

    def body(*refs):
        ins, outs = refs[:n], refs[n:2 * n]
        send_sems, recv_sems, local_sems = refs[2 * n:]
        x, y, c, chips = _place()
        me, sibling = (x, y, c), (x, y, 1 - c)

        def rows(w, px, py, pc):
            m = ins[w].shape[0]
            return outs[w].at[pl.ds((4 * px + 2 * py + pc) * m, m), :]

        def copy(w, k, block, to, src=None):
            return pltpu.make_async_remote_copy(
                src_ref=rows(w, *block) if src is None else src, dst_ref=rows(w, *block),
                send_sem=send_sems.at[7 * w + k], recv_sem=recv_sems.at[7 * w + k],
                device_id=to, device_id_type=MESH)

        started, owns = [], []
        for w in range(n):
            own = pltpu.make_async_copy(ins[w], rows(w, *me), local_sems.at[w])
            own.start()
            owns.append(own)
            first = [copy(w, 0, me, sibling, src=ins[w])]
            first += [copy(w, 1 + j, me, (*chip, c), src=ins[w]) for j, chip in enumerate(chips)]
            for cp in first:
                cp.start()
            started += first
        for w in range(n):
            for j, chip in enumerate(chips):
                copy(w, 1 + j, (*chip, c), me).wait_recv()
                cp = copy(w, 4 + j, (*chip, c), sibling)
                cp.start()
                started.append(cp)
        for w in range(n):
            copy(w, 0, sibling, me).wait_recv()
            for j, chip in enumerate(chips):
                copy(w, 4 + j, (*chip, 1 - c), me).wait_recv()
        for cp in started:
            cp.wait_send()
        for own in owns:
            own.wait()

    vmem = pl.BlockSpec(memory_space=pltpu.VMEM)
    return pl.pallas_call(
        body, name="gather_small",
        in_specs=[vmem] * n, out_specs=[vmem] * n,
        out_shape=[jax.ShapeDtypeStruct((8 * b.shape[0], b.shape[1]), b.dtype) for b in blocks],
        scratch_shapes=[pltpu.SemaphoreType.DMA((7 * n,)), pltpu.SemaphoreType.DMA((7 * n,)),
                        pltpu.SemaphoreType.DMA((n,))],
    )(*blocks)


def _adamw_math(w, g, m, v):
    m = ADAM_B1 * m + (1.0 - ADAM_B1) * g
    v = ADAM_B2 * v + (1.0 - ADAM_B2) * (g * g)
    m_hat = m / (1.0 - ADAM_B1 ** ADAM_STEP)
    v_hat = v / (1.0 - ADAM_B2 ** ADAM_STEP)
    delta = -ADAM_LR * (m_hat / (jnp.sqrt(v_hat) + ADAM_EPS) + ADAM_WD * w)
    return delta, m, v


def _adamw(w, g, m, v, name):
    k, n = w.shape
    tm = k // 4

    def body(w_ref, g_ref, m_ref, v_ref, go_ref, d_ref, mo_ref, vo_ref):
        g = g_ref[...]
        d, mm, vv = _adamw_math(w_ref[...], g, m_ref[...], v_ref[...])
        go_ref[...] = g
        d_ref[...] = d
        mo_ref[...] = mm
        vo_ref[...] = vv

    spec = pl.BlockSpec((tm, n), lambda i: (i, 0))
    shp = jax.ShapeDtypeStruct((k, n), F32)
    return pl.pallas_call(
        body, name=name, grid=(4,), in_specs=[spec] * 4, out_specs=[spec] * 4,
        out_shape=[shp] * 4, compiler_params=_params("parallel"),
    )(w, g, m, v)


VEC_SLOTS = {
    "g_mix_norm": (0, 0, D), "b_conv_proj": (0, D, D), "g_ffn_norm": (0, 2 * D, D),
    "g_final": (0, 3 * D, D), "b_in": (1, 0, INW), "conv_b": (2, 0, C), "ln_g": (2, C, C),
    "ln_b": (2, 2 * C, C), "sinks": (2, 3 * C, NQ), "loss": (2, 3 * C + 128, 1),
}
VEC_ROWS, VEC_COLS = 8, 4 * D
CW_ROWS = 32
SMALL_NAMES = ["g_mix_norm", "b_in", "sinks", "conv_w", "conv_b", "ln_g", "ln_b",
               "b_conv_proj", "g_ffn_norm", "g_final"]
CW_LANES = C // N_CHIPS


def _pack_small(gs, loss):
    row0 = jnp.concatenate([gs["g_mix_norm"], gs["b_conv_proj"], gs["g_ffn_norm"], gs["g_final"]], axis=1)
    row1 = jnp.pad(gs["b_in"], ((0, 0), (0, VEC_COLS - INW)))
    row2 = jnp.concatenate([gs["conv_b"], gs["ln_g"], gs["ln_b"],
                            jnp.pad(gs["sinks"], ((0, 0), (0, 128 - NQ))),
                            jnp.pad(loss.reshape(1, 1), ((0, 0), (0, VEC_COLS - 3 * C - 129)))], axis=1)
    vec = jnp.concatenate([row0, row1, row2, jnp.zeros((VEC_ROWS - 3, VEC_COLS), F32)], axis=0)
    cw = jnp.pad(gs["conv_w"], ((0, CW_ROWS - KW), (0, 0)))
    return vec, cw


def _small_update(s_idx, vec_all, cw_all, wmv):
    nsm = len(SMALL_NAMES)

    def body(s_ref, vec_ref, cw_ref, *refs):
        ins = refs[:3 * nsm]
        outs = refs[3 * nsm:7 * nsm]
        loss_ref = refs[7 * nsm]

        def total(slot):
            row, lane, width = slot
            acc = vec_ref[row:row + 1, lane:lane + width]
            for k in range(1, 8):
                acc = acc + vec_ref[k * VEC_ROWS + row:k * VEC_ROWS + row + 1, lane:lane + width]
            return acc

        loss_ref[...] = jnp.broadcast_to(total(VEC_SLOTS["loss"]), loss_ref.shape)
        for p, name in enumerate(SMALL_NAMES):
            w_ref, m_ref, v_ref = ins[3 * p:3 * p + 3]
            g_ref, d_ref, mo_ref, vo_ref = outs[4 * p:4 * p + 4]
            if name == "conv_w":
                g = jnp.zeros((KW, CW_LANES), F32)
                for s in range(N_CHIPS):
                    cand = cw_ref[0:KW, s * CW_LANES:(s + 1) * CW_LANES]
                    for k in range(1, 8):
                        cand = cand + cw_ref[k * CW_ROWS:k * CW_ROWS + KW, s * CW_LANES:(s + 1) * CW_LANES]
                    g = jnp.where(s_ref[0] == s, cand, g)
            else:
                g = total(VEC_SLOTS[name])
            d, mm, vv = _adamw_math(w_ref[...], g, m_ref[...], v_ref[...])
            g_ref[...] = g
            d_ref[...] = d
            mo_ref[...] = mm
            vo_ref[...] = vv

    vmem = pl.BlockSpec(memory_space=pltpu.VMEM)
    flat = [a for t in wmv for a in t]
    out_shape = []
    for w, _, _ in wmv:
        out_shape += [jax.ShapeDtypeStruct(w.shape, F32)] * 4
    out_shape.append(jax.ShapeDtypeStruct((1, 128), F32))
    res = pl.pallas_call(
        body, name="small_update",
        in_specs=[pl.BlockSpec(memory_space=pltpu.SMEM)] + [vmem] * (2 + len(flat)),
        out_specs=[vmem] * len(out_shape), out_shape=out_shape,
    )(s_idx, vec_all, cw_all, *flat)
    return [tuple(res[4 * p:4 * p + 4]) for p in range(nsm)], res[4 * nsm]


WEIGHT_ORDER = ["g_mix_norm", "w_in", "b_in", "sinks", "conv_w", "conv_b", "ln_g", "ln_b",
                "w_attn_proj", "w_conv_proj", "b_conv_proj", "w_out", "g_ffn_norm", "w_ffn_in",
                "w_ffn_down", "g_final"]


def kernel(x, g_mix_norm, w_in, b_in, sinks, conv_w, conv_b, ln_g, ln_b, w_attn_proj, w_conv_proj, b_conv_proj, w_out, g_ffn_norm, w_ffn_in, w_ffn_down, g_final, loss_target, m_g_mix_norm, m_w_in, m_b_in, m_sinks, m_conv_w, m_conv_b, m_ln_g, m_ln_b, m_w_attn_proj, m_w_conv_proj, m_b_conv_proj, m_w_out, m_g_ffn_norm, m_w_ffn_in, m_w_ffn_down, m_g_final, v_g_mix_norm, v_w_in, v_b_in, v_sinks, v_conv_w, v_conv_b, v_ln_g, v_ln_b, v_w_attn_proj, v_w_conv_proj, v_b_conv_proj, v_w_out, v_g_ffn_norm, v_w_ffn_in, v_w_ffn_down, v_g_final):
    w = dict(g_mix_norm=g_mix_norm, w_in=w_in, b_in=b_in, sinks=sinks, conv_w=conv_w, conv_b=conv_b,
             ln_g=ln_g, ln_b=ln_b, w_attn_proj=w_attn_proj, w_conv_proj=w_conv_proj,
             b_conv_proj=b_conv_proj, w_out=w_out, g_ffn_norm=g_ffn_norm, w_ffn_in=w_ffn_in,
             w_ffn_down=w_ffn_down, g_final=g_final)
    m = dict(g_mix_norm=m_g_mix_norm, w_in=m_w_in, b_in=m_b_in, sinks=m_sinks, conv_w=m_conv_w,
             conv_b=m_conv_b, ln_g=m_ln_g, ln_b=m_ln_b, w_attn_proj=m_w_attn_proj,
             w_conv_proj=m_w_conv_proj, b_conv_proj=m_b_conv_proj, w_out=m_w_out,
             g_ffn_norm=m_g_ffn_norm, w_ffn_in=m_w_ffn_in, w_ffn_down=m_w_ffn_down, g_final=m_g_final)
    v = dict(g_mix_norm=v_g_mix_norm, w_in=v_w_in, b_in=v_b_in, sinks=v_sinks, conv_w=v_conv_w,
             conv_b=v_conv_b, ln_g=v_ln_g, ln_b=v_ln_b, w_attn_proj=v_w_attn_proj,
             w_conv_proj=v_w_conv_proj, b_conv_proj=v_b_conv_proj, w_out=v_w_out,
             g_ffn_norm=v_g_ffn_norm, w_ffn_in=v_w_ffn_in, w_ffn_down=v_w_ffn_down, g_final=v_g_final)

    c_idx = lax.axis_index("c").astype(jnp.int32).reshape(1)
    s_idx = (2 * lax.axis_index("x") + lax.axis_index("y")).astype(jnp.int32).reshape(1)

    sc_idx = jnp.concatenate([s_idx, c_idx])
    out_g, out_d, out_m, out_v = {}, {}, {}, {}

    def gather_start(tag, shards):
        lands = [lax.empty((N_CHIPS,) + s.shape, s.dtype) for s in shards]
        return _chip_start("gather_start_" + tag, _gather_ends, GATHER_PEERS, shards, lands)

    def gather_finish(tag, state, after, first=0, count=None):
        send_sems, recv_sems, shards, lands, _ = state
        last = len(shards) if count is None else first + count
        _, lands = _chip_wait("gather_wait_" + tag, _gather_ends, send_sems, recv_sems,
                              shards[first:last], lands[first:last], after, first)
        return _pair_forward("pair_forward_" + tag, lands)

    names_b = ["w_attn_proj", "w_conv_proj", "w_out", "w_ffn_in", "w_ffn_down"]
    big = {name: (w[name][0], m[name][0], v[name][0]) for name in names_b}
    big["w_in"] = (w_in[0].T, m_w_in[0].T, v_w_in[0].T)
    state_a = gather_start("a", [big["w_in"][0].astype(BF16), jnp.pad(conv_w[0], ((0, CW_ROWS - KW), (0, 0)))])
    state_b = gather_start("b", [(big[name][0] + state_a[4][0, 0]).astype(BF16) for name in names_b])
    got_a = gather_finish("a", state_a, [state_b[4]])
    w_in_t_full = got_a[0].reshape(INW, D)
    conv_w_full = got_a[1].transpose(1, 0, 2).reshape(CW_ROWS, C)[:KW]

    xs, target = x[0], loss_target[0]
    g_final2 = g_final.reshape(1, D)
    h, qkv, glu, gl = _in_proj(xs, g_mix_norm, w_in_t_full, b_in)
    o, lse = _attn_fwd(qkv, sinks)
    u, cact = _conv_fwd(glu, conv_w_full, conv_b, ln_g, ln_b)
    w_ap4, w_cp4, w_out4 = gather_finish("b1", state_b, [o, cact], 0, 3)
    w_out_full = w_out4.reshape(D, D)
    ya, yc, mg, x1 = _mix_out(xs, o, cact, gl, w_ap4, w_cp4, b_conv_proj, w_out_full)
    w_fi4, = gather_finish("b2", state_b, [x1], 3, 1)
    h2, gu, act = _ffn_in(x1, g_ffn_norm, w_fi4)
    w_dn4, = gather_finish("b3", state_b, [act], 4, 1)
    w_dn_full = w_dn4.reshape(DFF, D)
    dx2, dx2b, dg_final, loss_part = _ffn_out_loss(x1, act, w_dn_full, g_final2, target)

    def reduce_start(tag, names, grads):
        from_sibling = _pair_exchange("pair_exchange_" + tag, grads)
        pair = [_pair_sum(c_idx, g, r, "pair_sum_" + name) for name, g, r in zip(names, grads, from_sibling)]
        lands = [lax.empty((3,) + p.shape[1:], p.dtype) for p in pair]
        return _chip_start("chip_start_" + tag, _reduce_ends, REDUCE_PEERS, pair, lands)

    def reduce_finish(tag, names, state, after):
        send_sems, recv_sems, pair, lands, _ = state
        pair, lands = _chip_wait("chip_wait_" + tag, _reduce_ends, send_sems, recv_sems, pair, lands, after)
        halves = [_chip_sum(sc_idx, p, r, "chip_sum_" + name) for name, p, r in zip(names, pair, lands)]
        for name, both in zip(names, _pair_share("pair_share_" + tag, halves)):
            wv, mv, vv = big[name]
            res = _adamw(wv, both.reshape(wv.shape), mv, vv, "adamw_" + name)
            if name == "w_in":
                res = [a.T for a in res]
            out_g[name], out_d[name], out_m[name], out_v[name] = [a[None] for a in res]

    dgu, dx1, dx1b, dg_ffn = _ffn_bwd(dx2, dx2b, gu, x1, g_ffn_norm, w_dn_full, w_fi4)
    names_1 = ["w_ffn_in", "w_ffn_down"]
    state_1 = reduce_start("1", names_1, [_grad_w(h2, dgu, "grad_w_ffn_in", 512, FSH, True),
                                          _grad_w(act, dx2b, "grad_w_ffn_down", 256, D, False)])
    dya, dyc, dgl, do, dc, db_cp = _mix_bwd(dx1b, gl, ya, yc, w_out_full, w_ap4, w_cp4, state_1[4])
    names_2 = ["w_out", "w_attn_proj", "w_conv_proj"]
    state_2 = reduce_start("2", names_2, [_grad_w(mg, dx1b, "grad_w_out", 512, D, False),
                                          _grad_w(o, dya, "grad_w_attn_proj", 512, 256, True),
                                          _grad_w(cact, dyc, "grad_w_conv_proj", 512, 256, True)])
    dglu, dconv_w, dconv_b, dln_g, dln_b = _conv_bwd(glu, u, dc, conv_w_full, ln_g, ln_b, state_2[4])
    dq, dkv, dsinks = _attn_bwd(qkv, o, do, lse, sinks)
    grad_x, dg_mix, db_in = _in_proj_bwd(dq, dkv, dglu, dgl, xs, dx1, g_mix_norm, w_in_t_full)
    names_3 = ["w_in"]
    gw_in_t = _grad_w_in_t(h, dq, dkv, dglu, dgl)
    state_3 = reduce_start("3", names_3, [gw_in_t.reshape(N_CHIPS, INW // N_CHIPS, D)])

    reduce_finish("1", names_1, state_1, [state_3[4]])
    reduce_finish("2", names_2, state_2, [out_d["w_ffn_down"]])

    gs = {"g_mix_norm": dg_mix, "b_in": db_in, "sinks": dsinks[:, 0].reshape(1, NQ),
          "conv_w": dconv_w, "conv_b": dconv_b, "ln_g": dln_g, "ln_b": dln_b,
          "b_conv_proj": db_cp, "g_ffn_norm": dg_ffn, "g_final": dg_final}
    vec, cw = _pack_small(gs, loss_part[0, 0])
    vec_all, cw_all = _gather_small([vec, cw])

    def view(a, name):
        if name == "conv_w":
            return a[0]
        if name == "g_final":
            return a.reshape(1, D)
        return a

    wmv = [(view(w[name], name), view(m[name], name), view(v[name], name)) for name in SMALL_NAMES]
    small_out, loss_row = _small_update(s_idx, vec_all, cw_all, wmv)
    for name, (g, d, mm, vv) in zip(SMALL_NAMES, small_out):
        shape = w[name].shape
        out_g[name], out_d[name], out_m[name], out_v[name] = (
            g.reshape(shape), d.reshape(shape), mm.reshape(shape), vv.reshape(shape))

    reduce_finish("3", names_3, state_3, [loss_row, out_d["w_conv_proj"]])

    loss = loss_row[0, 0]
    return (loss, grad_x[None], *[out_g[k] for k in WEIGHT_ORDER], *[out_d[k] for k in WEIGHT_ORDER],
            *[out_m[k] for k in WEIGHT_ORDER], *[out_v[k] for k in WEIGHT_ORDER])
```

```python
import functools

import jax
import jax.numpy as jnp
from jax import lax
from jax.experimental import pallas as pl
from jax.experimental.pallas import tpu as pltpu

F32 = jnp.float32
BF16 = jnp.bfloat16

T = 2048
D = 1024
HD = 64
NQ = 8
NKV = 2
GROUP = NQ // NKV
BLK = 128
AW = NQ * HD
KVW = NKV * HD
C = 512
KW = 31
QKVW = AW + 2 * KVW
GLU_OFF = QKVW
GATE_OFF = GLU_OFF + 2 * C
INW = GATE_OFF + 2 * D
DFF = 2816
EPS = 1e-5
NEG = -1e30
SCALE = HD ** -0.5
HALO = 32
N_CHIPS = 4
FSH = 2 * DFF // N_CHIPS

ADAM_LR = 0.001
ADAM_B1 = 0.9
ADAM_B2 = 0.999
ADAM_EPS = 1e-08
ADAM_WD = 0.01
ADAM_STEP = 10

VMEM_LIMIT = 56 * 1024 * 1024
ROW_TM = 512
MESH = pl.DeviceIdType.MESH


def _params(*sem):
    return pltpu.CompilerParams(dimension_semantics=sem, vmem_limit_bytes=VMEM_LIMIT)


def _dot(a, b):
    return jnp.dot(a, b, preferred_element_type=F32)


def _dot_nt(a, b):
    return lax.dot_general(a, b, (((1,), (1,)), ((), ())), preferred_element_type=F32)


def _dot_tn(a, b):
    return lax.dot_general(a, b, (((0,), (0,)), ((), ())), preferred_element_type=F32)


def _sigmoid(v):
    return 1.0 / (1.0 + jnp.exp(-v))


def _rows(tm, n):
    return pl.BlockSpec((tm, n), lambda i: (i, 0))


def _whole(shape):
    return pl.BlockSpec(shape, lambda i: tuple(0 for _ in shape))


def _in_proj(x, g_mix, w_in_t, b_in):
    tm = ROW_TM

    def body(x_ref, g_ref, w_ref, b_ref, h_ref, qkv_ref, glu_ref, gl_ref):
        xv = x_ref[...]
        r = lax.rsqrt(jnp.mean(xv * xv, axis=-1, keepdims=True) + EPS)
        h = (xv * r * g_ref[...]).astype(BF16)
        h_ref[...] = h
        qkv_ref[...] = (_dot_nt(h, w_ref[0:GLU_OFF, :]) + b_ref[:, 0:GLU_OFF]).astype(BF16)
        glu_ref[...] = (_dot_nt(h, w_ref[GLU_OFF:GATE_OFF, :]) + b_ref[:, GLU_OFF:GATE_OFF]).astype(BF16)
        gl_ref[...] = (_dot_nt(h, w_ref[GATE_OFF:INW, :]) + b_ref[:, GATE_OFF:INW]).astype(BF16)

    return pl.pallas_call(
        body, name="in_proj", grid=(T // tm,),
        in_specs=[_rows(tm, D), _whole((1, D)), _whole((INW, D)), _whole((1, INW))],
        out_specs=[_rows(tm, D), _rows(tm, QKVW), _rows(tm, 2 * C), _rows(tm, 2 * D)],
        out_shape=[jax.ShapeDtypeStruct((T, D), BF16), jax.ShapeDtypeStruct((T, QKVW), BF16),
                   jax.ShapeDtypeStruct((T, 2 * C), BF16), jax.ShapeDtypeStruct((T, 2 * D), BF16)],
        compiler_params=_params("parallel"),
    )(x, g_mix, w_in_t, b_in)


GROWS = GROUP * BLK
BAND = 2 * BLK


def _band(i):
    rb = pl.multiple_of(jnp.maximum(i - 1, 0) * BLK, BLK)
    row = lax.broadcasted_iota(jnp.int32, (GROWS, BAND), 0)
    kpos = rb + lax.broadcasted_iota(jnp.int32, (GROWS, BAND), 1)
    qpos = i * BLK + jnp.bitwise_and(row, BLK - 1)
    return rb, jnp.logical_and(kpos <= qpos, kpos > qpos - BLK)


def _sink_column(sink_ref, g):
    head = lax.shift_right_logical(lax.broadcasted_iota(jnp.int32, (GROWS, 1), 0), 7)
    col = jnp.full((GROWS, 1), sink_ref[0, g * GROUP], F32)
    for hh in range(1, GROUP):
        col = jnp.where(head == hh, sink_ref[0, g * GROUP + hh], col)
    return col


def _attn_fwd(qkv, sinks):
    def body(sink_ref, qkv_ref, o_ref, lse_ref, s_ref, p_ref):
        i = pl.program_id(0)
        r0 = pl.multiple_of(i * BLK, BLK)
        rb, valid = _band(i)
        for g in range(NKV):
            kband = qkv_ref[pl.ds(rb, BAND), AW + g * HD:AW + (g + 1) * HD]
            vband = qkv_ref[pl.ds(rb, BAND), AW + KVW + g * HD:AW + KVW + (g + 1) * HD]
            for hh in range(GROUP):
                h = g * GROUP + hh
                s_ref[hh * BLK:(hh + 1) * BLK, :] = _dot_nt(qkv_ref[pl.ds(r0, BLK), h * HD:(h + 1) * HD], kband)
            s = jnp.where(valid, s_ref[...] * SCALE, NEG)
            sink = _sink_column(sink_ref, g)
            m = jnp.maximum(jnp.max(s, axis=-1, keepdims=True), sink)
            p = jnp.exp(s - m)
            den = jnp.sum(p, axis=-1, keepdims=True) + jnp.exp(sink - m)
            p_ref[...] = (p * (1.0 / den)).astype(BF16)
            lse = m + jnp.log(den)
            for hh in range(GROUP):
                h = g * GROUP + hh
                o_ref[:, h * HD:(h + 1) * HD] = _dot(p_ref[hh * BLK:(hh + 1) * BLK, :], vband).astype(BF16)
                lse_ref[:, h:h + 1] = lse[hh * BLK:(hh + 1) * BLK]

    return pl.pallas_call(
        body, name="attn_fwd", grid=(T // BLK,),
        in_specs=[pl.BlockSpec(memory_space=pltpu.SMEM), _whole((T, QKVW))],
        out_specs=[_rows(BLK, AW), _rows(BLK, NQ)],
        out_shape=[jax.ShapeDtypeStruct((T, AW), BF16), jax.ShapeDtypeStruct((T, NQ), F32)],
        scratch_shapes=[pltpu.VMEM((GROWS, BAND), F32), pltpu.VMEM((GROWS, BAND), BF16)],
        compiler_params=_params("parallel"),
    )(sinks, qkv)


CONV_TM = 256
CONV_SUB = 32


def _glu(ab):
    a = ab[:, 0:C].astype(F32)
    b = ab[:, C:2 * C].astype(F32)
    return a * _sigmoid(b)


SUBLANES = 8


def _shifted_copies(ref):
    rows = ref.shape[1] - SUBLANES
    for r in range(1, SUBLANES):
        ref[r, 0:rows, :] = ref[0, r:r + rows, :]


def _shifted_rows(ref, start, size):
    r = start % SUBLANES
    return ref[r, start - r:start - r + size, :]


def _conv_fwd(glu, conv_w, conv_b, ln_g, ln_b, dep):
    tm = CONV_TM

    def body(cur_ref, prev_ref, w_ref, cb_ref, g_ref, b_ref, dep_ref, u_ref, c_ref, zs_ref):
        i = pl.program_id(0)
        zprev = _glu(prev_ref[tm - HALO:tm, :])
        zs_ref[0, 0:HALO, :] = jnp.where(i > 0, zprev, 0.0)
        zs_ref[0, HALO:HALO + tm, :] = _glu(cur_ref[...])
        _shifted_copies(zs_ref)
        for s in range(tm // CONV_SUB):
            base = HALO + s * CONV_SUB - (KW - 1)
            acc = jnp.broadcast_to(cb_ref[...], (CONV_SUB, C))
            for j in range(KW):
                acc = acc + w_ref[j:j + 1, :] * _shifted_rows(zs_ref, base + j, CONV_SUB)
            rows = slice(s * CONV_SUB, (s + 1) * CONV_SUB)
            u_ref[rows, :] = acc
            mu = jnp.mean(acc, axis=-1, keepdims=True)
            xc = acc - mu
            var = jnp.mean(xc * xc, axis=-1, keepdims=True)
            y = xc * lax.rsqrt(var + EPS) * g_ref[...] + b_ref[...]
            c_ref[rows, :] = (y * _sigmoid(y)).astype(BF16)

    return pl.pallas_call(
        body, name="conv_fwd", grid=(T // tm,),
        in_specs=[_rows(tm, 2 * C),
                  pl.BlockSpec((tm, 2 * C), lambda i: (jnp.maximum(i - 1, 0), 0)),
                  _whole((KW, C)), _whole((1, C)), _whole((1, C)), _whole((1, C)), _whole((8, 128))],
        out_specs=[_rows(tm, C), _rows(tm, C)],
        out_shape=[jax.ShapeDtypeStruct((T, C), F32), jax.ShapeDtypeStruct((T, C), BF16)],
        scratch_shapes=[pltpu.VMEM((SUBLANES, HALO + tm, C), F32)],
        compiler_params=_params("parallel"),
    )(glu, glu, conv_w, conv_b, ln_g, ln_b, dep)


def _mix_out(x, o, cact, gl, w_ap, w_cp, b_cp, w_out, dep):
    tm = ROW_TM

    def body(x_ref, o_ref, c_ref, gl_ref, wap_ref, wcp_ref, bcp_ref, wo_ref, dep_ref,
             ya_ref, yc_ref, mg_ref, x1_ref):
        ov, cv = o_ref[...], c_ref[...]
        ya = jnp.concatenate([_dot(ov, wap_ref[s]) for s in range(N_CHIPS)], axis=1)
        yc = jnp.concatenate([_dot(cv, wcp_ref[s]) for s in range(N_CHIPS)], axis=1) + bcp_ref[...]
        g0 = _sigmoid(gl_ref[:, 0:D].astype(F32))
        g1 = _sigmoid(gl_ref[:, D:2 * D].astype(F32))
        mg = (g0 * ya + g1 * yc).astype(BF16)
        ya_ref[...] = ya.astype(BF16)
        yc_ref[...] = yc.astype(BF16)
        mg_ref[...] = mg
        x1_ref[...] = x_ref[...] + _dot(mg, wo_ref[...])

    return pl.pallas_call(
        body, name="mix_out", grid=(T // tm,),
        in_specs=[_rows(tm, D), _rows(tm, AW), _rows(tm, C), _rows(tm, 2 * D),
                  _whole((N_CHIPS, AW, D // N_CHIPS)), _whole((N_CHIPS, C, D // N_CHIPS)), _whole((1, D)),
                  _whole((D, D)), _whole((8, 128))],
        out_specs=[_rows(tm, D), _rows(tm, D), _rows(tm, D), _rows(tm, D)],
        out_shape=[jax.ShapeDtypeStruct((T, D), BF16), jax.ShapeDtypeStruct((T, D), BF16),
                   jax.ShapeDtypeStruct((T, D), BF16), jax.ShapeDtypeStruct((T, D), F32)],
        compiler_params=_params("parallel"),
    )(x, o, cact, gl, w_ap, w_cp, b_cp, w_out, dep)


def _ffn_in(x1, g_ffn, w_fi, dep):
    tm = ROW_TM

    def body(x_ref, g_ref, w_ref, dep_ref, h_ref, gu_ref, act_ref):
        xv = x_ref[...]
        r = lax.rsqrt(jnp.mean(xv * xv, axis=-1, keepdims=True) + EPS)
        h = (xv * r * g_ref[...]).astype(BF16)
        h_ref[...] = h
        for s in range(N_CHIPS // 2):
            c0 = s * FSH
            gate = _dot(h, w_ref[s])
            up = _dot(h, w_ref[s + N_CHIPS // 2])
            gu_ref[:, c0:c0 + FSH] = gate.astype(BF16)
            gu_ref[:, DFF + c0:DFF + c0 + FSH] = up.astype(BF16)
            act_ref[:, c0:c0 + FSH] = (gate * _sigmoid(gate) * up).astype(BF16)

    return pl.pallas_call(
        body, name="ffn_in", grid=(T // tm,),
        in_specs=[_rows(tm, D), _whole((1, D)), _const((N_CHIPS, D, FSH)), _whole((8, 128))],
        out_specs=[_rows(tm, D), _rows(tm, 2 * DFF), _rows(tm, DFF)],
        out_shape=[jax.ShapeDtypeStruct((T, D), BF16), jax.ShapeDtypeStruct((T, 2 * DFF), BF16),
                   jax.ShapeDtypeStruct((T, DFF), BF16)],
        compiler_params=_params("parallel"),
    )(x1, g_ffn, w_fi, dep)


def _ffn_out_loss(x1, act, w_dn, g_final, target):
    tm = ROW_TM

    def body(x_ref, a_ref, w_ref, g_ref, t_ref, dx_ref, dxb_ref, dg_ref, loss_ref):
        i = pl.program_id(0)
        x2 = x_ref[...] + _dot(a_ref[...], w_ref[...])
        r = lax.rsqrt(jnp.mean(x2 * x2, axis=-1, keepdims=True) + EPS)
        xh = x2 * r
        g = g_ref[...]
        err = xh * g - t_ref[...]
        dy = err * (1.0 / D)
        dyg = dy * g
        dx = r * (dyg - xh * jnp.mean(dyg * xh, axis=-1, keepdims=True))
        dx_ref[...] = dx
        dxb_ref[...] = dx.astype(BF16)
        part = 0.5 * jnp.sum(jnp.mean(err * err, axis=-1, keepdims=True), axis=0, keepdims=True)

        @pl.when(i == 0)
        def _():
            dg_ref[...] = jnp.zeros_like(dg_ref)
            loss_ref[...] = jnp.zeros_like(loss_ref)

        dg_ref[...] += jnp.sum(dy * xh, axis=0, keepdims=True)
        loss_ref[...] += jnp.broadcast_to(part, loss_ref.shape)

    return pl.pallas_call(
        body, name="ffn_out_loss", grid=(T // tm,),
        in_specs=[_rows(tm, D), _rows(tm, DFF), _whole((DFF, D)), _whole((1, D)), _rows(tm, D)],
        out_specs=[_rows(tm, D), _rows(tm, D), _whole((1, D)), _whole((1, 128))],
        out_shape=[jax.ShapeDtypeStruct((T, D), F32), jax.ShapeDtypeStruct((T, D), BF16),
                   jax.ShapeDtypeStruct((1, D), F32), jax.ShapeDtypeStruct((1, 128), F32)],
        compiler_params=_params("arbitrary"),
    )(x1, act, w_dn, g_final, target)


def _const(shape):
    return pl.BlockSpec(shape, lambda i: tuple(0 for _ in shape), pipeline_mode=pl.Buffered(1))


def _ffn_bwd(dx2, dx2b, gu, x1, g_ffn, w_dn_t, w_fi_t):
    tm = ROW_TM // 2

    def body(dx_ref, dxb_ref, gu_ref, x_ref, g_ref, wdn_ref, wfi_ref,
             dgu_ref, dx1_ref, dx1b_ref, dg_ref):
        i = pl.program_id(0)
        dxb = dxb_ref[...]
        dh = jnp.zeros((tm, D), F32)
        for k in range(N_CHIPS // 2):
            c0 = k * FSH
            dact = _dot_nt(dxb, wdn_ref[c0:c0 + FSH, :])
            gate = gu_ref[:, c0:c0 + FSH].astype(F32)
            up = gu_ref[:, DFF + c0:DFF + c0 + FSH].astype(F32)
            s = _sigmoid(gate)
            dup = (dact * gate * s).astype(BF16)
            dgate = (dact * up * s * (1.0 + gate * (1.0 - s))).astype(BF16)
            dgu_ref[:, c0:c0 + FSH] = dgate
            dgu_ref[:, DFF + c0:DFF + c0 + FSH] = dup
            dh = dh + _dot_nt(dgate, wfi_ref[k]) + _dot_nt(dup, wfi_ref[k + N_CHIPS // 2])
        xv = x_ref[...]
        r = lax.rsqrt(jnp.mean(xv * xv, axis=-1, keepdims=True) + EPS)
        xh = xv * r
        dhg = dh * g_ref[...]
        dx1 = dx_ref[...] + r * (dhg - xh * jnp.mean(dhg * xh, axis=-1, keepdims=True))
        dx1_ref[...] = dx1
        dx1b_ref[...] = dx1.astype(BF16)

        @pl.when(i == 0)
        def _():
            dg_ref[...] = jnp.zeros_like(dg_ref)

        dg_ref[...] += jnp.sum(dh * xh, axis=0, keepdims=True)

    return pl.pallas_call(
        body, name="ffn_bwd", grid=(T // tm,),
        in_specs=[_rows(tm, D), _rows(tm, D), _rows(tm, 2 * DFF), _rows(tm, D), _whole((1, D)),
                  _const((DFF, D)), _const((N_CHIPS, D, FSH))],
        out_specs=[_rows(tm, 2 * DFF), _rows(tm, D), _rows(tm, D), _whole((1, D))],
        out_shape=[jax.ShapeDtypeStruct((T, 2 * DFF), BF16), jax.ShapeDtypeStruct((T, D), F32),
                   jax.ShapeDtypeStruct((T, D), BF16), jax.ShapeDtypeStruct((1, D), F32)],
        compiler_params=_params("arbitrary"),
    )(dx2, dx2b, gu, x1, g_ffn, w_dn_t, w_fi_t)


def _mix_bwd(dx1b, gl, ya, yc, w_out_t, w_ap_t, w_cp_t, dep):
    tm = ROW_TM

    def body(dx_ref, gl_ref, ya_ref, yc_ref, wo_ref, wap_ref, wcp_ref, dep_ref,
             dya_ref, dyc_ref, dgl_ref, do_ref, dc_ref, db_ref):
        i = pl.program_id(0)
        dm = _dot_nt(dx_ref[...], wo_ref[...])
        g0 = _sigmoid(gl_ref[:, 0:D].astype(F32))
        g1 = _sigmoid(gl_ref[:, D:2 * D].astype(F32))
        dya = dm * g0
        dyc = dm * g1
        dgl_ref[:, 0:D] = (dya * ya_ref[...].astype(F32) * (1.0 - g0)).astype(BF16)
        dgl_ref[:, D:2 * D] = (dyc * yc_ref[...].astype(F32) * (1.0 - g1)).astype(BF16)
        dyab = dya.astype(BF16)
        dycb = dyc.astype(BF16)
        dya_ref[...] = dyab
        dyc_ref[...] = dycb
        sw = D // N_CHIPS
        do = jnp.zeros((tm, AW), F32)
        dcv = jnp.zeros((tm, C), F32)
        for s in range(N_CHIPS):
            do = do + _dot_nt(dyab[:, s * sw:(s + 1) * sw], wap_ref[s])
            dcv = dcv + _dot_nt(dycb[:, s * sw:(s + 1) * sw], wcp_ref[s])
        do_ref[...] = do.astype(BF16)
        dc_ref[...] = dcv

        @pl.when(i == 0)
        def _():
            db_ref[...] = jnp.zeros_like(db_ref)

        db_ref[...] += jnp.sum(dyc, axis=0, keepdims=True)

    return pl.pallas_call(
        body, name="mix_bwd", grid=(T // tm,),
        in_specs=[_rows(tm, D), _rows(tm, 2 * D), _rows(tm, D), _rows(tm, D),
                  _whole((D, D)), _whole((N_CHIPS, AW, D // N_CHIPS)), _whole((N_CHIPS, C, D // N_CHIPS)),
                  _whole((8, 128))],
        out_specs=[_rows(tm, D), _rows(tm, D), _rows(tm, 2 * D), _rows(tm, AW), _rows(tm, C),
                   _whole((1, D))],
        out_shape=[jax.ShapeDtypeStruct((T, D), BF16), jax.ShapeDtypeStruct((T, D), BF16),
                   jax.ShapeDtypeStruct((T, 2 * D), BF16), jax.ShapeDtypeStruct((T, AW), BF16),
                   jax.ShapeDtypeStruct((T, C), F32), jax.ShapeDtypeStruct((1, D), F32)],
        compiler_params=_params("arbitrary"),
    )(dx1b, gl, ya, yc, w_out_t, w_ap_t, w_cp_t, dep)


def _conv_bwd(glu, u, dc, conv_w, ln_g, ln_b, dep):
    tm = CONV_TM
    nblk = T // tm

    def du_of(uv, dcv, g_ref, b_ref):
        mu = jnp.mean(uv, axis=-1, keepdims=True)
        xc = uv - mu
        var = jnp.mean(xc * xc, axis=-1, keepdims=True)
        rstd = lax.rsqrt(var + EPS)
        xh = xc * rstd
        y = xh * g_ref[...] + b_ref[...]
        sg = _sigmoid(y)
        dy = dcv * (sg * (1.0 + y * (1.0 - sg)))
        dxh = dy * g_ref[...]
        du = rstd * (dxh - jnp.mean(dxh, axis=-1, keepdims=True)
                     - xh * jnp.mean(dxh * xh, axis=-1, keepdims=True))
        return du, dy, xh

    def body(cur_ref, prev_ref, u_ref, un_ref, dc_ref, dcn_ref, w_ref, g_ref, b_ref, dep_ref,
             dglu_ref, dw_ref, dcb_ref, dg_ref, db_ref, zs_ref, dus_ref):
        i = pl.program_id(0)

        @pl.when(i == 0)
        def _():
            dw_ref[...] = jnp.zeros_like(dw_ref)
            dcb_ref[...] = jnp.zeros_like(dcb_ref)
            dg_ref[...] = jnp.zeros_like(dg_ref)
            db_ref[...] = jnp.zeros_like(db_ref)

        zprev = _glu(prev_ref[tm - HALO:tm, :])
        zs_ref[0, 0:HALO, :] = jnp.where(i > 0, zprev, 0.0)
        zs_ref[0, HALO:HALO + tm, :] = _glu(cur_ref[...])
        _shifted_copies(zs_ref)

        dun, _, _ = du_of(un_ref[0:HALO, :], dcn_ref[0:HALO, :], g_ref, b_ref)
        dus_ref[0, tm:tm + HALO, :] = jnp.where(i < nblk - 1, dun, 0.0)
        dg_acc = jnp.zeros((1, C), F32)
        db_acc = jnp.zeros((1, C), F32)
        dcb_acc = jnp.zeros((1, C), F32)
        for s in range(tm // CONV_SUB):
            rows = slice(s * CONV_SUB, (s + 1) * CONV_SUB)
            du, dy, xh = du_of(u_ref[rows, :], dc_ref[rows, :], g_ref, b_ref)
            dus_ref[0, rows, :] = du
            dg_acc = dg_acc + jnp.sum(dy * xh, axis=0, keepdims=True)
            db_acc = db_acc + jnp.sum(dy, axis=0, keepdims=True)
            dcb_acc = dcb_acc + jnp.sum(du, axis=0, keepdims=True)
        dg_ref[...] += dg_acc
        db_ref[...] += db_acc
        dcb_ref[...] += dcb_acc
        _shifted_copies(dus_ref)

        for j in range(KW):
            acc = jnp.zeros((CONV_SUB, C), F32)
            for s in range(tm // CONV_SUB):
                base = HALO + s * CONV_SUB - (KW - 1) + j
                acc = acc + dus_ref[0, s * CONV_SUB:(s + 1) * CONV_SUB, :] * _shifted_rows(zs_ref, base, CONV_SUB)
            dw_ref[j:j + 1, :] += jnp.sum(acc, axis=0, keepdims=True)

        for s in range(tm // CONV_SUB):
            rows = slice(s * CONV_SUB, (s + 1) * CONV_SUB)
            dz = jnp.zeros((CONV_SUB, C), F32)
            for j in range(KW):
                dz = dz + w_ref[j:j + 1, :] * _shifted_rows(dus_ref, s * CONV_SUB + (KW - 1) - j, CONV_SUB)
            a = cur_ref[rows, 0:C].astype(F32)
            sb = _sigmoid(cur_ref[rows, C:2 * C].astype(F32))
            dglu_ref[rows, 0:C] = (dz * sb).astype(BF16)
            dglu_ref[rows, C:2 * C] = (dz * a * sb * (1.0 - sb)).astype(BF16)

    nxt = lambda i: (jnp.minimum(i + 1, nblk - 1), 0)
    return pl.pallas_call(
        body, name="conv_bwd", grid=(nblk,),
        in_specs=[_rows(tm, 2 * C),
                  pl.BlockSpec((tm, 2 * C), lambda i: (jnp.maximum(i - 1, 0), 0)),
                  _rows(tm, C), pl.BlockSpec((tm, C), nxt),
                  _rows(tm, C), pl.BlockSpec((tm, C), nxt),
                  _whole((KW, C)), _whole((1, C)), _whole((1, C)), _whole((8, 128))],
        out_specs=[_rows(tm, 2 * C), _whole((KW, C)), _whole((1, C)), _whole((1, C)), _whole((1, C))],
        out_shape=[jax.ShapeDtypeStruct((T, 2 * C), BF16), jax.ShapeDtypeStruct((KW, C), F32),
                   jax.ShapeDtypeStruct((1, C), F32), jax.ShapeDtypeStruct((1, C), F32),
                   jax.ShapeDtypeStruct((1, C), F32)],
        scratch_shapes=[pltpu.VMEM((SUBLANES, HALO + tm, C), F32), pltpu.VMEM((SUBLANES, tm + HALO, C), F32)],
        compiler_params=_params("arbitrary"),
    )(glu, glu, u, u, dc, dc, conv_w, ln_g, ln_b, dep)


def _attn_bwd(qkv, o, do, lse, sinks):
    def body(sink_ref, qkv_ref, o_ref, do_ref, lse_ref, dq_ref, dkv_ref, ds_ref,
             s_ref, dp_ref, p_ref, dsb_ref):
        i = pl.program_id(0)

        @pl.when(i == 0)
        def _():
            dkv_ref[...] = jnp.zeros_like(dkv_ref)
            ds_ref[...] = jnp.zeros_like(ds_ref)

        r0 = pl.multiple_of(i * BLK, BLK)
        rb, valid = _band(i)
        for g in range(NKV):
            kband = qkv_ref[pl.ds(rb, BAND), AW + g * HD:AW + (g + 1) * HD]
            vband = qkv_ref[pl.ds(rb, BAND), AW + KVW + g * HD:AW + KVW + (g + 1) * HD]
            lse_parts, dl_parts = [], []
            for hh in range(GROUP):
                h = g * GROUP + hh
                hcol = slice(h * HD, (h + 1) * HD)
                doh = do_ref[:, hcol]
                s_ref[hh * BLK:(hh + 1) * BLK, :] = _dot_nt(qkv_ref[pl.ds(r0, BLK), hcol], kband)
                dp_ref[hh * BLK:(hh + 1) * BLK, :] = _dot_nt(doh, vband)
                lse_parts.append(lse_ref[:, h:h + 1])
                dl_parts.append(jnp.sum(doh.astype(F32) * o_ref[:, hcol].astype(F32), axis=-1, keepdims=True))
            lse = jnp.concatenate(lse_parts, axis=0)
            dl = jnp.concatenate(dl_parts, axis=0)
            p = jnp.where(valid, jnp.exp(s_ref[...] * SCALE - lse), 0.0)
            p_ref[...] = p.astype(BF16)
            dsb_ref[...] = (p * (dp_ref[...] - dl)).astype(BF16)
            dsink = -(jnp.exp(_sink_column(sink_ref, g) - lse) * dl)
            dk = jnp.zeros((BAND, HD), F32)
            dv = jnp.zeros((BAND, HD), F32)
            for hh in range(GROUP):
                h = g * GROUP + hh
                hcol = slice(h * HD, (h + 1) * HD)
                rows = slice(hh * BLK, (hh + 1) * BLK)
                dq_ref[:, hcol] = (_dot(dsb_ref[rows, :], kband) * SCALE).astype(BF16)
                dk = dk + _dot_tn(dsb_ref[rows, :], qkv_ref[pl.ds(r0, BLK), hcol])
                dv = dv + _dot_tn(p_ref[rows, :], do_ref[:, hcol])
                ds_ref[h:h + 1, :] += jnp.broadcast_to(jnp.sum(dsink[rows], axis=0, keepdims=True), (1, 128))
            dkv_ref[pl.ds(rb, BAND), g * HD:(g + 1) * HD] += dk * SCALE
            dkv_ref[pl.ds(rb, BAND), KVW + g * HD:KVW + (g + 1) * HD] += dv

    return pl.pallas_call(
        body, name="attn_bwd", grid=(T // BLK,),
        in_specs=[pl.BlockSpec(memory_space=pltpu.SMEM), _whole((T, QKVW)),
                  _rows(BLK, AW), _rows(BLK, AW), _rows(BLK, NQ)],
        out_specs=[_rows(BLK, AW), _whole((T, 2 * KVW)), _whole((NQ, 128))],
        out_shape=[jax.ShapeDtypeStruct((T, AW), BF16), jax.ShapeDtypeStruct((T, 2 * KVW), F32),
                   jax.ShapeDtypeStruct((NQ, 128), F32)],
        scratch_shapes=[pltpu.VMEM((GROWS, BAND), F32), pltpu.VMEM((GROWS, BAND), F32),
                        pltpu.VMEM((GROWS, BAND), BF16), pltpu.VMEM((GROWS, BAND), BF16)],
        compiler_params=_params("arbitrary"),
    )(sinks, qkv, o, do, lse)


PROJ_PARTS = [(0, AW), (AW, QKVW), (GLU_OFF, GATE_OFF), (GATE_OFF, INW)]


def _in_proj_bwd(dq, dkv, dglu, dgl, x, dx1, g_mix, w_in_t):
    tm = ROW_TM

    def body(dq_ref, dkv_ref, dglu_ref, dgl_ref, x_ref, dx1_ref, g_ref, w_ref, gx_ref, dg_ref, db_ref):
        i = pl.program_id(0)

        @pl.when(i == 0)
        def _():
            dg_ref[...] = jnp.zeros_like(dg_ref)
            db_ref[...] = jnp.zeros_like(db_ref)

        dh = jnp.zeros((tm, D), F32)
        for part_ref, (lo, hi) in zip((dq_ref, dkv_ref, dglu_ref, dgl_ref), PROJ_PARTS):
            part = part_ref[...]
            dh = dh + _dot(part.astype(BF16), w_ref[lo:hi, :])
            db_ref[:, lo:hi] += jnp.sum(part.astype(F32), axis=0, keepdims=True)
        xv = x_ref[...]
        r = lax.rsqrt(jnp.mean(xv * xv, axis=-1, keepdims=True) + EPS)
        xh = xv * r
        dhg = dh * g_ref[...]
        gx_ref[...] = dx1_ref[...] + r * (dhg - xh * jnp.mean(dhg * xh, axis=-1, keepdims=True))
        dg_ref[...] += jnp.sum(dh * xh, axis=0, keepdims=True)

    return pl.pallas_call(
        body, name="in_proj_bwd", grid=(T // tm,),
        in_specs=[_rows(tm, AW), _rows(tm, 2 * KVW), _rows(tm, 2 * C), _rows(tm, 2 * D),
                  _rows(tm, D), _rows(tm, D), _whole((1, D)), _const((INW, D))],
        out_specs=[_rows(tm, D), _whole((1, D)), _whole((1, INW))],
        out_shape=[jax.ShapeDtypeStruct((T, D), F32), jax.ShapeDtypeStruct((1, D), F32),
                   jax.ShapeDtypeStruct((1, INW), F32)],
        compiler_params=_params("arbitrary"),
    )(dq, dkv, dglu, dgl, x, dx1, g_mix, w_in_t)


def _grad_w_in_t(h, dq, dkv, dglu, dgl):
    tn, chunk = 512, 256

    def body(h_ref, dq_ref, dkv_ref, dglu_ref, dgl_ref, o_ref, pt_ref):
        @pl.when(pl.program_id(0) == 0)
        def _():
            for part_ref, (lo, hi) in zip((dq_ref, dkv_ref, dglu_ref, dgl_ref), PROJ_PARTS):
                for c0 in range(0, hi - lo, chunk):
                    pt_ref[lo + c0:lo + c0 + chunk, :] = part_ref[:, c0:c0 + chunk].astype(BF16).T

        hv = h_ref[...]
        for r0 in range(0, INW, QKVW):
            o_ref[r0:r0 + QKVW, :] = _dot(pt_ref[r0:r0 + QKVW, :], hv).astype(BF16)

    return pl.pallas_call(
        body, name="grad_w_in", grid=(D // tn,),
        in_specs=[pl.BlockSpec((T, tn), lambda j: (0, j)), _const((T, AW)), _const((T, 2 * KVW)),
                  _const((T, 2 * C)), _const((T, 2 * D))],
        out_specs=pl.BlockSpec((INW, tn), lambda j: (0, j)),
        out_shape=jax.ShapeDtypeStruct((INW, D), BF16),
        scratch_shapes=[pltpu.VMEM((INW, T), BF16)],
        compiler_params=_params("arbitrary"),
    )(h, dq, dkv, dglu, dgl)


def _grad_w(a, b, name, tk, tn, col_sharded):
    k, n = a.shape[1], b.shape[1]

    def body(a_ref, b_ref, o_ref, at_ref):
        @pl.when(pl.program_id(1) == 0)
        def _():
            at_ref[...] = a_ref[...].T

        o_ref[...] = _dot(at_ref[...], b_ref[...]).astype(BF16)

    if col_sharded:
        per = n // N_CHIPS // tn
        shape = (N_CHIPS, k, n // N_CHIPS)
        out_map = lambda i, j: (j // per, i, j % per)
    else:
        shape = (1, k, n)
        out_map = lambda i, j: (0, i, j)
    out = pl.pallas_call(
        body, name=name, grid=(k // tk, n // tn),
        in_specs=[pl.BlockSpec((T, tk), lambda i, j: (0, i)), pl.BlockSpec((T, tn), lambda i, j: (0, j))],
        out_specs=pl.BlockSpec((None, tk, tn), out_map),
        out_shape=jax.ShapeDtypeStruct(shape, BF16),
        scratch_shapes=[pltpu.VMEM((tk, T), BF16)],
        compiler_params=_params("parallel", "arbitrary"),
    )(a, b)
    return out if col_sharded else out.reshape(N_CHIPS, k // N_CHIPS, n)


HBM_SPEC = pl.BlockSpec(memory_space=pltpu.HBM)


def _place():
    x, y, c = lax.axis_index("x"), lax.axis_index("y"), lax.axis_index("c")
    chips = [(1 - x, y), (x, 1 - y), (1 - x, 1 - y)]
    return x, y, c, chips


SEM_SPEC = pl.BlockSpec(memory_space=pltpu.SEMAPHORE)
ANY_SPEC = pl.BlockSpec(memory_space=pl.ANY)
VMEM_SPEC = pl.BlockSpec(memory_space=pltpu.VMEM)
EFFECT = pltpu.SideEffectType.DATAFLOW_SIDE_EFFECTING


def _gather_ends(src, land, x, y, c, chips):
    kh = src.shape[0] // 2
    s_me = 2 * x + y
    ends = [(src.at[pl.ds(c * kh, kh)], land.at[s_me, pl.ds(c * kh, kh)], (*chip, c)) for chip in chips]
    return ends + [(src, land.at[s_me], (x, y, 1 - c))]


def _reduce_ends(src, land, x, y, c, chips):
    return [(src.at[2 * chip[0] + chip[1]], land.at[j], (*chip, c)) for j, chip in enumerate(chips)]


def _chip_copies(ends, srcs, lands, send_sems, recv_sems, first=0):
    x, y, c, chips = _place()
    copies = []
    for src, land in zip(srcs, lands):
        peers = ends(src, land, x, y, c, chips)
        for s, d, to in peers:
            k = first * len(peers) + len(copies)
            copies.append(pltpu.make_async_remote_copy(
                src_ref=s, dst_ref=d, send_sem=send_sems.at[k], recv_sem=recv_sems.at[k],
                device_id=to, device_id_type=MESH))
    return copies


GATHER_PEERS, REDUCE_PEERS = 4, 3


def _chip_start(name, ends, peers, srcs, lands):
    n = len(srcs)

    def body(*refs):
        copies = _chip_copies(ends, refs[:n], refs[n:2 * n], refs[2 * n], refs[2 * n + 1])
        for cp in copies:
            cp.start()
        token = refs[-1]
        token[...] = jnp.zeros_like(token)

    hbm = lambda a: pltpu.HBM(a.shape, a.dtype)
    res = pl.pallas_call(
        body, name=name,
        out_shape=(pltpu.SemaphoreType.DMA((peers * n,)), pltpu.SemaphoreType.DMA((peers * n,)),
                   *[hbm(a) for a in srcs], *[hbm(a) for a in lands],
                   jax.ShapeDtypeStruct((8, 128), F32)),
        in_specs=[HBM_SPEC] * (2 * n),
        out_specs=(SEM_SPEC, SEM_SPEC, *[HBM_SPEC] * (2 * n), VMEM_SPEC),
        input_output_aliases={i: 2 + i for i in range(2 * n)},
        compiler_params=pltpu.CompilerParams(has_side_effects=EFFECT),
    )(*[pltpu.with_memory_space_constraint(a, pltpu.HBM) for a in (*srcs, *lands)])
    return res[0], res[1], list(res[2:2 + n]), list(res[2 + n:2 + 2 * n]), res[-1]


def _chip_wait(name, ends, send_sems, recv_sems, srcs, lands, after, first=0):
    n, na = len(srcs), len(after)

    def body(*refs):
        copies = _chip_copies(ends, refs[:n], refs[n:2 * n], refs[2 * n], refs[2 * n + 1], first)
        for cp in copies:
            cp.wait_send()
            cp.wait_recv()

    hbm = lambda a: pltpu.HBM(a.shape, a.dtype)
    res = pl.pallas_call(
        body, name=name,
        out_shape=tuple(hbm(a) for a in (*srcs, *lands)),
        in_specs=[HBM_SPEC] * (2 * n) + [SEM_SPEC, SEM_SPEC] + [ANY_SPEC] * na,
        out_specs=tuple([HBM_SPEC] * (2 * n)),
        input_output_aliases={i: i for i in range(2 * n)},
        compiler_params=pltpu.CompilerParams(has_side_effects=EFFECT),
    )(*srcs, *lands, send_sems, recv_sems, *after)
    return list(res[:n]), list(res[n:])


def _forward_copies(lands, send_sems, recv_sems):
    x, y, c, chips = _place()
    copies = []
    for land in lands:
        kh = land.shape[1] // 2
        for chip in chips:
            blk = land.at[2 * chip[0] + chip[1], pl.ds(c * kh, kh)]
            k = len(copies)
            copies.append(pltpu.make_async_remote_copy(
                src_ref=blk, dst_ref=blk, send_sem=send_sems.at[k], recv_sem=recv_sems.at[k],
                device_id=(x, y, 1 - c), device_id_type=MESH))
    return copies


def _gather_relay(name, send_sems, recv_sems, srcs, lands, after, first):
    n, na = len(srcs), len(after)

    def body(*refs):
        land_refs = refs[n:2 * n]
        for cp in _chip_copies(_gather_ends, refs[:n], land_refs, refs[2 * n], refs[2 * n + 1], first):
            cp.wait_send()
            cp.wait_recv()
        out = refs[2 * n + 2 + na:]
        for cp in _forward_copies(land_refs, out[0], out[1]):
            cp.start()
        out[-1][...] = jnp.zeros_like(out[-1])

    hbm = lambda a: pltpu.HBM(a.shape, a.dtype)
    res = pl.pallas_call(
        body, name=name,
        out_shape=(pltpu.SemaphoreType.DMA((3 * n,)), pltpu.SemaphoreType.DMA((3 * n,)),
                   *[hbm(a) for a in lands], jax.ShapeDtypeStruct((8, 128), F32)),
        in_specs=[HBM_SPEC] * (2 * n) + [SEM_SPEC, SEM_SPEC] + [ANY_SPEC] * na,
        out_specs=(SEM_SPEC, SEM_SPEC, *[HBM_SPEC] * n, VMEM_SPEC),
        input_output_aliases={n + i: 2 + i for i in range(n)},
        compiler_params=pltpu.CompilerParams(has_side_effects=EFFECT),
    )(*srcs, *lands, send_sems, recv_sems, *after)
    return res[0], res[1], list(res[2:2 + n]), res[-1]


def _forward_wait(name, send_sems, recv_sems, lands, after):
    n, na = len(lands), len(after)

    def body(*refs):
        for cp in _forward_copies(refs[:n], refs[n], refs[n + 1]):
            cp.wait_send()
            cp.wait_recv()

    hbm = lambda a: pltpu.HBM(a.shape, a.dtype)
    res = pl.pallas_call(
        body, name=name,
        out_shape=tuple(hbm(a) for a in lands),
        in_specs=[HBM_SPEC] * n + [SEM_SPEC, SEM_SPEC] + [ANY_SPEC] * na,
        out_specs=tuple([HBM_SPEC] * n),
        input_output_aliases={i: i for i in range(n)},
        compiler_params=pltpu.CompilerParams(has_side_effects=EFFECT),
    )(*lands, send_sems, recv_sems, *after)
    return list(res)


def _pair_exchange(name, grads):
    n = len(grads)

    def body(*refs):
        ins, outs = refs[:n], refs[n:2 * n]
        send_sems, recv_sems = refs[2 * n:]
        x, y, c, _ = _place()
        copies = []
        for w in range(n):
            kh = ins[w].shape[1] // 2
            cp = pltpu.make_async_remote_copy(
                src_ref=ins[w].at[:, pl.ds((1 - c) * kh, kh)], dst_ref=outs[w],
                send_sem=send_sems.at[w], recv_sem=recv_sems.at[w],
                device_id=(x, y, 1 - c), device_id_type=MESH)
            cp.start()
            copies.append(cp)
        for cp in copies:
            cp.wait()

    return pl.pallas_call(
        body, name=name,
        in_specs=[HBM_SPEC] * n, out_specs=[HBM_SPEC] * n,
        out_shape=[jax.ShapeDtypeStruct((g.shape[0], g.shape[1] // 2, g.shape[2]), g.dtype) for g in grads],
        scratch_shapes=[pltpu.SemaphoreType.DMA((n,)), pltpu.SemaphoreType.DMA((n,))],
    )(*grads)


def _row_tile(k):
    for t in (256, 240, 128, 176, 64, 32, 16):
        if k % t == 0:
            return t
    raise ValueError(k)


def _pair_sum(c_idx, g, got, name):
    _, k, n = g.shape
    kh = k // 2
    tm = _row_tile(kh)
    nb = kh // tm

    def body(c_ref, g_ref, r_ref, o_ref):
        o_ref[...] = (g_ref[...].astype(F32) + r_ref[...].astype(F32)).astype(BF16)

    return pl.pallas_call(
        body, name=name,
        grid_spec=pltpu.PrefetchScalarGridSpec(
            num_scalar_prefetch=1, grid=(N_CHIPS, nb),
            in_specs=[pl.BlockSpec((1, tm, n), lambda s, i, c_ref: (s, c_ref[0] * nb + i, 0)),
                      pl.BlockSpec((1, tm, n), lambda s, i, c_ref: (s, i, 0))],
            out_specs=pl.BlockSpec((1, tm, n), lambda s, i, c_ref: (s, i, 0))),
        out_shape=jax.ShapeDtypeStruct((N_CHIPS, kh, n), BF16),
        compiler_params=_params("parallel", "parallel"),
    )(c_idx, g, got)


def _chip_sum(sc_idx, mine, got, name):
    _, kh, n = mine.shape
    tm = _row_tile(kh)

    def body(sc_ref, m_ref, r_ref, o_ref):
        acc = m_ref[0].astype(F32)
        for j in range(3):
            acc = acc + r_ref[j].astype(F32)
        o_ref[0] = acc

    return pl.pallas_call(
        body, name=name,
        grid_spec=pltpu.PrefetchScalarGridSpec(
            num_scalar_prefetch=1, grid=(kh // tm,),
            in_specs=[pl.BlockSpec((1, tm, n), lambda i, sc_ref: (sc_ref[0], i, 0)),
                      pl.BlockSpec((3, tm, n), lambda i, sc_ref: (0, i, 0))],
            out_specs=pl.BlockSpec((1, tm, n), lambda i, sc_ref: (sc_ref[1], i, 0))),
        out_shape=jax.ShapeDtypeStruct((2, kh, n), F32),
        compiler_params=_params("parallel"),
    )(sc_idx, mine, got)


def _pair_share(name, halves):
    n = len(halves)

    def body(*refs):
        bufs = refs[n:2 * n]
        send_sems, recv_sems = refs[2 * n:]
        x, y, c, _ = _place()
        copies = []
        for w in range(n):
            cp = pltpu.make_async_remote_copy(
                src_ref=bufs[w].at[c], dst_ref=bufs[w].at[c],
                send_sem=send_sems.at[w], recv_sem=recv_sems.at[w],
                device_id=(x, y, 1 - c), device_id_type=MESH)
            cp.start()
            copies.append(cp)
        for w in range(n):
            copies[w].wait_send()
            pltpu.make_async_remote_copy(
                src_ref=bufs[w].at[1 - c], dst_ref=bufs[w].at[1 - c],
                send_sem=send_sems.at[w], recv_sem=recv_sems.at[w],
                device_id=(x, y, c), device_id_type=MESH).wait_recv()

    return pl.pallas_call(
        body, name=name,
        in_specs=[HBM_SPEC] * n, out_specs=[HBM_SPEC] * n,
        out_shape=[jax.ShapeDtypeStruct(h.shape, h.dtype) for h in halves],
        input_output_aliases={w: w for w in range(n)},
        scratch_shapes=[pltpu.SemaphoreType.DMA((n,)), pltpu.SemaphoreType.DMA((n,))],
    )(*halves)


def _gather_small(blocks):
    n = len(blocks)

    def body(*refs):
        ins, outs = refs[:n], refs[n:2 * n]
        send_sems, recv_sems, local_sems = refs[2 * n:]
        x, y, c, chips = _place()
        me, sibling = (x, y, c), (x, y, 1 - c)

        def rows(w, px, py, pc):
            m = ins[w].shape[0]
            return outs[w].at[pl.ds((4 * px + 2 * py + pc) * m, m), :]

        def copy(w, k, block, to, src=None):
            return pltpu.make_async_remote_copy(
                src_ref=rows(w, *block) if src is None else src, dst_ref=rows(w, *block),
                send_sem=send_sems.at[7 * w + k], recv_sem=recv_sems.at[7 * w + k],
                device_id=to, device_id_type=MESH)

        started, owns = [], []
        for w in range(n):
            own = pltpu.make_async_copy(ins[w], rows(w, *me), local_sems.at[w])
            own.start()
            owns.append(own)
            first = [copy(w, 0, me, sibling, src=ins[w])]
            first += [copy(w, 1 + j, me, (*chip, c), src=ins[w]) for j, chip in enumerate(chips)]
            for cp in first:
                cp.start()
            started += first
        for w in range(n):
            for j, chip in enumerate(chips):
                copy(w, 1 + j, (*chip, c), me).wait_recv()
                cp = copy(w, 4 + j, (*chip, c), sibling)
                cp.start()
                started.append(cp)
        for w in range(n):
            copy(w, 0, sibling, me).wait_recv()
            for j, chip in enumerate(chips):
                copy(w, 4 + j, (*chip, 1 - c), me).wait_recv()
        for cp in started:
            cp.wait_send()
        for own in owns:
            own.wait()

    vmem = pl.BlockSpec(memory_space=pltpu.VMEM)
    return pl.pallas_call(
        body, name="gather_small",
        in_specs=[vmem] * n, out_specs=[vmem] * n,
        out_shape=[jax.ShapeDtypeStruct((8 * b.shape[0], b.shape[1]), b.dtype) for b in blocks],
        scratch_shapes=[pltpu.SemaphoreType.DMA((7 * n,)), pltpu.SemaphoreType.DMA((7 * n,)),
                        pltpu.SemaphoreType.DMA((n,))],
    )(*blocks)


def _adamw_math(w, g, m, v):
    m = ADAM_B1 * m + (1.0 - ADAM_B1) * g
    v = ADAM_B2 * v + (1.0 - ADAM_B2) * (g * g)
    m_hat = m / (1.0 - ADAM_B1 ** ADAM_STEP)
    v_hat = v / (1.0 - ADAM_B2 ** ADAM_STEP)
    delta = -ADAM_LR * (m_hat / (jnp.sqrt(v_hat) + ADAM_EPS) + ADAM_WD * w)
    return delta, m, v


def _adamw(w, g, m, v, name):
    k, n = w.shape
    tm = k // 4

    def body(w_ref, g_ref, m_ref, v_ref, go_ref, d_ref, mo_ref, vo_ref):
        g = g_ref[...]
        d, mm, vv = _adamw_math(w_ref[...], g, m_ref[...], v_ref[...])
        go_ref[...] = g
        d_ref[...] = d
        mo_ref[...] = mm
        vo_ref[...] = vv

    spec = pl.BlockSpec((tm, n), lambda i: (i, 0))
    shp = jax.ShapeDtypeStruct((k, n), F32)
    return pl.pallas_call(
        body, name=name, grid=(4,), in_specs=[spec] * 4, out_specs=[spec] * 4,
        out_shape=[shp] * 4, compiler_params=_params("parallel"),
    )(w, g, m, v)


VEC_SLOTS = {
    "g_mix_norm": (0, 0, D), "b_conv_proj": (0, D, D), "g_ffn_norm": (0, 2 * D, D),
    "g_final": (0, 3 * D, D), "b_in": (1, 0, INW), "conv_b": (2, 0, C), "ln_g": (2, C, C),
    "ln_b": (2, 2 * C, C), "sinks": (2, 3 * C, NQ), "loss": (2, 3 * C + 128, 1),
}
VEC_ROWS, VEC_COLS = 8, 4 * D
CW_ROWS = 32
SMALL_NAMES = ["g_mix_norm", "b_in", "sinks", "conv_w", "conv_b", "ln_g", "ln_b",
               "b_conv_proj", "g_ffn_norm", "g_final"]
CW_LANES = C // N_CHIPS


def _pack_small(gs, loss):
    row0 = jnp.concatenate([gs["g_mix_norm"], gs["b_conv_proj"], gs["g_ffn_norm"], gs["g_final"]], axis=1)
    row1 = jnp.pad(gs["b_in"], ((0, 0), (0, VEC_COLS - INW)))
    row2 = jnp.concatenate([gs["conv_b"], gs["ln_g"], gs["ln_b"],
                            jnp.pad(gs["sinks"], ((0, 0), (0, 128 - NQ))),
                            jnp.pad(loss.reshape(1, 1), ((0, 0), (0, VEC_COLS - 3 * C - 129)))], axis=1)
    vec = jnp.concatenate([row0, row1, row2, jnp.zeros((VEC_ROWS - 3, VEC_COLS), F32)], axis=0)
    cw = jnp.pad(gs["conv_w"], ((0, CW_ROWS - KW), (0, 0)))
    return vec, cw


def _small_update(s_idx, vec_all, cw_all, wmv):
    nsm = len(SMALL_NAMES)

    def body(s_ref, vec_ref, cw_ref, *refs):
        ins = refs[:3 * nsm]
        outs = refs[3 * nsm:7 * nsm]
        loss_ref = refs[7 * nsm]

        def total(slot):
            row, lane, width = slot
            acc = vec_ref[row:row + 1, lane:lane + width]
            for k in range(1, 8):
                acc = acc + vec_ref[k * VEC_ROWS + row:k * VEC_ROWS + row + 1, lane:lane + width]
            return acc

        loss_ref[...] = jnp.broadcast_to(total(VEC_SLOTS["loss"]), loss_ref.shape)
        for p, name in enumerate(SMALL_NAMES):
            w_ref, m_ref, v_ref = ins[3 * p:3 * p + 3]
            g_ref, d_ref, mo_ref, vo_ref = outs[4 * p:4 * p + 4]
            if name == "conv_w":
                g = jnp.zeros((KW, CW_LANES), F32)
                for s in range(N_CHIPS):
                    cand = cw_ref[0:KW, s * CW_LANES:(s + 1) * CW_LANES]
                    for k in range(1, 8):
                        cand = cand + cw_ref[k * CW_ROWS:k * CW_ROWS + KW, s * CW_LANES:(s + 1) * CW_LANES]
                    g = jnp.where(s_ref[0] == s, cand, g)
            else:
                g = total(VEC_SLOTS[name])
            d, mm, vv = _adamw_math(w_ref[...], g, m_ref[...], v_ref[...])
            g_ref[...] = g
            d_ref[...] = d
            mo_ref[...] = mm
            vo_ref[...] = vv

    vmem = pl.BlockSpec(memory_space=pltpu.VMEM)
    flat = [a for t in wmv for a in t]
    out_shape = []
    for w, _, _ in wmv:
        out_shape += [jax.ShapeDtypeStruct(w.shape, F32)] * 4
    out_shape.append(jax.ShapeDtypeStruct((1, 128), F32))
    res = pl.pallas_call(
        body, name="small_update",
        in_specs=[pl.BlockSpec(memory_space=pltpu.SMEM)] + [vmem] * (2 + len(flat)),
        out_specs=[vmem] * len(out_shape), out_shape=out_shape,
    )(s_idx, vec_all, cw_all, *flat)
    return [tuple(res[4 * p:4 * p + 4]) for p in range(nsm)], res[4 * nsm]


WEIGHT_ORDER = ["g_mix_norm", "w_in", "b_in", "sinks", "conv_w", "conv_b", "ln_g", "ln_b",
                "w_attn_proj", "w_conv_proj", "b_conv_proj", "w_out", "g_ffn_norm", "w_ffn_in",
                "w_ffn_down", "g_final"]


def kernel(x, g_mix_norm, w_in, b_in, sinks, conv_w, conv_b, ln_g, ln_b, w_attn_proj, w_conv_proj, b_conv_proj, w_out, g_ffn_norm, w_ffn_in, w_ffn_down, g_final, loss_target, m_g_mix_norm, m_w_in, m_b_in, m_sinks, m_conv_w, m_conv_b, m_ln_g, m_ln_b, m_w_attn_proj, m_w_conv_proj, m_b_conv_proj, m_w_out, m_g_ffn_norm, m_w_ffn_in, m_w_ffn_down, m_g_final, v_g_mix_norm, v_w_in, v_b_in, v_sinks, v_conv_w, v_conv_b, v_ln_g, v_ln_b, v_w_attn_proj, v_w_conv_proj, v_b_conv_proj, v_w_out, v_g_ffn_norm, v_w_ffn_in, v_w_ffn_down, v_g_final):
    w = dict(g_mix_norm=g_mix_norm, w_in=w_in, b_in=b_in, sinks=sinks, conv_w=conv_w, conv_b=conv_b,
             ln_g=ln_g, ln_b=ln_b, w_attn_proj=w_attn_proj, w_conv_proj=w_conv_proj,
             b_conv_proj=b_conv_proj, w_out=w_out, g_ffn_norm=g_ffn_norm, w_ffn_in=w_ffn_in,
             w_ffn_down=w_ffn_down, g_final=g_final)
    m = dict(g_mix_norm=m_g_mix_norm, w_in=m_w_in, b_in=m_b_in, sinks=m_sinks, conv_w=m_conv_w,
             conv_b=m_conv_b, ln_g=m_ln_g, ln_b=m_ln_b, w_attn_proj=m_w_attn_proj,
             w_conv_proj=m_w_conv_proj, b_conv_proj=m_b_conv_proj, w_out=m_w_out,
             g_ffn_norm=m_g_ffn_norm, w_ffn_in=m_w_ffn_in, w_ffn_down=m_w_ffn_down, g_final=m_g_final)
    v = dict(g_mix_norm=v_g_mix_norm, w_in=v_w_in, b_in=v_b_in, sinks=v_sinks, conv_w=v_conv_w,
             conv_b=v_conv_b, ln_g=v_ln_g, ln_b=v_ln_b, w_attn_proj=v_w_attn_proj,
             w_conv_proj=v_w_conv_proj, b_conv_proj=v_b_conv_proj, w_out=v_w_out,
             g_ffn_norm=v_g_ffn_norm, w_ffn_in=v_w_ffn_in, w_ffn_down=v_w_ffn_down, g_final=v_g_final)

    c_idx = lax.axis_index("c").astype(jnp.int32).reshape(1)
    s_idx = (2 * lax.axis_index("x") + lax.axis_index("y")).astype(jnp.int32).reshape(1)

    sc_idx = jnp.concatenate([s_idx, c_idx])
    out_g, out_d, out_m, out_v = {}, {}, {}, {}

    def gather_start(tag, shards):
        lands = [lax.empty((N_CHIPS,) + s.shape, s.dtype) for s in shards]
        return _chip_start("gather_start_" + tag, _gather_ends, GATHER_PEERS, shards, lands)

    def gather_relay(tag, state, after, first=0, count=None):
        send_sems, recv_sems, shards, lands, _ = state
        last = len(shards) if count is None else first + count
        return _gather_relay("gather_relay_" + tag, send_sems, recv_sems, shards[first:last],
                             lands[first:last], after, first)

    def gather_finish(tag, relay, after):
        return _forward_wait("forward_wait_" + tag, relay[0], relay[1], relay[2], after)

    names_b = ["w_attn_proj", "w_conv_proj", "w_out", "w_ffn_in", "w_ffn_down"]
    big = {name: (w[name][0], m[name][0], v[name][0]) for name in names_b}
    big["w_in"] = (w_in[0].T, m_w_in[0].T, v_w_in[0].T)
    state_a = gather_start("a", [big["w_in"][0].astype(BF16), jnp.pad(conv_w[0], ((0, CW_ROWS - KW), (0, 0)))])
    state_b = gather_start("b", [(big[name][0] + state_a[4][0, 0]).astype(BF16) for name in names_b])
    got_a = gather_finish("a", gather_relay("a", state_a, [state_b[4]]), [])
    w_in_t_full = got_a[0].reshape(INW, D)
    conv_w_full = got_a[1].transpose(1, 0, 2).reshape(CW_ROWS, C)[:KW]

    xs, target = x[0], loss_target[0]
    g_final2 = g_final.reshape(1, D)
    h, qkv, glu, gl = _in_proj(xs, g_mix_norm, w_in_t_full, b_in)
    o, lse = _attn_fwd(qkv, sinks)
    relay_1 = gather_relay("b1", state_b, [o], 0, 3)
    u, cact = _conv_fwd(glu, conv_w_full, conv_b, ln_g, ln_b, relay_1[3])
    w_ap4, w_cp4, w_out4 = gather_finish("b1", relay_1, [cact])
    w_out_full = w_out4.reshape(D, D)
    relay_2 = gather_relay("b2", state_b, [cact], 3, 1)
    ya, yc, mg, x1 = _mix_out(xs, o, cact, gl, w_ap4, w_cp4, b_conv_proj, w_out_full, relay_2[3])
    w_fi4, = gather_finish("b2", relay_2, [x1])
    relay_3 = gather_relay("b3", state_b, [x1], 4, 1)
    h2, gu, act = _ffn_in(x1, g_ffn_norm, w_fi4, relay_3[3])
    w_dn4, = gather_finish("b3", relay_3, [act])
    w_dn_full = w_dn4.reshape(DFF, D)
    dx2, dx2b, dg_final, loss_part = _ffn_out_loss(x1, act, w_dn_full, g_final2, target)

    def reduce_start(tag, names, grads):
        from_sibling = _pair_exchange("pair_exchange_" + tag, grads)
        pair = [_pair_sum(c_idx, g, r, "pair_sum_" + name) for name, g, r in zip(names, grads, from_sibling)]
        lands = [lax.empty((3,) + p.shape[1:], p.dtype) for p in pair]
        return _chip_start("chip_start_" + tag, _reduce_ends, REDUCE_PEERS, pair, lands)

    def reduce_finish(tag, names, state, after):
        send_sems, recv_sems, pair, lands, _ = state
        pair, lands = _chip_wait("chip_wait_" + tag, _reduce_ends, send_sems, recv_sems, pair, lands, after)
        halves = [_chip_sum(sc_idx, p, r, "chip_sum_" + name) for name, p, r in zip(names, pair, lands)]
        for name, both in zip(names, _pair_share("pair_share_" + tag, halves)):
            wv, mv, vv = big[name]
            res = _adamw(wv, both.reshape(wv.shape), mv, vv, "adamw_" + name)
            if name == "w_in":
                res = [a.T for a in res]
            out_g[name], out_d[name], out_m[name], out_v[name] = [a[None] for a in res]

    dgu, dx1, dx1b, dg_ffn = _ffn_bwd(dx2, dx2b, gu, x1, g_ffn_norm, w_dn_full, w_fi4)
    names_1 = ["w_ffn_in", "w_ffn_down"]
    state_1 = reduce_start("1", names_1, [_grad_w(h2, dgu, "grad_w_ffn_in", 512, FSH, True),
                                          _grad_w(act, dx2b, "grad_w_ffn_down", 256, D, False)])
    dya, dyc, dgl, do, dc, db_cp = _mix_bwd(dx1b, gl, ya, yc, w_out_full, w_ap4, w_cp4, state_1[4])
    grads_2 = [_grad_w(mg, dx1b, "grad_w_out", 512, D, False),
               _grad_w(o, dya, "grad_w_attn_proj", 512, 256, True),
               _grad_w(cact, dyc, "grad_w_conv_proj", 512, 256, True)]
    dglu, dconv_w, dconv_b, dln_g, dln_b = _conv_bwd(glu, u, dc, conv_w_full, ln_g, ln_b, state_1[4])
    dq, dkv, dsinks = _attn_bwd(qkv, o, do, lse, sinks)
    grad_x, dg_mix, db_in = _in_proj_bwd(dq, dkv, dglu, dgl, xs, dx1, g_mix_norm, w_in_t_full)
    names_2 = ["w_in", "w_out", "w_attn_proj", "w_conv_proj"]
    gw_in_t = _grad_w_in_t(h, dq, dkv, dglu, dgl)
    state_2 = reduce_start("2", names_2, [gw_in_t.reshape(N_CHIPS, INW // N_CHIPS, D)] + grads_2)

    reduce_finish("1", names_1, state_1, [state_2[4]])

    gs = {"g_mix_norm": dg_mix, "b_in": db_in, "sinks": dsinks[:, 0].reshape(1, NQ),
          "conv_w": dconv_w, "conv_b": dconv_b, "ln_g": dln_g, "ln_b": dln_b,
          "b_conv_proj": db_cp, "g_ffn_norm": dg_ffn, "g_final": dg_final}
    vec, cw = _pack_small(gs, loss_part[0, 0])
    vec_all, cw_all = _gather_small([vec, cw])

    def view(a, name):
        if name == "conv_w":
            return a[0]
        if name == "g_final":
            return a.reshape(1, D)
        return a

    wmv = [(view(w[name], name), view(m[name], name), view(v[name], name)) for name in SMALL_NAMES]
    small_out, loss_row = _small_update(s_idx, vec_all, cw_all, wmv)
    for name, (g, d, mm, vv) in zip(SMALL_NAMES, small_out):
        shape = w[name].shape
        out_g[name], out_d[name], out_m[name], out_v[name] = (
            g.reshape(shape), d.reshape(shape), mm.reshape(shape), vv.reshape(shape))

    reduce_finish("2", names_2, state_2, [loss_row, out_d["w_ffn_down"]])

    loss = loss_row[0, 0]
    return (loss, grad_x[None], *[out_g[k] for k in WEIGHT_ORDER], *[out_d[k] for k in WEIGHT_ORDER],
            *[out_m[k] for k in WEIGHT_ORDER], *[out_v[k] for k in WEIGHT_ORDER])
```

```python
import functools

import jax
import jax.numpy as jnp
from jax import lax
from jax.experimental import pallas as pl
from jax.experimental.pallas import tpu as pltpu

F32 = jnp.float32
BF16 = jnp.bfloat16

T = 2048
D = 1024
HD = 64
NQ = 8
NKV = 2
GROUP = NQ // NKV
BLK = 128
AW = NQ * HD
KVW = NKV * HD
C = 512
KW = 31
QKVW = AW + 2 * KVW
GLU_OFF = QKVW
GATE_OFF = GLU_OFF + 2 * C
INW = GATE_OFF + 2 * D
DFF = 2816
EPS = 1e-5
NEG = -1e30
SCALE = HD ** -0.5
HALO = 32
N_CHIPS = 4
FSH = 2 * DFF // N_CHIPS

ADAM_LR = 0.001
ADAM_B1 = 0.9
ADAM_B2 = 0.999
ADAM_EPS = 1e-08
ADAM_WD = 0.01
ADAM_STEP = 10

VMEM_LIMIT = 56 * 1024 * 1024
ROW_TM = 512
MESH = pl.DeviceIdType.MESH


def _params(*sem):
    return pltpu.CompilerParams(dimension_semantics=sem, vmem_limit_bytes=VMEM_LIMIT)


def _dot(a, b):
    return jnp.dot(a, b, preferred_element_type=F32)


def _dot_nt(a, b):
    return lax.dot_general(a, b, (((1,), (1,)), ((), ())), preferred_element_type=F32)


def _dot_tn(a, b):
    return lax.dot_general(a, b, (((0,), (0,)), ((), ())), preferred_element_type=F32)


def _sigmoid(v):
    return 1.0 / (1.0 + jnp.exp(-v))


def _rows(tm, n):
    return pl.BlockSpec((tm, n), lambda i: (i, 0))


def _whole(shape):
    return pl.BlockSpec(shape, lambda i: tuple(0 for _ in shape))


def _in_proj(x, g_mix, w_in_t, b_in):
    tm = ROW_TM

    def body(x_ref, g_ref, w_ref, b_ref, h_ref, qkv_ref, glu_ref, gl_ref):
        xv = x_ref[...]
        r = lax.rsqrt(jnp.mean(xv * xv, axis=-1, keepdims=True) + EPS)
        h = (xv * r * g_ref[...]).astype(BF16)
        h_ref[...] = h
        qkv_ref[...] = (_dot_nt(h, w_ref[0:GLU_OFF, :]) + b_ref[:, 0:GLU_OFF]).astype(BF16)
        glu_ref[...] = (_dot_nt(h, w_ref[GLU_OFF:GATE_OFF, :]) + b_ref[:, GLU_OFF:GATE_OFF]).astype(BF16)
        gl_ref[...] = (_dot_nt(h, w_ref[GATE_OFF:INW, :]) + b_ref[:, GATE_OFF:INW]).astype(BF16)

    return pl.pallas_call(
        body, name="in_proj", grid=(T // tm,),
        in_specs=[_rows(tm, D), _whole((1, D)), _whole((INW, D)), _whole((1, INW))],
        out_specs=[_rows(tm, D), _rows(tm, QKVW), _rows(tm, 2 * C), _rows(tm, 2 * D)],
        out_shape=[jax.ShapeDtypeStruct((T, D), BF16), jax.ShapeDtypeStruct((T, QKVW), BF16),
                   jax.ShapeDtypeStruct((T, 2 * C), BF16), jax.ShapeDtypeStruct((T, 2 * D), BF16)],
        compiler_params=_params("parallel"),
    )(x, g_mix, w_in_t, b_in)


GROWS = GROUP * BLK
BAND = 2 * BLK


def _band(i):
    rb = pl.multiple_of(jnp.maximum(i - 1, 0) * BLK, BLK)
    row = lax.broadcasted_iota(jnp.int32, (GROWS, BAND), 0)
    kpos = rb + lax.broadcasted_iota(jnp.int32, (GROWS, BAND), 1)
    qpos = i * BLK + jnp.bitwise_and(row, BLK - 1)
    return rb, jnp.logical_and(kpos <= qpos, kpos > qpos - BLK)


def _sink_column(sink_ref, g):
    head = lax.shift_right_logical(lax.broadcasted_iota(jnp.int32, (GROWS, 1), 0), 7)
    col = jnp.full((GROWS, 1), sink_ref[0, g * GROUP], F32)
    for hh in range(1, GROUP):
        col = jnp.where(head == hh, sink_ref[0, g * GROUP + hh], col)
    return col


def _attn_fwd(qkv, sinks):
    def body(sink_ref, qkv_ref, o_ref, lse_ref, s_ref, p_ref):
        i = pl.program_id(0)
        r0 = pl.multiple_of(i * BLK, BLK)
        rb, valid = _band(i)
        for g in range(NKV):
            kband = qkv_ref[pl.ds(rb, BAND), AW + g * HD:AW + (g + 1) * HD]
            vband = qkv_ref[pl.ds(rb, BAND), AW + KVW + g * HD:AW + KVW + (g + 1) * HD]
            for hh in range(GROUP):
                h = g * GROUP + hh
                s_ref[hh * BLK:(hh + 1) * BLK, :] = _dot_nt(qkv_ref[pl.ds(r0, BLK), h * HD:(h + 1) * HD], kband)
            s = jnp.where(valid, s_ref[...] * SCALE, NEG)
            sink = _sink_column(sink_ref, g)
            m = jnp.maximum(jnp.max(s, axis=-1, keepdims=True), sink)
            p = jnp.exp(s - m)
            den = jnp.sum(p, axis=-1, keepdims=True) + jnp.exp(sink - m)
            p_ref[...] = (p * (1.0 / den)).astype(BF16)
            lse = m + jnp.log(den)
            for hh in range(GROUP):
                h = g * GROUP + hh
                o_ref[:, h * HD:(h + 1) * HD] = _dot(p_ref[hh * BLK:(hh + 1) * BLK, :], vband).astype(BF16)
                lse_ref[:, h:h + 1] = lse[hh * BLK:(hh + 1) * BLK]

    return pl.pallas_call(
        body, name="attn_fwd", grid=(T // BLK,),
        in_specs=[pl.BlockSpec(memory_space=pltpu.SMEM), _whole((T, QKVW))],
        out_specs=[_rows(BLK, AW), _rows(BLK, NQ)],
        out_shape=[jax.ShapeDtypeStruct((T, AW), BF16), jax.ShapeDtypeStruct((T, NQ), F32)],
        scratch_shapes=[pltpu.VMEM((GROWS, BAND), F32), pltpu.VMEM((GROWS, BAND), BF16)],
        compiler_params=_params("parallel"),
    )(sinks, qkv)


CONV_TM = 256
CONV_SUB = 32


def _glu(ab):
    a = ab[:, 0:C].astype(F32)
    b = ab[:, C:2 * C].astype(F32)
    return a * _sigmoid(b)


SUBLANES = 8


def _shifted_copies(ref):
    rows = ref.shape[1] - SUBLANES
    for r in range(1, SUBLANES):
        ref[r, 0:rows, :] = ref[0, r:r + rows, :]


def _shifted_rows(ref, start, size):
    r = start % SUBLANES
    return ref[r, start - r:start - r + size, :]


def _conv_fwd(glu, conv_w, conv_b, ln_g, ln_b, dep):
    tm = CONV_TM

    def body(cur_ref, prev_ref, w_ref, cb_ref, g_ref, b_ref, dep_ref, u_ref, c_ref, zs_ref):
        i = pl.program_id(0)
        zprev = _glu(prev_ref[tm - HALO:tm, :])
        zs_ref[0, 0:HALO, :] = jnp.where(i > 0, zprev, 0.0)
        zs_ref[0, HALO:HALO + tm, :] = _glu(cur_ref[...])
        _shifted_copies(zs_ref)
        for s in range(tm // CONV_SUB):
            base = HALO + s * CONV_SUB - (KW - 1)
            acc = jnp.broadcast_to(cb_ref[...], (CONV_SUB, C))
            for j in range(KW):
                acc = acc + w_ref[j:j + 1, :] * _shifted_rows(zs_ref, base + j, CONV_SUB)
            rows = slice(s * CONV_SUB, (s + 1) * CONV_SUB)
            u_ref[rows, :] = acc
            mu = jnp.mean(acc, axis=-1, keepdims=True)
            xc = acc - mu
            var = jnp.mean(xc * xc, axis=-1, keepdims=True)
            y = xc * lax.rsqrt(var + EPS) * g_ref[...] + b_ref[...]
            c_ref[rows, :] = (y * _sigmoid(y)).astype(BF16)

    return pl.pallas_call(
        body, name="conv_fwd", grid=(T // tm,),
        in_specs=[_rows(tm, 2 * C),
                  pl.BlockSpec((tm, 2 * C), lambda i: (jnp.maximum(i - 1, 0), 0)),
                  _whole((KW, C)), _whole((1, C)), _whole((1, C)), _whole((1, C)), _whole((8, 128))],
        out_specs=[_rows(tm, C), _rows(tm, C)],
        out_shape=[jax.ShapeDtypeStruct((T, C), F32), jax.ShapeDtypeStruct((T, C), BF16)],
        scratch_shapes=[pltpu.VMEM((SUBLANES, HALO + tm, C), F32)],
        compiler_params=_params("parallel"),
    )(glu, glu, conv_w, conv_b, ln_g, ln_b, dep)


def _mix_out(x, o, cact, gl, w_ap, w_cp, b_cp, w_out, dep):
    tm = ROW_TM

    def body(x_ref, o_ref, c_ref, gl_ref, wap_ref, wcp_ref, bcp_ref, wo_ref, dep_ref,
             ya_ref, yc_ref, mg_ref, x1_ref):
        ov, cv = o_ref[...], c_ref[...]
        ya = jnp.concatenate([_dot(ov, wap_ref[s]) for s in range(N_CHIPS)], axis=1)
        yc = jnp.concatenate([_dot(cv, wcp_ref[s]) for s in range(N_CHIPS)], axis=1) + bcp_ref[...]
        g0 = _sigmoid(gl_ref[:, 0:D].astype(F32))
        g1 = _sigmoid(gl_ref[:, D:2 * D].astype(F32))
        mg = (g0 * ya + g1 * yc).astype(BF16)
        ya_ref[...] = ya.astype(BF16)
        yc_ref[...] = yc.astype(BF16)
        mg_ref[...] = mg
        x1_ref[...] = x_ref[...] + _dot(mg, wo_ref[...])

    return pl.pallas_call(
        body, name="mix_out", grid=(T // tm,),
        in_specs=[_rows(tm, D), _rows(tm, AW), _rows(tm, C), _rows(tm, 2 * D),
                  _whole((N_CHIPS, AW, D // N_CHIPS)), _whole((N_CHIPS, C, D // N_CHIPS)), _whole((1, D)),
                  _whole((D, D)), _whole((8, 128))],
        out_specs=[_rows(tm, D), _rows(tm, D), _rows(tm, D), _rows(tm, D)],
        out_shape=[jax.ShapeDtypeStruct((T, D), BF16), jax.ShapeDtypeStruct((T, D), BF16),
                   jax.ShapeDtypeStruct((T, D), BF16), jax.ShapeDtypeStruct((T, D), F32)],
        compiler_params=_params("parallel"),
    )(x, o, cact, gl, w_ap, w_cp, b_cp, w_out, dep)


def _ffn_in(x1, g_ffn, w_fi, dep):
    tm = ROW_TM

    def body(x_ref, g_ref, w_ref, dep_ref, h_ref, gu_ref, act_ref):
        xv = x_ref[...]
        r = lax.rsqrt(jnp.mean(xv * xv, axis=-1, keepdims=True) + EPS)
        h = (xv * r * g_ref[...]).astype(BF16)
        h_ref[...] = h
        for s in range(N_CHIPS // 2):
            c0 = s * FSH
            gate = _dot(h, w_ref[s])
            up = _dot(h, w_ref[s + N_CHIPS // 2])
            gu_ref[:, c0:c0 + FSH] = gate.astype(BF16)
            gu_ref[:, DFF + c0:DFF + c0 + FSH] = up.astype(BF16)
            act_ref[:, c0:c0 + FSH] = (gate * _sigmoid(gate) * up).astype(BF16)

    return pl.pallas_call(
        body, name="ffn_in", grid=(T // tm,),
        in_specs=[_rows(tm, D), _whole((1, D)), _const((N_CHIPS, D, FSH)), _whole((8, 128))],
        out_specs=[_rows(tm, D), _rows(tm, 2 * DFF), _rows(tm, DFF)],
        out_shape=[jax.ShapeDtypeStruct((T, D), BF16), jax.ShapeDtypeStruct((T, 2 * DFF), BF16),
                   jax.ShapeDtypeStruct((T, DFF), BF16)],
        compiler_params=_params("parallel"),
    )(x1, g_ffn, w_fi, dep)


def _ffn_out_loss(x1, act, w_dn, g_final, target):
    tm = ROW_TM

    def body(x_ref, a_ref, w_ref, g_ref, t_ref, dx_ref, dxb_ref, dg_ref, loss_ref):
        i = pl.program_id(0)
        x2 = x_ref[...] + _dot(a_ref[...], w_ref[...])
        r = lax.rsqrt(jnp.mean(x2 * x2, axis=-1, keepdims=True) + EPS)
        xh = x2 * r
        g = g_ref[...]
        err = xh * g - t_ref[...]
        dy = err * (1.0 / D)
        dyg = dy * g
        dx = r * (dyg - xh * jnp.mean(dyg * xh, axis=-1, keepdims=True))
        dx_ref[...] = dx
        dxb_ref[...] = dx.astype(BF16)
        part = 0.5 * jnp.sum(jnp.mean(err * err, axis=-1, keepdims=True), axis=0, keepdims=True)

        @pl.when(i == 0)
        def _():
            dg_ref[...] = jnp.zeros_like(dg_ref)
            loss_ref[...] = jnp.zeros_like(loss_ref)

        dg_ref[...] += jnp.sum(dy * xh, axis=0, keepdims=True)
        loss_ref[...] += jnp.broadcast_to(part, loss_ref.shape)

    return pl.pallas_call(
        body, name="ffn_out_loss", grid=(T // tm,),
        in_specs=[_rows(tm, D), _rows(tm, DFF), _whole((DFF, D)), _whole((1, D)), _rows(tm, D)],
        out_specs=[_rows(tm, D), _rows(tm, D), _whole((1, D)), _whole((1, 128))],
        out_shape=[jax.ShapeDtypeStruct((T, D), F32), jax.ShapeDtypeStruct((T, D), BF16),
                   jax.ShapeDtypeStruct((1, D), F32), jax.ShapeDtypeStruct((1, 128), F32)],
        compiler_params=_params("arbitrary"),
    )(x1, act, w_dn, g_final, target)


def _const(shape):
    return pl.BlockSpec(shape, lambda i: tuple(0 for _ in shape), pipeline_mode=pl.Buffered(1))


def _ffn_bwd(dx2, dx2b, gu, x1, g_ffn, w_dn_t, w_fi_t):
    tm = ROW_TM // 2

    def body(dx_ref, dxb_ref, gu_ref, x_ref, g_ref, wdn_ref, wfi_ref,
             dgu_ref, dx1_ref, dx1b_ref, dg_ref):
        i = pl.program_id(0)
        dxb = dxb_ref[...]
        dh = jnp.zeros((tm, D), F32)
        for k in range(N_CHIPS // 2):
            c0 = k * FSH
            dact = _dot_nt(dxb, wdn_ref[c0:c0 + FSH, :])
            gate = gu_ref[:, c0:c0 + FSH].astype(F32)
            up = gu_ref[:, DFF + c0:DFF + c0 + FSH].astype(F32)
            s = _sigmoid(gate)
            dup = (dact * gate * s).astype(BF16)
            dgate = (dact * up * s * (1.0 + gate * (1.0 - s))).astype(BF16)
            dgu_ref[:, c0:c0 + FSH] = dgate
            dgu_ref[:, DFF + c0:DFF + c0 + FSH] = dup
            dh = dh + _dot_nt(dgate, wfi_ref[k]) + _dot_nt(dup, wfi_ref[k + N_CHIPS // 2])
        xv = x_ref[...]
        r = lax.rsqrt(jnp.mean(xv * xv, axis=-1, keepdims=True) + EPS)
        xh = xv * r
        dhg = dh * g_ref[...]
        dx1 = dx_ref[...] + r * (dhg - xh * jnp.mean(dhg * xh, axis=-1, keepdims=True))
        dx1_ref[...] = dx1
        dx1b_ref[...] = dx1.astype(BF16)

        @pl.when(i == 0)
        def _():
            dg_ref[...] = jnp.zeros_like(dg_ref)

        dg_ref[...] += jnp.sum(dh * xh, axis=0, keepdims=True)

    return pl.pallas_call(
        body, name="ffn_bwd", grid=(T // tm,),
        in_specs=[_rows(tm, D), _rows(tm, D), _rows(tm, 2 * DFF), _rows(tm, D), _whole((1, D)),
                  _const((DFF, D)), _const((N_CHIPS, D, FSH))],
        out_specs=[_rows(tm, 2 * DFF), _rows(tm, D), _rows(tm, D), _whole((1, D))],
        out_shape=[jax.ShapeDtypeStruct((T, 2 * DFF), BF16), jax.ShapeDtypeStruct((T, D), F32),
                   jax.ShapeDtypeStruct((T, D), BF16), jax.ShapeDtypeStruct((1, D), F32)],
        compiler_params=_params("arbitrary"),
    )(dx2, dx2b, gu, x1, g_ffn, w_dn_t, w_fi_t)


def _mix_bwd(dx1b, gl, ya, yc, w_out_t, w_ap_t, w_cp_t, dep):
    tm = ROW_TM

    def body(dx_ref, gl_ref, ya_ref, yc_ref, wo_ref, wap_ref, wcp_ref, dep_ref,
             dya_ref, dyc_ref, dgl_ref, do_ref, dc_ref, db_ref):
        i = pl.program_id(0)
        dm = _dot_nt(dx_ref[...], wo_ref[...])
        g0 = _sigmoid(gl_ref[:, 0:D].astype(F32))
        g1 = _sigmoid(gl_ref[:, D:2 * D].astype(F32))
        dya = dm * g0
        dyc = dm * g1
        dgl_ref[:, 0:D] = (dya * ya_ref[...].astype(F32) * (1.0 - g0)).astype(BF16)
        dgl_ref[:, D:2 * D] = (dyc * yc_ref[...].astype(F32) * (1.0 - g1)).astype(BF16)
        dyab = dya.astype(BF16)
        dycb = dyc.astype(BF16)
        dya_ref[...] = dyab
        dyc_ref[...] = dycb
        sw = D // N_CHIPS
        do = jnp.zeros((tm, AW), F32)
        dcv = jnp.zeros((tm, C), F32)
        for s in range(N_CHIPS):
            do = do + _dot_nt(dyab[:, s * sw:(s + 1) * sw], wap_ref[s])
            dcv = dcv + _dot_nt(dycb[:, s * sw:(s + 1) * sw], wcp_ref[s])
        do_ref[...] = do.astype(BF16)
        dc_ref[...] = dcv

        @pl.when(i == 0)
        def _():
            db_ref[...] = jnp.zeros_like(db_ref)

        db_ref[...] += jnp.sum(dyc, axis=0, keepdims=True)

    return pl.pallas_call(
        body, name="mix_bwd", grid=(T // tm,),
        in_specs=[_rows(tm, D), _rows(tm, 2 * D), _rows(tm, D), _rows(tm, D),
                  _whole((D, D)), _whole((N_CHIPS, AW, D // N_CHIPS)), _whole((N_CHIPS, C, D // N_CHIPS)),
                  _whole((8, 128))],
        out_specs=[_rows(tm, D), _rows(tm, D), _rows(tm, 2 * D), _rows(tm, AW), _rows(tm, C),
                   _whole((1, D))],
        out_shape=[jax.ShapeDtypeStruct((T, D), BF16), jax.ShapeDtypeStruct((T, D), BF16),
                   jax.ShapeDtypeStruct((T, 2 * D), BF16), jax.ShapeDtypeStruct((T, AW), BF16),
                   jax.ShapeDtypeStruct((T, C), F32), jax.ShapeDtypeStruct((1, D), F32)],
        compiler_params=_params("arbitrary"),
    )(dx1b, gl, ya, yc, w_out_t, w_ap_t, w_cp_t, dep)


def _conv_bwd(glu, u, dc, conv_w, ln_g, ln_b, dep):
    tm = CONV_TM
    nblk = T // tm

    def du_of(uv, dcv, g_ref, b_ref):
        mu = jnp.mean(uv, axis=-1, keepdims=True)
        xc = uv - mu
        var = jnp.mean(xc * xc, axis=-1, keepdims=True)
        rstd = lax.rsqrt(var + EPS)
        xh = xc * rstd
        y = xh * g_ref[...] + b_ref[...]
        sg = _sigmoid(y)
        dy = dcv * (sg * (1.0 + y * (1.0 - sg)))
        dxh = dy * g_ref[...]
        du = rstd * (dxh - jnp.mean(dxh, axis=-1, keepdims=True)
                     - xh * jnp.mean(dxh * xh, axis=-1, keepdims=True))
        return du, dy, xh

    def body(cur_ref, prev_ref, u_ref, un_ref, dc_ref, dcn_ref, w_ref, g_ref, b_ref, dep_ref,
             dglu_ref, dw_ref, dcb_ref, dg_ref, db_ref, zs_ref, dus_ref):
        i = pl.program_id(0)

        @pl.when(i == 0)
        def _():
            dw_ref[...] = jnp.zeros_like(dw_ref)
            dcb_ref[...] = jnp.zeros_like(dcb_ref)
            dg_ref[...] = jnp.zeros_like(dg_ref)
            db_ref[...] = jnp.zeros_like(db_ref)

        zprev = _glu(prev_ref[tm - HALO:tm, :])
        zs_ref[0, 0:HALO, :] = jnp.where(i > 0, zprev, 0.0)
        zs_ref[0, HALO:HALO + tm, :] = _glu(cur_ref[...])
        _shifted_copies(zs_ref)

        dun, _, _ = du_of(un_ref[0:HALO, :], dcn_ref[0:HALO, :], g_ref, b_ref)
        dus_ref[0, tm:tm + HALO, :] = jnp.where(i < nblk - 1, dun, 0.0)
        dg_acc = jnp.zeros((1, C), F32)
        db_acc = jnp.zeros((1, C), F32)
        dcb_acc = jnp.zeros((1, C), F32)
        for s in range(tm // CONV_SUB):
            rows = slice(s * CONV_SUB, (s + 1) * CONV_SUB)
            du, dy, xh = du_of(u_ref[rows, :], dc_ref[rows, :], g_ref, b_ref)
            dus_ref[0, rows, :] = du
            dg_acc = dg_acc + jnp.sum(dy * xh, axis=0, keepdims=True)
            db_acc = db_acc + jnp.sum(dy, axis=0, keepdims=True)
            dcb_acc = dcb_acc + jnp.sum(du, axis=0, keepdims=True)
        dg_ref[...] += dg_acc
        db_ref[...] += db_acc
        dcb_ref[...] += dcb_acc
        _shifted_copies(dus_ref)

        for j in range(KW):
            acc = jnp.zeros((CONV_SUB, C), F32)
            for s in range(tm // CONV_SUB):
                base = HALO + s * CONV_SUB - (KW - 1) + j
                acc = acc + dus_ref[0, s * CONV_SUB:(s + 1) * CONV_SUB, :] * _shifted_rows(zs_ref, base, CONV_SUB)
            dw_ref[j:j + 1, :] += jnp.sum(acc, axis=0, keepdims=True)

        for s in range(tm // CONV_SUB):
            rows = slice(s * CONV_SUB, (s + 1) * CONV_SUB)
            dz = jnp.zeros((CONV_SUB, C), F32)
            for j in range(KW):
                dz = dz + w_ref[j:j + 1, :] * _shifted_rows(dus_ref, s * CONV_SUB + (KW - 1) - j, CONV_SUB)
            a = cur_ref[rows, 0:C].astype(F32)
            sb = _sigmoid(cur_ref[rows, C:2 * C].astype(F32))
            dglu_ref[rows, 0:C] = (dz * sb).astype(BF16)
            dglu_ref[rows, C:2 * C] = (dz * a * sb * (1.0 - sb)).astype(BF16)

    nxt = lambda i: (jnp.minimum(i + 1, nblk - 1), 0)
    return pl.pallas_call(
        body, name="conv_bwd", grid=(nblk,),
        in_specs=[_rows(tm, 2 * C),
                  pl.BlockSpec((tm, 2 * C), lambda i: (jnp.maximum(i - 1, 0), 0)),
                  _rows(tm, C), pl.BlockSpec((tm, C), nxt),
                  _rows(tm, C), pl.BlockSpec((tm, C), nxt),
                  _whole((KW, C)), _whole((1, C)), _whole((1, C)), _whole((8, 128))],
        out_specs=[_rows(tm, 2 * C), _whole((KW, C)), _whole((1, C)), _whole((1, C)), _whole((1, C))],
        out_shape=[jax.ShapeDtypeStruct((T, 2 * C), BF16), jax.ShapeDtypeStruct((KW, C), F32),
                   jax.ShapeDtypeStruct((1, C), F32), jax.ShapeDtypeStruct((1, C), F32),
                   jax.ShapeDtypeStruct((1, C), F32)],
        scratch_shapes=[pltpu.VMEM((SUBLANES, HALO + tm, C), F32), pltpu.VMEM((SUBLANES, tm + HALO, C), F32)],
        compiler_params=_params("arbitrary"),
    )(glu, glu, u, u, dc, dc, conv_w, ln_g, ln_b, dep)


def _attn_bwd(qkv, o, do, lse, sinks):
    def body(sink_ref, qkv_ref, o_ref, do_ref, lse_ref, dq_ref, dkv_ref, ds_ref,
             s_ref, dp_ref, p_ref, dsb_ref):
        i = pl.program_id(0)

        @pl.when(i == 0)
        def _():
            dkv_ref[...] = jnp.zeros_like(dkv_ref)
            ds_ref[...] = jnp.zeros_like(ds_ref)

        r0 = pl.multiple_of(i * BLK, BLK)
        rb, valid = _band(i)
        for g in range(NKV):
            kband = qkv_ref[pl.ds(rb, BAND), AW + g * HD:AW + (g + 1) * HD]
            vband = qkv_ref[pl.ds(rb, BAND), AW + KVW + g * HD:AW + KVW + (g + 1) * HD]
            lse_parts, dl_parts = [], []
            for hh in range(GROUP):
                h = g * GROUP + hh
                hcol = slice(h * HD, (h + 1) * HD)
                doh = do_ref[:, hcol]
                s_ref[hh * BLK:(hh + 1) * BLK, :] = _dot_nt(qkv_ref[pl.ds(r0, BLK), hcol], kband)
                dp_ref[hh * BLK:(hh + 1) * BLK, :] = _dot_nt(doh, vband)
                lse_parts.append(lse_ref[:, h:h + 1])
                dl_parts.append(jnp.sum(doh.astype(F32) * o_ref[:, hcol].astype(F32), axis=-1, keepdims=True))
            lse = jnp.concatenate(lse_parts, axis=0)
            dl = jnp.concatenate(dl_parts, axis=0)
            p = jnp.where(valid, jnp.exp(s_ref[...] * SCALE - lse), 0.0)
            p_ref[...] = p.astype(BF16)
            dsb_ref[...] = (p * (dp_ref[...] - dl)).astype(BF16)
            dsink = -(jnp.exp(_sink_column(sink_ref, g) - lse) * dl)
            dk = jnp.zeros((BAND, HD), F32)
            dv = jnp.zeros((BAND, HD), F32)
            for hh in range(GROUP):
                h = g * GROUP + hh
                hcol = slice(h * HD, (h + 1) * HD)
                rows = slice(hh * BLK, (hh + 1) * BLK)
                dq_ref[:, hcol] = (_dot(dsb_ref[rows, :], kband) * SCALE).astype(BF16)
                dk = dk + _dot_tn(dsb_ref[rows, :], qkv_ref[pl.ds(r0, BLK), hcol])
                dv = dv + _dot_tn(p_ref[rows, :], do_ref[:, hcol])
                ds_ref[h:h + 1, :] += jnp.broadcast_to(jnp.sum(dsink[rows], axis=0, keepdims=True), (1, 128))
            dkv_ref[pl.ds(rb, BAND), g * HD:(g + 1) * HD] += dk * SCALE
            dkv_ref[pl.ds(rb, BAND), KVW + g * HD:KVW + (g + 1) * HD] += dv

    return pl.pallas_call(
        body, name="attn_bwd", grid=(T // BLK,),
        in_specs=[pl.BlockSpec(memory_space=pltpu.SMEM), _whole((T, QKVW)),
                  _rows(BLK, AW), _rows(BLK, AW), _rows(BLK, NQ)],
        out_specs=[_rows(BLK, AW), _whole((T, 2 * KVW)), _whole((NQ, 128))],
        out_shape=[jax.ShapeDtypeStruct((T, AW), BF16), jax.ShapeDtypeStruct((T, 2 * KVW), F32),
                   jax.ShapeDtypeStruct((NQ, 128), F32)],
        scratch_shapes=[pltpu.VMEM((GROWS, BAND), F32), pltpu.VMEM((GROWS, BAND), F32),
                        pltpu.VMEM((GROWS, BAND), BF16), pltpu.VMEM((GROWS, BAND), BF16)],
        compiler_params=_params("arbitrary"),
    )(sinks, qkv, o, do, lse)


PROJ_PARTS = [(0, AW), (AW, QKVW), (GLU_OFF, GATE_OFF), (GATE_OFF, INW)]


def _in_proj_bwd(dq, dkv, dglu, dgl, x, dx1, g_mix, w_in_t, dep):
    tm = ROW_TM

    def body(dq_ref, dkv_ref, dglu_ref, dgl_ref, x_ref, dx1_ref, g_ref, w_ref, dep_ref,
             gx_ref, dg_ref, db_ref):
        i = pl.program_id(0)

        @pl.when(i == 0)
        def _():
            dg_ref[...] = jnp.zeros_like(dg_ref)
            db_ref[...] = jnp.zeros_like(db_ref)

        dh = jnp.zeros((tm, D), F32)
        for part_ref, (lo, hi) in zip((dq_ref, dkv_ref, dglu_ref, dgl_ref), PROJ_PARTS):
            part = part_ref[...]
            dh = dh + _dot(part.astype(BF16), w_ref[lo:hi, :])
            db_ref[:, lo:hi] += jnp.sum(part.astype(F32), axis=0, keepdims=True)
        xv = x_ref[...]
        r = lax.rsqrt(jnp.mean(xv * xv, axis=-1, keepdims=True) + EPS)
        xh = xv * r
        dhg = dh * g_ref[...]
        gx_ref[...] = dx1_ref[...] + r * (dhg - xh * jnp.mean(dhg * xh, axis=-1, keepdims=True))
        dg_ref[...] += jnp.sum(dh * xh, axis=0, keepdims=True)

    return pl.pallas_call(
        body, name="in_proj_bwd", grid=(T // tm,),
        in_specs=[_rows(tm, AW), _rows(tm, 2 * KVW), _rows(tm, 2 * C), _rows(tm, 2 * D),
                  _rows(tm, D), _rows(tm, D), _whole((1, D)), _const((INW, D)), _whole((8, 128))],
        out_specs=[_rows(tm, D), _whole((1, D)), _whole((1, INW))],
        out_shape=[jax.ShapeDtypeStruct((T, D), F32), jax.ShapeDtypeStruct((1, D), F32),
                   jax.ShapeDtypeStruct((1, INW), F32)],
        compiler_params=_params("arbitrary"),
    )(dq, dkv, dglu, dgl, x, dx1, g_mix, w_in_t, dep)


def _grad_w_in_t(h, dq, dkv, dglu, dgl):
    tn, chunk = 512, 256

    def body(h_ref, dq_ref, dkv_ref, dglu_ref, dgl_ref, o_ref, pt_ref):
        @pl.when(pl.program_id(0) == 0)
        def _():
            for part_ref, (lo, hi) in zip((dq_ref, dkv_ref, dglu_ref, dgl_ref), PROJ_PARTS):
                for c0 in range(0, hi - lo, chunk):
                    pt_ref[lo + c0:lo + c0 + chunk, :] = part_ref[:, c0:c0 + chunk].astype(BF16).T

        hv = h_ref[...]
        for r0 in range(0, INW, QKVW):
            o_ref[r0:r0 + QKVW, :] = _dot(pt_ref[r0:r0 + QKVW, :], hv).astype(BF16)

    return pl.pallas_call(
        body, name="grad_w_in", grid=(D // tn,),
        in_specs=[pl.BlockSpec((T, tn), lambda j: (0, j)), _const((T, AW)), _const((T, 2 * KVW)),
                  _const((T, 2 * C)), _const((T, 2 * D))],
        out_specs=pl.BlockSpec((INW, tn), lambda j: (0, j)),
        out_shape=jax.ShapeDtypeStruct((INW, D), BF16),
        scratch_shapes=[pltpu.VMEM((INW, T), BF16)],
        compiler_params=_params("arbitrary"),
    )(h, dq, dkv, dglu, dgl)


def _grad_w(a, b, name, tk, tn, col_sharded):
    k, n = a.shape[1], b.shape[1]

    def body(a_ref, b_ref, o_ref, at_ref):
        @pl.when(pl.program_id(1) == 0)
        def _():
            at_ref[...] = a_ref[...].T

        o_ref[...] = _dot(at_ref[...], b_ref[...]).astype(BF16)

    if col_sharded:
        per = n // N_CHIPS // tn
        shape = (N_CHIPS, k, n // N_CHIPS)
        out_map = lambda i, j: (j // per, i, j % per)
    else:
        shape = (1, k, n)
        out_map = lambda i, j: (0, i, j)
    out = pl.pallas_call(
        body, name=name, grid=(k // tk, n // tn),
        in_specs=[pl.BlockSpec((T, tk), lambda i, j: (0, i)), pl.BlockSpec((T, tn), lambda i, j: (0, j))],
        out_specs=pl.BlockSpec((None, tk, tn), out_map),
        out_shape=jax.ShapeDtypeStruct(shape, BF16),
        scratch_shapes=[pltpu.VMEM((tk, T), BF16)],
        compiler_params=_params("parallel", "arbitrary"),
    )(a, b)
    return out if col_sharded else out.reshape(N_CHIPS, k // N_CHIPS, n)


HBM_SPEC = pl.BlockSpec(memory_space=pltpu.HBM)


def _place():
    x, y, c = lax.axis_index("x"), lax.axis_index("y"), lax.axis_index("c")
    chips = [(1 - x, y), (x, 1 - y), (1 - x, 1 - y)]
    return x, y, c, chips


SEM_SPEC = pl.BlockSpec(memory_space=pltpu.SEMAPHORE)
ANY_SPEC = pl.BlockSpec(memory_space=pl.ANY)
VMEM_SPEC = pl.BlockSpec(memory_space=pltpu.VMEM)
EFFECT = pltpu.SideEffectType.DATAFLOW_SIDE_EFFECTING


def _gather_ends(src, land, x, y, c, chips):
    kh = src.shape[0] // 2
    s_me = 2 * x + y
    ends = [(src.at[pl.ds(c * kh, kh)], land.at[s_me, pl.ds(c * kh, kh)], (*chip, c)) for chip in chips]
    return ends + [(src, land.at[s_me], (x, y, 1 - c))]


def _reduce_ends(src, land, x, y, c, chips):
    return [(src.at[2 * chip[0] + chip[1]], land.at[j], (*chip, c)) for j, chip in enumerate(chips)]


def _chip_copies(ends, srcs, lands, send_sems, recv_sems, first=0):
    x, y, c, chips = _place()
    copies = []
    for src, land in zip(srcs, lands):
        peers = ends(src, land, x, y, c, chips)
        for s, d, to in peers:
            k = first * len(peers) + len(copies)
            copies.append(pltpu.make_async_remote_copy(
                src_ref=s, dst_ref=d, send_sem=send_sems.at[k], recv_sem=recv_sems.at[k],
                device_id=to, device_id_type=MESH))
    return copies


GATHER_PEERS, REDUCE_PEERS = 4, 3


def _chip_start(name, ends, peers, srcs, lands):
    n = len(srcs)

    def body(*refs):
        copies = _chip_copies(ends, refs[:n], refs[n:2 * n], refs[2 * n], refs[2 * n + 1])
        for cp in copies:
            cp.start()
        token = refs[-1]
        token[...] = jnp.zeros_like(token)

    hbm = lambda a: pltpu.HBM(a.shape, a.dtype)
    res = pl.pallas_call(
        body, name=name,
        out_shape=(pltpu.SemaphoreType.DMA((peers * n,)), pltpu.SemaphoreType.DMA((peers * n,)),
                   *[hbm(a) for a in srcs], *[hbm(a) for a in lands],
                   jax.ShapeDtypeStruct((8, 128), F32)),
        in_specs=[HBM_SPEC] * (2 * n),
        out_specs=(SEM_SPEC, SEM_SPEC, *[HBM_SPEC] * (2 * n), VMEM_SPEC),
        input_output_aliases={i: 2 + i for i in range(2 * n)},
        compiler_params=pltpu.CompilerParams(has_side_effects=EFFECT),
    )(*[pltpu.with_memory_space_constraint(a, pltpu.HBM) for a in (*srcs, *lands)])
    return res[0], res[1], list(res[2:2 + n]), list(res[2 + n:2 + 2 * n]), res[-1]


def _chip_wait(name, ends, send_sems, recv_sems, srcs, lands, after, first=0):
    n, na = len(srcs), len(after)

    def body(*refs):
        copies = _chip_copies(ends, refs[:n], refs[n:2 * n], refs[2 * n], refs[2 * n + 1], first)
        for cp in copies:
            cp.wait_send()
            cp.wait_recv()

    hbm = lambda a: pltpu.HBM(a.shape, a.dtype)
    res = pl.pallas_call(
        body, name=name,
        out_shape=tuple(hbm(a) for a in (*srcs, *lands)),
        in_specs=[HBM_SPEC] * (2 * n) + [SEM_SPEC, SEM_SPEC] + [ANY_SPEC] * na,
        out_specs=tuple([HBM_SPEC] * (2 * n)),
        input_output_aliases={i: i for i in range(2 * n)},
        compiler_params=pltpu.CompilerParams(has_side_effects=EFFECT),
    )(*srcs, *lands, send_sems, recv_sems, *after)
    return list(res[:n]), list(res[n:])


def _forward_copies(lands, send_sems, recv_sems):
    x, y, c, chips = _place()
    copies = []
    for land in lands:
        kh = land.shape[1] // 2
        for chip in chips:
            blk = land.at[2 * chip[0] + chip[1], pl.ds(c * kh, kh)]
            k = len(copies)
            copies.append(pltpu.make_async_remote_copy(
                src_ref=blk, dst_ref=blk, send_sem=send_sems.at[k], recv_sem=recv_sems.at[k],
                device_id=(x, y, 1 - c), device_id_type=MESH))
    return copies


def _gather_relay(name, send_sems, recv_sems, srcs, lands, after, first):
    n, na = len(srcs), len(after)

    def body(*refs):
        land_refs = refs[n:2 * n]
        for cp in _chip_copies(_gather_ends, refs[:n], land_refs, refs[2 * n], refs[2 * n + 1], first):
            cp.wait_send()
            cp.wait_recv()
        out = refs[2 * n + 2 + na:]
        for cp in _forward_copies(land_refs, out[0], out[1]):
            cp.start()
        out[-1][...] = jnp.zeros_like(out[-1])

    hbm = lambda a: pltpu.HBM(a.shape, a.dtype)
    res = pl.pallas_call(
        body, name=name,
        out_shape=(pltpu.SemaphoreType.DMA((3 * n,)), pltpu.SemaphoreType.DMA((3 * n,)),
                   *[hbm(a) for a in lands], jax.ShapeDtypeStruct((8, 128), F32)),
        in_specs=[HBM_SPEC] * (2 * n) + [SEM_SPEC, SEM_SPEC] + [ANY_SPEC] * na,
        out_specs=(SEM_SPEC, SEM_SPEC, *[HBM_SPEC] * n, VMEM_SPEC),
        input_output_aliases={n + i: 2 + i for i in range(n)},
        compiler_params=pltpu.CompilerParams(has_side_effects=EFFECT),
    )(*srcs, *lands, send_sems, recv_sems, *after)
    return res[0], res[1], list(res[2:2 + n]), res[-1]


def _forward_wait(name, send_sems, recv_sems, lands, after):
    n, na = len(lands), len(after)

    def body(*refs):
        for cp in _forward_copies(refs[:n], refs[n], refs[n + 1]):
            cp.wait_send()
            cp.wait_recv()

    hbm = lambda a: pltpu.HBM(a.shape, a.dtype)
    res = pl.pallas_call(
        body, name=name,
        out_shape=tuple(hbm(a) for a in lands),
        in_specs=[HBM_SPEC] * n + [SEM_SPEC, SEM_SPEC] + [ANY_SPEC] * na,
        out_specs=tuple([HBM_SPEC] * n),
        input_output_aliases={i: i for i in range(n)},
        compiler_params=pltpu.CompilerParams(has_side_effects=EFFECT),
    )(*lands, send_sems, recv_sems, *after)
    return list(res)


def _exchange_ends(src, land, x, y, c, chips):
    kh = src.shape[1] // 2
    return [(src.at[:, pl.ds((1 - c) * kh, kh)], land, (x, y, 1 - c))]


def _share_ends(src, land, x, y, c, chips):
    return [(src, land, (x, y, 1 - c))]


def _small_ends(src, land, x, y, c, chips):
    m = src.shape[0]
    rows = land.at[pl.ds((4 * x + 2 * y + c) * m, m)]
    peers = [(x, y, 1 - c)] + [(*chip, c) for chip in chips] + [(*chip, 1 - c) for chip in chips]
    return [(src, rows, to) for to in peers]


PAIR_PEERS, SMALL_PEERS = 1, 7


def _row_tile(k):
    for t in (256, 240, 128, 176, 64, 32, 16):
        if k % t == 0:
            return t
    raise ValueError(k)


def _pair_sum(c_idx, g, got, name):
    _, k, n = g.shape
    kh = k // 2
    tm = _row_tile(kh)
    nb = kh // tm

    def body(c_ref, g_ref, r_ref, o_ref):
        o_ref[...] = (g_ref[...].astype(F32) + r_ref[...].astype(F32)).astype(BF16)

    return pl.pallas_call(
        body, name=name,
        grid_spec=pltpu.PrefetchScalarGridSpec(
            num_scalar_prefetch=1, grid=(N_CHIPS, nb),
            in_specs=[pl.BlockSpec((1, tm, n), lambda s, i, c_ref: (s, c_ref[0] * nb + i, 0)),
                      pl.BlockSpec((1, tm, n), lambda s, i, c_ref: (s, i, 0))],
            out_specs=pl.BlockSpec((1, tm, n), lambda s, i, c_ref: (s, i, 0))),
        out_shape=jax.ShapeDtypeStruct((N_CHIPS, kh, n), BF16),
        compiler_params=_params("parallel", "parallel"),
    )(c_idx, g, got)


def _chip_sum(s_idx, mine, got, name):
    _, kh, n = mine.shape
    tm = _row_tile(kh)

    def body(s_ref, m_ref, r_ref, o_ref):
        acc = m_ref[0].astype(F32)
        for j in range(3):
            acc = acc + r_ref[j].astype(F32)
        o_ref[...] = acc

    return pl.pallas_call(
        body, name=name,
        grid_spec=pltpu.PrefetchScalarGridSpec(
            num_scalar_prefetch=1, grid=(kh // tm,),
            in_specs=[pl.BlockSpec((1, tm, n), lambda i, s_ref: (s_ref[0], i, 0)),
                      pl.BlockSpec((3, tm, n), lambda i, s_ref: (0, i, 0))],
            out_specs=pl.BlockSpec((tm, n), lambda i, s_ref: (i, 0))),
        out_shape=jax.ShapeDtypeStruct((kh, n), F32),
        compiler_params=_params("parallel"),
    )(s_idx, mine, got)


def _adamw_math(w, g, m, v):
    m = ADAM_B1 * m + (1.0 - ADAM_B1) * g
    v = ADAM_B2 * v + (1.0 - ADAM_B2) * (g * g)
    m_hat = m / (1.0 - ADAM_B1 ** ADAM_STEP)
    v_hat = v / (1.0 - ADAM_B2 ** ADAM_STEP)
    delta = -ADAM_LR * (m_hat / (jnp.sqrt(v_hat) + ADAM_EPS) + ADAM_WD * w)
    return delta, m, v


def _adamw(c_idx, w, g_mine, g_other, m, v, name):
    k, n = w.shape
    tm = k // 4

    def body(c_ref, w_ref, gm_ref, go_ref, m_ref, v_ref, g_ref, d_ref, mo_ref, vo_ref):
        g = jnp.where(pl.program_id(0) == c_ref[0], gm_ref[...], go_ref[...])
        d, mm, vv = _adamw_math(w_ref[...], g, m_ref[...], v_ref[...])
        g_ref[...] = g
        d_ref[...] = d
        mo_ref[...] = mm
        vo_ref[...] = vv

    full = pl.BlockSpec((tm, n), lambda h, i, c_ref: (2 * h + i, 0))
    mine = pl.BlockSpec((tm, n), lambda h, i, c_ref: (jnp.where(h == c_ref[0], i, 0), 0))
    other = pl.BlockSpec((tm, n), lambda h, i, c_ref: (jnp.where(h == c_ref[0], 0, i), 0))
    shp = jax.ShapeDtypeStruct((k, n), F32)
    return pl.pallas_call(
        body, name=name,
        grid_spec=pltpu.PrefetchScalarGridSpec(
            num_scalar_prefetch=1, grid=(2, 2),
            in_specs=[full, mine, other, full, full], out_specs=[full] * 4),
        out_shape=[shp] * 4, compiler_params=_params("arbitrary", "arbitrary"),
    )(c_idx, w, g_mine, g_other, m, v)


VEC_SLOTS = {
    "g_mix_norm": (0, 0, D), "b_conv_proj": (0, D, D), "g_ffn_norm": (0, 2 * D, D),
    "g_final": (0, 3 * D, D), "b_in": (1, 0, INW), "conv_b": (2, 0, C), "ln_g": (2, C, C),
    "ln_b": (2, 2 * C, C), "sinks": (2, 3 * C, NQ), "loss": (2, 3 * C + 128, 1),
}
VEC_ROWS, VEC_COLS = 8, 4 * D
CW_ROWS = 32
SMALL_NAMES = ["g_mix_norm", "b_in", "sinks", "conv_w", "conv_b", "ln_g", "ln_b",
               "b_conv_proj", "g_ffn_norm", "g_final"]
CW_LANES = C // N_CHIPS


def _pack_small(gs, loss):
    row0 = jnp.concatenate([gs["g_mix_norm"], gs["b_conv_proj"], gs["g_ffn_norm"], gs["g_final"]], axis=1)
    row1 = jnp.pad(gs["b_in"], ((0, 0), (0, VEC_COLS - INW)))
    row2 = jnp.concatenate([gs["conv_b"], gs["ln_g"], gs["ln_b"],
                            jnp.pad(gs["sinks"], ((0, 0), (0, 128 - NQ))),
                            jnp.pad(loss.reshape(1, 1), ((0, 0), (0, VEC_COLS - 3 * C - 129)))], axis=1)
    vec = jnp.concatenate([row0, row1, row2, jnp.zeros((VEC_ROWS - 3, VEC_COLS), F32)], axis=0)
    cw = jnp.pad(gs["conv_w"], ((0, CW_ROWS - KW), (0, 0)))
    return vec, cw


def _small_update(s_idx, vec_all, cw_all, wmv):
    nsm = len(SMALL_NAMES)

    def body(s_ref, vec_ref, cw_ref, *refs):
        ins = refs[:3 * nsm]
        outs = refs[3 * nsm:7 * nsm]
        loss_ref = refs[7 * nsm]

        def total(slot):
            row, lane, width = slot
            acc = vec_ref[row:row + 1, lane:lane + width]
            for k in range(1, 8):
                acc = acc + vec_ref[k * VEC_ROWS + row:k * VEC_ROWS + row + 1, lane:lane + width]
            return acc

        loss_ref[...] = jnp.broadcast_to(total(VEC_SLOTS["loss"]), loss_ref.shape)
        for p, name in enumerate(SMALL_NAMES):
            w_ref, m_ref, v_ref = ins[3 * p:3 * p + 3]
            g_ref, d_ref, mo_ref, vo_ref = outs[4 * p:4 * p + 4]
            if name == "conv_w":
                g = jnp.zeros((KW, CW_LANES), F32)
                for s in range(N_CHIPS):
                    cand = cw_ref[0:KW, s * CW_LANES:(s + 1) * CW_LANES]
                    for k in range(1, 8):
                        cand = cand + cw_ref[k * CW_ROWS:k * CW_ROWS + KW, s * CW_LANES:(s + 1) * CW_LANES]
                    g = jnp.where(s_ref[0] == s, cand, g)
            else:
                g = total(VEC_SLOTS[name])
            d, mm, vv = _adamw_math(w_ref[...], g, m_ref[...], v_ref[...])
            g_ref[...] = g
            d_ref[...] = d
            mo_ref[...] = mm
            vo_ref[...] = vv

    vmem = pl.BlockSpec(memory_space=pltpu.VMEM)
    flat = [a for t in wmv for a in t]
    out_shape = []
    for w, _, _ in wmv:
        out_shape += [jax.ShapeDtypeStruct(w.shape, F32)] * 4
    out_shape.append(jax.ShapeDtypeStruct((1, 128), F32))
    res = pl.pallas_call(
        body, name="small_update",
        in_specs=[pl.BlockSpec(memory_space=pltpu.SMEM)] + [vmem] * (2 + len(flat)),
        out_specs=[vmem] * len(out_shape), out_shape=out_shape,
    )(s_idx, vec_all, cw_all, *flat)
    return [tuple(res[4 * p:4 * p + 4]) for p in range(nsm)], res[4 * nsm]


WEIGHT_ORDER = ["g_mix_norm", "w_in", "b_in", "sinks", "conv_w", "conv_b", "ln_g", "ln_b",
                "w_attn_proj", "w_conv_proj", "b_conv_proj", "w_out", "g_ffn_norm", "w_ffn_in",
                "w_ffn_down", "g_final"]


def kernel(x, g_mix_norm, w_in, b_in, sinks, conv_w, conv_b, ln_g, ln_b, w_attn_proj, w_conv_proj, b_conv_proj, w_out, g_ffn_norm, w_ffn_in, w_ffn_down, g_final, loss_target, m_g_mix_norm, m_w_in, m_b_in, m_sinks, m_conv_w, m_conv_b, m_ln_g, m_ln_b, m_w_attn_proj, m_w_conv_proj, m_b_conv_proj, m_w_out, m_g_ffn_norm, m_w_ffn_in, m_w_ffn_down, m_g_final, v_g_mix_norm, v_w_in, v_b_in, v_sinks, v_conv_w, v_conv_b, v_ln_g, v_ln_b, v_w_attn_proj, v_w_conv_proj, v_b_conv_proj, v_w_out, v_g_ffn_norm, v_w_ffn_in, v_w_ffn_down, v_g_final):
    w = dict(g_mix_norm=g_mix_norm, w_in=w_in, b_in=b_in, sinks=sinks, conv_w=conv_w, conv_b=conv_b,
             ln_g=ln_g, ln_b=ln_b, w_attn_proj=w_attn_proj, w_conv_proj=w_conv_proj,
             b_conv_proj=b_conv_proj, w_out=w_out, g_ffn_norm=g_ffn_norm, w_ffn_in=w_ffn_in,
             w_ffn_down=w_ffn_down, g_final=g_final)
    m = dict(g_mix_norm=m_g_mix_norm, w_in=m_w_in, b_in=m_b_in, sinks=m_sinks, conv_w=m_conv_w,
             conv_b=m_conv_b, ln_g=m_ln_g, ln_b=m_ln_b, w_attn_proj=m_w_attn_proj,
             w_conv_proj=m_w_conv_proj, b_conv_proj=m_b_conv_proj, w_out=m_w_out,
             g_ffn_norm=m_g_ffn_norm, w_ffn_in=m_w_ffn_in, w_ffn_down=m_w_ffn_down, g_final=m_g_final)
    v = dict(g_mix_norm=v_g_mix_norm, w_in=v_w_in, b_in=v_b_in, sinks=v_sinks, conv_w=v_conv_w,
             conv_b=v_conv_b, ln_g=v_ln_g, ln_b=v_ln_b, w_attn_proj=v_w_attn_proj,
             w_conv_proj=v_w_conv_proj, b_conv_proj=v_b_conv_proj, w_out=v_w_out,
             g_ffn_norm=v_g_ffn_norm, w_ffn_in=v_w_ffn_in, w_ffn_down=v_w_ffn_down, g_final=v_g_final)

    c_idx = lax.axis_index("c").astype(jnp.int32).reshape(1)
    s_idx = (2 * lax.axis_index("x") + lax.axis_index("y")).astype(jnp.int32).reshape(1)

    out_g, out_d, out_m, out_v = {}, {}, {}, {}

    def gather_start(tag, shards):
        lands = [lax.empty((N_CHIPS,) + s.shape, s.dtype) for s in shards]
        return _chip_start("gather_start_" + tag, _gather_ends, GATHER_PEERS, shards, lands)

    def gather_relay(tag, state, after, first=0, count=None):
        send_sems, recv_sems, shards, lands, _ = state
        last = len(shards) if count is None else first + count
        return _gather_relay("gather_relay_" + tag, send_sems, recv_sems, shards[first:last],
                             lands[first:last], after, first)

    def gather_finish(tag, relay, after):
        return _forward_wait("forward_wait_" + tag, relay[0], relay[1], relay[2], after)

    names_b = ["w_attn_proj", "w_conv_proj", "w_out", "w_ffn_in", "w_ffn_down"]
    big = {name: (w[name][0], m[name][0], v[name][0]) for name in names_b}
    big["w_in"] = (w_in[0].T, m_w_in[0].T, v_w_in[0].T)
    state_a = gather_start("a", [big["w_in"][0].astype(BF16), jnp.pad(conv_w[0], ((0, CW_ROWS - KW), (0, 0)))])
    state_b = gather_start("b", [(big[name][0] + state_a[4][0, 0]).astype(BF16) for name in names_b])
    got_a = gather_finish("a", gather_relay("a", state_a, [state_b[4]]), [])
    w_in_t_full = got_a[0].reshape(INW, D)
    conv_w_full = got_a[1].transpose(1, 0, 2).reshape(CW_ROWS, C)[:KW]

    xs, target = x[0], loss_target[0]
    g_final2 = g_final.reshape(1, D)
    h, qkv, glu, gl = _in_proj(xs, g_mix_norm, w_in_t_full, b_in)
    o, lse = _attn_fwd(qkv, sinks)
    relay_1 = gather_relay("b1", state_b, [o], 0, 3)
    u, cact = _conv_fwd(glu, conv_w_full, conv_b, ln_g, ln_b, relay_1[3])
    w_ap4, w_cp4, w_out4 = gather_finish("b1", relay_1, [cact])
    w_out_full = w_out4.reshape(D, D)
    relay_2 = gather_relay("b2", state_b, [cact], 3, 1)
    ya, yc, mg, x1 = _mix_out(xs, o, cact, gl, w_ap4, w_cp4, b_conv_proj, w_out_full, relay_2[3])
    w_fi4, = gather_finish("b2", relay_2, [x1])
    relay_3 = gather_relay("b3", state_b, [x1], 4, 1)
    h2, gu, act = _ffn_in(x1, g_ffn_norm, w_fi4, relay_3[3])
    w_dn4, = gather_finish("b3", relay_3, [act])
    w_dn_full = w_dn4.reshape(DFF, D)
    dx2, dx2b, dg_final, loss_part = _ffn_out_loss(x1, act, w_dn_full, g_final2, target)

    def exchange_start(tag, grads):
        lands = [lax.empty((N_CHIPS, g.shape[1] // 2, g.shape[2]), g.dtype) for g in grads]
        return _chip_start("pair_start_" + tag, _exchange_ends, PAIR_PEERS, grads, lands)

    def reduce_start(tag, names, exchange, after):
        send_sems, recv_sems, grads, lands, _ = exchange
        grads, from_sibling = _chip_wait("pair_wait_" + tag, _exchange_ends, send_sems, recv_sems, grads, lands, after)
        pair = [_pair_sum(c_idx, g, r, "pair_sum_" + name) for name, g, r in zip(names, grads, from_sibling)]
        lands = [lax.empty((3,) + p.shape[1:], p.dtype) for p in pair]
        return _chip_start("chip_start_" + tag, _reduce_ends, REDUCE_PEERS, pair, lands)

    def reduce_sum(tag, names, state, after):
        send_sems, recv_sems, pair, lands, _ = state
        pair, lands = _chip_wait("chip_wait_" + tag, _reduce_ends, send_sems, recv_sems, pair, lands, after)
        mine = [_chip_sum(s_idx, p, r, "chip_sum_" + name) for name, p, r in zip(names, pair, lands)]
        others = [lax.empty(a.shape, a.dtype) for a in mine]
        return _chip_start("share_start_" + tag, _share_ends, PAIR_PEERS, mine, others)

    def reduce_finish(tag, names, share, after):
        send_sems, recv_sems, mine, others, _ = share
        mine, others = _chip_wait("share_wait_" + tag, _share_ends, send_sems, recv_sems, mine, others, after)
        for name, g_mine, g_other in zip(names, mine, others):
            wv, mv, vv = big[name]
            res = _adamw(c_idx, wv, g_mine, g_other, mv, vv, "adamw_" + name)
            if name == "w_in":
                res = [a.T for a in res]
            out_g[name], out_d[name], out_m[name], out_v[name] = [a[None] for a in res]

    dgu, dx1, dx1b, dg_ffn = _ffn_bwd(dx2, dx2b, gu, x1, g_ffn_norm, w_dn_full, w_fi4)
    names_1 = ["w_ffn_in", "w_ffn_down"]
    exchange_1 = exchange_start("1", [_grad_w(h2, dgu, "grad_w_ffn_in", 512, FSH, True),
                                      _grad_w(act, dx2b, "grad_w_ffn_down", 256, D, False)])
    dya, dyc, dgl, do, dc, db_cp = _mix_bwd(dx1b, gl, ya, yc, w_out_full, w_ap4, w_cp4, exchange_1[4])
    state_1 = reduce_start("1", names_1, exchange_1, [dya])
    grads_2 = [_grad_w(mg, dx1b, "grad_w_out", 512, D, False),
               _grad_w(o, dya, "grad_w_attn_proj", 512, 256, True),
               _grad_w(cact, dyc, "grad_w_conv_proj", 512, 256, True)]
    dglu, dconv_w, dconv_b, dln_g, dln_b = _conv_bwd(glu, u, dc, conv_w_full, ln_g, ln_b, state_1[4])
    dq, dkv, dsinks = _attn_bwd(qkv, o, do, lse, sinks)
    names_2 = ["w_in", "w_out", "w_attn_proj", "w_conv_proj"]
    gw_in_t = _grad_w_in_t(h, dq, dkv, dglu, dgl)
    exchange_2 = exchange_start("2", [gw_in_t.reshape(N_CHIPS, INW // N_CHIPS, D)] + grads_2)
    grad_x, dg_mix, db_in = _in_proj_bwd(dq, dkv, dglu, dgl, xs, dx1, g_mix_norm, w_in_t_full, exchange_2[4])
    state_2 = reduce_start("2", names_2, exchange_2, [grad_x])
    share_1 = reduce_sum("1", names_1, state_1, [state_2[4]])

    gs = {"g_mix_norm": dg_mix, "b_in": db_in, "sinks": dsinks[:, 0].reshape(1, NQ),
          "conv_w": dconv_w, "conv_b": dconv_b, "ln_g": dln_g, "ln_b": dln_b,
          "b_conv_proj": db_cp, "g_ffn_norm": dg_ffn, "g_final": dg_final}
    blocks = list(_pack_small(gs, loss_part[0, 0]))
    tables = [lax.empty((8 * b.shape[0], b.shape[1]), b.dtype) for b in blocks]
    small = _chip_start("small_start", _small_ends, SMALL_PEERS, blocks, tables)
    reduce_finish("1", names_1, share_1, [small[4]])
    blocks, tables = _chip_wait("small_wait", _small_ends, small[0], small[1], small[2], small[3],
                                [out_d["w_ffn_down"]])
    me = 4 * lax.axis_index("x") + 2 * lax.axis_index("y") + lax.axis_index("c")
    vec_all, cw_all = [lax.dynamic_update_slice(t, b, (me * b.shape[0], 0)) for t, b in zip(tables, blocks)]
    share_2 = reduce_sum("2", names_2, state_2, [vec_all])

    def view(a, name):
        if name == "conv_w":
            return a[0]
        if name == "g_final":
            return a.reshape(1, D)
        return a

    wmv = [(view(w[name], name), view(m[name], name), view(v[name], name)) for name in SMALL_NAMES]
    small_out, loss_row = _small_update(s_idx, vec_all, cw_all, wmv)
    for name, (g, d, mm, vv) in zip(SMALL_NAMES, small_out):
        shape = w[name].shape
        out_g[name], out_d[name], out_m[name], out_v[name] = (
            g.reshape(shape), d.reshape(shape), mm.reshape(shape), vv.reshape(shape))

    reduce_finish("2", names_2, share_2, [loss_row])

    loss = loss_row[0, 0]
    return (loss, grad_x[None], *[out_g[k] for k in WEIGHT_ORDER], *[out_d[k] for k in WEIGHT_ORDER],
            *[out_m[k] for k in WEIGHT_ORDER], *[out_v[k] for k in WEIGHT_ORDER])
```

```python
import functools

import jax
import jax.numpy as jnp
from jax import lax
from jax.experimental import pallas as pl
from jax.experimental.pallas import tpu as pltpu

F32 = jnp.float32
BF16 = jnp.bfloat16

T = 2048
D = 1024
HD = 64
NQ = 8
NKV = 2
GROUP = NQ // NKV
BLK = 128
AW = NQ * HD
KVW = NKV * HD
C = 512
KW = 31
QKVW = AW + 2 * KVW
GLU_OFF = QKVW
GATE_OFF = GLU_OFF + 2 * C
INW = GATE_OFF + 2 * D
DFF = 2816
EPS = 1e-5
NEG = -1e30
SCALE = HD ** -0.5
HALO = 32
N_CHIPS = 4
FSH = 2 * DFF // N_CHIPS

ADAM_LR = 0.001
ADAM_B1 = 0.9
ADAM_B2 = 0.999
ADAM_EPS = 1e-08
ADAM_WD = 0.01
ADAM_STEP = 10

VMEM_LIMIT = 56 * 1024 * 1024
ROW_TM = 512
MESH = pl.DeviceIdType.MESH


def _params(*sem):
    return pltpu.CompilerParams(dimension_semantics=sem, vmem_limit_bytes=VMEM_LIMIT)


def _dot(a, b):
    return jnp.dot(a, b, preferred_element_type=F32)


def _dot_nt(a, b):
    return lax.dot_general(a, b, (((1,), (1,)), ((), ())), preferred_element_type=F32)


def _dot_tn(a, b):
    return lax.dot_general(a, b, (((0,), (0,)), ((), ())), preferred_element_type=F32)


def _sigmoid(v):
    return 1.0 / (1.0 + jnp.exp(-v))


def _rows(tm, n):
    return pl.BlockSpec((tm, n), lambda i: (i, 0))


def _whole(shape):
    return pl.BlockSpec(shape, lambda i: tuple(0 for _ in shape))


def _in_proj(x, g_mix, w_in_t, b_in):
    tm = ROW_TM

    def body(x_ref, g_ref, w_ref, b_ref, h_ref, qkv_ref, glu_ref, gl_ref):
        xv = x_ref[...]
        r = lax.rsqrt(jnp.mean(xv * xv, axis=-1, keepdims=True) + EPS)
        h = (xv * r * g_ref[...]).astype(BF16)
        h_ref[...] = h
        qkv_ref[...] = (_dot_nt(h, w_ref[0:GLU_OFF, :]) + b_ref[:, 0:GLU_OFF]).astype(BF16)
        glu_ref[...] = (_dot_nt(h, w_ref[GLU_OFF:GATE_OFF, :]) + b_ref[:, GLU_OFF:GATE_OFF]).astype(BF16)
        gl_ref[...] = (_dot_nt(h, w_ref[GATE_OFF:INW, :]) + b_ref[:, GATE_OFF:INW]).astype(BF16)

    return pl.pallas_call(
        body, name="in_proj", grid=(T // tm,),
        in_specs=[_rows(tm, D), _whole((1, D)), _whole((INW, D)), _whole((1, INW))],
        out_specs=[_rows(tm, D), _rows(tm, QKVW), _rows(tm, 2 * C), _rows(tm, 2 * D)],
        out_shape=[jax.ShapeDtypeStruct((T, D), BF16), jax.ShapeDtypeStruct((T, QKVW), BF16),
                   jax.ShapeDtypeStruct((T, 2 * C), BF16), jax.ShapeDtypeStruct((T, 2 * D), BF16)],
        compiler_params=_params("parallel"),
    )(x, g_mix, w_in_t, b_in)


GROWS = GROUP * BLK
BAND = 2 * BLK


def _band(i):
    rb = pl.multiple_of(jnp.maximum(i - 1, 0) * BLK, BLK)
    row = lax.broadcasted_iota(jnp.int32, (GROWS, BAND), 0)
    kpos = rb + lax.broadcasted_iota(jnp.int32, (GROWS, BAND), 1)
    qpos = i * BLK + jnp.bitwise_and(row, BLK - 1)
    return rb, jnp.logical_and(kpos <= qpos, kpos > qpos - BLK)


def _sink_column(sink_ref, g):
    head = lax.shift_right_logical(lax.broadcasted_iota(jnp.int32, (GROWS, 1), 0), 7)
    col = jnp.full((GROWS, 1), sink_ref[0, g * GROUP], F32)
    for hh in range(1, GROUP):
        col = jnp.where(head == hh, sink_ref[0, g * GROUP + hh], col)
    return col


def _attn_fwd(qkv, sinks):
    def body(sink_ref, qkv_ref, o_ref, lse_ref, s_ref, p_ref):
        i = pl.program_id(0)
        r0 = pl.multiple_of(i * BLK, BLK)
        rb, valid = _band(i)
        for g in range(NKV):
            kband = qkv_ref[pl.ds(rb, BAND), AW + g * HD:AW + (g + 1) * HD]
            vband = qkv_ref[pl.ds(rb, BAND), AW + KVW + g * HD:AW + KVW + (g + 1) * HD]
            for hh in range(GROUP):
                h = g * GROUP + hh
                s_ref[hh * BLK:(hh + 1) * BLK, :] = _dot_nt(qkv_ref[pl.ds(r0, BLK), h * HD:(h + 1) * HD], kband)
            s = jnp.where(valid, s_ref[...] * SCALE, NEG)
            sink = _sink_column(sink_ref, g)
            m = jnp.maximum(jnp.max(s, axis=-1, keepdims=True), sink)
            p = jnp.exp(s - m)
            den = jnp.sum(p, axis=-1, keepdims=True) + jnp.exp(sink - m)
            p_ref[...] = (p * (1.0 / den)).astype(BF16)
            lse = m + jnp.log(den)
            for hh in range(GROUP):
                h = g * GROUP + hh
                o_ref[:, h * HD:(h + 1) * HD] = _dot(p_ref[hh * BLK:(hh + 1) * BLK, :], vband).astype(BF16)
                lse_ref[:, h:h + 1] = lse[hh * BLK:(hh + 1) * BLK]

    return pl.pallas_call(
        body, name="attn_fwd", grid=(T // BLK,),
        in_specs=[pl.BlockSpec(memory_space=pltpu.SMEM), _whole((T, QKVW))],
        out_specs=[_rows(BLK, AW), _rows(BLK, NQ)],
        out_shape=[jax.ShapeDtypeStruct((T, AW), BF16), jax.ShapeDtypeStruct((T, NQ), F32)],
        scratch_shapes=[pltpu.VMEM((GROWS, BAND), F32), pltpu.VMEM((GROWS, BAND), BF16)],
        compiler_params=_params("parallel"),
    )(sinks, qkv)


CONV_TM = 256
CONV_SUB = 32


def _glu(ab):
    a = ab[:, 0:C].astype(F32)
    b = ab[:, C:2 * C].astype(F32)
    return a * _sigmoid(b)


SUBLANES = 8


def _shifted_copies(ref):
    rows = ref.shape[1] - SUBLANES
    for r in range(1, SUBLANES):
        ref[r, 0:rows, :] = ref[0, r:r + rows, :]


def _shifted_rows(ref, start, size):
    r = start % SUBLANES
    return ref[r, start - r:start - r + size, :]


def _conv_fwd(glu, conv_w, conv_b, ln_g, ln_b, dep):
    tm = CONV_TM

    def body(cur_ref, prev_ref, w_ref, cb_ref, g_ref, b_ref, dep_ref, u_ref, c_ref, zs_ref):
        i = pl.program_id(0)
        zprev = _glu(prev_ref[tm - HALO:tm, :])
        zs_ref[0, 0:HALO, :] = jnp.where(i > 0, zprev, 0.0)
        zs_ref[0, HALO:HALO + tm, :] = _glu(cur_ref[...])
        _shifted_copies(zs_ref)
        for s in range(tm // CONV_SUB):
            base = HALO + s * CONV_SUB - (KW - 1)
            acc = jnp.broadcast_to(cb_ref[...], (CONV_SUB, C))
            for j in range(KW):
                acc = acc + w_ref[j:j + 1, :] * _shifted_rows(zs_ref, base + j, CONV_SUB)
            rows = slice(s * CONV_SUB, (s + 1) * CONV_SUB)
            u_ref[rows, :] = acc
            mu = jnp.mean(acc, axis=-1, keepdims=True)
            xc = acc - mu
            var = jnp.mean(xc * xc, axis=-1, keepdims=True)
            y = xc * lax.rsqrt(var + EPS) * g_ref[...] + b_ref[...]
            c_ref[rows, :] = (y * _sigmoid(y)).astype(BF16)

    return pl.pallas_call(
        body, name="conv_fwd", grid=(T // tm,),
        in_specs=[_rows(tm, 2 * C),
                  pl.BlockSpec((tm, 2 * C), lambda i: (jnp.maximum(i - 1, 0), 0)),
                  _whole((KW, C)), _whole((1, C)), _whole((1, C)), _whole((1, C)), _whole((8, 128))],
        out_specs=[_rows(tm, C), _rows(tm, C)],
        out_shape=[jax.ShapeDtypeStruct((T, C), F32), jax.ShapeDtypeStruct((T, C), BF16)],
        scratch_shapes=[pltpu.VMEM((SUBLANES, HALO + tm, C), F32)],
        compiler_params=_params("parallel"),
    )(glu, glu, conv_w, conv_b, ln_g, ln_b, dep)


def _mix_out(x, o, cact, gl, w_ap, w_cp, b_cp, w_out, dep):
    tm = ROW_TM

    def body(x_ref, o_ref, c_ref, gl_ref, wap_ref, wcp_ref, bcp_ref, wo_ref, dep_ref,
             ya_ref, yc_ref, mg_ref, x1_ref):
        ov, cv = o_ref[...], c_ref[...]
        ya = jnp.concatenate([_dot(ov, wap_ref[s]) for s in range(N_CHIPS)], axis=1)
        yc = jnp.concatenate([_dot(cv, wcp_ref[s]) for s in range(N_CHIPS)], axis=1) + bcp_ref[...]
        g0 = _sigmoid(gl_ref[:, 0:D].astype(F32))
        g1 = _sigmoid(gl_ref[:, D:2 * D].astype(F32))
        mg = (g0 * ya + g1 * yc).astype(BF16)
        ya_ref[...] = ya.astype(BF16)
        yc_ref[...] = yc.astype(BF16)
        mg_ref[...] = mg
        x1_ref[...] = x_ref[...] + _dot(mg, wo_ref[...])

    return pl.pallas_call(
        body, name="mix_out", grid=(T // tm,),
        in_specs=[_rows(tm, D), _rows(tm, AW), _rows(tm, C), _rows(tm, 2 * D),
                  _whole((N_CHIPS, AW, D // N_CHIPS)), _whole((N_CHIPS, C, D // N_CHIPS)), _whole((1, D)),
                  _whole((D, D)), _whole((8, 128))],
        out_specs=[_rows(tm, D), _rows(tm, D), _rows(tm, D), _rows(tm, D)],
        out_shape=[jax.ShapeDtypeStruct((T, D), BF16), jax.ShapeDtypeStruct((T, D), BF16),
                   jax.ShapeDtypeStruct((T, D), BF16), jax.ShapeDtypeStruct((T, D), F32)],
        compiler_params=_params("parallel"),
    )(x, o, cact, gl, w_ap, w_cp, b_cp, w_out, dep)


def _ffn_in(x1, g_ffn, w_fi, dep):
    tm = ROW_TM

    def body(x_ref, g_ref, w_ref, dep_ref, h_ref, gu_ref, act_ref):
        xv = x_ref[...]
        r = lax.rsqrt(jnp.mean(xv * xv, axis=-1, keepdims=True) + EPS)
        h = (xv * r * g_ref[...]).astype(BF16)
        h_ref[...] = h
        for s in range(N_CHIPS // 2):
            c0 = s * FSH
            gate = _dot(h, w_ref[s])
            up = _dot(h, w_ref[s + N_CHIPS // 2])
            gu_ref[:, c0:c0 + FSH] = gate.astype(BF16)
            gu_ref[:, DFF + c0:DFF + c0 + FSH] = up.astype(BF16)
            act_ref[:, c0:c0 + FSH] = (gate * _sigmoid(gate) * up).astype(BF16)

    return pl.pallas_call(
        body, name="ffn_in", grid=(T // tm,),
        in_specs=[_rows(tm, D), _whole((1, D)), _const((N_CHIPS, D, FSH)), _whole((8, 128))],
        out_specs=[_rows(tm, D), _rows(tm, 2 * DFF), _rows(tm, DFF)],
        out_shape=[jax.ShapeDtypeStruct((T, D), BF16), jax.ShapeDtypeStruct((T, 2 * DFF), BF16),
                   jax.ShapeDtypeStruct((T, DFF), BF16)],
        compiler_params=_params("parallel"),
    )(x1, g_ffn, w_fi, dep)


def _ffn_out_loss(x1, act, w_dn, g_final, target):
    tm = ROW_TM

    def body(x_ref, a_ref, w_ref, g_ref, t_ref, dx_ref, dxb_ref, dg_ref, loss_ref):
        i = pl.program_id(0)
        x2 = x_ref[...] + _dot(a_ref[...], w_ref[...])
        r = lax.rsqrt(jnp.mean(x2 * x2, axis=-1, keepdims=True) + EPS)
        xh = x2 * r
        g = g_ref[...]
        err = xh * g - t_ref[...]
        dy = err * (1.0 / D)
        dyg = dy * g
        dx = r * (dyg - xh * jnp.mean(dyg * xh, axis=-1, keepdims=True))
        dx_ref[...] = dx
        dxb_ref[...] = dx.astype(BF16)
        part = 0.5 * jnp.sum(jnp.mean(err * err, axis=-1, keepdims=True), axis=0, keepdims=True)

        @pl.when(i == 0)
        def _():
            dg_ref[...] = jnp.zeros_like(dg_ref)
            loss_ref[...] = jnp.zeros_like(loss_ref)

        dg_ref[...] += jnp.sum(dy * xh, axis=0, keepdims=True)
        loss_ref[...] += jnp.broadcast_to(part, loss_ref.shape)

    return pl.pallas_call(
        body, name="ffn_out_loss", grid=(T // tm,),
        in_specs=[_rows(tm, D), _rows(tm, DFF), _whole((DFF, D)), _whole((1, D)), _rows(tm, D)],
        out_specs=[_rows(tm, D), _rows(tm, D), _whole((1, D)), _whole((1, 128))],
        out_shape=[jax.ShapeDtypeStruct((T, D), F32), jax.ShapeDtypeStruct((T, D), BF16),
                   jax.ShapeDtypeStruct((1, D), F32), jax.ShapeDtypeStruct((1, 128), F32)],
        compiler_params=_params("arbitrary"),
    )(x1, act, w_dn, g_final, target)


def _const(shape):
    return pl.BlockSpec(shape, lambda i: tuple(0 for _ in shape), pipeline_mode=pl.Buffered(1))


def _ffn_bwd(dx2, dx2b, gu, x1, g_ffn, w_dn_t, w_fi_t):
    tm = ROW_TM // 2

    def body(dx_ref, dxb_ref, gu_ref, x_ref, g_ref, wdn_ref, wfi_ref,
             dgu_ref, dx1_ref, dx1b_ref, dg_ref):
        i = pl.program_id(0)
        dxb = dxb_ref[...]
        dh = jnp.zeros((tm, D), F32)
        for k in range(N_CHIPS // 2):
            c0 = k * FSH
            dact = _dot_nt(dxb, wdn_ref[c0:c0 + FSH, :])
            gate = gu_ref[:, c0:c0 + FSH].astype(F32)
            up = gu_ref[:, DFF + c0:DFF + c0 + FSH].astype(F32)
            s = _sigmoid(gate)
            dup = (dact * gate * s).astype(BF16)
            dgate = (dact * up * s * (1.0 + gate * (1.0 - s))).astype(BF16)
            dgu_ref[:, c0:c0 + FSH] = dgate
            dgu_ref[:, DFF + c0:DFF + c0 + FSH] = dup
            dh = dh + _dot_nt(dgate, wfi_ref[k]) + _dot_nt(dup, wfi_ref[k + N_CHIPS // 2])
        xv = x_ref[...]
        r = lax.rsqrt(jnp.mean(xv * xv, axis=-1, keepdims=True) + EPS)
        xh = xv * r
        dhg = dh * g_ref[...]
        dx1 = dx_ref[...] + r * (dhg - xh * jnp.mean(dhg * xh, axis=-1, keepdims=True))
        dx1_ref[...] = dx1
        dx1b_ref[...] = dx1.astype(BF16)

        @pl.when(i == 0)
        def _():
            dg_ref[...] = jnp.zeros_like(dg_ref)

        dg_ref[...] += jnp.sum(dh * xh, axis=0, keepdims=True)

    return pl.pallas_call(
        body, name="ffn_bwd", grid=(T // tm,),
        in_specs=[_rows(tm, D), _rows(tm, D), _rows(tm, 2 * DFF), _rows(tm, D), _whole((1, D)),
                  _const((DFF, D)), _const((N_CHIPS, D, FSH))],
        out_specs=[_rows(tm, 2 * DFF), _rows(tm, D), _rows(tm, D), _whole((1, D))],
        out_shape=[jax.ShapeDtypeStruct((T, 2 * DFF), BF16), jax.ShapeDtypeStruct((T, D), F32),
                   jax.ShapeDtypeStruct((T, D), BF16), jax.ShapeDtypeStruct((1, D), F32)],
        compiler_params=_params("arbitrary"),
    )(dx2, dx2b, gu, x1, g_ffn, w_dn_t, w_fi_t)


def _mix_bwd(dx1b, gl, ya, yc, w_out_t, w_ap_t, w_cp_t, dep):
    tm = ROW_TM

    def body(dx_ref, gl_ref, ya_ref, yc_ref, wo_ref, wap_ref, wcp_ref, dep_ref,
             dya_ref, dyc_ref, dgl_ref, do_ref, dc_ref, db_ref):
        i = pl.program_id(0)
        dm = _dot_nt(dx_ref[...], wo_ref[...])
        g0 = _sigmoid(gl_ref[:, 0:D].astype(F32))
        g1 = _sigmoid(gl_ref[:, D:2 * D].astype(F32))
        dya = dm * g0
        dyc = dm * g1
        dgl_ref[:, 0:D] = (dya * ya_ref[...].astype(F32) * (1.0 - g0)).astype(BF16)
        dgl_ref[:, D:2 * D] = (dyc * yc_ref[...].astype(F32) * (1.0 - g1)).astype(BF16)
        dyab = dya.astype(BF16)
        dycb = dyc.astype(BF16)
        dya_ref[...] = dyab
        dyc_ref[...] = dycb
        sw = D // N_CHIPS
        do = jnp.zeros((tm, AW), F32)
        dcv = jnp.zeros((tm, C), F32)
        for s in range(N_CHIPS):
            do = do + _dot_nt(dyab[:, s * sw:(s + 1) * sw], wap_ref[s])
            dcv = dcv + _dot_nt(dycb[:, s * sw:(s + 1) * sw], wcp_ref[s])
        do_ref[...] = do.astype(BF16)
        dc_ref[...] = dcv

        @pl.when(i == 0)
        def _():
            db_ref[...] = jnp.zeros_like(db_ref)

        db_ref[...] += jnp.sum(dyc, axis=0, keepdims=True)

    return pl.pallas_call(
        body, name="mix_bwd", grid=(T // tm,),
        in_specs=[_rows(tm, D), _rows(tm, 2 * D), _rows(tm, D), _rows(tm, D),
                  _whole((D, D)), _whole((N_CHIPS, AW, D // N_CHIPS)), _whole((N_CHIPS, C, D // N_CHIPS)),
                  _whole((8, 128))],
        out_specs=[_rows(tm, D), _rows(tm, D), _rows(tm, 2 * D), _rows(tm, AW), _rows(tm, C),
                   _whole((1, D))],
        out_shape=[jax.ShapeDtypeStruct((T, D), BF16), jax.ShapeDtypeStruct((T, D), BF16),
                   jax.ShapeDtypeStruct((T, 2 * D), BF16), jax.ShapeDtypeStruct((T, AW), BF16),
                   jax.ShapeDtypeStruct((T, C), F32), jax.ShapeDtypeStruct((1, D), F32)],
        compiler_params=_params("arbitrary"),
    )(dx1b, gl, ya, yc, w_out_t, w_ap_t, w_cp_t, dep)


def _conv_bwd(glu, u, dc, conv_w, ln_g, ln_b, dep):
    tm = CONV_TM
    nblk = T // tm

    def du_of(uv, dcv, g_ref, b_ref):
        mu = jnp.mean(uv, axis=-1, keepdims=True)
        xc = uv - mu
        var = jnp.mean(xc * xc, axis=-1, keepdims=True)
        rstd = lax.rsqrt(var + EPS)
        xh = xc * rstd
        y = xh * g_ref[...] + b_ref[...]
        sg = _sigmoid(y)
        dy = dcv * (sg * (1.0 + y * (1.0 - sg)))
        dxh = dy * g_ref[...]
        du = rstd * (dxh - jnp.mean(dxh, axis=-1, keepdims=True)
                     - xh * jnp.mean(dxh * xh, axis=-1, keepdims=True))
        return du, dy, xh

    def body(cur_ref, prev_ref, u_ref, un_ref, dc_ref, dcn_ref, w_ref, g_ref, b_ref, dep_ref,
             dglu_ref, dw_ref, dcb_ref, dg_ref, db_ref, zs_ref, dus_ref):
        i = pl.program_id(0)

        @pl.when(i == 0)
        def _():
            dw_ref[...] = jnp.zeros_like(dw_ref)
            dcb_ref[...] = jnp.zeros_like(dcb_ref)
            dg_ref[...] = jnp.zeros_like(dg_ref)
            db_ref[...] = jnp.zeros_like(db_ref)

        zprev = _glu(prev_ref[tm - HALO:tm, :])
        zs_ref[0, 0:HALO, :] = jnp.where(i > 0, zprev, 0.0)
        zs_ref[0, HALO:HALO + tm, :] = _glu(cur_ref[...])
        _shifted_copies(zs_ref)

        dun, _, _ = du_of(un_ref[0:HALO, :], dcn_ref[0:HALO, :], g_ref, b_ref)
        dus_ref[0, tm:tm + HALO, :] = jnp.where(i < nblk - 1, dun, 0.0)
        dg_acc = jnp.zeros((1, C), F32)
        db_acc = jnp.zeros((1, C), F32)
        dcb_acc = jnp.zeros((1, C), F32)
        for s in range(tm // CONV_SUB):
            rows = slice(s * CONV_SUB, (s + 1) * CONV_SUB)
            du, dy, xh = du_of(u_ref[rows, :], dc_ref[rows, :], g_ref, b_ref)
            dus_ref[0, rows, :] = du
            dg_acc = dg_acc + jnp.sum(dy * xh, axis=0, keepdims=True)
            db_acc = db_acc + jnp.sum(dy, axis=0, keepdims=True)
            dcb_acc = dcb_acc + jnp.sum(du, axis=0, keepdims=True)
        dg_ref[...] += dg_acc
        db_ref[...] += db_acc
        dcb_ref[...] += dcb_acc
        _shifted_copies(dus_ref)

        for j in range(KW):
            acc = jnp.zeros((CONV_SUB, C), F32)
            for s in range(tm // CONV_SUB):
                base = HALO + s * CONV_SUB - (KW - 1) + j
                acc = acc + dus_ref[0, s * CONV_SUB:(s + 1) * CONV_SUB, :] * _shifted_rows(zs_ref, base, CONV_SUB)
            dw_ref[j:j + 1, :] += jnp.sum(acc, axis=0, keepdims=True)

        for s in range(tm // CONV_SUB):
            rows = slice(s * CONV_SUB, (s + 1) * CONV_SUB)
            dz = jnp.zeros((CONV_SUB, C), F32)
            for j in range(KW):
                dz = dz + w_ref[j:j + 1, :] * _shifted_rows(dus_ref, s * CONV_SUB + (KW - 1) - j, CONV_SUB)
            a = cur_ref[rows, 0:C].astype(F32)
            sb = _sigmoid(cur_ref[rows, C:2 * C].astype(F32))
            dglu_ref[rows, 0:C] = (dz * sb).astype(BF16)
            dglu_ref[rows, C:2 * C] = (dz * a * sb * (1.0 - sb)).astype(BF16)

    nxt = lambda i: (jnp.minimum(i + 1, nblk - 1), 0)
    return pl.pallas_call(
        body, name="conv_bwd", grid=(nblk,),
        in_specs=[_rows(tm, 2 * C),
                  pl.BlockSpec((tm, 2 * C), lambda i: (jnp.maximum(i - 1, 0), 0)),
                  _rows(tm, C), pl.BlockSpec((tm, C), nxt),
                  _rows(tm, C), pl.BlockSpec((tm, C), nxt),
                  _whole((KW, C)), _whole((1, C)), _whole((1, C)), _whole((8, 128))],
        out_specs=[_rows(tm, 2 * C), _whole((KW, C)), _whole((1, C)), _whole((1, C)), _whole((1, C))],
        out_shape=[jax.ShapeDtypeStruct((T, 2 * C), BF16), jax.ShapeDtypeStruct((KW, C), F32),
                   jax.ShapeDtypeStruct((1, C), F32), jax.ShapeDtypeStruct((1, C), F32),
                   jax.ShapeDtypeStruct((1, C), F32)],
        scratch_shapes=[pltpu.VMEM((SUBLANES, HALO + tm, C), F32), pltpu.VMEM((SUBLANES, tm + HALO, C), F32)],
        compiler_params=_params("arbitrary"),
    )(glu, glu, u, u, dc, dc, conv_w, ln_g, ln_b, dep)


def _attn_bwd(qkv, o, do, lse, sinks):
    def body(sink_ref, qkv_ref, o_ref, do_ref, lse_ref, dq_ref, dkv_ref, ds_ref,
             s_ref, dp_ref, p_ref, dsb_ref):
        i = pl.program_id(0)

        @pl.when(i == 0)
        def _():
            dkv_ref[...] = jnp.zeros_like(dkv_ref)
            ds_ref[...] = jnp.zeros_like(ds_ref)

        r0 = pl.multiple_of(i * BLK, BLK)
        rb, valid = _band(i)
        for g in range(NKV):
            kband = qkv_ref[pl.ds(rb, BAND), AW + g * HD:AW + (g + 1) * HD]
            vband = qkv_ref[pl.ds(rb, BAND), AW + KVW + g * HD:AW + KVW + (g + 1) * HD]
            lse_parts, dl_parts = [], []
            for hh in range(GROUP):
                h = g * GROUP + hh
                hcol = slice(h * HD, (h + 1) * HD)
                doh = do_ref[:, hcol]
                s_ref[hh * BLK:(hh + 1) * BLK, :] = _dot_nt(qkv_ref[pl.ds(r0, BLK), hcol], kband)
                dp_ref[hh * BLK:(hh + 1) * BLK, :] = _dot_nt(doh, vband)
                lse_parts.append(lse_ref[:, h:h + 1])
                dl_parts.append(jnp.sum(doh.astype(F32) * o_ref[:, hcol].astype(F32), axis=-1, keepdims=True))
            lse = jnp.concatenate(lse_parts, axis=0)
            dl = jnp.concatenate(dl_parts, axis=0)
            p = jnp.where(valid, jnp.exp(s_ref[...] * SCALE - lse), 0.0)
            p_ref[...] = p.astype(BF16)
            dsb_ref[...] = (p * (dp_ref[...] - dl)).astype(BF16)
            dsink = -(jnp.exp(_sink_column(sink_ref, g) - lse) * dl)
            dk = jnp.zeros((BAND, HD), F32)
            dv = jnp.zeros((BAND, HD), F32)
            for hh in range(GROUP):
                h = g * GROUP + hh
                hcol = slice(h * HD, (h + 1) * HD)
                rows = slice(hh * BLK, (hh + 1) * BLK)
                dq_ref[:, hcol] = (_dot(dsb_ref[rows, :], kband) * SCALE).astype(BF16)
                dk = dk + _dot_tn(dsb_ref[rows, :], qkv_ref[pl.ds(r0, BLK), hcol])
                dv = dv + _dot_tn(p_ref[rows, :], do_ref[:, hcol])
                ds_ref[h:h + 1, :] += jnp.broadcast_to(jnp.sum(dsink[rows], axis=0, keepdims=True), (1, 128))
            dkv_ref[pl.ds(rb, BAND), g * HD:(g + 1) * HD] += dk * SCALE
            dkv_ref[pl.ds(rb, BAND), KVW + g * HD:KVW + (g + 1) * HD] += dv

    return pl.pallas_call(
        body, name="attn_bwd", grid=(T // BLK,),
        in_specs=[pl.BlockSpec(memory_space=pltpu.SMEM), _whole((T, QKVW)),
                  _rows(BLK, AW), _rows(BLK, AW), _rows(BLK, NQ)],
        out_specs=[_rows(BLK, AW), _whole((T, 2 * KVW)), _whole((NQ, 128))],
        out_shape=[jax.ShapeDtypeStruct((T, AW), BF16), jax.ShapeDtypeStruct((T, 2 * KVW), F32),
                   jax.ShapeDtypeStruct((NQ, 128), F32)],
        scratch_shapes=[pltpu.VMEM((GROWS, BAND), F32), pltpu.VMEM((GROWS, BAND), F32),
                        pltpu.VMEM((GROWS, BAND), BF16), pltpu.VMEM((GROWS, BAND), BF16)],
        compiler_params=_params("arbitrary"),
    )(sinks, qkv, o, do, lse)


PROJ_PARTS = [(0, AW), (AW, QKVW), (GLU_OFF, GATE_OFF), (GATE_OFF, INW)]


def _in_proj_bwd(dq, dkv, dglu, dgl, x, dx1, g_mix, w_in_t, dep):
    tm = ROW_TM

    def body(dq_ref, dkv_ref, dglu_ref, dgl_ref, x_ref, dx1_ref, g_ref, w_ref, dep_ref,
             gx_ref, dg_ref, db_ref):
        i = pl.program_id(0)

        @pl.when(i == 0)
        def _():
            dg_ref[...] = jnp.zeros_like(dg_ref)
            db_ref[...] = jnp.zeros_like(db_ref)

        dh = jnp.zeros((tm, D), F32)
        for part_ref, (lo, hi) in zip((dq_ref, dkv_ref, dglu_ref, dgl_ref), PROJ_PARTS):
            part = part_ref[...]
            dh = dh + _dot(part.astype(BF16), w_ref[lo:hi, :])
            db_ref[:, lo:hi] += jnp.sum(part.astype(F32), axis=0, keepdims=True)
        xv = x_ref[...]
        r = lax.rsqrt(jnp.mean(xv * xv, axis=-1, keepdims=True) + EPS)
        xh = xv * r
        dhg = dh * g_ref[...]
        gx_ref[...] = dx1_ref[...] + r * (dhg - xh * jnp.mean(dhg * xh, axis=-1, keepdims=True))
        dg_ref[...] += jnp.sum(dh * xh, axis=0, keepdims=True)

    return pl.pallas_call(
        body, name="in_proj_bwd", grid=(T // tm,),
        in_specs=[_rows(tm, AW), _rows(tm, 2 * KVW), _rows(tm, 2 * C), _rows(tm, 2 * D),
                  _rows(tm, D), _rows(tm, D), _whole((1, D)), _const((INW, D)), _whole((8, 128))],
        out_specs=[_rows(tm, D), _whole((1, D)), _whole((1, INW))],
        out_shape=[jax.ShapeDtypeStruct((T, D), F32), jax.ShapeDtypeStruct((1, D), F32),
                   jax.ShapeDtypeStruct((1, INW), F32)],
        compiler_params=_params("arbitrary"),
    )(dq, dkv, dglu, dgl, x, dx1, g_mix, w_in_t, dep)


def _grad_w_in_t(h, dq, dkv, dglu, dgl):
    tn, chunk = 512, 256

    def body(h_ref, dq_ref, dkv_ref, dglu_ref, dgl_ref, o_ref, pt_ref):
        @pl.when(pl.program_id(0) == 0)
        def _():
            for part_ref, (lo, hi) in zip((dq_ref, dkv_ref, dglu_ref, dgl_ref), PROJ_PARTS):
                for c0 in range(0, hi - lo, chunk):
                    pt_ref[lo + c0:lo + c0 + chunk, :] = part_ref[:, c0:c0 + chunk].astype(BF16).T

        hv = h_ref[...]
        for r0 in range(0, INW, QKVW):
            o_ref[r0:r0 + QKVW, :] = _dot(pt_ref[r0:r0 + QKVW, :], hv).astype(BF16)

    return pl.pallas_call(
        body, name="grad_w_in", grid=(D // tn,),
        in_specs=[pl.BlockSpec((T, tn), lambda j: (0, j)), _const((T, AW)), _const((T, 2 * KVW)),
                  _const((T, 2 * C)), _const((T, 2 * D))],
        out_specs=pl.BlockSpec((INW, tn), lambda j: (0, j)),
        out_shape=jax.ShapeDtypeStruct((INW, D), BF16),
        scratch_shapes=[pltpu.VMEM((INW, T), BF16)],
        compiler_params=_params("arbitrary"),
    )(h, dq, dkv, dglu, dgl)


def _grad_w(a, b, name, tk, tn, col_sharded):
    k, n = a.shape[1], b.shape[1]

    def body(a_ref, b_ref, o_ref, at_ref):
        @pl.when(pl.program_id(1) == 0)
        def _():
            at_ref[...] = a_ref[...].T

        o_ref[...] = _dot(at_ref[...], b_ref[...]).astype(BF16)

    if col_sharded:
        per = n // N_CHIPS // tn
        shape = (N_CHIPS, k, n // N_CHIPS)
        out_map = lambda i, j: (j // per, i, j % per)
    else:
        shape = (1, k, n)
        out_map = lambda i, j: (0, i, j)
    out = pl.pallas_call(
        body, name=name, grid=(k // tk, n // tn),
        in_specs=[pl.BlockSpec((T, tk), lambda i, j: (0, i)), pl.BlockSpec((T, tn), lambda i, j: (0, j))],
        out_specs=pl.BlockSpec((None, tk, tn), out_map),
        out_shape=jax.ShapeDtypeStruct(shape, BF16),
        scratch_shapes=[pltpu.VMEM((tk, T), BF16)],
        compiler_params=_params("parallel", "arbitrary"),
    )(a, b)
    return out if col_sharded else out.reshape(N_CHIPS, k // N_CHIPS, n)


HBM_SPEC = pl.BlockSpec(memory_space=pltpu.HBM)


def _place():
    x, y, c = lax.axis_index("x"), lax.axis_index("y"), lax.axis_index("c")
    chips = [(1 - x, y), (x, 1 - y), (1 - x, 1 - y)]
    return x, y, c, chips


SEM_SPEC = pl.BlockSpec(memory_space=pltpu.SEMAPHORE)
ANY_SPEC = pl.BlockSpec(memory_space=pl.ANY)
VMEM_SPEC = pl.BlockSpec(memory_space=pltpu.VMEM)
EFFECT = pltpu.SideEffectType.DATAFLOW_SIDE_EFFECTING


def _gather_ends(src, land, x, y, c, chips):
    kh = src.shape[0] // 2
    s_me = 2 * x + y
    ends = [(src.at[pl.ds(c * kh, kh)], land.at[s_me, pl.ds(c * kh, kh)], (*chip, c)) for chip in chips]
    return ends + [(src, land.at[s_me], (x, y, 1 - c))]


def _reduce_ends(src, land, x, y, c, chips):
    return [(src.at[2 * chip[0] + chip[1]], land.at[j], (*chip, c)) for j, chip in enumerate(chips)]


def _chip_copies(ends, srcs, lands, send_sems, recv_sems, first=0):
    x, y, c, chips = _place()
    copies = []
    for src, land in zip(srcs, lands):
        peers = ends(src, land, x, y, c, chips)
        for s, d, to in peers:
            k = first * len(peers) + len(copies)
            copies.append(pltpu.make_async_remote_copy(
                src_ref=s, dst_ref=d, send_sem=send_sems.at[k], recv_sem=recv_sems.at[k],
                device_id=to, device_id_type=MESH))
    return copies


GATHER_PEERS, REDUCE_PEERS = 4, 3


def _chip_start(name, ends, peers, srcs, lands):
    n = len(srcs)

    def body(*refs):
        copies = _chip_copies(ends, refs[:n], refs[n:2 * n], refs[2 * n], refs[2 * n + 1])
        for cp in copies:
            cp.start()
        token = refs[-1]
        token[...] = jnp.zeros_like(token)

    hbm = lambda a: pltpu.HBM(a.shape, a.dtype)
    res = pl.pallas_call(
        body, name=name,
        out_shape=(pltpu.SemaphoreType.DMA((peers * n,)), pltpu.SemaphoreType.DMA((peers * n,)),
                   *[hbm(a) for a in srcs], *[hbm(a) for a in lands],
                   jax.ShapeDtypeStruct((8, 128), F32)),
        in_specs=[HBM_SPEC] * (2 * n),
        out_specs=(SEM_SPEC, SEM_SPEC, *[HBM_SPEC] * (2 * n), VMEM_SPEC),
        input_output_aliases={i: 2 + i for i in range(2 * n)},
        compiler_params=pltpu.CompilerParams(has_side_effects=EFFECT),
    )(*[pltpu.with_memory_space_constraint(a, pltpu.HBM) for a in (*srcs, *lands)])
    return res[0], res[1], list(res[2:2 + n]), list(res[2 + n:2 + 2 * n]), res[-1]


def _chip_wait(name, ends, send_sems, recv_sems, srcs, lands, after, first=0):
    n, na = len(srcs), len(after)

    def body(*refs):
        copies = _chip_copies(ends, refs[:n], refs[n:2 * n], refs[2 * n], refs[2 * n + 1], first)
        for cp in copies:
            cp.wait_send()
            cp.wait_recv()

    hbm = lambda a: pltpu.HBM(a.shape, a.dtype)
    res = pl.pallas_call(
        body, name=name,
        out_shape=tuple(hbm(a) for a in (*srcs, *lands)),
        in_specs=[HBM_SPEC] * (2 * n) + [SEM_SPEC, SEM_SPEC] + [ANY_SPEC] * na,
        out_specs=tuple([HBM_SPEC] * (2 * n)),
        input_output_aliases={i: i for i in range(2 * n)},
        compiler_params=pltpu.CompilerParams(has_side_effects=EFFECT),
    )(*srcs, *lands, send_sems, recv_sems, *after)
    return list(res[:n]), list(res[n:])


def _forward_copies(lands, send_sems, recv_sems):
    x, y, c, chips = _place()
    copies = []
    for land in lands:
        kh = land.shape[1] // 2
        for chip in chips:
            blk = land.at[2 * chip[0] + chip[1], pl.ds(c * kh, kh)]
            k = len(copies)
            copies.append(pltpu.make_async_remote_copy(
                src_ref=blk, dst_ref=blk, send_sem=send_sems.at[k], recv_sem=recv_sems.at[k],
                device_id=(x, y, 1 - c), device_id_type=MESH))
    return copies


def _gather_relay(name, send_sems, recv_sems, srcs, lands, after, first):
    n, na = len(srcs), len(after)

    def body(*refs):
        land_refs = refs[n:2 * n]
        for cp in _chip_copies(_gather_ends, refs[:n], land_refs, refs[2 * n], refs[2 * n + 1], first):
            cp.wait_send()
            cp.wait_recv()
        out = refs[2 * n + 2 + na:]
        for cp in _forward_copies(land_refs, out[0], out[1]):
            cp.start()
        out[-1][...] = jnp.zeros_like(out[-1])

    hbm = lambda a: pltpu.HBM(a.shape, a.dtype)
    res = pl.pallas_call(
        body, name=name,
        out_shape=(pltpu.SemaphoreType.DMA((3 * n,)), pltpu.SemaphoreType.DMA((3 * n,)),
                   *[hbm(a) for a in lands], jax.ShapeDtypeStruct((8, 128), F32)),
        in_specs=[HBM_SPEC] * (2 * n) + [SEM_SPEC, SEM_SPEC] + [ANY_SPEC] * na,
        out_specs=(SEM_SPEC, SEM_SPEC, *[HBM_SPEC] * n, VMEM_SPEC),
        input_output_aliases={n + i: 2 + i for i in range(n)},
        compiler_params=pltpu.CompilerParams(has_side_effects=EFFECT),
    )(*srcs, *lands, send_sems, recv_sems, *after)
    return res[0], res[1], list(res[2:2 + n]), res[-1]


def _forward_wait(name, send_sems, recv_sems, lands, after):
    n, na = len(lands), len(after)

    def body(*refs):
        for cp in _forward_copies(refs[:n], refs[n], refs[n + 1]):
            cp.wait_send()
            cp.wait_recv()

    hbm = lambda a: pltpu.HBM(a.shape, a.dtype)
    res = pl.pallas_call(
        body, name=name,
        out_shape=tuple(hbm(a) for a in lands),
        in_specs=[HBM_SPEC] * n + [SEM_SPEC, SEM_SPEC] + [ANY_SPEC] * na,
        out_specs=tuple([HBM_SPEC] * n),
        input_output_aliases={i: i for i in range(n)},
        compiler_params=pltpu.CompilerParams(has_side_effects=EFFECT),
    )(*lands, send_sems, recv_sems, *after)
    return list(res)


def _exchange_ends(src, land, x, y, c, chips):
    kh = src.shape[1] // 2
    return [(src.at[:, pl.ds((1 - c) * kh, kh)], land, (x, y, 1 - c))]


def _share_ends(src, land, x, y, c, chips):
    return [(src, land, (x, y, 1 - c))]


def _small_ends(src, land, x, y, c, chips):
    m = src.shape[0]
    rows = land.at[pl.ds((4 * x + 2 * y + c) * m, m)]
    peers = [(x, y, 1 - c)] + [(*chip, c) for chip in chips] + [(*chip, 1 - c) for chip in chips]
    return [(src, rows, to) for to in peers]


PAIR_PEERS, SMALL_PEERS = 1, 7


def _row_tile(k):
    for t in (256, 240, 128, 176, 64, 32, 16):
        if k % t == 0:
            return t
    raise ValueError(k)


def _pair_sum(c_idx, g, got, name):
    _, k, n = g.shape
    kh = k // 2
    tm = _row_tile(kh)
    nb = kh // tm

    def body(c_ref, g_ref, r_ref, o_ref):
        o_ref[...] = (g_ref[...].astype(F32) + r_ref[...].astype(F32)).astype(BF16)

    return pl.pallas_call(
        body, name=name,
        grid_spec=pltpu.PrefetchScalarGridSpec(
            num_scalar_prefetch=1, grid=(N_CHIPS, nb),
            in_specs=[pl.BlockSpec((1, tm, n), lambda s, i, c_ref: (s, c_ref[0] * nb + i, 0)),
                      pl.BlockSpec((1, tm, n), lambda s, i, c_ref: (s, i, 0))],
            out_specs=pl.BlockSpec((1, tm, n), lambda s, i, c_ref: (s, i, 0))),
        out_shape=jax.ShapeDtypeStruct((N_CHIPS, kh, n), BF16),
        compiler_params=_params("parallel", "parallel"),
    )(c_idx, g, got)


def _chip_sum(s_idx, mine, got, name):
    _, kh, n = mine.shape
    tm = _row_tile(kh)

    def body(s_ref, m_ref, r_ref, o_ref):
        acc = m_ref[0].astype(F32)
        for j in range(3):
            acc = acc + r_ref[j].astype(F32)
        o_ref[...] = acc

    return pl.pallas_call(
        body, name=name,
        grid_spec=pltpu.PrefetchScalarGridSpec(
            num_scalar_prefetch=1, grid=(kh // tm,),
            in_specs=[pl.BlockSpec((1, tm, n), lambda i, s_ref: (s_ref[0], i, 0)),
                      pl.BlockSpec((3, tm, n), lambda i, s_ref: (0, i, 0))],
            out_specs=pl.BlockSpec((tm, n), lambda i, s_ref: (i, 0))),
        out_shape=jax.ShapeDtypeStruct((kh, n), F32),
        compiler_params=_params("parallel"),
    )(s_idx, mine, got)


def _adamw_math(w, g, m, v):
    m = ADAM_B1 * m + (1.0 - ADAM_B1) * g
    v = ADAM_B2 * v + (1.0 - ADAM_B2) * (g * g)
    m_hat = m / (1.0 - ADAM_B1 ** ADAM_STEP)
    v_hat = v / (1.0 - ADAM_B2 ** ADAM_STEP)
    delta = -ADAM_LR * (m_hat / (jnp.sqrt(v_hat) + ADAM_EPS) + ADAM_WD * w)
    return delta, m, v


def _adamw(c_idx, w, g_mine, g_other, m, v, name):
    k, n = w.shape
    tm = k // 4

    def body(c_ref, w_ref, gm_ref, go_ref, m_ref, v_ref, g_ref, d_ref, mo_ref, vo_ref):
        g = jnp.where(pl.program_id(0) == c_ref[0], gm_ref[...], go_ref[...])
        d, mm, vv = _adamw_math(w_ref[...], g, m_ref[...], v_ref[...])
        g_ref[...] = g
        d_ref[...] = d
        mo_ref[...] = mm
        vo_ref[...] = vv

    full = pl.BlockSpec((tm, n), lambda h, i, c_ref: (2 * h + i, 0))
    mine = pl.BlockSpec((tm, n), lambda h, i, c_ref: (jnp.where(h == c_ref[0], i, 0), 0))
    other = pl.BlockSpec((tm, n), lambda h, i, c_ref: (jnp.where(h == c_ref[0], 0, i), 0))
    shp = jax.ShapeDtypeStruct((k, n), F32)
    return pl.pallas_call(
        body, name=name,
        grid_spec=pltpu.PrefetchScalarGridSpec(
            num_scalar_prefetch=1, grid=(2, 2),
            in_specs=[full, mine, other, full, full], out_specs=[full] * 4),
        out_shape=[shp] * 4, compiler_params=_params("arbitrary", "arbitrary"),
    )(c_idx, w, g_mine, g_other, m, v)


VEC_SLOTS = {
    "g_mix_norm": (0, 0, D), "b_conv_proj": (0, D, D), "g_ffn_norm": (0, 2 * D, D),
    "g_final": (0, 3 * D, D), "b_in": (1, 0, INW), "conv_b": (2, 0, C), "ln_g": (2, C, C),
    "ln_b": (2, 2 * C, C), "sinks": (2, 3 * C, NQ), "loss": (2, 3 * C + 128, 1),
}
VEC_ROWS, VEC_COLS = 8, 4 * D
CW_ROWS = 32
SMALL_NAMES = ["g_mix_norm", "b_in", "sinks", "conv_w", "conv_b", "ln_g", "ln_b",
               "b_conv_proj", "g_ffn_norm", "g_final"]
CW_LANES = C // N_CHIPS


def _pack_small(gs, loss):
    row0 = jnp.concatenate([gs["g_mix_norm"], gs["b_conv_proj"], gs["g_ffn_norm"], gs["g_final"]], axis=1)
    row1 = jnp.pad(gs["b_in"], ((0, 0), (0, VEC_COLS - INW)))
    row2 = jnp.concatenate([gs["conv_b"], gs["ln_g"], gs["ln_b"],
                            jnp.pad(gs["sinks"], ((0, 0), (0, 128 - NQ))),
                            jnp.pad(loss.reshape(1, 1), ((0, 0), (0, VEC_COLS - 3 * C - 129)))], axis=1)
    vec = jnp.concatenate([row0, row1, row2, jnp.zeros((VEC_ROWS - 3, VEC_COLS), F32)], axis=0)
    cw = jnp.pad(gs["conv_w"], ((0, CW_ROWS - KW), (0, 0)))
    return vec, cw


def _small_update(s_idx, vec_all, cw_all, wmv):
    nsm = len(SMALL_NAMES)

    def body(s_ref, vec_ref, cw_ref, *refs):
        ins = refs[:3 * nsm]
        outs = refs[3 * nsm:7 * nsm]
        loss_ref = refs[7 * nsm]

        def total(slot):
            row, lane, width = slot
            acc = vec_ref[row:row + 1, lane:lane + width]
            for k in range(1, 8):
                acc = acc + vec_ref[k * VEC_ROWS + row:k * VEC_ROWS + row + 1, lane:lane + width]
            return acc

        loss_ref[...] = jnp.broadcast_to(total(VEC_SLOTS["loss"]), loss_ref.shape)
        for p, name in enumerate(SMALL_NAMES):
            w_ref, m_ref, v_ref = ins[3 * p:3 * p + 3]
            g_ref, d_ref, mo_ref, vo_ref = outs[4 * p:4 * p + 4]
            if name == "conv_w":
                g = jnp.zeros((KW, CW_LANES), F32)
                for s in range(N_CHIPS):
                    cand = cw_ref[0:KW, s * CW_LANES:(s + 1) * CW_LANES]
                    for k in range(1, 8):
                        cand = cand + cw_ref[k * CW_ROWS:k * CW_ROWS + KW, s * CW_LANES:(s + 1) * CW_LANES]
                    g = jnp.where(s_ref[0] == s, cand, g)
            else:
                g = total(VEC_SLOTS[name])
            d, mm, vv = _adamw_math(w_ref[...], g, m_ref[...], v_ref[...])
            g_ref[...] = g
            d_ref[...] = d
            mo_ref[...] = mm
            vo_ref[...] = vv

    vmem = pl.BlockSpec(memory_space=pltpu.VMEM)
    flat = [a for t in wmv for a in t]
    out_shape = []
    for w, _, _ in wmv:
        out_shape += [jax.ShapeDtypeStruct(w.shape, F32)] * 4
    out_shape.append(jax.ShapeDtypeStruct((1, 128), F32))
    res = pl.pallas_call(
        body, name="small_update",
        in_specs=[pl.BlockSpec(memory_space=pltpu.SMEM)] + [vmem] * (2 + len(flat)),
        out_specs=[vmem] * len(out_shape), out_shape=out_shape,
    )(s_idx, vec_all, cw_all, *flat)
    return [tuple(res[4 * p:4 * p + 4]) for p in range(nsm)], res[4 * nsm]


WEIGHT_ORDER = ["g_mix_norm", "w_in", "b_in", "sinks", "conv_w", "conv_b", "ln_g", "ln_b",
                "w_attn_proj", "w_conv_proj", "b_conv_proj", "w_out", "g_ffn_norm", "w_ffn_in",
                "w_ffn_down", "g_final"]


def kernel(x, g_mix_norm, w_in, b_in, sinks, conv_w, conv_b, ln_g, ln_b, w_attn_proj, w_conv_proj, b_conv_proj, w_out, g_ffn_norm, w_ffn_in, w_ffn_down, g_final, loss_target, m_g_mix_norm, m_w_in, m_b_in, m_sinks, m_conv_w, m_conv_b, m_ln_g, m_ln_b, m_w_attn_proj, m_w_conv_proj, m_b_conv_proj, m_w_out, m_g_ffn_norm, m_w_ffn_in, m_w_ffn_down, m_g_final, v_g_mix_norm, v_w_in, v_b_in, v_sinks, v_conv_w, v_conv_b, v_ln_g, v_ln_b, v_w_attn_proj, v_w_conv_proj, v_b_conv_proj, v_w_out, v_g_ffn_norm, v_w_ffn_in, v_w_ffn_down, v_g_final):
    w = dict(g_mix_norm=g_mix_norm, w_in=w_in, b_in=b_in, sinks=sinks, conv_w=conv_w, conv_b=conv_b,
             ln_g=ln_g, ln_b=ln_b, w_attn_proj=w_attn_proj, w_conv_proj=w_conv_proj,
             b_conv_proj=b_conv_proj, w_out=w_out, g_ffn_norm=g_ffn_norm, w_ffn_in=w_ffn_in,
             w_ffn_down=w_ffn_down, g_final=g_final)
    m = dict(g_mix_norm=m_g_mix_norm, w_in=m_w_in, b_in=m_b_in, sinks=m_sinks, conv_w=m_conv_w,
             conv_b=m_conv_b, ln_g=m_ln_g, ln_b=m_ln_b, w_attn_proj=m_w_attn_proj,
             w_conv_proj=m_w_conv_proj, b_conv_proj=m_b_conv_proj, w_out=m_w_out,
             g_ffn_norm=m_g_ffn_norm, w_ffn_in=m_w_ffn_in, w_ffn_down=m_w_ffn_down, g_final=m_g_final)
    v = dict(g_mix_norm=v_g_mix_norm, w_in=v_w_in, b_in=v_b_in, sinks=v_sinks, conv_w=v_conv_w,
             conv_b=v_conv_b, ln_g=v_ln_g, ln_b=v_ln_b, w_attn_proj=v_w_attn_proj,
             w_conv_proj=v_w_conv_proj, b_conv_proj=v_b_conv_proj, w_out=v_w_out,
             g_ffn_norm=v_g_ffn_norm, w_ffn_in=v_w_ffn_in, w_ffn_down=v_w_ffn_down, g_final=v_g_final)

    c_idx = lax.axis_index("c").astype(jnp.int32).reshape(1)
    s_idx = (2 * lax.axis_index("x") + lax.axis_index("y")).astype(jnp.int32).reshape(1)

    out_g, out_d, out_m, out_v = {}, {}, {}, {}

    def gather_start(tag, shards):
        lands = [lax.empty((N_CHIPS,) + s.shape, s.dtype) for s in shards]
        return _chip_start("gather_start_" + tag, _gather_ends, GATHER_PEERS, shards, lands)

    def gather_relay(tag, state, after, first=0, count=None):
        send_sems, recv_sems, shards, lands, _ = state
        last = len(shards) if count is None else first + count
        return _gather_relay("gather_relay_" + tag, send_sems, recv_sems, shards[first:last],
                             lands[first:last], after, first)

    def gather_finish(tag, relay, after):
        return _forward_wait("forward_wait_" + tag, relay[0], relay[1], relay[2], after)

    names_b = ["w_attn_proj", "w_conv_proj", "w_out", "w_ffn_in", "w_ffn_down"]
    big = {name: (w[name][0], m[name][0], v[name][0]) for name in names_b}
    big["w_in"] = (w_in[0].T, m_w_in[0].T, v_w_in[0].T)
    state_a = gather_start("a", [big["w_in"][0].astype(BF16), jnp.pad(conv_w[0], ((0, CW_ROWS - KW), (0, 0)))])
    state_b = gather_start("b", [(big[name][0] + state_a[4][0, 0]).astype(BF16) for name in names_b])
    got_a = gather_finish("a", gather_relay("a", state_a, [state_b[4]]), [])
    w_in_t_full = got_a[0].reshape(INW, D)
    conv_w_full = got_a[1].transpose(1, 0, 2).reshape(CW_ROWS, C)[:KW]

    xs, target = x[0], loss_target[0]
    g_final2 = g_final.reshape(1, D)
    h, qkv, glu, gl = _in_proj(xs, g_mix_norm, w_in_t_full, b_in)
    o, lse = _attn_fwd(qkv, sinks)
    relay_1 = gather_relay("b1", state_b, [o], 0, 3)
    u, cact = _conv_fwd(glu, conv_w_full, conv_b, ln_g, ln_b, relay_1[3])
    w_ap4, w_cp4, w_out4 = gather_finish("b1", relay_1, [cact])
    w_out_full = w_out4.reshape(D, D)
    relay_2 = gather_relay("b2", state_b, [cact], 3, 1)
    ya, yc, mg, x1 = _mix_out(xs, o, cact, gl, w_ap4, w_cp4, b_conv_proj, w_out_full, relay_2[3])
    w_fi4, = gather_finish("b2", relay_2, [x1])
    relay_3 = gather_relay("b3", state_b, [x1], 4, 1)
    h2, gu, act = _ffn_in(x1, g_ffn_norm, w_fi4, relay_3[3])
    w_dn4, = gather_finish("b3", relay_3, [act])
    w_dn_full = w_dn4.reshape(DFF, D)
    dx2, dx2b, dg_final, loss_part = _ffn_out_loss(x1, act, w_dn_full, g_final2, target)

    def exchange_start(tag, grads):
        lands = [lax.empty((N_CHIPS, g.shape[1] // 2, g.shape[2]), g.dtype) for g in grads]
        return _chip_start("pair_start_" + tag, _exchange_ends, PAIR_PEERS, grads, lands)

    def reduce_start(tag, names, exchange, after):
        send_sems, recv_sems, grads, lands, _ = exchange
        grads, from_sibling = _chip_wait("pair_wait_" + tag, _exchange_ends, send_sems, recv_sems, grads, lands, after)
        pair = [_pair_sum(c_idx, g, r, "pair_sum_" + name) for name, g, r in zip(names, grads, from_sibling)]
        lands = [lax.empty((3,) + p.shape[1:], p.dtype) for p in pair]
        return _chip_start("chip_start_" + tag, _reduce_ends, REDUCE_PEERS, pair, lands)

    def reduce_sum(tag, names, state, after):
        send_sems, recv_sems, pair, lands, _ = state
        pair, lands = _chip_wait("chip_wait_" + tag, _reduce_ends, send_sems, recv_sems, pair, lands, after)
        mine = [_chip_sum(s_idx, p, r, "chip_sum_" + name) for name, p, r in zip(names, pair, lands)]
        others = [lax.empty(a.shape, a.dtype) for a in mine]
        return _chip_start("share_start_" + tag, _share_ends, PAIR_PEERS, mine, others)

    def reduce_finish(tag, names, share, after):
        send_sems, recv_sems, mine, others, _ = share
        mine, others = _chip_wait("share_wait_" + tag, _share_ends, send_sems, recv_sems, mine, others, after)
        for name, g_mine, g_other in zip(names, mine, others):
            wv, mv, vv = big[name]
            res = _adamw(c_idx, wv, g_mine, g_other, mv, vv, "adamw_" + name)
            if name == "w_in":
                res = [a.T for a in res]
            out_g[name], out_d[name], out_m[name], out_v[name] = [a[None] for a in res]

    dgu, dx1, dx1b, dg_ffn = _ffn_bwd(dx2, dx2b, gu, x1, g_ffn_norm, w_dn_full, w_fi4)
    names_1 = ["w_ffn_in", "w_ffn_down"]
    exchange_1 = exchange_start("1", [_grad_w(h2, dgu, "grad_w_ffn_in", 512, FSH, True),
                                      _grad_w(act, dx2b, "grad_w_ffn_down", 256, D, False)])
    dya, dyc, dgl, do, dc, db_cp = _mix_bwd(dx1b, gl, ya, yc, w_out_full, w_ap4, w_cp4, exchange_1[4])
    state_1 = reduce_start("1", names_1, exchange_1, [dya])
    grads_2 = [_grad_w(mg, dx1b, "grad_w_out", 512, D, False),
               _grad_w(o, dya, "grad_w_attn_proj", 512, 256, True),
               _grad_w(cact, dyc, "grad_w_conv_proj", 512, 256, True)]
    dglu, dconv_w, dconv_b, dln_g, dln_b = _conv_bwd(glu, u, dc, conv_w_full, ln_g, ln_b, state_1[4])
    dq, dkv, dsinks = _attn_bwd(qkv, o, do, lse, sinks)
    names_2 = ["w_in", "w_out", "w_attn_proj", "w_conv_proj"]
    gw_in_t = _grad_w_in_t(h, dq, dkv, dglu, dgl)
    exchange_2 = exchange_start("2", [gw_in_t.reshape(N_CHIPS, INW // N_CHIPS, D)] + grads_2)
    grad_x, dg_mix, db_in = _in_proj_bwd(dq, dkv, dglu, dgl, xs, dx1, g_mix_norm, w_in_t_full, exchange_2[4])

    gs = {"g_mix_norm": dg_mix, "b_in": db_in, "sinks": dsinks[:, 0].reshape(1, NQ),
          "conv_w": dconv_w, "conv_b": dconv_b, "ln_g": dln_g, "ln_b": dln_b,
          "b_conv_proj": db_cp, "g_ffn_norm": dg_ffn, "g_final": dg_final}
    blocks = list(_pack_small(gs, loss_part[0, 0]))
    tables = [lax.empty((8 * b.shape[0], b.shape[1]), b.dtype) for b in blocks]
    small = _chip_start("small_start", _small_ends, SMALL_PEERS, blocks, tables)

    state_2 = reduce_start("2", names_2, exchange_2, [grad_x, small[4]])
    share_1 = reduce_sum("1", names_1, state_1, [state_2[4]])
    reduce_finish("1", names_1, share_1, [state_2[4]])
    blocks, tables = _chip_wait("small_wait", _small_ends, small[0], small[1], small[2], small[3],
                                [out_d["w_ffn_down"]])
    me = 4 * lax.axis_index("x") + 2 * lax.axis_index("y") + lax.axis_index("c")
    vec_all, cw_all = [lax.dynamic_update_slice(t, b, (me * b.shape[0], 0)) for t, b in zip(tables, blocks)]
    share_2 = reduce_sum("2", names_2, state_2, [vec_all])

    def view(a, name):
        if name == "conv_w":
            return a[0]
        if name == "g_final":
            return a.reshape(1, D)
        return a

    wmv = [(view(w[name], name), view(m[name], name), view(v[name], name)) for name in SMALL_NAMES]
    small_out, loss_row = _small_update(s_idx, vec_all, cw_all, wmv)
    for name, (g, d, mm, vv) in zip(SMALL_NAMES, small_out):
        shape = w[name].shape
        out_g[name], out_d[name], out_m[name], out_v[name] = (
            g.reshape(shape), d.reshape(shape), mm.reshape(shape), vv.reshape(shape))

    reduce_finish("2", names_2, share_2, [loss_row])

    loss = loss_row[0, 0]
    return (loss, grad_x[None], *[out_g[k] for k in WEIGHT_ORDER], *[out_d[k] for k in WEIGHT_ORDER],
            *[out_m[k] for k in WEIGHT_ORDER], *[out_v[k] for k in WEIGHT_ORDER])
```

```python
import functools

import jax
import jax.numpy as jnp
from jax import lax
from jax.experimental import pallas as pl
from jax.experimental.pallas import tpu as pltpu

F32 = jnp.float32
BF16 = jnp.bfloat16

T = 2048
D = 1024
HD = 64
NQ = 8
NKV = 2
GROUP = NQ // NKV
BLK = 128
AW = NQ * HD
KVW = NKV * HD
C = 512
KW = 31
QKVW = AW + 2 * KVW
GLU_OFF = QKVW
GATE_OFF = GLU_OFF + 2 * C
INW = GATE_OFF + 2 * D
DFF = 2816
EPS = 1e-5
NEG = -1e30
SCALE = HD ** -0.5
HALO = 32
N_CHIPS = 4
FSH = 2 * DFF // N_CHIPS

ADAM_LR = 0.001
ADAM_B1 = 0.9
ADAM_B2 = 0.999
ADAM_EPS = 1e-08
ADAM_WD = 0.01
ADAM_STEP = 10

VMEM_LIMIT = 56 * 1024 * 1024
ROW_TM = 512
MESH = pl.DeviceIdType.MESH


def _params(*sem):
    return pltpu.CompilerParams(dimension_semantics=sem, vmem_limit_bytes=VMEM_LIMIT)


def _dot(a, b):
    return jnp.dot(a, b, preferred_element_type=F32)


def _dot_nt(a, b):
    return lax.dot_general(a, b, (((1,), (1,)), ((), ())), preferred_element_type=F32)


def _dot_tn(a, b):
    return lax.dot_general(a, b, (((0,), (0,)), ((), ())), preferred_element_type=F32)


def _sigmoid(v):
    return 1.0 / (1.0 + jnp.exp(-v))


def _rows(tm, n):
    return pl.BlockSpec((tm, n), lambda i: (i, 0))


def _whole(shape):
    return pl.BlockSpec(shape, lambda i: tuple(0 for _ in shape))


def _in_proj(x, g_mix, w_in_t, b_in):
    tm = ROW_TM

    def body(x_ref, g_ref, w_ref, b_ref, h_ref, qkv_ref, glu_ref, gl_ref):
        xv = x_ref[...]
        r = lax.rsqrt(jnp.mean(xv * xv, axis=-1, keepdims=True) + EPS)
        h = (xv * r * g_ref[...]).astype(BF16)
        h_ref[...] = h
        qkv_ref[...] = (_dot_nt(h, w_ref[0:GLU_OFF, :]) + b_ref[:, 0:GLU_OFF]).astype(BF16)
        glu_ref[...] = (_dot_nt(h, w_ref[GLU_OFF:GATE_OFF, :]) + b_ref[:, GLU_OFF:GATE_OFF]).astype(BF16)
        gl_ref[...] = (_dot_nt(h, w_ref[GATE_OFF:INW, :]) + b_ref[:, GATE_OFF:INW]).astype(BF16)

    return pl.pallas_call(
        body, name="in_proj", grid=(T // tm,),
        in_specs=[_rows(tm, D), _whole((1, D)), _whole((INW, D)), _whole((1, INW))],
        out_specs=[_rows(tm, D), _rows(tm, QKVW), _rows(tm, 2 * C), _rows(tm, 2 * D)],
        out_shape=[jax.ShapeDtypeStruct((T, D), BF16), jax.ShapeDtypeStruct((T, QKVW), BF16),
                   jax.ShapeDtypeStruct((T, 2 * C), BF16), jax.ShapeDtypeStruct((T, 2 * D), BF16)],
        compiler_params=_params("parallel"),
    )(x, g_mix, w_in_t, b_in)


GROWS = GROUP * BLK
BAND = 2 * BLK


def _band(i):
    rb = pl.multiple_of(jnp.maximum(i - 1, 0) * BLK, BLK)
    row = lax.broadcasted_iota(jnp.int32, (GROWS, BAND), 0)
    kpos = rb + lax.broadcasted_iota(jnp.int32, (GROWS, BAND), 1)
    qpos = i * BLK + jnp.bitwise_and(row, BLK - 1)
    return rb, jnp.logical_and(kpos <= qpos, kpos > qpos - BLK)


def _sink_column(sink_ref, g):
    head = lax.shift_right_logical(lax.broadcasted_iota(jnp.int32, (GROWS, 1), 0), 7)
    col = jnp.full((GROWS, 1), sink_ref[0, g * GROUP], F32)
    for hh in range(1, GROUP):
        col = jnp.where(head == hh, sink_ref[0, g * GROUP + hh], col)
    return col


def _attn_fwd(qkv, sinks):
    def body(sink_ref, qkv_ref, o_ref, lse_ref, s_ref, p_ref):
        i = pl.program_id(0)
        r0 = pl.multiple_of(i * BLK, BLK)
        rb, valid = _band(i)
        for g in range(NKV):
            kband = qkv_ref[pl.ds(rb, BAND), AW + g * HD:AW + (g + 1) * HD]
            vband = qkv_ref[pl.ds(rb, BAND), AW + KVW + g * HD:AW + KVW + (g + 1) * HD]
            for hh in range(GROUP):
                h = g * GROUP + hh
                s_ref[hh * BLK:(hh + 1) * BLK, :] = _dot_nt(qkv_ref[pl.ds(r0, BLK), h * HD:(h + 1) * HD], kband)
            s = jnp.where(valid, s_ref[...] * SCALE, NEG)
            sink = _sink_column(sink_ref, g)
            m = jnp.maximum(jnp.max(s, axis=-1, keepdims=True), sink)
            p = jnp.exp(s - m)
            den = jnp.sum(p, axis=-1, keepdims=True) + jnp.exp(sink - m)
            p_ref[...] = (p * (1.0 / den)).astype(BF16)
            lse = m + jnp.log(den)
            for hh in range(GROUP):
                h = g * GROUP + hh
                o_ref[:, h * HD:(h + 1) * HD] = _dot(p_ref[hh * BLK:(hh + 1) * BLK, :], vband).astype(BF16)
                lse_ref[:, h:h + 1] = lse[hh * BLK:(hh + 1) * BLK]

    return pl.pallas_call(
        body, name="attn_fwd", grid=(T // BLK,),
        in_specs=[pl.BlockSpec(memory_space=pltpu.SMEM), _whole((T, QKVW))],
        out_specs=[_rows(BLK, AW), _rows(BLK, NQ)],
        out_shape=[jax.ShapeDtypeStruct((T, AW), BF16), jax.ShapeDtypeStruct((T, NQ), F32)],
        scratch_shapes=[pltpu.VMEM((GROWS, BAND), F32), pltpu.VMEM((GROWS, BAND), BF16)],
        compiler_params=_params("parallel"),
    )(sinks, qkv)


CONV_TM = 256
CONV_SUB = 32


def _glu(ab):
    a = ab[:, 0:C].astype(F32)
    b = ab[:, C:2 * C].astype(F32)
    return a * _sigmoid(b)


SUBLANES = 8


def _shifted_copies(ref):
    rows = ref.shape[1] - SUBLANES
    for r in range(1, SUBLANES):
        ref[r, 0:rows, :] = ref[0, r:r + rows, :]


def _shifted_rows(ref, start, size):
    r = start % SUBLANES
    return ref[r, start - r:start - r + size, :]


def _conv_fwd(glu, conv_w, conv_b, ln_g, ln_b, dep):
    tm = CONV_TM

    def body(cur_ref, prev_ref, w_ref, cb_ref, g_ref, b_ref, dep_ref, u_ref, c_ref, zs_ref):
        i = pl.program_id(0)
        zprev = _glu(prev_ref[tm - HALO:tm, :])
        zs_ref[0, 0:HALO, :] = jnp.where(i > 0, zprev, 0.0)
        zs_ref[0, HALO:HALO + tm, :] = _glu(cur_ref[...])
        _shifted_copies(zs_ref)
        for s in range(tm // CONV_SUB):
            base = HALO + s * CONV_SUB - (KW - 1)
            acc = jnp.broadcast_to(cb_ref[...], (CONV_SUB, C))
            for j in range(KW):
                acc = acc + w_ref[j:j + 1, :] * _shifted_rows(zs_ref, base + j, CONV_SUB)
            rows = slice(s * CONV_SUB, (s + 1) * CONV_SUB)
            u_ref[rows, :] = acc
            mu = jnp.mean(acc, axis=-1, keepdims=True)
            xc = acc - mu
            var = jnp.mean(xc * xc, axis=-1, keepdims=True)
            y = xc * lax.rsqrt(var + EPS) * g_ref[...] + b_ref[...]
            c_ref[rows, :] = (y * _sigmoid(y)).astype(BF16)

    return pl.pallas_call(
        body, name="conv_fwd", grid=(T // tm,),
        in_specs=[_rows(tm, 2 * C),
                  pl.BlockSpec((tm, 2 * C), lambda i: (jnp.maximum(i - 1, 0), 0)),
                  _whole((KW, C)), _whole((1, C)), _whole((1, C)), _whole((1, C)), _whole((8, 128))],
        out_specs=[_rows(tm, C), _rows(tm, C)],
        out_shape=[jax.ShapeDtypeStruct((T, C), F32), jax.ShapeDtypeStruct((T, C), BF16)],
        scratch_shapes=[pltpu.VMEM((SUBLANES, HALO + tm, C), F32)],
        compiler_params=_params("parallel"),
    )(glu, glu, conv_w, conv_b, ln_g, ln_b, dep)


def _mix_out(x, o, cact, gl, w_ap, w_cp, b_cp, w_out, dep):
    tm = ROW_TM

    def body(x_ref, o_ref, c_ref, gl_ref, wap_ref, wcp_ref, bcp_ref, wo_ref, dep_ref,
             ya_ref, yc_ref, mg_ref, x1_ref):
        ov, cv = o_ref[...], c_ref[...]
        ya = jnp.concatenate([_dot(ov, wap_ref[s]) for s in range(N_CHIPS)], axis=1)
        yc = jnp.concatenate([_dot(cv, wcp_ref[s]) for s in range(N_CHIPS)], axis=1) + bcp_ref[...]
        g0 = _sigmoid(gl_ref[:, 0:D].astype(F32))
        g1 = _sigmoid(gl_ref[:, D:2 * D].astype(F32))
        mg = (g0 * ya + g1 * yc).astype(BF16)
        ya_ref[...] = ya.astype(BF16)
        yc_ref[...] = yc.astype(BF16)
        mg_ref[...] = mg
        x1_ref[...] = x_ref[...] + _dot(mg, wo_ref[...])

    return pl.pallas_call(
        body, name="mix_out", grid=(T // tm,),
        in_specs=[_rows(tm, D), _rows(tm, AW), _rows(tm, C), _rows(tm, 2 * D),
                  _whole((N_CHIPS, AW, D // N_CHIPS)), _whole((N_CHIPS, C, D // N_CHIPS)), _whole((1, D)),
                  _whole((D, D)), _whole((8, 128))],
        out_specs=[_rows(tm, D), _rows(tm, D), _rows(tm, D), _rows(tm, D)],
        out_shape=[jax.ShapeDtypeStruct((T, D), BF16), jax.ShapeDtypeStruct((T, D), BF16),
                   jax.ShapeDtypeStruct((T, D), BF16), jax.ShapeDtypeStruct((T, D), F32)],
        compiler_params=_params("parallel"),
    )(x, o, cact, gl, w_ap, w_cp, b_cp, w_out, dep)


def _ffn_in(x1, g_ffn, w_fi, dep):
    tm = ROW_TM

    def body(x_ref, g_ref, w_ref, dep_ref, h_ref, gu_ref, act_ref):
        xv = x_ref[...]
        r = lax.rsqrt(jnp.mean(xv * xv, axis=-1, keepdims=True) + EPS)
        h = (xv * r * g_ref[...]).astype(BF16)
        h_ref[...] = h
        for s in range(N_CHIPS // 2):
            c0 = s * FSH
            gate = _dot(h, w_ref[s])
            up = _dot(h, w_ref[s + N_CHIPS // 2])
            gu_ref[:, c0:c0 + FSH] = gate.astype(BF16)
            gu_ref[:, DFF + c0:DFF + c0 + FSH] = up.astype(BF16)
            act_ref[:, c0:c0 + FSH] = (gate * _sigmoid(gate) * up).astype(BF16)

    return pl.pallas_call(
        body, name="ffn_in", grid=(T // tm,),
        in_specs=[_rows(tm, D), _whole((1, D)), _const((N_CHIPS, D, FSH)), _whole((8, 128))],
        out_specs=[_rows(tm, D), _rows(tm, 2 * DFF), _rows(tm, DFF)],
        out_shape=[jax.ShapeDtypeStruct((T, D), BF16), jax.ShapeDtypeStruct((T, 2 * DFF), BF16),
                   jax.ShapeDtypeStruct((T, DFF), BF16)],
        compiler_params=_params("parallel"),
    )(x1, g_ffn, w_fi, dep)


def _ffn_out_loss(x1, act, w_dn, g_final, target):
    tm = ROW_TM

    def body(x_ref, a_ref, w_ref, g_ref, t_ref, dx_ref, dxb_ref, dg_ref, loss_ref):
        i = pl.program_id(0)
        x2 = x_ref[...] + _dot(a_ref[...], w_ref[...])
        r = lax.rsqrt(jnp.mean(x2 * x2, axis=-1, keepdims=True) + EPS)
        xh = x2 * r
        g = g_ref[...]
        err = xh * g - t_ref[...]
        dy = err * (1.0 / D)
        dyg = dy * g
        dx = r * (dyg - xh * jnp.mean(dyg * xh, axis=-1, keepdims=True))
        dx_ref[...] = dx
        dxb_ref[...] = dx.astype(BF16)
        part = 0.5 * jnp.sum(jnp.mean(err * err, axis=-1, keepdims=True), axis=0, keepdims=True)

        @pl.when(i == 0)
        def _():
            dg_ref[...] = jnp.zeros_like(dg_ref)
            loss_ref[...] = jnp.zeros_like(loss_ref)

        dg_ref[...] += jnp.sum(dy * xh, axis=0, keepdims=True)
        loss_ref[...] += jnp.broadcast_to(part, loss_ref.shape)

    return pl.pallas_call(
        body, name="ffn_out_loss", grid=(T // tm,),
        in_specs=[_rows(tm, D), _rows(tm, DFF), _whole((DFF, D)), _whole((1, D)), _rows(tm, D)],
        out_specs=[_rows(tm, D), _rows(tm, D), _whole((1, D)), _whole((1, 128))],
        out_shape=[jax.ShapeDtypeStruct((T, D), F32), jax.ShapeDtypeStruct((T, D), BF16),
                   jax.ShapeDtypeStruct((1, D), F32), jax.ShapeDtypeStruct((1, 128), F32)],
        compiler_params=_params("arbitrary"),
    )(x1, act, w_dn, g_final, target)


def _const(shape):
    return pl.BlockSpec(shape, lambda i: tuple(0 for _ in shape), pipeline_mode=pl.Buffered(1))


def _ffn_bwd(dx2, dx2b, gu, x1, g_ffn, w_dn_t, w_fi_t):
    tm = ROW_TM // 2

    def body(dx_ref, dxb_ref, gu_ref, x_ref, g_ref, wdn_ref, wfi_ref,
             dgu_ref, dx1_ref, dx1b_ref, dg_ref):
        i = pl.program_id(0)
        dxb = dxb_ref[...]
        dh = jnp.zeros((tm, D), F32)
        for k in range(N_CHIPS // 2):
            c0 = k * FSH
            dact = _dot_nt(dxb, wdn_ref[c0:c0 + FSH, :])
            gate = gu_ref[:, c0:c0 + FSH].astype(F32)
            up = gu_ref[:, DFF + c0:DFF + c0 + FSH].astype(F32)
            s = _sigmoid(gate)
            dup = (dact * gate * s).astype(BF16)
            dgate = (dact * up * s * (1.0 + gate * (1.0 - s))).astype(BF16)
            dgu_ref[:, c0:c0 + FSH] = dgate
            dgu_ref[:, DFF + c0:DFF + c0 + FSH] = dup
            dh = dh + _dot_nt(dgate, wfi_ref[k]) + _dot_nt(dup, wfi_ref[k + N_CHIPS // 2])
        xv = x_ref[...]
        r = lax.rsqrt(jnp.mean(xv * xv, axis=-1, keepdims=True) + EPS)
        xh = xv * r
        dhg = dh * g_ref[...]
        dx1 = dx_ref[...] + r * (dhg - xh * jnp.mean(dhg * xh, axis=-1, keepdims=True))
        dx1_ref[...] = dx1
        dx1b_ref[...] = dx1.astype(BF16)

        @pl.when(i == 0)
        def _():
            dg_ref[...] = jnp.zeros_like(dg_ref)

        dg_ref[...] += jnp.sum(dh * xh, axis=0, keepdims=True)

    return pl.pallas_call(
        body, name="ffn_bwd", grid=(T // tm,),
        in_specs=[_rows(tm, D), _rows(tm, D), _rows(tm, 2 * DFF), _rows(tm, D), _whole((1, D)),
                  _const((DFF, D)), _const((N_CHIPS, D, FSH))],
        out_specs=[_rows(tm, 2 * DFF), _rows(tm, D), _rows(tm, D), _whole((1, D))],
        out_shape=[jax.ShapeDtypeStruct((T, 2 * DFF), BF16), jax.ShapeDtypeStruct((T, D), F32),
                   jax.ShapeDtypeStruct((T, D), BF16), jax.ShapeDtypeStruct((1, D), F32)],
        compiler_params=_params("arbitrary"),
    )(dx2, dx2b, gu, x1, g_ffn, w_dn_t, w_fi_t)


def _mix_bwd(dx1b, gl, ya, yc, w_out_t, w_ap_t, w_cp_t, dep):
    tm = ROW_TM

    def body(dx_ref, gl_ref, ya_ref, yc_ref, wo_ref, wap_ref, wcp_ref, dep_ref,
             dya_ref, dyc_ref, dgl_ref, do_ref, dc_ref, db_ref):
        i = pl.program_id(0)
        dm = _dot_nt(dx_ref[...], wo_ref[...])
        g0 = _sigmoid(gl_ref[:, 0:D].astype(F32))
        g1 = _sigmoid(gl_ref[:, D:2 * D].astype(F32))
        dya = dm * g0
        dyc = dm * g1
        dgl_ref[:, 0:D] = (dya * ya_ref[...].astype(F32) * (1.0 - g0)).astype(BF16)
        dgl_ref[:, D:2 * D] = (dyc * yc_ref[...].astype(F32) * (1.0 - g1)).astype(BF16)
        dyab = dya.astype(BF16)
        dycb = dyc.astype(BF16)
        dya_ref[...] = dyab
        dyc_ref[...] = dycb
        sw = D // N_CHIPS
        do = jnp.zeros((tm, AW), F32)
        dcv = jnp.zeros((tm, C), F32)
        for s in range(N_CHIPS):
            do = do + _dot_nt(dyab[:, s * sw:(s + 1) * sw], wap_ref[s])
            dcv = dcv + _dot_nt(dycb[:, s * sw:(s + 1) * sw], wcp_ref[s])
        do_ref[...] = do.astype(BF16)
        dc_ref[...] = dcv

        @pl.when(i == 0)
        def _():
            db_ref[...] = jnp.zeros_like(db_ref)

        db_ref[...] += jnp.sum(dyc, axis=0, keepdims=True)

    return pl.pallas_call(
        body, name="mix_bwd", grid=(T // tm,),
        in_specs=[_rows(tm, D), _rows(tm, 2 * D), _rows(tm, D), _rows(tm, D),
                  _whole((D, D)), _whole((N_CHIPS, AW, D // N_CHIPS)), _whole((N_CHIPS, C, D // N_CHIPS)),
                  _whole((8, 128))],
        out_specs=[_rows(tm, D), _rows(tm, D), _rows(tm, 2 * D), _rows(tm, AW), _rows(tm, C),
                   _whole((1, D))],
        out_shape=[jax.ShapeDtypeStruct((T, D), BF16), jax.ShapeDtypeStruct((T, D), BF16),
                   jax.ShapeDtypeStruct((T, 2 * D), BF16), jax.ShapeDtypeStruct((T, AW), BF16),
                   jax.ShapeDtypeStruct((T, C), F32), jax.ShapeDtypeStruct((1, D), F32)],
        compiler_params=_params("arbitrary"),
    )(dx1b, gl, ya, yc, w_out_t, w_ap_t, w_cp_t, dep)


def _conv_bwd(glu, u, dc, conv_w, ln_g, ln_b, dep):
    tm = CONV_TM
    nblk = T // tm

    def du_of(uv, dcv, g_ref, b_ref):
        mu = jnp.mean(uv, axis=-1, keepdims=True)
        xc = uv - mu
        var = jnp.mean(xc * xc, axis=-1, keepdims=True)
        rstd = lax.rsqrt(var + EPS)
        xh = xc * rstd
        y = xh * g_ref[...] + b_ref[...]
        sg = _sigmoid(y)
        dy = dcv * (sg * (1.0 + y * (1.0 - sg)))
        dxh = dy * g_ref[...]
        du = rstd * (dxh - jnp.mean(dxh, axis=-1, keepdims=True)
                     - xh * jnp.mean(dxh * xh, axis=-1, keepdims=True))
        return du, dy, xh

    def body(cur_ref, prev_ref, u_ref, un_ref, dc_ref, dcn_ref, w_ref, g_ref, b_ref, dep_ref,
             dglu_ref, dw_ref, dcb_ref, dg_ref, db_ref, zs_ref, dus_ref):
        i = pl.program_id(0)

        @pl.when(i == 0)
        def _():
            dw_ref[...] = jnp.zeros_like(dw_ref)
            dcb_ref[...] = jnp.zeros_like(dcb_ref)
            dg_ref[...] = jnp.zeros_like(dg_ref)
            db_ref[...] = jnp.zeros_like(db_ref)

        zprev = _glu(prev_ref[tm - HALO:tm, :])
        zs_ref[0, 0:HALO, :] = jnp.where(i > 0, zprev, 0.0)
        zs_ref[0, HALO:HALO + tm, :] = _glu(cur_ref[...])
        _shifted_copies(zs_ref)

        dun, _, _ = du_of(un_ref[0:HALO, :], dcn_ref[0:HALO, :], g_ref, b_ref)
        dus_ref[0, tm:tm + HALO, :] = jnp.where(i < nblk - 1, dun, 0.0)
        dg_acc = jnp.zeros((1, C), F32)
        db_acc = jnp.zeros((1, C), F32)
        dcb_acc = jnp.zeros((1, C), F32)
        for s in range(tm // CONV_SUB):
            rows = slice(s * CONV_SUB, (s + 1) * CONV_SUB)
            du, dy, xh = du_of(u_ref[rows, :], dc_ref[rows, :], g_ref, b_ref)
            dus_ref[0, rows, :] = du
            dg_acc = dg_acc + jnp.sum(dy * xh, axis=0, keepdims=True)
            db_acc = db_acc + jnp.sum(dy, axis=0, keepdims=True)
            dcb_acc = dcb_acc + jnp.sum(du, axis=0, keepdims=True)
        dg_ref[...] += dg_acc
        db_ref[...] += db_acc
        dcb_ref[...] += dcb_acc
        _shifted_copies(dus_ref)

        for j in range(KW):
            acc = jnp.zeros((CONV_SUB, C), F32)
            for s in range(tm // CONV_SUB):
                base = HALO + s * CONV_SUB - (KW - 1) + j
                acc = acc + dus_ref[0, s * CONV_SUB:(s + 1) * CONV_SUB, :] * _shifted_rows(zs_ref, base, CONV_SUB)
            dw_ref[j:j + 1, :] += jnp.sum(acc, axis=0, keepdims=True)

        for s in range(tm // CONV_SUB):
            rows = slice(s * CONV_SUB, (s + 1) * CONV_SUB)
            dz = jnp.zeros((CONV_SUB, C), F32)
            for j in range(KW):
                dz = dz + w_ref[j:j + 1, :] * _shifted_rows(dus_ref, s * CONV_SUB + (KW - 1) - j, CONV_SUB)
            a = cur_ref[rows, 0:C].astype(F32)
            sb = _sigmoid(cur_ref[rows, C:2 * C].astype(F32))
            dglu_ref[rows, 0:C] = (dz * sb).astype(BF16)
            dglu_ref[rows, C:2 * C] = (dz * a * sb * (1.0 - sb)).astype(BF16)

    nxt = lambda i: (jnp.minimum(i + 1, nblk - 1), 0)
    return pl.pallas_call(
        body, name="conv_bwd", grid=(nblk,),
        in_specs=[_rows(tm, 2 * C),
                  pl.BlockSpec((tm, 2 * C), lambda i: (jnp.maximum(i - 1, 0), 0)),
                  _rows(tm, C), pl.BlockSpec((tm, C), nxt),
                  _rows(tm, C), pl.BlockSpec((tm, C), nxt),
                  _whole((KW, C)), _whole((1, C)), _whole((1, C)), _whole((8, 128))],
        out_specs=[_rows(tm, 2 * C), _whole((KW, C)), _whole((1, C)), _whole((1, C)), _whole((1, C))],
        out_shape=[jax.ShapeDtypeStruct((T, 2 * C), BF16), jax.ShapeDtypeStruct((KW, C), F32),
                   jax.ShapeDtypeStruct((1, C), F32), jax.ShapeDtypeStruct((1, C), F32),
                   jax.ShapeDtypeStruct((1, C), F32)],
        scratch_shapes=[pltpu.VMEM((SUBLANES, HALO + tm, C), F32), pltpu.VMEM((SUBLANES, tm + HALO, C), F32)],
        compiler_params=_params("arbitrary"),
    )(glu, glu, u, u, dc, dc, conv_w, ln_g, ln_b, dep)


def _attn_bwd(qkv, o, do, lse, sinks, dep):
    def body(sink_ref, qkv_ref, o_ref, do_ref, lse_ref, dep_ref, dq_ref, dkv_ref, ds_ref,
             s_ref, dp_ref, p_ref, dsb_ref):
        i = pl.program_id(0)

        @pl.when(i == 0)
        def _():
            dkv_ref[...] = jnp.zeros_like(dkv_ref)
            ds_ref[...] = jnp.zeros_like(ds_ref)

        r0 = pl.multiple_of(i * BLK, BLK)
        rb, valid = _band(i)
        for g in range(NKV):
            kband = qkv_ref[pl.ds(rb, BAND), AW + g * HD:AW + (g + 1) * HD]
            vband = qkv_ref[pl.ds(rb, BAND), AW + KVW + g * HD:AW + KVW + (g + 1) * HD]
            lse_parts, dl_parts = [], []
            for hh in range(GROUP):
                h = g * GROUP + hh
                hcol = slice(h * HD, (h + 1) * HD)
                doh = do_ref[:, hcol]
                s_ref[hh * BLK:(hh + 1) * BLK, :] = _dot_nt(qkv_ref[pl.ds(r0, BLK), hcol], kband)
                dp_ref[hh * BLK:(hh + 1) * BLK, :] = _dot_nt(doh, vband)
                lse_parts.append(lse_ref[:, h:h + 1])
                dl_parts.append(jnp.sum(doh.astype(F32) * o_ref[:, hcol].astype(F32), axis=-1, keepdims=True))
            lse = jnp.concatenate(lse_parts, axis=0)
            dl = jnp.concatenate(dl_parts, axis=0)
            p = jnp.where(valid, jnp.exp(s_ref[...] * SCALE - lse), 0.0)
            p_ref[...] = p.astype(BF16)
            dsb_ref[...] = (p * (dp_ref[...] - dl)).astype(BF16)
            dsink = -(jnp.exp(_sink_column(sink_ref, g) - lse) * dl)
            dk = jnp.zeros((BAND, HD), F32)
            dv = jnp.zeros((BAND, HD), F32)
            for hh in range(GROUP):
                h = g * GROUP + hh
                hcol = slice(h * HD, (h + 1) * HD)
                rows = slice(hh * BLK, (hh + 1) * BLK)
                dq_ref[:, hcol] = (_dot(dsb_ref[rows, :], kband) * SCALE).astype(BF16)
                dk = dk + _dot_tn(dsb_ref[rows, :], qkv_ref[pl.ds(r0, BLK), hcol])
                dv = dv + _dot_tn(p_ref[rows, :], do_ref[:, hcol])
                ds_ref[h:h + 1, :] += jnp.broadcast_to(jnp.sum(dsink[rows], axis=0, keepdims=True), (1, 128))
            dkv_ref[pl.ds(rb, BAND), g * HD:(g + 1) * HD] += dk * SCALE
            dkv_ref[pl.ds(rb, BAND), KVW + g * HD:KVW + (g + 1) * HD] += dv

    return pl.pallas_call(
        body, name="attn_bwd", grid=(T // BLK,),
        in_specs=[pl.BlockSpec(memory_space=pltpu.SMEM), _whole((T, QKVW)),
                  _rows(BLK, AW), _rows(BLK, AW), _rows(BLK, NQ), _whole((8, 128))],
        out_specs=[_rows(BLK, AW), _whole((T, 2 * KVW)), _whole((NQ, 128))],
        out_shape=[jax.ShapeDtypeStruct((T, AW), BF16), jax.ShapeDtypeStruct((T, 2 * KVW), F32),
                   jax.ShapeDtypeStruct((NQ, 128), F32)],
        scratch_shapes=[pltpu.VMEM((GROWS, BAND), F32), pltpu.VMEM((GROWS, BAND), F32),
                        pltpu.VMEM((GROWS, BAND), BF16), pltpu.VMEM((GROWS, BAND), BF16)],
        compiler_params=_params("arbitrary"),
    )(sinks, qkv, o, do, lse, dep)


PROJ_PARTS = [(0, AW), (AW, QKVW), (GLU_OFF, GATE_OFF), (GATE_OFF, INW)]


def _in_proj_bwd(dq, dkv, dglu, dgl, x, dx1, g_mix, w_in_t, dep):
    tm = ROW_TM

    def body(dq_ref, dkv_ref, dglu_ref, dgl_ref, x_ref, dx1_ref, g_ref, w_ref, dep_ref,
             gx_ref, dg_ref, db_ref):
        i = pl.program_id(0)

        @pl.when(i == 0)
        def _():
            dg_ref[...] = jnp.zeros_like(dg_ref)
            db_ref[...] = jnp.zeros_like(db_ref)

        dh = jnp.zeros((tm, D), F32)
        for part_ref, (lo, hi) in zip((dq_ref, dkv_ref, dglu_ref, dgl_ref), PROJ_PARTS):
            part = part_ref[...]
            dh = dh + _dot(part.astype(BF16), w_ref[lo:hi, :])
            db_ref[:, lo:hi] += jnp.sum(part.astype(F32), axis=0, keepdims=True)
        xv = x_ref[...]
        r = lax.rsqrt(jnp.mean(xv * xv, axis=-1, keepdims=True) + EPS)
        xh = xv * r
        dhg = dh * g_ref[...]
        gx_ref[...] = dx1_ref[...] + r * (dhg - xh * jnp.mean(dhg * xh, axis=-1, keepdims=True))
        dg_ref[...] += jnp.sum(dh * xh, axis=0, keepdims=True)

    return pl.pallas_call(
        body, name="in_proj_bwd", grid=(T // tm,),
        in_specs=[_rows(tm, AW), _rows(tm, 2 * KVW), _rows(tm, 2 * C), _rows(tm, 2 * D),
                  _rows(tm, D), _rows(tm, D), _whole((1, D)), _const((INW, D)), _whole((8, 128))],
        out_specs=[_rows(tm, D), _whole((1, D)), _whole((1, INW))],
        out_shape=[jax.ShapeDtypeStruct((T, D), F32), jax.ShapeDtypeStruct((1, D), F32),
                   jax.ShapeDtypeStruct((1, INW), F32)],
        compiler_params=_params("arbitrary"),
    )(dq, dkv, dglu, dgl, x, dx1, g_mix, w_in_t, dep)


def _grad_w_in_t(h, dq, dkv, dglu, dgl):
    tn, chunk = 512, 256

    def body(h_ref, dq_ref, dkv_ref, dglu_ref, dgl_ref, o_ref, pt_ref):
        @pl.when(pl.program_id(0) == 0)
        def _():
            for part_ref, (lo, hi) in zip((dq_ref, dkv_ref, dglu_ref, dgl_ref), PROJ_PARTS):
                for c0 in range(0, hi - lo, chunk):
                    pt_ref[lo + c0:lo + c0 + chunk, :] = part_ref[:, c0:c0 + chunk].astype(BF16).T

        hv = h_ref[...]
        for r0 in range(0, INW, QKVW):
            o_ref[r0:r0 + QKVW, :] = _dot(pt_ref[r0:r0 + QKVW, :], hv).astype(BF16)

    return pl.pallas_call(
        body, name="grad_w_in", grid=(D // tn,),
        in_specs=[pl.BlockSpec((T, tn), lambda j: (0, j)), _const((T, AW)), _const((T, 2 * KVW)),
                  _const((T, 2 * C)), _const((T, 2 * D))],
        out_specs=pl.BlockSpec((INW, tn), lambda j: (0, j)),
        out_shape=jax.ShapeDtypeStruct((INW, D), BF16),
        scratch_shapes=[pltpu.VMEM((INW, T), BF16)],
        compiler_params=_params("arbitrary"),
    )(h, dq, dkv, dglu, dgl)


def _grad_w(a, b, name, tk, tn, col_sharded):
    k, n = a.shape[1], b.shape[1]

    def body(a_ref, b_ref, o_ref, at_ref):
        @pl.when(pl.program_id(1) == 0)
        def _():
            at_ref[...] = a_ref[...].T

        o_ref[...] = _dot(at_ref[...], b_ref[...]).astype(BF16)

    if col_sharded:
        per = n // N_CHIPS // tn
        shape = (N_CHIPS, k, n // N_CHIPS)
        out_map = lambda i, j: (j // per, i, j % per)
    else:
        shape = (1, k, n)
        out_map = lambda i, j: (0, i, j)
    out = pl.pallas_call(
        body, name=name, grid=(k // tk, n // tn),
        in_specs=[pl.BlockSpec((T, tk), lambda i, j: (0, i)), pl.BlockSpec((T, tn), lambda i, j: (0, j))],
        out_specs=pl.BlockSpec((None, tk, tn), out_map),
        out_shape=jax.ShapeDtypeStruct(shape, BF16),
        scratch_shapes=[pltpu.VMEM((tk, T), BF16)],
        compiler_params=_params("parallel", "arbitrary"),
    )(a, b)
    return out if col_sharded else out.reshape(N_CHIPS, k // N_CHIPS, n)


HBM_SPEC = pl.BlockSpec(memory_space=pltpu.HBM)


def _place():
    x, y, c = lax.axis_index("x"), lax.axis_index("y"), lax.axis_index("c")
    chips = [(1 - x, y), (x, 1 - y), (1 - x, 1 - y)]
    return x, y, c, chips


SEM_SPEC = pl.BlockSpec(memory_space=pltpu.SEMAPHORE)
ANY_SPEC = pl.BlockSpec(memory_space=pl.ANY)
VMEM_SPEC = pl.BlockSpec(memory_space=pltpu.VMEM)
EFFECT = pltpu.SideEffectType.DATAFLOW_SIDE_EFFECTING


def _gather_ends(src, land, x, y, c, chips):
    kh = src.shape[0] // 2
    s_me = 2 * x + y
    ends = [(src.at[pl.ds(c * kh, kh)], land.at[s_me, pl.ds(c * kh, kh)], (*chip, c)) for chip in chips]
    return ends + [(src, land.at[s_me], (x, y, 1 - c))]


def _reduce_ends(src, land, x, y, c, chips):
    return [(src.at[2 * chip[0] + chip[1]], land.at[j], (*chip, c)) for j, chip in enumerate(chips)]


def _chip_copies(ends, srcs, lands, send_sems, recv_sems, first=0):
    x, y, c, chips = _place()
    copies = []
    for src, land in zip(srcs, lands):
        peers = ends(src, land, x, y, c, chips)
        for s, d, to in peers:
            k = first * len(peers) + len(copies)
            copies.append(pltpu.make_async_remote_copy(
                src_ref=s, dst_ref=d, send_sem=send_sems.at[k], recv_sem=recv_sems.at[k],
                device_id=to, device_id_type=MESH))
    return copies


GATHER_PEERS, REDUCE_PEERS = 4, 3


def _chip_start(name, ends, peers, srcs, lands):
    n = len(srcs)

    def body(*refs):
        copies = _chip_copies(ends, refs[:n], refs[n:2 * n], refs[2 * n], refs[2 * n + 1])
        for cp in copies:
            cp.start()
        token = refs[-1]
        token[...] = jnp.zeros_like(token)

    hbm = lambda a: pltpu.HBM(a.shape, a.dtype)
    res = pl.pallas_call(
        body, name=name,
        out_shape=(pltpu.SemaphoreType.DMA((peers * n,)), pltpu.SemaphoreType.DMA((peers * n,)),
                   *[hbm(a) for a in srcs], *[hbm(a) for a in lands],
                   jax.ShapeDtypeStruct((8, 128), F32)),
        in_specs=[HBM_SPEC] * (2 * n),
        out_specs=(SEM_SPEC, SEM_SPEC, *[HBM_SPEC] * (2 * n), VMEM_SPEC),
        input_output_aliases={i: 2 + i for i in range(2 * n)},
        compiler_params=pltpu.CompilerParams(has_side_effects=EFFECT),
    )(*[pltpu.with_memory_space_constraint(a, pltpu.HBM) for a in (*srcs, *lands)])
    return res[0], res[1], list(res[2:2 + n]), list(res[2 + n:2 + 2 * n]), res[-1]


def _chip_wait(name, ends, send_sems, recv_sems, srcs, lands, after, first=0):
    n, na = len(srcs), len(after)

    def body(*refs):
        copies = _chip_copies(ends, refs[:n], refs[n:2 * n], refs[2 * n], refs[2 * n + 1], first)
        for cp in copies:
            cp.wait_send()
            cp.wait_recv()

    hbm = lambda a: pltpu.HBM(a.shape, a.dtype)
    res = pl.pallas_call(
        body, name=name,
        out_shape=tuple(hbm(a) for a in (*srcs, *lands)),
        in_specs=[HBM_SPEC] * (2 * n) + [SEM_SPEC, SEM_SPEC] + [ANY_SPEC] * na,
        out_specs=tuple([HBM_SPEC] * (2 * n)),
        input_output_aliases={i: i for i in range(2 * n)},
        compiler_params=pltpu.CompilerParams(has_side_effects=EFFECT),
    )(*srcs, *lands, send_sems, recv_sems, *after)
    return list(res[:n]), list(res[n:])


def _forward_copies(lands, send_sems, recv_sems):
    x, y, c, chips = _place()
    copies = []
    for land in lands:
        kh = land.shape[1] // 2
        for chip in chips:
            blk = land.at[2 * chip[0] + chip[1], pl.ds(c * kh, kh)]
            k = len(copies)
            copies.append(pltpu.make_async_remote_copy(
                src_ref=blk, dst_ref=blk, send_sem=send_sems.at[k], recv_sem=recv_sems.at[k],
                device_id=(x, y, 1 - c), device_id_type=MESH))
    return copies


def _gather_relay(name, send_sems, recv_sems, srcs, lands, after, first):
    n, na = len(srcs), len(after)

    def body(*refs):
        land_refs = refs[n:2 * n]
        for cp in _chip_copies(_gather_ends, refs[:n], land_refs, refs[2 * n], refs[2 * n + 1], first):
            cp.wait_send()
            cp.wait_recv()
        out = refs[2 * n + 2 + na:]
        for cp in _forward_copies(land_refs, out[0], out[1]):
            cp.start()
        out[-1][...] = jnp.zeros_like(out[-1])

    hbm = lambda a: pltpu.HBM(a.shape, a.dtype)
    res = pl.pallas_call(
        body, name=name,
        out_shape=(pltpu.SemaphoreType.DMA((3 * n,)), pltpu.SemaphoreType.DMA((3 * n,)),
                   *[hbm(a) for a in lands], jax.ShapeDtypeStruct((8, 128), F32)),
        in_specs=[HBM_SPEC] * (2 * n) + [SEM_SPEC, SEM_SPEC] + [ANY_SPEC] * na,
        out_specs=(SEM_SPEC, SEM_SPEC, *[HBM_SPEC] * n, VMEM_SPEC),
        input_output_aliases={n + i: 2 + i for i in range(n)},
        compiler_params=pltpu.CompilerParams(has_side_effects=EFFECT),
    )(*srcs, *lands, send_sems, recv_sems, *after)
    return res[0], res[1], list(res[2:2 + n]), res[-1]


def _forward_wait(name, send_sems, recv_sems, lands, after):
    n, na = len(lands), len(after)

    def body(*refs):
        for cp in _forward_copies(refs[:n], refs[n], refs[n + 1]):
            cp.wait_send()
            cp.wait_recv()

    hbm = lambda a: pltpu.HBM(a.shape, a.dtype)
    res = pl.pallas_call(
        body, name=name,
        out_shape=tuple(hbm(a) for a in lands),
        in_specs=[HBM_SPEC] * n + [SEM_SPEC, SEM_SPEC] + [ANY_SPEC] * na,
        out_specs=tuple([HBM_SPEC] * n),
        input_output_aliases={i: i for i in range(n)},
        compiler_params=pltpu.CompilerParams(has_side_effects=EFFECT),
    )(*lands, send_sems, recv_sems, *after)
    return list(res)


def _exchange_ends(src, land, x, y, c, chips):
    kh = src.shape[1] // 2
    return [(src.at[:, pl.ds((1 - c) * kh, kh)], land, (x, y, 1 - c))]


def _share_ends(src, land, x, y, c, chips):
    return [(src, land, (x, y, 1 - c))]


def _small_ends(src, land, x, y, c, chips):
    m = src.shape[0]
    rows = land.at[pl.ds((4 * x + 2 * y + c) * m, m)]
    peers = [(x, y, 1 - c)] + [(*chip, c) for chip in chips] + [(*chip, 1 - c) for chip in chips]
    return [(src, rows, to) for to in peers]


PAIR_PEERS, SMALL_PEERS = 1, 7


def _row_tile(k):
    for t in (256, 240, 128, 176, 64, 32, 16):
        if k % t == 0:
            return t
    raise ValueError(k)


def _pair_sum(c_idx, g, got, name):
    _, k, n = g.shape
    kh = k // 2
    tm = _row_tile(kh)
    nb = kh // tm

    def body(c_ref, g_ref, r_ref, o_ref):
        o_ref[...] = (g_ref[...].astype(F32) + r_ref[...].astype(F32)).astype(BF16)

    return pl.pallas_call(
        body, name=name,
        grid_spec=pltpu.PrefetchScalarGridSpec(
            num_scalar_prefetch=1, grid=(N_CHIPS, nb),
            in_specs=[pl.BlockSpec((1, tm, n), lambda s, i, c_ref: (s, c_ref[0] * nb + i, 0)),
                      pl.BlockSpec((1, tm, n), lambda s, i, c_ref: (s, i, 0))],
            out_specs=pl.BlockSpec((1, tm, n), lambda s, i, c_ref: (s, i, 0))),
        out_shape=jax.ShapeDtypeStruct((N_CHIPS, kh, n), BF16),
        compiler_params=_params("parallel", "parallel"),
    )(c_idx, g, got)


def _chip_sum(s_idx, mine, got, name):
    _, kh, n = mine.shape
    tm = _row_tile(kh)

    def body(s_ref, m_ref, r_ref, o_ref):
        acc = m_ref[0].astype(F32)
        for j in range(3):
            acc = acc + r_ref[j].astype(F32)
        o_ref[...] = acc

    return pl.pallas_call(
        body, name=name,
        grid_spec=pltpu.PrefetchScalarGridSpec(
            num_scalar_prefetch=1, grid=(kh // tm,),
            in_specs=[pl.BlockSpec((1, tm, n), lambda i, s_ref: (s_ref[0], i, 0)),
                      pl.BlockSpec((3, tm, n), lambda i, s_ref: (0, i, 0))],
            out_specs=pl.BlockSpec((tm, n), lambda i, s_ref: (i, 0))),
        out_shape=jax.ShapeDtypeStruct((kh, n), F32),
        compiler_params=_params("parallel"),
    )(s_idx, mine, got)


def _adamw_math(w, g, m, v):
    m = ADAM_B1 * m + (1.0 - ADAM_B1) * g
    v = ADAM_B2 * v + (1.0 - ADAM_B2) * (g * g)
    m_hat = m / (1.0 - ADAM_B1 ** ADAM_STEP)
    v_hat = v / (1.0 - ADAM_B2 ** ADAM_STEP)
    delta = -ADAM_LR * (m_hat / (jnp.sqrt(v_hat) + ADAM_EPS) + ADAM_WD * w)
    return delta, m, v


def _adamw(c_idx, w, g_mine, g_other, m, v, name):
    k, n = w.shape
    tm = k // 4

    def body(c_ref, w_ref, gm_ref, go_ref, m_ref, v_ref, g_ref, d_ref, mo_ref, vo_ref):
        g = jnp.where(pl.program_id(0) == c_ref[0], gm_ref[...], go_ref[...])
        d, mm, vv = _adamw_math(w_ref[...], g, m_ref[...], v_ref[...])
        g_ref[...] = g
        d_ref[...] = d
        mo_ref[...] = mm
        vo_ref[...] = vv

    full = pl.BlockSpec((tm, n), lambda h, i, c_ref: (2 * h + i, 0))
    mine = pl.BlockSpec((tm, n), lambda h, i, c_ref: (jnp.where(h == c_ref[0], i, 0), 0))
    other = pl.BlockSpec((tm, n), lambda h, i, c_ref: (jnp.where(h == c_ref[0], 0, i), 0))
    shp = jax.ShapeDtypeStruct((k, n), F32)
    return pl.pallas_call(
        body, name=name,
        grid_spec=pltpu.PrefetchScalarGridSpec(
            num_scalar_prefetch=1, grid=(2, 2),
            in_specs=[full, mine, other, full, full], out_specs=[full] * 4),
        out_shape=[shp] * 4, compiler_params=_params("arbitrary", "arbitrary"),
    )(c_idx, w, g_mine, g_other, m, v)


VEC_SLOTS = {
    "g_mix_norm": (0, 0, D), "b_conv_proj": (0, D, D), "g_ffn_norm": (0, 2 * D, D),
    "g_final": (0, 3 * D, D), "b_in": (1, 0, INW), "conv_b": (2, 0, C), "ln_g": (2, C, C),
    "ln_b": (2, 2 * C, C), "sinks": (2, 3 * C, NQ), "loss": (2, 3 * C + 128, 1),
}
VEC_ROWS, VEC_COLS = 8, 4 * D
CW_ROWS = 32
SMALL_NAMES = ["g_mix_norm", "b_in", "sinks", "conv_w", "conv_b", "ln_g", "ln_b",
               "b_conv_proj", "g_ffn_norm", "g_final"]
CW_LANES = C // N_CHIPS


def _pack_small(gs, loss):
    row0 = jnp.concatenate([gs["g_mix_norm"], gs["b_conv_proj"], gs["g_ffn_norm"], gs["g_final"]], axis=1)
    row1 = jnp.pad(gs["b_in"], ((0, 0), (0, VEC_COLS - INW)))
    row2 = jnp.concatenate([gs["conv_b"], gs["ln_g"], gs["ln_b"],
                            jnp.pad(gs["sinks"], ((0, 0), (0, 128 - NQ))),
                            jnp.pad(loss.reshape(1, 1), ((0, 0), (0, VEC_COLS - 3 * C - 129)))], axis=1)
    vec = jnp.concatenate([row0, row1, row2, jnp.zeros((VEC_ROWS - 3, VEC_COLS), F32)], axis=0)
    cw = jnp.pad(gs["conv_w"], ((0, CW_ROWS - KW), (0, 0)))
    return vec, cw


def _small_update(s_idx, vec_all, cw_all, wmv):
    nsm = len(SMALL_NAMES)

    def body(s_ref, vec_ref, cw_ref, *refs):
        ins = refs[:3 * nsm]
        outs = refs[3 * nsm:7 * nsm]
        loss_ref = refs[7 * nsm]

        def total(slot):
            row, lane, width = slot
            acc = vec_ref[row:row + 1, lane:lane + width]
            for k in range(1, 8):
                acc = acc + vec_ref[k * VEC_ROWS + row:k * VEC_ROWS + row + 1, lane:lane + width]
            return acc

        loss_ref[...] = jnp.broadcast_to(total(VEC_SLOTS["loss"]), loss_ref.shape)
        for p, name in enumerate(SMALL_NAMES):
            w_ref, m_ref, v_ref = ins[3 * p:3 * p + 3]
            g_ref, d_ref, mo_ref, vo_ref = outs[4 * p:4 * p + 4]
            if name == "conv_w":
                g = jnp.zeros((KW, CW_LANES), F32)
                for s in range(N_CHIPS):
                    cand = cw_ref[0:KW, s * CW_LANES:(s + 1) * CW_LANES]
                    for k in range(1, 8):
                        cand = cand + cw_ref[k * CW_ROWS:k * CW_ROWS + KW, s * CW_LANES:(s + 1) * CW_LANES]
                    g = jnp.where(s_ref[0] == s, cand, g)
            else:
                g = total(VEC_SLOTS[name])
            d, mm, vv = _adamw_math(w_ref[...], g, m_ref[...], v_ref[...])
            g_ref[...] = g
            d_ref[...] = d
            mo_ref[...] = mm
            vo_ref[...] = vv

    vmem = pl.BlockSpec(memory_space=pltpu.VMEM)
    flat = [a for t in wmv for a in t]
    out_shape = []
    for w, _, _ in wmv:
        out_shape += [jax.ShapeDtypeStruct(w.shape, F32)] * 4
    out_shape.append(jax.ShapeDtypeStruct((1, 128), F32))
    res = pl.pallas_call(
        body, name="small_update",
        in_specs=[pl.BlockSpec(memory_space=pltpu.SMEM)] + [vmem] * (2 + len(flat)),
        out_specs=[vmem] * len(out_shape), out_shape=out_shape,
    )(s_idx, vec_all, cw_all, *flat)
    return [tuple(res[4 * p:4 * p + 4]) for p in range(nsm)], res[4 * nsm]


WEIGHT_ORDER = ["g_mix_norm", "w_in", "b_in", "sinks", "conv_w", "conv_b", "ln_g", "ln_b",
                "w_attn_proj", "w_conv_proj", "b_conv_proj", "w_out", "g_ffn_norm", "w_ffn_in",
                "w_ffn_down", "g_final"]


def kernel(x, g_mix_norm, w_in, b_in, sinks, conv_w, conv_b, ln_g, ln_b, w_attn_proj, w_conv_proj, b_conv_proj, w_out, g_ffn_norm, w_ffn_in, w_ffn_down, g_final, loss_target, m_g_mix_norm, m_w_in, m_b_in, m_sinks, m_conv_w, m_conv_b, m_ln_g, m_ln_b, m_w_attn_proj, m_w_conv_proj, m_b_conv_proj, m_w_out, m_g_ffn_norm, m_w_ffn_in, m_w_ffn_down, m_g_final, v_g_mix_norm, v_w_in, v_b_in, v_sinks, v_conv_w, v_conv_b, v_ln_g, v_ln_b, v_w_attn_proj, v_w_conv_proj, v_b_conv_proj, v_w_out, v_g_ffn_norm, v_w_ffn_in, v_w_ffn_down, v_g_final):
    w = dict(g_mix_norm=g_mix_norm, w_in=w_in, b_in=b_in, sinks=sinks, conv_w=conv_w, conv_b=conv_b,
             ln_g=ln_g, ln_b=ln_b, w_attn_proj=w_attn_proj, w_conv_proj=w_conv_proj,
             b_conv_proj=b_conv_proj, w_out=w_out, g_ffn_norm=g_ffn_norm, w_ffn_in=w_ffn_in,
             w_ffn_down=w_ffn_down, g_final=g_final)
    m = dict(g_mix_norm=m_g_mix_norm, w_in=m_w_in, b_in=m_b_in, sinks=m_sinks, conv_w=m_conv_w,
             conv_b=m_conv_b, ln_g=m_ln_g, ln_b=m_ln_b, w_attn_proj=m_w_attn_proj,
             w_conv_proj=m_w_conv_proj, b_conv_proj=m_b_conv_proj, w_out=m_w_out,
             g_ffn_norm=m_g_ffn_norm, w_ffn_in=m_w_ffn_in, w_ffn_down=m_w_ffn_down, g_final=m_g_final)
    v = dict(g_mix_norm=v_g_mix_norm, w_in=v_w_in, b_in=v_b_in, sinks=v_sinks, conv_w=v_conv_w,
             conv_b=v_conv_b, ln_g=v_ln_g, ln_b=v_ln_b, w_attn_proj=v_w_attn_proj,
             w_conv_proj=v_w_conv_proj, b_conv_proj=v_b_conv_proj, w_out=v_w_out,
             g_ffn_norm=v_g_ffn_norm, w_ffn_in=v_w_ffn_in, w_ffn_down=v_w_ffn_down, g_final=v_g_final)

    c_idx = lax.axis_index("c").astype(jnp.int32).reshape(1)
    s_idx = (2 * lax.axis_index("x") + lax.axis_index("y")).astype(jnp.int32).reshape(1)

    out_g, out_d, out_m, out_v = {}, {}, {}, {}

    def gather_start(tag, shards):
        lands = [lax.empty((N_CHIPS,) + s.shape, s.dtype) for s in shards]
        return _chip_start("gather_start_" + tag, _gather_ends, GATHER_PEERS, shards, lands)

    def gather_relay(tag, state, after, first=0, count=None):
        send_sems, recv_sems, shards, lands, _ = state
        last = len(shards) if count is None else first + count
        return _gather_relay("gather_relay_" + tag, send_sems, recv_sems, shards[first:last],
                             lands[first:last], after, first)

    def gather_finish(tag, relay, after):
        return _forward_wait("forward_wait_" + tag, relay[0], relay[1], relay[2], after)

    names_b = ["w_attn_proj", "w_conv_proj", "w_out", "w_ffn_in", "w_ffn_down"]
    big = {name: (w[name][0], m[name][0], v[name][0]) for name in names_b}
    big["w_in"] = (w_in[0].T, m_w_in[0].T, v_w_in[0].T)
    state_a = gather_start("a", [big["w_in"][0].astype(BF16), jnp.pad(conv_w[0], ((0, CW_ROWS - KW), (0, 0)))])
    state_b = gather_start("b", [(big[name][0] + state_a[4][0, 0]).astype(BF16) for name in names_b])
    got_a = gather_finish("a", gather_relay("a", state_a, [state_b[4]]), [])
    w_in_t_full = got_a[0].reshape(INW, D)
    conv_w_full = got_a[1].transpose(1, 0, 2).reshape(CW_ROWS, C)[:KW]

    xs, target = x[0], loss_target[0]
    g_final2 = g_final.reshape(1, D)
    h, qkv, glu, gl = _in_proj(xs, g_mix_norm, w_in_t_full, b_in)
    o, lse = _attn_fwd(qkv, sinks)
    relay_1 = gather_relay("b1", state_b, [o], 0, 3)
    u, cact = _conv_fwd(glu, conv_w_full, conv_b, ln_g, ln_b, relay_1[3])
    w_ap4, w_cp4, w_out4 = gather_finish("b1", relay_1, [cact])
    w_out_full = w_out4.reshape(D, D)
    relay_2 = gather_relay("b2", state_b, [cact], 3, 1)
    ya, yc, mg, x1 = _mix_out(xs, o, cact, gl, w_ap4, w_cp4, b_conv_proj, w_out_full, relay_2[3])
    w_fi4, = gather_finish("b2", relay_2, [x1])
    relay_3 = gather_relay("b3", state_b, [x1], 4, 1)
    h2, gu, act = _ffn_in(x1, g_ffn_norm, w_fi4, relay_3[3])
    w_dn4, = gather_finish("b3", relay_3, [act])
    w_dn_full = w_dn4.reshape(DFF, D)
    dx2, dx2b, dg_final, loss_part = _ffn_out_loss(x1, act, w_dn_full, g_final2, target)

    def exchange_start(tag, grads):
        lands = [lax.empty((N_CHIPS, g.shape[1] // 2, g.shape[2]), g.dtype) for g in grads]
        return _chip_start("pair_start_" + tag, _exchange_ends, PAIR_PEERS, grads, lands)

    def reduce_start(tag, names, exchange, after):
        send_sems, recv_sems, grads, lands, _ = exchange
        grads, from_sibling = _chip_wait("pair_wait_" + tag, _exchange_ends, send_sems, recv_sems, grads, lands, after)
        pair = [_pair_sum(c_idx, g, r, "pair_sum_" + name) for name, g, r in zip(names, grads, from_sibling)]
        lands = [lax.empty((3,) + p.shape[1:], p.dtype) for p in pair]
        return _chip_start("chip_start_" + tag, _reduce_ends, REDUCE_PEERS, pair, lands)

    def reduce_sum(tag, names, state, after):
        send_sems, recv_sems, pair, lands, _ = state
        pair, lands = _chip_wait("chip_wait_" + tag, _reduce_ends, send_sems, recv_sems, pair, lands, after)
        mine = [_chip_sum(s_idx, p, r, "chip_sum_" + name) for name, p, r in zip(names, pair, lands)]
        others = [lax.empty(a.shape, a.dtype) for a in mine]
        return _chip_start("share_start_" + tag, _share_ends, PAIR_PEERS, mine, others)

    def reduce_finish(tag, names, share, after):
        send_sems, recv_sems, mine, others, _ = share
        mine, others = _chip_wait("share_wait_" + tag, _share_ends, send_sems, recv_sems, mine, others, after)
        for name, g_mine, g_other in zip(names, mine, others):
            wv, mv, vv = big[name]
            res = _adamw(c_idx, wv, g_mine, g_other, mv, vv, "adamw_" + name)
            if name == "w_in":
                res = [a.T for a in res]
            out_g[name], out_d[name], out_m[name], out_v[name] = [a[None] for a in res]

    dgu, dx1, dx1b, dg_ffn = _ffn_bwd(dx2, dx2b, gu, x1, g_ffn_norm, w_dn_full, w_fi4)
    names_1 = ["w_ffn_in", "w_ffn_down", "w_out", "w_attn_proj", "w_conv_proj"]
    grads_1 = [_grad_w(h2, dgu, "grad_w_ffn_in", 512, FSH, True),
               _grad_w(act, dx2b, "grad_w_ffn_down", 256, D, False)]
    dya, dyc, dgl, do, dc, db_cp = _mix_bwd(dx1b, gl, ya, yc, w_out_full, w_ap4, w_cp4, relay_3[3])
    grads_1 += [_grad_w(mg, dx1b, "grad_w_out", 512, D, False),
                _grad_w(o, dya, "grad_w_attn_proj", 512, 256, True),
                _grad_w(cact, dyc, "grad_w_conv_proj", 512, 256, True)]
    exchange_1 = exchange_start("1", grads_1)
    dglu, dconv_w, dconv_b, dln_g, dln_b = _conv_bwd(glu, u, dc, conv_w_full, ln_g, ln_b, exchange_1[4])
    state_1 = reduce_start("1", names_1, exchange_1, [dglu])
    dq, dkv, dsinks = _attn_bwd(qkv, o, do, lse, sinks, state_1[4])
    names_2 = ["w_in"]
    gw_in_t = _grad_w_in_t(h, dq, dkv, dglu, dgl)
    exchange_2 = exchange_start("2", [gw_in_t.reshape(N_CHIPS, INW // N_CHIPS, D)])
    grad_x, dg_mix, db_in = _in_proj_bwd(dq, dkv, dglu, dgl, xs, dx1, g_mix_norm, w_in_t_full, exchange_2[4])

    gs = {"g_mix_norm": dg_mix, "b_in": db_in, "sinks": dsinks[:, 0].reshape(1, NQ),
          "conv_w": dconv_w, "conv_b": dconv_b, "ln_g": dln_g, "ln_b": dln_b,
          "b_conv_proj": db_cp, "g_ffn_norm": dg_ffn, "g_final": dg_final}
    blocks = list(_pack_small(gs, loss_part[0, 0]))
    tables = [lax.empty((8 * b.shape[0], b.shape[1]), b.dtype) for b in blocks]
    small = _chip_start("small_start", _small_ends, SMALL_PEERS, blocks, tables)

    state_2 = reduce_start("2", names_2, exchange_2, [grad_x, small[4]])
    share_1 = reduce_sum("1", names_1, state_1, [state_2[4]])
    reduce_finish("1", names_1, share_1, [state_2[4]])
    blocks, tables = _chip_wait("small_wait", _small_ends, small[0], small[1], small[2], small[3],
                                [out_d["w_ffn_down"]])
    me = 4 * lax.axis_index("x") + 2 * lax.axis_index("y") + lax.axis_index("c")
    vec_all, cw_all = [lax.dynamic_update_slice(t, b, (me * b.shape[0], 0)) for t, b in zip(tables, blocks)]
    share_2 = reduce_sum("2", names_2, state_2, [vec_all])

    def view(a, name):
        if name == "conv_w":
            return a[0]
        if name == "g_final":
            return a.reshape(1, D)
        return a

    wmv = [(view(w[name], name), view(m[name], name), view(v[name], name)) for name in SMALL_NAMES]
    small_out, loss_row = _small_update(s_idx, vec_all, cw_all, wmv)
    for name, (g, d, mm, vv) in zip(SMALL_NAMES, small_out):
        shape = w[name].shape
        out_g[name], out_d[name], out_m[name], out_v[name] = (
            g.reshape(shape), d.reshape(shape), mm.reshape(shape), vv.reshape(shape))

    reduce_finish("2", names_2, share_2, [loss_row])

    loss = loss_row[0, 0]
    return (loss, grad_x[None], *[out_g[k] for k in WEIGHT_ORDER], *[out_d[k] for k in WEIGHT_ORDER],
            *[out_m[k] for k in WEIGHT_ORDER], *[out_v[k] for k in WEIGHT_ORDER])
```

```python
import functools

import jax
import jax.numpy as jnp
from jax import lax
from jax.experimental import pallas as pl
from jax.experimental.pallas import tpu as pltpu

F32 = jnp.float32
BF16 = jnp.bfloat16

T = 2048
D = 1024
HD = 64
NQ = 8
NKV = 2
GROUP = NQ // NKV
BLK = 128
AW = NQ * HD
KVW = NKV * HD
C = 512
KW = 31
QKVW = AW + 2 * KVW
GLU_OFF = QKVW
GATE_OFF = GLU_OFF + 2 * C
INW = GATE_OFF + 2 * D
DFF = 2816
EPS = 1e-5
NEG = -1e30
SCALE = HD ** -0.5
HALO = 32
N_CHIPS = 4
FSH = 2 * DFF // N_CHIPS

ADAM_LR = 0.001
ADAM_B1 = 0.9
ADAM_B2 = 0.999
ADAM_EPS = 1e-08
ADAM_WD = 0.01
ADAM_STEP = 10

VMEM_LIMIT = 56 * 1024 * 1024
ROW_TM = 512
MESH = pl.DeviceIdType.MESH


def _params(*sem):
    return pltpu.CompilerParams(dimension_semantics=sem, vmem_limit_bytes=VMEM_LIMIT)


def _dot(a, b):
    return jnp.dot(a, b, preferred_element_type=F32)


def _dot_nt(a, b):
    return lax.dot_general(a, b, (((1,), (1,)), ((), ())), preferred_element_type=F32)


def _dot_tn(a, b):
    return lax.dot_general(a, b, (((0,), (0,)), ((), ())), preferred_element_type=F32)


def _sigmoid(v):
    return 1.0 / (1.0 + jnp.exp(-v))


def _rows(tm, n):
    return pl.BlockSpec((tm, n), lambda i: (i, 0))


def _whole(shape):
    return pl.BlockSpec(shape, lambda i: tuple(0 for _ in shape))


def _in_proj(x, g_mix, w_in_t, b_in):
    tm = ROW_TM

    def body(x_ref, g_ref, w_ref, b_ref, h_ref, qkv_ref, glu_ref, gl_ref):
        xv = x_ref[...]
        r = lax.rsqrt(jnp.mean(xv * xv, axis=-1, keepdims=True) + EPS)
        h = (xv * r * g_ref[...]).astype(BF16)
        h_ref[...] = h
        qkv_ref[...] = (_dot_nt(h, w_ref[0:GLU_OFF, :]) + b_ref[:, 0:GLU_OFF]).astype(BF16)
        glu_ref[...] = (_dot_nt(h, w_ref[GLU_OFF:GATE_OFF, :]) + b_ref[:, GLU_OFF:GATE_OFF]).astype(BF16)
        gl_ref[...] = (_dot_nt(h, w_ref[GATE_OFF:INW, :]) + b_ref[:, GATE_OFF:INW]).astype(BF16)

    return pl.pallas_call(
        body, name="in_proj", grid=(T // tm,),
        in_specs=[_rows(tm, D), _whole((1, D)), _whole((INW, D)), _whole((1, INW))],
        out_specs=[_rows(tm, D), _rows(tm, QKVW), _rows(tm, 2 * C), _rows(tm, 2 * D)],
        out_shape=[jax.ShapeDtypeStruct((T, D), BF16), jax.ShapeDtypeStruct((T, QKVW), BF16),
                   jax.ShapeDtypeStruct((T, 2 * C), BF16), jax.ShapeDtypeStruct((T, 2 * D), BF16)],
        compiler_params=_params("parallel"),
    )(x, g_mix, w_in_t, b_in)


GROWS = GROUP * BLK
BAND = 2 * BLK


def _band(i):
    rb = pl.multiple_of(jnp.maximum(i - 1, 0) * BLK, BLK)
    row = lax.broadcasted_iota(jnp.int32, (GROWS, BAND), 0)
    kpos = rb + lax.broadcasted_iota(jnp.int32, (GROWS, BAND), 1)
    qpos = i * BLK + jnp.bitwise_and(row, BLK - 1)
    return rb, jnp.logical_and(kpos <= qpos, kpos > qpos - BLK)


def _sink_column(sink_ref, g):
    head = lax.shift_right_logical(lax.broadcasted_iota(jnp.int32, (GROWS, 1), 0), 7)
    col = jnp.full((GROWS, 1), sink_ref[0, g * GROUP], F32)
    for hh in range(1, GROUP):
        col = jnp.where(head == hh, sink_ref[0, g * GROUP + hh], col)
    return col


def _attn_fwd(qkv, sinks):
    def body(sink_ref, qkv_ref, o_ref, lse_ref, s_ref, p_ref):
        i = pl.program_id(0)
        r0 = pl.multiple_of(i * BLK, BLK)
        rb, valid = _band(i)
        for g in range(NKV):
            kband = qkv_ref[pl.ds(rb, BAND), AW + g * HD:AW + (g + 1) * HD]
            vband = qkv_ref[pl.ds(rb, BAND), AW + KVW + g * HD:AW + KVW + (g + 1) * HD]
            for hh in range(GROUP):
                h = g * GROUP + hh
                s_ref[hh * BLK:(hh + 1) * BLK, :] = _dot_nt(qkv_ref[pl.ds(r0, BLK), h * HD:(h + 1) * HD], kband)
            s = jnp.where(valid, s_ref[...] * SCALE, NEG)
            sink = _sink_column(sink_ref, g)
            m = jnp.maximum(jnp.max(s, axis=-1, keepdims=True), sink)
            p = jnp.exp(s - m)
            den = jnp.sum(p, axis=-1, keepdims=True) + jnp.exp(sink - m)
            p_ref[...] = (p * (1.0 / den)).astype(BF16)
            lse = m + jnp.log(den)
            for hh in range(GROUP):
                h = g * GROUP + hh
                o_ref[:, h * HD:(h + 1) * HD] = _dot(p_ref[hh * BLK:(hh + 1) * BLK, :], vband).astype(BF16)
                lse_ref[:, h:h + 1] = lse[hh * BLK:(hh + 1) * BLK]

    return pl.pallas_call(
        body, name="attn_fwd", grid=(T // BLK,),
        in_specs=[pl.BlockSpec(memory_space=pltpu.SMEM), _whole((T, QKVW))],
        out_specs=[_rows(BLK, AW), _rows(BLK, NQ)],
        out_shape=[jax.ShapeDtypeStruct((T, AW), BF16), jax.ShapeDtypeStruct((T, NQ), F32)],
        scratch_shapes=[pltpu.VMEM((GROWS, BAND), F32), pltpu.VMEM((GROWS, BAND), BF16)],
        compiler_params=_params("parallel"),
    )(sinks, qkv)


CONV_TM = 256
CONV_SUB = 32


def _glu(ab):
    a = ab[:, 0:C].astype(F32)
    b = ab[:, C:2 * C].astype(F32)
    return a * _sigmoid(b)


SUBLANES = 8


def _shifted_copies(ref):
    rows = ref.shape[1] - SUBLANES
    for r in range(1, SUBLANES):
        ref[r, 0:rows, :] = ref[0, r:r + rows, :]


def _shifted_rows(ref, start, size):
    r = start % SUBLANES
    return ref[r, start - r:start - r + size, :]


def _conv_fwd(glu, conv_w, conv_b, ln_g, ln_b, dep):
    tm = CONV_TM

    def body(cur_ref, prev_ref, w_ref, cb_ref, g_ref, b_ref, dep_ref, u_ref, c_ref, zs_ref):
        i = pl.program_id(0)
        zprev = _glu(prev_ref[tm - HALO:tm, :])
        zs_ref[0, 0:HALO, :] = jnp.where(i > 0, zprev, 0.0)
        zs_ref[0, HALO:HALO + tm, :] = _glu(cur_ref[...])
        _shifted_copies(zs_ref)
        for s in range(tm // CONV_SUB):
            base = HALO + s * CONV_SUB - (KW - 1)
            acc = jnp.broadcast_to(cb_ref[...], (CONV_SUB, C))
            for j in range(KW):
                acc = acc + w_ref[j:j + 1, :] * _shifted_rows(zs_ref, base + j, CONV_SUB)
            rows = slice(s * CONV_SUB, (s + 1) * CONV_SUB)
            u_ref[rows, :] = acc
            mu = jnp.mean(acc, axis=-1, keepdims=True)
            xc = acc - mu
            var = jnp.mean(xc * xc, axis=-1, keepdims=True)
            y = xc * lax.rsqrt(var + EPS) * g_ref[...] + b_ref[...]
            c_ref[rows, :] = (y * _sigmoid(y)).astype(BF16)

    return pl.pallas_call(
        body, name="conv_fwd", grid=(T // tm,),
        in_specs=[_rows(tm, 2 * C),
                  pl.BlockSpec((tm, 2 * C), lambda i: (jnp.maximum(i - 1, 0), 0)),
                  _whole((KW, C)), _whole((1, C)), _whole((1, C)), _whole((1, C)), _whole((8, 128))],
        out_specs=[_rows(tm, C), _rows(tm, C)],
        out_shape=[jax.ShapeDtypeStruct((T, C), F32), jax.ShapeDtypeStruct((T, C), BF16)],
        scratch_shapes=[pltpu.VMEM((SUBLANES, HALO + tm, C), F32)],
        compiler_params=_params("parallel"),
    )(glu, glu, conv_w, conv_b, ln_g, ln_b, dep)


def _branch_outputs(o, cact, wap_ref, wcp_ref, bcp_ref):
    ya = jnp.concatenate([_dot(o, wap_ref[s]) for s in range(N_CHIPS)], axis=1)
    yc = jnp.concatenate([_dot(cact, wcp_ref[s]) for s in range(N_CHIPS)], axis=1) + bcp_ref[...]
    return ya, yc


def _mix_out(x, o, cact, gl, w_ap, w_cp, b_cp, w_out, dep):
    tm = ROW_TM

    def body(x_ref, o_ref, c_ref, gl_ref, wap_ref, wcp_ref, bcp_ref, wo_ref, dep_ref, mg_ref, x1_ref):
        ya, yc = _branch_outputs(o_ref[...], c_ref[...], wap_ref, wcp_ref, bcp_ref)
        g0 = _sigmoid(gl_ref[:, 0:D].astype(F32))
        g1 = _sigmoid(gl_ref[:, D:2 * D].astype(F32))
        mg = (g0 * ya + g1 * yc).astype(BF16)
        mg_ref[...] = mg
        x1_ref[...] = x_ref[...] + _dot(mg, wo_ref[...])

    return pl.pallas_call(
        body, name="mix_out", grid=(T // tm,),
        in_specs=[_rows(tm, D), _rows(tm, AW), _rows(tm, C), _rows(tm, 2 * D),
                  _whole((N_CHIPS, AW, D // N_CHIPS)), _whole((N_CHIPS, C, D // N_CHIPS)), _whole((1, D)),
                  _whole((D, D)), _whole((8, 128))],
        out_specs=[_rows(tm, D), _rows(tm, D)],
        out_shape=[jax.ShapeDtypeStruct((T, D), BF16), jax.ShapeDtypeStruct((T, D), F32)],
        compiler_params=_params("parallel"),
    )(x, o, cact, gl, w_ap, w_cp, b_cp, w_out, dep)


def _ffn_in(x1, g_ffn, w_fi, dep):
    tm = ROW_TM

    def body(x_ref, g_ref, w_ref, dep_ref, h_ref, gu_ref, act_ref):
        xv = x_ref[...]
        r = lax.rsqrt(jnp.mean(xv * xv, axis=-1, keepdims=True) + EPS)
        h = (xv * r * g_ref[...]).astype(BF16)
        h_ref[...] = h
        for s in range(N_CHIPS // 2):
            c0 = s * FSH
            gate = _dot(h, w_ref[s])
            up = _dot(h, w_ref[s + N_CHIPS // 2])
            gu_ref[:, c0:c0 + FSH] = gate.astype(BF16)
            gu_ref[:, DFF + c0:DFF + c0 + FSH] = up.astype(BF16)
            act_ref[:, c0:c0 + FSH] = (gate * _sigmoid(gate) * up).astype(BF16)

    return pl.pallas_call(
        body, name="ffn_in", grid=(T // tm,),
        in_specs=[_rows(tm, D), _whole((1, D)), _const((N_CHIPS, D, FSH)), _whole((8, 128))],
        out_specs=[_rows(tm, D), _rows(tm, 2 * DFF), _rows(tm, DFF)],
        out_shape=[jax.ShapeDtypeStruct((T, D), BF16), jax.ShapeDtypeStruct((T, 2 * DFF), BF16),
                   jax.ShapeDtypeStruct((T, DFF), BF16)],
        compiler_params=_params("parallel"),
    )(x1, g_ffn, w_fi, dep)


def _ffn_out_loss(x1, act, w_dn, g_final, target):
    tm = ROW_TM

    def body(x_ref, a_ref, w_ref, g_ref, t_ref, dx_ref, dxb_ref, dg_ref, loss_ref):
        i = pl.program_id(0)
        x2 = x_ref[...] + _dot(a_ref[...], w_ref[...])
        r = lax.rsqrt(jnp.mean(x2 * x2, axis=-1, keepdims=True) + EPS)
        xh = x2 * r
        g = g_ref[...]
        err = xh * g - t_ref[...]
        dy = err * (1.0 / D)
        dyg = dy * g
        dx = r * (dyg - xh * jnp.mean(dyg * xh, axis=-1, keepdims=True))
        dx_ref[...] = dx
        dxb_ref[...] = dx.astype(BF16)
        part = 0.5 * jnp.sum(jnp.mean(err * err, axis=-1, keepdims=True), axis=0, keepdims=True)

        @pl.when(i == 0)
        def _():
            dg_ref[...] = jnp.zeros_like(dg_ref)
            loss_ref[...] = jnp.zeros_like(loss_ref)

        dg_ref[...] += jnp.sum(dy * xh, axis=0, keepdims=True)
        loss_ref[...] += jnp.broadcast_to(part, loss_ref.shape)

    return pl.pallas_call(
        body, name="ffn_out_loss", grid=(T // tm,),
        in_specs=[_rows(tm, D), _rows(tm, DFF), _whole((DFF, D)), _whole((1, D)), _rows(tm, D)],
        out_specs=[_rows(tm, D), _rows(tm, D), _whole((1, D)), _whole((1, 128))],
        out_shape=[jax.ShapeDtypeStruct((T, D), F32), jax.ShapeDtypeStruct((T, D), BF16),
                   jax.ShapeDtypeStruct((1, D), F32), jax.ShapeDtypeStruct((1, 128), F32)],
        compiler_params=_params("arbitrary"),
    )(x1, act, w_dn, g_final, target)


def _const(shape):
    return pl.BlockSpec(shape, lambda i: tuple(0 for _ in shape), pipeline_mode=pl.Buffered(1))


def _ffn_bwd(dx2, dx2b, gu, x1, g_ffn, w_dn_t, w_fi_t):
    tm = ROW_TM // 2

    def body(dx_ref, dxb_ref, gu_ref, x_ref, g_ref, wdn_ref, wfi_ref,
             dgu_ref, dx1_ref, dx1b_ref, dg_ref):
        i = pl.program_id(0)
        dxb = dxb_ref[...]
        dh = jnp.zeros((tm, D), F32)
        for k in range(N_CHIPS // 2):
            c0 = k * FSH
            dact = _dot_nt(dxb, wdn_ref[c0:c0 + FSH, :])
            gate = gu_ref[:, c0:c0 + FSH].astype(F32)
            up = gu_ref[:, DFF + c0:DFF + c0 + FSH].astype(F32)
            s = _sigmoid(gate)
            dup = (dact * gate * s).astype(BF16)
            dgate = (dact * up * s * (1.0 + gate * (1.0 - s))).astype(BF16)
            dgu_ref[:, c0:c0 + FSH] = dgate
            dgu_ref[:, DFF + c0:DFF + c0 + FSH] = dup
            dh = dh + _dot_nt(dgate, wfi_ref[k]) + _dot_nt(dup, wfi_ref[k + N_CHIPS // 2])
        xv = x_ref[...]
        r = lax.rsqrt(jnp.mean(xv * xv, axis=-1, keepdims=True) + EPS)
        xh = xv * r
        dhg = dh * g_ref[...]
        dx1 = dx_ref[...] + r * (dhg - xh * jnp.mean(dhg * xh, axis=-1, keepdims=True))
        dx1_ref[...] = dx1
        dx1b_ref[...] = dx1.astype(BF16)

        @pl.when(i == 0)
        def _():
            dg_ref[...] = jnp.zeros_like(dg_ref)

        dg_ref[...] += jnp.sum(dh * xh, axis=0, keepdims=True)

    return pl.pallas_call(
        body, name="ffn_bwd", grid=(T // tm,),
        in_specs=[_rows(tm, D), _rows(tm, D), _rows(tm, 2 * DFF), _rows(tm, D), _whole((1, D)),
                  _const((DFF, D)), _const((N_CHIPS, D, FSH))],
        out_specs=[_rows(tm, 2 * DFF), _rows(tm, D), _rows(tm, D), _whole((1, D))],
        out_shape=[jax.ShapeDtypeStruct((T, 2 * DFF), BF16), jax.ShapeDtypeStruct((T, D), F32),
                   jax.ShapeDtypeStruct((T, D), BF16), jax.ShapeDtypeStruct((1, D), F32)],
        compiler_params=_params("arbitrary"),
    )(dx2, dx2b, gu, x1, g_ffn, w_dn_t, w_fi_t)


def _mix_bwd(dx1b, gl, o, cact, b_cp, w_out, w_ap, w_cp, dep):
    tm = ROW_TM

    def body(dx_ref, gl_ref, o_ref, c_ref, bcp_ref, wo_ref, wap_ref, wcp_ref, dep_ref,
             dya_ref, dyc_ref, dgl_ref, do_ref, dc_ref, db_ref):
        i = pl.program_id(0)
        dm = _dot_nt(dx_ref[...], wo_ref[...])
        ya, yc = _branch_outputs(o_ref[...], c_ref[...], wap_ref, wcp_ref, bcp_ref)
        g0 = _sigmoid(gl_ref[:, 0:D].astype(F32))
        g1 = _sigmoid(gl_ref[:, D:2 * D].astype(F32))
        dya = dm * g0
        dyc = dm * g1
        dgl_ref[:, 0:D] = (dya * ya * (1.0 - g0)).astype(BF16)
        dgl_ref[:, D:2 * D] = (dyc * yc * (1.0 - g1)).astype(BF16)
        dyab = dya.astype(BF16)
        dycb = dyc.astype(BF16)
        dya_ref[...] = dyab
        dyc_ref[...] = dycb
        sw = D // N_CHIPS
        do = jnp.zeros((tm, AW), F32)
        dcv = jnp.zeros((tm, C), F32)
        for s in range(N_CHIPS):
            do = do + _dot_nt(dyab[:, s * sw:(s + 1) * sw], wap_ref[s])
            dcv = dcv + _dot_nt(dycb[:, s * sw:(s + 1) * sw], wcp_ref[s])
        do_ref[...] = do.astype(BF16)
        dc_ref[...] = dcv.astype(BF16)

        @pl.when(i == 0)
        def _():
            db_ref[...] = jnp.zeros_like(db_ref)

        db_ref[...] += jnp.sum(dyc, axis=0, keepdims=True)

    return pl.pallas_call(
        body, name="mix_bwd", grid=(T // tm,),
        in_specs=[_rows(tm, D), _rows(tm, 2 * D), _rows(tm, AW), _rows(tm, C), _whole((1, D)),
                  _whole((D, D)), _whole((N_CHIPS, AW, D // N_CHIPS)), _whole((N_CHIPS, C, D // N_CHIPS)),
                  _whole((8, 128))],
        out_specs=[_rows(tm, D), _rows(tm, D), _rows(tm, 2 * D), _rows(tm, AW), _rows(tm, C),
                   _whole((1, D))],
        out_shape=[jax.ShapeDtypeStruct((T, D), BF16), jax.ShapeDtypeStruct((T, D), BF16),
                   jax.ShapeDtypeStruct((T, 2 * D), BF16), jax.ShapeDtypeStruct((T, AW), BF16),
                   jax.ShapeDtypeStruct((T, C), BF16), jax.ShapeDtypeStruct((1, D), F32)],
        compiler_params=_params("arbitrary"),
    )(dx1b, gl, o, cact, b_cp, w_out, w_ap, w_cp, dep)


def _conv_bwd(glu, u, dc, conv_w, ln_g, ln_b, dep):
    tm = CONV_TM
    nblk = T // tm

    def du_of(uv, dcv, g_ref, b_ref):
        mu = jnp.mean(uv, axis=-1, keepdims=True)
        xc = uv - mu
        var = jnp.mean(xc * xc, axis=-1, keepdims=True)
        rstd = lax.rsqrt(var + EPS)
        xh = xc * rstd
        y = xh * g_ref[...] + b_ref[...]
        sg = _sigmoid(y)
        dy = dcv * (sg * (1.0 + y * (1.0 - sg)))
        dxh = dy * g_ref[...]
        du = rstd * (dxh - jnp.mean(dxh, axis=-1, keepdims=True)
                     - xh * jnp.mean(dxh * xh, axis=-1, keepdims=True))
        return du, dy, xh

    def body(cur_ref, prev_ref, u_ref, un_ref, dc_ref, dcn_ref, w_ref, g_ref, b_ref, dep_ref,
             dglu_ref, dw_ref, dcb_ref, dg_ref, db_ref, zs_ref, dus_ref):
        i = pl.program_id(0)

        @pl.when(i == 0)
        def _():
            dw_ref[...] = jnp.zeros_like(dw_ref)
            dcb_ref[...] = jnp.zeros_like(dcb_ref)
            dg_ref[...] = jnp.zeros_like(dg_ref)
            db_ref[...] = jnp.zeros_like(db_ref)

        zprev = _glu(prev_ref[tm - HALO:tm, :])
        zs_ref[0, 0:HALO, :] = jnp.where(i > 0, zprev, 0.0)
        zs_ref[0, HALO:HALO + tm, :] = _glu(cur_ref[...])
        _shifted_copies(zs_ref)

        dun, _, _ = du_of(un_ref[0:HALO, :], dcn_ref[0:HALO, :].astype(F32), g_ref, b_ref)
        dus_ref[0, tm:tm + HALO, :] = jnp.where(i < nblk - 1, dun, 0.0)
        dg_acc = jnp.zeros((1, C), F32)
        db_acc = jnp.zeros((1, C), F32)
        dcb_acc = jnp.zeros((1, C), F32)
        for s in range(tm // CONV_SUB):
            rows = slice(s * CONV_SUB, (s + 1) * CONV_SUB)
            du, dy, xh = du_of(u_ref[rows, :], dc_ref[rows, :].astype(F32), g_ref, b_ref)
            dus_ref[0, rows, :] = du
            dg_acc = dg_acc + jnp.sum(dy * xh, axis=0, keepdims=True)
            db_acc = db_acc + jnp.sum(dy, axis=0, keepdims=True)
            dcb_acc = dcb_acc + jnp.sum(du, axis=0, keepdims=True)
        dg_ref[...] += dg_acc
        db_ref[...] += db_acc
        dcb_ref[...] += dcb_acc
        _shifted_copies(dus_ref)

        for j in range(KW):
            acc = jnp.zeros((CONV_SUB, C), F32)
            for s in range(tm // CONV_SUB):
                base = HALO + s * CONV_SUB - (KW - 1) + j
                acc = acc + dus_ref[0, s * CONV_SUB:(s + 1) * CONV_SUB, :] * _shifted_rows(zs_ref, base, CONV_SUB)
            dw_ref[j:j + 1, :] += jnp.sum(acc, axis=0, keepdims=True)

        for s in range(tm // CONV_SUB):
            rows = slice(s * CONV_SUB, (s + 1) * CONV_SUB)
            dz = jnp.zeros((CONV_SUB, C), F32)
            for j in range(KW):
                dz = dz + w_ref[j:j + 1, :] * _shifted_rows(dus_ref, s * CONV_SUB + (KW - 1) - j, CONV_SUB)
            a = cur_ref[rows, 0:C].astype(F32)
            sb = _sigmoid(cur_ref[rows, C:2 * C].astype(F32))
            dglu_ref[rows, 0:C] = (dz * sb).astype(BF16)
            dglu_ref[rows, C:2 * C] = (dz * a * sb * (1.0 - sb)).astype(BF16)

    nxt = lambda i: (jnp.minimum(i + 1, nblk - 1), 0)
    return pl.pallas_call(
        body, name="conv_bwd", grid=(nblk,),
        in_specs=[_rows(tm, 2 * C),
                  pl.BlockSpec((tm, 2 * C), lambda i: (jnp.maximum(i - 1, 0), 0)),
                  _rows(tm, C), pl.BlockSpec((tm, C), nxt),
                  _rows(tm, C), pl.BlockSpec((tm, C), nxt),
                  _whole((KW, C)), _whole((1, C)), _whole((1, C)), _whole((8, 128))],
        out_specs=[_rows(tm, 2 * C), _whole((KW, C)), _whole((1, C)), _whole((1, C)), _whole((1, C))],
        out_shape=[jax.ShapeDtypeStruct((T, 2 * C), BF16), jax.ShapeDtypeStruct((KW, C), F32),
                   jax.ShapeDtypeStruct((1, C), F32), jax.ShapeDtypeStruct((1, C), F32),
                   jax.ShapeDtypeStruct((1, C), F32)],
        scratch_shapes=[pltpu.VMEM((SUBLANES, HALO + tm, C), F32), pltpu.VMEM((SUBLANES, tm + HALO, C), F32)],
        compiler_params=_params("arbitrary"),
    )(glu, glu, u, u, dc, dc, conv_w, ln_g, ln_b, dep)


def _attn_bwd(qkv, o, do, lse, sinks, dep):
    def body(sink_ref, qkv_ref, o_ref, do_ref, lse_ref, dep_ref, dq_ref, dkv_ref, ds_ref,
             s_ref, dp_ref, p_ref, dsb_ref):
        i = pl.program_id(0)

        @pl.when(i == 0)
        def _():
            dkv_ref[...] = jnp.zeros_like(dkv_ref)
            ds_ref[...] = jnp.zeros_like(ds_ref)

        r0 = pl.multiple_of(i * BLK, BLK)
        rb, valid = _band(i)
        for g in range(NKV):
            kband = qkv_ref[pl.ds(rb, BAND), AW + g * HD:AW + (g + 1) * HD]
            vband = qkv_ref[pl.ds(rb, BAND), AW + KVW + g * HD:AW + KVW + (g + 1) * HD]
            lse_parts, dl_parts = [], []
            for hh in range(GROUP):
                h = g * GROUP + hh
                hcol = slice(h * HD, (h + 1) * HD)
                doh = do_ref[:, hcol]
                s_ref[hh * BLK:(hh + 1) * BLK, :] = _dot_nt(qkv_ref[pl.ds(r0, BLK), hcol], kband)
                dp_ref[hh * BLK:(hh + 1) * BLK, :] = _dot_nt(doh, vband)
                lse_parts.append(lse_ref[:, h:h + 1])
                dl_parts.append(jnp.sum(doh.astype(F32) * o_ref[:, hcol].astype(F32), axis=-1, keepdims=True))
            lse = jnp.concatenate(lse_parts, axis=0)
            dl = jnp.concatenate(dl_parts, axis=0)
            p = jnp.where(valid, jnp.exp(s_ref[...] * SCALE - lse), 0.0)
            p_ref[...] = p.astype(BF16)
            dsb_ref[...] = (p * (dp_ref[...] - dl)).astype(BF16)
            dsink = -(jnp.exp(_sink_column(sink_ref, g) - lse) * dl)
            dk = jnp.zeros((BAND, HD), F32)
            dv = jnp.zeros((BAND, HD), F32)
            for hh in range(GROUP):
                h = g * GROUP + hh
                hcol = slice(h * HD, (h + 1) * HD)
                rows = slice(hh * BLK, (hh + 1) * BLK)
                dq_ref[:, hcol] = (_dot(dsb_ref[rows, :], kband) * SCALE).astype(BF16)
                dk = dk + _dot_tn(dsb_ref[rows, :], qkv_ref[pl.ds(r0, BLK), hcol])
                dv = dv + _dot_tn(p_ref[rows, :], do_ref[:, hcol])
                ds_ref[h:h + 1, :] += jnp.broadcast_to(jnp.sum(dsink[rows], axis=0, keepdims=True), (1, 128))
            dkv_ref[pl.ds(rb, BAND), g * HD:(g + 1) * HD] += dk * SCALE
            dkv_ref[pl.ds(rb, BAND), KVW + g * HD:KVW + (g + 1) * HD] += dv

    return pl.pallas_call(
        body, name="attn_bwd", grid=(T // BLK,),
        in_specs=[pl.BlockSpec(memory_space=pltpu.SMEM), _whole((T, QKVW)),
                  _rows(BLK, AW), _rows(BLK, AW), _rows(BLK, NQ), _whole((8, 128))],
        out_specs=[_rows(BLK, AW), _whole((T, 2 * KVW)), _whole((NQ, 128))],
        out_shape=[jax.ShapeDtypeStruct((T, AW), BF16), jax.ShapeDtypeStruct((T, 2 * KVW), F32),
                   jax.ShapeDtypeStruct((NQ, 128), F32)],
        scratch_shapes=[pltpu.VMEM((GROWS, BAND), F32), pltpu.VMEM((GROWS, BAND), F32),
                        pltpu.VMEM((GROWS, BAND), BF16), pltpu.VMEM((GROWS, BAND), BF16)],
        compiler_params=_params("arbitrary"),
    )(sinks, qkv, o, do, lse, dep)


PROJ_PARTS = [(0, AW), (AW, QKVW), (GLU_OFF, GATE_OFF), (GATE_OFF, INW)]


def _in_proj_bwd(dq, dkv, dglu, dgl, x, dx1, g_mix, w_in_t, dep):
    tm = ROW_TM

    def body(dq_ref, dkv_ref, dglu_ref, dgl_ref, x_ref, dx1_ref, g_ref, w_ref, dep_ref,
             gx_ref, dg_ref, db_ref):
        i = pl.program_id(0)

        @pl.when(i == 0)
        def _():
            dg_ref[...] = jnp.zeros_like(dg_ref)
            db_ref[...] = jnp.zeros_like(db_ref)

        dh = jnp.zeros((tm, D), F32)
        for part_ref, (lo, hi) in zip((dq_ref, dkv_ref, dglu_ref, dgl_ref), PROJ_PARTS):
            part = part_ref[...]
            dh = dh + _dot(part.astype(BF16), w_ref[lo:hi, :])
            db_ref[:, lo:hi] += jnp.sum(part.astype(F32), axis=0, keepdims=True)
        xv = x_ref[...]
        r = lax.rsqrt(jnp.mean(xv * xv, axis=-1, keepdims=True) + EPS)
        xh = xv * r
        dhg = dh * g_ref[...]
        gx_ref[...] = dx1_ref[...] + r * (dhg - xh * jnp.mean(dhg * xh, axis=-1, keepdims=True))
        dg_ref[...] += jnp.sum(dh * xh, axis=0, keepdims=True)

    return pl.pallas_call(
        body, name="in_proj_bwd", grid=(T // tm,),
        in_specs=[_rows(tm, AW), _rows(tm, 2 * KVW), _rows(tm, 2 * C), _rows(tm, 2 * D),
                  _rows(tm, D), _rows(tm, D), _whole((1, D)), _const((INW, D)), _whole((8, 128))],
        out_specs=[_rows(tm, D), _whole((1, D)), _whole((1, INW))],
        out_shape=[jax.ShapeDtypeStruct((T, D), F32), jax.ShapeDtypeStruct((1, D), F32),
                   jax.ShapeDtypeStruct((1, INW), F32)],
        compiler_params=_params("arbitrary"),
    )(dq, dkv, dglu, dgl, x, dx1, g_mix, w_in_t, dep)


def _grad_w_in_t(h, dq, dkv, dglu, dgl):
    tn, chunk = 512, 256

    def body(h_ref, dq_ref, dkv_ref, dglu_ref, dgl_ref, o_ref, pt_ref):
        @pl.when(pl.program_id(0) == 0)
        def _():
            for part_ref, (lo, hi) in zip((dq_ref, dkv_ref, dglu_ref, dgl_ref), PROJ_PARTS):
                for c0 in range(0, hi - lo, chunk):
                    pt_ref[lo + c0:lo + c0 + chunk, :] = part_ref[:, c0:c0 + chunk].astype(BF16).T

        hv = h_ref[...]
        for r0 in range(0, INW, QKVW):
            o_ref[r0:r0 + QKVW, :] = _dot(pt_ref[r0:r0 + QKVW, :], hv).astype(BF16)

    return pl.pallas_call(
        body, name="grad_w_in", grid=(D // tn,),
        in_specs=[pl.BlockSpec((T, tn), lambda j: (0, j)), _const((T, AW)), _const((T, 2 * KVW)),
                  _const((T, 2 * C)), _const((T, 2 * D))],
        out_specs=pl.BlockSpec((INW, tn), lambda j: (0, j)),
        out_shape=jax.ShapeDtypeStruct((INW, D), BF16),
        scratch_shapes=[pltpu.VMEM((INW, T), BF16)],
        compiler_params=_params("arbitrary"),
    )(h, dq, dkv, dglu, dgl)


def _grad_w(a, b, name, tk, tn, col_sharded):
    k, n = a.shape[1], b.shape[1]

    def body(a_ref, b_ref, o_ref, at_ref):
        @pl.when(pl.program_id(1) == 0)
        def _():
            at_ref[...] = a_ref[...].T

        o_ref[...] = _dot(at_ref[...], b_ref[...]).astype(BF16)

    if col_sharded:
        per = n // N_CHIPS // tn
        shape = (N_CHIPS, k, n // N_CHIPS)
        out_map = lambda i, j: (j // per, i, j % per)
    else:
        shape = (1, k, n)
        out_map = lambda i, j: (0, i, j)
    out = pl.pallas_call(
        body, name=name, grid=(k // tk, n // tn),
        in_specs=[pl.BlockSpec((T, tk), lambda i, j: (0, i)), pl.BlockSpec((T, tn), lambda i, j: (0, j))],
        out_specs=pl.BlockSpec((None, tk, tn), out_map),
        out_shape=jax.ShapeDtypeStruct(shape, BF16),
        scratch_shapes=[pltpu.VMEM((tk, T), BF16)],
        compiler_params=_params("parallel", "arbitrary"),
    )(a, b)
    return out if col_sharded else out.reshape(N_CHIPS, k // N_CHIPS, n)


HBM_SPEC = pl.BlockSpec(memory_space=pltpu.HBM)


def _place():
    x, y, c = lax.axis_index("x"), lax.axis_index("y"), lax.axis_index("c")
    chips = [(1 - x, y), (x, 1 - y), (1 - x, 1 - y)]
    return x, y, c, chips


SEM_SPEC = pl.BlockSpec(memory_space=pltpu.SEMAPHORE)
ANY_SPEC = pl.BlockSpec(memory_space=pl.ANY)
VMEM_SPEC = pl.BlockSpec(memory_space=pltpu.VMEM)
EFFECT = pltpu.SideEffectType.DATAFLOW_SIDE_EFFECTING


def _gather_ends(src, land, x, y, c, chips):
    kh = src.shape[0] // 2
    s_me = 2 * x + y
    ends = [(src.at[pl.ds(c * kh, kh)], land.at[s_me, pl.ds(c * kh, kh)], (*chip, c)) for chip in chips]
    return ends + [(src, land.at[s_me], (x, y, 1 - c))]


def _reduce_ends(src, land, x, y, c, chips):
    return [(src.at[2 * chip[0] + chip[1]], land.at[j], (*chip, c)) for j, chip in enumerate(chips)]


def _chip_copies(ends, srcs, lands, send_sems, recv_sems, first=0):
    x, y, c, chips = _place()
    copies = []
    for src, land in zip(srcs, lands):
        peers = ends(src, land, x, y, c, chips)
        for s, d, to in peers:
            k = first * len(peers) + len(copies)
            copies.append(pltpu.make_async_remote_copy(
                src_ref=s, dst_ref=d, send_sem=send_sems.at[k], recv_sem=recv_sems.at[k],
                device_id=to, device_id_type=MESH))
    return copies


GATHER_PEERS, REDUCE_PEERS = 4, 3


def _chip_start(name, ends, peers, srcs, lands):
    n = len(srcs)

    def body(*refs):
        copies = _chip_copies(ends, refs[:n], refs[n:2 * n], refs[2 * n], refs[2 * n + 1])
        for cp in copies:
            cp.start()
        token = refs[-1]
        token[...] = jnp.zeros_like(token)

    hbm = lambda a: pltpu.HBM(a.shape, a.dtype)
    res = pl.pallas_call(
        body, name=name,
        out_shape=(pltpu.SemaphoreType.DMA((peers * n,)), pltpu.SemaphoreType.DMA((peers * n,)),
                   *[hbm(a) for a in srcs], *[hbm(a) for a in lands],
                   jax.ShapeDtypeStruct((8, 128), F32)),
        in_specs=[HBM_SPEC] * (2 * n),
        out_specs=(SEM_SPEC, SEM_SPEC, *[HBM_SPEC] * (2 * n), VMEM_SPEC),
        input_output_aliases={i: 2 + i for i in range(2 * n)},
        compiler_params=pltpu.CompilerParams(has_side_effects=EFFECT),
    )(*[pltpu.with_memory_space_constraint(a, pltpu.HBM) for a in (*srcs, *lands)])
    return res[0], res[1], list(res[2:2 + n]), list(res[2 + n:2 + 2 * n]), res[-1]


def _chip_wait(name, ends, send_sems, recv_sems, srcs, lands, after, first=0):
    n, na = len(srcs), len(after)

    def body(*refs):
        copies = _chip_copies(ends, refs[:n], refs[n:2 * n], refs[2 * n], refs[2 * n + 1], first)
        for cp in copies:
            cp.wait_send()
            cp.wait_recv()

    hbm = lambda a: pltpu.HBM(a.shape, a.dtype)
    res = pl.pallas_call(
        body, name=name,
        out_shape=tuple(hbm(a) for a in (*srcs, *lands)),
        in_specs=[HBM_SPEC] * (2 * n) + [SEM_SPEC, SEM_SPEC] + [ANY_SPEC] * na,
        out_specs=tuple([HBM_SPEC] * (2 * n)),
        input_output_aliases={i: i for i in range(2 * n)},
        compiler_params=pltpu.CompilerParams(has_side_effects=EFFECT),
    )(*srcs, *lands, send_sems, recv_sems, *after)
    return list(res[:n]), list(res[n:])


def _forward_copies(lands, send_sems, recv_sems):
    x, y, c, chips = _place()
    copies = []
    for land in lands:
        kh = land.shape[1] // 2
        for chip in chips:
            blk = land.at[2 * chip[0] + chip[1], pl.ds(c * kh, kh)]
            k = len(copies)
            copies.append(pltpu.make_async_remote_copy(
                src_ref=blk, dst_ref=blk, send_sem=send_sems.at[k], recv_sem=recv_sems.at[k],
                device_id=(x, y, 1 - c), device_id_type=MESH))
    return copies


def _gather_relay(name, send_sems, recv_sems, srcs, lands, after, first):
    n, na = len(srcs), len(after)

    def body(*refs):
        land_refs = refs[n:2 * n]
        for cp in _chip_copies(_gather_ends, refs[:n], land_refs, refs[2 * n], refs[2 * n + 1], first):
            cp.wait_send()
            cp.wait_recv()
        out = refs[2 * n + 2 + na:]
        for cp in _forward_copies(land_refs, out[0], out[1]):
            cp.start()
        out[-1][...] = jnp.zeros_like(out[-1])

    hbm = lambda a: pltpu.HBM(a.shape, a.dtype)
    res = pl.pallas_call(
        body, name=name,
        out_shape=(pltpu.SemaphoreType.DMA((3 * n,)), pltpu.SemaphoreType.DMA((3 * n,)),
                   *[hbm(a) for a in lands], jax.ShapeDtypeStruct((8, 128), F32)),
        in_specs=[HBM_SPEC] * (2 * n) + [SEM_SPEC, SEM_SPEC] + [ANY_SPEC] * na,
        out_specs=(SEM_SPEC, SEM_SPEC, *[HBM_SPEC] * n, VMEM_SPEC),
        input_output_aliases={n + i: 2 + i for i in range(n)},
        compiler_params=pltpu.CompilerParams(has_side_effects=EFFECT),
    )(*srcs, *lands, send_sems, recv_sems, *after)
    return res[0], res[1], list(res[2:2 + n]), res[-1]


def _forward_wait(name, send_sems, recv_sems, lands, after):
    n, na = len(lands), len(after)

    def body(*refs):
        for cp in _forward_copies(refs[:n], refs[n], refs[n + 1]):
            cp.wait_send()
            cp.wait_recv()

    hbm = lambda a: pltpu.HBM(a.shape, a.dtype)
    res = pl.pallas_call(
        body, name=name,
        out_shape=tuple(hbm(a) for a in lands),
        in_specs=[HBM_SPEC] * n + [SEM_SPEC, SEM_SPEC] + [ANY_SPEC] * na,
        out_specs=tuple([HBM_SPEC] * n),
        input_output_aliases={i: i for i in range(n)},
        compiler_params=pltpu.CompilerParams(has_side_effects=EFFECT),
    )(*lands, send_sems, recv_sems, *after)
    return list(res)


def _exchange_ends(src, land, x, y, c, chips):
    kh = src.shape[1] // 2
    return [(src.at[:, pl.ds((1 - c) * kh, kh)], land, (x, y, 1 - c))]


def _share_ends(src, land, x, y, c, chips):
    return [(src, land, (x, y, 1 - c))]


def _small_ends(src, land, x, y, c, chips):
    m = src.shape[0]
    rows = land.at[pl.ds((4 * x + 2 * y + c) * m, m)]
    peers = [(x, y, 1 - c)] + [(*chip, c) for chip in chips] + [(*chip, 1 - c) for chip in chips]
    return [(src, rows, to) for to in peers]


PAIR_PEERS, SMALL_PEERS = 1, 7


def _row_tile(k):
    for t in (256, 240, 128, 176, 64, 32, 16):
        if k % t == 0:
            return t
    raise ValueError(k)


def _pair_sum(c_idx, g, got, name):
    _, k, n = g.shape
    kh = k // 2
    tm = _row_tile(kh)
    nb = kh // tm

    def body(c_ref, g_ref, r_ref, o_ref):
        o_ref[...] = (g_ref[...].astype(F32) + r_ref[...].astype(F32)).astype(BF16)

    return pl.pallas_call(
        body, name=name,
        grid_spec=pltpu.PrefetchScalarGridSpec(
            num_scalar_prefetch=1, grid=(N_CHIPS, nb),
            in_specs=[pl.BlockSpec((1, tm, n), lambda s, i, c_ref: (s, c_ref[0] * nb + i, 0)),
                      pl.BlockSpec((1, tm, n), lambda s, i, c_ref: (s, i, 0))],
            out_specs=pl.BlockSpec((1, tm, n), lambda s, i, c_ref: (s, i, 0))),
        out_shape=jax.ShapeDtypeStruct((N_CHIPS, kh, n), BF16),
        compiler_params=_params("parallel", "parallel"),
    )(c_idx, g, got)


def _chip_sum(s_idx, mine, got, name):
    _, kh, n = mine.shape
    tm = _row_tile(kh)

    def body(s_ref, m_ref, r_ref, o_ref):
        acc = m_ref[0].astype(F32)
        for j in range(3):
            acc = acc + r_ref[j].astype(F32)
        o_ref[...] = acc

    return pl.pallas_call(
        body, name=name,
        grid_spec=pltpu.PrefetchScalarGridSpec(
            num_scalar_prefetch=1, grid=(kh // tm,),
            in_specs=[pl.BlockSpec((1, tm, n), lambda i, s_ref: (s_ref[0], i, 0)),
                      pl.BlockSpec((3, tm, n), lambda i, s_ref: (0, i, 0))],
            out_specs=pl.BlockSpec((tm, n), lambda i, s_ref: (i, 0))),
        out_shape=jax.ShapeDtypeStruct((kh, n), F32),
        compiler_params=_params("parallel"),
    )(s_idx, mine, got)


def _adamw_math(w, g, m, v):
    m = ADAM_B1 * m + (1.0 - ADAM_B1) * g
    v = ADAM_B2 * v + (1.0 - ADAM_B2) * (g * g)
    m_hat = m / (1.0 - ADAM_B1 ** ADAM_STEP)
    v_hat = v / (1.0 - ADAM_B2 ** ADAM_STEP)
    delta = -ADAM_LR * (m_hat / (jnp.sqrt(v_hat) + ADAM_EPS) + ADAM_WD * w)
    return delta, m, v


def _adamw(c_idx, w, g_mine, g_other, m, v, name):
    k, n = w.shape
    tm = k // 4

    def body(c_ref, w_ref, gm_ref, go_ref, m_ref, v_ref, g_ref, d_ref, mo_ref, vo_ref):
        g = jnp.where(pl.program_id(0) == c_ref[0], gm_ref[...], go_ref[...])
        d, mm, vv = _adamw_math(w_ref[...], g, m_ref[...], v_ref[...])
        g_ref[...] = g
        d_ref[...] = d
        mo_ref[...] = mm
        vo_ref[...] = vv

    full = pl.BlockSpec((tm, n), lambda h, i, c_ref: (2 * h + i, 0))
    mine = pl.BlockSpec((tm, n), lambda h, i, c_ref: (jnp.where(h == c_ref[0], i, 0), 0))
    other = pl.BlockSpec((tm, n), lambda h, i, c_ref: (jnp.where(h == c_ref[0], 0, i), 0))
    shp = jax.ShapeDtypeStruct((k, n), F32)
    return pl.pallas_call(
        body, name=name,
        grid_spec=pltpu.PrefetchScalarGridSpec(
            num_scalar_prefetch=1, grid=(2, 2),
            in_specs=[full, mine, other, full, full], out_specs=[full] * 4),
        out_shape=[shp] * 4, compiler_params=_params("arbitrary", "arbitrary"),
    )(c_idx, w, g_mine, g_other, m, v)


VEC_SLOTS = {
    "g_mix_norm": (0, 0, D), "b_conv_proj": (0, D, D), "g_ffn_norm": (0, 2 * D, D),
    "g_final": (0, 3 * D, D), "b_in": (1, 0, INW), "conv_b": (2, 0, C), "ln_g": (2, C, C),
    "ln_b": (2, 2 * C, C), "sinks": (2, 3 * C, NQ), "loss": (2, 3 * C + 128, 1),
}
VEC_ROWS, VEC_COLS = 8, 4 * D
CW_ROWS = 32
SMALL_NAMES = ["g_mix_norm", "b_in", "sinks", "conv_w", "conv_b", "ln_g", "ln_b",
               "b_conv_proj", "g_ffn_norm", "g_final"]
CW_LANES = C // N_CHIPS


def _pack_small(gs, loss):
    row0 = jnp.concatenate([gs["g_mix_norm"], gs["b_conv_proj"], gs["g_ffn_norm"], gs["g_final"]], axis=1)
    row1 = jnp.pad(gs["b_in"], ((0, 0), (0, VEC_COLS - INW)))
    row2 = jnp.concatenate([gs["conv_b"], gs["ln_g"], gs["ln_b"],
                            jnp.pad(gs["sinks"], ((0, 0), (0, 128 - NQ))),
                            jnp.pad(loss.reshape(1, 1), ((0, 0), (0, VEC_COLS - 3 * C - 129)))], axis=1)
    vec = jnp.concatenate([row0, row1, row2, jnp.zeros((VEC_ROWS - 3, VEC_COLS), F32)], axis=0)
    cw = jnp.pad(gs["conv_w"], ((0, CW_ROWS - KW), (0, 0)))
    return vec, cw


def _small_update(s_idx, vec_all, cw_all, wmv):
    nsm = len(SMALL_NAMES)

    def body(s_ref, vec_ref, cw_ref, *refs):
        ins = refs[:3 * nsm]
        outs = refs[3 * nsm:7 * nsm]
        loss_ref = refs[7 * nsm]

        def total(slot):
            row, lane, width = slot
            acc = vec_ref[row:row + 1, lane:lane + width]
            for k in range(1, 8):
                acc = acc + vec_ref[k * VEC_ROWS + row:k * VEC_ROWS + row + 1, lane:lane + width]
            return acc

        loss_ref[...] = jnp.broadcast_to(total(VEC_SLOTS["loss"]), loss_ref.shape)
        for p, name in enumerate(SMALL_NAMES):
            w_ref, m_ref, v_ref = ins[3 * p:3 * p + 3]
            g_ref, d_ref, mo_ref, vo_ref = outs[4 * p:4 * p + 4]
            if name == "conv_w":
                g = jnp.zeros((KW, CW_LANES), F32)
                for s in range(N_CHIPS):
                    cand = cw_ref[0:KW, s * CW_LANES:(s + 1) * CW_LANES]
                    for k in range(1, 8):
                        cand = cand + cw_ref[k * CW_ROWS:k * CW_ROWS + KW, s * CW_LANES:(s + 1) * CW_LANES]
                    g = jnp.where(s_ref[0] == s, cand, g)
            else:
                g = total(VEC_SLOTS[name])
            d, mm, vv = _adamw_math(w_ref[...], g, m_ref[...], v_ref[...])
            g_ref[...] = g
            d_ref[...] = d
            mo_ref[...] = mm
            vo_ref[...] = vv

    vmem = pl.BlockSpec(memory_space=pltpu.VMEM)
    flat = [a for t in wmv for a in t]
    out_shape = []
    for w, _, _ in wmv:
        out_shape += [jax.ShapeDtypeStruct(w.shape, F32)] * 4
    out_shape.append(jax.ShapeDtypeStruct((1, 128), F32))
    res = pl.pallas_call(
        body, name="small_update",
        in_specs=[pl.BlockSpec(memory_space=pltpu.SMEM)] + [vmem] * (2 + len(flat)),
        out_specs=[vmem] * len(out_shape), out_shape=out_shape,
    )(s_idx, vec_all, cw_all, *flat)
    return [tuple(res[4 * p:4 * p + 4]) for p in range(nsm)], res[4 * nsm]


WEIGHT_ORDER = ["g_mix_norm", "w_in", "b_in", "sinks", "conv_w", "conv_b", "ln_g", "ln_b",
                "w_attn_proj", "w_conv_proj", "b_conv_proj", "w_out", "g_ffn_norm", "w_ffn_in",
                "w_ffn_down", "g_final"]


def kernel(x, g_mix_norm, w_in, b_in, sinks, conv_w, conv_b, ln_g, ln_b, w_attn_proj, w_conv_proj, b_conv_proj, w_out, g_ffn_norm, w_ffn_in, w_ffn_down, g_final, loss_target, m_g_mix_norm, m_w_in, m_b_in, m_sinks, m_conv_w, m_conv_b, m_ln_g, m_ln_b, m_w_attn_proj, m_w_conv_proj, m_b_conv_proj, m_w_out, m_g_ffn_norm, m_w_ffn_in, m_w_ffn_down, m_g_final, v_g_mix_norm, v_w_in, v_b_in, v_sinks, v_conv_w, v_conv_b, v_ln_g, v_ln_b, v_w_attn_proj, v_w_conv_proj, v_b_conv_proj, v_w_out, v_g_ffn_norm, v_w_ffn_in, v_w_ffn_down, v_g_final):
    w = dict(g_mix_norm=g_mix_norm, w_in=w_in, b_in=b_in, sinks=sinks, conv_w=conv_w, conv_b=conv_b,
             ln_g=ln_g, ln_b=ln_b, w_attn_proj=w_attn_proj, w_conv_proj=w_conv_proj,
             b_conv_proj=b_conv_proj, w_out=w_out, g_ffn_norm=g_ffn_norm, w_ffn_in=w_ffn_in,
             w_ffn_down=w_ffn_down, g_final=g_final)
    m = dict(g_mix_norm=m_g_mix_norm, w_in=m_w_in, b_in=m_b_in, sinks=m_sinks, conv_w=m_conv_w,
             conv_b=m_conv_b, ln_g=m_ln_g, ln_b=m_ln_b, w_attn_proj=m_w_attn_proj,
             w_conv_proj=m_w_conv_proj, b_conv_proj=m_b_conv_proj, w_out=m_w_out,
             g_ffn_norm=m_g_ffn_norm, w_ffn_in=m_w_ffn_in, w_ffn_down=m_w_ffn_down, g_final=m_g_final)
    v = dict(g_mix_norm=v_g_mix_norm, w_in=v_w_in, b_in=v_b_in, sinks=v_sinks, conv_w=v_conv_w,
             conv_b=v_conv_b, ln_g=v_ln_g, ln_b=v_ln_b, w_attn_proj=v_w_attn_proj,
             w_conv_proj=v_w_conv_proj, b_conv_proj=v_b_conv_proj, w_out=v_w_out,
             g_ffn_norm=v_g_ffn_norm, w_ffn_in=v_w_ffn_in, w_ffn_down=v_w_ffn_down, g_final=v_g_final)

    c_idx = lax.axis_index("c").astype(jnp.int32).reshape(1)
    s_idx = (2 * lax.axis_index("x") + lax.axis_index("y")).astype(jnp.int32).reshape(1)

    out_g, out_d, out_m, out_v = {}, {}, {}, {}

    def gather_start(tag, shards):
        lands = [lax.empty((N_CHIPS,) + s.shape, s.dtype) for s in shards]
        return _chip_start("gather_start_" + tag, _gather_ends, GATHER_PEERS, shards, lands)

    def gather_relay(tag, state, after, first=0, count=None):
        send_sems, recv_sems, shards, lands, _ = state
        last = len(shards) if count is None else first + count
        return _gather_relay("gather_relay_" + tag, send_sems, recv_sems, shards[first:last],
                             lands[first:last], after, first)

    def gather_finish(tag, relay, after):
        return _forward_wait("forward_wait_" + tag, relay[0], relay[1], relay[2], after)

    names_b = ["w_attn_proj", "w_conv_proj", "w_out", "w_ffn_in", "w_ffn_down"]
    big = {name: (w[name][0], m[name][0], v[name][0]) for name in names_b}
    big["w_in"] = (w_in[0].T, m_w_in[0].T, v_w_in[0].T)
    state_a = gather_start("a", [big["w_in"][0].astype(BF16), jnp.pad(conv_w[0], ((0, CW_ROWS - KW), (0, 0)))])
    state_b = gather_start("b", [(big[name][0] + state_a[4][0, 0]).astype(BF16) for name in names_b])
    got_a = gather_finish("a", gather_relay("a", state_a, [state_b[4]]), [])
    w_in_t_full = got_a[0].reshape(INW, D)
    conv_w_full = got_a[1].transpose(1, 0, 2).reshape(CW_ROWS, C)[:KW]

    xs, target = x[0], loss_target[0]
    g_final2 = g_final.reshape(1, D)
    h, qkv, glu, gl = _in_proj(xs, g_mix_norm, w_in_t_full, b_in)
    o, lse = _attn_fwd(qkv, sinks)
    relay_1 = gather_relay("b1", state_b, [o], 0, 3)
    u, cact = _conv_fwd(glu, conv_w_full, conv_b, ln_g, ln_b, relay_1[3])
    w_ap4, w_cp4, w_out4 = gather_finish("b1", relay_1, [cact])
    w_out_full = w_out4.reshape(D, D)
    relay_2 = gather_relay("b2", state_b, [cact], 3, 1)
    mg, x1 = _mix_out(xs, o, cact, gl, w_ap4, w_cp4, b_conv_proj, w_out_full, relay_2[3])
    w_fi4, = gather_finish("b2", relay_2, [x1])
    relay_3 = gather_relay("b3", state_b, [x1], 4, 1)
    h2, gu, act = _ffn_in(x1, g_ffn_norm, w_fi4, relay_3[3])
    w_dn4, = gather_finish("b3", relay_3, [act])
    w_dn_full = w_dn4.reshape(DFF, D)
    dx2, dx2b, dg_final, loss_part = _ffn_out_loss(x1, act, w_dn_full, g_final2, target)

    def exchange_start(tag, grads):
        lands = [lax.empty((N_CHIPS, g.shape[1] // 2, g.shape[2]), g.dtype) for g in grads]
        return _chip_start("pair_start_" + tag, _exchange_ends, PAIR_PEERS, grads, lands)

    def reduce_start(tag, names, exchange, after):
        send_sems, recv_sems, grads, lands, _ = exchange
        grads, from_sibling = _chip_wait("pair_wait_" + tag, _exchange_ends, send_sems, recv_sems, grads, lands, after)
        pair = [_pair_sum(c_idx, g, r, "pair_sum_" + name) for name, g, r in zip(names, grads, from_sibling)]
        lands = [lax.empty((3,) + p.shape[1:], p.dtype) for p in pair]
        return _chip_start("chip_start_" + tag, _reduce_ends, REDUCE_PEERS, pair, lands)

    def reduce_sum(tag, names, state, after):
        send_sems, recv_sems, pair, lands, _ = state
        pair, lands = _chip_wait("chip_wait_" + tag, _reduce_ends, send_sems, recv_sems, pair, lands, after)
        mine = [_chip_sum(s_idx, p, r, "chip_sum_" + name) for name, p, r in zip(names, pair, lands)]
        others = [lax.empty(a.shape, a.dtype) for a in mine]
        return _chip_start("share_start_" + tag, _share_ends, PAIR_PEERS, mine, others)

    def reduce_finish(tag, names, share, after):
        send_sems, recv_sems, mine, others, _ = share
        mine, others = _chip_wait("share_wait_" + tag, _share_ends, send_sems, recv_sems, mine, others, after)
        for name, g_mine, g_other in zip(names, mine, others):
            wv, mv, vv = big[name]
            res = _adamw(c_idx, wv, g_mine, g_other, mv, vv, "adamw_" + name)
            if name == "w_in":
                res = [a.T for a in res]
            out_g[name], out_d[name], out_m[name], out_v[name] = [a[None] for a in res]

    dgu, dx1, dx1b, dg_ffn = _ffn_bwd(dx2, dx2b, gu, x1, g_ffn_norm, w_dn_full, w_fi4)
    names_1 = ["w_ffn_in", "w_ffn_down", "w_out", "w_attn_proj", "w_conv_proj"]
    grads_1 = [_grad_w(h2, dgu, "grad_w_ffn_in", 512, FSH, True),
               _grad_w(act, dx2b, "grad_w_ffn_down", 256, D, False)]
    dya, dyc, dgl, do, dc, db_cp = _mix_bwd(dx1b, gl, o, cact, b_conv_proj, w_out_full, w_ap4, w_cp4,
                                            relay_3[3])
    grads_1 += [_grad_w(mg, dx1b, "grad_w_out", 512, D, False),
                _grad_w(o, dya, "grad_w_attn_proj", 512, 256, True),
                _grad_w(cact, dyc, "grad_w_conv_proj", 512, 256, True)]
    exchange_1 = exchange_start("1", grads_1)
    dglu, dconv_w, dconv_b, dln_g, dln_b = _conv_bwd(glu, u, dc, conv_w_full, ln_g, ln_b, exchange_1[4])
    state_1 = reduce_start("1", names_1, exchange_1, [dglu])
    dq, dkv, dsinks = _attn_bwd(qkv, o, do, lse, sinks, state_1[4])
    names_2 = ["w_in"]
    gw_in_t = _grad_w_in_t(h, dq, dkv, dglu, dgl)
    exchange_2 = exchange_start("2", [gw_in_t.reshape(N_CHIPS, INW // N_CHIPS, D)])
    grad_x, dg_mix, db_in = _in_proj_bwd(dq, dkv, dglu, dgl, xs, dx1, g_mix_norm, w_in_t_full, exchange_2[4])

    gs = {"g_mix_norm": dg_mix, "b_in": db_in, "sinks": dsinks[:, 0].reshape(1, NQ),
          "conv_w": dconv_w, "conv_b": dconv_b, "ln_g": dln_g, "ln_b": dln_b,
          "b_conv_proj": db_cp, "g_ffn_norm": dg_ffn, "g_final": dg_final}
    blocks = list(_pack_small(gs, loss_part[0, 0]))
    tables = [lax.empty((8 * b.shape[0], b.shape[1]), b.dtype) for b in blocks]
    small = _chip_start("small_start", _small_ends, SMALL_PEERS, blocks, tables)

    state_2 = reduce_start("2", names_2, exchange_2, [grad_x, small[4]])
    share_1 = reduce_sum("1", names_1, state_1, [state_2[4]])
    blocks, tables = _chip_wait("small_wait", _small_ends, small[0], small[1], small[2], small[3],
                                [share_1[4]])
    me = 4 * lax.axis_index("x") + 2 * lax.axis_index("y") + lax.axis_index("c")
    vec_all, cw_all = [lax.dynamic_update_slice(t, b, (me * b.shape[0], 0)) for t, b in zip(tables, blocks)]

    def view(a, name):
        if name == "conv_w":
            return a[0]
        if name == "g_final":
            return a.reshape(1, D)
        return a

    wmv = [(view(w[name], name), view(m[name], name), view(v[name], name)) for name in SMALL_NAMES]
    small_out, loss_row = _small_update(s_idx, vec_all, cw_all, wmv)
    for name, (g, d, mm, vv) in zip(SMALL_NAMES, small_out):
        shape = w[name].shape
        out_g[name], out_d[name], out_m[name], out_v[name] = (
            g.reshape(shape), d.reshape(shape), mm.reshape(shape), vv.reshape(shape))

    reduce_finish("1", names_1, share_1, [loss_row])
    share_2 = reduce_sum("2", names_2, state_2, [out_d["w_conv_proj"]])
    reduce_finish("2", names_2, share_2, [])

    loss = loss_row[0, 0]
    return (loss, grad_x[None], *[out_g[k] for k in WEIGHT_ORDER], *[out_d[k] for k in WEIGHT_ORDER],
            *[out_m[k] for k in WEIGHT_ORDER], *[out_v[k] for k in WEIGHT_ORDER])
```

```python
import functools

import jax
import jax.numpy as jnp
from jax import lax
from jax.experimental import pallas as pl
from jax.experimental.pallas import tpu as pltpu

F32 = jnp.float32
BF16 = jnp.bfloat16

T = 2048
D = 1024
HD = 64
NQ = 8
NKV = 2
GROUP = NQ // NKV
BLK = 128
AW = NQ * HD
KVW = NKV * HD
C = 512
KW = 31
QKVW = AW + 2 * KVW
GLU_OFF = QKVW
GATE_OFF = GLU_OFF + 2 * C
INW = GATE_OFF + 2 * D
DFF = 2816
EPS = 1e-5
NEG = -1e30
SCALE = HD ** -0.5
HALO = 32
N_CHIPS = 4
FSH = 2 * DFF // N_CHIPS

ADAM_LR = 0.001
ADAM_B1 = 0.9
ADAM_B2 = 0.999
ADAM_EPS = 1e-08
ADAM_WD = 0.01
ADAM_STEP = 10

VMEM_LIMIT = 56 * 1024 * 1024
ROW_TM = 512
MESH = pl.DeviceIdType.MESH


def _params(*sem):
    return pltpu.CompilerParams(dimension_semantics=sem, vmem_limit_bytes=VMEM_LIMIT)


def _dot(a, b):
    return jnp.dot(a, b, preferred_element_type=F32)


def _dot_nt(a, b):
    return lax.dot_general(a, b, (((1,), (1,)), ((), ())), preferred_element_type=F32)


def _dot_tn(a, b):
    return lax.dot_general(a, b, (((0,), (0,)), ((), ())), preferred_element_type=F32)


def _sigmoid(v):
    return 1.0 / (1.0 + jnp.exp(-v))


def _rows(tm, n):
    return pl.BlockSpec((tm, n), lambda i: (i, 0))


def _whole(shape):
    return pl.BlockSpec(shape, lambda i: tuple(0 for _ in shape))


def _in_proj(x, g_mix, w_in_t, b_in):
    tm = ROW_TM

    def body(x_ref, g_ref, w_ref, b_ref, h_ref, qkv_ref, glu_ref, gl_ref):
        xv = x_ref[...]
        r = lax.rsqrt(jnp.mean(xv * xv, axis=-1, keepdims=True) + EPS)
        h = (xv * r * g_ref[...]).astype(BF16)
        h_ref[...] = h
        qkv_ref[...] = (_dot_nt(h, w_ref[0:GLU_OFF, :]) + b_ref[:, 0:GLU_OFF]).astype(BF16)
        glu_ref[...] = (_dot_nt(h, w_ref[GLU_OFF:GATE_OFF, :]) + b_ref[:, GLU_OFF:GATE_OFF]).astype(BF16)
        gl_ref[...] = (_dot_nt(h, w_ref[GATE_OFF:INW, :]) + b_ref[:, GATE_OFF:INW]).astype(BF16)

    return pl.pallas_call(
        body, name="in_proj", grid=(T // tm,),
        in_specs=[_rows(tm, D), _whole((1, D)), _whole((INW, D)), _whole((1, INW))],
        out_specs=[_rows(tm, D), _rows(tm, QKVW), _rows(tm, 2 * C), _rows(tm, 2 * D)],
        out_shape=[jax.ShapeDtypeStruct((T, D), BF16), jax.ShapeDtypeStruct((T, QKVW), BF16),
                   jax.ShapeDtypeStruct((T, 2 * C), BF16), jax.ShapeDtypeStruct((T, 2 * D), BF16)],
        compiler_params=_params("parallel"),
    )(x, g_mix, w_in_t, b_in)


GROWS = GROUP * BLK
BAND = 2 * BLK


def _band(i):
    rb = pl.multiple_of(jnp.maximum(i - 1, 0) * BLK, BLK)
    row = lax.broadcasted_iota(jnp.int32, (GROWS, BAND), 0)
    kpos = rb + lax.broadcasted_iota(jnp.int32, (GROWS, BAND), 1)
    qpos = i * BLK + jnp.bitwise_and(row, BLK - 1)
    return rb, jnp.logical_and(kpos <= qpos, kpos > qpos - BLK)


def _sink_column(sink_ref, g):
    head = lax.shift_right_logical(lax.broadcasted_iota(jnp.int32, (GROWS, 1), 0), 7)
    col = jnp.full((GROWS, 1), sink_ref[0, g * GROUP], F32)
    for hh in range(1, GROUP):
        col = jnp.where(head == hh, sink_ref[0, g * GROUP + hh], col)
    return col


def _attn_fwd(qkv, sinks):
    def body(sink_ref, qkv_ref, o_ref, lse_ref, s_ref, p_ref):
        i = pl.program_id(0)
        r0 = pl.multiple_of(i * BLK, BLK)
        rb, valid = _band(i)
        for g in range(NKV):
            kband = qkv_ref[pl.ds(rb, BAND), AW + g * HD:AW + (g + 1) * HD]
            vband = qkv_ref[pl.ds(rb, BAND), AW + KVW + g * HD:AW + KVW + (g + 1) * HD]
            for hh in range(GROUP):
                h = g * GROUP + hh
                s_ref[hh * BLK:(hh + 1) * BLK, :] = _dot_nt(qkv_ref[pl.ds(r0, BLK), h * HD:(h + 1) * HD], kband)
            s = jnp.where(valid, s_ref[...] * SCALE, NEG)
            sink = _sink_column(sink_ref, g)
            m = jnp.maximum(jnp.max(s, axis=-1, keepdims=True), sink)
            p = jnp.exp(s - m)
            den = jnp.sum(p, axis=-1, keepdims=True) + jnp.exp(sink - m)
            p_ref[...] = (p * (1.0 / den)).astype(BF16)
            lse = m + jnp.log(den)
            for hh in range(GROUP):
                h = g * GROUP + hh
                o_ref[:, h * HD:(h + 1) * HD] = _dot(p_ref[hh * BLK:(hh + 1) * BLK, :], vband).astype(BF16)
                lse_ref[:, h:h + 1] = lse[hh * BLK:(hh + 1) * BLK]

    return pl.pallas_call(
        body, name="attn_fwd", grid=(T // BLK,),
        in_specs=[pl.BlockSpec(memory_space=pltpu.SMEM), _whole((T, QKVW))],
        out_specs=[_rows(BLK, AW), _rows(BLK, NQ)],
        out_shape=[jax.ShapeDtypeStruct((T, AW), BF16), jax.ShapeDtypeStruct((T, NQ), F32)],
        scratch_shapes=[pltpu.VMEM((GROWS, BAND), F32), pltpu.VMEM((GROWS, BAND), BF16)],
        compiler_params=_params("parallel"),
    )(sinks, qkv)


CONV_TM = 256
CONV_SUB = 32


def _glu(ab):
    a = ab[:, 0:C].astype(F32)
    b = ab[:, C:2 * C].astype(F32)
    return a * _sigmoid(b)


SUBLANES = 8


def _shifted_copies(ref):
    rows = ref.shape[1] - SUBLANES
    for r in range(1, SUBLANES):
        ref[r, 0:rows, :] = ref[0, r:r + rows, :]


def _shifted_rows(ref, start, size):
    r = start % SUBLANES
    return ref[r, start - r:start - r + size, :]


def _conv_fwd(glu, conv_w, conv_b, ln_g, ln_b, dep):
    tm = CONV_TM

    def body(cur_ref, prev_ref, w_ref, cb_ref, g_ref, b_ref, dep_ref, u_ref, c_ref, zs_ref):
        i = pl.program_id(0)
        zprev = _glu(prev_ref[tm - HALO:tm, :])
        zs_ref[0, 0:HALO, :] = jnp.where(i > 0, zprev, 0.0)
        zs_ref[0, HALO:HALO + tm, :] = _glu(cur_ref[...])
        _shifted_copies(zs_ref)
        for s in range(tm // CONV_SUB):
            base = HALO + s * CONV_SUB - (KW - 1)
            acc = jnp.broadcast_to(cb_ref[...], (CONV_SUB, C))
            for j in range(KW):
                acc = acc + w_ref[j:j + 1, :] * _shifted_rows(zs_ref, base + j, CONV_SUB)
            rows = slice(s * CONV_SUB, (s + 1) * CONV_SUB)
            u_ref[rows, :] = acc
            mu = jnp.mean(acc, axis=-1, keepdims=True)
            xc = acc - mu
            var = jnp.mean(xc * xc, axis=-1, keepdims=True)
            y = xc * lax.rsqrt(var + EPS) * g_ref[...] + b_ref[...]
            c_ref[rows, :] = (y * _sigmoid(y)).astype(BF16)

    return pl.pallas_call(
        body, name="conv_fwd", grid=(T // tm,),
        in_specs=[_rows(tm, 2 * C),
                  pl.BlockSpec((tm, 2 * C), lambda i: (jnp.maximum(i - 1, 0), 0)),
                  _whole((KW, C)), _whole((1, C)), _whole((1, C)), _whole((1, C)), _whole((8, 128))],
        out_specs=[_rows(tm, C), _rows(tm, C)],
        out_shape=[jax.ShapeDtypeStruct((T, C), F32), jax.ShapeDtypeStruct((T, C), BF16)],
        scratch_shapes=[pltpu.VMEM((SUBLANES, HALO + tm, C), F32)],
        compiler_params=_params("parallel"),
    )(glu, glu, conv_w, conv_b, ln_g, ln_b, dep)


def _branch_outputs(o, cact, wap_ref, wcp_ref, bcp_ref):
    ya = jnp.concatenate([_dot(o, wap_ref[s]) for s in range(N_CHIPS)], axis=1)
    yc = jnp.concatenate([_dot(cact, wcp_ref[s]) for s in range(N_CHIPS)], axis=1) + bcp_ref[...]
    return ya, yc


def _mix_out(x, o, cact, gl, w_ap, w_cp, b_cp, w_out, dep):
    tm = ROW_TM

    def body(x_ref, o_ref, c_ref, gl_ref, wap_ref, wcp_ref, bcp_ref, wo_ref, dep_ref, mg_ref, x1_ref):
        ya, yc = _branch_outputs(o_ref[...], c_ref[...], wap_ref, wcp_ref, bcp_ref)
        g0 = _sigmoid(gl_ref[:, 0:D].astype(F32))
        g1 = _sigmoid(gl_ref[:, D:2 * D].astype(F32))
        mg = (g0 * ya + g1 * yc).astype(BF16)
        mg_ref[...] = mg
        x1_ref[...] = x_ref[...] + _dot(mg, wo_ref[...])

    return pl.pallas_call(
        body, name="mix_out", grid=(T // tm,),
        in_specs=[_rows(tm, D), _rows(tm, AW), _rows(tm, C), _rows(tm, 2 * D),
                  _whole((N_CHIPS, AW, D // N_CHIPS)), _whole((N_CHIPS, C, D // N_CHIPS)), _whole((1, D)),
                  _whole((D, D)), _whole((8, 128))],
        out_specs=[_rows(tm, D), _rows(tm, D)],
        out_shape=[jax.ShapeDtypeStruct((T, D), BF16), jax.ShapeDtypeStruct((T, D), F32)],
        compiler_params=_params("parallel"),
    )(x, o, cact, gl, w_ap, w_cp, b_cp, w_out, dep)


def _ffn_in(x1, g_ffn, w_fi, dep):
    tm = ROW_TM

    def body(x_ref, g_ref, w_ref, dep_ref, h_ref, gu_ref, act_ref):
        xv = x_ref[...]
        r = lax.rsqrt(jnp.mean(xv * xv, axis=-1, keepdims=True) + EPS)
        h = (xv * r * g_ref[...]).astype(BF16)
        h_ref[...] = h
        for s in range(N_CHIPS // 2):
            c0 = s * FSH
            gate = _dot(h, w_ref[s])
            up = _dot(h, w_ref[s + N_CHIPS // 2])
            gu_ref[:, c0:c0 + FSH] = gate.astype(BF16)
            gu_ref[:, DFF + c0:DFF + c0 + FSH] = up.astype(BF16)
            act_ref[:, c0:c0 + FSH] = (gate * _sigmoid(gate) * up).astype(BF16)

    return pl.pallas_call(
        body, name="ffn_in", grid=(T // tm,),
        in_specs=[_rows(tm, D), _whole((1, D)), _const((N_CHIPS, D, FSH)), _whole((8, 128))],
        out_specs=[_rows(tm, D), _rows(tm, 2 * DFF), _rows(tm, DFF)],
        out_shape=[jax.ShapeDtypeStruct((T, D), BF16), jax.ShapeDtypeStruct((T, 2 * DFF), BF16),
                   jax.ShapeDtypeStruct((T, DFF), BF16)],
        compiler_params=_params("parallel"),
    )(x1, g_ffn, w_fi, dep)


def _ffn_out_loss(x1, act, w_dn, g_final, target):
    tm = ROW_TM

    def body(x_ref, a_ref, w_ref, g_ref, t_ref, dx_ref, dxb_ref, dg_ref, loss_ref):
        i = pl.program_id(0)
        x2 = x_ref[...] + _dot(a_ref[...], w_ref[...])
        r = lax.rsqrt(jnp.mean(x2 * x2, axis=-1, keepdims=True) + EPS)
        xh = x2 * r
        g = g_ref[...]
        err = xh * g - t_ref[...]
        dy = err * (1.0 / D)
        dyg = dy * g
        dx = r * (dyg - xh * jnp.mean(dyg * xh, axis=-1, keepdims=True))
        dx_ref[...] = dx
        dxb_ref[...] = dx.astype(BF16)
        part = 0.5 * jnp.sum(jnp.mean(err * err, axis=-1, keepdims=True), axis=0, keepdims=True)

        @pl.when(i == 0)
        def _():
            dg_ref[...] = jnp.zeros_like(dg_ref)
            loss_ref[...] = jnp.zeros_like(loss_ref)

        dg_ref[...] += jnp.sum(dy * xh, axis=0, keepdims=True)
        loss_ref[...] += jnp.broadcast_to(part, loss_ref.shape)

    return pl.pallas_call(
        body, name="ffn_out_loss", grid=(T // tm,),
        in_specs=[_rows(tm, D), _rows(tm, DFF), _whole((DFF, D)), _whole((1, D)), _rows(tm, D)],
        out_specs=[_rows(tm, D), _rows(tm, D), _whole((1, D)), _whole((1, 128))],
        out_shape=[jax.ShapeDtypeStruct((T, D), F32), jax.ShapeDtypeStruct((T, D), BF16),
                   jax.ShapeDtypeStruct((1, D), F32), jax.ShapeDtypeStruct((1, 128), F32)],
        compiler_params=_params("arbitrary"),
    )(x1, act, w_dn, g_final, target)


def _const(shape):
    return pl.BlockSpec(shape, lambda i: tuple(0 for _ in shape), pipeline_mode=pl.Buffered(1))


def _ffn_bwd(dx2, dx2b, gu, x1, g_ffn, w_dn_t, w_fi_t):
    tm = ROW_TM // 2

    def body(dx_ref, dxb_ref, gu_ref, x_ref, g_ref, wdn_ref, wfi_ref,
             dgu_ref, dx1_ref, dx1b_ref, dg_ref):
        i = pl.program_id(0)
        dxb = dxb_ref[...]
        dh = jnp.zeros((tm, D), F32)
        for k in range(N_CHIPS // 2):
            c0 = k * FSH
            dact = _dot_nt(dxb, wdn_ref[c0:c0 + FSH, :])
            gate = gu_ref[:, c0:c0 + FSH].astype(F32)
            up = gu_ref[:, DFF + c0:DFF + c0 + FSH].astype(F32)
            s = _sigmoid(gate)
            dup = (dact * gate * s).astype(BF16)
            dgate = (dact * up * s * (1.0 + gate * (1.0 - s))).astype(BF16)
            dgu_ref[:, c0:c0 + FSH] = dgate
            dgu_ref[:, DFF + c0:DFF + c0 + FSH] = dup
            dh = dh + _dot_nt(dgate, wfi_ref[k]) + _dot_nt(dup, wfi_ref[k + N_CHIPS // 2])
        xv = x_ref[...]
        r = lax.rsqrt(jnp.mean(xv * xv, axis=-1, keepdims=True) + EPS)
        xh = xv * r
        dhg = dh * g_ref[...]
        dx1 = dx_ref[...] + r * (dhg - xh * jnp.mean(dhg * xh, axis=-1, keepdims=True))
        dx1_ref[...] = dx1
        dx1b_ref[...] = dx1.astype(BF16)

        @pl.when(i == 0)
        def _():
            dg_ref[...] = jnp.zeros_like(dg_ref)

        dg_ref[...] += jnp.sum(dh * xh, axis=0, keepdims=True)

    return pl.pallas_call(
        body, name="ffn_bwd", grid=(T // tm,),
        in_specs=[_rows(tm, D), _rows(tm, D), _rows(tm, 2 * DFF), _rows(tm, D), _whole((1, D)),
                  _const((DFF, D)), _const((N_CHIPS, D, FSH))],
        out_specs=[_rows(tm, 2 * DFF), _rows(tm, D), _rows(tm, D), _whole((1, D))],
        out_shape=[jax.ShapeDtypeStruct((T, 2 * DFF), BF16), jax.ShapeDtypeStruct((T, D), F32),
                   jax.ShapeDtypeStruct((T, D), BF16), jax.ShapeDtypeStruct((1, D), F32)],
        compiler_params=_params("arbitrary"),
    )(dx2, dx2b, gu, x1, g_ffn, w_dn_t, w_fi_t)


def _mix_bwd(dx1b, gl, o, cact, b_cp, w_out, w_ap, w_cp, dep):
    tm = ROW_TM

    def body(dx_ref, gl_ref, o_ref, c_ref, bcp_ref, wo_ref, wap_ref, wcp_ref, dep_ref,
             dya_ref, dyc_ref, dgl_ref, do_ref, dc_ref, db_ref):
        i = pl.program_id(0)
        dm = _dot_nt(dx_ref[...], wo_ref[...])
        ya, yc = _branch_outputs(o_ref[...], c_ref[...], wap_ref, wcp_ref, bcp_ref)
        g0 = _sigmoid(gl_ref[:, 0:D].astype(F32))
        g1 = _sigmoid(gl_ref[:, D:2 * D].astype(F32))
        dya = dm * g0
        dyc = dm * g1
        dgl_ref[:, 0:D] = (dya * ya * (1.0 - g0)).astype(BF16)
        dgl_ref[:, D:2 * D] = (dyc * yc * (1.0 - g1)).astype(BF16)
        dyab = dya.astype(BF16)
        dycb = dyc.astype(BF16)
        dya_ref[...] = dyab
        dyc_ref[...] = dycb
        sw = D // N_CHIPS
        do = jnp.zeros((tm, AW), F32)
        dcv = jnp.zeros((tm, C), F32)
        for s in range(N_CHIPS):
            do = do + _dot_nt(dyab[:, s * sw:(s + 1) * sw], wap_ref[s])
            dcv = dcv + _dot_nt(dycb[:, s * sw:(s + 1) * sw], wcp_ref[s])
        do_ref[...] = do.astype(BF16)
        dc_ref[...] = dcv.astype(BF16)

        @pl.when(i == 0)
        def _():
            db_ref[...] = jnp.zeros_like(db_ref)

        db_ref[...] += jnp.sum(dyc, axis=0, keepdims=True)

    return pl.pallas_call(
        body, name="mix_bwd", grid=(T // tm,),
        in_specs=[_rows(tm, D), _rows(tm, 2 * D), _rows(tm, AW), _rows(tm, C), _whole((1, D)),
                  _whole((D, D)), _whole((N_CHIPS, AW, D // N_CHIPS)), _whole((N_CHIPS, C, D // N_CHIPS)),
                  _whole((8, 128))],
        out_specs=[_rows(tm, D), _rows(tm, D), _rows(tm, 2 * D), _rows(tm, AW), _rows(tm, C),
                   _whole((1, D))],
        out_shape=[jax.ShapeDtypeStruct((T, D), BF16), jax.ShapeDtypeStruct((T, D), BF16),
                   jax.ShapeDtypeStruct((T, 2 * D), BF16), jax.ShapeDtypeStruct((T, AW), BF16),
                   jax.ShapeDtypeStruct((T, C), BF16), jax.ShapeDtypeStruct((1, D), F32)],
        compiler_params=_params("arbitrary"),
    )(dx1b, gl, o, cact, b_cp, w_out, w_ap, w_cp, dep)


def _conv_bwd(glu, u, dc, conv_w, ln_g, ln_b, dep):
    tm = CONV_TM
    nblk = T // tm

    def du_of(uv, dcv, g_ref, b_ref):
        mu = jnp.mean(uv, axis=-1, keepdims=True)
        xc = uv - mu
        var = jnp.mean(xc * xc, axis=-1, keepdims=True)
        rstd = lax.rsqrt(var + EPS)
        xh = xc * rstd
        y = xh * g_ref[...] + b_ref[...]
        sg = _sigmoid(y)
        dy = dcv * (sg * (1.0 + y * (1.0 - sg)))
        dxh = dy * g_ref[...]
        du = rstd * (dxh - jnp.mean(dxh, axis=-1, keepdims=True)
                     - xh * jnp.mean(dxh * xh, axis=-1, keepdims=True))
        return du, dy, xh

    def body(cur_ref, prev_ref, u_ref, un_ref, dc_ref, dcn_ref, w_ref, g_ref, b_ref, dep_ref,
             dglu_ref, dw_ref, dcb_ref, dg_ref, db_ref, zs_ref, dus_ref):
        i = pl.program_id(0)

        @pl.when(i == 0)
        def _():
            dw_ref[...] = jnp.zeros_like(dw_ref)
            dcb_ref[...] = jnp.zeros_like(dcb_ref)
            dg_ref[...] = jnp.zeros_like(dg_ref)
            db_ref[...] = jnp.zeros_like(db_ref)

        zprev = _glu(prev_ref[tm - HALO:tm, :])
        zs_ref[0, 0:HALO, :] = jnp.where(i > 0, zprev, 0.0)
        zs_ref[0, HALO:HALO + tm, :] = _glu(cur_ref[...])
        _shifted_copies(zs_ref)

        dun, _, _ = du_of(un_ref[0:HALO, :], dcn_ref[0:HALO, :].astype(F32), g_ref, b_ref)
        dus_ref[0, tm:tm + HALO, :] = jnp.where(i < nblk - 1, dun, 0.0)
        dg_acc = jnp.zeros((1, C), F32)
        db_acc = jnp.zeros((1, C), F32)
        dcb_acc = jnp.zeros((1, C), F32)
        for s in range(tm // CONV_SUB):
            rows = slice(s * CONV_SUB, (s + 1) * CONV_SUB)
            du, dy, xh = du_of(u_ref[rows, :], dc_ref[rows, :].astype(F32), g_ref, b_ref)
            dus_ref[0, rows, :] = du
            dg_acc = dg_acc + jnp.sum(dy * xh, axis=0, keepdims=True)
            db_acc = db_acc + jnp.sum(dy, axis=0, keepdims=True)
            dcb_acc = dcb_acc + jnp.sum(du, axis=0, keepdims=True)
        dg_ref[...] += dg_acc
        db_ref[...] += db_acc
        dcb_ref[...] += dcb_acc
        _shifted_copies(dus_ref)

        for j in range(KW):
            acc = jnp.zeros((CONV_SUB, C), F32)
            for s in range(tm // CONV_SUB):
                base = HALO + s * CONV_SUB - (KW - 1) + j
                acc = acc + dus_ref[0, s * CONV_SUB:(s + 1) * CONV_SUB, :] * _shifted_rows(zs_ref, base, CONV_SUB)
            dw_ref[j:j + 1, :] += jnp.sum(acc, axis=0, keepdims=True)

        for s in range(tm // CONV_SUB):
            rows = slice(s * CONV_SUB, (s + 1) * CONV_SUB)
            dz = jnp.zeros((CONV_SUB, C), F32)
            for j in range(KW):
                dz = dz + w_ref[j:j + 1, :] * _shifted_rows(dus_ref, s * CONV_SUB + (KW - 1) - j, CONV_SUB)
            a = cur_ref[rows, 0:C].astype(F32)
            sb = _sigmoid(cur_ref[rows, C:2 * C].astype(F32))
            dglu_ref[rows, 0:C] = (dz * sb).astype(BF16)
            dglu_ref[rows, C:2 * C] = (dz * a * sb * (1.0 - sb)).astype(BF16)

    nxt = lambda i: (jnp.minimum(i + 1, nblk - 1), 0)
    return pl.pallas_call(
        body, name="conv_bwd", grid=(nblk,),
        in_specs=[_rows(tm, 2 * C),
                  pl.BlockSpec((tm, 2 * C), lambda i: (jnp.maximum(i - 1, 0), 0)),
                  _rows(tm, C), pl.BlockSpec((tm, C), nxt),
                  _rows(tm, C), pl.BlockSpec((tm, C), nxt),
                  _whole((KW, C)), _whole((1, C)), _whole((1, C)), _whole((8, 128))],
        out_specs=[_rows(tm, 2 * C), _whole((KW, C)), _whole((1, C)), _whole((1, C)), _whole((1, C))],
        out_shape=[jax.ShapeDtypeStruct((T, 2 * C), BF16), jax.ShapeDtypeStruct((KW, C), F32),
                   jax.ShapeDtypeStruct((1, C), F32), jax.ShapeDtypeStruct((1, C), F32),
                   jax.ShapeDtypeStruct((1, C), F32)],
        scratch_shapes=[pltpu.VMEM((SUBLANES, HALO + tm, C), F32), pltpu.VMEM((SUBLANES, tm + HALO, C), F32)],
        compiler_params=_params("arbitrary"),
    )(glu, glu, u, u, dc, dc, conv_w, ln_g, ln_b, dep)


def _attn_bwd(qkv, o, do, lse, sinks, dep):
    def body(sink_ref, qkv_ref, o_ref, do_ref, lse_ref, dep_ref, dq_ref, dkv_ref, ds_ref,
             s_ref, dp_ref, p_ref, dsb_ref):
        i = pl.program_id(0)

        @pl.when(i == 0)
        def _():
            dkv_ref[...] = jnp.zeros_like(dkv_ref)
            ds_ref[...] = jnp.zeros_like(ds_ref)

        r0 = pl.multiple_of(i * BLK, BLK)
        rb, valid = _band(i)
        for g in range(NKV):
            kband = qkv_ref[pl.ds(rb, BAND), AW + g * HD:AW + (g + 1) * HD]
            vband = qkv_ref[pl.ds(rb, BAND), AW + KVW + g * HD:AW + KVW + (g + 1) * HD]
            lse_parts, dl_parts = [], []
            for hh in range(GROUP):
                h = g * GROUP + hh
                hcol = slice(h * HD, (h + 1) * HD)
                doh = do_ref[:, hcol]
                s_ref[hh * BLK:(hh + 1) * BLK, :] = _dot_nt(qkv_ref[pl.ds(r0, BLK), hcol], kband)
                dp_ref[hh * BLK:(hh + 1) * BLK, :] = _dot_nt(doh, vband)
                lse_parts.append(lse_ref[:, h:h + 1])
                dl_parts.append(jnp.sum(doh.astype(F32) * o_ref[:, hcol].astype(F32), axis=-1, keepdims=True))
            lse = jnp.concatenate(lse_parts, axis=0)
            dl = jnp.concatenate(dl_parts, axis=0)
            p = jnp.where(valid, jnp.exp(s_ref[...] * SCALE - lse), 0.0)
            p_ref[...] = p.astype(BF16)
            dsb_ref[...] = (p * (dp_ref[...] - dl)).astype(BF16)
            dsink = -(jnp.exp(_sink_column(sink_ref, g) - lse) * dl)
            dk = jnp.zeros((BAND, HD), F32)
            dv = jnp.zeros((BAND, HD), F32)
            for hh in range(GROUP):
                h = g * GROUP + hh
                hcol = slice(h * HD, (h + 1) * HD)
                rows = slice(hh * BLK, (hh + 1) * BLK)
                dq_ref[:, hcol] = (_dot(dsb_ref[rows, :], kband) * SCALE).astype(BF16)
                dk = dk + _dot_tn(dsb_ref[rows, :], qkv_ref[pl.ds(r0, BLK), hcol])
                dv = dv + _dot_tn(p_ref[rows, :], do_ref[:, hcol])
                ds_ref[h:h + 1, :] += jnp.broadcast_to(jnp.sum(dsink[rows], axis=0, keepdims=True), (1, 128))
            dkv_ref[pl.ds(rb, BAND), g * HD:(g + 1) * HD] += dk * SCALE
            dkv_ref[pl.ds(rb, BAND), KVW + g * HD:KVW + (g + 1) * HD] += dv

    return pl.pallas_call(
        body, name="attn_bwd", grid=(T // BLK,),
        in_specs=[pl.BlockSpec(memory_space=pltpu.SMEM), _whole((T, QKVW)),
                  _rows(BLK, AW), _rows(BLK, AW), _rows(BLK, NQ), _whole((8, 128))],
        out_specs=[_rows(BLK, AW), _whole((T, 2 * KVW)), _whole((NQ, 128))],
        out_shape=[jax.ShapeDtypeStruct((T, AW), BF16), jax.ShapeDtypeStruct((T, 2 * KVW), F32),
                   jax.ShapeDtypeStruct((NQ, 128), F32)],
        scratch_shapes=[pltpu.VMEM((GROWS, BAND), F32), pltpu.VMEM((GROWS, BAND), F32),
                        pltpu.VMEM((GROWS, BAND), BF16), pltpu.VMEM((GROWS, BAND), BF16)],
        compiler_params=_params("arbitrary"),
    )(sinks, qkv, o, do, lse, dep)


PROJ_PARTS = [(0, AW), (AW, QKVW), (GLU_OFF, GATE_OFF), (GATE_OFF, INW)]


def _in_proj_bwd(dq, dkv, dglu, dgl, x, dx1, g_mix, w_in_t, dep):
    tm = ROW_TM

    def body(dq_ref, dkv_ref, dglu_ref, dgl_ref, x_ref, dx1_ref, g_ref, w_ref, dep_ref,
             gx_ref, dg_ref, db_ref):
        i = pl.program_id(0)

        @pl.when(i == 0)
        def _():
            dg_ref[...] = jnp.zeros_like(dg_ref)
            db_ref[...] = jnp.zeros_like(db_ref)

        dh = jnp.zeros((tm, D), F32)
        for part_ref, (lo, hi) in zip((dq_ref, dkv_ref, dglu_ref, dgl_ref), PROJ_PARTS):
            part = part_ref[...]
            dh = dh + _dot(part.astype(BF16), w_ref[lo:hi, :])
            db_ref[:, lo:hi] += jnp.sum(part.astype(F32), axis=0, keepdims=True)
        xv = x_ref[...]
        r = lax.rsqrt(jnp.mean(xv * xv, axis=-1, keepdims=True) + EPS)
        xh = xv * r
        dhg = dh * g_ref[...]
        gx_ref[...] = dx1_ref[...] + r * (dhg - xh * jnp.mean(dhg * xh, axis=-1, keepdims=True))
        dg_ref[...] += jnp.sum(dh * xh, axis=0, keepdims=True)

    return pl.pallas_call(
        body, name="in_proj_bwd", grid=(T // tm,),
        in_specs=[_rows(tm, AW), _rows(tm, 2 * KVW), _rows(tm, 2 * C), _rows(tm, 2 * D),
                  _rows(tm, D), _rows(tm, D), _whole((1, D)), _const((INW, D)), _whole((8, 128))],
        out_specs=[_rows(tm, D), _whole((1, D)), _whole((1, INW))],
        out_shape=[jax.ShapeDtypeStruct((T, D), F32), jax.ShapeDtypeStruct((1, D), F32),
                   jax.ShapeDtypeStruct((1, INW), F32)],
        compiler_params=_params("arbitrary"),
    )(dq, dkv, dglu, dgl, x, dx1, g_mix, w_in_t, dep)


def _grad_w_in_t(h, dq, dkv, dglu, dgl):
    tn, chunk = 512, 256

    def body(h_ref, dq_ref, dkv_ref, dglu_ref, dgl_ref, o_ref):
        hv = h_ref[...]
        for part_ref, (lo, hi) in zip((dq_ref, dkv_ref, dglu_ref, dgl_ref), PROJ_PARTS):
            for c0 in range(0, hi - lo, chunk):
                o_ref[lo + c0:lo + c0 + chunk, :] = _dot_tn(
                    part_ref[:, c0:c0 + chunk].astype(BF16), hv).astype(BF16)

    return pl.pallas_call(
        body, name="grad_w_in", grid=(D // tn,),
        in_specs=[pl.BlockSpec((T, tn), lambda j: (0, j)), _const((T, AW)), _const((T, 2 * KVW)),
                  _const((T, 2 * C)), _const((T, 2 * D))],
        out_specs=pl.BlockSpec((INW, tn), lambda j: (0, j)),
        out_shape=jax.ShapeDtypeStruct((INW, D), BF16),
        compiler_params=_params("parallel"),
    )(h, dq, dkv, dglu, dgl)


def _grad_w(a, b, name, tk, tn, col_sharded):
    k, n = a.shape[1], b.shape[1]

    single = n == tn
    sw = n // N_CHIPS

    def body(a_ref, b_ref, o_ref, at_ref):
        if single:
            res = _dot_tn(a_ref[...], b_ref[...]).astype(BF16)
            if col_sharded:
                for s in range(N_CHIPS):
                    o_ref[s] = res[:, s * sw:(s + 1) * sw]
            else:
                o_ref[...] = res
            return

        @pl.when(pl.program_id(1) == 0)
        def _():
            at_ref[...] = a_ref[...].T

        o_ref[...] = _dot(at_ref[...], b_ref[...]).astype(BF16)

    if col_sharded and single:
        shape = (N_CHIPS, k, sw)
        out_spec = pl.BlockSpec((N_CHIPS, tk, sw), lambda i, j: (0, i, 0))
    elif col_sharded:
        per = sw // tn
        shape = (N_CHIPS, k, sw)
        out_spec = pl.BlockSpec((None, tk, tn), lambda i, j: (j // per, i, j % per))
    else:
        shape = (1, k, n)
        out_spec = pl.BlockSpec((None, tk, tn), lambda i, j: (0, i, j))
    out = pl.pallas_call(
        body, name=name, grid=(k // tk, n // tn),
        in_specs=[pl.BlockSpec((T, tk), lambda i, j: (0, i)), pl.BlockSpec((T, tn), lambda i, j: (0, j))],
        out_specs=out_spec,
        out_shape=jax.ShapeDtypeStruct(shape, BF16),
        scratch_shapes=[pltpu.VMEM((tk, T), BF16)],
        compiler_params=_params("parallel", "arbitrary"),
    )(a, b)
    return out if col_sharded else out.reshape(N_CHIPS, k // N_CHIPS, n)


HBM_SPEC = pl.BlockSpec(memory_space=pltpu.HBM)


def _place():
    x, y, c = lax.axis_index("x"), lax.axis_index("y"), lax.axis_index("c")
    chips = [(1 - x, y), (x, 1 - y), (1 - x, 1 - y)]
    return x, y, c, chips


SEM_SPEC = pl.BlockSpec(memory_space=pltpu.SEMAPHORE)
ANY_SPEC = pl.BlockSpec(memory_space=pl.ANY)
VMEM_SPEC = pl.BlockSpec(memory_space=pltpu.VMEM)
EFFECT = pltpu.SideEffectType.DATAFLOW_SIDE_EFFECTING


def _gather_ends(src, land, x, y, c, chips):
    kh = src.shape[0] // 2
    s_me = 2 * x + y
    ends = [(src.at[pl.ds(c * kh, kh)], land.at[s_me, pl.ds(c * kh, kh)], (*chip, c)) for chip in chips]
    return ends + [(src, land.at[s_me], (x, y, 1 - c))]


def _reduce_ends(src, land, x, y, c, chips):
    return [(src.at[2 * chip[0] + chip[1]], land.at[j], (*chip, c)) for j, chip in enumerate(chips)]


def _chip_copies(ends, srcs, lands, send_sems, recv_sems, first=0):
    x, y, c, chips = _place()
    copies = []
    for src, land in zip(srcs, lands):
        peers = ends(src, land, x, y, c, chips)
        for s, d, to in peers:
            k = first * len(peers) + len(copies)
            copies.append(pltpu.make_async_remote_copy(
                src_ref=s, dst_ref=d, send_sem=send_sems.at[k], recv_sem=recv_sems.at[k],
                device_id=to, device_id_type=MESH))
    return copies


GATHER_PEERS, REDUCE_PEERS = 4, 3


def _chip_start(name, ends, peers, srcs, lands):
    n = len(srcs)

    def body(*refs):
        copies = _chip_copies(ends, refs[:n], refs[n:2 * n], refs[2 * n], refs[2 * n + 1])
        for cp in copies:
            cp.start()
        token = refs[-1]
        token[...] = jnp.zeros_like(token)

    hbm = lambda a: pltpu.HBM(a.shape, a.dtype)
    res = pl.pallas_call(
        body, name=name,
        out_shape=(pltpu.SemaphoreType.DMA((peers * n,)), pltpu.SemaphoreType.DMA((peers * n,)),
                   *[hbm(a) for a in srcs], *[hbm(a) for a in lands],
                   jax.ShapeDtypeStruct((8, 128), F32)),
        in_specs=[HBM_SPEC] * (2 * n),
        out_specs=(SEM_SPEC, SEM_SPEC, *[HBM_SPEC] * (2 * n), VMEM_SPEC),
        input_output_aliases={i: 2 + i for i in range(2 * n)},
        compiler_params=pltpu.CompilerParams(has_side_effects=EFFECT),
    )(*[pltpu.with_memory_space_constraint(a, pltpu.HBM) for a in (*srcs, *lands)])
    return res[0], res[1], list(res[2:2 + n]), list(res[2 + n:2 + 2 * n]), res[-1]


def _chip_wait(name, ends, send_sems, recv_sems, srcs, lands, after, first=0):
    n, na = len(srcs), len(after)

    def body(*refs):
        copies = _chip_copies(ends, refs[:n], refs[n:2 * n], refs[2 * n], refs[2 * n + 1], first)
        for cp in copies:
            cp.wait_send()
            cp.wait_recv()

    hbm = lambda a: pltpu.HBM(a.shape, a.dtype)
    res = pl.pallas_call(
        body, name=name,
        out_shape=tuple(hbm(a) for a in (*srcs, *lands)),
        in_specs=[HBM_SPEC] * (2 * n) + [SEM_SPEC, SEM_SPEC] + [ANY_SPEC] * na,
        out_specs=tuple([HBM_SPEC] * (2 * n)),
        input_output_aliases={i: i for i in range(2 * n)},
        compiler_params=pltpu.CompilerParams(has_side_effects=EFFECT),
    )(*srcs, *lands, send_sems, recv_sems, *after)
    return list(res[:n]), list(res[n:])


def _forward_copies(lands, send_sems, recv_sems):
    x, y, c, chips = _place()
    copies = []
    for land in lands:
        kh = land.shape[1] // 2
        for chip in chips:
            blk = land.at[2 * chip[0] + chip[1], pl.ds(c * kh, kh)]
            k = len(copies)
            copies.append(pltpu.make_async_remote_copy(
                src_ref=blk, dst_ref=blk, send_sem=send_sems.at[k], recv_sem=recv_sems.at[k],
                device_id=(x, y, 1 - c), device_id_type=MESH))
    return copies


def _gather_relay(name, send_sems, recv_sems, srcs, lands, after, first):
    n, na = len(srcs), len(after)

    def body(*refs):
        land_refs = refs[n:2 * n]
        for cp in _chip_copies(_gather_ends, refs[:n], land_refs, refs[2 * n], refs[2 * n + 1], first):
            cp.wait_send()
            cp.wait_recv()
        out = refs[2 * n + 2 + na:]
        for cp in _forward_copies(land_refs, out[0], out[1]):
            cp.start()
        out[-1][...] = jnp.zeros_like(out[-1])

    hbm = lambda a: pltpu.HBM(a.shape, a.dtype)
    res = pl.pallas_call(
        body, name=name,
        out_shape=(pltpu.SemaphoreType.DMA((3 * n,)), pltpu.SemaphoreType.DMA((3 * n,)),
                   *[hbm(a) for a in lands], jax.ShapeDtypeStruct((8, 128), F32)),
        in_specs=[HBM_SPEC] * (2 * n) + [SEM_SPEC, SEM_SPEC] + [ANY_SPEC] * na,
        out_specs=(SEM_SPEC, SEM_SPEC, *[HBM_SPEC] * n, VMEM_SPEC),
        input_output_aliases={n + i: 2 + i for i in range(n)},
        compiler_params=pltpu.CompilerParams(has_side_effects=EFFECT),
    )(*srcs, *lands, send_sems, recv_sems, *after)
    return res[0], res[1], list(res[2:2 + n]), res[-1]


def _forward_wait(name, send_sems, recv_sems, lands, after):
    n, na = len(lands), len(after)

    def body(*refs):
        for cp in _forward_copies(refs[:n], refs[n], refs[n + 1]):
            cp.wait_send()
            cp.wait_recv()

    hbm = lambda a: pltpu.HBM(a.shape, a.dtype)
    res = pl.pallas_call(
        body, name=name,
        out_shape=tuple(hbm(a) for a in lands),
        in_specs=[HBM_SPEC] * n + [SEM_SPEC, SEM_SPEC] + [ANY_SPEC] * na,
        out_specs=tuple([HBM_SPEC] * n),
        input_output_aliases={i: i for i in range(n)},
        compiler_params=pltpu.CompilerParams(has_side_effects=EFFECT),
    )(*lands, send_sems, recv_sems, *after)
    return list(res)


def _exchange_ends(src, land, x, y, c, chips):
    kh = src.shape[1] // 2
    return [(src.at[:, pl.ds((1 - c) * kh, kh)], land, (x, y, 1 - c))]


def _share_ends(src, land, x, y, c, chips):
    return [(src, land, (x, y, 1 - c))]


def _small_ends(src, land, x, y, c, chips):
    m = src.shape[0]
    rows = land.at[pl.ds((4 * x + 2 * y + c) * m, m)]
    peers = [(x, y, 1 - c)] + [(*chip, c) for chip in chips] + [(*chip, 1 - c) for chip in chips]
    return [(src, rows, to) for to in peers]


PAIR_PEERS, SMALL_PEERS = 1, 7


def _row_tile(k):
    for t in (256, 240, 128, 176, 64, 32, 16):
        if k % t == 0:
            return t
    raise ValueError(k)


def _pair_sum(c_idx, g, got, name):
    _, k, n = g.shape
    kh = k // 2
    tm = _row_tile(kh)
    nb = kh // tm

    def body(c_ref, g_ref, r_ref, o_ref):
        o_ref[...] = (g_ref[...].astype(F32) + r_ref[...].astype(F32)).astype(BF16)

    return pl.pallas_call(
        body, name=name,
        grid_spec=pltpu.PrefetchScalarGridSpec(
            num_scalar_prefetch=1, grid=(N_CHIPS, nb),
            in_specs=[pl.BlockSpec((1, tm, n), lambda s, i, c_ref: (s, c_ref[0] * nb + i, 0)),
                      pl.BlockSpec((1, tm, n), lambda s, i, c_ref: (s, i, 0))],
            out_specs=pl.BlockSpec((1, tm, n), lambda s, i, c_ref: (s, i, 0))),
        out_shape=jax.ShapeDtypeStruct((N_CHIPS, kh, n), BF16),
        compiler_params=_params("parallel", "parallel"),
    )(c_idx, g, got)


def _chip_sum(s_idx, mine, got, name):
    _, kh, n = mine.shape
    tm = _row_tile(kh)

    def body(s_ref, m_ref, r_ref, o_ref):
        acc = m_ref[0].astype(F32)
        for j in range(3):
            acc = acc + r_ref[j].astype(F32)
        o_ref[...] = acc

    return pl.pallas_call(
        body, name=name,
        grid_spec=pltpu.PrefetchScalarGridSpec(
            num_scalar_prefetch=1, grid=(kh // tm,),
            in_specs=[pl.BlockSpec((1, tm, n), lambda i, s_ref: (s_ref[0], i, 0)),
                      pl.BlockSpec((3, tm, n), lambda i, s_ref: (0, i, 0))],
            out_specs=pl.BlockSpec((tm, n), lambda i, s_ref: (i, 0))),
        out_shape=jax.ShapeDtypeStruct((kh, n), F32),
        compiler_params=_params("parallel"),
    )(s_idx, mine, got)


def _adamw_math(w, g, m, v):
    m = ADAM_B1 * m + (1.0 - ADAM_B1) * g
    v = ADAM_B2 * v + (1.0 - ADAM_B2) * (g * g)
    m_hat = m / (1.0 - ADAM_B1 ** ADAM_STEP)
    v_hat = v / (1.0 - ADAM_B2 ** ADAM_STEP)
    delta = -ADAM_LR * (m_hat / (jnp.sqrt(v_hat) + ADAM_EPS) + ADAM_WD * w)
    return delta, m, v


def _adamw(c_idx, w, g_mine, g_other, m, v, name):
    k, n = w.shape
    tm = k // 4

    def body(c_ref, w_ref, gm_ref, go_ref, m_ref, v_ref, g_ref, d_ref, mo_ref, vo_ref):
        g = jnp.where(pl.program_id(0) == c_ref[0], gm_ref[...], go_ref[...])
        d, mm, vv = _adamw_math(w_ref[...], g, m_ref[...], v_ref[...])
        g_ref[...] = g
        d_ref[...] = d
        mo_ref[...] = mm
        vo_ref[...] = vv

    full = pl.BlockSpec((tm, n), lambda h, i, c_ref: (2 * h + i, 0))
    mine = pl.BlockSpec((tm, n), lambda h, i, c_ref: (jnp.where(h == c_ref[0], i, 0), 0))
    other = pl.BlockSpec((tm, n), lambda h, i, c_ref: (jnp.where(h == c_ref[0], 0, i), 0))
    shp = jax.ShapeDtypeStruct((k, n), F32)
    return pl.pallas_call(
        body, name=name,
        grid_spec=pltpu.PrefetchScalarGridSpec(
            num_scalar_prefetch=1, grid=(2, 2),
            in_specs=[full, mine, other, full, full], out_specs=[full] * 4),
        out_shape=[shp] * 4, compiler_params=_params("arbitrary", "arbitrary"),
    )(c_idx, w, g_mine, g_other, m, v)


VEC_SLOTS = {
    "g_mix_norm": (0, 0, D), "b_conv_proj": (0, D, D), "g_ffn_norm": (0, 2 * D, D),
    "g_final": (0, 3 * D, D), "b_in": (1, 0, INW), "conv_b": (2, 0, C), "ln_g": (2, C, C),
    "ln_b": (2, 2 * C, C), "sinks": (2, 3 * C, NQ), "loss": (2, 3 * C + 128, 1),
}
VEC_ROWS, VEC_COLS = 8, 4 * D
CW_ROWS = 32
SMALL_NAMES = ["g_mix_norm", "b_in", "sinks", "conv_w", "conv_b", "ln_g", "ln_b",
               "b_conv_proj", "g_ffn_norm", "g_final"]
CW_LANES = C // N_CHIPS


def _pack_small(gs, loss):
    row0 = jnp.concatenate([gs["g_mix_norm"], gs["b_conv_proj"], gs["g_ffn_norm"], gs["g_final"]], axis=1)
    row1 = jnp.pad(gs["b_in"], ((0, 0), (0, VEC_COLS - INW)))
    row2 = jnp.concatenate([gs["conv_b"], gs["ln_g"], gs["ln_b"],
                            jnp.pad(gs["sinks"], ((0, 0), (0, 128 - NQ))),
                            jnp.pad(loss.reshape(1, 1), ((0, 0), (0, VEC_COLS - 3 * C - 129)))], axis=1)
    vec = jnp.concatenate([row0, row1, row2, jnp.zeros((VEC_ROWS - 3, VEC_COLS), F32)], axis=0)
    cw = jnp.pad(gs["conv_w"], ((0, CW_ROWS - KW), (0, 0)))
    return vec, cw


def _small_update(s_idx, vec_all, cw_all, wmv):
    nsm = len(SMALL_NAMES)

    def body(s_ref, vec_ref, cw_ref, *refs):
        ins = refs[:3 * nsm]
        outs = refs[3 * nsm:7 * nsm]
        loss_ref = refs[7 * nsm]

        def total(slot):
            row, lane, width = slot
            acc = vec_ref[row:row + 1, lane:lane + width]
            for k in range(1, 8):
                acc = acc + vec_ref[k * VEC_ROWS + row:k * VEC_ROWS + row + 1, lane:lane + width]
            return acc

        loss_ref[...] = jnp.broadcast_to(total(VEC_SLOTS["loss"]), loss_ref.shape)
        for p, name in enumerate(SMALL_NAMES):
            w_ref, m_ref, v_ref = ins[3 * p:3 * p + 3]
            g_ref, d_ref, mo_ref, vo_ref = outs[4 * p:4 * p + 4]
            if name == "conv_w":
                g = jnp.zeros((KW, CW_LANES), F32)
                for s in range(N_CHIPS):
                    cand = cw_ref[0:KW, s * CW_LANES:(s + 1) * CW_LANES]
                    for k in range(1, 8):
                        cand = cand + cw_ref[k * CW_ROWS:k * CW_ROWS + KW, s * CW_LANES:(s + 1) * CW_LANES]
                    g = jnp.where(s_ref[0] == s, cand, g)
            else:
                g = total(VEC_SLOTS[name])
            d, mm, vv = _adamw_math(w_ref[...], g, m_ref[...], v_ref[...])
            g_ref[...] = g
            d_ref[...] = d
            mo_ref[...] = mm
            vo_ref[...] = vv

    vmem = pl.BlockSpec(memory_space=pltpu.VMEM)
    flat = [a for t in wmv for a in t]
    out_shape = []
    for w, _, _ in wmv:
        out_shape += [jax.ShapeDtypeStruct(w.shape, F32)] * 4
    out_shape.append(jax.ShapeDtypeStruct((1, 128), F32))
    res = pl.pallas_call(
        body, name="small_update",
        in_specs=[pl.BlockSpec(memory_space=pltpu.SMEM)] + [vmem] * (2 + len(flat)),
        out_specs=[vmem] * len(out_shape), out_shape=out_shape,
    )(s_idx, vec_all, cw_all, *flat)
    return [tuple(res[4 * p:4 * p + 4]) for p in range(nsm)], res[4 * nsm]


WEIGHT_ORDER = ["g_mix_norm", "w_in", "b_in", "sinks", "conv_w", "conv_b", "ln_g", "ln_b",
                "w_attn_proj", "w_conv_proj", "b_conv_proj", "w_out", "g_ffn_norm", "w_ffn_in",
                "w_ffn_down", "g_final"]


def kernel(x, g_mix_norm, w_in, b_in, sinks, conv_w, conv_b, ln_g, ln_b, w_attn_proj, w_conv_proj, b_conv_proj, w_out, g_ffn_norm, w_ffn_in, w_ffn_down, g_final, loss_target, m_g_mix_norm, m_w_in, m_b_in, m_sinks, m_conv_w, m_conv_b, m_ln_g, m_ln_b, m_w_attn_proj, m_w_conv_proj, m_b_conv_proj, m_w_out, m_g_ffn_norm, m_w_ffn_in, m_w_ffn_down, m_g_final, v_g_mix_norm, v_w_in, v_b_in, v_sinks, v_conv_w, v_conv_b, v_ln_g, v_ln_b, v_w_attn_proj, v_w_conv_proj, v_b_conv_proj, v_w_out, v_g_ffn_norm, v_w_ffn_in, v_w_ffn_down, v_g_final):
    w = dict(g_mix_norm=g_mix_norm, w_in=w_in, b_in=b_in, sinks=sinks, conv_w=conv_w, conv_b=conv_b,
             ln_g=ln_g, ln_b=ln_b, w_attn_proj=w_attn_proj, w_conv_proj=w_conv_proj,
             b_conv_proj=b_conv_proj, w_out=w_out, g_ffn_norm=g_ffn_norm, w_ffn_in=w_ffn_in,
             w_ffn_down=w_ffn_down, g_final=g_final)
    m = dict(g_mix_norm=m_g_mix_norm, w_in=m_w_in, b_in=m_b_in, sinks=m_sinks, conv_w=m_conv_w,
             conv_b=m_conv_b, ln_g=m_ln_g, ln_b=m_ln_b, w_attn_proj=m_w_attn_proj,
             w_conv_proj=m_w_conv_proj, b_conv_proj=m_b_conv_proj, w_out=m_w_out,
             g_ffn_norm=m_g_ffn_norm, w_ffn_in=m_w_ffn_in, w_ffn_down=m_w_ffn_down, g_final=m_g_final)
    v = dict(g_mix_norm=v_g_mix_norm, w_in=v_w_in, b_in=v_b_in, sinks=v_sinks, conv_w=v_conv_w,
             conv_b=v_conv_b, ln_g=v_ln_g, ln_b=v_ln_b, w_attn_proj=v_w_attn_proj,
             w_conv_proj=v_w_conv_proj, b_conv_proj=v_b_conv_proj, w_out=v_w_out,
             g_ffn_norm=v_g_ffn_norm, w_ffn_in=v_w_ffn_in, w_ffn_down=v_w_ffn_down, g_final=v_g_final)

    c_idx = lax.axis_index("c").astype(jnp.int32).reshape(1)
    s_idx = (2 * lax.axis_index("x") + lax.axis_index("y")).astype(jnp.int32).reshape(1)

    out_g, out_d, out_m, out_v = {}, {}, {}, {}

    def gather_start(tag, shards):
        lands = [lax.empty((N_CHIPS,) + s.shape, s.dtype) for s in shards]
        return _chip_start("gather_start_" + tag, _gather_ends, GATHER_PEERS, shards, lands)

    def gather_relay(tag, state, after, first=0, count=None):
        send_sems, recv_sems, shards, lands, _ = state
        last = len(shards) if count is None else first + count
        return _gather_relay("gather_relay_" + tag, send_sems, recv_sems, shards[first:last],
                             lands[first:last], after, first)

    def gather_finish(tag, relay, after):
        return _forward_wait("forward_wait_" + tag, relay[0], relay[1], relay[2], after)

    names_b = ["w_attn_proj", "w_conv_proj", "w_out", "w_ffn_in", "w_ffn_down"]
    big = {name: (w[name][0], m[name][0], v[name][0]) for name in names_b}
    big["w_in"] = (w_in[0].T, m_w_in[0].T, v_w_in[0].T)
    state_a = gather_start("a", [big["w_in"][0].astype(BF16), jnp.pad(conv_w[0], ((0, CW_ROWS - KW), (0, 0)))])
    state_b = gather_start("b", [(big[name][0] + state_a[4][0, 0]).astype(BF16) for name in names_b])
    got_a = gather_finish("a", gather_relay("a", state_a, [state_b[4]]), [])
    w_in_t_full = got_a[0].reshape(INW, D)
    conv_w_full = got_a[1].transpose(1, 0, 2).reshape(CW_ROWS, C)[:KW]

    xs, target = x[0], loss_target[0]
    g_final2 = g_final.reshape(1, D)
    h, qkv, glu, gl = _in_proj(xs, g_mix_norm, w_in_t_full, b_in)
    o, lse = _attn_fwd(qkv, sinks)
    relay_1 = gather_relay("b1", state_b, [o], 0, 3)
    u, cact = _conv_fwd(glu, conv_w_full, conv_b, ln_g, ln_b, relay_1[3])
    w_ap4, w_cp4, w_out4 = gather_finish("b1", relay_1, [cact])
    w_out_full = w_out4.reshape(D, D)
    relay_2 = gather_relay("b2", state_b, [cact], 3, 1)
    mg, x1 = _mix_out(xs, o, cact, gl, w_ap4, w_cp4, b_conv_proj, w_out_full, relay_2[3])
    w_fi4, = gather_finish("b2", relay_2, [x1])
    relay_3 = gather_relay("b3", state_b, [x1], 4, 1)
    h2, gu, act = _ffn_in(x1, g_ffn_norm, w_fi4, relay_3[3])
    w_dn4, = gather_finish("b3", relay_3, [act])
    w_dn_full = w_dn4.reshape(DFF, D)
    dx2, dx2b, dg_final, loss_part = _ffn_out_loss(x1, act, w_dn_full, g_final2, target)

    def exchange_start(tag, grads):
        lands = [lax.empty((N_CHIPS, g.shape[1] // 2, g.shape[2]), g.dtype) for g in grads]
        return _chip_start("pair_start_" + tag, _exchange_ends, PAIR_PEERS, grads, lands)

    def reduce_start(tag, names, exchange, after):
        send_sems, recv_sems, grads, lands, _ = exchange
        grads, from_sibling = _chip_wait("pair_wait_" + tag, _exchange_ends, send_sems, recv_sems, grads, lands, after)
        pair = [_pair_sum(c_idx, g, r, "pair_sum_" + name) for name, g, r in zip(names, grads, from_sibling)]
        lands = [lax.empty((3,) + p.shape[1:], p.dtype) for p in pair]
        return _chip_start("chip_start_" + tag, _reduce_ends, REDUCE_PEERS, pair, lands)

    def reduce_sum(tag, names, state, after):
        send_sems, recv_sems, pair, lands, _ = state
        pair, lands = _chip_wait("chip_wait_" + tag, _reduce_ends, send_sems, recv_sems, pair, lands, after)
        mine = [_chip_sum(s_idx, p, r, "chip_sum_" + name) for name, p, r in zip(names, pair, lands)]
        others = [lax.empty(a.shape, a.dtype) for a in mine]
        return _chip_start("share_start_" + tag, _share_ends, PAIR_PEERS, mine, others)

    def reduce_finish(tag, names, share, after):
        send_sems, recv_sems, mine, others, _ = share
        mine, others = _chip_wait("share_wait_" + tag, _share_ends, send_sems, recv_sems, mine, others, after)
        for name, g_mine, g_other in zip(names, mine, others):
            wv, mv, vv = big[name]
            res = _adamw(c_idx, wv, g_mine, g_other, mv, vv, "adamw_" + name)
            if name == "w_in":
                res = [a.T for a in res]
            out_g[name], out_d[name], out_m[name], out_v[name] = [a[None] for a in res]

    dgu, dx1, dx1b, dg_ffn = _ffn_bwd(dx2, dx2b, gu, x1, g_ffn_norm, w_dn_full, w_fi4)
    names_1 = ["w_ffn_in", "w_ffn_down", "w_out", "w_attn_proj", "w_conv_proj"]
    grads_1 = [_grad_w(h2, dgu, "grad_w_ffn_in", 512, FSH, True),
               _grad_w(act, dx2b, "grad_w_ffn_down", 256, D, False)]
    dya, dyc, dgl, do, dc, db_cp = _mix_bwd(dx1b, gl, o, cact, b_conv_proj, w_out_full, w_ap4, w_cp4,
                                            relay_3[3])
    grads_1 += [_grad_w(mg, dx1b, "grad_w_out", 512, D, False),
                _grad_w(o, dya, "grad_w_attn_proj", 512, D, True),
                _grad_w(cact, dyc, "grad_w_conv_proj", 512, D, True)]
    exchange_1 = exchange_start("1", grads_1)
    dglu, dconv_w, dconv_b, dln_g, dln_b = _conv_bwd(glu, u, dc, conv_w_full, ln_g, ln_b, exchange_1[4])
    state_1 = reduce_start("1", names_1, exchange_1, [dglu])
    dq, dkv, dsinks = _attn_bwd(qkv, o, do, lse, sinks, state_1[4])
    names_2 = ["w_in"]
    gw_in_t = _grad_w_in_t(h, dq, dkv, dglu, dgl)
    exchange_2 = exchange_start("2", [gw_in_t.reshape(N_CHIPS, INW // N_CHIPS, D)])
    grad_x, dg_mix, db_in = _in_proj_bwd(dq, dkv, dglu, dgl, xs, dx1, g_mix_norm, w_in_t_full, exchange_2[4])

    gs = {"g_mix_norm": dg_mix, "b_in": db_in, "sinks": dsinks[:, 0].reshape(1, NQ),
          "conv_w": dconv_w, "conv_b": dconv_b, "ln_g": dln_g, "ln_b": dln_b,
          "b_conv_proj": db_cp, "g_ffn_norm": dg_ffn, "g_final": dg_final}
    blocks = list(_pack_small(gs, loss_part[0, 0]))
    tables = [lax.empty((8 * b.shape[0], b.shape[1]), b.dtype) for b in blocks]
    small = _chip_start("small_start", _small_ends, SMALL_PEERS, blocks, tables)

    state_2 = reduce_start("2", names_2, exchange_2, [grad_x, small[4]])
    share_1 = reduce_sum("1", names_1, state_1, [state_2[4]])
    blocks, tables = _chip_wait("small_wait", _small_ends, small[0], small[1], small[2], small[3],
                                [share_1[4]])
    me = 4 * lax.axis_index("x") + 2 * lax.axis_index("y") + lax.axis_index("c")
    vec_all, cw_all = [lax.dynamic_update_slice(t, b, (me * b.shape[0], 0)) for t, b in zip(tables, blocks)]

    def view(a, name):
        if name == "conv_w":
            return a[0]
        if name == "g_final":
            return a.reshape(1, D)
        return a

    wmv = [(view(w[name], name), view(m[name], name), view(v[name], name)) for name in SMALL_NAMES]
    small_out, loss_row = _small_update(s_idx, vec_all, cw_all, wmv)
    for name, (g, d, mm, vv) in zip(SMALL_NAMES, small_out):
        shape = w[name].shape
        out_g[name], out_d[name], out_m[name], out_v[name] = (
            g.reshape(shape), d.reshape(shape), mm.reshape(shape), vv.reshape(shape))

    reduce_finish("1", names_1, share_1, [loss_row])
    share_2 = reduce_sum("2", names_2, state_2, [out_d["w_conv_proj"]])
    reduce_finish("2", names_2, share_2, [])

    loss = loss_row[0, 0]
    return (loss, grad_x[None], *[out_g[k] for k in WEIGHT_ORDER], *[out_d[k] for k in WEIGHT_ORDER],
            *[out_m[k] for k in WEIGHT_ORDER], *[out_v[k] for k in WEIGHT_ORDER])
```

```python
import functools

import jax
import jax.numpy as jnp
from jax import lax
from jax.experimental import pallas as pl
from jax.experimental.pallas import tpu as pltpu

F32 = jnp.float32
BF16 = jnp.bfloat16

T = 2048
D = 1024
HD = 64
NQ = 8
NKV = 2
GROUP = NQ // NKV
BLK = 128
AW = NQ * HD
KVW = NKV * HD
C = 512
KW = 31
QKVW = AW + 2 * KVW
GLU_OFF = QKVW
GATE_OFF = GLU_OFF + 2 * C
INW = GATE_OFF + 2 * D
DFF = 2816
EPS = 1e-5
NEG = -1e30
SCALE = HD ** -0.5
HALO = 32
N_CHIPS = 4
FSH = 2 * DFF // N_CHIPS

ADAM_LR = 0.001
ADAM_B1 = 0.9
ADAM_B2 = 0.999
ADAM_EPS = 1e-08
ADAM_WD = 0.01
ADAM_STEP = 10

VMEM_LIMIT = 56 * 1024 * 1024
ROW_TM = 512
MESH = pl.DeviceIdType.MESH


def _params(*sem):
    return pltpu.CompilerParams(dimension_semantics=sem, vmem_limit_bytes=VMEM_LIMIT)


def _dot(a, b):
    return jnp.dot(a, b, preferred_element_type=F32)


def _dot_nt(a, b):
    return lax.dot_general(a, b, (((1,), (1,)), ((), ())), preferred_element_type=F32)


def _dot_tn(a, b):
    return lax.dot_general(a, b, (((0,), (0,)), ((), ())), preferred_element_type=F32)


def _sigmoid(v):
    return 1.0 / (1.0 + jnp.exp(-v))


def _rows(tm, n):
    return pl.BlockSpec((tm, n), lambda i: (i, 0))


def _whole(shape):
    return pl.BlockSpec(shape, lambda i: tuple(0 for _ in shape))


def _in_proj(x, g_mix, w_in_t, b_in):
    tm = ROW_TM

    def body(x_ref, g_ref, w_ref, b_ref, h_ref, qkv_ref, glu_ref, gl_ref):
        xv = x_ref[...]
        r = lax.rsqrt(jnp.mean(xv * xv, axis=-1, keepdims=True) + EPS)
        h = (xv * r * g_ref[...]).astype(BF16)
        h_ref[...] = h
        qkv_ref[...] = (_dot_nt(h, w_ref[0:GLU_OFF, :]) + b_ref[:, 0:GLU_OFF]).astype(BF16)
        glu_ref[...] = (_dot_nt(h, w_ref[GLU_OFF:GATE_OFF, :]) + b_ref[:, GLU_OFF:GATE_OFF]).astype(BF16)
        gl_ref[...] = (_dot_nt(h, w_ref[GATE_OFF:INW, :]) + b_ref[:, GATE_OFF:INW]).astype(BF16)

    return pl.pallas_call(
        body, name="in_proj", grid=(T // tm,),
        in_specs=[_rows(tm, D), _whole((1, D)), _whole((INW, D)), _whole((1, INW))],
        out_specs=[_rows(tm, D), _rows(tm, QKVW), _rows(tm, 2 * C), _rows(tm, 2 * D)],
        out_shape=[jax.ShapeDtypeStruct((T, D), BF16), jax.ShapeDtypeStruct((T, QKVW), BF16),
                   jax.ShapeDtypeStruct((T, 2 * C), BF16), jax.ShapeDtypeStruct((T, 2 * D), BF16)],
        compiler_params=_params("parallel"),
    )(x, g_mix, w_in_t, b_in)


GROWS = GROUP * BLK
BAND = 2 * BLK
ATT_SUB = 2


def _band(i):
    rb = pl.multiple_of(jnp.maximum(i - 1, 0) * BLK, BLK)
    row = lax.broadcasted_iota(jnp.int32, (GROWS, BAND), 0)
    kpos = rb + lax.broadcasted_iota(jnp.int32, (GROWS, BAND), 1)
    qpos = i * BLK + jnp.bitwise_and(row, BLK - 1)
    return rb, jnp.logical_and(kpos <= qpos, kpos > qpos - BLK)


def _sink_column(sink_ref, g):
    head = lax.shift_right_logical(lax.broadcasted_iota(jnp.int32, (GROWS, 1), 0), 7)
    col = jnp.full((GROWS, 1), sink_ref[0, g * GROUP], F32)
    for hh in range(1, GROUP):
        col = jnp.where(head == hh, sink_ref[0, g * GROUP + hh], col)
    return col


def _attn_fwd(qkv, sinks):
    def body(sink_ref, qkv_ref, o_ref, lse_ref, s_ref, p_ref):
        slots = [(sub, g) for sub in range(ATT_SUB) for g in range(NKV)]
        bands = [_band(pl.program_id(0) * ATT_SUB + sub) for sub in range(ATT_SUB)]
        for n, (sub, g) in enumerate(slots):
            rb = bands[sub][0]
            r0 = pl.multiple_of((pl.program_id(0) * ATT_SUB + sub) * BLK, BLK)
            kband = qkv_ref[pl.ds(rb, BAND), AW + g * HD:AW + (g + 1) * HD]
            for hh in range(GROUP):
                h = g * GROUP + hh
                s_ref[n, hh * BLK:(hh + 1) * BLK, :] = _dot_nt(
                    qkv_ref[pl.ds(r0, BLK), h * HD:(h + 1) * HD], kband)
        lses = []
        for n, (sub, g) in enumerate(slots):
            s = jnp.where(bands[sub][1], s_ref[n] * SCALE, NEG)
            sink = _sink_column(sink_ref, g)
            m = jnp.maximum(jnp.max(s, axis=-1, keepdims=True), sink)
            p = jnp.exp(s - m)
            den = jnp.sum(p, axis=-1, keepdims=True) + jnp.exp(sink - m)
            p_ref[n] = (p * (1.0 / den)).astype(BF16)
            lses.append(m + jnp.log(den))
        for n, (sub, g) in enumerate(slots):
            rb = bands[sub][0]
            rows = slice(sub * BLK, (sub + 1) * BLK)
            vband = qkv_ref[pl.ds(rb, BAND), AW + KVW + g * HD:AW + KVW + (g + 1) * HD]
            for hh in range(GROUP):
                h = g * GROUP + hh
                o_ref[rows, h * HD:(h + 1) * HD] = _dot(p_ref[n, hh * BLK:(hh + 1) * BLK, :], vband).astype(BF16)
                lse_ref[rows, h:h + 1] = lses[n][hh * BLK:(hh + 1) * BLK]

    nslot = ATT_SUB * NKV
    return pl.pallas_call(
        body, name="attn_fwd", grid=(T // (ATT_SUB * BLK),),
        in_specs=[pl.BlockSpec(memory_space=pltpu.SMEM), _whole((T, QKVW))],
        out_specs=[_rows(ATT_SUB * BLK, AW), _rows(ATT_SUB * BLK, NQ)],
        out_shape=[jax.ShapeDtypeStruct((T, AW), BF16), jax.ShapeDtypeStruct((T, NQ), F32)],
        scratch_shapes=[pltpu.VMEM((nslot, GROWS, BAND), F32), pltpu.VMEM((nslot, GROWS, BAND), BF16)],
        compiler_params=_params("parallel"),
    )(sinks, qkv)


CONV_TM = 256
CONV_SUB = 32


def _glu(ab):
    a = ab[:, 0:C].astype(F32)
    b = ab[:, C:2 * C].astype(F32)
    return a * _sigmoid(b)


SUBLANES = 8


def _shifted_copies(ref):
    rows = ref.shape[1] - SUBLANES
    for r in range(1, SUBLANES):
        ref[r, 0:rows, :] = ref[0, r:r + rows, :]


def _shifted_rows(ref, start, size):
    r = start % SUBLANES
    return ref[r, start - r:start - r + size, :]


def _conv_fwd(glu, conv_w, conv_b, ln_g, ln_b, dep):
    tm = CONV_TM

    def body(cur_ref, prev_ref, w_ref, cb_ref, g_ref, b_ref, dep_ref, u_ref, c_ref, zs_ref):
        i = pl.program_id(0)
        zprev = _glu(prev_ref[tm - HALO:tm, :])
        zs_ref[0, 0:HALO, :] = jnp.where(i > 0, zprev, 0.0)
        zs_ref[0, HALO:HALO + tm, :] = _glu(cur_ref[...])
        _shifted_copies(zs_ref)
        for s in range(tm // CONV_SUB):
            base = HALO + s * CONV_SUB - (KW - 1)
            acc = jnp.broadcast_to(cb_ref[...], (CONV_SUB, C))
            for j in range(KW):
                acc = acc + w_ref[j:j + 1, :] * _shifted_rows(zs_ref, base + j, CONV_SUB)
            rows = slice(s * CONV_SUB, (s + 1) * CONV_SUB)
            u_ref[rows, :] = acc
            mu = jnp.mean(acc, axis=-1, keepdims=True)
            xc = acc - mu
            var = jnp.mean(xc * xc, axis=-1, keepdims=True)
            y = xc * lax.rsqrt(var + EPS) * g_ref[...] + b_ref[...]
            c_ref[rows, :] = (y * _sigmoid(y)).astype(BF16)

    return pl.pallas_call(
        body, name="conv_fwd", grid=(T // tm,),
        in_specs=[_rows(tm, 2 * C),
                  pl.BlockSpec((tm, 2 * C), lambda i: (jnp.maximum(i - 1, 0), 0)),
                  _whole((KW, C)), _whole((1, C)), _whole((1, C)), _whole((1, C)), _whole((8, 128))],
        out_specs=[_rows(tm, C), _rows(tm, C)],
        out_shape=[jax.ShapeDtypeStruct((T, C), F32), jax.ShapeDtypeStruct((T, C), BF16)],
        scratch_shapes=[pltpu.VMEM((SUBLANES, HALO + tm, C), F32)],
        compiler_params=_params("parallel"),
    )(glu, glu, conv_w, conv_b, ln_g, ln_b, dep)


def _branch_outputs(o, cact, wap_ref, wcp_ref, bcp_ref):
    ya = jnp.concatenate([_dot(o, wap_ref[s]) for s in range(N_CHIPS)], axis=1)
    yc = jnp.concatenate([_dot(cact, wcp_ref[s]) for s in range(N_CHIPS)], axis=1) + bcp_ref[...]
    return ya, yc


def _mix_out(x, o, cact, gl, w_ap, w_cp, b_cp, w_out, dep):
    tm = ROW_TM

    def body(x_ref, o_ref, c_ref, gl_ref, wap_ref, wcp_ref, bcp_ref, wo_ref, dep_ref, mg_ref, x1_ref):
        ya, yc = _branch_outputs(o_ref[...], c_ref[...], wap_ref, wcp_ref, bcp_ref)
        g0 = _sigmoid(gl_ref[:, 0:D].astype(F32))
        g1 = _sigmoid(gl_ref[:, D:2 * D].astype(F32))
        mg = (g0 * ya + g1 * yc).astype(BF16)
        mg_ref[...] = mg
        x1_ref[...] = x_ref[...] + _dot(mg, wo_ref[...])

    return pl.pallas_call(
        body, name="mix_out", grid=(T // tm,),
        in_specs=[_rows(tm, D), _rows(tm, AW), _rows(tm, C), _rows(tm, 2 * D),
                  _whole((N_CHIPS, AW, D // N_CHIPS)), _whole((N_CHIPS, C, D // N_CHIPS)), _whole((1, D)),
                  _whole((D, D)), _whole((8, 128))],
        out_specs=[_rows(tm, D), _rows(tm, D)],
        out_shape=[jax.ShapeDtypeStruct((T, D), BF16), jax.ShapeDtypeStruct((T, D), F32)],
        compiler_params=_params("parallel"),
    )(x, o, cact, gl, w_ap, w_cp, b_cp, w_out, dep)


def _ffn_in(x1, g_ffn, w_fi, dep):
    tm = ROW_TM

    def body(x_ref, g_ref, w_ref, dep_ref, h_ref, gu_ref, act_ref):
        xv = x_ref[...]
        r = lax.rsqrt(jnp.mean(xv * xv, axis=-1, keepdims=True) + EPS)
        h = (xv * r * g_ref[...]).astype(BF16)
        h_ref[...] = h
        for s in range(N_CHIPS // 2):
            c0 = s * FSH
            gate = _dot(h, w_ref[s])
            up = _dot(h, w_ref[s + N_CHIPS // 2])
            gu_ref[:, c0:c0 + FSH] = gate.astype(BF16)
            gu_ref[:, DFF + c0:DFF + c0 + FSH] = up.astype(BF16)
            act_ref[:, c0:c0 + FSH] = (gate * _sigmoid(gate) * up).astype(BF16)

    return pl.pallas_call(
        body, name="ffn_in", grid=(T // tm,),
        in_specs=[_rows(tm, D), _whole((1, D)), _const((N_CHIPS, D, FSH)), _whole((8, 128))],
        out_specs=[_rows(tm, D), _rows(tm, 2 * DFF), _rows(tm, DFF)],
        out_shape=[jax.ShapeDtypeStruct((T, D), BF16), jax.ShapeDtypeStruct((T, 2 * DFF), BF16),
                   jax.ShapeDtypeStruct((T, DFF), BF16)],
        compiler_params=_params("parallel"),
    )(x1, g_ffn, w_fi, dep)


def _ffn_out_loss(x1, act, w_dn, g_final, target):
    tm = ROW_TM

    def body(x_ref, a_ref, w_ref, g_ref, t_ref, dx_ref, dxb_ref, dg_ref, loss_ref):
        i = pl.program_id(0)
        x2 = x_ref[...] + _dot(a_ref[...], w_ref[...])
        r = lax.rsqrt(jnp.mean(x2 * x2, axis=-1, keepdims=True) + EPS)
        xh = x2 * r
        g = g_ref[...]
        err = xh * g - t_ref[...]
        dy = err * (1.0 / D)
        dyg = dy * g
        dx = r * (dyg - xh * jnp.mean(dyg * xh, axis=-1, keepdims=True))
        dx_ref[...] = dx
        dxb_ref[...] = dx.astype(BF16)
        part = 0.5 * jnp.sum(jnp.mean(err * err, axis=-1, keepdims=True), axis=0, keepdims=True)

        @pl.when(i == 0)
        def _():
            dg_ref[...] = jnp.zeros_like(dg_ref)
            loss_ref[...] = jnp.zeros_like(loss_ref)

        dg_ref[...] += jnp.sum(dy * xh, axis=0, keepdims=True)
        loss_ref[...] += jnp.broadcast_to(part, loss_ref.shape)

    return pl.pallas_call(
        body, name="ffn_out_loss", grid=(T // tm,),
        in_specs=[_rows(tm, D), _rows(tm, DFF), _whole((DFF, D)), _whole((1, D)), _rows(tm, D)],
        out_specs=[_rows(tm, D), _rows(tm, D), _whole((1, D)), _whole((1, 128))],
        out_shape=[jax.ShapeDtypeStruct((T, D), F32), jax.ShapeDtypeStruct((T, D), BF16),
                   jax.ShapeDtypeStruct((1, D), F32), jax.ShapeDtypeStruct((1, 128), F32)],
        compiler_params=_params("arbitrary"),
    )(x1, act, w_dn, g_final, target)


def _const(shape):
    return pl.BlockSpec(shape, lambda i: tuple(0 for _ in shape), pipeline_mode=pl.Buffered(1))


def _ffn_bwd(dx2, dx2b, gu, x1, g_ffn, w_dn_t, w_fi_t):
    tm = ROW_TM // 2

    def body(dx_ref, dxb_ref, gu_ref, x_ref, g_ref, wdn_ref, wfi_ref,
             dgu_ref, dx1_ref, dx1b_ref, dg_ref):
        i = pl.program_id(0)
        dxb = dxb_ref[...]
        dh = jnp.zeros((tm, D), F32)
        for k in range(N_CHIPS // 2):
            c0 = k * FSH
            dact = _dot_nt(dxb, wdn_ref[c0:c0 + FSH, :])
            gate = gu_ref[:, c0:c0 + FSH].astype(F32)
            up = gu_ref[:, DFF + c0:DFF + c0 + FSH].astype(F32)
            s = _sigmoid(gate)
            dup = (dact * gate * s).astype(BF16)
            dgate = (dact * up * s * (1.0 + gate * (1.0 - s))).astype(BF16)
            dgu_ref[:, c0:c0 + FSH] = dgate
            dgu_ref[:, DFF + c0:DFF + c0 + FSH] = dup
            dh = dh + _dot_nt(dgate, wfi_ref[k]) + _dot_nt(dup, wfi_ref[k + N_CHIPS // 2])
        xv = x_ref[...]
        r = lax.rsqrt(jnp.mean(xv * xv, axis=-1, keepdims=True) + EPS)
        xh = xv * r
        dhg = dh * g_ref[...]
        dx1 = dx_ref[...] + r * (dhg - xh * jnp.mean(dhg * xh, axis=-1, keepdims=True))
        dx1_ref[...] = dx1
        dx1b_ref[...] = dx1.astype(BF16)

        @pl.when(i == 0)
        def _():
            dg_ref[...] = jnp.zeros_like(dg_ref)

        dg_ref[...] += jnp.sum(dh * xh, axis=0, keepdims=True)

    return pl.pallas_call(
        body, name="ffn_bwd", grid=(T // tm,),
        in_specs=[_rows(tm, D), _rows(tm, D), _rows(tm, 2 * DFF), _rows(tm, D), _whole((1, D)),
                  _const((DFF, D)), _const((N_CHIPS, D, FSH))],
        out_specs=[_rows(tm, 2 * DFF), _rows(tm, D), _rows(tm, D), _whole((1, D))],
        out_shape=[jax.ShapeDtypeStruct((T, 2 * DFF), BF16), jax.ShapeDtypeStruct((T, D), F32),
                   jax.ShapeDtypeStruct((T, D), BF16), jax.ShapeDtypeStruct((1, D), F32)],
        compiler_params=_params("arbitrary"),
    )(dx2, dx2b, gu, x1, g_ffn, w_dn_t, w_fi_t)


def _mix_bwd(dx1b, gl, o, cact, b_cp, w_out, w_ap, w_cp, dep):
    tm = ROW_TM

    def body(dx_ref, gl_ref, o_ref, c_ref, bcp_ref, wo_ref, wap_ref, wcp_ref, dep_ref,
             dya_ref, dyc_ref, dgl_ref, do_ref, dc_ref, db_ref):
        i = pl.program_id(0)
        dm = _dot_nt(dx_ref[...], wo_ref[...])
        ya, yc = _branch_outputs(o_ref[...], c_ref[...], wap_ref, wcp_ref, bcp_ref)
        g0 = _sigmoid(gl_ref[:, 0:D].astype(F32))
        g1 = _sigmoid(gl_ref[:, D:2 * D].astype(F32))
        dya = dm * g0
        dyc = dm * g1
        dgl_ref[:, 0:D] = (dya * ya * (1.0 - g0)).astype(BF16)
        dgl_ref[:, D:2 * D] = (dyc * yc * (1.0 - g1)).astype(BF16)
        dyab = dya.astype(BF16)
        dycb = dyc.astype(BF16)
        dya_ref[...] = dyab
        dyc_ref[...] = dycb
        sw = D // N_CHIPS
        do = jnp.zeros((tm, AW), F32)
        dcv = jnp.zeros((tm, C), F32)
        for s in range(N_CHIPS):
            do = do + _dot_nt(dyab[:, s * sw:(s + 1) * sw], wap_ref[s])
            dcv = dcv + _dot_nt(dycb[:, s * sw:(s + 1) * sw], wcp_ref[s])
        do_ref[...] = do.astype(BF16)
        dc_ref[...] = dcv.astype(BF16)

        @pl.when(i == 0)
        def _():
            db_ref[...] = jnp.zeros_like(db_ref)

        db_ref[...] += jnp.sum(dyc, axis=0, keepdims=True)

    return pl.pallas_call(
        body, name="mix_bwd", grid=(T // tm,),
        in_specs=[_rows(tm, D), _rows(tm, 2 * D), _rows(tm, AW), _rows(tm, C), _whole((1, D)),
                  _whole((D, D)), _whole((N_CHIPS, AW, D // N_CHIPS)), _whole((N_CHIPS, C, D // N_CHIPS)),
                  _whole((8, 128))],
        out_specs=[_rows(tm, D), _rows(tm, D), _rows(tm, 2 * D), _rows(tm, AW), _rows(tm, C),
                   _whole((1, D))],
        out_shape=[jax.ShapeDtypeStruct((T, D), BF16), jax.ShapeDtypeStruct((T, D), BF16),
                   jax.ShapeDtypeStruct((T, 2 * D), BF16), jax.ShapeDtypeStruct((T, AW), BF16),
                   jax.ShapeDtypeStruct((T, C), BF16), jax.ShapeDtypeStruct((1, D), F32)],
        compiler_params=_params("arbitrary"),
    )(dx1b, gl, o, cact, b_cp, w_out, w_ap, w_cp, dep)


def _conv_bwd(glu, u, dc, conv_w, ln_g, ln_b, dep):
    tm = CONV_TM
    nblk = T // tm

    def du_of(uv, dcv, g_ref, b_ref):
        mu = jnp.mean(uv, axis=-1, keepdims=True)
        xc = uv - mu
        var = jnp.mean(xc * xc, axis=-1, keepdims=True)
        rstd = lax.rsqrt(var + EPS)
        xh = xc * rstd
        y = xh * g_ref[...] + b_ref[...]
        sg = _sigmoid(y)
        dy = dcv * (sg * (1.0 + y * (1.0 - sg)))
        dxh = dy * g_ref[...]
        du = rstd * (dxh - jnp.mean(dxh, axis=-1, keepdims=True)
                     - xh * jnp.mean(dxh * xh, axis=-1, keepdims=True))
        return du, dy, xh

    def body(cur_ref, prev_ref, u_ref, un_ref, dc_ref, dcn_ref, w_ref, g_ref, b_ref, dep_ref,
             dglu_ref, dw_ref, dcb_ref, dg_ref, db_ref, zs_ref, dus_ref):
        i = pl.program_id(0)

        @pl.when(i == 0)
        def _():
            dw_ref[...] = jnp.zeros_like(dw_ref)
            dcb_ref[...] = jnp.zeros_like(dcb_ref)
            dg_ref[...] = jnp.zeros_like(dg_ref)
            db_ref[...] = jnp.zeros_like(db_ref)

        zprev = _glu(prev_ref[tm - HALO:tm, :])
        zs_ref[0, 0:HALO, :] = jnp.where(i > 0, zprev, 0.0)
        zs_ref[0, HALO:HALO + tm, :] = _glu(cur_ref[...])
        _shifted_copies(zs_ref)

        dun, _, _ = du_of(un_ref[0:HALO, :], dcn_ref[0:HALO, :].astype(F32), g_ref, b_ref)
        dus_ref[0, tm:tm + HALO, :] = jnp.where(i < nblk - 1, dun, 0.0)
        dg_acc = jnp.zeros((1, C), F32)
        db_acc = jnp.zeros((1, C), F32)
        dcb_acc = jnp.zeros((1, C), F32)
        for s in range(tm // CONV_SUB):
            rows = slice(s * CONV_SUB, (s + 1) * CONV_SUB)
            du, dy, xh = du_of(u_ref[rows, :], dc_ref[rows, :].astype(F32), g_ref, b_ref)
            dus_ref[0, rows, :] = du
            dg_acc = dg_acc + jnp.sum(dy * xh, axis=0, keepdims=True)
            db_acc = db_acc + jnp.sum(dy, axis=0, keepdims=True)
            dcb_acc = dcb_acc + jnp.sum(du, axis=0, keepdims=True)
        dg_ref[...] += dg_acc
        db_ref[...] += db_acc
        dcb_ref[...] += dcb_acc
        _shifted_copies(dus_ref)

        for j in range(KW):
            acc = jnp.zeros((CONV_SUB, C), F32)
            for s in range(tm // CONV_SUB):
                base = HALO + s * CONV_SUB - (KW - 1) + j
                acc = acc + dus_ref[0, s * CONV_SUB:(s + 1) * CONV_SUB, :] * _shifted_rows(zs_ref, base, CONV_SUB)
            dw_ref[j:j + 1, :] += jnp.sum(acc, axis=0, keepdims=True)

        for s in range(tm // CONV_SUB):
            rows = slice(s * CONV_SUB, (s + 1) * CONV_SUB)
            dz = jnp.zeros((CONV_SUB, C), F32)
            for j in range(KW):
                dz = dz + w_ref[j:j + 1, :] * _shifted_rows(dus_ref, s * CONV_SUB + (KW - 1) - j, CONV_SUB)
            a = cur_ref[rows, 0:C].astype(F32)
            sb = _sigmoid(cur_ref[rows, C:2 * C].astype(F32))
            dglu_ref[rows, 0:C] = (dz * sb).astype(BF16)
            dglu_ref[rows, C:2 * C] = (dz * a * sb * (1.0 - sb)).astype(BF16)

    nxt = lambda i: (jnp.minimum(i + 1, nblk - 1), 0)
    return pl.pallas_call(
        body, name="conv_bwd", grid=(nblk,),
        in_specs=[_rows(tm, 2 * C),
                  pl.BlockSpec((tm, 2 * C), lambda i: (jnp.maximum(i - 1, 0), 0)),
                  _rows(tm, C), pl.BlockSpec((tm, C), nxt),
                  _rows(tm, C), pl.BlockSpec((tm, C), nxt),
                  _whole((KW, C)), _whole((1, C)), _whole((1, C)), _whole((8, 128))],
        out_specs=[_rows(tm, 2 * C), _whole((KW, C)), _whole((1, C)), _whole((1, C)), _whole((1, C))],
        out_shape=[jax.ShapeDtypeStruct((T, 2 * C), BF16), jax.ShapeDtypeStruct((KW, C), F32),
                   jax.ShapeDtypeStruct((1, C), F32), jax.ShapeDtypeStruct((1, C), F32),
                   jax.ShapeDtypeStruct((1, C), F32)],
        scratch_shapes=[pltpu.VMEM((SUBLANES, HALO + tm, C), F32), pltpu.VMEM((SUBLANES, tm + HALO, C), F32)],
        compiler_params=_params("arbitrary"),
    )(glu, glu, u, u, dc, dc, conv_w, ln_g, ln_b, dep)


def _attn_bwd(qkv, o, do, lse, sinks, dep):
    nsub = 1

    def body(sink_ref, qkv_ref, o_ref, do_ref, lse_ref, dep_ref, dq_ref, dkv_ref, ds_ref,
             s_ref, dp_ref, p_ref, dsb_ref):
        i = pl.program_id(0)

        @pl.when(i == 0)
        def _():
            dkv_ref[...] = jnp.zeros_like(dkv_ref)
            ds_ref[...] = jnp.zeros_like(ds_ref)

        slots = [(sub, g) for sub in range(nsub) for g in range(NKV)]
        bands = [_band(i * nsub + sub) for sub in range(nsub)]
        r0s = [pl.multiple_of((i * nsub + sub) * BLK, BLK) for sub in range(nsub)]
        lses, dls = [], []
        for n, (sub, g) in enumerate(slots):
            rb, blk = bands[sub][0], slice(sub * BLK, (sub + 1) * BLK)
            kband = qkv_ref[pl.ds(rb, BAND), AW + g * HD:AW + (g + 1) * HD]
            vband = qkv_ref[pl.ds(rb, BAND), AW + KVW + g * HD:AW + KVW + (g + 1) * HD]
            lse_parts, dl_parts = [], []
            for hh in range(GROUP):
                h = g * GROUP + hh
                hcol = slice(h * HD, (h + 1) * HD)
                doh = do_ref[blk, hcol]
                s_ref[n, hh * BLK:(hh + 1) * BLK, :] = _dot_nt(qkv_ref[pl.ds(r0s[sub], BLK), hcol], kband)
                dp_ref[n, hh * BLK:(hh + 1) * BLK, :] = _dot_nt(doh, vband)
                lse_parts.append(lse_ref[blk, h:h + 1])
                dl_parts.append(jnp.sum(doh.astype(F32) * o_ref[blk, hcol].astype(F32), axis=-1, keepdims=True))
            lses.append(jnp.concatenate(lse_parts, axis=0))
            dls.append(jnp.concatenate(dl_parts, axis=0))
        dsinks = []
        for n, (sub, g) in enumerate(slots):
            p = jnp.where(bands[sub][1], jnp.exp(s_ref[n] * SCALE - lses[n]), 0.0)
            p_ref[n] = p.astype(BF16)
            dsb_ref[n] = (p * (dp_ref[n] - dls[n])).astype(BF16)
            dsinks.append(-(jnp.exp(_sink_column(sink_ref, g) - lses[n]) * dls[n]))
        for n, (sub, g) in enumerate(slots):
            rb, blk = bands[sub][0], slice(sub * BLK, (sub + 1) * BLK)
            kband = qkv_ref[pl.ds(rb, BAND), AW + g * HD:AW + (g + 1) * HD]
            dk = jnp.zeros((BAND, HD), F32)
            dv = jnp.zeros((BAND, HD), F32)
            for hh in range(GROUP):
                h = g * GROUP + hh
                hcol = slice(h * HD, (h + 1) * HD)
                rows = slice(hh * BLK, (hh + 1) * BLK)
                dq_ref[blk, hcol] = (_dot(dsb_ref[n, rows, :], kband) * SCALE).astype(BF16)
                dk = dk + _dot_tn(dsb_ref[n, rows, :], qkv_ref[pl.ds(r0s[sub], BLK), hcol])
                dv = dv + _dot_tn(p_ref[n, rows, :], do_ref[blk, hcol])
                ds_ref[h:h + 1, :] += jnp.broadcast_to(
                    jnp.sum(dsinks[n][rows], axis=0, keepdims=True), (1, 128))
            dkv_ref[pl.ds(rb, BAND), g * HD:(g + 1) * HD] += dk * SCALE
            dkv_ref[pl.ds(rb, BAND), KVW + g * HD:KVW + (g + 1) * HD] += dv

    nslot, tq = nsub * NKV, nsub * BLK
    return pl.pallas_call(
        body, name="attn_bwd", grid=(T // tq,),
        in_specs=[pl.BlockSpec(memory_space=pltpu.SMEM), _whole((T, QKVW)),
                  _rows(tq, AW), _rows(tq, AW), _rows(tq, NQ), _whole((8, 128))],
        out_specs=[_rows(tq, AW), _whole((T, 2 * KVW)), _whole((NQ, 128))],
        out_shape=[jax.ShapeDtypeStruct((T, AW), BF16), jax.ShapeDtypeStruct((T, 2 * KVW), F32),
                   jax.ShapeDtypeStruct((NQ, 128), F32)],
        scratch_shapes=[pltpu.VMEM((nslot, GROWS, BAND), F32), pltpu.VMEM((nslot, GROWS, BAND), F32),
                        pltpu.VMEM((nslot, GROWS, BAND), BF16), pltpu.VMEM((nslot, GROWS, BAND), BF16)],
        compiler_params=_params("arbitrary"),
    )(sinks, qkv, o, do, lse, dep)


PROJ_PARTS = [(0, AW), (AW, QKVW), (GLU_OFF, GATE_OFF), (GATE_OFF, INW)]


def _in_proj_bwd(dq, dkv, dglu, dgl, x, dx1, g_mix, w_in_t, dep):
    tm = ROW_TM

    def body(dq_ref, dkv_ref, dglu_ref, dgl_ref, x_ref, dx1_ref, g_ref, w_ref, dep_ref,
             gx_ref, dg_ref, db_ref):
        i = pl.program_id(0)

        @pl.when(i == 0)
        def _():
            dg_ref[...] = jnp.zeros_like(dg_ref)
            db_ref[...] = jnp.zeros_like(db_ref)

        dh = jnp.zeros((tm, D), F32)
        for part_ref, (lo, hi) in zip((dq_ref, dkv_ref, dglu_ref, dgl_ref), PROJ_PARTS):
            part = part_ref[...]
            dh = dh + _dot(part.astype(BF16), w_ref[lo:hi, :])
            db_ref[:, lo:hi] += jnp.sum(part.astype(F32), axis=0, keepdims=True)
        xv = x_ref[...]
        r = lax.rsqrt(jnp.mean(xv * xv, axis=-1, keepdims=True) + EPS)
        xh = xv * r
        dhg = dh * g_ref[...]
        gx_ref[...] = dx1_ref[...] + r * (dhg - xh * jnp.mean(dhg * xh, axis=-1, keepdims=True))
        dg_ref[...] += jnp.sum(dh * xh, axis=0, keepdims=True)

    return pl.pallas_call(
        body, name="in_proj_bwd", grid=(T // tm,),
        in_specs=[_rows(tm, AW), _rows(tm, 2 * KVW), _rows(tm, 2 * C), _rows(tm, 2 * D),
                  _rows(tm, D), _rows(tm, D), _whole((1, D)), _const((INW, D)), _whole((8, 128))],
        out_specs=[_rows(tm, D), _whole((1, D)), _whole((1, INW))],
        out_shape=[jax.ShapeDtypeStruct((T, D), F32), jax.ShapeDtypeStruct((1, D), F32),
                   jax.ShapeDtypeStruct((1, INW), F32)],
        compiler_params=_params("arbitrary"),
    )(dq, dkv, dglu, dgl, x, dx1, g_mix, w_in_t, dep)


def _grad_w_in_t(h, dq, dkv, dglu, dgl):
    tn, chunk = 512, 256

    def body(h_ref, dq_ref, dkv_ref, dglu_ref, dgl_ref, o_ref):
        hv = h_ref[...]
        for part_ref, (lo, hi) in zip((dq_ref, dkv_ref, dglu_ref, dgl_ref), PROJ_PARTS):
            for c0 in range(0, hi - lo, chunk):
                o_ref[lo + c0:lo + c0 + chunk, :] = _dot_tn(
                    part_ref[:, c0:c0 + chunk].astype(BF16), hv).astype(BF16)

    return pl.pallas_call(
        body, name="grad_w_in", grid=(D // tn,),
        in_specs=[pl.BlockSpec((T, tn), lambda j: (0, j)), _const((T, AW)), _const((T, 2 * KVW)),
                  _const((T, 2 * C)), _const((T, 2 * D))],
        out_specs=pl.BlockSpec((INW, tn), lambda j: (0, j)),
        out_shape=jax.ShapeDtypeStruct((INW, D), BF16),
        compiler_params=_params("parallel"),
    )(h, dq, dkv, dglu, dgl)


def _grad_w(a, b, name, tk, tn, col_sharded):
    k, n = a.shape[1], b.shape[1]

    single = n == tn
    sw = n // N_CHIPS

    def body(a_ref, b_ref, o_ref, at_ref):
        if single:
            res = _dot_tn(a_ref[...], b_ref[...]).astype(BF16)
            if col_sharded:
                for s in range(N_CHIPS):
                    o_ref[s] = res[:, s * sw:(s + 1) * sw]
            else:
                o_ref[...] = res
            return

        @pl.when(pl.program_id(1) == 0)
        def _():
            at_ref[...] = a_ref[...].T

        o_ref[...] = _dot(at_ref[...], b_ref[...]).astype(BF16)

    if col_sharded and single:
        shape = (N_CHIPS, k, sw)
        out_spec = pl.BlockSpec((N_CHIPS, tk, sw), lambda i, j: (0, i, 0))
    elif col_sharded:
        per = sw // tn
        shape = (N_CHIPS, k, sw)
        out_spec = pl.BlockSpec((None, tk, tn), lambda i, j: (j // per, i, j % per))
    else:
        shape = (1, k, n)
        out_spec = pl.BlockSpec((None, tk, tn), lambda i, j: (0, i, j))
    out = pl.pallas_call(
        body, name=name, grid=(k // tk, n // tn),
        in_specs=[pl.BlockSpec((T, tk), lambda i, j: (0, i)), pl.BlockSpec((T, tn), lambda i, j: (0, j))],
        out_specs=out_spec,
        out_shape=jax.ShapeDtypeStruct(shape, BF16),
        scratch_shapes=[pltpu.VMEM((tk, T), BF16)],
        compiler_params=_params("parallel", "arbitrary"),
    )(a, b)
    return out if col_sharded else out.reshape(N_CHIPS, k // N_CHIPS, n)


HBM_SPEC = pl.BlockSpec(memory_space=pltpu.HBM)


def _place():
    x, y, c = lax.axis_index("x"), lax.axis_index("y"), lax.axis_index("c")
    chips = [(1 - x, y), (x, 1 - y), (1 - x, 1 - y)]
    return x, y, c, chips


SEM_SPEC = pl.BlockSpec(memory_space=pltpu.SEMAPHORE)
ANY_SPEC = pl.BlockSpec(memory_space=pl.ANY)
VMEM_SPEC = pl.BlockSpec(memory_space=pltpu.VMEM)
EFFECT = pltpu.SideEffectType.DATAFLOW_SIDE_EFFECTING


def _gather_ends(src, land, x, y, c, chips):
    kh = src.shape[0] // 2
    s_me = 2 * x + y
    ends = [(src.at[pl.ds(c * kh, kh)], land.at[s_me, pl.ds(c * kh, kh)], (*chip, c)) for chip in chips]
    return ends + [(src, land.at[s_me], (x, y, 1 - c))]


def _reduce_ends(src, land, x, y, c, chips):
    return [(src.at[2 * chip[0] + chip[1]], land.at[j], (*chip, c)) for j, chip in enumerate(chips)]


def _chip_copies(ends, srcs, lands, send_sems, recv_sems, first=0):
    x, y, c, chips = _place()
    copies = []
    for src, land in zip(srcs, lands):
        peers = ends(src, land, x, y, c, chips)
        for s, d, to in peers:
            k = first * len(peers) + len(copies)
            copies.append(pltpu.make_async_remote_copy(
                src_ref=s, dst_ref=d, send_sem=send_sems.at[k], recv_sem=recv_sems.at[k],
                device_id=to, device_id_type=MESH))
    return copies


GATHER_PEERS, REDUCE_PEERS = 4, 3


def _chip_start(name, ends, peers, srcs, lands):
    n = len(srcs)

    def body(*refs):
        copies = _chip_copies(ends, refs[:n], refs[n:2 * n], refs[2 * n], refs[2 * n + 1])
        for cp in copies:
            cp.start()
        token = refs[-1]
        token[...] = jnp.zeros_like(token)

    hbm = lambda a: pltpu.HBM(a.shape, a.dtype)
    res = pl.pallas_call(
        body, name=name,
        out_shape=(pltpu.SemaphoreType.DMA((peers * n,)), pltpu.SemaphoreType.DMA((peers * n,)),
                   *[hbm(a) for a in srcs], *[hbm(a) for a in lands],
                   jax.ShapeDtypeStruct((8, 128), F32)),
        in_specs=[HBM_SPEC] * (2 * n),
        out_specs=(SEM_SPEC, SEM_SPEC, *[HBM_SPEC] * (2 * n), VMEM_SPEC),
        input_output_aliases={i: 2 + i for i in range(2 * n)},
        compiler_params=pltpu.CompilerParams(has_side_effects=EFFECT),
    )(*[pltpu.with_memory_space_constraint(a, pltpu.HBM) for a in (*srcs, *lands)])
    return res[0], res[1], list(res[2:2 + n]), list(res[2 + n:2 + 2 * n]), res[-1]


def _chip_wait(name, ends, send_sems, recv_sems, srcs, lands, after, first=0):
    n, na = len(srcs), len(after)

    def body(*refs):
        copies = _chip_copies(ends, refs[:n], refs[n:2 * n], refs[2 * n], refs[2 * n + 1], first)
        for cp in copies:
            cp.wait_send()
            cp.wait_recv()

    hbm = lambda a: pltpu.HBM(a.shape, a.dtype)
    res = pl.pallas_call(
        body, name=name,
        out_shape=tuple(hbm(a) for a in (*srcs, *lands)),
        in_specs=[HBM_SPEC] * (2 * n) + [SEM_SPEC, SEM_SPEC] + [ANY_SPEC] * na,
        out_specs=tuple([HBM_SPEC] * (2 * n)),
        input_output_aliases={i: i for i in range(2 * n)},
        compiler_params=pltpu.CompilerParams(has_side_effects=EFFECT),
    )(*srcs, *lands, send_sems, recv_sems, *after)
    return list(res[:n]), list(res[n:])


def _forward_copies(lands, send_sems, recv_sems):
    x, y, c, chips = _place()
    copies = []
    for land in lands:
        kh = land.shape[1] // 2
        for chip in chips:
            blk = land.at[2 * chip[0] + chip[1], pl.ds(c * kh, kh)]
            k = len(copies)
            copies.append(pltpu.make_async_remote_copy(
                src_ref=blk, dst_ref=blk, send_sem=send_sems.at[k], recv_sem=recv_sems.at[k],
                device_id=(x, y, 1 - c), device_id_type=MESH))
    return copies


def _gather_relay(name, send_sems, recv_sems, srcs, lands, after, first):
    n, na = len(srcs), len(after)

    def body(*refs):
        land_refs = refs[n:2 * n]
        for cp in _chip_copies(_gather_ends, refs[:n], land_refs, refs[2 * n], refs[2 * n + 1], first):
            cp.wait_send()
            cp.wait_recv()
        out = refs[2 * n + 2 + na:]
        for cp in _forward_copies(land_refs, out[0], out[1]):
            cp.start()
        out[-1][...] = jnp.zeros_like(out[-1])

    hbm = lambda a: pltpu.HBM(a.shape, a.dtype)
    res = pl.pallas_call(
        body, name=name,
        out_shape=(pltpu.SemaphoreType.DMA((3 * n,)), pltpu.SemaphoreType.DMA((3 * n,)),
                   *[hbm(a) for a in lands], jax.ShapeDtypeStruct((8, 128), F32)),
        in_specs=[HBM_SPEC] * (2 * n) + [SEM_SPEC, SEM_SPEC] + [ANY_SPEC] * na,
        out_specs=(SEM_SPEC, SEM_SPEC, *[HBM_SPEC] * n, VMEM_SPEC),
        input_output_aliases={n + i: 2 + i for i in range(n)},
        compiler_params=pltpu.CompilerParams(has_side_effects=EFFECT),
    )(*srcs, *lands, send_sems, recv_sems, *after)
    return res[0], res[1], list(res[2:2 + n]), res[-1]


def _forward_wait(name, send_sems, recv_sems, lands, after):
    n, na = len(lands), len(after)

    def body(*refs):
        for cp in _forward_copies(refs[:n], refs[n], refs[n + 1]):
            cp.wait_send()
            cp.wait_recv()

    hbm = lambda a: pltpu.HBM(a.shape, a.dtype)
    res = pl.pallas_call(
        body, name=name,
        out_shape=tuple(hbm(a) for a in lands),
        in_specs=[HBM_SPEC] * n + [SEM_SPEC, SEM_SPEC] + [ANY_SPEC] * na,
        out_specs=tuple([HBM_SPEC] * n),
        input_output_aliases={i: i for i in range(n)},
        compiler_params=pltpu.CompilerParams(has_side_effects=EFFECT),
    )(*lands, send_sems, recv_sems, *after)
    return list(res)


def _exchange_ends(src, land, x, y, c, chips):
    kh = src.shape[1] // 2
    return [(src.at[:, pl.ds((1 - c) * kh, kh)], land, (x, y, 1 - c))]


def _share_ends(src, land, x, y, c, chips):
    return [(src, land, (x, y, 1 - c))]


def _small_ends(src, land, x, y, c, chips):
    m = src.shape[0]
    rows = land.at[pl.ds((4 * x + 2 * y + c) * m, m)]
    peers = [(x, y, 1 - c)] + [(*chip, c) for chip in chips] + [(*chip, 1 - c) for chip in chips]
    return [(src, rows, to) for to in peers]


PAIR_PEERS, SMALL_PEERS = 1, 7


def _row_tile(k):
    for t in (256, 240, 128, 176, 64, 32, 16):
        if k % t == 0:
            return t
    raise ValueError(k)


def _pair_sum(c_idx, g, got, name):
    _, k, n = g.shape
    kh = k // 2
    tm = _row_tile(kh)
    nb = kh // tm

    def body(c_ref, g_ref, r_ref, o_ref):
        o_ref[...] = (g_ref[...].astype(F32) + r_ref[...].astype(F32)).astype(BF16)

    return pl.pallas_call(
        body, name=name,
        grid_spec=pltpu.PrefetchScalarGridSpec(
            num_scalar_prefetch=1, grid=(nb,),
            in_specs=[pl.BlockSpec((N_CHIPS, tm, n), lambda i, c_ref: (0, c_ref[0] * nb + i, 0)),
                      pl.BlockSpec((N_CHIPS, tm, n), lambda i, c_ref: (0, i, 0))],
            out_specs=pl.BlockSpec((N_CHIPS, tm, n), lambda i, c_ref: (0, i, 0))),
        out_shape=jax.ShapeDtypeStruct((N_CHIPS, kh, n), BF16),
        compiler_params=_params("parallel"),
    )(c_idx, g, got)


def _chip_sum(s_idx, mine, got, name):
    _, kh, n = mine.shape
    tm = _row_tile(kh)

    def body(s_ref, m_ref, r_ref, o_ref):
        acc = m_ref[0].astype(F32)
        for j in range(3):
            acc = acc + r_ref[j].astype(F32)
        o_ref[...] = acc

    return pl.pallas_call(
        body, name=name,
        grid_spec=pltpu.PrefetchScalarGridSpec(
            num_scalar_prefetch=1, grid=(kh // tm,),
            in_specs=[pl.BlockSpec((1, tm, n), lambda i, s_ref: (s_ref[0], i, 0)),
                      pl.BlockSpec((3, tm, n), lambda i, s_ref: (0, i, 0))],
            out_specs=pl.BlockSpec((tm, n), lambda i, s_ref: (i, 0))),
        out_shape=jax.ShapeDtypeStruct((kh, n), F32),
        compiler_params=_params("parallel"),
    )(s_idx, mine, got)


def _adamw_math(w, g, m, v):
    m = ADAM_B1 * m + (1.0 - ADAM_B1) * g
    v = ADAM_B2 * v + (1.0 - ADAM_B2) * (g * g)
    m_hat = m / (1.0 - ADAM_B1 ** ADAM_STEP)
    v_hat = v / (1.0 - ADAM_B2 ** ADAM_STEP)
    delta = -ADAM_LR * (m_hat / (jnp.sqrt(v_hat) + ADAM_EPS) + ADAM_WD * w)
    return delta, m, v


def _adamw(c_idx, w, g_mine, g_other, m, v, name):
    k, n = w.shape
    tm = k // 4

    def body(c_ref, w_ref, gm_ref, go_ref, m_ref, v_ref, g_ref, d_ref, mo_ref, vo_ref):
        g = jnp.where(pl.program_id(0) == c_ref[0], gm_ref[...], go_ref[...])
        d, mm, vv = _adamw_math(w_ref[...], g, m_ref[...], v_ref[...])
        g_ref[...] = g
        d_ref[...] = d
        mo_ref[...] = mm
        vo_ref[...] = vv

    full = pl.BlockSpec((tm, n), lambda h, i, c_ref: (2 * h + i, 0))
    mine = pl.BlockSpec((tm, n), lambda h, i, c_ref: (jnp.where(h == c_ref[0], i, 0), 0))
    other = pl.BlockSpec((tm, n), lambda h, i, c_ref: (jnp.where(h == c_ref[0], 0, i), 0))
    shp = jax.ShapeDtypeStruct((k, n), F32)
    return pl.pallas_call(
        body, name=name,
        grid_spec=pltpu.PrefetchScalarGridSpec(
            num_scalar_prefetch=1, grid=(2, 2),
            in_specs=[full, mine, other, full, full], out_specs=[full] * 4),
        out_shape=[shp] * 4, compiler_params=_params("arbitrary", "arbitrary"),
    )(c_idx, w, g_mine, g_other, m, v)


VEC_SLOTS = {
    "g_mix_norm": (0, 0, D), "b_conv_proj": (0, D, D), "g_ffn_norm": (0, 2 * D, D),
    "g_final": (0, 3 * D, D), "b_in": (1, 0, INW), "conv_b": (2, 0, C), "ln_g": (2, C, C),
    "ln_b": (2, 2 * C, C), "sinks": (2, 3 * C, NQ), "loss": (2, 3 * C + 128, 1),
}
VEC_ROWS, VEC_COLS = 8, 4 * D
CW_ROWS = 32
SMALL_NAMES = ["g_mix_norm", "b_in", "sinks", "conv_w", "conv_b", "ln_g", "ln_b",
               "b_conv_proj", "g_ffn_norm", "g_final"]
CW_LANES = C // N_CHIPS


def _pack_small(gs, loss):
    row0 = jnp.concatenate([gs["g_mix_norm"], gs["b_conv_proj"], gs["g_ffn_norm"], gs["g_final"]], axis=1)
    row1 = jnp.pad(gs["b_in"], ((0, 0), (0, VEC_COLS - INW)))
    row2 = jnp.concatenate([gs["conv_b"], gs["ln_g"], gs["ln_b"],
                            jnp.pad(gs["sinks"], ((0, 0), (0, 128 - NQ))),
                            jnp.pad(loss.reshape(1, 1), ((0, 0), (0, VEC_COLS - 3 * C - 129)))], axis=1)
    vec = jnp.concatenate([row0, row1, row2, jnp.zeros((VEC_ROWS - 3, VEC_COLS), F32)], axis=0)
    cw = jnp.pad(gs["conv_w"], ((0, CW_ROWS - KW), (0, 0)))
    return vec, cw


def _small_update(s_idx, vec_all, cw_all, wmv):
    nsm = len(SMALL_NAMES)

    def body(s_ref, vec_ref, cw_ref, *refs):
        ins = refs[:3 * nsm]
        outs = refs[3 * nsm:7 * nsm]
        loss_ref = refs[7 * nsm]

        def total(slot):
            row, lane, width = slot
            acc = vec_ref[row:row + 1, lane:lane + width]
            for k in range(1, 8):
                acc = acc + vec_ref[k * VEC_ROWS + row:k * VEC_ROWS + row + 1, lane:lane + width]
            return acc

        loss_ref[...] = jnp.broadcast_to(total(VEC_SLOTS["loss"]), loss_ref.shape)
        for p, name in enumerate(SMALL_NAMES):
            w_ref, m_ref, v_ref = ins[3 * p:3 * p + 3]
            g_ref, d_ref, mo_ref, vo_ref = outs[4 * p:4 * p + 4]
            if name == "conv_w":
                g = jnp.zeros((KW, CW_LANES), F32)
                for s in range(N_CHIPS):
                    cand = cw_ref[0:KW, s * CW_LANES:(s + 1) * CW_LANES]
                    for k in range(1, 8):
                        cand = cand + cw_ref[k * CW_ROWS:k * CW_ROWS + KW, s * CW_LANES:(s + 1) * CW_LANES]
                    g = jnp.where(s_ref[0] == s, cand, g)
            else:
                g = total(VEC_SLOTS[name])
            d, mm, vv = _adamw_math(w_ref[...], g, m_ref[...], v_ref[...])
            g_ref[...] = g
            d_ref[...] = d
            mo_ref[...] = mm
            vo_ref[...] = vv

    vmem = pl.BlockSpec(memory_space=pltpu.VMEM)
    flat = [a for t in wmv for a in t]
    out_shape = []
    for w, _, _ in wmv:
        out_shape += [jax.ShapeDtypeStruct(w.shape, F32)] * 4
    out_shape.append(jax.ShapeDtypeStruct((1, 128), F32))
    res = pl.pallas_call(
        body, name="small_update",
        in_specs=[pl.BlockSpec(memory_space=pltpu.SMEM)] + [vmem] * (2 + len(flat)),
        out_specs=[vmem] * len(out_shape), out_shape=out_shape,
    )(s_idx, vec_all, cw_all, *flat)
    return [tuple(res[4 * p:4 * p + 4]) for p in range(nsm)], res[4 * nsm]


WEIGHT_ORDER = ["g_mix_norm", "w_in", "b_in", "sinks", "conv_w", "conv_b", "ln_g", "ln_b",
                "w_attn_proj", "w_conv_proj", "b_conv_proj", "w_out", "g_ffn_norm", "w_ffn_in",
                "w_ffn_down", "g_final"]


def kernel(x, g_mix_norm, w_in, b_in, sinks, conv_w, conv_b, ln_g, ln_b, w_attn_proj, w_conv_proj, b_conv_proj, w_out, g_ffn_norm, w_ffn_in, w_ffn_down, g_final, loss_target, m_g_mix_norm, m_w_in, m_b_in, m_sinks, m_conv_w, m_conv_b, m_ln_g, m_ln_b, m_w_attn_proj, m_w_conv_proj, m_b_conv_proj, m_w_out, m_g_ffn_norm, m_w_ffn_in, m_w_ffn_down, m_g_final, v_g_mix_norm, v_w_in, v_b_in, v_sinks, v_conv_w, v_conv_b, v_ln_g, v_ln_b, v_w_attn_proj, v_w_conv_proj, v_b_conv_proj, v_w_out, v_g_ffn_norm, v_w_ffn_in, v_w_ffn_down, v_g_final):
    w = dict(g_mix_norm=g_mix_norm, w_in=w_in, b_in=b_in, sinks=sinks, conv_w=conv_w, conv_b=conv_b,
             ln_g=ln_g, ln_b=ln_b, w_attn_proj=w_attn_proj, w_conv_proj=w_conv_proj,
             b_conv_proj=b_conv_proj, w_out=w_out, g_ffn_norm=g_ffn_norm, w_ffn_in=w_ffn_in,
             w_ffn_down=w_ffn_down, g_final=g_final)
    m = dict(g_mix_norm=m_g_mix_norm, w_in=m_w_in, b_in=m_b_in, sinks=m_sinks, conv_w=m_conv_w,
             conv_b=m_conv_b, ln_g=m_ln_g, ln_b=m_ln_b, w_attn_proj=m_w_attn_proj,
             w_conv_proj=m_w_conv_proj, b_conv_proj=m_b_conv_proj, w_out=m_w_out,
             g_ffn_norm=m_g_ffn_norm, w_ffn_in=m_w_ffn_in, w_ffn_down=m_w_ffn_down, g_final=m_g_final)
    v = dict(g_mix_norm=v_g_mix_norm, w_in=v_w_in, b_in=v_b_in, sinks=v_sinks, conv_w=v_conv_w,
             conv_b=v_conv_b, ln_g=v_ln_g, ln_b=v_ln_b, w_attn_proj=v_w_attn_proj,
             w_conv_proj=v_w_conv_proj, b_conv_proj=v_b_conv_proj, w_out=v_w_out,
             g_ffn_norm=v_g_ffn_norm, w_ffn_in=v_w_ffn_in, w_ffn_down=v_w_ffn_down, g_final=v_g_final)

    c_idx = lax.axis_index("c").astype(jnp.int32).reshape(1)
    s_idx = (2 * lax.axis_index("x") + lax.axis_index("y")).astype(jnp.int32).reshape(1)

    out_g, out_d, out_m, out_v = {}, {}, {}, {}

    def gather_start(tag, shards):
        lands = [lax.empty((N_CHIPS,) + s.shape, s.dtype) for s in shards]
        return _chip_start("gather_start_" + tag, _gather_ends, GATHER_PEERS, shards, lands)

    def gather_relay(tag, state, after, first=0, count=None):
        send_sems, recv_sems, shards, lands, _ = state
        last = len(shards) if count is None else first + count
        return _gather_relay("gather_relay_" + tag, send_sems, recv_sems, shards[first:last],
                             lands[first:last], after, first)

    def gather_finish(tag, relay, after):
        return _forward_wait("forward_wait_" + tag, relay[0], relay[1], relay[2], after)

    names_b = ["w_attn_proj", "w_conv_proj", "w_out", "w_ffn_in", "w_ffn_down"]
    big = {name: (w[name][0], m[name][0], v[name][0]) for name in names_b}
    big["w_in"] = (w_in[0].T, m_w_in[0].T, v_w_in[0].T)
    state_a = gather_start("a", [big["w_in"][0].astype(BF16), jnp.pad(conv_w[0], ((0, CW_ROWS - KW), (0, 0)))])
    state_b = gather_start("b", [(big[name][0] + state_a[4][0, 0]).astype(BF16) for name in names_b])
    got_a = gather_finish("a", gather_relay("a", state_a, [state_b[4]]), [])
    w_in_t_full = got_a[0].reshape(INW, D)
    conv_w_full = got_a[1].transpose(1, 0, 2).reshape(CW_ROWS, C)[:KW]

    xs, target = x[0], loss_target[0]
    g_final2 = g_final.reshape(1, D)
    h, qkv, glu, gl = _in_proj(xs, g_mix_norm, w_in_t_full, b_in)
    o, lse = _attn_fwd(qkv, sinks)
    relay_1 = gather_relay("b1", state_b, [o], 0, 3)
    u, cact = _conv_fwd(glu, conv_w_full, conv_b, ln_g, ln_b, relay_1[3])
    w_ap4, w_cp4, w_out4 = gather_finish("b1", relay_1, [cact])
    w_out_full = w_out4.reshape(D, D)
    relay_2 = gather_relay("b2", state_b, [cact], 3, 1)
    mg, x1 = _mix_out(xs, o, cact, gl, w_ap4, w_cp4, b_conv_proj, w_out_full, relay_2[3])
    w_fi4, = gather_finish("b2", relay_2, [x1])
    relay_3 = gather_relay("b3", state_b, [x1], 4, 1)
    h2, gu, act = _ffn_in(x1, g_ffn_norm, w_fi4, relay_3[3])
    w_dn4, = gather_finish("b3", relay_3, [act])
    w_dn_full = w_dn4.reshape(DFF, D)
    dx2, dx2b, dg_final, loss_part = _ffn_out_loss(x1, act, w_dn_full, g_final2, target)

    def exchange_start(tag, grads):
        lands = [lax.empty((N_CHIPS, g.shape[1] // 2, g.shape[2]), g.dtype) for g in grads]
        return _chip_start("pair_start_" + tag, _exchange_ends, PAIR_PEERS, grads, lands)

    def reduce_start(tag, names, exchange, after):
        send_sems, recv_sems, grads, lands, _ = exchange
        grads, from_sibling = _chip_wait("pair_wait_" + tag, _exchange_ends, send_sems, recv_sems, grads, lands, after)
        pair = [_pair_sum(c_idx, g, r, "pair_sum_" + name) for name, g, r in zip(names, grads, from_sibling)]
        lands = [lax.empty((3,) + p.shape[1:], p.dtype) for p in pair]
        return _chip_start("chip_start_" + tag, _reduce_ends, REDUCE_PEERS, pair, lands)

    def reduce_sum(tag, names, state, after):
        send_sems, recv_sems, pair, lands, _ = state
        pair, lands = _chip_wait("chip_wait_" + tag, _reduce_ends, send_sems, recv_sems, pair, lands, after)
        mine = [_chip_sum(s_idx, p, r, "chip_sum_" + name) for name, p, r in zip(names, pair, lands)]
        others = [lax.empty(a.shape, a.dtype) for a in mine]
        return _chip_start("share_start_" + tag, _share_ends, PAIR_PEERS, mine, others)

    def reduce_finish(tag, names, share, after):
        send_sems, recv_sems, mine, others, _ = share
        mine, others = _chip_wait("share_wait_" + tag, _share_ends, send_sems, recv_sems, mine, others, after)
        for name, g_mine, g_other in zip(names, mine, others):
            wv, mv, vv = big[name]
            res = _adamw(c_idx, wv, g_mine, g_other, mv, vv, "adamw_" + name)
            if name == "w_in":
                res = [a.T for a in res]
            out_g[name], out_d[name], out_m[name], out_v[name] = [a[None] for a in res]

    dgu, dx1, dx1b, dg_ffn = _ffn_bwd(dx2, dx2b, gu, x1, g_ffn_norm, w_dn_full, w_fi4)
    names_1 = ["w_ffn_in", "w_ffn_down", "w_out", "w_attn_proj", "w_conv_proj"]
    grads_1 = [_grad_w(h2, dgu, "grad_w_ffn_in", 512, FSH, True),
               _grad_w(act, dx2b, "grad_w_ffn_down", 256, D, False)]
    dya, dyc, dgl, do, dc, db_cp = _mix_bwd(dx1b, gl, o, cact, b_conv_proj, w_out_full, w_ap4, w_cp4,
                                            relay_3[3])
    grads_1 += [_grad_w(mg, dx1b, "grad_w_out", 512, D, False),
                _grad_w(o, dya, "grad_w_attn_proj", 512, D, True),
                _grad_w(cact, dyc, "grad_w_conv_proj", 512, D, True)]
    exchange_1 = exchange_start("1", grads_1)
    dglu, dconv_w, dconv_b, dln_g, dln_b = _conv_bwd(glu, u, dc, conv_w_full, ln_g, ln_b, exchange_1[4])
    state_1 = reduce_start("1", names_1, exchange_1, [dglu])
    dq, dkv, dsinks = _attn_bwd(qkv, o, do, lse, sinks, state_1[4])
    names_2 = ["w_in"]
    gw_in_t = _grad_w_in_t(h, dq, dkv, dglu, dgl)
    exchange_2 = exchange_start("2", [gw_in_t.reshape(N_CHIPS, INW // N_CHIPS, D)])
    grad_x, dg_mix, db_in = _in_proj_bwd(dq, dkv, dglu, dgl, xs, dx1, g_mix_norm, w_in_t_full, exchange_2[4])

    gs = {"g_mix_norm": dg_mix, "b_in": db_in, "sinks": dsinks[:, 0].reshape(1, NQ),
          "conv_w": dconv_w, "conv_b": dconv_b, "ln_g": dln_g, "ln_b": dln_b,
          "b_conv_proj": db_cp, "g_ffn_norm": dg_ffn, "g_final": dg_final}
    blocks = list(_pack_small(gs, loss_part[0, 0]))
    tables = [lax.empty((8 * b.shape[0], b.shape[1]), b.dtype) for b in blocks]
    small = _chip_start("small_start", _small_ends, SMALL_PEERS, blocks, tables)

    state_2 = reduce_start("2", names_2, exchange_2, [grad_x, small[4]])
    share_1 = reduce_sum("1", names_1, state_1, [state_2[4]])
    blocks, tables = _chip_wait("small_wait", _small_ends, small[0], small[1], small[2], small[3],
                                [share_1[4]])
    me = 4 * lax.axis_index("x") + 2 * lax.axis_index("y") + lax.axis_index("c")
    vec_all, cw_all = [lax.dynamic_update_slice(t, b, (me * b.shape[0], 0)) for t, b in zip(tables, blocks)]

    def view(a, name):
        if name == "conv_w":
            return a[0]
        if name == "g_final":
            return a.reshape(1, D)
        return a

    wmv = [(view(w[name], name), view(m[name], name), view(v[name], name)) for name in SMALL_NAMES]
    small_out, loss_row = _small_update(s_idx, vec_all, cw_all, wmv)
    for name, (g, d, mm, vv) in zip(SMALL_NAMES, small_out):
        shape = w[name].shape
        out_g[name], out_d[name], out_m[name], out_v[name] = (
            g.reshape(shape), d.reshape(shape), mm.reshape(shape), vv.reshape(shape))

    reduce_finish("1", names_1, share_1, [loss_row])
    share_2 = reduce_sum("2", names_2, state_2, [out_d["w_conv_proj"]])
    reduce_finish("2", names_2, share_2, [])

    loss = loss_row[0, 0]
    return (loss, grad_x[None], *[out_g[k] for k in WEIGHT_ORDER], *[out_d[k] for k in WEIGHT_ORDER],
            *[out_m[k] for k in WEIGHT_ORDER], *[out_v[k] for k in WEIGHT_ORDER])
```

```python
import functools

import jax
import jax.numpy as jnp
from jax import lax
from jax.experimental import pallas as pl
from jax.experimental.pallas import tpu as pltpu

F32 = jnp.float32
BF16 = jnp.bfloat16

T = 2048
D = 1024
HD = 64
NQ = 8
NKV = 2
GROUP = NQ // NKV
BLK = 128
AW = NQ * HD
KVW = NKV * HD
C = 512
KW = 31
QKVW = AW + 2 * KVW
GLU_OFF = QKVW
GATE_OFF = GLU_OFF + 2 * C
INW = GATE_OFF + 2 * D
DFF = 2816
EPS = 1e-5
NEG = -1e30
SCALE = HD ** -0.5
HALO = 32
N_CHIPS = 4
FSH = 2 * DFF // N_CHIPS

ADAM_LR = 0.001
ADAM_B1 = 0.9
ADAM_B2 = 0.999
ADAM_EPS = 1e-08
ADAM_WD = 0.01
ADAM_STEP = 10

VMEM_LIMIT = 56 * 1024 * 1024
ROW_TM = 512
MESH = pl.DeviceIdType.MESH


def _params(*sem):
    return pltpu.CompilerParams(dimension_semantics=sem, vmem_limit_bytes=VMEM_LIMIT)


def _dot(a, b):
    return jnp.dot(a, b, preferred_element_type=F32)


def _dot_nt(a, b):
    return lax.dot_general(a, b, (((1,), (1,)), ((), ())), preferred_element_type=F32)


def _dot_tn(a, b):
    return lax.dot_general(a, b, (((0,), (0,)), ((), ())), preferred_element_type=F32)


def _sigmoid(v):
    return 1.0 / (1.0 + jnp.exp(-v))


def _rows(tm, n):
    return pl.BlockSpec((tm, n), lambda i: (i, 0))


def _whole(shape):
    return pl.BlockSpec(shape, lambda i: tuple(0 for _ in shape))


def _in_proj(x, g_mix, w_in_t, b_in):
    tm = ROW_TM

    def body(x_ref, g_ref, w_ref, b_ref, h_ref, qkv_ref, glu_ref, gl_ref):
        xv = x_ref[...]
        r = lax.rsqrt(jnp.mean(xv * xv, axis=-1, keepdims=True) + EPS)
        h = (xv * r * g_ref[...]).astype(BF16)
        h_ref[...] = h
        qkv_ref[...] = (_dot_nt(h, w_ref[0:GLU_OFF, :]) + b_ref[:, 0:GLU_OFF]).astype(BF16)
        glu_ref[...] = (_dot_nt(h, w_ref[GLU_OFF:GATE_OFF, :]) + b_ref[:, GLU_OFF:GATE_OFF]).astype(BF16)
        gl_ref[...] = (_dot_nt(h, w_ref[GATE_OFF:INW, :]) + b_ref[:, GATE_OFF:INW]).astype(BF16)

    return pl.pallas_call(
        body, name="in_proj", grid=(T // tm,),
        in_specs=[_rows(tm, D), _whole((1, D)), _whole((INW, D)), _whole((1, INW))],
        out_specs=[_rows(tm, D), _rows(tm, QKVW), _rows(tm, 2 * C), _rows(tm, 2 * D)],
        out_shape=[jax.ShapeDtypeStruct((T, D), BF16), jax.ShapeDtypeStruct((T, QKVW), BF16),
                   jax.ShapeDtypeStruct((T, 2 * C), BF16), jax.ShapeDtypeStruct((T, 2 * D), BF16)],
        compiler_params=_params("parallel"),
    )(x, g_mix, w_in_t, b_in)


GROWS = GROUP * BLK
BAND = 2 * BLK
ATT_SUB = 2


def _band(i):
    rb = pl.multiple_of(jnp.maximum(i - 1, 0) * BLK, BLK)
    row = lax.broadcasted_iota(jnp.int32, (GROWS, BAND), 0)
    kpos = rb + lax.broadcasted_iota(jnp.int32, (GROWS, BAND), 1)
    qpos = i * BLK + jnp.bitwise_and(row, BLK - 1)
    return rb, jnp.logical_and(kpos <= qpos, kpos > qpos - BLK)


def _sink_column(sink_ref, g):
    head = lax.shift_right_logical(lax.broadcasted_iota(jnp.int32, (GROWS, 1), 0), 7)
    col = jnp.full((GROWS, 1), sink_ref[0, g * GROUP], F32)
    for hh in range(1, GROUP):
        col = jnp.where(head == hh, sink_ref[0, g * GROUP + hh], col)
    return col


def _attn_fwd(qkv, sinks):
    def body(sink_ref, qkv_ref, o_ref, lse_ref, p_ref, s_ref):
        slots = [(sub, g) for sub in range(ATT_SUB) for g in range(NKV)]
        bands = [_band(pl.program_id(0) * ATT_SUB + sub) for sub in range(ATT_SUB)]
        for n, (sub, g) in enumerate(slots):
            rb = bands[sub][0]
            r0 = pl.multiple_of((pl.program_id(0) * ATT_SUB + sub) * BLK, BLK)
            kband = qkv_ref[pl.ds(rb, BAND), AW + g * HD:AW + (g + 1) * HD]
            for hh in range(GROUP):
                h = g * GROUP + hh
                s_ref[n, hh * BLK:(hh + 1) * BLK, :] = _dot_nt(
                    qkv_ref[pl.ds(r0, BLK), h * HD:(h + 1) * HD], kband)
        lses = []
        for n, (sub, g) in enumerate(slots):
            s = jnp.where(bands[sub][1], s_ref[n] * SCALE, NEG)
            sink = _sink_column(sink_ref, g)
            m = jnp.maximum(jnp.max(s, axis=-1, keepdims=True), sink)
            p = jnp.exp(s - m)
            den = jnp.sum(p, axis=-1, keepdims=True) + jnp.exp(sink - m)
            p_ref[n] = (p * (1.0 / den)).astype(BF16)
            lses.append(m + jnp.log(den))
        for n, (sub, g) in enumerate(slots):
            rb = bands[sub][0]
            rows = slice(sub * BLK, (sub + 1) * BLK)
            vband = qkv_ref[pl.ds(rb, BAND), AW + KVW + g * HD:AW + KVW + (g + 1) * HD]
            for hh in range(GROUP):
                h = g * GROUP + hh
                o_ref[rows, h * HD:(h + 1) * HD] = _dot(p_ref[n, hh * BLK:(hh + 1) * BLK, :], vband).astype(BF16)
                lse_ref[rows, h:h + 1] = lses[n][hh * BLK:(hh + 1) * BLK]

    nslot = ATT_SUB * NKV
    return pl.pallas_call(
        body, name="attn_fwd", grid=(T // (ATT_SUB * BLK),),
        in_specs=[pl.BlockSpec(memory_space=pltpu.SMEM), _whole((T, QKVW))],
        out_specs=[_rows(ATT_SUB * BLK, AW), _rows(ATT_SUB * BLK, NQ),
                   pl.BlockSpec((nslot, GROWS, BAND), lambda i: (i, 0, 0))],
        out_shape=[jax.ShapeDtypeStruct((T, AW), BF16), jax.ShapeDtypeStruct((T, NQ), F32),
                   jax.ShapeDtypeStruct((T // BLK * NKV, GROWS, BAND), BF16)],
        scratch_shapes=[pltpu.VMEM((nslot, GROWS, BAND), F32)],
        compiler_params=_params("parallel"),
    )(sinks, qkv)


CONV_TM = 256
CONV_SUB = 32


def _glu(ab):
    a = ab[:, 0:C].astype(F32)
    b = ab[:, C:2 * C].astype(F32)
    return a * _sigmoid(b)


SUBLANES = 8


def _shifted_copies(ref):
    rows = ref.shape[1] - SUBLANES
    for r in range(1, SUBLANES):
        ref[r, 0:rows, :] = ref[0, r:r + rows, :]


def _shifted_rows(ref, start, size):
    r = start % SUBLANES
    return ref[r, start - r:start - r + size, :]


def _conv_fwd(glu, conv_w, conv_b, ln_g, ln_b, dep):
    tm = CONV_TM

    def body(cur_ref, prev_ref, w_ref, cb_ref, g_ref, b_ref, dep_ref, u_ref, c_ref, zs_ref):
        i = pl.program_id(0)
        zprev = _glu(prev_ref[tm - HALO:tm, :])
        zs_ref[0, 0:HALO, :] = jnp.where(i > 0, zprev, 0.0)
        zs_ref[0, HALO:HALO + tm, :] = _glu(cur_ref[...])
        _shifted_copies(zs_ref)
        for s in range(tm // CONV_SUB):
            base = HALO + s * CONV_SUB - (KW - 1)
            acc = jnp.broadcast_to(cb_ref[...], (CONV_SUB, C))
            for j in range(KW):
                acc = acc + w_ref[j:j + 1, :] * _shifted_rows(zs_ref, base + j, CONV_SUB)
            rows = slice(s * CONV_SUB, (s + 1) * CONV_SUB)
            u_ref[rows, :] = acc
            mu = jnp.mean(acc, axis=-1, keepdims=True)
            xc = acc - mu
            var = jnp.mean(xc * xc, axis=-1, keepdims=True)
            y = xc * lax.rsqrt(var + EPS) * g_ref[...] + b_ref[...]
            c_ref[rows, :] = (y * _sigmoid(y)).astype(BF16)

    return pl.pallas_call(
        body, name="conv_fwd", grid=(T // tm,),
        in_specs=[_rows(tm, 2 * C),
                  pl.BlockSpec((tm, 2 * C), lambda i: (jnp.maximum(i - 1, 0), 0)),
                  _whole((KW, C)), _whole((1, C)), _whole((1, C)), _whole((1, C)), _whole((8, 128))],
        out_specs=[_rows(tm, C), _rows(tm, C)],
        out_shape=[jax.ShapeDtypeStruct((T, C), F32), jax.ShapeDtypeStruct((T, C), BF16)],
        scratch_shapes=[pltpu.VMEM((SUBLANES, HALO + tm, C), F32)],
        compiler_params=_params("parallel"),
    )(glu, glu, conv_w, conv_b, ln_g, ln_b, dep)


def _branch_outputs(o, cact, wap_ref, wcp_ref, bcp_ref):
    ya = jnp.concatenate([_dot(o, wap_ref[s]) for s in range(N_CHIPS)], axis=1)
    yc = jnp.concatenate([_dot(cact, wcp_ref[s]) for s in range(N_CHIPS)], axis=1) + bcp_ref[...]
    return ya, yc


def _mix_out(x, o, cact, gl, w_ap, w_cp, b_cp, w_out, dep):
    tm = ROW_TM

    def body(x_ref, o_ref, c_ref, gl_ref, wap_ref, wcp_ref, bcp_ref, wo_ref, dep_ref,
             ya_ref, yc_ref, mg_ref, x1_ref):
        ya, yc = _branch_outputs(o_ref[...], c_ref[...], wap_ref, wcp_ref, bcp_ref)
        g0 = _sigmoid(gl_ref[:, 0:D].astype(F32))
        g1 = _sigmoid(gl_ref[:, D:2 * D].astype(F32))
        mg = (g0 * ya + g1 * yc).astype(BF16)
        ya_ref[...] = ya.astype(BF16)
        yc_ref[...] = yc.astype(BF16)
        mg_ref[...] = mg
        x1_ref[...] = x_ref[...] + _dot(mg, wo_ref[...])

    return pl.pallas_call(
        body, name="mix_out", grid=(T // tm,),
        in_specs=[_rows(tm, D), _rows(tm, AW), _rows(tm, C), _rows(tm, 2 * D),
                  _whole((N_CHIPS, AW, D // N_CHIPS)), _whole((N_CHIPS, C, D // N_CHIPS)), _whole((1, D)),
                  _whole((D, D)), _whole((8, 128))],
        out_specs=[_rows(tm, D), _rows(tm, D), _rows(tm, D), _rows(tm, D)],
        out_shape=[jax.ShapeDtypeStruct((T, D), BF16), jax.ShapeDtypeStruct((T, D), BF16),
                   jax.ShapeDtypeStruct((T, D), BF16), jax.ShapeDtypeStruct((T, D), F32)],
        compiler_params=_params("parallel"),
    )(x, o, cact, gl, w_ap, w_cp, b_cp, w_out, dep)


def _ffn_in(x1, g_ffn, w_fi, dep):
    tm = ROW_TM

    def body(x_ref, g_ref, w_ref, dep_ref, h_ref, gu_ref, act_ref):
        xv = x_ref[...]
        r = lax.rsqrt(jnp.mean(xv * xv, axis=-1, keepdims=True) + EPS)
        h = (xv * r * g_ref[...]).astype(BF16)
        h_ref[...] = h
        for s in range(N_CHIPS // 2):
            c0 = s * FSH
            gate = _dot(h, w_ref[s])
            up = _dot(h, w_ref[s + N_CHIPS // 2])
            gu_ref[:, c0:c0 + FSH] = gate.astype(BF16)
            gu_ref[:, DFF + c0:DFF + c0 + FSH] = up.astype(BF16)
            act_ref[:, c0:c0 + FSH] = (gate * _sigmoid(gate) * up).astype(BF16)

    return pl.pallas_call(
        body, name="ffn_in", grid=(T // tm,),
        in_specs=[_rows(tm, D), _whole((1, D)), _const((N_CHIPS, D, FSH)), _whole((8, 128))],
        out_specs=[_rows(tm, D), _rows(tm, 2 * DFF), _rows(tm, DFF)],
        out_shape=[jax.ShapeDtypeStruct((T, D), BF16), jax.ShapeDtypeStruct((T, 2 * DFF), BF16),
                   jax.ShapeDtypeStruct((T, DFF), BF16)],
        compiler_params=_params("parallel"),
    )(x1, g_ffn, w_fi, dep)


def _ffn_out_loss(x1, act, w_dn, g_final, target):
    tm = ROW_TM

    def body(x_ref, a_ref, w_ref, g_ref, t_ref, dx_ref, dxb_ref, dg_ref, loss_ref):
        i = pl.program_id(0)
        x2 = x_ref[...] + _dot(a_ref[...], w_ref[...])
        r = lax.rsqrt(jnp.mean(x2 * x2, axis=-1, keepdims=True) + EPS)
        xh = x2 * r
        g = g_ref[...]
        err = xh * g - t_ref[...]
        dy = err * (1.0 / D)
        dyg = dy * g
        dx = r * (dyg - xh * jnp.mean(dyg * xh, axis=-1, keepdims=True))
        dx_ref[...] = dx
        dxb_ref[...] = dx.astype(BF16)
        part = 0.5 * jnp.sum(jnp.mean(err * err, axis=-1, keepdims=True), axis=0, keepdims=True)

        @pl.when(i == 0)
        def _():
            dg_ref[...] = jnp.zeros_like(dg_ref)
            loss_ref[...] = jnp.zeros_like(loss_ref)

        dg_ref[...] += jnp.sum(dy * xh, axis=0, keepdims=True)
        loss_ref[...] += jnp.broadcast_to(part, loss_ref.shape)

    return pl.pallas_call(
        body, name="ffn_out_loss", grid=(T // tm,),
        in_specs=[_rows(tm, D), _rows(tm, DFF), _whole((DFF, D)), _whole((1, D)), _rows(tm, D)],
        out_specs=[_rows(tm, D), _rows(tm, D), _whole((1, D)), _whole((1, 128))],
        out_shape=[jax.ShapeDtypeStruct((T, D), F32), jax.ShapeDtypeStruct((T, D), BF16),
                   jax.ShapeDtypeStruct((1, D), F32), jax.ShapeDtypeStruct((1, 128), F32)],
        compiler_params=_params("arbitrary"),
    )(x1, act, w_dn, g_final, target)


def _const(shape):
    return pl.BlockSpec(shape, lambda i: tuple(0 for _ in shape), pipeline_mode=pl.Buffered(1))


def _ffn_bwd(dx2, dx2b, gu, x1, g_ffn, w_dn_t, w_fi_t):
    tm = ROW_TM // 2

    def body(dx_ref, dxb_ref, gu_ref, x_ref, g_ref, wdn_ref, wfi_ref,
             dgu_ref, dx1_ref, dx1b_ref, dg_ref):
        i = pl.program_id(0)
        dxb = dxb_ref[...]
        dh = jnp.zeros((tm, D), F32)
        for k in range(N_CHIPS // 2):
            c0 = k * FSH
            dact = _dot_nt(dxb, wdn_ref[c0:c0 + FSH, :])
            gate = gu_ref[:, c0:c0 + FSH].astype(F32)
            up = gu_ref[:, DFF + c0:DFF + c0 + FSH].astype(F32)
            s = _sigmoid(gate)
            dup = (dact * gate * s).astype(BF16)
            dgate = (dact * up * s * (1.0 + gate * (1.0 - s))).astype(BF16)
            dgu_ref[:, c0:c0 + FSH] = dgate
            dgu_ref[:, DFF + c0:DFF + c0 + FSH] = dup
            dh = dh + _dot_nt(dgate, wfi_ref[k]) + _dot_nt(dup, wfi_ref[k + N_CHIPS // 2])
        xv = x_ref[...]
        r = lax.rsqrt(jnp.mean(xv * xv, axis=-1, keepdims=True) + EPS)
        xh = xv * r
        dhg = dh * g_ref[...]
        dx1 = dx_ref[...] + r * (dhg - xh * jnp.mean(dhg * xh, axis=-1, keepdims=True))
        dx1_ref[...] = dx1
        dx1b_ref[...] = dx1.astype(BF16)

        @pl.when(i == 0)
        def _():
            dg_ref[...] = jnp.zeros_like(dg_ref)

        dg_ref[...] += jnp.sum(dh * xh, axis=0, keepdims=True)

    return pl.pallas_call(
        body, name="ffn_bwd", grid=(T // tm,),
        in_specs=[_rows(tm, D), _rows(tm, D), _rows(tm, 2 * DFF), _rows(tm, D), _whole((1, D)),
                  _const((DFF, D)), _const((N_CHIPS, D, FSH))],
        out_specs=[_rows(tm, 2 * DFF), _rows(tm, D), _rows(tm, D), _whole((1, D))],
        out_shape=[jax.ShapeDtypeStruct((T, 2 * DFF), BF16), jax.ShapeDtypeStruct((T, D), F32),
                   jax.ShapeDtypeStruct((T, D), BF16), jax.ShapeDtypeStruct((1, D), F32)],
        compiler_params=_params("arbitrary"),
    )(dx2, dx2b, gu, x1, g_ffn, w_dn_t, w_fi_t)


def _mix_bwd(dx1b, gl, ya, yc, w_out, w_ap, w_cp, dep):
    tm = ROW_TM

    def body(dx_ref, gl_ref, ya_ref, yc_ref, wo_ref, wap_ref, wcp_ref, dep_ref,
             dya_ref, dyc_ref, dgl_ref, do_ref, dc_ref, db_ref):
        i = pl.program_id(0)
        dm = _dot_nt(dx_ref[...], wo_ref[...])
        ya, yc = ya_ref[...].astype(F32), yc_ref[...].astype(F32)
        g0 = _sigmoid(gl_ref[:, 0:D].astype(F32))
        g1 = _sigmoid(gl_ref[:, D:2 * D].astype(F32))
        dya = dm * g0
        dyc = dm * g1
        dgl_ref[:, 0:D] = (dya * ya * (1.0 - g0)).astype(BF16)
        dgl_ref[:, D:2 * D] = (dyc * yc * (1.0 - g1)).astype(BF16)
        dyab = dya.astype(BF16)
        dycb = dyc.astype(BF16)
        dya_ref[...] = dyab
        dyc_ref[...] = dycb
        sw = D // N_CHIPS
        do = jnp.zeros((tm, AW), F32)
        dcv = jnp.zeros((tm, C), F32)
        for s in range(N_CHIPS):
            do = do + _dot_nt(dyab[:, s * sw:(s + 1) * sw], wap_ref[s])
            dcv = dcv + _dot_nt(dycb[:, s * sw:(s + 1) * sw], wcp_ref[s])
        do_ref[...] = do.astype(BF16)
        dc_ref[...] = dcv.astype(BF16)

        @pl.when(i == 0)
        def _():
            db_ref[...] = jnp.zeros_like(db_ref)

        db_ref[...] += jnp.sum(dyc, axis=0, keepdims=True)

    return pl.pallas_call(
        body, name="mix_bwd", grid=(T // tm,),
        in_specs=[_rows(tm, D), _rows(tm, 2 * D), _rows(tm, D), _rows(tm, D),
                  _whole((D, D)), _whole((N_CHIPS, AW, D // N_CHIPS)), _whole((N_CHIPS, C, D // N_CHIPS)),
                  _whole((8, 128))],
        out_specs=[_rows(tm, D), _rows(tm, D), _rows(tm, 2 * D), _rows(tm, AW), _rows(tm, C),
                   _whole((1, D))],
        out_shape=[jax.ShapeDtypeStruct((T, D), BF16), jax.ShapeDtypeStruct((T, D), BF16),
                   jax.ShapeDtypeStruct((T, 2 * D), BF16), jax.ShapeDtypeStruct((T, AW), BF16),
                   jax.ShapeDtypeStruct((T, C), BF16), jax.ShapeDtypeStruct((1, D), F32)],
        compiler_params=_params("arbitrary"),
    )(dx1b, gl, ya, yc, w_out, w_ap, w_cp, dep)


def _conv_bwd(glu, u, dc, conv_w, ln_g, ln_b, dep):
    tm = CONV_TM
    nblk = T // tm

    def du_of(uv, dcv, g_ref, b_ref):
        mu = jnp.mean(uv, axis=-1, keepdims=True)
        xc = uv - mu
        var = jnp.mean(xc * xc, axis=-1, keepdims=True)
        rstd = lax.rsqrt(var + EPS)
        xh = xc * rstd
        y = xh * g_ref[...] + b_ref[...]
        sg = _sigmoid(y)
        dy = dcv * (sg * (1.0 + y * (1.0 - sg)))
        dxh = dy * g_ref[...]
        du = rstd * (dxh - jnp.mean(dxh, axis=-1, keepdims=True)
                     - xh * jnp.mean(dxh * xh, axis=-1, keepdims=True))
        return du, dy, xh

    def body(cur_ref, prev_ref, u_ref, un_ref, dc_ref, dcn_ref, w_ref, g_ref, b_ref, dep_ref,
             dglu_ref, dw_ref, dcb_ref, dg_ref, db_ref, zs_ref, dus_ref):
        i = pl.program_id(0)

        @pl.when(i == 0)
        def _():
            dw_ref[...] = jnp.zeros_like(dw_ref)
            dcb_ref[...] = jnp.zeros_like(dcb_ref)
            dg_ref[...] = jnp.zeros_like(dg_ref)
            db_ref[...] = jnp.zeros_like(db_ref)

        zprev = _glu(prev_ref[tm - HALO:tm, :])
        zs_ref[0, 0:HALO, :] = jnp.where(i > 0, zprev, 0.0)
        zs_ref[0, HALO:HALO + tm, :] = _glu(cur_ref[...])
        _shifted_copies(zs_ref)

        dun, _, _ = du_of(un_ref[0:HALO, :], dcn_ref[0:HALO, :].astype(F32), g_ref, b_ref)
        dus_ref[0, tm:tm + HALO, :] = jnp.where(i < nblk - 1, dun, 0.0)
        dg_acc = jnp.zeros((1, C), F32)
        db_acc = jnp.zeros((1, C), F32)
        dcb_acc = jnp.zeros((1, C), F32)
        for s in range(tm // CONV_SUB):
            rows = slice(s * CONV_SUB, (s + 1) * CONV_SUB)
            du, dy, xh = du_of(u_ref[rows, :], dc_ref[rows, :].astype(F32), g_ref, b_ref)
            dus_ref[0, rows, :] = du
            dg_acc = dg_acc + jnp.sum(dy * xh, axis=0, keepdims=True)
            db_acc = db_acc + jnp.sum(dy, axis=0, keepdims=True)
            dcb_acc = dcb_acc + jnp.sum(du, axis=0, keepdims=True)
        dg_ref[...] += dg_acc
        db_ref[...] += db_acc
        dcb_ref[...] += dcb_acc
        _shifted_copies(dus_ref)

        for j in range(KW):
            acc = jnp.zeros((CONV_SUB, C), F32)
            for s in range(tm // CONV_SUB):
                base = HALO + s * CONV_SUB - (KW - 1) + j
                acc = acc + dus_ref[0, s * CONV_SUB:(s + 1) * CONV_SUB, :] * _shifted_rows(zs_ref, base, CONV_SUB)
            dw_ref[j:j + 1, :] += jnp.sum(acc, axis=0, keepdims=True)

        for s in range(tm // CONV_SUB):
            rows = slice(s * CONV_SUB, (s + 1) * CONV_SUB)
            dz = jnp.zeros((CONV_SUB, C), F32)
            for j in range(KW):
                dz = dz + w_ref[j:j + 1, :] * _shifted_rows(dus_ref, s * CONV_SUB + (KW - 1) - j, CONV_SUB)
            a = cur_ref[rows, 0:C].astype(F32)
            sb = _sigmoid(cur_ref[rows, C:2 * C].astype(F32))
            dglu_ref[rows, 0:C] = (dz * sb).astype(BF16)
            dglu_ref[rows, C:2 * C] = (dz * a * sb * (1.0 - sb)).astype(BF16)

    nxt = lambda i: (jnp.minimum(i + 1, nblk - 1), 0)
    return pl.pallas_call(
        body, name="conv_bwd", grid=(nblk,),
        in_specs=[_rows(tm, 2 * C),
                  pl.BlockSpec((tm, 2 * C), lambda i: (jnp.maximum(i - 1, 0), 0)),
                  _rows(tm, C), pl.BlockSpec((tm, C), nxt),
                  _rows(tm, C), pl.BlockSpec((tm, C), nxt),
                  _whole((KW, C)), _whole((1, C)), _whole((1, C)), _whole((8, 128))],
        out_specs=[_rows(tm, 2 * C), _whole((KW, C)), _whole((1, C)), _whole((1, C)), _whole((1, C))],
        out_shape=[jax.ShapeDtypeStruct((T, 2 * C), BF16), jax.ShapeDtypeStruct((KW, C), F32),
                   jax.ShapeDtypeStruct((1, C), F32), jax.ShapeDtypeStruct((1, C), F32),
                   jax.ShapeDtypeStruct((1, C), F32)],
        scratch_shapes=[pltpu.VMEM((SUBLANES, HALO + tm, C), F32), pltpu.VMEM((SUBLANES, tm + HALO, C), F32)],
        compiler_params=_params("arbitrary"),
    )(glu, glu, u, u, dc, dc, conv_w, ln_g, ln_b, dep)


def _attn_bwd(qkv, o, do, lse, p, sinks, dep):
    nsub = 1

    def body(sink_ref, qkv_ref, o_ref, do_ref, lse_ref, p_ref, dep_ref, dq_ref, dkv_ref, ds_ref,
             dp_ref, dsb_ref):
        i = pl.program_id(0)

        @pl.when(i == 0)
        def _():
            dkv_ref[...] = jnp.zeros_like(dkv_ref)
            ds_ref[...] = jnp.zeros_like(ds_ref)

        slots = [(sub, g) for sub in range(nsub) for g in range(NKV)]
        bands = [_band(i * nsub + sub) for sub in range(nsub)]
        r0s = [pl.multiple_of((i * nsub + sub) * BLK, BLK) for sub in range(nsub)]
        lses, dls = [], []
        for n, (sub, g) in enumerate(slots):
            rb, blk = bands[sub][0], slice(sub * BLK, (sub + 1) * BLK)
            vband = qkv_ref[pl.ds(rb, BAND), AW + KVW + g * HD:AW + KVW + (g + 1) * HD]
            lse_parts, dl_parts = [], []
            for hh in range(GROUP):
                h = g * GROUP + hh
                hcol = slice(h * HD, (h + 1) * HD)
                doh = do_ref[blk, hcol]
                dp_ref[n, hh * BLK:(hh + 1) * BLK, :] = _dot_nt(doh, vband)
                lse_parts.append(lse_ref[blk, h:h + 1])
                dl_parts.append(jnp.sum(doh.astype(F32) * o_ref[blk, hcol].astype(F32), axis=-1, keepdims=True))
            lses.append(jnp.concatenate(lse_parts, axis=0))
            dls.append(jnp.concatenate(dl_parts, axis=0))
        dsinks = []
        for n, (sub, g) in enumerate(slots):
            dsb_ref[n] = (p_ref[n].astype(F32) * (dp_ref[n] - dls[n])).astype(BF16)
            dsinks.append(-(jnp.exp(_sink_column(sink_ref, g) - lses[n]) * dls[n]))
        for n, (sub, g) in enumerate(slots):
            rb, blk = bands[sub][0], slice(sub * BLK, (sub + 1) * BLK)
            kband = qkv_ref[pl.ds(rb, BAND), AW + g * HD:AW + (g + 1) * HD]
            dk = jnp.zeros((BAND, HD), F32)
            dv = jnp.zeros((BAND, HD), F32)
            for hh in range(GROUP):
                h = g * GROUP + hh
                hcol = slice(h * HD, (h + 1) * HD)
                rows = slice(hh * BLK, (hh + 1) * BLK)
                dq_ref[blk, hcol] = (_dot(dsb_ref[n, rows, :], kband) * SCALE).astype(BF16)
                dk = dk + _dot_tn(dsb_ref[n, rows, :], qkv_ref[pl.ds(r0s[sub], BLK), hcol])
                dv = dv + _dot_tn(p_ref[n, rows, :], do_ref[blk, hcol])
                ds_ref[h:h + 1, :] += jnp.broadcast_to(
                    jnp.sum(dsinks[n][rows], axis=0, keepdims=True), (1, 128))
            dkv_ref[pl.ds(rb, BAND), g * HD:(g + 1) * HD] += dk * SCALE
            dkv_ref[pl.ds(rb, BAND), KVW + g * HD:KVW + (g + 1) * HD] += dv

    nslot, tq = nsub * NKV, nsub * BLK
    return pl.pallas_call(
        body, name="attn_bwd", grid=(T // tq,),
        in_specs=[pl.BlockSpec(memory_space=pltpu.SMEM), _whole((T, QKVW)),
                  _rows(tq, AW), _rows(tq, AW), _rows(tq, NQ),
                  pl.BlockSpec((nslot, GROWS, BAND), lambda i: (i, 0, 0)), _whole((8, 128))],
        out_specs=[_rows(tq, AW), _whole((T, 2 * KVW)), _whole((NQ, 128))],
        out_shape=[jax.ShapeDtypeStruct((T, AW), BF16), jax.ShapeDtypeStruct((T, 2 * KVW), F32),
                   jax.ShapeDtypeStruct((NQ, 128), F32)],
        scratch_shapes=[pltpu.VMEM((nslot, GROWS, BAND), F32), pltpu.VMEM((nslot, GROWS, BAND), BF16)],
        compiler_params=_params("arbitrary"),
    )(sinks, qkv, o, do, lse, p, dep)


PROJ_PARTS = [(0, AW), (AW, QKVW), (GLU_OFF, GATE_OFF), (GATE_OFF, INW)]


def _in_proj_bwd(dq, dkv, dglu, dgl, x, dx1, g_mix, w_in_t, dep):
    tm = ROW_TM

    def body(dq_ref, dkv_ref, dglu_ref, dgl_ref, x_ref, dx1_ref, g_ref, w_ref, dep_ref,
             gx_ref, dg_ref, db_ref):
        i = pl.program_id(0)

        @pl.when(i == 0)
        def _():
            dg_ref[...] = jnp.zeros_like(dg_ref)
            db_ref[...] = jnp.zeros_like(db_ref)

        dh = jnp.zeros((tm, D), F32)
        for part_ref, (lo, hi) in zip((dq_ref, dkv_ref, dglu_ref, dgl_ref), PROJ_PARTS):
            part = part_ref[...]
            dh = dh + _dot(part.astype(BF16), w_ref[lo:hi, :])
            db_ref[:, lo:hi] += jnp.sum(part.astype(F32), axis=0, keepdims=True)
        xv = x_ref[...]
        r = lax.rsqrt(jnp.mean(xv * xv, axis=-1, keepdims=True) + EPS)
        xh = xv * r
        dhg = dh * g_ref[...]
        gx_ref[...] = dx1_ref[...] + r * (dhg - xh * jnp.mean(dhg * xh, axis=-1, keepdims=True))
        dg_ref[...] += jnp.sum(dh * xh, axis=0, keepdims=True)

    return pl.pallas_call(
        body, name="in_proj_bwd", grid=(T // tm,),
        in_specs=[_rows(tm, AW), _rows(tm, 2 * KVW), _rows(tm, 2 * C), _rows(tm, 2 * D),
                  _rows(tm, D), _rows(tm, D), _whole((1, D)), _const((INW, D)), _whole((8, 128))],
        out_specs=[_rows(tm, D), _whole((1, D)), _whole((1, INW))],
        out_shape=[jax.ShapeDtypeStruct((T, D), F32), jax.ShapeDtypeStruct((1, D), F32),
                   jax.ShapeDtypeStruct((1, INW), F32)],
        compiler_params=_params("arbitrary"),
    )(dq, dkv, dglu, dgl, x, dx1, g_mix, w_in_t, dep)


def _grad_w_in_t(h, dq, dkv, dglu, dgl):
    tn, chunk = 512, 256

    def body(h_ref, dq_ref, dkv_ref, dglu_ref, dgl_ref, o_ref):
        hv = h_ref[...]
        for part_ref, (lo, hi) in zip((dq_ref, dkv_ref, dglu_ref, dgl_ref), PROJ_PARTS):
            for c0 in range(0, hi - lo, chunk):
                o_ref[lo + c0:lo + c0 + chunk, :] = _dot_tn(
                    part_ref[:, c0:c0 + chunk].astype(BF16), hv).astype(BF16)

    return pl.pallas_call(
        body, name="grad_w_in", grid=(D // tn,),
        in_specs=[pl.BlockSpec((T, tn), lambda j: (0, j)), _const((T, AW)), _const((T, 2 * KVW)),
                  _const((T, 2 * C)), _const((T, 2 * D))],
        out_specs=pl.BlockSpec((INW, tn), lambda j: (0, j)),
        out_shape=jax.ShapeDtypeStruct((INW, D), BF16),
        compiler_params=_params("parallel"),
    )(h, dq, dkv, dglu, dgl)


def _grad_w(a, b, name, tk, tn, col_sharded):
    k, n = a.shape[1], b.shape[1]

    single = n == tn
    sw = n // N_CHIPS

    def body(a_ref, b_ref, o_ref, at_ref):
        if single:
            res = _dot_tn(a_ref[...], b_ref[...]).astype(BF16)
            if col_sharded:
                for s in range(N_CHIPS):
                    o_ref[s] = res[:, s * sw:(s + 1) * sw]
            else:
                o_ref[...] = res
            return

        @pl.when(pl.program_id(1) == 0)
        def _():
            at_ref[...] = a_ref[...].T

        o_ref[...] = _dot(at_ref[...], b_ref[...]).astype(BF16)

    if col_sharded and single:
        shape = (N_CHIPS, k, sw)
        out_spec = pl.BlockSpec((N_CHIPS, tk, sw), lambda i, j: (0, i, 0))
    elif col_sharded:
        per = sw // tn
        shape = (N_CHIPS, k, sw)
        out_spec = pl.BlockSpec((None, tk, tn), lambda i, j: (j // per, i, j % per))
    else:
        shape = (1, k, n)
        out_spec = pl.BlockSpec((None, tk, tn), lambda i, j: (0, i, j))
    out = pl.pallas_call(
        body, name=name, grid=(k // tk, n // tn),
        in_specs=[pl.BlockSpec((T, tk), lambda i, j: (0, i)), pl.BlockSpec((T, tn), lambda i, j: (0, j))],
        out_specs=out_spec,
        out_shape=jax.ShapeDtypeStruct(shape, BF16),
        scratch_shapes=[pltpu.VMEM((tk, T), BF16)],
        compiler_params=_params("parallel", "arbitrary"),
    )(a, b)
    return out if col_sharded else out.reshape(N_CHIPS, k // N_CHIPS, n)


HBM_SPEC = pl.BlockSpec(memory_space=pltpu.HBM)


def _place():
    x, y, c = lax.axis_index("x"), lax.axis_index("y"), lax.axis_index("c")
    chips = [(1 - x, y), (x, 1 - y), (1 - x, 1 - y)]
    return x, y, c, chips


SEM_SPEC = pl.BlockSpec(memory_space=pltpu.SEMAPHORE)
ANY_SPEC = pl.BlockSpec(memory_space=pl.ANY)
VMEM_SPEC = pl.BlockSpec(memory_space=pltpu.VMEM)
EFFECT = pltpu.SideEffectType.DATAFLOW_SIDE_EFFECTING


def _gather_ends(src, land, x, y, c, chips):
    kh = src.shape[0] // 2
    s_me = 2 * x + y
    ends = [(src.at[pl.ds(c * kh, kh)], land.at[s_me, pl.ds(c * kh, kh)], (*chip, c)) for chip in chips]
    return ends + [(src, land.at[s_me], (x, y, 1 - c))]


def _reduce_ends(src, land, x, y, c, chips):
    return [(src.at[2 * chip[0] + chip[1]], land.at[j], (*chip, c)) for j, chip in enumerate(chips)]


def _chip_copies(ends, srcs, lands, send_sems, recv_sems, first=0):
    x, y, c, chips = _place()
    copies = []
    for src, land in zip(srcs, lands):
        peers = ends(src, land, x, y, c, chips)
        for s, d, to in peers:
            k = first * len(peers) + len(copies)
            copies.append(pltpu.make_async_remote_copy(
                src_ref=s, dst_ref=d, send_sem=send_sems.at[k], recv_sem=recv_sems.at[k],
                device_id=to, device_id_type=MESH))
    return copies


GATHER_PEERS, REDUCE_PEERS = 4, 3


def _chip_start(name, ends, peers, srcs, lands):
    n = len(srcs)

    def body(*refs):
        copies = _chip_copies(ends, refs[:n], refs[n:2 * n], refs[2 * n], refs[2 * n + 1])
        for cp in copies:
            cp.start()
        token = refs[-1]
        token[...] = jnp.zeros_like(token)

    hbm = lambda a: pltpu.HBM(a.shape, a.dtype)
    res = pl.pallas_call(
        body, name=name,
        out_shape=(pltpu.SemaphoreType.DMA((peers * n,)), pltpu.SemaphoreType.DMA((peers * n,)),
                   *[hbm(a) for a in srcs], *[hbm(a) for a in lands],
                   jax.ShapeDtypeStruct((8, 128), F32)),
        in_specs=[HBM_SPEC] * (2 * n),
        out_specs=(SEM_SPEC, SEM_SPEC, *[HBM_SPEC] * (2 * n), VMEM_SPEC),
        input_output_aliases={i: 2 + i for i in range(2 * n)},
        compiler_params=pltpu.CompilerParams(has_side_effects=EFFECT),
    )(*[pltpu.with_memory_space_constraint(a, pltpu.HBM) for a in (*srcs, *lands)])
    return res[0], res[1], list(res[2:2 + n]), list(res[2 + n:2 + 2 * n]), res[-1]


def _chip_wait(name, ends, send_sems, recv_sems, srcs, lands, after, first=0):
    n, na = len(srcs), len(after)

    def body(*refs):
        copies = _chip_copies(ends, refs[:n], refs[n:2 * n], refs[2 * n], refs[2 * n + 1], first)
        for cp in copies:
            cp.wait_send()
            cp.wait_recv()

    hbm = lambda a: pltpu.HBM(a.shape, a.dtype)
    res = pl.pallas_call(
        body, name=name,
        out_shape=tuple(hbm(a) for a in (*srcs, *lands)),
        in_specs=[HBM_SPEC] * (2 * n) + [SEM_SPEC, SEM_SPEC] + [ANY_SPEC] * na,
        out_specs=tuple([HBM_SPEC] * (2 * n)),
        input_output_aliases={i: i for i in range(2 * n)},
        compiler_params=pltpu.CompilerParams(has_side_effects=EFFECT),
    )(*srcs, *lands, send_sems, recv_sems, *after)
    return list(res[:n]), list(res[n:])


def _forward_copies(lands, send_sems, recv_sems):
    x, y, c, chips = _place()
    copies = []
    for land in lands:
        kh = land.shape[1] // 2
        for chip in chips:
            blk = land.at[2 * chip[0] + chip[1], pl.ds(c * kh, kh)]
            k = len(copies)
            copies.append(pltpu.make_async_remote_copy(
                src_ref=blk, dst_ref=blk, send_sem=send_sems.at[k], recv_sem=recv_sems.at[k],
                device_id=(x, y, 1 - c), device_id_type=MESH))
    return copies


def _gather_relay(name, send_sems, recv_sems, srcs, lands, after, first):
    n, na = len(srcs), len(after)

    def body(*refs):
        land_refs = refs[n:2 * n]
        for cp in _chip_copies(_gather_ends, refs[:n], land_refs, refs[2 * n], refs[2 * n + 1], first):
            cp.wait_send()
            cp.wait_recv()
        out = refs[2 * n + 2 + na:]
        for cp in _forward_copies(land_refs, out[0], out[1]):
            cp.start()
        out[-1][...] = jnp.zeros_like(out[-1])

    hbm = lambda a: pltpu.HBM(a.shape, a.dtype)
    res = pl.pallas_call(
        body, name=name,
        out_shape=(pltpu.SemaphoreType.DMA((3 * n,)), pltpu.SemaphoreType.DMA((3 * n,)),
                   *[hbm(a) for a in lands], jax.ShapeDtypeStruct((8, 128), F32)),
        in_specs=[HBM_SPEC] * (2 * n) + [SEM_SPEC, SEM_SPEC] + [ANY_SPEC] * na,
        out_specs=(SEM_SPEC, SEM_SPEC, *[HBM_SPEC] * n, VMEM_SPEC),
        input_output_aliases={n + i: 2 + i for i in range(n)},
        compiler_params=pltpu.CompilerParams(has_side_effects=EFFECT),
    )(*srcs, *lands, send_sems, recv_sems, *after)
    return res[0], res[1], list(res[2:2 + n]), res[-1]


def _forward_wait(name, send_sems, recv_sems, lands, after):
    n, na = len(lands), len(after)

    def body(*refs):
        for cp in _forward_copies(refs[:n], refs[n], refs[n + 1]):
            cp.wait_send()
            cp.wait_recv()

    hbm = lambda a: pltpu.HBM(a.shape, a.dtype)
    res = pl.pallas_call(
        body, name=name,
        out_shape=tuple(hbm(a) for a in lands),
        in_specs=[HBM_SPEC] * n + [SEM_SPEC, SEM_SPEC] + [ANY_SPEC] * na,
        out_specs=tuple([HBM_SPEC] * n),
        input_output_aliases={i: i for i in range(n)},
        compiler_params=pltpu.CompilerParams(has_side_effects=EFFECT),
    )(*lands, send_sems, recv_sems, *after)
    return list(res)


def _exchange_ends(src, land, x, y, c, chips):
    kh = src.shape[1] // 2
    return [(src.at[:, pl.ds((1 - c) * kh, kh)], land, (x, y, 1 - c))]


def _share_ends(src, land, x, y, c, chips):
    return [(src, land, (x, y, 1 - c))]


def _small_ends(src, land, x, y, c, chips):
    m = src.shape[0]
    rows = land.at[pl.ds((4 * x + 2 * y + c) * m, m)]
    peers = [(x, y, 1 - c)] + [(*chip, c) for chip in chips] + [(*chip, 1 - c) for chip in chips]
    return [(src, rows, to) for to in peers]


PAIR_PEERS, SMALL_PEERS = 1, 7


def _row_tile(k):
    for t in (256, 240, 128, 176, 64, 32, 16):
        if k % t == 0:
            return t
    raise ValueError(k)


def _pair_sum(c_idx, g, got, name):
    _, k, n = g.shape
    kh = k // 2
    tm = _row_tile(kh)
    nb = kh // tm

    def body(c_ref, g_ref, r_ref, o_ref):
        o_ref[...] = (g_ref[...].astype(F32) + r_ref[...].astype(F32)).astype(BF16)

    return pl.pallas_call(
        body, name=name,
        grid_spec=pltpu.PrefetchScalarGridSpec(
            num_scalar_prefetch=1, grid=(nb,),
            in_specs=[pl.BlockSpec((N_CHIPS, tm, n), lambda i, c_ref: (0, c_ref[0] * nb + i, 0)),
                      pl.BlockSpec((N_CHIPS, tm, n), lambda i, c_ref: (0, i, 0))],
            out_specs=pl.BlockSpec((N_CHIPS, tm, n), lambda i, c_ref: (0, i, 0))),
        out_shape=jax.ShapeDtypeStruct((N_CHIPS, kh, n), BF16),
        compiler_params=_params("parallel"),
    )(c_idx, g, got)


def _chip_sum(s_idx, mine, got, name):
    _, kh, n = mine.shape
    tm = _row_tile(kh)

    def body(s_ref, m_ref, r_ref, o_ref):
        acc = m_ref[0].astype(F32)
        for j in range(3):
            acc = acc + r_ref[j].astype(F32)
        o_ref[...] = acc

    return pl.pallas_call(
        body, name=name,
        grid_spec=pltpu.PrefetchScalarGridSpec(
            num_scalar_prefetch=1, grid=(kh // tm,),
            in_specs=[pl.BlockSpec((1, tm, n), lambda i, s_ref: (s_ref[0], i, 0)),
                      pl.BlockSpec((3, tm, n), lambda i, s_ref: (0, i, 0))],
            out_specs=pl.BlockSpec((tm, n), lambda i, s_ref: (i, 0))),
        out_shape=jax.ShapeDtypeStruct((kh, n), F32),
        compiler_params=_params("parallel"),
    )(s_idx, mine, got)


def _adamw_math(w, g, m, v):
    m = ADAM_B1 * m + (1.0 - ADAM_B1) * g
    v = ADAM_B2 * v + (1.0 - ADAM_B2) * (g * g)
    m_hat = m / (1.0 - ADAM_B1 ** ADAM_STEP)
    v_hat = v / (1.0 - ADAM_B2 ** ADAM_STEP)
    delta = -ADAM_LR * (m_hat / (jnp.sqrt(v_hat) + ADAM_EPS) + ADAM_WD * w)
    return delta, m, v


def _adamw(c_idx, w, g_mine, g_other, m, v, name):
    k, n = w.shape
    tm = k // 4

    def body(c_ref, w_ref, gm_ref, go_ref, m_ref, v_ref, g_ref, d_ref, mo_ref, vo_ref):
        g = jnp.where(pl.program_id(0) == c_ref[0], gm_ref[...], go_ref[...])
        d, mm, vv = _adamw_math(w_ref[...], g, m_ref[...], v_ref[...])
        g_ref[...] = g
        d_ref[...] = d
        mo_ref[...] = mm
        vo_ref[...] = vv

    full = pl.BlockSpec((tm, n), lambda h, i, c_ref: (2 * h + i, 0))
    mine = pl.BlockSpec((tm, n), lambda h, i, c_ref: (jnp.where(h == c_ref[0], i, 0), 0))
    other = pl.BlockSpec((tm, n), lambda h, i, c_ref: (jnp.where(h == c_ref[0], 0, i), 0))
    shp = jax.ShapeDtypeStruct((k, n), F32)
    return pl.pallas_call(
        body, name=name,
        grid_spec=pltpu.PrefetchScalarGridSpec(
            num_scalar_prefetch=1, grid=(2, 2),
            in_specs=[full, mine, other, full, full], out_specs=[full] * 4),
        out_shape=[shp] * 4, compiler_params=_params("arbitrary", "arbitrary"),
    )(c_idx, w, g_mine, g_other, m, v)


VEC_SLOTS = {
    "g_mix_norm": (0, 0, D), "b_conv_proj": (0, D, D), "g_ffn_norm": (0, 2 * D, D),
    "g_final": (0, 3 * D, D), "b_in": (1, 0, INW), "conv_b": (2, 0, C), "ln_g": (2, C, C),
    "ln_b": (2, 2 * C, C), "sinks": (2, 3 * C, NQ), "loss": (2, 3 * C + 128, 1),
}
VEC_ROWS, VEC_COLS = 8, 4 * D
CW_ROWS = 32
SMALL_NAMES = ["g_mix_norm", "b_in", "sinks", "conv_w", "conv_b", "ln_g", "ln_b",
               "b_conv_proj", "g_ffn_norm", "g_final"]
CW_LANES = C // N_CHIPS


def _pack_small(gs, loss):
    row0 = jnp.concatenate([gs["g_mix_norm"], gs["b_conv_proj"], gs["g_ffn_norm"], gs["g_final"]], axis=1)
    row1 = jnp.pad(gs["b_in"], ((0, 0), (0, VEC_COLS - INW)))
    row2 = jnp.concatenate([gs["conv_b"], gs["ln_g"], gs["ln_b"],
                            jnp.pad(gs["sinks"], ((0, 0), (0, 128 - NQ))),
                            jnp.pad(loss.reshape(1, 1), ((0, 0), (0, VEC_COLS - 3 * C - 129)))], axis=1)
    vec = jnp.concatenate([row0, row1, row2, jnp.zeros((VEC_ROWS - 3, VEC_COLS), F32)], axis=0)
    cw = jnp.pad(gs["conv_w"], ((0, CW_ROWS - KW), (0, 0)))
    return vec, cw


def _small_update(idx, vec_own, cw_own, vec_all, cw_all, wmv):
    nsm = len(SMALL_NAMES)

    def body(s_ref, vown_ref, cown_ref, vec_ref, cw_ref, *refs):
        ins = refs[:3 * nsm]
        outs = refs[3 * nsm:7 * nsm]
        loss_ref = refs[7 * nsm]
        me = s_ref[1]

        def summed(own_ref, table_ref, rows_per_dev, r0, nrows, lane, width):
            acc = None
            for k in range(8):
                piece = jnp.where(me == k, own_ref[r0:r0 + nrows, lane:lane + width],
                                  table_ref[k * rows_per_dev + r0:k * rows_per_dev + r0 + nrows, lane:lane + width])
                acc = piece if acc is None else acc + piece
            return acc

        def total(slot):
            row, lane, width = slot
            return summed(vown_ref, vec_ref, VEC_ROWS, row, 1, lane, width)

        loss_ref[...] = jnp.broadcast_to(total(VEC_SLOTS["loss"]), loss_ref.shape)
        for p, name in enumerate(SMALL_NAMES):
            w_ref, m_ref, v_ref = ins[3 * p:3 * p + 3]
            g_ref, d_ref, mo_ref, vo_ref = outs[4 * p:4 * p + 4]
            if name == "conv_w":
                g = jnp.zeros((KW, CW_LANES), F32)
                for s in range(N_CHIPS):
                    cand = summed(cown_ref, cw_ref, CW_ROWS, 0, KW, s * CW_LANES, CW_LANES)
                    g = jnp.where(s_ref[0] == s, cand, g)
            else:
                g = total(VEC_SLOTS[name])
            d, mm, vv = _adamw_math(w_ref[...], g, m_ref[...], v_ref[...])
            g_ref[...] = g
            d_ref[...] = d
            mo_ref[...] = mm
            vo_ref[...] = vv

    vmem = pl.BlockSpec(memory_space=pltpu.VMEM)
    flat = [a for t in wmv for a in t]
    out_shape = []
    for w, _, _ in wmv:
        out_shape += [jax.ShapeDtypeStruct(w.shape, F32)] * 4
    out_shape.append(jax.ShapeDtypeStruct((1, 128), F32))
    res = pl.pallas_call(
        body, name="small_update",
        in_specs=[pl.BlockSpec(memory_space=pltpu.SMEM)] + [vmem] * (4 + len(flat)),
        out_specs=[vmem] * len(out_shape), out_shape=out_shape,
    )(idx, vec_own, cw_own, vec_all, cw_all, *flat)
    return [tuple(res[4 * p:4 * p + 4]) for p in range(nsm)], res[4 * nsm]


WEIGHT_ORDER = ["g_mix_norm", "w_in", "b_in", "sinks", "conv_w", "conv_b", "ln_g", "ln_b",
                "w_attn_proj", "w_conv_proj", "b_conv_proj", "w_out", "g_ffn_norm", "w_ffn_in",
                "w_ffn_down", "g_final"]


def kernel(x, g_mix_norm, w_in, b_in, sinks, conv_w, conv_b, ln_g, ln_b, w_attn_proj, w_conv_proj, b_conv_proj, w_out, g_ffn_norm, w_ffn_in, w_ffn_down, g_final, loss_target, m_g_mix_norm, m_w_in, m_b_in, m_sinks, m_conv_w, m_conv_b, m_ln_g, m_ln_b, m_w_attn_proj, m_w_conv_proj, m_b_conv_proj, m_w_out, m_g_ffn_norm, m_w_ffn_in, m_w_ffn_down, m_g_final, v_g_mix_norm, v_w_in, v_b_in, v_sinks, v_conv_w, v_conv_b, v_ln_g, v_ln_b, v_w_attn_proj, v_w_conv_proj, v_b_conv_proj, v_w_out, v_g_ffn_norm, v_w_ffn_in, v_w_ffn_down, v_g_final):
    w = dict(g_mix_norm=g_mix_norm, w_in=w_in, b_in=b_in, sinks=sinks, conv_w=conv_w, conv_b=conv_b,
             ln_g=ln_g, ln_b=ln_b, w_attn_proj=w_attn_proj, w_conv_proj=w_conv_proj,
             b_conv_proj=b_conv_proj, w_out=w_out, g_ffn_norm=g_ffn_norm, w_ffn_in=w_ffn_in,
             w_ffn_down=w_ffn_down, g_final=g_final)
    m = dict(g_mix_norm=m_g_mix_norm, w_in=m_w_in, b_in=m_b_in, sinks=m_sinks, conv_w=m_conv_w,
             conv_b=m_conv_b, ln_g=m_ln_g, ln_b=m_ln_b, w_attn_proj=m_w_attn_proj,
             w_conv_proj=m_w_conv_proj, b_conv_proj=m_b_conv_proj, w_out=m_w_out,
             g_ffn_norm=m_g_ffn_norm, w_ffn_in=m_w_ffn_in, w_ffn_down=m_w_ffn_down, g_final=m_g_final)
    v = dict(g_mix_norm=v_g_mix_norm, w_in=v_w_in, b_in=v_b_in, sinks=v_sinks, conv_w=v_conv_w,
             conv_b=v_conv_b, ln_g=v_ln_g, ln_b=v_ln_b, w_attn_proj=v_w_attn_proj,
             w_conv_proj=v_w_conv_proj, b_conv_proj=v_b_conv_proj, w_out=v_w_out,
             g_ffn_norm=v_g_ffn_norm, w_ffn_in=v_w_ffn_in, w_ffn_down=v_w_ffn_down, g_final=v_g_final)

    c_idx = lax.axis_index("c").astype(jnp.int32).reshape(1)
    s_idx = (2 * lax.axis_index("x") + lax.axis_index("y")).astype(jnp.int32).reshape(1)

    out_g, out_d, out_m, out_v = {}, {}, {}, {}

    def gather_start(tag, shards):
        lands = [lax.empty((N_CHIPS,) + s.shape, s.dtype) for s in shards]
        return _chip_start("gather_start_" + tag, _gather_ends, GATHER_PEERS, shards, lands)

    def gather_relay(tag, state, after, first=0, count=None):
        send_sems, recv_sems, shards, lands, _ = state
        last = len(shards) if count is None else first + count
        return _gather_relay("gather_relay_" + tag, send_sems, recv_sems, shards[first:last],
                             lands[first:last], after, first)

    def gather_finish(tag, relay, after):
        return _forward_wait("forward_wait_" + tag, relay[0], relay[1], relay[2], after)

    names_b = ["w_attn_proj", "w_conv_proj", "w_out", "w_ffn_in", "w_ffn_down"]
    big = {name: (w[name][0], m[name][0], v[name][0]) for name in names_b}
    big["w_in"] = (w_in[0].T, m_w_in[0].T, v_w_in[0].T)
    state_a = gather_start("a", [big["w_in"][0].astype(BF16), jnp.pad(conv_w[0], ((0, CW_ROWS - KW), (0, 0)))])
    state_b = gather_start("b", [(big[name][0] + state_a[4][0, 0]).astype(BF16) for name in names_b])
    got_a = gather_finish("a", gather_relay("a", state_a, [state_b[4]]), [])
    w_in_t_full = got_a[0].reshape(INW, D)
    conv_w_full = got_a[1].transpose(1, 0, 2).reshape(CW_ROWS, C)[:KW]

    xs, target = x[0], loss_target[0]
    g_final2 = g_final.reshape(1, D)
    h, qkv, glu, gl = _in_proj(xs, g_mix_norm, w_in_t_full, b_in)
    o, lse, probs = _attn_fwd(qkv, sinks)
    relay_1 = gather_relay("b1", state_b, [o], 0, 3)
    u, cact = _conv_fwd(glu, conv_w_full, conv_b, ln_g, ln_b, relay_1[3])
    w_ap4, w_cp4, w_out4 = gather_finish("b1", relay_1, [cact])
    w_out_full = w_out4.reshape(D, D)
    relay_2 = gather_relay("b2", state_b, [cact], 3, 1)
    ya, yc, mg, x1 = _mix_out(xs, o, cact, gl, w_ap4, w_cp4, b_conv_proj, w_out_full, relay_2[3])
    w_fi4, = gather_finish("b2", relay_2, [x1])
    relay_3 = gather_relay("b3", state_b, [x1], 4, 1)
    h2, gu, act = _ffn_in(x1, g_ffn_norm, w_fi4, relay_3[3])
    w_dn4, = gather_finish("b3", relay_3, [act])
    w_dn_full = w_dn4.reshape(DFF, D)
    dx2, dx2b, dg_final, loss_part = _ffn_out_loss(x1, act, w_dn_full, g_final2, target)

    def exchange_start(tag, grads):
        lands = [lax.empty((N_CHIPS, g.shape[1] // 2, g.shape[2]), g.dtype) for g in grads]
        return _chip_start("pair_start_" + tag, _exchange_ends, PAIR_PEERS, grads, lands)

    def reduce_start(tag, names, exchange, after):
        send_sems, recv_sems, grads, lands, _ = exchange
        grads, from_sibling = _chip_wait("pair_wait_" + tag, _exchange_ends, send_sems, recv_sems, grads, lands, after)
        pair = [_pair_sum(c_idx, g, r, "pair_sum_" + name) for name, g, r in zip(names, grads, from_sibling)]
        lands = [lax.empty((3,) + p.shape[1:], p.dtype) for p in pair]
        return _chip_start("chip_start_" + tag, _reduce_ends, REDUCE_PEERS, pair, lands)

    def reduce_sum(tag, names, state, after):
        send_sems, recv_sems, pair, lands, _ = state
        pair, lands = _chip_wait("chip_wait_" + tag, _reduce_ends, send_sems, recv_sems, pair, lands, after)
        mine = [_chip_sum(s_idx, p, r, "chip_sum_" + name) for name, p, r in zip(names, pair, lands)]
        others = [lax.empty(a.shape, a.dtype) for a in mine]
        return _chip_start("share_start_" + tag, _share_ends, PAIR_PEERS, mine, others)

    def reduce_finish(tag, names, share, after):
        send_sems, recv_sems, mine, others, _ = share
        mine, others = _chip_wait("share_wait_" + tag, _share_ends, send_sems, recv_sems, mine, others, after)
        for name, g_mine, g_other in zip(names, mine, others):
            wv, mv, vv = big[name]
            res = _adamw(c_idx, wv, g_mine, g_other, mv, vv, "adamw_" + name)
            if name == "w_in":
                res = [a.T for a in res]
            out_g[name], out_d[name], out_m[name], out_v[name] = [a[None] for a in res]

    dgu, dx1, dx1b, dg_ffn = _ffn_bwd(dx2, dx2b, gu, x1, g_ffn_norm, w_dn_full, w_fi4)
    names_1 = ["w_ffn_in", "w_ffn_down", "w_out", "w_attn_proj", "w_conv_proj"]
    grads_1 = [_grad_w(h2, dgu, "grad_w_ffn_in", 512, FSH, True),
               _grad_w(act, dx2b, "grad_w_ffn_down", 256, D, False)]
    dya, dyc, dgl, do, dc, db_cp = _mix_bwd(dx1b, gl, ya, yc, w_out_full, w_ap4, w_cp4, relay_3[3])
    grads_1 += [_grad_w(mg, dx1b, "grad_w_out", 512, D, False),
                _grad_w(o, dya, "grad_w_attn_proj", 512, D, True),
                _grad_w(cact, dyc, "grad_w_conv_proj", 512, D, True)]
    exchange_1 = exchange_start("1", grads_1)
    dglu, dconv_w, dconv_b, dln_g, dln_b = _conv_bwd(glu, u, dc, conv_w_full, ln_g, ln_b, exchange_1[4])
    state_1 = reduce_start("1", names_1, exchange_1, [dglu])
    dq, dkv, dsinks = _attn_bwd(qkv, o, do, lse, probs, sinks, state_1[4])
    names_2 = ["w_in"]
    gw_in_t = _grad_w_in_t(h, dq, dkv, dglu, dgl)
    exchange_2 = exchange_start("2", [gw_in_t.reshape(N_CHIPS, INW // N_CHIPS, D)])
    grad_x, dg_mix, db_in = _in_proj_bwd(dq, dkv, dglu, dgl, xs, dx1, g_mix_norm, w_in_t_full, exchange_2[4])

    gs = {"g_mix_norm": dg_mix, "b_in": db_in, "sinks": dsinks[:, 0].reshape(1, NQ),
          "conv_w": dconv_w, "conv_b": dconv_b, "ln_g": dln_g, "ln_b": dln_b,
          "b_conv_proj": db_cp, "g_ffn_norm": dg_ffn, "g_final": dg_final}
    blocks = list(_pack_small(gs, loss_part[0, 0]))
    tables = [lax.empty((8 * b.shape[0], b.shape[1]), b.dtype) for b in blocks]
    small = _chip_start("small_start", _small_ends, SMALL_PEERS, blocks, tables)

    state_2 = reduce_start("2", names_2, exchange_2, [grad_x, small[4]])
    share_1 = reduce_sum("1", names_1, state_1, [state_2[4]])
    blocks, tables = _chip_wait("small_wait", _small_ends, small[0], small[1], small[2], small[3],
                                [share_1[4]])
    me = (4 * lax.axis_index("x") + 2 * lax.axis_index("y") + lax.axis_index("c")).astype(jnp.int32)

    def view(a, name):
        if name == "conv_w":
            return a[0]
        if name == "g_final":
            return a.reshape(1, D)
        return a

    wmv = [(view(w[name], name), view(m[name], name), view(v[name], name)) for name in SMALL_NAMES]
    small_out, loss_row = _small_update(jnp.concatenate([s_idx, me.reshape(1)]), blocks[0], blocks[1],
                                        tables[0], tables[1], wmv)
    for name, (g, d, mm, vv) in zip(SMALL_NAMES, small_out):
        shape = w[name].shape
        out_g[name], out_d[name], out_m[name], out_v[name] = (
            g.reshape(shape), d.reshape(shape), mm.reshape(shape), vv.reshape(shape))

    reduce_finish("1", names_1, share_1, [loss_row])
    share_2 = reduce_sum("2", names_2, state_2, [out_d["w_conv_proj"]])
    reduce_finish("2", names_2, share_2, [])

    loss = loss_row[0, 0]
    return (loss, grad_x[None], *[out_g[k] for k in WEIGHT_ORDER], *[out_d[k] for k in WEIGHT_ORDER],
            *[out_m[k] for k in WEIGHT_ORDER], *[out_v[k] for k in WEIGHT_ORDER])
```

```python
import functools

import jax
import jax.numpy as jnp
from jax import lax
from jax.experimental import pallas as pl
from jax.experimental.pallas import tpu as pltpu

F32 = jnp.float32
BF16 = jnp.bfloat16

T = 2048
D = 1024
HD = 64
NQ = 8
NKV = 2
GROUP = NQ // NKV
BLK = 128
AW = NQ * HD
KVW = NKV * HD
C = 512
KW = 31
QKVW = AW + 2 * KVW
GLU_OFF = QKVW
GATE_OFF = GLU_OFF + 2 * C
INW = GATE_OFF + 2 * D
DFF = 2816
EPS = 1e-5
NEG = -1e30
SCALE = HD ** -0.5
HALO = 32
N_CHIPS = 4
FSH = 2 * DFF // N_CHIPS

ADAM_LR = 0.001
ADAM_B1 = 0.9
ADAM_B2 = 0.999
ADAM_EPS = 1e-08
ADAM_WD = 0.01
ADAM_STEP = 10

VMEM_LIMIT = 56 * 1024 * 1024
ROW_TM = 512
MESH = pl.DeviceIdType.MESH


def _params(*sem):
    return pltpu.CompilerParams(dimension_semantics=sem, vmem_limit_bytes=VMEM_LIMIT)


def _dot(a, b):
    return jnp.dot(a, b, preferred_element_type=F32)


def _dot_nt(a, b):
    return lax.dot_general(a, b, (((1,), (1,)), ((), ())), preferred_element_type=F32)


def _dot_tn(a, b):
    return lax.dot_general(a, b, (((0,), (0,)), ((), ())), preferred_element_type=F32)


def _sigmoid(v):
    return 1.0 / (1.0 + jnp.exp(-v))


def _rows(tm, n):
    return pl.BlockSpec((tm, n), lambda i: (i, 0))


def _whole(shape):
    return pl.BlockSpec(shape, lambda i: tuple(0 for _ in shape))


def _in_proj(x, g_mix, w_in_t, b_in):
    tm = ROW_TM

    def body(x_ref, g_ref, w_ref, b_ref, h_ref, qkv_ref, glu_ref, gl_ref):
        xv = x_ref[...]
        r = lax.rsqrt(jnp.mean(xv * xv, axis=-1, keepdims=True) + EPS)
        h = (xv * r * g_ref[...]).astype(BF16)
        h_ref[...] = h
        qkv_ref[...] = (_dot_nt(h, w_ref[0:GLU_OFF, :]) + b_ref[:, 0:GLU_OFF]).astype(BF16)
        glu_ref[...] = (_dot_nt(h, w_ref[GLU_OFF:GATE_OFF, :]) + b_ref[:, GLU_OFF:GATE_OFF]).astype(BF16)
        gl_ref[...] = (_dot_nt(h, w_ref[GATE_OFF:INW, :]) + b_ref[:, GATE_OFF:INW]).astype(BF16)

    return pl.pallas_call(
        body, name="in_proj", grid=(T // tm,),
        in_specs=[_rows(tm, D), _whole((1, D)), _whole((INW, D)), _whole((1, INW))],
        out_specs=[_rows(tm, D), _rows(tm, QKVW), _rows(tm, 2 * C), _rows(tm, 2 * D)],
        out_shape=[jax.ShapeDtypeStruct((T, D), BF16), jax.ShapeDtypeStruct((T, QKVW), BF16),
                   jax.ShapeDtypeStruct((T, 2 * C), BF16), jax.ShapeDtypeStruct((T, 2 * D), BF16)],
        compiler_params=_params("parallel"),
    )(x, g_mix, w_in_t, b_in)


GROWS = GROUP * BLK
BAND = 2 * BLK
ATT_SUB = 2


def _band(i):
    rb = pl.multiple_of(jnp.maximum(i - 1, 0) * BLK, BLK)
    row = lax.broadcasted_iota(jnp.int32, (GROWS, BAND), 0)
    kpos = rb + lax.broadcasted_iota(jnp.int32, (GROWS, BAND), 1)
    qpos = i * BLK + jnp.bitwise_and(row, BLK - 1)
    return rb, jnp.logical_and(kpos <= qpos, kpos > qpos - BLK)


def _sink_column(sink_ref, g):
    head = lax.shift_right_logical(lax.broadcasted_iota(jnp.int32, (GROWS, 1), 0), 7)
    col = jnp.full((GROWS, 1), sink_ref[0, g * GROUP], F32)
    for hh in range(1, GROUP):
        col = jnp.where(head == hh, sink_ref[0, g * GROUP + hh], col)
    return col


def _attn_fwd(qkv, sinks):
    def body(sink_ref, qkv_ref, o_ref, lse_ref, p_ref, s_ref):
        slots = [(sub, g) for sub in range(ATT_SUB) for g in range(NKV)]
        bands = [_band(pl.program_id(0) * ATT_SUB + sub) for sub in range(ATT_SUB)]
        for n, (sub, g) in enumerate(slots):
            rb = bands[sub][0]
            r0 = pl.multiple_of((pl.program_id(0) * ATT_SUB + sub) * BLK, BLK)
            kband = qkv_ref[pl.ds(rb, BAND), AW + g * HD:AW + (g + 1) * HD]
            for hh in range(GROUP):
                h = g * GROUP + hh
                s_ref[n, hh * BLK:(hh + 1) * BLK, :] = _dot_nt(
                    qkv_ref[pl.ds(r0, BLK), h * HD:(h + 1) * HD], kband)
        lses = []
        for n, (sub, g) in enumerate(slots):
            s = jnp.where(bands[sub][1], s_ref[n] * SCALE, NEG)
            sink = _sink_column(sink_ref, g)
            m = jnp.maximum(jnp.max(s, axis=-1, keepdims=True), sink)
            p = jnp.exp(s - m)
            den = jnp.sum(p, axis=-1, keepdims=True) + jnp.exp(sink - m)
            p_ref[n] = (p * (1.0 / den)).astype(BF16)
            lses.append(m + jnp.log(den))
        for n, (sub, g) in enumerate(slots):
            rb = bands[sub][0]
            rows = slice(sub * BLK, (sub + 1) * BLK)
            vband = qkv_ref[pl.ds(rb, BAND), AW + KVW + g * HD:AW + KVW + (g + 1) * HD]
            for hh in range(GROUP):
                h = g * GROUP + hh
                o_ref[rows, h * HD:(h + 1) * HD] = _dot(p_ref[n, hh * BLK:(hh + 1) * BLK, :], vband).astype(BF16)
                lse_ref[rows, h:h + 1] = lses[n][hh * BLK:(hh + 1) * BLK]

    nslot = ATT_SUB * NKV
    return pl.pallas_call(
        body, name="attn_fwd", grid=(T // (ATT_SUB * BLK),),
        in_specs=[pl.BlockSpec(memory_space=pltpu.SMEM), _whole((T, QKVW))],
        out_specs=[_rows(ATT_SUB * BLK, AW), _rows(ATT_SUB * BLK, NQ),
                   pl.BlockSpec((nslot, GROWS, BAND), lambda i: (i, 0, 0))],
        out_shape=[jax.ShapeDtypeStruct((T, AW), BF16), jax.ShapeDtypeStruct((T, NQ), F32),
                   jax.ShapeDtypeStruct((T // BLK * NKV, GROWS, BAND), BF16)],
        scratch_shapes=[pltpu.VMEM((nslot, GROWS, BAND), F32)],
        compiler_params=_params("parallel"),
    )(sinks, qkv)


CONV_TM = 256
CONV_SUB = 32


def _glu(ab):
    a = ab[:, 0:C].astype(F32)
    b = ab[:, C:2 * C].astype(F32)
    return a * _sigmoid(b)


SUBLANES = 8


def _shifted_copies(ref):
    rows = ref.shape[1] - SUBLANES
    for r in range(1, SUBLANES):
        ref[r, 0:rows, :] = ref[0, r:r + rows, :]


def _shifted_rows(ref, start, size):
    r = start % SUBLANES
    return ref[r, start - r:start - r + size, :]


def _conv_fwd(glu, conv_w, conv_b, ln_g, ln_b, dep):
    tm = CONV_TM

    def body(cur_ref, prev_ref, w_ref, cb_ref, g_ref, b_ref, dep_ref, u_ref, c_ref, zs_ref):
        i = pl.program_id(0)
        zprev = _glu(prev_ref[tm - HALO:tm, :])
        zs_ref[0, 0:HALO, :] = jnp.where(i > 0, zprev, 0.0)
        zs_ref[0, HALO:HALO + tm, :] = _glu(cur_ref[...])
        _shifted_copies(zs_ref)
        for s in range(tm // CONV_SUB):
            base = HALO + s * CONV_SUB - (KW - 1)
            acc = jnp.broadcast_to(cb_ref[...], (CONV_SUB, C))
            for j in range(KW):
                acc = acc + w_ref[j:j + 1, :] * _shifted_rows(zs_ref, base + j, CONV_SUB)
            rows = slice(s * CONV_SUB, (s + 1) * CONV_SUB)
            u_ref[rows, :] = acc
            mu = jnp.mean(acc, axis=-1, keepdims=True)
            xc = acc - mu
            var = jnp.mean(xc * xc, axis=-1, keepdims=True)
            y = xc * lax.rsqrt(var + EPS) * g_ref[...] + b_ref[...]
            c_ref[rows, :] = (y * _sigmoid(y)).astype(BF16)

    return pl.pallas_call(
        body, name="conv_fwd", grid=(T // tm,),
        in_specs=[_rows(tm, 2 * C),
                  pl.BlockSpec((tm, 2 * C), lambda i: (jnp.maximum(i - 1, 0), 0)),
                  _whole((KW, C)), _whole((1, C)), _whole((1, C)), _whole((1, C)), _whole((8, 128))],
        out_specs=[_rows(tm, C), _rows(tm, C)],
        out_shape=[jax.ShapeDtypeStruct((T, C), F32), jax.ShapeDtypeStruct((T, C), BF16)],
        scratch_shapes=[pltpu.VMEM((SUBLANES, HALO + tm, C), F32)],
        compiler_params=_params("parallel"),
    )(glu, glu, conv_w, conv_b, ln_g, ln_b, dep)


def _branch_outputs(o, cact, wap_ref, wcp_ref, bcp_ref):
    ya = jnp.concatenate([_dot(o, wap_ref[s]) for s in range(N_CHIPS)], axis=1)
    yc = jnp.concatenate([_dot(cact, wcp_ref[s]) for s in range(N_CHIPS)], axis=1) + bcp_ref[...]
    return ya, yc


def _mix_out(x, o, cact, gl, w_ap, w_cp, b_cp, w_out, dep, begun=None):
    tm = ROW_TM
    steps = T // tm // 2
    first = 0 if begun is None else steps
    rows = lambda n: pl.BlockSpec((tm, n), lambda i: (i + first, 0))
    extra = [] if begun is None else list(begun)

    def body(x_ref, o_ref, c_ref, gl_ref, wap_ref, wcp_ref, bcp_ref, wo_ref, dep_ref, *rest):
        ya_ref, yc_ref, mg_ref, x1_ref = rest[len(extra):]
        ya, yc = _branch_outputs(o_ref[...], c_ref[...], wap_ref, wcp_ref, bcp_ref)
        g0 = _sigmoid(gl_ref[:, 0:D].astype(F32))
        g1 = _sigmoid(gl_ref[:, D:2 * D].astype(F32))
        mg = (g0 * ya + g1 * yc).astype(BF16)
        ya_ref[...] = ya.astype(BF16)
        yc_ref[...] = yc.astype(BF16)
        mg_ref[...] = mg
        x1_ref[...] = x_ref[...] + _dot(mg, wo_ref[...])

    return pl.pallas_call(
        body, name="mix_out_first" if begun is None else "mix_out_second", grid=(steps,),
        in_specs=[rows(D), rows(AW), rows(C), rows(2 * D),
                  _whole((N_CHIPS, AW, D // N_CHIPS)), _whole((N_CHIPS, C, D // N_CHIPS)), _whole((1, D)),
                  _whole((D, D)), _whole((8, 128))] + [ANY_SPEC] * len(extra),
        out_specs=[rows(D), rows(D), rows(D), rows(D)],
        out_shape=[jax.ShapeDtypeStruct((T, D), BF16), jax.ShapeDtypeStruct((T, D), BF16),
                   jax.ShapeDtypeStruct((T, D), BF16), jax.ShapeDtypeStruct((T, D), F32)],
        input_output_aliases={9 + k: k for k in range(len(extra))},
        compiler_params=_params("parallel"),
    )(x, o, cact, gl, w_ap, w_cp, b_cp, w_out, dep, *extra)


def _swiglu_half(h, wg_ref, wu_ref, gu_ref, act_ref):
    gate = _dot(h, wg_ref[0])
    up = _dot(h, wu_ref[0])
    gu_ref[:, 0:FSH] = gate.astype(BF16)
    gu_ref[:, FSH:2 * FSH] = up.astype(BF16)
    act_ref[...] = (gate * _sigmoid(gate) * up).astype(BF16)


def _shard_spec(k):
    return pl.BlockSpec((1, D, FSH), lambda i: (k, 0, 0), pipeline_mode=pl.Buffered(1))


def _ffn_in_first(x1, g_ffn, w_fi, dep):
    tm = ROW_TM

    def body(x_ref, g_ref, wg_ref, wu_ref, dep_ref, h_ref, gu_ref, act_ref):
        xv = x_ref[...]
        r = lax.rsqrt(jnp.mean(xv * xv, axis=-1, keepdims=True) + EPS)
        h = (xv * r * g_ref[...]).astype(BF16)
        h_ref[...] = h
        _swiglu_half(h, wg_ref, wu_ref, gu_ref, act_ref)

    return pl.pallas_call(
        body, name="ffn_in_first", grid=(T // tm,),
        in_specs=[_rows(tm, D), _whole((1, D)), _shard_spec(0), _shard_spec(2), _whole((8, 128))],
        out_specs=[_rows(tm, D), pl.BlockSpec((tm, 2 * FSH), lambda i: (i, 0)),
                   pl.BlockSpec((tm, FSH), lambda i: (i, 0))],
        out_shape=[jax.ShapeDtypeStruct((T, D), BF16), jax.ShapeDtypeStruct((T, 2 * DFF), BF16),
                   jax.ShapeDtypeStruct((T, DFF), BF16)],
        compiler_params=_params("parallel"),
    )(x1, g_ffn, w_fi, w_fi, dep)


def _ffn_in_second(h2, w_fi, gu, act, dep):
    tm = ROW_TM

    def body(h_ref, wg_ref, wu_ref, gu_in, act_in, dep_ref, gu_ref, act_ref):
        _swiglu_half(h_ref[...], wg_ref, wu_ref, gu_ref, act_ref)

    return pl.pallas_call(
        body, name="ffn_in_second", grid=(T // tm,),
        in_specs=[_rows(tm, D), _shard_spec(1), _shard_spec(3), ANY_SPEC, ANY_SPEC, _whole((8, 128))],
        out_specs=[pl.BlockSpec((tm, 2 * FSH), lambda i: (i, 1)), pl.BlockSpec((tm, FSH), lambda i: (i, 1))],
        out_shape=[jax.ShapeDtypeStruct((T, 2 * DFF), BF16), jax.ShapeDtypeStruct((T, DFF), BF16)],
        input_output_aliases={3: 0, 4: 1},
        compiler_params=_params("parallel"),
    )(h2, w_fi, w_fi, gu, act, dep)


def _ffn_out_loss(x1, act, w_dn, g_final, target):
    tm = ROW_TM

    def body(x_ref, a_ref, w_ref, g_ref, t_ref, dx_ref, dxb_ref, dg_ref, loss_ref):
        i = pl.program_id(0)
        x2 = x_ref[...] + _dot(a_ref[...], w_ref[...])
        r = lax.rsqrt(jnp.mean(x2 * x2, axis=-1, keepdims=True) + EPS)
        xh = x2 * r
        g = g_ref[...]
        err = xh * g - t_ref[...]
        dy = err * (1.0 / D)
        dyg = dy * g
        dx = r * (dyg - xh * jnp.mean(dyg * xh, axis=-1, keepdims=True))
        dx_ref[...] = dx
        dxb_ref[...] = dx.astype(BF16)
        part = 0.5 * jnp.sum(jnp.mean(err * err, axis=-1, keepdims=True), axis=0, keepdims=True)

        @pl.when(i == 0)
        def _():
            dg_ref[...] = jnp.zeros_like(dg_ref)
            loss_ref[...] = jnp.zeros_like(loss_ref)

        dg_ref[...] += jnp.sum(dy * xh, axis=0, keepdims=True)
        loss_ref[...] += jnp.broadcast_to(part, loss_ref.shape)

    return pl.pallas_call(
        body, name="ffn_out_loss", grid=(T // tm,),
        in_specs=[_rows(tm, D), _rows(tm, DFF), _whole((DFF, D)), _whole((1, D)), _rows(tm, D)],
        out_specs=[_rows(tm, D), _rows(tm, D), _whole((1, D)), _whole((1, 128))],
        out_shape=[jax.ShapeDtypeStruct((T, D), F32), jax.ShapeDtypeStruct((T, D), BF16),
                   jax.ShapeDtypeStruct((1, D), F32), jax.ShapeDtypeStruct((1, 128), F32)],
        compiler_params=_params("arbitrary"),
    )(x1, act, w_dn, g_final, target)


def _const(shape):
    return pl.BlockSpec(shape, lambda i: tuple(0 for _ in shape), pipeline_mode=pl.Buffered(1))


def _ffn_bwd(dx2, dx2b, gu, x1, g_ffn, w_dn_t, w_fi_t):
    tm = ROW_TM // 2

    def body(dx_ref, dxb_ref, gu_ref, x_ref, g_ref, wdn_ref, wfi_ref,
             dgu_ref, dx1_ref, dx1b_ref, dg_ref):
        i = pl.program_id(0)
        dxb = dxb_ref[...]
        dh = jnp.zeros((tm, D), F32)
        for k in range(N_CHIPS // 2):
            c0 = k * FSH
            dact = _dot_nt(dxb, wdn_ref[c0:c0 + FSH, :])
            gate = gu_ref[:, 2 * c0:2 * c0 + FSH].astype(F32)
            up = gu_ref[:, 2 * c0 + FSH:2 * c0 + 2 * FSH].astype(F32)
            s = _sigmoid(gate)
            dup = (dact * gate * s).astype(BF16)
            dgate = (dact * up * s * (1.0 + gate * (1.0 - s))).astype(BF16)
            dgu_ref[:, c0:c0 + FSH] = dgate
            dgu_ref[:, DFF + c0:DFF + c0 + FSH] = dup
            dh = dh + _dot_nt(dgate, wfi_ref[k]) + _dot_nt(dup, wfi_ref[k + N_CHIPS // 2])
        xv = x_ref[...]
        r = lax.rsqrt(jnp.mean(xv * xv, axis=-1, keepdims=True) + EPS)
        xh = xv * r
        dhg = dh * g_ref[...]
        dx1 = dx_ref[...] + r * (dhg - xh * jnp.mean(dhg * xh, axis=-1, keepdims=True))
        dx1_ref[...] = dx1
        dx1b_ref[...] = dx1.astype(BF16)

        @pl.when(i == 0)
        def _():
            dg_ref[...] = jnp.zeros_like(dg_ref)

        dg_ref[...] += jnp.sum(dh * xh, axis=0, keepdims=True)

    return pl.pallas_call(
        body, name="ffn_bwd", grid=(T // tm,),
        in_specs=[_rows(tm, D), _rows(tm, D), _rows(tm, 2 * DFF), _rows(tm, D), _whole((1, D)),
                  _const((DFF, D)), _const((N_CHIPS, D, FSH))],
        out_specs=[_rows(tm, 2 * DFF), _rows(tm, D), _rows(tm, D), _whole((1, D))],
        out_shape=[jax.ShapeDtypeStruct((T, 2 * DFF), BF16), jax.ShapeDtypeStruct((T, D), F32),
                   jax.ShapeDtypeStruct((T, D), BF16), jax.ShapeDtypeStruct((1, D), F32)],
        compiler_params=_params("arbitrary"),
    )(dx2, dx2b, gu, x1, g_ffn, w_dn_t, w_fi_t)


def _mix_bwd(dx1b, gl, ya, yc, w_out, w_ap, w_cp, dep):
    tm = ROW_TM

    def body(dx_ref, gl_ref, ya_ref, yc_ref, wo_ref, wap_ref, wcp_ref, dep_ref,
             dya_ref, dyc_ref, dgl_ref, do_ref, dc_ref, db_ref):
        i = pl.program_id(0)
        dm = _dot_nt(dx_ref[...], wo_ref[...])
        ya, yc = ya_ref[...].astype(F32), yc_ref[...].astype(F32)
        g0 = _sigmoid(gl_ref[:, 0:D].astype(F32))
        g1 = _sigmoid(gl_ref[:, D:2 * D].astype(F32))
        dya = dm * g0
        dyc = dm * g1
        dgl_ref[:, 0:D] = (dya * ya * (1.0 - g0)).astype(BF16)
        dgl_ref[:, D:2 * D] = (dyc * yc * (1.0 - g1)).astype(BF16)
        dyab = dya.astype(BF16)
        dycb = dyc.astype(BF16)
        dya_ref[...] = dyab
        dyc_ref[...] = dycb
        sw = D // N_CHIPS
        do = jnp.zeros((tm, AW), F32)
        dcv = jnp.zeros((tm, C), F32)
        for s in range(N_CHIPS):
            do = do + _dot_nt(dyab[:, s * sw:(s + 1) * sw], wap_ref[s])
            dcv = dcv + _dot_nt(dycb[:, s * sw:(s + 1) * sw], wcp_ref[s])
        do_ref[...] = do.astype(BF16)
        dc_ref[...] = dcv.astype(BF16)

        @pl.when(i == 0)
        def _():
            db_ref[...] = jnp.zeros_like(db_ref)

        db_ref[...] += jnp.sum(dyc, axis=0, keepdims=True)

    return pl.pallas_call(
        body, name="mix_bwd", grid=(T // tm,),
        in_specs=[_rows(tm, D), _rows(tm, 2 * D), _rows(tm, D), _rows(tm, D),
                  _whole((D, D)), _whole((N_CHIPS, AW, D // N_CHIPS)), _whole((N_CHIPS, C, D // N_CHIPS)),
                  _whole((8, 128))],
        out_specs=[_rows(tm, D), _rows(tm, D), _rows(tm, 2 * D), _rows(tm, AW), _rows(tm, C),
                   _whole((1, D))],
        out_shape=[jax.ShapeDtypeStruct((T, D), BF16), jax.ShapeDtypeStruct((T, D), BF16),
                   jax.ShapeDtypeStruct((T, 2 * D), BF16), jax.ShapeDtypeStruct((T, AW), BF16),
                   jax.ShapeDtypeStruct((T, C), BF16), jax.ShapeDtypeStruct((1, D), F32)],
        compiler_params=_params("arbitrary"),
    )(dx1b, gl, ya, yc, w_out, w_ap, w_cp, dep)


def _conv_bwd(glu, u, dc, conv_w, ln_g, ln_b, dep):
    tm = CONV_TM
    nblk = T // tm

    def du_of(uv, dcv, g_ref, b_ref):
        mu = jnp.mean(uv, axis=-1, keepdims=True)
        xc = uv - mu
        var = jnp.mean(xc * xc, axis=-1, keepdims=True)
        rstd = lax.rsqrt(var + EPS)
        xh = xc * rstd
        y = xh * g_ref[...] + b_ref[...]
        sg = _sigmoid(y)
        dy = dcv * (sg * (1.0 + y * (1.0 - sg)))
        dxh = dy * g_ref[...]
        du = rstd * (dxh - jnp.mean(dxh, axis=-1, keepdims=True)
                     - xh * jnp.mean(dxh * xh, axis=-1, keepdims=True))
        return du, dy, xh

    def body(cur_ref, prev_ref, u_ref, un_ref, dc_ref, dcn_ref, w_ref, g_ref, b_ref, dep_ref,
             dglu_ref, dw_ref, dcb_ref, dg_ref, db_ref, zs_ref, dus_ref):
        i = pl.program_id(0)

        @pl.when(i == 0)
        def _():
            dw_ref[...] = jnp.zeros_like(dw_ref)
            dcb_ref[...] = jnp.zeros_like(dcb_ref)
            dg_ref[...] = jnp.zeros_like(dg_ref)
            db_ref[...] = jnp.zeros_like(db_ref)

        zprev = _glu(prev_ref[tm - HALO:tm, :])
        zs_ref[0, 0:HALO, :] = jnp.where(i > 0, zprev, 0.0)
        zs_ref[0, HALO:HALO + tm, :] = _glu(cur_ref[...])
        _shifted_copies(zs_ref)

        dun, _, _ = du_of(un_ref[0:HALO, :], dcn_ref[0:HALO, :].astype(F32), g_ref, b_ref)
        dus_ref[0, tm:tm + HALO, :] = jnp.where(i < nblk - 1, dun, 0.0)
        dg_acc = jnp.zeros((1, C), F32)
        db_acc = jnp.zeros((1, C), F32)
        dcb_acc = jnp.zeros((1, C), F32)
        for s in range(tm // CONV_SUB):
            rows = slice(s * CONV_SUB, (s + 1) * CONV_SUB)
            du, dy, xh = du_of(u_ref[rows, :], dc_ref[rows, :].astype(F32), g_ref, b_ref)
            dus_ref[0, rows, :] = du
            dg_acc = dg_acc + jnp.sum(dy * xh, axis=0, keepdims=True)
            db_acc = db_acc + jnp.sum(dy, axis=0, keepdims=True)
            dcb_acc = dcb_acc + jnp.sum(du, axis=0, keepdims=True)
        dg_ref[...] += dg_acc
        db_ref[...] += db_acc
        dcb_ref[...] += dcb_acc
        _shifted_copies(dus_ref)

        for j in range(KW):
            acc = jnp.zeros((CONV_SUB, C), F32)
            for s in range(tm // CONV_SUB):
                base = HALO + s * CONV_SUB - (KW - 1) + j
                acc = acc + dus_ref[0, s * CONV_SUB:(s + 1) * CONV_SUB, :] * _shifted_rows(zs_ref, base, CONV_SUB)
            dw_ref[j:j + 1, :] += jnp.sum(acc, axis=0, keepdims=True)

        for s in range(tm // CONV_SUB):
            rows = slice(s * CONV_SUB, (s + 1) * CONV_SUB)
            dz = jnp.zeros((CONV_SUB, C), F32)
            for j in range(KW):
                dz = dz + w_ref[j:j + 1, :] * _shifted_rows(dus_ref, s * CONV_SUB + (KW - 1) - j, CONV_SUB)
            a = cur_ref[rows, 0:C].astype(F32)
            sb = _sigmoid(cur_ref[rows, C:2 * C].astype(F32))
            dglu_ref[rows, 0:C] = (dz * sb).astype(BF16)
            dglu_ref[rows, C:2 * C] = (dz * a * sb * (1.0 - sb)).astype(BF16)

    nxt = lambda i: (jnp.minimum(i + 1, nblk - 1), 0)
    return pl.pallas_call(
        body, name="conv_bwd", grid=(nblk,),
        in_specs=[_rows(tm, 2 * C),
                  pl.BlockSpec((tm, 2 * C), lambda i: (jnp.maximum(i - 1, 0), 0)),
                  _rows(tm, C), pl.BlockSpec((tm, C), nxt),
                  _rows(tm, C), pl.BlockSpec((tm, C), nxt),
                  _whole((KW, C)), _whole((1, C)), _whole((1, C)), _whole((8, 128))],
        out_specs=[_rows(tm, 2 * C), _whole((KW, C)), _whole((1, C)), _whole((1, C)), _whole((1, C))],
        out_shape=[jax.ShapeDtypeStruct((T, 2 * C), BF16), jax.ShapeDtypeStruct((KW, C), F32),
                   jax.ShapeDtypeStruct((1, C), F32), jax.ShapeDtypeStruct((1, C), F32),
                   jax.ShapeDtypeStruct((1, C), F32)],
        scratch_shapes=[pltpu.VMEM((SUBLANES, HALO + tm, C), F32), pltpu.VMEM((SUBLANES, tm + HALO, C), F32)],
        compiler_params=_params("arbitrary"),
    )(glu, glu, u, u, dc, dc, conv_w, ln_g, ln_b, dep)


def _attn_bwd(qkv, o, do, lse, p, sinks, dep):
    nsub = 1

    def body(sink_ref, qkv_ref, o_ref, do_ref, lse_ref, p_ref, dep_ref, dq_ref, dkv_ref, ds_ref,
             dp_ref, dsb_ref):
        i = pl.program_id(0)

        @pl.when(i == 0)
        def _():
            dkv_ref[...] = jnp.zeros_like(dkv_ref)
            ds_ref[...] = jnp.zeros_like(ds_ref)

        slots = [(sub, g) for sub in range(nsub) for g in range(NKV)]
        bands = [_band(i * nsub + sub) for sub in range(nsub)]
        r0s = [pl.multiple_of((i * nsub + sub) * BLK, BLK) for sub in range(nsub)]
        lses, dls = [], []
        for n, (sub, g) in enumerate(slots):
            rb, blk = bands[sub][0], slice(sub * BLK, (sub + 1) * BLK)
            vband = qkv_ref[pl.ds(rb, BAND), AW + KVW + g * HD:AW + KVW + (g + 1) * HD]
            lse_parts, dl_parts = [], []
            for hh in range(GROUP):
                h = g * GROUP + hh
                hcol = slice(h * HD, (h + 1) * HD)
                doh = do_ref[blk, hcol]
                dp_ref[n, hh * BLK:(hh + 1) * BLK, :] = _dot_nt(doh, vband)
                lse_parts.append(lse_ref[blk, h:h + 1])
                dl_parts.append(jnp.sum(doh.astype(F32) * o_ref[blk, hcol].astype(F32), axis=-1, keepdims=True))
            lses.append(jnp.concatenate(lse_parts, axis=0))
            dls.append(jnp.concatenate(dl_parts, axis=0))
        dsinks = []
        for n, (sub, g) in enumerate(slots):
            dsb_ref[n] = (p_ref[n].astype(F32) * (dp_ref[n] - dls[n])).astype(BF16)
            dsinks.append(-(jnp.exp(_sink_column(sink_ref, g) - lses[n]) * dls[n]))
        for n, (sub, g) in enumerate(slots):
            rb, blk = bands[sub][0], slice(sub * BLK, (sub + 1) * BLK)
            kband = qkv_ref[pl.ds(rb, BAND), AW + g * HD:AW + (g + 1) * HD]
            dk = jnp.zeros((BAND, HD), F32)
            dv = jnp.zeros((BAND, HD), F32)
            for hh in range(GROUP):
                h = g * GROUP + hh
                hcol = slice(h * HD, (h + 1) * HD)
                rows = slice(hh * BLK, (hh + 1) * BLK)
                dq_ref[blk, hcol] = (_dot(dsb_ref[n, rows, :], kband) * SCALE).astype(BF16)
                dk = dk + _dot_tn(dsb_ref[n, rows, :], qkv_ref[pl.ds(r0s[sub], BLK), hcol])
                dv = dv + _dot_tn(p_ref[n, rows, :], do_ref[blk, hcol])
                ds_ref[h:h + 1, :] += jnp.broadcast_to(
                    jnp.sum(dsinks[n][rows], axis=0, keepdims=True), (1, 128))
            dkv_ref[pl.ds(rb, BAND), g * HD:(g + 1) * HD] += dk * SCALE
            dkv_ref[pl.ds(rb, BAND), KVW + g * HD:KVW + (g + 1) * HD] += dv

    nslot, tq = nsub * NKV, nsub * BLK
    return pl.pallas_call(
        body, name="attn_bwd", grid=(T // tq,),
        in_specs=[pl.BlockSpec(memory_space=pltpu.SMEM), _whole((T, QKVW)),
                  _rows(tq, AW), _rows(tq, AW), _rows(tq, NQ),
                  pl.BlockSpec((nslot, GROWS, BAND), lambda i: (i, 0, 0)), _whole((8, 128))],
        out_specs=[_rows(tq, AW), _whole((T, 2 * KVW)), _whole((NQ, 128))],
        out_shape=[jax.ShapeDtypeStruct((T, AW), BF16), jax.ShapeDtypeStruct((T, 2 * KVW), F32),
                   jax.ShapeDtypeStruct((NQ, 128), F32)],
        scratch_shapes=[pltpu.VMEM((nslot, GROWS, BAND), F32), pltpu.VMEM((nslot, GROWS, BAND), BF16)],
        compiler_params=_params("arbitrary"),
    )(sinks, qkv, o, do, lse, p, dep)


PROJ_PARTS = [(0, AW), (AW, QKVW), (GLU_OFF, GATE_OFF), (GATE_OFF, INW)]


def _in_proj_bwd(dq, dkv, dglu, dgl, x, dx1, g_mix, w_in_t, dep):
    tm = ROW_TM

    def body(dq_ref, dkv_ref, dglu_ref, dgl_ref, x_ref, dx1_ref, g_ref, w_ref, dep_ref,
             gx_ref, dg_ref, db_ref):
        i = pl.program_id(0)

        @pl.when(i == 0)
        def _():
            dg_ref[...] = jnp.zeros_like(dg_ref)
            db_ref[...] = jnp.zeros_like(db_ref)

        dh = jnp.zeros((tm, D), F32)
        for part_ref, (lo, hi) in zip((dq_ref, dkv_ref, dglu_ref, dgl_ref), PROJ_PARTS):
            part = part_ref[...]
            dh = dh + _dot(part.astype(BF16), w_ref[lo:hi, :])
            db_ref[:, lo:hi] += jnp.sum(part.astype(F32), axis=0, keepdims=True)
        xv = x_ref[...]
        r = lax.rsqrt(jnp.mean(xv * xv, axis=-1, keepdims=True) + EPS)
        xh = xv * r
        dhg = dh * g_ref[...]
        gx_ref[...] = dx1_ref[...] + r * (dhg - xh * jnp.mean(dhg * xh, axis=-1, keepdims=True))
        dg_ref[...] += jnp.sum(dh * xh, axis=0, keepdims=True)

    return pl.pallas_call(
        body, name="in_proj_bwd", grid=(T // tm,),
        in_specs=[_rows(tm, AW), _rows(tm, 2 * KVW), _rows(tm, 2 * C), _rows(tm, 2 * D),
                  _rows(tm, D), _rows(tm, D), _whole((1, D)), _const((INW, D)), _whole((8, 128))],
        out_specs=[_rows(tm, D), _whole((1, D)), _whole((1, INW))],
        out_shape=[jax.ShapeDtypeStruct((T, D), F32), jax.ShapeDtypeStruct((1, D), F32),
                   jax.ShapeDtypeStruct((1, INW), F32)],
        compiler_params=_params("arbitrary"),
    )(dq, dkv, dglu, dgl, x, dx1, g_mix, w_in_t, dep)


def _grad_w_in_t(h, dq, dkv, dglu, dgl):
    tn, chunk = 512, 256

    def body(h_ref, dq_ref, dkv_ref, dglu_ref, dgl_ref, o_ref):
        hv = h_ref[...]
        for part_ref, (lo, hi) in zip((dq_ref, dkv_ref, dglu_ref, dgl_ref), PROJ_PARTS):
            for c0 in range(0, hi - lo, chunk):
                o_ref[lo + c0:lo + c0 + chunk, :] = _dot_tn(
                    part_ref[:, c0:c0 + chunk].astype(BF16), hv).astype(BF16)

    return pl.pallas_call(
        body, name="grad_w_in", grid=(D // tn,),
        in_specs=[pl.BlockSpec((T, tn), lambda j: (0, j)), _const((T, AW)), _const((T, 2 * KVW)),
                  _const((T, 2 * C)), _const((T, 2 * D))],
        out_specs=pl.BlockSpec((INW, tn), lambda j: (0, j)),
        out_shape=jax.ShapeDtypeStruct((INW, D), BF16),
        compiler_params=_params("parallel"),
    )(h, dq, dkv, dglu, dgl)


def _grad_w(a, b, name, tk, tn, col_sharded):
    k, n = a.shape[1], b.shape[1]

    single = n == tn
    sw = n // N_CHIPS

    def body(a_ref, b_ref, o_ref, at_ref):
        if single:
            res = _dot_tn(a_ref[...], b_ref[...]).astype(BF16)
            if col_sharded:
                for s in range(N_CHIPS):
                    o_ref[s] = res[:, s * sw:(s + 1) * sw]
            else:
                o_ref[...] = res
            return

        @pl.when(pl.program_id(1) == 0)
        def _():
            at_ref[...] = a_ref[...].T

        o_ref[...] = _dot(at_ref[...], b_ref[...]).astype(BF16)

    if col_sharded and single:
        shape = (N_CHIPS, k, sw)
        out_spec = pl.BlockSpec((N_CHIPS, tk, sw), lambda i, j: (0, i, 0))
    elif col_sharded:
        per = sw // tn
        shape = (N_CHIPS, k, sw)
        out_spec = pl.BlockSpec((None, tk, tn), lambda i, j: (j // per, i, j % per))
    else:
        shape = (1, k, n)
        out_spec = pl.BlockSpec((None, tk, tn), lambda i, j: (0, i, j))
    out = pl.pallas_call(
        body, name=name, grid=(k // tk, n // tn),
        in_specs=[pl.BlockSpec((T, tk), lambda i, j: (0, i)), pl.BlockSpec((T, tn), lambda i, j: (0, j))],
        out_specs=out_spec,
        out_shape=jax.ShapeDtypeStruct(shape, BF16),
        scratch_shapes=[pltpu.VMEM((tk, T), BF16)],
        compiler_params=_params("parallel", "arbitrary"),
    )(a, b)
    return out if col_sharded else out.reshape(N_CHIPS, k // N_CHIPS, n)


HBM_SPEC = pl.BlockSpec(memory_space=pltpu.HBM)


def _place():
    x, y, c = lax.axis_index("x"), lax.axis_index("y"), lax.axis_index("c")
    chips = [(1 - x, y), (x, 1 - y), (1 - x, 1 - y)]
    return x, y, c, chips


SEM_SPEC = pl.BlockSpec(memory_space=pltpu.SEMAPHORE)
ANY_SPEC = pl.BlockSpec(memory_space=pl.ANY)
VMEM_SPEC = pl.BlockSpec(memory_space=pltpu.VMEM)
EFFECT = pltpu.SideEffectType.DATAFLOW_SIDE_EFFECTING


def _gather_ends(src, land, x, y, c, chips):
    kh = src.shape[0] // 2
    s_me = 2 * x + y
    ends = [(src.at[pl.ds(c * kh, kh)], land.at[s_me, pl.ds(c * kh, kh)], (*chip, c)) for chip in chips]
    return ends + [(src, land.at[s_me], (x, y, 1 - c))]


def _reduce_ends(src, land, x, y, c, chips):
    return [(src.at[2 * chip[0] + chip[1]], land.at[j], (*chip, c)) for j, chip in enumerate(chips)]


def _chip_copies(ends, srcs, lands, send_sems, recv_sems, first=0):
    x, y, c, chips = _place()
    copies = []
    for src, land in zip(srcs, lands):
        peers = ends(src, land, x, y, c, chips)
        for s, d, to in peers:
            k = first * len(peers) + len(copies)
            copies.append(pltpu.make_async_remote_copy(
                src_ref=s, dst_ref=d, send_sem=send_sems.at[k], recv_sem=recv_sems.at[k],
                device_id=to, device_id_type=MESH))
    return copies


GATHER_PEERS, REDUCE_PEERS = 4, 3


def _chip_start(name, ends, peers, srcs, lands):
    n = len(srcs)

    def body(*refs):
        copies = _chip_copies(ends, refs[:n], refs[n:2 * n], refs[2 * n], refs[2 * n + 1])
        for cp in copies:
            cp.start()
        token = refs[-1]
        token[...] = jnp.zeros_like(token)

    hbm = lambda a: pltpu.HBM(a.shape, a.dtype)
    res = pl.pallas_call(
        body, name=name,
        out_shape=(pltpu.SemaphoreType.DMA((peers * n,)), pltpu.SemaphoreType.DMA((peers * n,)),
                   *[hbm(a) for a in srcs], *[hbm(a) for a in lands],
                   jax.ShapeDtypeStruct((8, 128), F32)),
        in_specs=[HBM_SPEC] * (2 * n),
        out_specs=(SEM_SPEC, SEM_SPEC, *[HBM_SPEC] * (2 * n), VMEM_SPEC),
        input_output_aliases={i: 2 + i for i in range(2 * n)},
        compiler_params=pltpu.CompilerParams(has_side_effects=EFFECT),
    )(*[pltpu.with_memory_space_constraint(a, pltpu.HBM) for a in (*srcs, *lands)])
    return res[0], res[1], list(res[2:2 + n]), list(res[2 + n:2 + 2 * n]), res[-1]


def _chip_wait(name, ends, send_sems, recv_sems, srcs, lands, after, first=0):
    n, na = len(srcs), len(after)

    def body(*refs):
        copies = _chip_copies(ends, refs[:n], refs[n:2 * n], refs[2 * n], refs[2 * n + 1], first)
        for cp in copies:
            cp.wait_send()
            cp.wait_recv()

    hbm = lambda a: pltpu.HBM(a.shape, a.dtype)
    res = pl.pallas_call(
        body, name=name,
        out_shape=tuple(hbm(a) for a in (*srcs, *lands)),
        in_specs=[HBM_SPEC] * (2 * n) + [SEM_SPEC, SEM_SPEC] + [ANY_SPEC] * na,
        out_specs=tuple([HBM_SPEC] * (2 * n)),
        input_output_aliases={i: i for i in range(2 * n)},
        compiler_params=pltpu.CompilerParams(has_side_effects=EFFECT),
    )(*srcs, *lands, send_sems, recv_sems, *after)
    return list(res[:n]), list(res[n:])


def _forward_copies(lands, send_sems, recv_sems):
    x, y, c, chips = _place()
    copies = []
    for land in lands:
        kh = land.shape[1] // 2
        for chip in chips:
            blk = land.at[2 * chip[0] + chip[1], pl.ds(c * kh, kh)]
            k = len(copies)
            copies.append(pltpu.make_async_remote_copy(
                src_ref=blk, dst_ref=blk, send_sem=send_sems.at[k], recv_sem=recv_sems.at[k],
                device_id=(x, y, 1 - c), device_id_type=MESH))
    return copies


def _gather_relay(name, send_sems, recv_sems, srcs, lands, after, first):
    n, na = len(srcs), len(after)

    def body(*refs):
        land_refs = refs[n:2 * n]
        for cp in _chip_copies(_gather_ends, refs[:n], land_refs, refs[2 * n], refs[2 * n + 1], first):
            cp.wait_send()
            cp.wait_recv()
        out = refs[2 * n + 2 + na:]
        for cp in _forward_copies(land_refs, out[0], out[1]):
            cp.start()
        out[-1][...] = jnp.zeros_like(out[-1])

    hbm = lambda a: pltpu.HBM(a.shape, a.dtype)
    res = pl.pallas_call(
        body, name=name,
        out_shape=(pltpu.SemaphoreType.DMA((3 * n,)), pltpu.SemaphoreType.DMA((3 * n,)),
                   *[hbm(a) for a in lands], jax.ShapeDtypeStruct((8, 128), F32)),
        in_specs=[HBM_SPEC] * (2 * n) + [SEM_SPEC, SEM_SPEC] + [ANY_SPEC] * na,
        out_specs=(SEM_SPEC, SEM_SPEC, *[HBM_SPEC] * n, VMEM_SPEC),
        input_output_aliases={n + i: 2 + i for i in range(n)},
        compiler_params=pltpu.CompilerParams(has_side_effects=EFFECT),
    )(*srcs, *lands, send_sems, recv_sems, *after)
    return res[0], res[1], list(res[2:2 + n]), res[-1]


def _forward_wait(name, send_sems, recv_sems, lands, after):
    n, na = len(lands), len(after)

    def body(*refs):
        for cp in _forward_copies(refs[:n], refs[n], refs[n + 1]):
            cp.wait_send()
            cp.wait_recv()

    hbm = lambda a: pltpu.HBM(a.shape, a.dtype)
    res = pl.pallas_call(
        body, name=name,
        out_shape=tuple(hbm(a) for a in lands),
        in_specs=[HBM_SPEC] * n + [SEM_SPEC, SEM_SPEC] + [ANY_SPEC] * na,
        out_specs=tuple([HBM_SPEC] * n),
        input_output_aliases={i: i for i in range(n)},
        compiler_params=pltpu.CompilerParams(has_side_effects=EFFECT),
    )(*lands, send_sems, recv_sems, *after)
    return list(res)


def _exchange_ends(src, land, x, y, c, chips):
    kh = src.shape[1] // 2
    return [(src.at[:, pl.ds((1 - c) * kh, kh)], land, (x, y, 1 - c))]


def _share_ends(src, land, x, y, c, chips):
    return [(src, land, (x, y, 1 - c))]


def _small_ends(src, land, x, y, c, chips):
    m = src.shape[0]
    rows = land.at[pl.ds((4 * x + 2 * y + c) * m, m)]
    peers = [(x, y, 1 - c)] + [(*chip, c) for chip in chips] + [(*chip, 1 - c) for chip in chips]
    return [(src, rows, to) for to in peers]


PAIR_PEERS, SMALL_PEERS = 1, 7


def _row_tile(k):
    for t in (256, 240, 128, 176, 64, 32, 16):
        if k % t == 0:
            return t
    raise ValueError(k)


def _pair_sum(c_idx, g, got, name):
    _, k, n = g.shape
    kh = k // 2
    tm = _row_tile(kh)
    nb = kh // tm

    def body(c_ref, g_ref, r_ref, o_ref):
        o_ref[...] = (g_ref[...].astype(F32) + r_ref[...].astype(F32)).astype(BF16)

    return pl.pallas_call(
        body, name=name,
        grid_spec=pltpu.PrefetchScalarGridSpec(
            num_scalar_prefetch=1, grid=(nb,),
            in_specs=[pl.BlockSpec((N_CHIPS, tm, n), lambda i, c_ref: (0, c_ref[0] * nb + i, 0)),
                      pl.BlockSpec((N_CHIPS, tm, n), lambda i, c_ref: (0, i, 0))],
            out_specs=pl.BlockSpec((N_CHIPS, tm, n), lambda i, c_ref: (0, i, 0))),
        out_shape=jax.ShapeDtypeStruct((N_CHIPS, kh, n), BF16),
        compiler_params=_params("parallel"),
    )(c_idx, g, got)


def _chip_sum(s_idx, mine, got, name):
    _, kh, n = mine.shape
    tm = _row_tile(kh)

    def body(s_ref, m_ref, r_ref, o_ref):
        acc = m_ref[0].astype(F32)
        for j in range(3):
            acc = acc + r_ref[j].astype(F32)
        o_ref[...] = acc

    return pl.pallas_call(
        body, name=name,
        grid_spec=pltpu.PrefetchScalarGridSpec(
            num_scalar_prefetch=1, grid=(kh // tm,),
            in_specs=[pl.BlockSpec((1, tm, n), lambda i, s_ref: (s_ref[0], i, 0)),
                      pl.BlockSpec((3, tm, n), lambda i, s_ref: (0, i, 0))],
            out_specs=pl.BlockSpec((tm, n), lambda i, s_ref: (i, 0))),
        out_shape=jax.ShapeDtypeStruct((kh, n), F32),
        compiler_params=_params("parallel"),
    )(s_idx, mine, got)


def _adamw_math(w, g, m, v):
    m = ADAM_B1 * m + (1.0 - ADAM_B1) * g
    v = ADAM_B2 * v + (1.0 - ADAM_B2) * (g * g)
    m_hat = m / (1.0 - ADAM_B1 ** ADAM_STEP)
    v_hat = v / (1.0 - ADAM_B2 ** ADAM_STEP)
    delta = -ADAM_LR * (m_hat / (jnp.sqrt(v_hat) + ADAM_EPS) + ADAM_WD * w)
    return delta, m, v


def _adamw(c_idx, w, g_mine, g_other, m, v, name):
    k, n = w.shape
    tm = k // 4

    def body(c_ref, w_ref, gm_ref, go_ref, m_ref, v_ref, g_ref, d_ref, mo_ref, vo_ref):
        g = jnp.where(pl.program_id(0) == c_ref[0], gm_ref[...], go_ref[...])
        d, mm, vv = _adamw_math(w_ref[...], g, m_ref[...], v_ref[...])
        g_ref[...] = g
        d_ref[...] = d
        mo_ref[...] = mm
        vo_ref[...] = vv

    full = pl.BlockSpec((tm, n), lambda h, i, c_ref: (2 * h + i, 0))
    mine = pl.BlockSpec((tm, n), lambda h, i, c_ref: (jnp.where(h == c_ref[0], i, 0), 0))
    other = pl.BlockSpec((tm, n), lambda h, i, c_ref: (jnp.where(h == c_ref[0], 0, i), 0))
    shp = jax.ShapeDtypeStruct((k, n), F32)
    return pl.pallas_call(
        body, name=name,
        grid_spec=pltpu.PrefetchScalarGridSpec(
            num_scalar_prefetch=1, grid=(2, 2),
            in_specs=[full, mine, other, full, full], out_specs=[full] * 4),
        out_shape=[shp] * 4, compiler_params=_params("arbitrary", "arbitrary"),
    )(c_idx, w, g_mine, g_other, m, v)


VEC_SLOTS = {
    "g_mix_norm": (0, 0, D), "b_conv_proj": (0, D, D), "g_ffn_norm": (0, 2 * D, D),
    "g_final": (0, 3 * D, D), "b_in": (1, 0, INW), "conv_b": (2, 0, C), "ln_g": (2, C, C),
    "ln_b": (2, 2 * C, C), "sinks": (2, 3 * C, NQ), "loss": (2, 3 * C + 128, 1),
}
VEC_ROWS, VEC_COLS = 8, 4 * D
CW_ROWS = 32
SMALL_NAMES = ["g_mix_norm", "b_in", "sinks", "conv_w", "conv_b", "ln_g", "ln_b",
               "b_conv_proj", "g_ffn_norm", "g_final"]
CW_LANES = C // N_CHIPS


def _pack_small(gs, loss):
    row0 = jnp.concatenate([gs["g_mix_norm"], gs["b_conv_proj"], gs["g_ffn_norm"], gs["g_final"]], axis=1)
    row1 = jnp.pad(gs["b_in"], ((0, 0), (0, VEC_COLS - INW)))
    row2 = jnp.concatenate([gs["conv_b"], gs["ln_g"], gs["ln_b"],
                            jnp.pad(gs["sinks"], ((0, 0), (0, 128 - NQ))),
                            jnp.pad(loss.reshape(1, 1), ((0, 0), (0, VEC_COLS - 3 * C - 129)))], axis=1)
    vec = jnp.concatenate([row0, row1, row2, jnp.zeros((VEC_ROWS - 3, VEC_COLS), F32)], axis=0)
    cw = jnp.pad(gs["conv_w"], ((0, CW_ROWS - KW), (0, 0)))
    return vec, cw


def _small_update(idx, vec_own, cw_own, vec_all, cw_all, wmv):
    nsm = len(SMALL_NAMES)

    def body(s_ref, vown_ref, cown_ref, vec_ref, cw_ref, *refs):
        ins = refs[:3 * nsm]
        outs = refs[3 * nsm:7 * nsm]
        loss_ref = refs[7 * nsm]
        me = s_ref[1]

        def summed(own_ref, table_ref, rows_per_dev, r0, nrows, lane, width):
            acc = None
            for k in range(8):
                piece = jnp.where(me == k, own_ref[r0:r0 + nrows, lane:lane + width],
                                  table_ref[k * rows_per_dev + r0:k * rows_per_dev + r0 + nrows, lane:lane + width])
                acc = piece if acc is None else acc + piece
            return acc

        def total(slot):
            row, lane, width = slot
            return summed(vown_ref, vec_ref, VEC_ROWS, row, 1, lane, width)

        loss_ref[...] = jnp.broadcast_to(total(VEC_SLOTS["loss"]), loss_ref.shape)
        for p, name in enumerate(SMALL_NAMES):
            w_ref, m_ref, v_ref = ins[3 * p:3 * p + 3]
            g_ref, d_ref, mo_ref, vo_ref = outs[4 * p:4 * p + 4]
            if name == "conv_w":
                g = jnp.zeros((KW, CW_LANES), F32)
                for s in range(N_CHIPS):
                    cand = summed(cown_ref, cw_ref, CW_ROWS, 0, KW, s * CW_LANES, CW_LANES)
                    g = jnp.where(s_ref[0] == s, cand, g)
            else:
                g = total(VEC_SLOTS[name])
            d, mm, vv = _adamw_math(w_ref[...], g, m_ref[...], v_ref[...])
            g_ref[...] = g
            d_ref[...] = d
            mo_ref[...] = mm
            vo_ref[...] = vv

    vmem = pl.BlockSpec(memory_space=pltpu.VMEM)
    flat = [a for t in wmv for a in t]
    out_shape = []
    for w, _, _ in wmv:
        out_shape += [jax.ShapeDtypeStruct(w.shape, F32)] * 4
    out_shape.append(jax.ShapeDtypeStruct((1, 128), F32))
    res = pl.pallas_call(
        body, name="small_update",
        in_specs=[pl.BlockSpec(memory_space=pltpu.SMEM)] + [vmem] * (4 + len(flat)),
        out_specs=[vmem] * len(out_shape), out_shape=out_shape,
    )(idx, vec_own, cw_own, vec_all, cw_all, *flat)
    return [tuple(res[4 * p:4 * p + 4]) for p in range(nsm)], res[4 * nsm]


WEIGHT_ORDER = ["g_mix_norm", "w_in", "b_in", "sinks", "conv_w", "conv_b", "ln_g", "ln_b",
                "w_attn_proj", "w_conv_proj", "b_conv_proj", "w_out", "g_ffn_norm", "w_ffn_in",
                "w_ffn_down", "g_final"]


def kernel(x, g_mix_norm, w_in, b_in, sinks, conv_w, conv_b, ln_g, ln_b, w_attn_proj, w_conv_proj, b_conv_proj, w_out, g_ffn_norm, w_ffn_in, w_ffn_down, g_final, loss_target, m_g_mix_norm, m_w_in, m_b_in, m_sinks, m_conv_w, m_conv_b, m_ln_g, m_ln_b, m_w_attn_proj, m_w_conv_proj, m_b_conv_proj, m_w_out, m_g_ffn_norm, m_w_ffn_in, m_w_ffn_down, m_g_final, v_g_mix_norm, v_w_in, v_b_in, v_sinks, v_conv_w, v_conv_b, v_ln_g, v_ln_b, v_w_attn_proj, v_w_conv_proj, v_b_conv_proj, v_w_out, v_g_ffn_norm, v_w_ffn_in, v_w_ffn_down, v_g_final):
    w = dict(g_mix_norm=g_mix_norm, w_in=w_in, b_in=b_in, sinks=sinks, conv_w=conv_w, conv_b=conv_b,
             ln_g=ln_g, ln_b=ln_b, w_attn_proj=w_attn_proj, w_conv_proj=w_conv_proj,
             b_conv_proj=b_conv_proj, w_out=w_out, g_ffn_norm=g_ffn_norm, w_ffn_in=w_ffn_in,
             w_ffn_down=w_ffn_down, g_final=g_final)
    m = dict(g_mix_norm=m_g_mix_norm, w_in=m_w_in, b_in=m_b_in, sinks=m_sinks, conv_w=m_conv_w,
             conv_b=m_conv_b, ln_g=m_ln_g, ln_b=m_ln_b, w_attn_proj=m_w_attn_proj,
             w_conv_proj=m_w_conv_proj, b_conv_proj=m_b_conv_proj, w_out=m_w_out,
             g_ffn_norm=m_g_ffn_norm, w_ffn_in=m_w_ffn_in, w_ffn_down=m_w_ffn_down, g_final=m_g_final)
    v = dict(g_mix_norm=v_g_mix_norm, w_in=v_w_in, b_in=v_b_in, sinks=v_sinks, conv_w=v_conv_w,
             conv_b=v_conv_b, ln_g=v_ln_g, ln_b=v_ln_b, w_attn_proj=v_w_attn_proj,
             w_conv_proj=v_w_conv_proj, b_conv_proj=v_b_conv_proj, w_out=v_w_out,
             g_ffn_norm=v_g_ffn_norm, w_ffn_in=v_w_ffn_in, w_ffn_down=v_w_ffn_down, g_final=v_g_final)

    c_idx = lax.axis_index("c").astype(jnp.int32).reshape(1)
    s_idx = (2 * lax.axis_index("x") + lax.axis_index("y")).astype(jnp.int32).reshape(1)

    out_g, out_d, out_m, out_v = {}, {}, {}, {}

    def gather_start(tag, shards):
        lands = [lax.empty((N_CHIPS,) + s.shape, s.dtype) for s in shards]
        return _chip_start("gather_start_" + tag, _gather_ends, GATHER_PEERS, shards, lands)

    def gather_relay(tag, state, after, first=0, count=None):
        send_sems, recv_sems, shards, lands, _ = state
        last = len(shards) if count is None else first + count
        return _gather_relay("gather_relay_" + tag, send_sems, recv_sems, shards[first:last],
                             lands[first:last], after, first)

    def gather_finish(tag, relay, after):
        return _forward_wait("forward_wait_" + tag, relay[0], relay[1], relay[2], after)

    names_b = ["w_attn_proj", "w_conv_proj", "w_out", "w_ffn_in", "w_ffn_down"]
    big = {name: (w[name][0], m[name][0], v[name][0]) for name in names_b}
    big["w_in"] = (w_in[0].T, m_w_in[0].T, v_w_in[0].T)
    state_a = gather_start("a", [big["w_in"][0].astype(BF16), jnp.pad(conv_w[0], ((0, CW_ROWS - KW), (0, 0)))])
    state_b = gather_start("b", [(big[name][0] + state_a[4][0, 0]).astype(BF16) for name in names_b])
    got_a = gather_finish("a", gather_relay("a", state_a, [state_b[4]]), [])
    w_in_t_full = got_a[0].reshape(INW, D)
    conv_w_full = got_a[1].transpose(1, 0, 2).reshape(CW_ROWS, C)[:KW]

    xs, target = x[0], loss_target[0]
    g_final2 = g_final.reshape(1, D)
    h, qkv, glu, gl = _in_proj(xs, g_mix_norm, w_in_t_full, b_in)
    o, lse, probs = _attn_fwd(qkv, sinks)
    relay_1 = gather_relay("b1", state_b, [o], 0, 3)
    u, cact = _conv_fwd(glu, conv_w_full, conv_b, ln_g, ln_b, relay_1[3])
    w_ap4, w_cp4, w_out4 = gather_finish("b1", relay_1, [cact])
    w_out_full = w_out4.reshape(D, D)
    mixed = _mix_out(xs, o, cact, gl, w_ap4, w_cp4, b_conv_proj, w_out_full, relay_1[3])
    relay_2 = gather_relay("b2", state_b, [mixed[3]], 3, 1)
    ya, yc, mg, x1 = _mix_out(xs, o, cact, gl, w_ap4, w_cp4, b_conv_proj, w_out_full, relay_2[3], mixed)
    w_fi4, = gather_finish("b2", relay_2, [x1])
    h2, gu, act = _ffn_in_first(x1, g_ffn_norm, w_fi4, relay_2[3])
    relay_3 = gather_relay("b3", state_b, [h2], 4, 1)
    gu, act = _ffn_in_second(h2, w_fi4, gu, act, relay_3[3])
    w_dn4, = gather_finish("b3", relay_3, [act])
    w_dn_full = w_dn4.reshape(DFF, D)
    dx2, dx2b, dg_final, loss_part = _ffn_out_loss(x1, act, w_dn_full, g_final2, target)

    def exchange_start(tag, grads):
        lands = [lax.empty((N_CHIPS, g.shape[1] // 2, g.shape[2]), g.dtype) for g in grads]
        return _chip_start("pair_start_" + tag, _exchange_ends, PAIR_PEERS, grads, lands)

    def reduce_start(tag, names, exchange, after):
        send_sems, recv_sems, grads, lands, _ = exchange
        grads, from_sibling = _chip_wait("pair_wait_" + tag, _exchange_ends, send_sems, recv_sems, grads, lands, after)
        pair = [_pair_sum(c_idx, g, r, "pair_sum_" + name) for name, g, r in zip(names, grads, from_sibling)]
        lands = [lax.empty((3,) + p.shape[1:], p.dtype) for p in pair]
        return _chip_start("chip_start_" + tag, _reduce_ends, REDUCE_PEERS, pair, lands)

    def reduce_sum(tag, names, state, after):
        send_sems, recv_sems, pair, lands, _ = state
        pair, lands = _chip_wait("chip_wait_" + tag, _reduce_ends, send_sems, recv_sems, pair, lands, after)
        mine = [_chip_sum(s_idx, p, r, "chip_sum_" + name) for name, p, r in zip(names, pair, lands)]
        others = [lax.empty(a.shape, a.dtype) for a in mine]
        return _chip_start("share_start_" + tag, _share_ends, PAIR_PEERS, mine, others)

    def reduce_finish(tag, names, share, after):
        send_sems, recv_sems, mine, others, _ = share
        mine, others = _chip_wait("share_wait_" + tag, _share_ends, send_sems, recv_sems, mine, others, after)
        for name, g_mine, g_other in zip(names, mine, others):
            wv, mv, vv = big[name]
            res = _adamw(c_idx, wv, g_mine, g_other, mv, vv, "adamw_" + name)
            if name == "w_in":
                res = [a.T for a in res]
            out_g[name], out_d[name], out_m[name], out_v[name] = [a[None] for a in res]

    dgu, dx1, dx1b, dg_ffn = _ffn_bwd(dx2, dx2b, gu, x1, g_ffn_norm, w_dn_full, w_fi4)
    names_1 = ["w_ffn_in", "w_ffn_down", "w_out", "w_attn_proj", "w_conv_proj"]
    grads_1 = [_grad_w(h2, dgu, "grad_w_ffn_in", 512, FSH, True),
               _grad_w(act, dx2b, "grad_w_ffn_down", 256, D, False)]
    dya, dyc, dgl, do, dc, db_cp = _mix_bwd(dx1b, gl, ya, yc, w_out_full, w_ap4, w_cp4, relay_3[3])
    grads_1 += [_grad_w(mg, dx1b, "grad_w_out", 512, D, False),
                _grad_w(o, dya, "grad_w_attn_proj", 512, D, True),
                _grad_w(cact, dyc, "grad_w_conv_proj", 512, D, True)]
    exchange_1 = exchange_start("1", grads_1)
    dglu, dconv_w, dconv_b, dln_g, dln_b = _conv_bwd(glu, u, dc, conv_w_full, ln_g, ln_b, exchange_1[4])
    state_1 = reduce_start("1", names_1, exchange_1, [dglu])
    dq, dkv, dsinks = _attn_bwd(qkv, o, do, lse, probs, sinks, state_1[4])
    names_2 = ["w_in"]
    gw_in_t = _grad_w_in_t(h, dq, dkv, dglu, dgl)
    exchange_2 = exchange_start("2", [gw_in_t.reshape(N_CHIPS, INW // N_CHIPS, D)])
    grad_x, dg_mix, db_in = _in_proj_bwd(dq, dkv, dglu, dgl, xs, dx1, g_mix_norm, w_in_t_full, exchange_2[4])

    gs = {"g_mix_norm": dg_mix, "b_in": db_in, "sinks": dsinks[:, 0].reshape(1, NQ),
          "conv_w": dconv_w, "conv_b": dconv_b, "ln_g": dln_g, "ln_b": dln_b,
          "b_conv_proj": db_cp, "g_ffn_norm": dg_ffn, "g_final": dg_final}
    blocks = list(_pack_small(gs, loss_part[0, 0]))
    tables = [lax.empty((8 * b.shape[0], b.shape[1]), b.dtype) for b in blocks]
    small = _chip_start("small_start", _small_ends, SMALL_PEERS, blocks, tables)

    state_2 = reduce_start("2", names_2, exchange_2, [grad_x, small[4]])
    share_1 = reduce_sum("1", names_1, state_1, [state_2[4]])
    blocks, tables = _chip_wait("small_wait", _small_ends, small[0], small[1], small[2], small[3],
                                [share_1[4]])
    me = (4 * lax.axis_index("x") + 2 * lax.axis_index("y") + lax.axis_index("c")).astype(jnp.int32)

    def view(a, name):
        if name == "conv_w":
            return a[0]
        if name == "g_final":
            return a.reshape(1, D)
        return a

    wmv = [(view(w[name], name), view(m[name], name), view(v[name], name)) for name in SMALL_NAMES]
    small_out, loss_row = _small_update(jnp.concatenate([s_idx, me.reshape(1)]), blocks[0], blocks[1],
                                        tables[0], tables[1], wmv)
    for name, (g, d, mm, vv) in zip(SMALL_NAMES, small_out):
        shape = w[name].shape
        out_g[name], out_d[name], out_m[name], out_v[name] = (
            g.reshape(shape), d.reshape(shape), mm.reshape(shape), vv.reshape(shape))

    reduce_finish("1", names_1, share_1, [loss_row])
    share_2 = reduce_sum("2", names_2, state_2, [out_d["w_conv_proj"]])
    reduce_finish("2", names_2, share_2, [])

    loss = loss_row[0, 0]
    return (loss, grad_x[None], *[out_g[k] for k in WEIGHT_ORDER], *[out_d[k] for k in WEIGHT_ORDER],
            *[out_m[k] for k in WEIGHT_ORDER], *[out_v[k] for k in WEIGHT_ORDER])
```

```python
import functools

import jax
import jax.numpy as jnp
from jax import lax
from jax.experimental import pallas as pl
from jax.experimental.pallas import tpu as pltpu

F32 = jnp.float32
BF16 = jnp.bfloat16

T = 2048
D = 1024
HD = 64
NQ = 8
NKV = 2
GROUP = NQ // NKV
BLK = 128
AW = NQ * HD
KVW = NKV * HD
C = 512
KW = 31
QKVW = AW + 2 * KVW
GLU_OFF = QKVW
GATE_OFF = GLU_OFF + 2 * C
INW = GATE_OFF + 2 * D
DFF = 2816
EPS = 1e-5
NEG = -1e30
SCALE = HD ** -0.5
HALO = 32
N_CHIPS = 4
FSH = 2 * DFF // N_CHIPS

ADAM_LR = 0.001
ADAM_B1 = 0.9
ADAM_B2 = 0.999
ADAM_EPS = 1e-08
ADAM_WD = 0.01
ADAM_STEP = 10

VMEM_LIMIT = 56 * 1024 * 1024
ROW_TM = 512
MESH = pl.DeviceIdType.MESH


def _params(*sem):
    return pltpu.CompilerParams(dimension_semantics=sem, vmem_limit_bytes=VMEM_LIMIT)


def _dot(a, b):
    return jnp.dot(a, b, preferred_element_type=F32)


def _dot_nt(a, b):
    return lax.dot_general(a, b, (((1,), (1,)), ((), ())), preferred_element_type=F32)


def _dot_tn(a, b):
    return lax.dot_general(a, b, (((0,), (0,)), ((), ())), preferred_element_type=F32)


def _sigmoid(v):
    return 1.0 / (1.0 + jnp.exp(-v))


def _rows(tm, n):
    return pl.BlockSpec((tm, n), lambda i: (i, 0))


def _whole(shape):
    return pl.BlockSpec(shape, lambda i: tuple(0 for _ in shape))


def _in_proj(x, g_mix, w_in_t, b_in):
    tm = ROW_TM

    def body(x_ref, g_ref, w_ref, b_ref, h_ref, qkv_ref, glu_ref, gl_ref):
        xv = x_ref[...]
        r = lax.rsqrt(jnp.mean(xv * xv, axis=-1, keepdims=True) + EPS)
        h = (xv * r * g_ref[...]).astype(BF16)
        h_ref[...] = h
        qkv_ref[...] = (_dot_nt(h, w_ref[0:GLU_OFF, :]) + b_ref[:, 0:GLU_OFF]).astype(BF16)
        glu_ref[...] = (_dot_nt(h, w_ref[GLU_OFF:GATE_OFF, :]) + b_ref[:, GLU_OFF:GATE_OFF]).astype(BF16)
        gl_ref[...] = (_dot_nt(h, w_ref[GATE_OFF:INW, :]) + b_ref[:, GATE_OFF:INW]).astype(BF16)

    return pl.pallas_call(
        body, name="in_proj", grid=(T // tm,),
        in_specs=[_rows(tm, D), _whole((1, D)), _whole((INW, D)), _whole((1, INW))],
        out_specs=[_rows(tm, D), _rows(tm, QKVW), _rows(tm, 2 * C), _rows(tm, 2 * D)],
        out_shape=[jax.ShapeDtypeStruct((T, D), BF16), jax.ShapeDtypeStruct((T, QKVW), BF16),
                   jax.ShapeDtypeStruct((T, 2 * C), BF16), jax.ShapeDtypeStruct((T, 2 * D), BF16)],
        compiler_params=_params("parallel"),
    )(x, g_mix, w_in_t, b_in)


GROWS = GROUP * BLK
BAND = 2 * BLK
ATT_SUB = 2


def _band(i):
    rb = pl.multiple_of(jnp.maximum(i - 1, 0) * BLK, BLK)
    row = lax.broadcasted_iota(jnp.int32, (GROWS, BAND), 0)
    kpos = rb + lax.broadcasted_iota(jnp.int32, (GROWS, BAND), 1)
    qpos = i * BLK + jnp.bitwise_and(row, BLK - 1)
    return rb, jnp.logical_and(kpos <= qpos, kpos > qpos - BLK)


def _sink_column(sink_ref, g):
    head = lax.shift_right_logical(lax.broadcasted_iota(jnp.int32, (GROWS, 1), 0), 7)
    col = jnp.full((GROWS, 1), sink_ref[0, g * GROUP], F32)
    for hh in range(1, GROUP):
        col = jnp.where(head == hh, sink_ref[0, g * GROUP + hh], col)
    return col


def _attn_fwd(qkv, sinks):
    def body(sink_ref, qkv_ref, o_ref, lse_ref, p_ref, s_ref):
        slots = [(sub, g) for sub in range(ATT_SUB) for g in range(NKV)]
        bands = [_band(pl.program_id(0) * ATT_SUB + sub) for sub in range(ATT_SUB)]
        for n, (sub, g) in enumerate(slots):
            rb = bands[sub][0]
            r0 = pl.multiple_of((pl.program_id(0) * ATT_SUB + sub) * BLK, BLK)
            kband = qkv_ref[pl.ds(rb, BAND), AW + g * HD:AW + (g + 1) * HD]
            for hh in range(GROUP):
                h = g * GROUP + hh
                s_ref[n, hh * BLK:(hh + 1) * BLK, :] = _dot_nt(
                    qkv_ref[pl.ds(r0, BLK), h * HD:(h + 1) * HD], kband)
        lses = []
        for n, (sub, g) in enumerate(slots):
            s = jnp.where(bands[sub][1], s_ref[n] * SCALE, NEG)
            sink = _sink_column(sink_ref, g)
            m = jnp.maximum(jnp.max(s, axis=-1, keepdims=True), sink)
            p = jnp.exp(s - m)
            den = jnp.sum(p, axis=-1, keepdims=True) + jnp.exp(sink - m)
            p_ref[n] = (p * (1.0 / den)).astype(BF16)
            lses.append(m + jnp.log(den))
        for n, (sub, g) in enumerate(slots):
            rb = bands[sub][0]
            rows = slice(sub * BLK, (sub + 1) * BLK)
            vband = qkv_ref[pl.ds(rb, BAND), AW + KVW + g * HD:AW + KVW + (g + 1) * HD]
            for hh in range(GROUP):
                h = g * GROUP + hh
                o_ref[rows, h * HD:(h + 1) * HD] = _dot(p_ref[n, hh * BLK:(hh + 1) * BLK, :], vband).astype(BF16)
                lse_ref[rows, h:h + 1] = lses[n][hh * BLK:(hh + 1) * BLK]

    nslot = ATT_SUB * NKV
    return pl.pallas_call(
        body, name="attn_fwd", grid=(T // (ATT_SUB * BLK),),
        in_specs=[pl.BlockSpec(memory_space=pltpu.SMEM), _whole((T, QKVW))],
        out_specs=[_rows(ATT_SUB * BLK, AW), _rows(ATT_SUB * BLK, NQ),
                   pl.BlockSpec((nslot, GROWS, BAND), lambda i: (i, 0, 0))],
        out_shape=[jax.ShapeDtypeStruct((T, AW), BF16), jax.ShapeDtypeStruct((T, NQ), F32),
                   jax.ShapeDtypeStruct((T // BLK * NKV, GROWS, BAND), BF16)],
        scratch_shapes=[pltpu.VMEM((nslot, GROWS, BAND), F32)],
        compiler_params=_params("parallel"),
    )(sinks, qkv)


CONV_TM = 256
CONV_SUB = 32


def _glu(ab):
    a = ab[:, 0:C].astype(F32)
    b = ab[:, C:2 * C].astype(F32)
    return a * _sigmoid(b)


SUBLANES = 8


def _shifted_copies(ref):
    rows = ref.shape[1] - SUBLANES
    for r in range(1, SUBLANES):
        ref[r, 0:rows, :] = ref[0, r:r + rows, :]


def _shifted_rows(ref, start, size):
    r = start % SUBLANES
    return ref[r, start - r:start - r + size, :]


def _conv_fwd(glu, conv_w, conv_b, ln_g, ln_b, dep):
    tm = CONV_TM

    def body(cur_ref, prev_ref, w_ref, cb_ref, g_ref, b_ref, dep_ref, u_ref, c_ref, zs_ref):
        i = pl.program_id(0)
        zprev = _glu(prev_ref[tm - HALO:tm, :])
        zs_ref[0, 0:HALO, :] = jnp.where(i > 0, zprev, 0.0)
        zs_ref[0, HALO:HALO + tm, :] = _glu(cur_ref[...])
        _shifted_copies(zs_ref)
        for s in range(tm // CONV_SUB):
            base = HALO + s * CONV_SUB - (KW - 1)
            acc = jnp.broadcast_to(cb_ref[...], (CONV_SUB, C))
            for j in range(KW):
                acc = acc + w_ref[j:j + 1, :] * _shifted_rows(zs_ref, base + j, CONV_SUB)
            rows = slice(s * CONV_SUB, (s + 1) * CONV_SUB)
            u_ref[rows, :] = acc
            mu = jnp.mean(acc, axis=-1, keepdims=True)
            xc = acc - mu
            var = jnp.mean(xc * xc, axis=-1, keepdims=True)
            y = xc * lax.rsqrt(var + EPS) * g_ref[...] + b_ref[...]
            c_ref[rows, :] = (y * _sigmoid(y)).astype(BF16)

    return pl.pallas_call(
        body, name="conv_fwd", grid=(T // tm,),
        in_specs=[_rows(tm, 2 * C),
                  pl.BlockSpec((tm, 2 * C), lambda i: (jnp.maximum(i - 1, 0), 0)),
                  _whole((KW, C)), _whole((1, C)), _whole((1, C)), _whole((1, C)), _whole((8, 128))],
        out_specs=[_rows(tm, C), _rows(tm, C)],
        out_shape=[jax.ShapeDtypeStruct((T, C), F32), jax.ShapeDtypeStruct((T, C), BF16)],
        scratch_shapes=[pltpu.VMEM((SUBLANES, HALO + tm, C), F32)],
        compiler_params=_params("parallel"),
    )(glu, glu, conv_w, conv_b, ln_g, ln_b, dep)


def _branch_outputs(o, cact, wap_ref, wcp_ref, bcp_ref):
    ya = jnp.concatenate([_dot(o, wap_ref[s]) for s in range(N_CHIPS)], axis=1)
    yc = jnp.concatenate([_dot(cact, wcp_ref[s]) for s in range(N_CHIPS)], axis=1) + bcp_ref[...]
    return ya, yc


def _mix_out(x, o, cact, gl, w_ap, w_cp, b_cp, w_out, dep, begun=None):
    tm = ROW_TM
    steps = T // tm // 2
    first = 0 if begun is None else steps
    rows = lambda n: pl.BlockSpec((tm, n), lambda i: (i + first, 0))
    extra = [] if begun is None else list(begun)

    def body(x_ref, o_ref, c_ref, gl_ref, wap_ref, wcp_ref, bcp_ref, wo_ref, dep_ref, *rest):
        ya_ref, yc_ref, mg_ref, x1_ref = rest[len(extra):]
        ya, yc = _branch_outputs(o_ref[...], c_ref[...], wap_ref, wcp_ref, bcp_ref)
        g0 = _sigmoid(gl_ref[:, 0:D].astype(F32))
        g1 = _sigmoid(gl_ref[:, D:2 * D].astype(F32))
        mg = (g0 * ya + g1 * yc).astype(BF16)
        ya_ref[...] = ya.astype(BF16)
        yc_ref[...] = yc.astype(BF16)
        mg_ref[...] = mg
        x1_ref[...] = x_ref[...] + _dot(mg, wo_ref[...])

    return pl.pallas_call(
        body, name="mix_out_first" if begun is None else "mix_out_second", grid=(steps,),
        in_specs=[rows(D), rows(AW), rows(C), rows(2 * D),
                  _whole((N_CHIPS, AW, D // N_CHIPS)), _whole((N_CHIPS, C, D // N_CHIPS)), _whole((1, D)),
                  _whole((D, D)), _whole((8, 128))] + [ANY_SPEC] * len(extra),
        out_specs=[rows(D), rows(D), rows(D), rows(D)],
        out_shape=[jax.ShapeDtypeStruct((T, D), BF16), jax.ShapeDtypeStruct((T, D), BF16),
                   jax.ShapeDtypeStruct((T, D), BF16), jax.ShapeDtypeStruct((T, D), F32)],
        input_output_aliases={9 + k: k for k in range(len(extra))},
        compiler_params=_params("parallel"),
    )(x, o, cact, gl, w_ap, w_cp, b_cp, w_out, dep, *extra)


def _swiglu_half(h, wg_ref, wu_ref, gu_ref, act_ref):
    gate = _dot(h, wg_ref[0])
    up = _dot(h, wu_ref[0])
    gu_ref[:, 0:FSH] = gate.astype(BF16)
    gu_ref[:, FSH:2 * FSH] = up.astype(BF16)
    act_ref[...] = (gate * _sigmoid(gate) * up).astype(BF16)


def _shard_spec(k):
    return pl.BlockSpec((1, D, FSH), lambda i: (k, 0, 0), pipeline_mode=pl.Buffered(1))


def _ffn_in_first(x1, g_ffn, w_fi, dep):
    tm = ROW_TM

    def body(x_ref, g_ref, wg_ref, wu_ref, dep_ref, h_ref, gu_ref, act_ref):
        xv = x_ref[...]
        r = lax.rsqrt(jnp.mean(xv * xv, axis=-1, keepdims=True) + EPS)
        h = (xv * r * g_ref[...]).astype(BF16)
        h_ref[...] = h
        _swiglu_half(h, wg_ref, wu_ref, gu_ref, act_ref)

    return pl.pallas_call(
        body, name="ffn_in_first", grid=(T // tm,),
        in_specs=[_rows(tm, D), _whole((1, D)), _shard_spec(0), _shard_spec(2), _whole((8, 128))],
        out_specs=[_rows(tm, D), pl.BlockSpec((tm, 2 * FSH), lambda i: (i, 0)),
                   pl.BlockSpec((tm, FSH), lambda i: (i, 0))],
        out_shape=[jax.ShapeDtypeStruct((T, D), BF16), jax.ShapeDtypeStruct((T, 2 * DFF), BF16),
                   jax.ShapeDtypeStruct((T, DFF), BF16)],
        compiler_params=_params("parallel"),
    )(x1, g_ffn, w_fi, w_fi, dep)


def _ffn_in_second(h2, w_fi, gu, act, dep):
    tm = ROW_TM

    def body(h_ref, wg_ref, wu_ref, gu_in, act_in, dep_ref, gu_ref, act_ref):
        _swiglu_half(h_ref[...], wg_ref, wu_ref, gu_ref, act_ref)

    return pl.pallas_call(
        body, name="ffn_in_second", grid=(T // tm,),
        in_specs=[_rows(tm, D), _shard_spec(1), _shard_spec(3), ANY_SPEC, ANY_SPEC, _whole((8, 128))],
        out_specs=[pl.BlockSpec((tm, 2 * FSH), lambda i: (i, 1)), pl.BlockSpec((tm, FSH), lambda i: (i, 1))],
        out_shape=[jax.ShapeDtypeStruct((T, 2 * DFF), BF16), jax.ShapeDtypeStruct((T, DFF), BF16)],
        input_output_aliases={3: 0, 4: 1},
        compiler_params=_params("parallel"),
    )(h2, w_fi, w_fi, gu, act, dep)


def _ffn_out_loss(x1, act, w_dn, g_final, target):
    tm = ROW_TM

    def body(x_ref, a_ref, w_ref, g_ref, t_ref, dx_ref, dxb_ref, dg_ref, loss_ref):
        i = pl.program_id(0)
        x2 = x_ref[...] + _dot(a_ref[...], w_ref[...])
        r = lax.rsqrt(jnp.mean(x2 * x2, axis=-1, keepdims=True) + EPS)
        xh = x2 * r
        g = g_ref[...]
        err = xh * g - t_ref[...]
        dy = err * (1.0 / D)
        dyg = dy * g
        dx = r * (dyg - xh * jnp.mean(dyg * xh, axis=-1, keepdims=True))
        dx_ref[...] = dx
        dxb_ref[...] = dx.astype(BF16)
        part = 0.5 * jnp.sum(jnp.mean(err * err, axis=-1, keepdims=True), axis=0, keepdims=True)

        @pl.when(i == 0)
        def _():
            dg_ref[...] = jnp.zeros_like(dg_ref)
            loss_ref[...] = jnp.zeros_like(loss_ref)

        dg_ref[...] += jnp.sum(dy * xh, axis=0, keepdims=True)
        loss_ref[...] += jnp.broadcast_to(part, loss_ref.shape)

    return pl.pallas_call(
        body, name="ffn_out_loss", grid=(T // tm,),
        in_specs=[_rows(tm, D), _rows(tm, DFF), _whole((DFF, D)), _whole((1, D)), _rows(tm, D)],
        out_specs=[_rows(tm, D), _rows(tm, D), _whole((1, D)), _whole((1, 128))],
        out_shape=[jax.ShapeDtypeStruct((T, D), F32), jax.ShapeDtypeStruct((T, D), BF16),
                   jax.ShapeDtypeStruct((1, D), F32), jax.ShapeDtypeStruct((1, 128), F32)],
        compiler_params=_params("arbitrary"),
    )(x1, act, w_dn, g_final, target)


def _const(shape):
    return pl.BlockSpec(shape, lambda i: tuple(0 for _ in shape), pipeline_mode=pl.Buffered(1))


def _ffn_bwd(dx2, dx2b, gu, x1, g_ffn, w_dn_t, w_fi_t):
    tm = ROW_TM // 2

    def body(dx_ref, dxb_ref, gu_ref, x_ref, g_ref, wdn_ref, wfi_ref,
             dgu_ref, dx1_ref, dx1b_ref, dg_ref):
        i = pl.program_id(0)
        dxb = dxb_ref[...]
        dh = jnp.zeros((tm, D), F32)
        for k in range(N_CHIPS // 2):
            c0 = k * FSH
            dact = _dot_nt(dxb, wdn_ref[c0:c0 + FSH, :])
            gate = gu_ref[:, 2 * c0:2 * c0 + FSH].astype(F32)
            up = gu_ref[:, 2 * c0 + FSH:2 * c0 + 2 * FSH].astype(F32)
            s = _sigmoid(gate)
            dup = (dact * gate * s).astype(BF16)
            dgate = (dact * up * s * (1.0 + gate * (1.0 - s))).astype(BF16)
            dgu_ref[:, c0:c0 + FSH] = dgate
            dgu_ref[:, DFF + c0:DFF + c0 + FSH] = dup
            dh = dh + _dot_nt(dgate, wfi_ref[k]) + _dot_nt(dup, wfi_ref[k + N_CHIPS // 2])
        xv = x_ref[...]
        r = lax.rsqrt(jnp.mean(xv * xv, axis=-1, keepdims=True) + EPS)
        xh = xv * r
        dhg = dh * g_ref[...]
        dx1 = dx_ref[...] + r * (dhg - xh * jnp.mean(dhg * xh, axis=-1, keepdims=True))
        dx1_ref[...] = dx1
        dx1b_ref[...] = dx1.astype(BF16)

        @pl.when(i == 0)
        def _():
            dg_ref[...] = jnp.zeros_like(dg_ref)

        dg_ref[...] += jnp.sum(dh * xh, axis=0, keepdims=True)

    return pl.pallas_call(
        body, name="ffn_bwd", grid=(T // tm,),
        in_specs=[_rows(tm, D), _rows(tm, D), _rows(tm, 2 * DFF), _rows(tm, D), _whole((1, D)),
                  _const((DFF, D)), _const((N_CHIPS, D, FSH))],
        out_specs=[_rows(tm, 2 * DFF), _rows(tm, D), _rows(tm, D), _whole((1, D))],
        out_shape=[jax.ShapeDtypeStruct((T, 2 * DFF), BF16), jax.ShapeDtypeStruct((T, D), F32),
                   jax.ShapeDtypeStruct((T, D), BF16), jax.ShapeDtypeStruct((1, D), F32)],
        compiler_params=_params("arbitrary"),
    )(dx2, dx2b, gu, x1, g_ffn, w_dn_t, w_fi_t)


def _mix_bwd(dx1b, gl, ya, yc, w_out, w_ap, w_cp, dep):
    tm = ROW_TM

    def body(dx_ref, gl_ref, ya_ref, yc_ref, wo_ref, wap_ref, wcp_ref, dep_ref,
             dya_ref, dyc_ref, dgl_ref, do_ref, dc_ref, db_ref):
        i = pl.program_id(0)
        dm = _dot_nt(dx_ref[...], wo_ref[...])
        ya, yc = ya_ref[...].astype(F32), yc_ref[...].astype(F32)
        g0 = _sigmoid(gl_ref[:, 0:D].astype(F32))
        g1 = _sigmoid(gl_ref[:, D:2 * D].astype(F32))
        dya = dm * g0
        dyc = dm * g1
        dgl_ref[:, 0:D] = (dya * ya * (1.0 - g0)).astype(BF16)
        dgl_ref[:, D:2 * D] = (dyc * yc * (1.0 - g1)).astype(BF16)
        dyab = dya.astype(BF16)
        dycb = dyc.astype(BF16)
        dya_ref[...] = dyab
        dyc_ref[...] = dycb
        sw = D // N_CHIPS
        do = jnp.zeros((tm, AW), F32)
        dcv = jnp.zeros((tm, C), F32)
        for s in range(N_CHIPS):
            do = do + _dot_nt(dyab[:, s * sw:(s + 1) * sw], wap_ref[s])
            dcv = dcv + _dot_nt(dycb[:, s * sw:(s + 1) * sw], wcp_ref[s])
        do_ref[...] = do.astype(BF16)
        dc_ref[...] = dcv.astype(BF16)

        @pl.when(i == 0)
        def _():
            db_ref[...] = jnp.zeros_like(db_ref)

        db_ref[...] += jnp.sum(dyc, axis=0, keepdims=True)

    return pl.pallas_call(
        body, name="mix_bwd", grid=(T // tm,),
        in_specs=[_rows(tm, D), _rows(tm, 2 * D), _rows(tm, D), _rows(tm, D),
                  _whole((D, D)), _whole((N_CHIPS, AW, D // N_CHIPS)), _whole((N_CHIPS, C, D // N_CHIPS)),
                  _whole((8, 128))],
        out_specs=[_rows(tm, D), _rows(tm, D), _rows(tm, 2 * D), _rows(tm, AW), _rows(tm, C),
                   _whole((1, D))],
        out_shape=[jax.ShapeDtypeStruct((T, D), BF16), jax.ShapeDtypeStruct((T, D), BF16),
                   jax.ShapeDtypeStruct((T, 2 * D), BF16), jax.ShapeDtypeStruct((T, AW), BF16),
                   jax.ShapeDtypeStruct((T, C), BF16), jax.ShapeDtypeStruct((1, D), F32)],
        compiler_params=_params("arbitrary"),
    )(dx1b, gl, ya, yc, w_out, w_ap, w_cp, dep)


def _conv_bwd(glu, u, dc, conv_w, ln_g, ln_b, dep):
    tm = CONV_TM
    nblk = T // tm

    def du_of(uv, dcv, g_ref, b_ref):
        mu = jnp.mean(uv, axis=-1, keepdims=True)
        xc = uv - mu
        var = jnp.mean(xc * xc, axis=-1, keepdims=True)
        rstd = lax.rsqrt(var + EPS)
        xh = xc * rstd
        y = xh * g_ref[...] + b_ref[...]
        sg = _sigmoid(y)
        dy = dcv * (sg * (1.0 + y * (1.0 - sg)))
        dxh = dy * g_ref[...]
        du = rstd * (dxh - jnp.mean(dxh, axis=-1, keepdims=True)
                     - xh * jnp.mean(dxh * xh, axis=-1, keepdims=True))
        return du, dy, xh

    def body(cur_ref, prev_ref, u_ref, un_ref, dc_ref, dcn_ref, w_ref, g_ref, b_ref, dep_ref,
             dglu_ref, dw_ref, dcb_ref, dg_ref, db_ref, zs_ref, dus_ref):
        i = pl.program_id(0)

        @pl.when(i == 0)
        def _():
            dw_ref[...] = jnp.zeros_like(dw_ref)
            dcb_ref[...] = jnp.zeros_like(dcb_ref)
            dg_ref[...] = jnp.zeros_like(dg_ref)
            db_ref[...] = jnp.zeros_like(db_ref)

        zprev = _glu(prev_ref[tm - HALO:tm, :])
        zs_ref[0, 0:HALO, :] = jnp.where(i > 0, zprev, 0.0)
        zs_ref[0, HALO:HALO + tm, :] = _glu(cur_ref[...])
        _shifted_copies(zs_ref)

        dun, _, _ = du_of(un_ref[0:HALO, :], dcn_ref[0:HALO, :].astype(F32), g_ref, b_ref)
        dus_ref[0, tm:tm + HALO, :] = jnp.where(i < nblk - 1, dun, 0.0)
        dg_acc = jnp.zeros((1, C), F32)
        db_acc = jnp.zeros((1, C), F32)
        dcb_acc = jnp.zeros((1, C), F32)
        for s in range(tm // CONV_SUB):
            rows = slice(s * CONV_SUB, (s + 1) * CONV_SUB)
            du, dy, xh = du_of(u_ref[rows, :], dc_ref[rows, :].astype(F32), g_ref, b_ref)
            dus_ref[0, rows, :] = du
            dg_acc = dg_acc + jnp.sum(dy * xh, axis=0, keepdims=True)
            db_acc = db_acc + jnp.sum(dy, axis=0, keepdims=True)
            dcb_acc = dcb_acc + jnp.sum(du, axis=0, keepdims=True)
        dg_ref[...] += dg_acc
        db_ref[...] += db_acc
        dcb_ref[...] += dcb_acc
        _shifted_copies(dus_ref)

        for j in range(KW):
            acc = jnp.zeros((CONV_SUB, C), F32)
            for s in range(tm // CONV_SUB):
                base = HALO + s * CONV_SUB - (KW - 1) + j
                acc = acc + dus_ref[0, s * CONV_SUB:(s + 1) * CONV_SUB, :] * _shifted_rows(zs_ref, base, CONV_SUB)
            dw_ref[j:j + 1, :] += jnp.sum(acc, axis=0, keepdims=True)

        for s in range(tm // CONV_SUB):
            rows = slice(s * CONV_SUB, (s + 1) * CONV_SUB)
            dz = jnp.zeros((CONV_SUB, C), F32)
            for j in range(KW):
                dz = dz + w_ref[j:j + 1, :] * _shifted_rows(dus_ref, s * CONV_SUB + (KW - 1) - j, CONV_SUB)
            a = cur_ref[rows, 0:C].astype(F32)
            sb = _sigmoid(cur_ref[rows, C:2 * C].astype(F32))
            dglu_ref[rows, 0:C] = (dz * sb).astype(BF16)
            dglu_ref[rows, C:2 * C] = (dz * a * sb * (1.0 - sb)).astype(BF16)

    nxt = lambda i: (jnp.minimum(i + 1, nblk - 1), 0)
    return pl.pallas_call(
        body, name="conv_bwd", grid=(nblk,),
        in_specs=[_rows(tm, 2 * C),
                  pl.BlockSpec((tm, 2 * C), lambda i: (jnp.maximum(i - 1, 0), 0)),
                  _rows(tm, C), pl.BlockSpec((tm, C), nxt),
                  _rows(tm, C), pl.BlockSpec((tm, C), nxt),
                  _whole((KW, C)), _whole((1, C)), _whole((1, C)), _whole((8, 128))],
        out_specs=[_rows(tm, 2 * C), _whole((KW, C)), _whole((1, C)), _whole((1, C)), _whole((1, C))],
        out_shape=[jax.ShapeDtypeStruct((T, 2 * C), BF16), jax.ShapeDtypeStruct((KW, C), F32),
                   jax.ShapeDtypeStruct((1, C), F32), jax.ShapeDtypeStruct((1, C), F32),
                   jax.ShapeDtypeStruct((1, C), F32)],
        scratch_shapes=[pltpu.VMEM((SUBLANES, HALO + tm, C), F32), pltpu.VMEM((SUBLANES, tm + HALO, C), F32)],
        compiler_params=_params("arbitrary"),
    )(glu, glu, u, u, dc, dc, conv_w, ln_g, ln_b, dep)


def _attn_bwd(qkv, do, lse, p, sinks, dep):
    nsub = 1

    def body(sink_ref, qkv_ref, do_ref, lse_ref, p_ref, dep_ref, dq_ref, dkv_ref, ds_ref,
             dp_ref, dsb_ref):
        i = pl.program_id(0)

        @pl.when(i == 0)
        def _():
            dkv_ref[...] = jnp.zeros_like(dkv_ref)
            ds_ref[...] = jnp.zeros_like(ds_ref)

        slots = [(sub, g) for sub in range(nsub) for g in range(NKV)]
        bands = [_band(i * nsub + sub) for sub in range(nsub)]
        r0s = [pl.multiple_of((i * nsub + sub) * BLK, BLK) for sub in range(nsub)]
        lses = []
        for n, (sub, g) in enumerate(slots):
            rb, blk = bands[sub][0], slice(sub * BLK, (sub + 1) * BLK)
            vband = qkv_ref[pl.ds(rb, BAND), AW + KVW + g * HD:AW + KVW + (g + 1) * HD]
            lse_parts = []
            for hh in range(GROUP):
                h = g * GROUP + hh
                dp_ref[n, hh * BLK:(hh + 1) * BLK, :] = _dot_nt(do_ref[blk, h * HD:(h + 1) * HD], vband)
                lse_parts.append(lse_ref[blk, h:h + 1])
            lses.append(jnp.concatenate(lse_parts, axis=0))
        dsinks = []
        for n, (sub, g) in enumerate(slots):
            pf, dpv = p_ref[n].astype(F32), dp_ref[n]
            dl = jnp.sum(pf * dpv, axis=-1, keepdims=True)
            dsb_ref[n] = (pf * (dpv - dl)).astype(BF16)
            dsinks.append(-(jnp.exp(_sink_column(sink_ref, g) - lses[n]) * dl))
        for n, (sub, g) in enumerate(slots):
            rb, blk = bands[sub][0], slice(sub * BLK, (sub + 1) * BLK)
            kband = qkv_ref[pl.ds(rb, BAND), AW + g * HD:AW + (g + 1) * HD]
            dk = jnp.zeros((BAND, HD), F32)
            dv = jnp.zeros((BAND, HD), F32)
            for hh in range(GROUP):
                h = g * GROUP + hh
                hcol = slice(h * HD, (h + 1) * HD)
                rows = slice(hh * BLK, (hh + 1) * BLK)
                dq_ref[blk, hcol] = (_dot(dsb_ref[n, rows, :], kband) * SCALE).astype(BF16)
                dk = dk + _dot_tn(dsb_ref[n, rows, :], qkv_ref[pl.ds(r0s[sub], BLK), hcol])
                dv = dv + _dot_tn(p_ref[n, rows, :], do_ref[blk, hcol])
                ds_ref[h:h + 1, :] += jnp.broadcast_to(
                    jnp.sum(dsinks[n][rows], axis=0, keepdims=True), (1, 128))
            dkv_ref[pl.ds(rb, BAND), g * HD:(g + 1) * HD] += dk * SCALE
            dkv_ref[pl.ds(rb, BAND), KVW + g * HD:KVW + (g + 1) * HD] += dv

    nslot, tq = nsub * NKV, nsub * BLK
    return pl.pallas_call(
        body, name="attn_bwd", grid=(T // tq,),
        in_specs=[pl.BlockSpec(memory_space=pltpu.SMEM), _whole((T, QKVW)),
                  _rows(tq, AW), _rows(tq, NQ),
                  pl.BlockSpec((nslot, GROWS, BAND), lambda i: (i, 0, 0)), _whole((8, 128))],
        out_specs=[_rows(tq, AW), _whole((T, 2 * KVW)), _whole((NQ, 128))],
        out_shape=[jax.ShapeDtypeStruct((T, AW), BF16), jax.ShapeDtypeStruct((T, 2 * KVW), F32),
                   jax.ShapeDtypeStruct((NQ, 128), F32)],
        scratch_shapes=[pltpu.VMEM((nslot, GROWS, BAND), F32), pltpu.VMEM((nslot, GROWS, BAND), BF16)],
        compiler_params=_params("arbitrary"),
    )(sinks, qkv, do, lse, p, dep)


PROJ_PARTS = [(0, AW), (AW, QKVW), (GLU_OFF, GATE_OFF), (GATE_OFF, INW)]


def _in_proj_bwd(dq, dkv, dglu, dgl, x, dx1, g_mix, w_in_t, dep):
    tm = ROW_TM

    def body(dq_ref, dkv_ref, dglu_ref, dgl_ref, x_ref, dx1_ref, g_ref, w_ref, dep_ref,
             gx_ref, dg_ref, db_ref):
        i = pl.program_id(0)

        @pl.when(i == 0)
        def _():
            dg_ref[...] = jnp.zeros_like(dg_ref)
            db_ref[...] = jnp.zeros_like(db_ref)

        dh = jnp.zeros((tm, D), F32)
        for part_ref, (lo, hi) in zip((dq_ref, dkv_ref, dglu_ref, dgl_ref), PROJ_PARTS):
            part = part_ref[...]
            dh = dh + _dot(part.astype(BF16), w_ref[lo:hi, :])
            db_ref[:, lo:hi] += jnp.sum(part.astype(F32), axis=0, keepdims=True)
        xv = x_ref[...]
        r = lax.rsqrt(jnp.mean(xv * xv, axis=-1, keepdims=True) + EPS)
        xh = xv * r
        dhg = dh * g_ref[...]
        gx_ref[...] = dx1_ref[...] + r * (dhg - xh * jnp.mean(dhg * xh, axis=-1, keepdims=True))
        dg_ref[...] += jnp.sum(dh * xh, axis=0, keepdims=True)

    return pl.pallas_call(
        body, name="in_proj_bwd", grid=(T // tm,),
        in_specs=[_rows(tm, AW), _rows(tm, 2 * KVW), _rows(tm, 2 * C), _rows(tm, 2 * D),
                  _rows(tm, D), _rows(tm, D), _whole((1, D)), _const((INW, D)), _whole((8, 128))],
        out_specs=[_rows(tm, D), _whole((1, D)), _whole((1, INW))],
        out_shape=[jax.ShapeDtypeStruct((T, D), F32), jax.ShapeDtypeStruct((1, D), F32),
                   jax.ShapeDtypeStruct((1, INW), F32)],
        compiler_params=_params("arbitrary"),
    )(dq, dkv, dglu, dgl, x, dx1, g_mix, w_in_t, dep)


def _grad_w_in_t(h, dq, dkv, dglu, dgl):
    tn, chunk = 512, 256

    def body(h_ref, dq_ref, dkv_ref, dglu_ref, dgl_ref, o_ref):
        hv = h_ref[...]
        for part_ref, (lo, hi) in zip((dq_ref, dkv_ref, dglu_ref, dgl_ref), PROJ_PARTS):
            for c0 in range(0, hi - lo, chunk):
                o_ref[lo + c0:lo + c0 + chunk, :] = _dot_tn(
                    part_ref[:, c0:c0 + chunk].astype(BF16), hv).astype(BF16)

    return pl.pallas_call(
        body, name="grad_w_in", grid=(D // tn,),
        in_specs=[pl.BlockSpec((T, tn), lambda j: (0, j)), _const((T, AW)), _const((T, 2 * KVW)),
                  _const((T, 2 * C)), _const((T, 2 * D))],
        out_specs=pl.BlockSpec((INW, tn), lambda j: (0, j)),
        out_shape=jax.ShapeDtypeStruct((INW, D), BF16),
        compiler_params=_params("parallel"),
    )(h, dq, dkv, dglu, dgl)


def _grad_w(a, b, name, tk, tn, col_sharded):
    k, n = a.shape[1], b.shape[1]

    single = n == tn
    sw = n // N_CHIPS

    def body(a_ref, b_ref, o_ref, at_ref):
        if single:
            res = _dot_tn(a_ref[...], b_ref[...]).astype(BF16)
            if col_sharded:
                for s in range(N_CHIPS):
                    o_ref[s] = res[:, s * sw:(s + 1) * sw]
            else:
                o_ref[...] = res
            return

        @pl.when(pl.program_id(1) == 0)
        def _():
            at_ref[...] = a_ref[...].T

        o_ref[...] = _dot(at_ref[...], b_ref[...]).astype(BF16)

    if col_sharded and single:
        shape = (N_CHIPS, k, sw)
        out_spec = pl.BlockSpec((N_CHIPS, tk, sw), lambda i, j: (0, i, 0))
    elif col_sharded:
        per = sw // tn
        shape = (N_CHIPS, k, sw)
        out_spec = pl.BlockSpec((None, tk, tn), lambda i, j: (j // per, i, j % per))
    else:
        shape = (1, k, n)
        out_spec = pl.BlockSpec((None, tk, tn), lambda i, j: (0, i, j))
    out = pl.pallas_call(
        body, name=name, grid=(k // tk, n // tn),
        in_specs=[pl.BlockSpec((T, tk), lambda i, j: (0, i)), pl.BlockSpec((T, tn), lambda i, j: (0, j))],
        out_specs=out_spec,
        out_shape=jax.ShapeDtypeStruct(shape, BF16),
        scratch_shapes=[pltpu.VMEM((tk, T), BF16)],
        compiler_params=_params("parallel", "arbitrary"),
    )(a, b)
    return out if col_sharded else out.reshape(N_CHIPS, k // N_CHIPS, n)


HBM_SPEC = pl.BlockSpec(memory_space=pltpu.HBM)


def _place():
    x, y, c = lax.axis_index("x"), lax.axis_index("y"), lax.axis_index("c")
    chips = [(1 - x, y), (x, 1 - y), (1 - x, 1 - y)]
    return x, y, c, chips


SEM_SPEC = pl.BlockSpec(memory_space=pltpu.SEMAPHORE)
ANY_SPEC = pl.BlockSpec(memory_space=pl.ANY)
VMEM_SPEC = pl.BlockSpec(memory_space=pltpu.VMEM)
EFFECT = pltpu.SideEffectType.DATAFLOW_SIDE_EFFECTING


def _gather_ends(src, land, x, y, c, chips):
    kh = src.shape[0] // 2
    s_me = 2 * x + y
    ends = [(src.at[pl.ds(c * kh, kh)], land.at[s_me, pl.ds(c * kh, kh)], (*chip, c)) for chip in chips]
    return ends + [(src, land.at[s_me], (x, y, 1 - c))]


def _reduce_ends(src, land, x, y, c, chips):
    return [(src.at[2 * chip[0] + chip[1]], land.at[j], (*chip, c)) for j, chip in enumerate(chips)]


def _chip_copies(ends, srcs, lands, send_sems, recv_sems, first=0):
    x, y, c, chips = _place()
    copies = []
    for src, land in zip(srcs, lands):
        peers = ends(src, land, x, y, c, chips)
        for s, d, to in peers:
            k = first * len(peers) + len(copies)
            copies.append(pltpu.make_async_remote_copy(
                src_ref=s, dst_ref=d, send_sem=send_sems.at[k], recv_sem=recv_sems.at[k],
                device_id=to, device_id_type=MESH))
    return copies


GATHER_PEERS, REDUCE_PEERS = 4, 3


def _chip_start(name, ends, peers, srcs, lands):
    n = len(srcs)

    def body(*refs):
        copies = _chip_copies(ends, refs[:n], refs[n:2 * n], refs[2 * n], refs[2 * n + 1])
        for cp in copies:
            cp.start()
        token = refs[-1]
        token[...] = jnp.zeros_like(token)

    hbm = lambda a: pltpu.HBM(a.shape, a.dtype)
    res = pl.pallas_call(
        body, name=name,
        out_shape=(pltpu.SemaphoreType.DMA((peers * n,)), pltpu.SemaphoreType.DMA((peers * n,)),
                   *[hbm(a) for a in srcs], *[hbm(a) for a in lands],
                   jax.ShapeDtypeStruct((8, 128), F32)),
        in_specs=[HBM_SPEC] * (2 * n),
        out_specs=(SEM_SPEC, SEM_SPEC, *[HBM_SPEC] * (2 * n), VMEM_SPEC),
        input_output_aliases={i: 2 + i for i in range(2 * n)},
        compiler_params=pltpu.CompilerParams(has_side_effects=EFFECT),
    )(*[pltpu.with_memory_space_constraint(a, pltpu.HBM) for a in (*srcs, *lands)])
    return res[0], res[1], list(res[2:2 + n]), list(res[2 + n:2 + 2 * n]), res[-1]


def _chip_wait(name, ends, send_sems, recv_sems, srcs, lands, after, first=0):
    n, na = len(srcs), len(after)

    def body(*refs):
        copies = _chip_copies(ends, refs[:n], refs[n:2 * n], refs[2 * n], refs[2 * n + 1], first)
        for cp in copies:
            cp.wait_send()
            cp.wait_recv()

    hbm = lambda a: pltpu.HBM(a.shape, a.dtype)
    res = pl.pallas_call(
        body, name=name,
        out_shape=tuple(hbm(a) for a in (*srcs, *lands)),
        in_specs=[HBM_SPEC] * (2 * n) + [SEM_SPEC, SEM_SPEC] + [ANY_SPEC] * na,
        out_specs=tuple([HBM_SPEC] * (2 * n)),
        input_output_aliases={i: i for i in range(2 * n)},
        compiler_params=pltpu.CompilerParams(has_side_effects=EFFECT),
    )(*srcs, *lands, send_sems, recv_sems, *after)
    return list(res[:n]), list(res[n:])


def _forward_copies(lands, send_sems, recv_sems):
    x, y, c, chips = _place()
    copies = []
    for land in lands:
        kh = land.shape[1] // 2
        for chip in chips:
            blk = land.at[2 * chip[0] + chip[1], pl.ds(c * kh, kh)]
            k = len(copies)
            copies.append(pltpu.make_async_remote_copy(
                src_ref=blk, dst_ref=blk, send_sem=send_sems.at[k], recv_sem=recv_sems.at[k],
                device_id=(x, y, 1 - c), device_id_type=MESH))
    return copies


def _gather_relay(name, send_sems, recv_sems, srcs, lands, after, first):
    n, na = len(srcs), len(after)

    def body(*refs):
        land_refs = refs[n:2 * n]
        for cp in _chip_copies(_gather_ends, refs[:n], land_refs, refs[2 * n], refs[2 * n + 1], first):
            cp.wait_send()
            cp.wait_recv()
        out = refs[2 * n + 2 + na:]
        for cp in _forward_copies(land_refs, out[0], out[1]):
            cp.start()
        out[-1][...] = jnp.zeros_like(out[-1])

    hbm = lambda a: pltpu.HBM(a.shape, a.dtype)
    res = pl.pallas_call(
        body, name=name,
        out_shape=(pltpu.SemaphoreType.DMA((3 * n,)), pltpu.SemaphoreType.DMA((3 * n,)),
                   *[hbm(a) for a in lands], jax.ShapeDtypeStruct((8, 128), F32)),
        in_specs=[HBM_SPEC] * (2 * n) + [SEM_SPEC, SEM_SPEC] + [ANY_SPEC] * na,
        out_specs=(SEM_SPEC, SEM_SPEC, *[HBM_SPEC] * n, VMEM_SPEC),
        input_output_aliases={n + i: 2 + i for i in range(n)},
        compiler_params=pltpu.CompilerParams(has_side_effects=EFFECT),
    )(*srcs, *lands, send_sems, recv_sems, *after)
    return res[0], res[1], list(res[2:2 + n]), res[-1]


def _forward_wait(name, send_sems, recv_sems, lands, after):
    n, na = len(lands), len(after)

    def body(*refs):
        for cp in _forward_copies(refs[:n], refs[n], refs[n + 1]):
            cp.wait_send()
            cp.wait_recv()

    hbm = lambda a: pltpu.HBM(a.shape, a.dtype)
    res = pl.pallas_call(
        body, name=name,
        out_shape=tuple(hbm(a) for a in lands),
        in_specs=[HBM_SPEC] * n + [SEM_SPEC, SEM_SPEC] + [ANY_SPEC] * na,
        out_specs=tuple([HBM_SPEC] * n),
        input_output_aliases={i: i for i in range(n)},
        compiler_params=pltpu.CompilerParams(has_side_effects=EFFECT),
    )(*lands, send_sems, recv_sems, *after)
    return list(res)


def _exchange_ends(src, land, x, y, c, chips):
    kh = src.shape[1] // 2
    return [(src.at[:, pl.ds((1 - c) * kh, kh)], land, (x, y, 1 - c))]


def _share_ends(src, land, x, y, c, chips):
    return [(src, land, (x, y, 1 - c))]


def _small_ends(src, land, x, y, c, chips):
    m = src.shape[0]
    rows = land.at[pl.ds((4 * x + 2 * y + c) * m, m)]
    peers = [(x, y, 1 - c)] + [(*chip, c) for chip in chips] + [(*chip, 1 - c) for chip in chips]
    return [(src, rows, to) for to in peers]


PAIR_PEERS, SMALL_PEERS = 1, 7


def _row_tile(k):
    for t in (256, 240, 128, 176, 64, 32, 16):
        if k % t == 0:
            return t
    raise ValueError(k)


def _pair_sum(c_idx, g, got, name):
    _, k, n = g.shape
    kh = k // 2
    tm = _row_tile(kh)
    nb = kh // tm

    def body(c_ref, g_ref, r_ref, o_ref):
        o_ref[...] = (g_ref[...].astype(F32) + r_ref[...].astype(F32)).astype(BF16)

    return pl.pallas_call(
        body, name=name,
        grid_spec=pltpu.PrefetchScalarGridSpec(
            num_scalar_prefetch=1, grid=(nb,),
            in_specs=[pl.BlockSpec((N_CHIPS, tm, n), lambda i, c_ref: (0, c_ref[0] * nb + i, 0)),
                      pl.BlockSpec((N_CHIPS, tm, n), lambda i, c_ref: (0, i, 0))],
            out_specs=pl.BlockSpec((N_CHIPS, tm, n), lambda i, c_ref: (0, i, 0))),
        out_shape=jax.ShapeDtypeStruct((N_CHIPS, kh, n), BF16),
        compiler_params=_params("parallel"),
    )(c_idx, g, got)


def _chip_sum(s_idx, mine, got, name):
    _, kh, n = mine.shape
    tm = _row_tile(kh)

    def body(s_ref, m_ref, r_ref, o_ref):
        acc = m_ref[0].astype(F32)
        for j in range(3):
            acc = acc + r_ref[j].astype(F32)
        o_ref[...] = acc

    return pl.pallas_call(
        body, name=name,
        grid_spec=pltpu.PrefetchScalarGridSpec(
            num_scalar_prefetch=1, grid=(kh // tm,),
            in_specs=[pl.BlockSpec((1, tm, n), lambda i, s_ref: (s_ref[0], i, 0)),
                      pl.BlockSpec((3, tm, n), lambda i, s_ref: (0, i, 0))],
            out_specs=pl.BlockSpec((tm, n), lambda i, s_ref: (i, 0))),
        out_shape=jax.ShapeDtypeStruct((kh, n), F32),
        compiler_params=_params("parallel"),
    )(s_idx, mine, got)


def _adamw_math(w, g, m, v):
    m = ADAM_B1 * m + (1.0 - ADAM_B1) * g
    v = ADAM_B2 * v + (1.0 - ADAM_B2) * (g * g)
    m_hat = m / (1.0 - ADAM_B1 ** ADAM_STEP)
    v_hat = v / (1.0 - ADAM_B2 ** ADAM_STEP)
    delta = -ADAM_LR * (m_hat / (jnp.sqrt(v_hat) + ADAM_EPS) + ADAM_WD * w)
    return delta, m, v


def _adamw(c_idx, w, g_mine, g_other, m, v, name):
    k, n = w.shape
    tm = k // 4

    def body(c_ref, w_ref, gm_ref, go_ref, m_ref, v_ref, g_ref, d_ref, mo_ref, vo_ref):
        g = jnp.where(pl.program_id(0) == c_ref[0], gm_ref[...], go_ref[...])
        d, mm, vv = _adamw_math(w_ref[...], g, m_ref[...], v_ref[...])
        g_ref[...] = g
        d_ref[...] = d
        mo_ref[...] = mm
        vo_ref[...] = vv

    full = pl.BlockSpec((tm, n), lambda h, i, c_ref: (2 * h + i, 0))
    mine = pl.BlockSpec((tm, n), lambda h, i, c_ref: (jnp.where(h == c_ref[0], i, 0), 0))
    other = pl.BlockSpec((tm, n), lambda h, i, c_ref: (jnp.where(h == c_ref[0], 0, i), 0))
    shp = jax.ShapeDtypeStruct((k, n), F32)
    return pl.pallas_call(
        body, name=name,
        grid_spec=pltpu.PrefetchScalarGridSpec(
            num_scalar_prefetch=1, grid=(2, 2),
            in_specs=[full, mine, other, full, full], out_specs=[full] * 4),
        out_shape=[shp] * 4, compiler_params=_params("arbitrary", "arbitrary"),
    )(c_idx, w, g_mine, g_other, m, v)


VEC_SLOTS = {
    "g_mix_norm": (0, 0, D), "b_conv_proj": (0, D, D), "g_ffn_norm": (0, 2 * D, D),
    "g_final": (0, 3 * D, D), "b_in": (1, 0, INW), "conv_b": (2, 0, C), "ln_g": (2, C, C),
    "ln_b": (2, 2 * C, C), "sinks": (2, 3 * C, NQ), "loss": (2, 3 * C + 128, 1),
}
VEC_ROWS, VEC_COLS = 8, 4 * D
CW_ROWS = 32
SMALL_NAMES = ["g_mix_norm", "b_in", "sinks", "conv_w", "conv_b", "ln_g", "ln_b",
               "b_conv_proj", "g_ffn_norm", "g_final"]
CW_LANES = C // N_CHIPS


def _pack_small(gs, loss):
    row0 = jnp.concatenate([gs["g_mix_norm"], gs["b_conv_proj"], gs["g_ffn_norm"], gs["g_final"]], axis=1)
    row1 = jnp.pad(gs["b_in"], ((0, 0), (0, VEC_COLS - INW)))
    row2 = jnp.concatenate([gs["conv_b"], gs["ln_g"], gs["ln_b"],
                            jnp.pad(gs["sinks"], ((0, 0), (0, 128 - NQ))),
                            jnp.pad(loss.reshape(1, 1), ((0, 0), (0, VEC_COLS - 3 * C - 129)))], axis=1)
    vec = jnp.concatenate([row0, row1, row2, jnp.zeros((VEC_ROWS - 3, VEC_COLS), F32)], axis=0)
    cw = jnp.pad(gs["conv_w"], ((0, CW_ROWS - KW), (0, 0)))
    return vec, cw


def _small_update(idx, vec_own, cw_own, vec_all, cw_all, wmv):
    nsm = len(SMALL_NAMES)

    def body(s_ref, vown_ref, cown_ref, vec_ref, cw_ref, *refs):
        ins = refs[:3 * nsm]
        outs = refs[3 * nsm:7 * nsm]
        loss_ref = refs[7 * nsm]
        me = s_ref[1]

        def summed(own_ref, table_ref, rows_per_dev, r0, nrows, lane, width):
            acc = None
            for k in range(8):
                piece = jnp.where(me == k, own_ref[r0:r0 + nrows, lane:lane + width],
                                  table_ref[k * rows_per_dev + r0:k * rows_per_dev + r0 + nrows, lane:lane + width])
                acc = piece if acc is None else acc + piece
            return acc

        def total(slot):
            row, lane, width = slot
            return summed(vown_ref, vec_ref, VEC_ROWS, row, 1, lane, width)

        loss_ref[...] = jnp.broadcast_to(total(VEC_SLOTS["loss"]), loss_ref.shape)
        for p, name in enumerate(SMALL_NAMES):
            w_ref, m_ref, v_ref = ins[3 * p:3 * p + 3]
            g_ref, d_ref, mo_ref, vo_ref = outs[4 * p:4 * p + 4]
            if name == "conv_w":
                g = jnp.zeros((KW, CW_LANES), F32)
                for s in range(N_CHIPS):
                    cand = summed(cown_ref, cw_ref, CW_ROWS, 0, KW, s * CW_LANES, CW_LANES)
                    g = jnp.where(s_ref[0] == s, cand, g)
            else:
                g = total(VEC_SLOTS[name])
            d, mm, vv = _adamw_math(w_ref[...], g, m_ref[...], v_ref[...])
            g_ref[...] = g
            d_ref[...] = d
            mo_ref[...] = mm
            vo_ref[...] = vv

    vmem = pl.BlockSpec(memory_space=pltpu.VMEM)
    flat = [a for t in wmv for a in t]
    out_shape = []
    for w, _, _ in wmv:
        out_shape += [jax.ShapeDtypeStruct(w.shape, F32)] * 4
    out_shape.append(jax.ShapeDtypeStruct((1, 128), F32))
    res = pl.pallas_call(
        body, name="small_update",
        in_specs=[pl.BlockSpec(memory_space=pltpu.SMEM)] + [vmem] * (4 + len(flat)),
        out_specs=[vmem] * len(out_shape), out_shape=out_shape,
    )(idx, vec_own, cw_own, vec_all, cw_all, *flat)
    return [tuple(res[4 * p:4 * p + 4]) for p in range(nsm)], res[4 * nsm]


WEIGHT_ORDER = ["g_mix_norm", "w_in", "b_in", "sinks", "conv_w", "conv_b", "ln_g", "ln_b",
                "w_attn_proj", "w_conv_proj", "b_conv_proj", "w_out", "g_ffn_norm", "w_ffn_in",
                "w_ffn_down", "g_final"]


def kernel(x, g_mix_norm, w_in, b_in, sinks, conv_w, conv_b, ln_g, ln_b, w_attn_proj, w_conv_proj, b_conv_proj, w_out, g_ffn_norm, w_ffn_in, w_ffn_down, g_final, loss_target, m_g_mix_norm, m_w_in, m_b_in, m_sinks, m_conv_w, m_conv_b, m_ln_g, m_ln_b, m_w_attn_proj, m_w_conv_proj, m_b_conv_proj, m_w_out, m_g_ffn_norm, m_w_ffn_in, m_w_ffn_down, m_g_final, v_g_mix_norm, v_w_in, v_b_in, v_sinks, v_conv_w, v_conv_b, v_ln_g, v_ln_b, v_w_attn_proj, v_w_conv_proj, v_b_conv_proj, v_w_out, v_g_ffn_norm, v_w_ffn_in, v_w_ffn_down, v_g_final):
    w = dict(g_mix_norm=g_mix_norm, w_in=w_in, b_in=b_in, sinks=sinks, conv_w=conv_w, conv_b=conv_b,
             ln_g=ln_g, ln_b=ln_b, w_attn_proj=w_attn_proj, w_conv_proj=w_conv_proj,
             b_conv_proj=b_conv_proj, w_out=w_out, g_ffn_norm=g_ffn_norm, w_ffn_in=w_ffn_in,
             w_ffn_down=w_ffn_down, g_final=g_final)
    m = dict(g_mix_norm=m_g_mix_norm, w_in=m_w_in, b_in=m_b_in, sinks=m_sinks, conv_w=m_conv_w,
             conv_b=m_conv_b, ln_g=m_ln_g, ln_b=m_ln_b, w_attn_proj=m_w_attn_proj,
             w_conv_proj=m_w_conv_proj, b_conv_proj=m_b_conv_proj, w_out=m_w_out,
             g_ffn_norm=m_g_ffn_norm, w_ffn_in=m_w_ffn_in, w_ffn_down=m_w_ffn_down, g_final=m_g_final)
    v = dict(g_mix_norm=v_g_mix_norm, w_in=v_w_in, b_in=v_b_in, sinks=v_sinks, conv_w=v_conv_w,
             conv_b=v_conv_b, ln_g=v_ln_g, ln_b=v_ln_b, w_attn_proj=v_w_attn_proj,
             w_conv_proj=v_w_conv_proj, b_conv_proj=v_b_conv_proj, w_out=v_w_out,
             g_ffn_norm=v_g_ffn_norm, w_ffn_in=v_w_ffn_in, w_ffn_down=v_w_ffn_down, g_final=v_g_final)

    c_idx = lax.axis_index("c").astype(jnp.int32).reshape(1)
    s_idx = (2 * lax.axis_index("x") + lax.axis_index("y")).astype(jnp.int32).reshape(1)

    out_g, out_d, out_m, out_v = {}, {}, {}, {}

    def gather_start(tag, shards):
        lands = [lax.empty((N_CHIPS,) + s.shape, s.dtype) for s in shards]
        return _chip_start("gather_start_" + tag, _gather_ends, GATHER_PEERS, shards, lands)

    def gather_relay(tag, state, after, first=0, count=None):
        send_sems, recv_sems, shards, lands, _ = state
        last = len(shards) if count is None else first + count
        return _gather_relay("gather_relay_" + tag, send_sems, recv_sems, shards[first:last],
                             lands[first:last], after, first)

    def gather_finish(tag, relay, after):
        return _forward_wait("forward_wait_" + tag, relay[0], relay[1], relay[2], after)

    names_b = ["w_attn_proj", "w_conv_proj", "w_out", "w_ffn_in", "w_ffn_down"]
    big = {name: (w[name][0], m[name][0], v[name][0]) for name in names_b}
    big["w_in"] = (w_in[0].T, m_w_in[0].T, v_w_in[0].T)
    state_a = gather_start("a", [big["w_in"][0].astype(BF16), jnp.pad(conv_w[0], ((0, CW_ROWS - KW), (0, 0)))])
    state_b = gather_start("b", [(big[name][0] + state_a[4][0, 0]).astype(BF16) for name in names_b])
    got_a = gather_finish("a", gather_relay("a", state_a, [state_b[4]]), [])
    w_in_t_full = got_a[0].reshape(INW, D)
    conv_w_full = got_a[1].transpose(1, 0, 2).reshape(CW_ROWS, C)[:KW]

    xs, target = x[0], loss_target[0]
    g_final2 = g_final.reshape(1, D)
    h, qkv, glu, gl = _in_proj(xs, g_mix_norm, w_in_t_full, b_in)
    o, lse, probs = _attn_fwd(qkv, sinks)
    relay_1 = gather_relay("b1", state_b, [o], 0, 3)
    u, cact = _conv_fwd(glu, conv_w_full, conv_b, ln_g, ln_b, relay_1[3])
    w_ap4, w_cp4, w_out4 = gather_finish("b1", relay_1, [cact])
    w_out_full = w_out4.reshape(D, D)
    mixed = _mix_out(xs, o, cact, gl, w_ap4, w_cp4, b_conv_proj, w_out_full, relay_1[3])
    relay_2 = gather_relay("b2", state_b, [mixed[3]], 3, 1)
    ya, yc, mg, x1 = _mix_out(xs, o, cact, gl, w_ap4, w_cp4, b_conv_proj, w_out_full, relay_2[3], mixed)
    w_fi4, = gather_finish("b2", relay_2, [x1])
    h2, gu, act = _ffn_in_first(x1, g_ffn_norm, w_fi4, relay_2[3])
    relay_3 = gather_relay("b3", state_b, [h2], 4, 1)
    gu, act = _ffn_in_second(h2, w_fi4, gu, act, relay_3[3])
    w_dn4, = gather_finish("b3", relay_3, [act])
    w_dn_full = w_dn4.reshape(DFF, D)
    dx2, dx2b, dg_final, loss_part = _ffn_out_loss(x1, act, w_dn_full, g_final2, target)

    def exchange_start(tag, grads):
        lands = [lax.empty((N_CHIPS, g.shape[1] // 2, g.shape[2]), g.dtype) for g in grads]
        return _chip_start("pair_start_" + tag, _exchange_ends, PAIR_PEERS, grads, lands)

    def reduce_start(tag, names, exchange, after):
        send_sems, recv_sems, grads, lands, _ = exchange
        grads, from_sibling = _chip_wait("pair_wait_" + tag, _exchange_ends, send_sems, recv_sems, grads, lands, after)
        pair = [_pair_sum(c_idx, g, r, "pair_sum_" + name) for name, g, r in zip(names, grads, from_sibling)]
        lands = [lax.empty((3,) + p.shape[1:], p.dtype) for p in pair]
        return _chip_start("chip_start_" + tag, _reduce_ends, REDUCE_PEERS, pair, lands)

    def reduce_sum(tag, names, state, after):
        send_sems, recv_sems, pair, lands, _ = state
        pair, lands = _chip_wait("chip_wait_" + tag, _reduce_ends, send_sems, recv_sems, pair, lands, after)
        mine = [_chip_sum(s_idx, p, r, "chip_sum_" + name) for name, p, r in zip(names, pair, lands)]
        others = [lax.empty(a.shape, a.dtype) for a in mine]
        return _chip_start("share_start_" + tag, _share_ends, PAIR_PEERS, mine, others)

    def reduce_finish(tag, names, share, after):
        send_sems, recv_sems, mine, others, _ = share
        mine, others = _chip_wait("share_wait_" + tag, _share_ends, send_sems, recv_sems, mine, others, after)
        for name, g_mine, g_other in zip(names, mine, others):
            wv, mv, vv = big[name]
            res = _adamw(c_idx, wv, g_mine, g_other, mv, vv, "adamw_" + name)
            if name == "w_in":
                res = [a.T for a in res]
            out_g[name], out_d[name], out_m[name], out_v[name] = [a[None] for a in res]

    dgu, dx1, dx1b, dg_ffn = _ffn_bwd(dx2, dx2b, gu, x1, g_ffn_norm, w_dn_full, w_fi4)
    names_1 = ["w_ffn_in", "w_ffn_down", "w_out", "w_attn_proj", "w_conv_proj"]
    grads_1 = [_grad_w(h2, dgu, "grad_w_ffn_in", 512, FSH, True),
               _grad_w(act, dx2b, "grad_w_ffn_down", 256, D, False)]
    dya, dyc, dgl, do, dc, db_cp = _mix_bwd(dx1b, gl, ya, yc, w_out_full, w_ap4, w_cp4, relay_3[3])
    grads_1 += [_grad_w(mg, dx1b, "grad_w_out", 512, D, False),
                _grad_w(o, dya, "grad_w_attn_proj", 512, D, True),
                _grad_w(cact, dyc, "grad_w_conv_proj", 512, D, True)]
    exchange_1 = exchange_start("1", grads_1)
    dq, dkv, dsinks = _attn_bwd(qkv, do, lse, probs, sinks, exchange_1[4])
    state_1 = reduce_start("1", names_1, exchange_1, [dq])
    dglu, dconv_w, dconv_b, dln_g, dln_b = _conv_bwd(glu, u, dc, conv_w_full, ln_g, ln_b, state_1[4])
    names_2 = ["w_in"]
    gw_in_t = _grad_w_in_t(h, dq, dkv, dglu, dgl)
    exchange_2 = exchange_start("2", [gw_in_t.reshape(N_CHIPS, INW // N_CHIPS, D)])
    grad_x, dg_mix, db_in = _in_proj_bwd(dq, dkv, dglu, dgl, xs, dx1, g_mix_norm, w_in_t_full, exchange_2[4])

    gs = {"g_mix_norm": dg_mix, "b_in": db_in, "sinks": dsinks[:, 0].reshape(1, NQ),
          "conv_w": dconv_w, "conv_b": dconv_b, "ln_g": dln_g, "ln_b": dln_b,
          "b_conv_proj": db_cp, "g_ffn_norm": dg_ffn, "g_final": dg_final}
    blocks = list(_pack_small(gs, loss_part[0, 0]))
    tables = [lax.empty((8 * b.shape[0], b.shape[1]), b.dtype) for b in blocks]
    small = _chip_start("small_start", _small_ends, SMALL_PEERS, blocks, tables)

    state_2 = reduce_start("2", names_2, exchange_2, [grad_x, small[4]])
    share_1 = reduce_sum("1", names_1, state_1, [state_2[4]])
    blocks, tables = _chip_wait("small_wait", _small_ends, small[0], small[1], small[2], small[3],
                                [share_1[4]])
    me = (4 * lax.axis_index("x") + 2 * lax.axis_index("y") + lax.axis_index("c")).astype(jnp.int32)

    def view(a, name):
        if name == "conv_w":
            return a[0]
        if name == "g_final":
            return a.reshape(1, D)
        return a

    wmv = [(view(w[name], name), view(m[name], name), view(v[name], name)) for name in SMALL_NAMES]
    small_out, loss_row = _small_update(jnp.concatenate([s_idx, me.reshape(1)]), blocks[0], blocks[1],
                                        tables[0], tables[1], wmv)
    for name, (g, d, mm, vv) in zip(SMALL_NAMES, small_out):
        shape = w[name].shape
        out_g[name], out_d[name], out_m[name], out_v[name] = (
            g.reshape(shape), d.reshape(shape), mm.reshape(shape), vv.reshape(shape))

    reduce_finish("1", names_1, share_1, [loss_row])
    share_2 = reduce_sum("2", names_2, state_2, [out_d["w_conv_proj"]])
    reduce_finish("2", names_2, share_2, [])

    loss = loss_row[0, 0]
    return (loss, grad_x[None], *[out_g[k] for k in WEIGHT_ORDER], *[out_d[k] for k in WEIGHT_ORDER],
            *[out_m[k] for k in WEIGHT_ORDER], *[out_v[k] for k in WEIGHT_ORDER])
```

```python
import functools

import jax
import jax.numpy as jnp
from jax import lax
from jax.experimental import pallas as pl
from jax.experimental.pallas import tpu as pltpu

F32 = jnp.float32
BF16 = jnp.bfloat16

T = 2048
D = 1024
HD = 64
NQ = 8
NKV = 2
GROUP = NQ // NKV
BLK = 128
AW = NQ * HD
KVW = NKV * HD
C = 512
KW = 31
QKVW = AW + 2 * KVW
GLU_OFF = QKVW
GATE_OFF = GLU_OFF + 2 * C
INW = GATE_OFF + 2 * D
DFF = 2816
EPS = 1e-5
NEG = -1e30
SCALE = HD ** -0.5
HALO = 32
N_CHIPS = 4
FSH = 2 * DFF // N_CHIPS

ADAM_LR = 0.001
ADAM_B1 = 0.9
ADAM_B2 = 0.999
ADAM_EPS = 1e-08
ADAM_WD = 0.01
ADAM_STEP = 10

VMEM_LIMIT = 56 * 1024 * 1024
ROW_TM = 512
MESH = pl.DeviceIdType.MESH


def _params(*sem):
    return pltpu.CompilerParams(dimension_semantics=sem, vmem_limit_bytes=VMEM_LIMIT)


def _dot(a, b):
    return jnp.dot(a, b, preferred_element_type=F32)


def _dot_nt(a, b):
    return lax.dot_general(a, b, (((1,), (1,)), ((), ())), preferred_element_type=F32)


def _dot_tn(a, b):
    return lax.dot_general(a, b, (((0,), (0,)), ((), ())), preferred_element_type=F32)


def _sigmoid(v):
    return 1.0 / (1.0 + jnp.exp(-v))


def _rows(tm, n):
    return pl.BlockSpec((tm, n), lambda i: (i, 0))


def _whole(shape):
    return pl.BlockSpec(shape, lambda i: tuple(0 for _ in shape))


def _in_proj(x, g_mix, w_in_t, b_in):
    tm = ROW_TM

    def body(x_ref, g_ref, w_ref, b_ref, h_ref, qkv_ref, glu_ref, gl_ref):
        xv = x_ref[...]
        r = lax.rsqrt(jnp.mean(xv * xv, axis=-1, keepdims=True) + EPS)
        h = (xv * r * g_ref[...]).astype(BF16)
        h_ref[...] = h
        qkv_ref[...] = (_dot_nt(h, w_ref[0:GLU_OFF, :]) + b_ref[:, 0:GLU_OFF]).astype(BF16)
        glu_ref[...] = (_dot_nt(h, w_ref[GLU_OFF:GATE_OFF, :]) + b_ref[:, GLU_OFF:GATE_OFF]).astype(BF16)
        gl_ref[...] = (_dot_nt(h, w_ref[GATE_OFF:INW, :]) + b_ref[:, GATE_OFF:INW]).astype(BF16)

    return pl.pallas_call(
        body, name="in_proj", grid=(T // tm,),
        in_specs=[_rows(tm, D), _whole((1, D)), _whole((INW, D)), _whole((1, INW))],
        out_specs=[_rows(tm, D), _rows(tm, QKVW), _rows(tm, 2 * C), _rows(tm, 2 * D)],
        out_shape=[jax.ShapeDtypeStruct((T, D), BF16), jax.ShapeDtypeStruct((T, QKVW), BF16),
                   jax.ShapeDtypeStruct((T, 2 * C), BF16), jax.ShapeDtypeStruct((T, 2 * D), BF16)],
        compiler_params=_params("parallel"),
    )(x, g_mix, w_in_t, b_in)


GROWS = GROUP * BLK
BAND = 2 * BLK
ATT_SUB = 2


def _band(i):
    rb = pl.multiple_of(jnp.maximum(i - 1, 0) * BLK, BLK)
    row = lax.broadcasted_iota(jnp.int32, (GROWS, BAND), 0)
    kpos = rb + lax.broadcasted_iota(jnp.int32, (GROWS, BAND), 1)
    qpos = i * BLK + jnp.bitwise_and(row, BLK - 1)
    return rb, jnp.logical_and(kpos <= qpos, kpos > qpos - BLK)


def _sink_column(sink_ref, g):
    head = lax.shift_right_logical(lax.broadcasted_iota(jnp.int32, (GROWS, 1), 0), 7)
    col = jnp.full((GROWS, 1), sink_ref[0, g * GROUP], F32)
    for hh in range(1, GROUP):
        col = jnp.where(head == hh, sink_ref[0, g * GROUP + hh], col)
    return col


def _attn_fwd(qkv, sinks):
    def body(sink_ref, qkv_ref, o_ref, lse_ref, p_ref, s_ref):
        slots = [(sub, g) for sub in range(ATT_SUB) for g in range(NKV)]
        bands = [_band(pl.program_id(0) * ATT_SUB + sub) for sub in range(ATT_SUB)]
        for n, (sub, g) in enumerate(slots):
            rb = bands[sub][0]
            r0 = pl.multiple_of((pl.program_id(0) * ATT_SUB + sub) * BLK, BLK)
            kband = qkv_ref[pl.ds(rb, BAND), AW + g * HD:AW + (g + 1) * HD]
            for hh in range(GROUP):
                h = g * GROUP + hh
                s_ref[n, hh * BLK:(hh + 1) * BLK, :] = _dot_nt(
                    qkv_ref[pl.ds(r0, BLK), h * HD:(h + 1) * HD], kband)
        lses = []
        for n, (sub, g) in enumerate(slots):
            s = jnp.where(bands[sub][1], s_ref[n] * SCALE, NEG)
            sink = _sink_column(sink_ref, g)
            m = jnp.maximum(jnp.max(s, axis=-1, keepdims=True), sink)
            p = jnp.exp(s - m)
            den = jnp.sum(p, axis=-1, keepdims=True) + jnp.exp(sink - m)
            p_ref[n] = (p * (1.0 / den)).astype(BF16)
            lses.append(m + jnp.log(den))
        for n, (sub, g) in enumerate(slots):
            rb = bands[sub][0]
            rows = slice(sub * BLK, (sub + 1) * BLK)
            vband = qkv_ref[pl.ds(rb, BAND), AW + KVW + g * HD:AW + KVW + (g + 1) * HD]
            for hh in range(GROUP):
                h = g * GROUP + hh
                o_ref[rows, h * HD:(h + 1) * HD] = _dot(p_ref[n, hh * BLK:(hh + 1) * BLK, :], vband).astype(BF16)
                lse_ref[rows, h:h + 1] = lses[n][hh * BLK:(hh + 1) * BLK]

    nslot = ATT_SUB * NKV
    return pl.pallas_call(
        body, name="attn_fwd", grid=(T // (ATT_SUB * BLK),),
        in_specs=[pl.BlockSpec(memory_space=pltpu.SMEM), _whole((T, QKVW))],
        out_specs=[_rows(ATT_SUB * BLK, AW), _rows(ATT_SUB * BLK, NQ),
                   pl.BlockSpec((nslot, GROWS, BAND), lambda i: (i, 0, 0))],
        out_shape=[jax.ShapeDtypeStruct((T, AW), BF16), jax.ShapeDtypeStruct((T, NQ), F32),
                   jax.ShapeDtypeStruct((T // BLK * NKV, GROWS, BAND), BF16)],
        scratch_shapes=[pltpu.VMEM((nslot, GROWS, BAND), F32)],
        compiler_params=_params("parallel"),
    )(sinks, qkv)


CONV_TM = 256
CONV_SUB = 32


def _glu(ab):
    a = ab[:, 0:C].astype(F32)
    b = ab[:, C:2 * C].astype(F32)
    return a * _sigmoid(b)


SUBLANES = 8


def _shifted_copies(ref):
    rows = ref.shape[1] - SUBLANES
    for r in range(1, SUBLANES):
        ref[r, 0:rows, :] = ref[0, r:r + rows, :]


def _shifted_rows(ref, start, size):
    r = start % SUBLANES
    return ref[r, start - r:start - r + size, :]


def _conv_fwd(glu, conv_w, conv_b, ln_g, ln_b, dep):
    tm = CONV_TM

    def body(cur_ref, prev_ref, w_ref, cb_ref, g_ref, b_ref, dep_ref, u_ref, c_ref, zs_ref):
        i = pl.program_id(0)
        zprev = _glu(prev_ref[tm - HALO:tm, :])
        zs_ref[0, 0:HALO, :] = jnp.where(i > 0, zprev, 0.0)
        zs_ref[0, HALO:HALO + tm, :] = _glu(cur_ref[...])
        _shifted_copies(zs_ref)
        for s in range(tm // CONV_SUB):
            base = HALO + s * CONV_SUB - (KW - 1)
            acc = jnp.broadcast_to(cb_ref[...], (CONV_SUB, C))
            for j in range(KW):
                acc = acc + w_ref[j:j + 1, :] * _shifted_rows(zs_ref, base + j, CONV_SUB)
            rows = slice(s * CONV_SUB, (s + 1) * CONV_SUB)
            u_ref[rows, :] = acc
            mu = jnp.mean(acc, axis=-1, keepdims=True)
            xc = acc - mu
            var = jnp.mean(xc * xc, axis=-1, keepdims=True)
            y = xc * lax.rsqrt(var + EPS) * g_ref[...] + b_ref[...]
            c_ref[rows, :] = (y * _sigmoid(y)).astype(BF16)

    return pl.pallas_call(
        body, name="conv_fwd", grid=(T // tm,),
        in_specs=[_rows(tm, 2 * C),
                  pl.BlockSpec((tm, 2 * C), lambda i: (jnp.maximum(i - 1, 0), 0)),
                  _whole((KW, C)), _whole((1, C)), _whole((1, C)), _whole((1, C)), _whole((8, 128))],
        out_specs=[_rows(tm, C), _rows(tm, C)],
        out_shape=[jax.ShapeDtypeStruct((T, C), F32), jax.ShapeDtypeStruct((T, C), BF16)],
        scratch_shapes=[pltpu.VMEM((SUBLANES, HALO + tm, C), F32)],
        compiler_params=_params("parallel"),
    )(glu, glu, conv_w, conv_b, ln_g, ln_b, dep)


def _branch_outputs(o, cact, wap_ref, wcp_ref, bcp_ref):
    ya = jnp.concatenate([_dot(o, wap_ref[s]) for s in range(N_CHIPS)], axis=1)
    yc = jnp.concatenate([_dot(cact, wcp_ref[s]) for s in range(N_CHIPS)], axis=1) + bcp_ref[...]
    return ya, yc


def _mix_out(x, o, cact, gl, w_ap, w_cp, b_cp, w_out, dep, begun=None):
    tm = ROW_TM
    steps = T // tm // 2
    first = 0 if begun is None else steps
    rows = lambda n: pl.BlockSpec((tm, n), lambda i: (i + first, 0))
    extra = [] if begun is None else list(begun)

    def body(x_ref, o_ref, c_ref, gl_ref, wap_ref, wcp_ref, bcp_ref, wo_ref, dep_ref, *rest):
        ya_ref, yc_ref, mg_ref, x1_ref = rest[len(extra):]
        ya, yc = _branch_outputs(o_ref[...], c_ref[...], wap_ref, wcp_ref, bcp_ref)
        g0 = _sigmoid(gl_ref[:, 0:D].astype(F32))
        g1 = _sigmoid(gl_ref[:, D:2 * D].astype(F32))
        mg = (g0 * ya + g1 * yc).astype(BF16)
        ya_ref[...] = ya.astype(BF16)
        yc_ref[...] = yc.astype(BF16)
        mg_ref[...] = mg
        x1_ref[...] = x_ref[...] + _dot(mg, wo_ref[...])

    return pl.pallas_call(
        body, name="mix_out_first" if begun is None else "mix_out_second", grid=(steps,),
        in_specs=[rows(D), rows(AW), rows(C), rows(2 * D),
                  _whole((N_CHIPS, AW, D // N_CHIPS)), _whole((N_CHIPS, C, D // N_CHIPS)), _whole((1, D)),
                  _whole((D, D)), _whole((8, 128))] + [ANY_SPEC] * len(extra),
        out_specs=[rows(D), rows(D), rows(D), rows(D)],
        out_shape=[jax.ShapeDtypeStruct((T, D), BF16), jax.ShapeDtypeStruct((T, D), BF16),
                   jax.ShapeDtypeStruct((T, D), BF16), jax.ShapeDtypeStruct((T, D), F32)],
        input_output_aliases={9 + k: k for k in range(len(extra))},
        compiler_params=_params("parallel"),
    )(x, o, cact, gl, w_ap, w_cp, b_cp, w_out, dep, *extra)


def _swiglu_half(h, wg_ref, wu_ref, gu_ref, act_ref):
    gate = _dot(h, wg_ref[0])
    up = _dot(h, wu_ref[0])
    gu_ref[:, 0:FSH] = gate.astype(BF16)
    gu_ref[:, FSH:2 * FSH] = up.astype(BF16)
    act_ref[...] = (gate * _sigmoid(gate) * up).astype(BF16)


def _shard_spec(k):
    return pl.BlockSpec((1, D, FSH), lambda i: (k, 0, 0), pipeline_mode=pl.Buffered(1))


def _ffn_in_first(x1, g_ffn, w_fi, dep):
    tm = ROW_TM

    def body(x_ref, g_ref, wg_ref, wu_ref, dep_ref, h_ref, gu_ref, act_ref):
        xv = x_ref[...]
        r = lax.rsqrt(jnp.mean(xv * xv, axis=-1, keepdims=True) + EPS)
        h = (xv * r * g_ref[...]).astype(BF16)
        h_ref[...] = h
        _swiglu_half(h, wg_ref, wu_ref, gu_ref, act_ref)

    return pl.pallas_call(
        body, name="ffn_in_first", grid=(T // tm,),
        in_specs=[_rows(tm, D), _whole((1, D)), _shard_spec(0), _shard_spec(2), _whole((8, 128))],
        out_specs=[_rows(tm, D), pl.BlockSpec((tm, 2 * FSH), lambda i: (i, 0)),
                   pl.BlockSpec((tm, FSH), lambda i: (i, 0))],
        out_shape=[jax.ShapeDtypeStruct((T, D), BF16), jax.ShapeDtypeStruct((T, 2 * DFF), BF16),
                   jax.ShapeDtypeStruct((T, DFF), BF16)],
        compiler_params=_params("parallel"),
    )(x1, g_ffn, w_fi, w_fi, dep)


def _ffn_in_second(h2, w_fi, gu, act, dep):
    tm = ROW_TM

    def body(h_ref, wg_ref, wu_ref, gu_in, act_in, dep_ref, gu_ref, act_ref):
        _swiglu_half(h_ref[...], wg_ref, wu_ref, gu_ref, act_ref)

    return pl.pallas_call(
        body, name="ffn_in_second", grid=(T // tm,),
        in_specs=[_rows(tm, D), _shard_spec(1), _shard_spec(3), ANY_SPEC, ANY_SPEC, _whole((8, 128))],
        out_specs=[pl.BlockSpec((tm, 2 * FSH), lambda i: (i, 1)), pl.BlockSpec((tm, FSH), lambda i: (i, 1))],
        out_shape=[jax.ShapeDtypeStruct((T, 2 * DFF), BF16), jax.ShapeDtypeStruct((T, DFF), BF16)],
        input_output_aliases={3: 0, 4: 1},
        compiler_params=_params("parallel"),
    )(h2, w_fi, w_fi, gu, act, dep)


def _ffn_out_loss(x1, act, w_dn, g_final, target):
    tm = ROW_TM // 2

    def body(x_ref, a_ref, w_ref, g_ref, t_ref, dx_ref, dxb_ref, dg_ref, loss_ref):
        i = pl.program_id(0)
        x2 = x_ref[...] + _dot(a_ref[...], w_ref[...])
        r = lax.rsqrt(jnp.mean(x2 * x2, axis=-1, keepdims=True) + EPS)
        xh = x2 * r
        g = g_ref[...]
        err = xh * g - t_ref[...]
        dy = err * (1.0 / D)
        dyg = dy * g
        dx = r * (dyg - xh * jnp.mean(dyg * xh, axis=-1, keepdims=True))
        dx_ref[...] = dx
        dxb_ref[...] = dx.astype(BF16)
        part = 0.5 * jnp.sum(jnp.mean(err * err, axis=-1, keepdims=True), axis=0, keepdims=True)

        @pl.when(i == 0)
        def _():
            dg_ref[...] = jnp.zeros_like(dg_ref)
            loss_ref[...] = jnp.zeros_like(loss_ref)

        dg_ref[...] += jnp.sum(dy * xh, axis=0, keepdims=True)
        loss_ref[...] += jnp.broadcast_to(part, loss_ref.shape)

    return pl.pallas_call(
        body, name="ffn_out_loss", grid=(T // tm,),
        in_specs=[_rows(tm, D), _rows(tm, DFF), _whole((DFF, D)), _whole((1, D)), _rows(tm, D)],
        out_specs=[_rows(tm, D), _rows(tm, D), _whole((1, D)), _whole((1, 128))],
        out_shape=[jax.ShapeDtypeStruct((T, D), F32), jax.ShapeDtypeStruct((T, D), BF16),
                   jax.ShapeDtypeStruct((1, D), F32), jax.ShapeDtypeStruct((1, 128), F32)],
        compiler_params=_params("arbitrary"),
    )(x1, act, w_dn, g_final, target)


def _const(shape):
    return pl.BlockSpec(shape, lambda i: tuple(0 for _ in shape), pipeline_mode=pl.Buffered(1))


def _ffn_bwd(dx2, dx2b, gu, x1, g_ffn, w_dn_t, w_fi_t):
    tm = ROW_TM // 2

    def body(dx_ref, dxb_ref, gu_ref, x_ref, g_ref, wdn_ref, wfi_ref,
             dgu_ref, dx1_ref, dx1b_ref, dg_ref):
        i = pl.program_id(0)
        dxb = dxb_ref[...]
        dh = jnp.zeros((tm, D), F32)
        dacts = [_dot_nt(dxb, wdn_ref[k * FSH:(k + 1) * FSH, :]) for k in range(N_CHIPS // 2)]
        for k in range(N_CHIPS // 2):
            c0 = k * FSH
            dact = dacts[k]
            gate = gu_ref[:, 2 * c0:2 * c0 + FSH].astype(F32)
            up = gu_ref[:, 2 * c0 + FSH:2 * c0 + 2 * FSH].astype(F32)
            s = _sigmoid(gate)
            dup = (dact * gate * s).astype(BF16)
            dgate = (dact * up * s * (1.0 + gate * (1.0 - s))).astype(BF16)
            dgu_ref[:, c0:c0 + FSH] = dgate
            dgu_ref[:, DFF + c0:DFF + c0 + FSH] = dup
            dh = dh + _dot_nt(dgate, wfi_ref[k]) + _dot_nt(dup, wfi_ref[k + N_CHIPS // 2])
        xv = x_ref[...]
        r = lax.rsqrt(jnp.mean(xv * xv, axis=-1, keepdims=True) + EPS)
        xh = xv * r
        dhg = dh * g_ref[...]
        dx1 = dx_ref[...] + r * (dhg - xh * jnp.mean(dhg * xh, axis=-1, keepdims=True))
        dx1_ref[...] = dx1
        dx1b_ref[...] = dx1.astype(BF16)

        @pl.when(i == 0)
        def _():
            dg_ref[...] = jnp.zeros_like(dg_ref)

        dg_ref[...] += jnp.sum(dh * xh, axis=0, keepdims=True)

    return pl.pallas_call(
        body, name="ffn_bwd", grid=(T // tm,),
        in_specs=[_rows(tm, D), _rows(tm, D), _rows(tm, 2 * DFF), _rows(tm, D), _whole((1, D)),
                  _const((DFF, D)), _const((N_CHIPS, D, FSH))],
        out_specs=[_rows(tm, 2 * DFF), _rows(tm, D), _rows(tm, D), _whole((1, D))],
        out_shape=[jax.ShapeDtypeStruct((T, 2 * DFF), BF16), jax.ShapeDtypeStruct((T, D), F32),
                   jax.ShapeDtypeStruct((T, D), BF16), jax.ShapeDtypeStruct((1, D), F32)],
        compiler_params=_params("arbitrary"),
    )(dx2, dx2b, gu, x1, g_ffn, w_dn_t, w_fi_t)


def _mix_bwd(dx1b, gl, ya, yc, w_out, w_ap, w_cp, dep):
    tm = ROW_TM

    def body(dx_ref, gl_ref, ya_ref, yc_ref, wo_ref, wap_ref, wcp_ref, dep_ref,
             dya_ref, dyc_ref, dgl_ref, do_ref, dc_ref, db_ref):
        i = pl.program_id(0)
        dm = _dot_nt(dx_ref[...], wo_ref[...])
        ya, yc = ya_ref[...].astype(F32), yc_ref[...].astype(F32)
        g0 = _sigmoid(gl_ref[:, 0:D].astype(F32))
        g1 = _sigmoid(gl_ref[:, D:2 * D].astype(F32))
        dya = dm * g0
        dyc = dm * g1
        dgl_ref[:, 0:D] = (dya * ya * (1.0 - g0)).astype(BF16)
        dgl_ref[:, D:2 * D] = (dyc * yc * (1.0 - g1)).astype(BF16)
        dyab = dya.astype(BF16)
        dycb = dyc.astype(BF16)
        dya_ref[...] = dyab
        dyc_ref[...] = dycb
        sw = D // N_CHIPS
        do = jnp.zeros((tm, AW), F32)
        dcv = jnp.zeros((tm, C), F32)
        for s in range(N_CHIPS):
            do = do + _dot_nt(dyab[:, s * sw:(s + 1) * sw], wap_ref[s])
            dcv = dcv + _dot_nt(dycb[:, s * sw:(s + 1) * sw], wcp_ref[s])
        do_ref[...] = do.astype(BF16)
        dc_ref[...] = dcv.astype(BF16)

        @pl.when(i == 0)
        def _():
            db_ref[...] = jnp.zeros_like(db_ref)

        db_ref[...] += jnp.sum(dyc, axis=0, keepdims=True)

    return pl.pallas_call(
        body, name="mix_bwd", grid=(T // tm,),
        in_specs=[_rows(tm, D), _rows(tm, 2 * D), _rows(tm, D), _rows(tm, D),
                  _whole((D, D)), _whole((N_CHIPS, AW, D // N_CHIPS)), _whole((N_CHIPS, C, D // N_CHIPS)),
                  _whole((8, 128))],
        out_specs=[_rows(tm, D), _rows(tm, D), _rows(tm, 2 * D), _rows(tm, AW), _rows(tm, C),
                   _whole((1, D))],
        out_shape=[jax.ShapeDtypeStruct((T, D), BF16), jax.ShapeDtypeStruct((T, D), BF16),
                   jax.ShapeDtypeStruct((T, 2 * D), BF16), jax.ShapeDtypeStruct((T, AW), BF16),
                   jax.ShapeDtypeStruct((T, C), BF16), jax.ShapeDtypeStruct((1, D), F32)],
        compiler_params=_params("arbitrary"),
    )(dx1b, gl, ya, yc, w_out, w_ap, w_cp, dep)


def _conv_bwd(glu, u, dc, conv_w, ln_g, ln_b, dep):
    tm = CONV_TM
    nblk = T // tm

    def du_of(uv, dcv, g_ref, b_ref):
        mu = jnp.mean(uv, axis=-1, keepdims=True)
        xc = uv - mu
        var = jnp.mean(xc * xc, axis=-1, keepdims=True)
        rstd = lax.rsqrt(var + EPS)
        xh = xc * rstd
        y = xh * g_ref[...] + b_ref[...]
        sg = _sigmoid(y)
        dy = dcv * (sg * (1.0 + y * (1.0 - sg)))
        dxh = dy * g_ref[...]
        du = rstd * (dxh - jnp.mean(dxh, axis=-1, keepdims=True)
                     - xh * jnp.mean(dxh * xh, axis=-1, keepdims=True))
        return du, dy, xh

    def body(cur_ref, prev_ref, u_ref, un_ref, dc_ref, dcn_ref, w_ref, g_ref, b_ref, dep_ref,
             dglu_ref, dw_ref, dcb_ref, dg_ref, db_ref, zs_ref, dus_ref):
        i = pl.program_id(0)

        @pl.when(i == 0)
        def _():
            dw_ref[...] = jnp.zeros_like(dw_ref)
            dcb_ref[...] = jnp.zeros_like(dcb_ref)
            dg_ref[...] = jnp.zeros_like(dg_ref)
            db_ref[...] = jnp.zeros_like(db_ref)

        zprev = _glu(prev_ref[tm - HALO:tm, :])
        zs_ref[0, 0:HALO, :] = jnp.where(i > 0, zprev, 0.0)
        zs_ref[0, HALO:HALO + tm, :] = _glu(cur_ref[...])
        _shifted_copies(zs_ref)

        dun, _, _ = du_of(un_ref[0:HALO, :], dcn_ref[0:HALO, :].astype(F32), g_ref, b_ref)
        dus_ref[0, tm:tm + HALO, :] = jnp.where(i < nblk - 1, dun, 0.0)
        dg_acc = jnp.zeros((1, C), F32)
        db_acc = jnp.zeros((1, C), F32)
        dcb_acc = jnp.zeros((1, C), F32)
        for s in range(tm // CONV_SUB):
            rows = slice(s * CONV_SUB, (s + 1) * CONV_SUB)
            du, dy, xh = du_of(u_ref[rows, :], dc_ref[rows, :].astype(F32), g_ref, b_ref)
            dus_ref[0, rows, :] = du
            dg_acc = dg_acc + jnp.sum(dy * xh, axis=0, keepdims=True)
            db_acc = db_acc + jnp.sum(dy, axis=0, keepdims=True)
            dcb_acc = dcb_acc + jnp.sum(du, axis=0, keepdims=True)
        dg_ref[...] += dg_acc
        db_ref[...] += db_acc
        dcb_ref[...] += dcb_acc
        _shifted_copies(dus_ref)

        for j in range(KW):
            acc = jnp.zeros((CONV_SUB, C), F32)
            for s in range(tm // CONV_SUB):
                base = HALO + s * CONV_SUB - (KW - 1) + j
                acc = acc + dus_ref[0, s * CONV_SUB:(s + 1) * CONV_SUB, :] * _shifted_rows(zs_ref, base, CONV_SUB)
            dw_ref[j:j + 1, :] += jnp.sum(acc, axis=0, keepdims=True)

        for s in range(tm // CONV_SUB):
            rows = slice(s * CONV_SUB, (s + 1) * CONV_SUB)
            dz = jnp.zeros((CONV_SUB, C), F32)
            for j in range(KW):
                dz = dz + w_ref[j:j + 1, :] * _shifted_rows(dus_ref, s * CONV_SUB + (KW - 1) - j, CONV_SUB)
            a = cur_ref[rows, 0:C].astype(F32)
            sb = _sigmoid(cur_ref[rows, C:2 * C].astype(F32))
            dglu_ref[rows, 0:C] = (dz * sb).astype(BF16)
            dglu_ref[rows, C:2 * C] = (dz * a * sb * (1.0 - sb)).astype(BF16)

    nxt = lambda i: (jnp.minimum(i + 1, nblk - 1), 0)
    return pl.pallas_call(
        body, name="conv_bwd", grid=(nblk,),
        in_specs=[_rows(tm, 2 * C),
                  pl.BlockSpec((tm, 2 * C), lambda i: (jnp.maximum(i - 1, 0), 0)),
                  _rows(tm, C), pl.BlockSpec((tm, C), nxt),
                  _rows(tm, C), pl.BlockSpec((tm, C), nxt),
                  _whole((KW, C)), _whole((1, C)), _whole((1, C)), _whole((8, 128))],
        out_specs=[_rows(tm, 2 * C), _whole((KW, C)), _whole((1, C)), _whole((1, C)), _whole((1, C))],
        out_shape=[jax.ShapeDtypeStruct((T, 2 * C), BF16), jax.ShapeDtypeStruct((KW, C), F32),
                   jax.ShapeDtypeStruct((1, C), F32), jax.ShapeDtypeStruct((1, C), F32),
                   jax.ShapeDtypeStruct((1, C), F32)],
        scratch_shapes=[pltpu.VMEM((SUBLANES, HALO + tm, C), F32), pltpu.VMEM((SUBLANES, tm + HALO, C), F32)],
        compiler_params=_params("arbitrary"),
    )(glu, glu, u, u, dc, dc, conv_w, ln_g, ln_b, dep)


def _attn_bwd(qkv, do, lse, p, sinks, dep):
    nsub = 1

    def body(sink_ref, qkv_ref, do_ref, lse_ref, p_ref, dep_ref, dq_ref, dkv_ref, ds_ref,
             dp_ref, dsb_ref):
        i = pl.program_id(0)

        @pl.when(i == 0)
        def _():
            dkv_ref[...] = jnp.zeros_like(dkv_ref)
            ds_ref[...] = jnp.zeros_like(ds_ref)

        slots = [(sub, g) for sub in range(nsub) for g in range(NKV)]
        bands = [_band(i * nsub + sub) for sub in range(nsub)]
        r0s = [pl.multiple_of((i * nsub + sub) * BLK, BLK) for sub in range(nsub)]
        lses = []
        for n, (sub, g) in enumerate(slots):
            rb, blk = bands[sub][0], slice(sub * BLK, (sub + 1) * BLK)
            vband = qkv_ref[pl.ds(rb, BAND), AW + KVW + g * HD:AW + KVW + (g + 1) * HD]
            lse_parts = []
            for hh in range(GROUP):
                h = g * GROUP + hh
                dp_ref[n, hh * BLK:(hh + 1) * BLK, :] = _dot_nt(do_ref[blk, h * HD:(h + 1) * HD], vband)
                lse_parts.append(lse_ref[blk, h:h + 1])
            lses.append(jnp.concatenate(lse_parts, axis=0))
        dsinks = []
        for n, (sub, g) in enumerate(slots):
            pf, dpv = p_ref[n].astype(F32), dp_ref[n]
            dl = jnp.sum(pf * dpv, axis=-1, keepdims=True)
            dsb_ref[n] = (pf * (dpv - dl)).astype(BF16)
            dsinks.append(-(jnp.exp(_sink_column(sink_ref, g) - lses[n]) * dl))
        for n, (sub, g) in enumerate(slots):
            rb, blk = bands[sub][0], slice(sub * BLK, (sub + 1) * BLK)
            kband = qkv_ref[pl.ds(rb, BAND), AW + g * HD:AW + (g + 1) * HD]
            dk = jnp.zeros((BAND, HD), F32)
            dv = jnp.zeros((BAND, HD), F32)
            for hh in range(GROUP):
                h = g * GROUP + hh
                hcol = slice(h * HD, (h + 1) * HD)
                rows = slice(hh * BLK, (hh + 1) * BLK)
                dq_ref[blk, hcol] = (_dot(dsb_ref[n, rows, :], kband) * SCALE).astype(BF16)
                dk = dk + _dot_tn(dsb_ref[n, rows, :], qkv_ref[pl.ds(r0s[sub], BLK), hcol])
                dv = dv + _dot_tn(p_ref[n, rows, :], do_ref[blk, hcol])
                ds_ref[h:h + 1, :] += jnp.broadcast_to(
                    jnp.sum(dsinks[n][rows], axis=0, keepdims=True), (1, 128))
            dkv_ref[pl.ds(rb, BAND), g * HD:(g + 1) * HD] += dk * SCALE
            dkv_ref[pl.ds(rb, BAND), KVW + g * HD:KVW + (g + 1) * HD] += dv

    nslot, tq = nsub * NKV, nsub * BLK
    return pl.pallas_call(
        body, name="attn_bwd", grid=(T // tq,),
        in_specs=[pl.BlockSpec(memory_space=pltpu.SMEM), _whole((T, QKVW)),
                  _rows(tq, AW), _rows(tq, NQ),
                  pl.BlockSpec((nslot, GROWS, BAND), lambda i: (i, 0, 0)), _whole((8, 128))],
        out_specs=[_rows(tq, AW), _whole((T, 2 * KVW)), _whole((NQ, 128))],
        out_shape=[jax.ShapeDtypeStruct((T, AW), BF16), jax.ShapeDtypeStruct((T, 2 * KVW), F32),
                   jax.ShapeDtypeStruct((NQ, 128), F32)],
        scratch_shapes=[pltpu.VMEM((nslot, GROWS, BAND), F32), pltpu.VMEM((nslot, GROWS, BAND), BF16)],
        compiler_params=_params("arbitrary"),
    )(sinks, qkv, do, lse, p, dep)


PROJ_PARTS = [(0, AW), (AW, QKVW), (GLU_OFF, GATE_OFF), (GATE_OFF, INW)]


def _in_proj_bwd(dq, dkv, dglu, dgl, x, dx1, g_mix, w_in_t, dep):
    tm = ROW_TM

    def body(dq_ref, dkv_ref, dglu_ref, dgl_ref, x_ref, dx1_ref, g_ref, w_ref, dep_ref,
             gx_ref, dg_ref, db_ref):
        i = pl.program_id(0)

        @pl.when(i == 0)
        def _():
            dg_ref[...] = jnp.zeros_like(dg_ref)
            db_ref[...] = jnp.zeros_like(db_ref)

        dh = jnp.zeros((tm, D), F32)
        for part_ref, (lo, hi) in zip((dq_ref, dkv_ref, dglu_ref, dgl_ref), PROJ_PARTS):
            part = part_ref[...]
            dh = dh + _dot(part.astype(BF16), w_ref[lo:hi, :])
            db_ref[:, lo:hi] += jnp.sum(part.astype(F32), axis=0, keepdims=True)
        xv = x_ref[...]
        r = lax.rsqrt(jnp.mean(xv * xv, axis=-1, keepdims=True) + EPS)
        xh = xv * r
        dhg = dh * g_ref[...]
        gx_ref[...] = dx1_ref[...] + r * (dhg - xh * jnp.mean(dhg * xh, axis=-1, keepdims=True))
        dg_ref[...] += jnp.sum(dh * xh, axis=0, keepdims=True)

    return pl.pallas_call(
        body, name="in_proj_bwd", grid=(T // tm,),
        in_specs=[_rows(tm, AW), _rows(tm, 2 * KVW), _rows(tm, 2 * C), _rows(tm, 2 * D),
                  _rows(tm, D), _rows(tm, D), _whole((1, D)), _const((INW, D)), _whole((8, 128))],
        out_specs=[_rows(tm, D), _whole((1, D)), _whole((1, INW))],
        out_shape=[jax.ShapeDtypeStruct((T, D), F32), jax.ShapeDtypeStruct((1, D), F32),
                   jax.ShapeDtypeStruct((1, INW), F32)],
        compiler_params=_params("arbitrary"),
    )(dq, dkv, dglu, dgl, x, dx1, g_mix, w_in_t, dep)


def _grad_w_in_t(h, dq, dkv, dglu, dgl):
    tn, chunk = 512, 256

    def body(h_ref, dq_ref, dkv_ref, dglu_ref, dgl_ref, o_ref):
        hv = h_ref[...]
        for part_ref, (lo, hi) in zip((dq_ref, dkv_ref, dglu_ref, dgl_ref), PROJ_PARTS):
            for c0 in range(0, hi - lo, chunk):
                o_ref[lo + c0:lo + c0 + chunk, :] = _dot_tn(
                    part_ref[:, c0:c0 + chunk].astype(BF16), hv).astype(BF16)

    return pl.pallas_call(
        body, name="grad_w_in", grid=(D // tn,),
        in_specs=[pl.BlockSpec((T, tn), lambda j: (0, j)), _const((T, AW)), _const((T, 2 * KVW)),
                  _const((T, 2 * C)), _const((T, 2 * D))],
        out_specs=pl.BlockSpec((INW, tn), lambda j: (0, j)),
        out_shape=jax.ShapeDtypeStruct((INW, D), BF16),
        compiler_params=_params("parallel"),
    )(h, dq, dkv, dglu, dgl)


def _grad_w(a, b, name, tk, tn, col_sharded):
    k, n = a.shape[1], b.shape[1]

    single = n == tn
    sw = n // N_CHIPS

    def body(a_ref, b_ref, o_ref, at_ref):
        if single:
            res = _dot_tn(a_ref[...], b_ref[...]).astype(BF16)
            if col_sharded:
                for s in range(N_CHIPS):
                    o_ref[s] = res[:, s * sw:(s + 1) * sw]
            else:
                o_ref[...] = res
            return

        @pl.when(pl.program_id(1) == 0)
        def _():
            at_ref[...] = a_ref[...].T

        o_ref[...] = _dot(at_ref[...], b_ref[...]).astype(BF16)

    if col_sharded and single:
        shape = (N_CHIPS, k, sw)
        out_spec = pl.BlockSpec((N_CHIPS, tk, sw), lambda i, j: (0, i, 0))
    elif col_sharded:
        per = sw // tn
        shape = (N_CHIPS, k, sw)
        out_spec = pl.BlockSpec((None, tk, tn), lambda i, j: (j // per, i, j % per))
    else:
        shape = (1, k, n)
        out_spec = pl.BlockSpec((None, tk, tn), lambda i, j: (0, i, j))
    out = pl.pallas_call(
        body, name=name, grid=(k // tk, n // tn),
        in_specs=[pl.BlockSpec((T, tk), lambda i, j: (0, i)), pl.BlockSpec((T, tn), lambda i, j: (0, j))],
        out_specs=out_spec,
        out_shape=jax.ShapeDtypeStruct(shape, BF16),
        scratch_shapes=[pltpu.VMEM((tk, T), BF16)],
        compiler_params=_params("parallel", "arbitrary"),
    )(a, b)
    return out if col_sharded else out.reshape(N_CHIPS, k // N_CHIPS, n)


HBM_SPEC = pl.BlockSpec(memory_space=pltpu.HBM)


def _place():
    x, y, c = lax.axis_index("x"), lax.axis_index("y"), lax.axis_index("c")
    chips = [(1 - x, y), (x, 1 - y), (1 - x, 1 - y)]
    return x, y, c, chips


SEM_SPEC = pl.BlockSpec(memory_space=pltpu.SEMAPHORE)
ANY_SPEC = pl.BlockSpec(memory_space=pl.ANY)
VMEM_SPEC = pl.BlockSpec(memory_space=pltpu.VMEM)
EFFECT = pltpu.SideEffectType.DATAFLOW_SIDE_EFFECTING


def _gather_ends(src, land, x, y, c, chips):
    kh = src.shape[0] // 2
    s_me = 2 * x + y
    ends = [(src.at[pl.ds(c * kh, kh)], land.at[s_me, pl.ds(c * kh, kh)], (*chip, c)) for chip in chips]
    return ends + [(src, land.at[s_me], (x, y, 1 - c))]


def _reduce_ends(src, land, x, y, c, chips):
    return [(src.at[2 * chip[0] + chip[1]], land.at[j], (*chip, c)) for j, chip in enumerate(chips)]


def _chip_copies(ends, srcs, lands, send_sems, recv_sems, first=0):
    x, y, c, chips = _place()
    copies = []
    for src, land in zip(srcs, lands):
        peers = ends(src, land, x, y, c, chips)
        for s, d, to in peers:
            k = first * len(peers) + len(copies)
            copies.append(pltpu.make_async_remote_copy(
                src_ref=s, dst_ref=d, send_sem=send_sems.at[k], recv_sem=recv_sems.at[k],
                device_id=to, device_id_type=MESH))
    return copies


GATHER_PEERS, REDUCE_PEERS = 4, 3


def _chip_start(name, ends, peers, srcs, lands):
    n = len(srcs)

    def body(*refs):
        copies = _chip_copies(ends, refs[:n], refs[n:2 * n], refs[2 * n], refs[2 * n + 1])
        for cp in copies:
            cp.start()
        token = refs[-1]
        token[...] = jnp.zeros_like(token)

    hbm = lambda a: pltpu.HBM(a.shape, a.dtype)
    res = pl.pallas_call(
        body, name=name,
        out_shape=(pltpu.SemaphoreType.DMA((peers * n,)), pltpu.SemaphoreType.DMA((peers * n,)),
                   *[hbm(a) for a in srcs], *[hbm(a) for a in lands],
                   jax.ShapeDtypeStruct((8, 128), F32)),
        in_specs=[HBM_SPEC] * (2 * n),
        out_specs=(SEM_SPEC, SEM_SPEC, *[HBM_SPEC] * (2 * n), VMEM_SPEC),
        input_output_aliases={i: 2 + i for i in range(2 * n)},
        compiler_params=pltpu.CompilerParams(has_side_effects=EFFECT),
    )(*[pltpu.with_memory_space_constraint(a, pltpu.HBM) for a in (*srcs, *lands)])
    return res[0], res[1], list(res[2:2 + n]), list(res[2 + n:2 + 2 * n]), res[-1]


def _chip_wait(name, ends, send_sems, recv_sems, srcs, lands, after, first=0):
    n, na = len(srcs), len(after)

    def body(*refs):
        copies = _chip_copies(ends, refs[:n], refs[n:2 * n], refs[2 * n], refs[2 * n + 1], first)
        for cp in copies:
            cp.wait_send()
            cp.wait_recv()

    hbm = lambda a: pltpu.HBM(a.shape, a.dtype)
    res = pl.pallas_call(
        body, name=name,
        out_shape=tuple(hbm(a) for a in (*srcs, *lands)),
        in_specs=[HBM_SPEC] * (2 * n) + [SEM_SPEC, SEM_SPEC] + [ANY_SPEC] * na,
        out_specs=tuple([HBM_SPEC] * (2 * n)),
        input_output_aliases={i: i for i in range(2 * n)},
        compiler_params=pltpu.CompilerParams(has_side_effects=EFFECT),
    )(*srcs, *lands, send_sems, recv_sems, *after)
    return list(res[:n]), list(res[n:])


def _forward_copies(lands, send_sems, recv_sems):
    x, y, c, chips = _place()
    copies = []
    for land in lands:
        kh = land.shape[1] // 2
        for chip in chips:
            blk = land.at[2 * chip[0] + chip[1], pl.ds(c * kh, kh)]
            k = len(copies)
            copies.append(pltpu.make_async_remote_copy(
                src_ref=blk, dst_ref=blk, send_sem=send_sems.at[k], recv_sem=recv_sems.at[k],
                device_id=(x, y, 1 - c), device_id_type=MESH))
    return copies


def _gather_relay(name, send_sems, recv_sems, srcs, lands, after, first):
    n, na = len(srcs), len(after)

    def body(*refs):
        land_refs = refs[n:2 * n]
        for cp in _chip_copies(_gather_ends, refs[:n], land_refs, refs[2 * n], refs[2 * n + 1], first):
            cp.wait_send()
            cp.wait_recv()
        out = refs[2 * n + 2 + na:]
        for cp in _forward_copies(land_refs, out[0], out[1]):
            cp.start()
        out[-1][...] = jnp.zeros_like(out[-1])

    hbm = lambda a: pltpu.HBM(a.shape, a.dtype)
    res = pl.pallas_call(
        body, name=name,
        out_shape=(pltpu.SemaphoreType.DMA((3 * n,)), pltpu.SemaphoreType.DMA((3 * n,)),
                   *[hbm(a) for a in lands], jax.ShapeDtypeStruct((8, 128), F32)),
        in_specs=[HBM_SPEC] * (2 * n) + [SEM_SPEC, SEM_SPEC] + [ANY_SPEC] * na,
        out_specs=(SEM_SPEC, SEM_SPEC, *[HBM_SPEC] * n, VMEM_SPEC),
        input_output_aliases={n + i: 2 + i for i in range(n)},
        compiler_params=pltpu.CompilerParams(has_side_effects=EFFECT),
    )(*srcs, *lands, send_sems, recv_sems, *after)
    return res[0], res[1], list(res[2:2 + n]), res[-1]


def _forward_wait(name, send_sems, recv_sems, lands, after):
    n, na = len(lands), len(after)

    def body(*refs):
        for cp in _forward_copies(refs[:n], refs[n], refs[n + 1]):
            cp.wait_send()
            cp.wait_recv()

    hbm = lambda a: pltpu.HBM(a.shape, a.dtype)
    res = pl.pallas_call(
        body, name=name,
        out_shape=tuple(hbm(a) for a in lands),
        in_specs=[HBM_SPEC] * n + [SEM_SPEC, SEM_SPEC] + [ANY_SPEC] * na,
        out_specs=tuple([HBM_SPEC] * n),
        input_output_aliases={i: i for i in range(n)},
        compiler_params=pltpu.CompilerParams(has_side_effects=EFFECT),
    )(*lands, send_sems, recv_sems, *after)
    return list(res)


def _exchange_ends(src, land, x, y, c, chips):
    kh = src.shape[1] // 2
    return [(src.at[:, pl.ds((1 - c) * kh, kh)], land, (x, y, 1 - c))]


def _share_ends(src, land, x, y, c, chips):
    return [(src, land, (x, y, 1 - c))]


def _small_ends(src, land, x, y, c, chips):
    m = src.shape[0]
    rows = land.at[pl.ds((4 * x + 2 * y + c) * m, m)]
    peers = [(x, y, 1 - c)] + [(*chip, c) for chip in chips] + [(*chip, 1 - c) for chip in chips]
    return [(src, rows, to) for to in peers]


PAIR_PEERS, SMALL_PEERS = 1, 7


def _row_tile(k):
    for t in (256, 240, 128, 176, 64, 32, 16):
        if k % t == 0:
            return t
    raise ValueError(k)


def _pair_sum(c_idx, g, got, name):
    _, k, n = g.shape
    kh = k // 2
    tm = _row_tile(kh)
    nb = kh // tm

    def body(c_ref, g_ref, r_ref, o_ref):
        o_ref[...] = (g_ref[...].astype(F32) + r_ref[...].astype(F32)).astype(BF16)

    return pl.pallas_call(
        body, name=name,
        grid_spec=pltpu.PrefetchScalarGridSpec(
            num_scalar_prefetch=1, grid=(nb,),
            in_specs=[pl.BlockSpec((N_CHIPS, tm, n), lambda i, c_ref: (0, c_ref[0] * nb + i, 0)),
                      pl.BlockSpec((N_CHIPS, tm, n), lambda i, c_ref: (0, i, 0))],
            out_specs=pl.BlockSpec((N_CHIPS, tm, n), lambda i, c_ref: (0, i, 0))),
        out_shape=jax.ShapeDtypeStruct((N_CHIPS, kh, n), BF16),
        compiler_params=_params("parallel"),
    )(c_idx, g, got)


def _chip_sum(s_idx, mine, got, name):
    _, kh, n = mine.shape
    tm = _row_tile(kh)

    def body(s_ref, m_ref, r_ref, o_ref):
        acc = m_ref[0].astype(F32)
        for j in range(3):
            acc = acc + r_ref[j].astype(F32)
        o_ref[...] = acc

    return pl.pallas_call(
        body, name=name,
        grid_spec=pltpu.PrefetchScalarGridSpec(
            num_scalar_prefetch=1, grid=(kh // tm,),
            in_specs=[pl.BlockSpec((1, tm, n), lambda i, s_ref: (s_ref[0], i, 0)),
                      pl.BlockSpec((3, tm, n), lambda i, s_ref: (0, i, 0))],
            out_specs=pl.BlockSpec((tm, n), lambda i, s_ref: (i, 0))),
        out_shape=jax.ShapeDtypeStruct((kh, n), F32),
        compiler_params=_params("parallel"),
    )(s_idx, mine, got)


def _adamw_math(w, g, m, v):
    m = ADAM_B1 * m + (1.0 - ADAM_B1) * g
    v = ADAM_B2 * v + (1.0 - ADAM_B2) * (g * g)
    m_hat = m / (1.0 - ADAM_B1 ** ADAM_STEP)
    v_hat = v / (1.0 - ADAM_B2 ** ADAM_STEP)
    delta = -ADAM_LR * (m_hat / (jnp.sqrt(v_hat) + ADAM_EPS) + ADAM_WD * w)
    return delta, m, v


def _adamw(c_idx, w, g_mine, g_other, m, v, name):
    k, n = w.shape
    tm = k // 4

    def body(c_ref, w_ref, gm_ref, go_ref, m_ref, v_ref, g_ref, d_ref, mo_ref, vo_ref):
        g = jnp.where(pl.program_id(0) == c_ref[0], gm_ref[...], go_ref[...])
        d, mm, vv = _adamw_math(w_ref[...], g, m_ref[...], v_ref[...])
        g_ref[...] = g
        d_ref[...] = d
        mo_ref[...] = mm
        vo_ref[...] = vv

    full = pl.BlockSpec((tm, n), lambda h, i, c_ref: (2 * h + i, 0))
    mine = pl.BlockSpec((tm, n), lambda h, i, c_ref: (jnp.where(h == c_ref[0], i, 0), 0))
    other = pl.BlockSpec((tm, n), lambda h, i, c_ref: (jnp.where(h == c_ref[0], 0, i), 0))
    shp = jax.ShapeDtypeStruct((k, n), F32)
    return pl.pallas_call(
        body, name=name,
        grid_spec=pltpu.PrefetchScalarGridSpec(
            num_scalar_prefetch=1, grid=(2, 2),
            in_specs=[full, mine, other, full, full], out_specs=[full] * 4),
        out_shape=[shp] * 4, compiler_params=_params("arbitrary", "arbitrary"),
    )(c_idx, w, g_mine, g_other, m, v)


VEC_SLOTS = {
    "g_mix_norm": (0, 0, D), "b_conv_proj": (0, D, D), "g_ffn_norm": (0, 2 * D, D),
    "g_final": (0, 3 * D, D), "b_in": (1, 0, INW), "conv_b": (2, 0, C), "ln_g": (2, C, C),
    "ln_b": (2, 2 * C, C), "sinks": (2, 3 * C, NQ), "loss": (2, 3 * C + 128, 1),
}
VEC_ROWS, VEC_COLS = 8, 4 * D
CW_ROWS = 32
SMALL_NAMES = ["g_mix_norm", "b_in", "sinks", "conv_w", "conv_b", "ln_g", "ln_b",
               "b_conv_proj", "g_ffn_norm", "g_final"]
CW_LANES = C // N_CHIPS


def _pack_small(gs, loss):
    row0 = jnp.concatenate([gs["g_mix_norm"], gs["b_conv_proj"], gs["g_ffn_norm"], gs["g_final"]], axis=1)
    row1 = jnp.pad(gs["b_in"], ((0, 0), (0, VEC_COLS - INW)))
    row2 = jnp.concatenate([gs["conv_b"], gs["ln_g"], gs["ln_b"],
                            jnp.pad(gs["sinks"], ((0, 0), (0, 128 - NQ))),
                            jnp.pad(loss.reshape(1, 1), ((0, 0), (0, VEC_COLS - 3 * C - 129)))], axis=1)
    vec = jnp.concatenate([row0, row1, row2, jnp.zeros((VEC_ROWS - 3, VEC_COLS), F32)], axis=0)
    cw = jnp.pad(gs["conv_w"], ((0, CW_ROWS - KW), (0, 0)))
    return vec, cw


def _small_update(idx, vec_own, cw_own, vec_all, cw_all, wmv):
    nsm = len(SMALL_NAMES)

    def body(s_ref, vown_ref, cown_ref, vec_ref, cw_ref, *refs):
        ins = refs[:3 * nsm]
        outs = refs[3 * nsm:7 * nsm]
        loss_ref = refs[7 * nsm]
        me = s_ref[1]

        def summed(own_ref, table_ref, rows_per_dev, r0, nrows, lane, width):
            acc = None
            for k in range(8):
                piece = jnp.where(me == k, own_ref[r0:r0 + nrows, lane:lane + width],
                                  table_ref[k * rows_per_dev + r0:k * rows_per_dev + r0 + nrows, lane:lane + width])
                acc = piece if acc is None else acc + piece
            return acc

        def total(slot):
            row, lane, width = slot
            return summed(vown_ref, vec_ref, VEC_ROWS, row, 1, lane, width)

        loss_ref[...] = jnp.broadcast_to(total(VEC_SLOTS["loss"]), loss_ref.shape)
        for p, name in enumerate(SMALL_NAMES):
            w_ref, m_ref, v_ref = ins[3 * p:3 * p + 3]
            g_ref, d_ref, mo_ref, vo_ref = outs[4 * p:4 * p + 4]
            if name == "conv_w":
                g = jnp.zeros((KW, CW_LANES), F32)
                for s in range(N_CHIPS):
                    cand = summed(cown_ref, cw_ref, CW_ROWS, 0, KW, s * CW_LANES, CW_LANES)
                    g = jnp.where(s_ref[0] == s, cand, g)
            else:
                g = total(VEC_SLOTS[name])
            d, mm, vv = _adamw_math(w_ref[...], g, m_ref[...], v_ref[...])
            g_ref[...] = g
            d_ref[...] = d
            mo_ref[...] = mm
            vo_ref[...] = vv

    vmem = pl.BlockSpec(memory_space=pltpu.VMEM)
    flat = [a for t in wmv for a in t]
    out_shape = []
    for w, _, _ in wmv:
        out_shape += [jax.ShapeDtypeStruct(w.shape, F32)] * 4
    out_shape.append(jax.ShapeDtypeStruct((1, 128), F32))
    res = pl.pallas_call(
        body, name="small_update",
        in_specs=[pl.BlockSpec(memory_space=pltpu.SMEM)] + [vmem] * (4 + len(flat)),
        out_specs=[vmem] * len(out_shape), out_shape=out_shape,
    )(idx, vec_own, cw_own, vec_all, cw_all, *flat)
    return [tuple(res[4 * p:4 * p + 4]) for p in range(nsm)], res[4 * nsm]


WEIGHT_ORDER = ["g_mix_norm", "w_in", "b_in", "sinks", "conv_w", "conv_b", "ln_g", "ln_b",
                "w_attn_proj", "w_conv_proj", "b_conv_proj", "w_out", "g_ffn_norm", "w_ffn_in",
                "w_ffn_down", "g_final"]


def kernel(x, g_mix_norm, w_in, b_in, sinks, conv_w, conv_b, ln_g, ln_b, w_attn_proj, w_conv_proj, b_conv_proj, w_out, g_ffn_norm, w_ffn_in, w_ffn_down, g_final, loss_target, m_g_mix_norm, m_w_in, m_b_in, m_sinks, m_conv_w, m_conv_b, m_ln_g, m_ln_b, m_w_attn_proj, m_w_conv_proj, m_b_conv_proj, m_w_out, m_g_ffn_norm, m_w_ffn_in, m_w_ffn_down, m_g_final, v_g_mix_norm, v_w_in, v_b_in, v_sinks, v_conv_w, v_conv_b, v_ln_g, v_ln_b, v_w_attn_proj, v_w_conv_proj, v_b_conv_proj, v_w_out, v_g_ffn_norm, v_w_ffn_in, v_w_ffn_down, v_g_final):
    w = dict(g_mix_norm=g_mix_norm, w_in=w_in, b_in=b_in, sinks=sinks, conv_w=conv_w, conv_b=conv_b,
             ln_g=ln_g, ln_b=ln_b, w_attn_proj=w_attn_proj, w_conv_proj=w_conv_proj,
             b_conv_proj=b_conv_proj, w_out=w_out, g_ffn_norm=g_ffn_norm, w_ffn_in=w_ffn_in,
             w_ffn_down=w_ffn_down, g_final=g_final)
    m = dict(g_mix_norm=m_g_mix_norm, w_in=m_w_in, b_in=m_b_in, sinks=m_sinks, conv_w=m_conv_w,
             conv_b=m_conv_b, ln_g=m_ln_g, ln_b=m_ln_b, w_attn_proj=m_w_attn_proj,
             w_conv_proj=m_w_conv_proj, b_conv_proj=m_b_conv_proj, w_out=m_w_out,
             g_ffn_norm=m_g_ffn_norm, w_ffn_in=m_w_ffn_in, w_ffn_down=m_w_ffn_down, g_final=m_g_final)
    v = dict(g_mix_norm=v_g_mix_norm, w_in=v_w_in, b_in=v_b_in, sinks=v_sinks, conv_w=v_conv_w,
             conv_b=v_conv_b, ln_g=v_ln_g, ln_b=v_ln_b, w_attn_proj=v_w_attn_proj,
             w_conv_proj=v_w_conv_proj, b_conv_proj=v_b_conv_proj, w_out=v_w_out,
             g_ffn_norm=v_g_ffn_norm, w_ffn_in=v_w_ffn_in, w_ffn_down=v_w_ffn_down, g_final=v_g_final)

    c_idx = lax.axis_index("c").astype(jnp.int32).reshape(1)
    s_idx = (2 * lax.axis_index("x") + lax.axis_index("y")).astype(jnp.int32).reshape(1)

    out_g, out_d, out_m, out_v = {}, {}, {}, {}

    def gather_start(tag, shards):
        lands = [lax.empty((N_CHIPS,) + s.shape, s.dtype) for s in shards]
        return _chip_start("gather_start_" + tag, _gather_ends, GATHER_PEERS, shards, lands)

    def gather_relay(tag, state, after, first=0, count=None):
        send_sems, recv_sems, shards, lands, _ = state
        last = len(shards) if count is None else first + count
        return _gather_relay("gather_relay_" + tag, send_sems, recv_sems, shards[first:last],
                             lands[first:last], after, first)

    def gather_finish(tag, relay, after):
        return _forward_wait("forward_wait_" + tag, relay[0], relay[1], relay[2], after)

    names_b = ["w_attn_proj", "w_conv_proj", "w_out", "w_ffn_in", "w_ffn_down"]
    big = {name: (w[name][0], m[name][0], v[name][0]) for name in names_b}
    big["w_in"] = (w_in[0].T, m_w_in[0].T, v_w_in[0].T)
    state_a = gather_start("a", [big["w_in"][0].astype(BF16), jnp.pad(conv_w[0], ((0, CW_ROWS - KW), (0, 0)))])
    state_b = gather_start("b", [(big[name][0] + state_a[4][0, 0]).astype(BF16) for name in names_b])
    got_a = gather_finish("a", gather_relay("a", state_a, [state_b[4]]), [])
    w_in_t_full = got_a[0].reshape(INW, D)
    conv_w_full = got_a[1].transpose(1, 0, 2).reshape(CW_ROWS, C)[:KW]

    xs, target = x[0], loss_target[0]
    g_final2 = g_final.reshape(1, D)
    h, qkv, glu, gl = _in_proj(xs, g_mix_norm, w_in_t_full, b_in)
    o, lse, probs = _attn_fwd(qkv, sinks)
    relay_1 = gather_relay("b1", state_b, [o], 0, 3)
    u, cact = _conv_fwd(glu, conv_w_full, conv_b, ln_g, ln_b, relay_1[3])
    w_ap4, w_cp4, w_out4 = gather_finish("b1", relay_1, [cact])
    w_out_full = w_out4.reshape(D, D)
    mixed = _mix_out(xs, o, cact, gl, w_ap4, w_cp4, b_conv_proj, w_out_full, relay_1[3])
    relay_2 = gather_relay("b2", state_b, [mixed[3]], 3, 1)
    ya, yc, mg, x1 = _mix_out(xs, o, cact, gl, w_ap4, w_cp4, b_conv_proj, w_out_full, relay_2[3], mixed)
    w_fi4, = gather_finish("b2", relay_2, [x1])
    h2, gu, act = _ffn_in_first(x1, g_ffn_norm, w_fi4, relay_2[3])
    relay_3 = gather_relay("b3", state_b, [h2], 4, 1)
    gu, act = _ffn_in_second(h2, w_fi4, gu, act, relay_3[3])
    w_dn4, = gather_finish("b3", relay_3, [act])
    w_dn_full = w_dn4.reshape(DFF, D)
    dx2, dx2b, dg_final, loss_part = _ffn_out_loss(x1, act, w_dn_full, g_final2, target)

    def exchange_start(tag, grads):
        lands = [lax.empty((N_CHIPS, g.shape[1] // 2, g.shape[2]), g.dtype) for g in grads]
        return _chip_start("pair_start_" + tag, _exchange_ends, PAIR_PEERS, grads, lands)

    def reduce_start(tag, names, exchange, after):
        send_sems, recv_sems, grads, lands, _ = exchange
        grads, from_sibling = _chip_wait("pair_wait_" + tag, _exchange_ends, send_sems, recv_sems, grads, lands, after)
        pair = [_pair_sum(c_idx, g, r, "pair_sum_" + name) for name, g, r in zip(names, grads, from_sibling)]
        lands = [lax.empty((3,) + p.shape[1:], p.dtype) for p in pair]
        return _chip_start("chip_start_" + tag, _reduce_ends, REDUCE_PEERS, pair, lands)

    def reduce_sum(tag, names, state, after):
        send_sems, recv_sems, pair, lands, _ = state
        pair, lands = _chip_wait("chip_wait_" + tag, _reduce_ends, send_sems, recv_sems, pair, lands, after)
        mine = [_chip_sum(s_idx, p, r, "chip_sum_" + name) for name, p, r in zip(names, pair, lands)]
        others = [lax.empty(a.shape, a.dtype) for a in mine]
        return _chip_start("share_start_" + tag, _share_ends, PAIR_PEERS, mine, others)

    def reduce_finish(tag, names, share, after):
        send_sems, recv_sems, mine, others, _ = share
        mine, others = _chip_wait("share_wait_" + tag, _share_ends, send_sems, recv_sems, mine, others, after)
        for name, g_mine, g_other in zip(names, mine, others):
            wv, mv, vv = big[name]
            res = _adamw(c_idx, wv, g_mine, g_other, mv, vv, "adamw_" + name)
            if name == "w_in":
                res = [a.T for a in res]
            out_g[name], out_d[name], out_m[name], out_v[name] = [a[None] for a in res]

    dgu, dx1, dx1b, dg_ffn = _ffn_bwd(dx2, dx2b, gu, x1, g_ffn_norm, w_dn_full, w_fi4)
    names_1 = ["w_ffn_in", "w_ffn_down", "w_out", "w_attn_proj", "w_conv_proj"]
    grads_1 = [_grad_w(h2, dgu, "grad_w_ffn_in", 512, FSH, True),
               _grad_w(act, dx2b, "grad_w_ffn_down", 256, D, False)]
    dya, dyc, dgl, do, dc, db_cp = _mix_bwd(dx1b, gl, ya, yc, w_out_full, w_ap4, w_cp4, relay_3[3])
    grads_1 += [_grad_w(mg, dx1b, "grad_w_out", 512, D, False),
                _grad_w(o, dya, "grad_w_attn_proj", 512, D, True),
                _grad_w(cact, dyc, "grad_w_conv_proj", 512, D, True)]
    exchange_1 = exchange_start("1", grads_1)
    dq, dkv, dsinks = _attn_bwd(qkv, do, lse, probs, sinks, exchange_1[4])
    state_1 = reduce_start("1", names_1, exchange_1, [dq])
    dglu, dconv_w, dconv_b, dln_g, dln_b = _conv_bwd(glu, u, dc, conv_w_full, ln_g, ln_b, state_1[4])
    names_2 = ["w_in"]
    gw_in_t = _grad_w_in_t(h, dq, dkv, dglu, dgl)
    exchange_2 = exchange_start("2", [gw_in_t.reshape(N_CHIPS, INW // N_CHIPS, D)])
    grad_x, dg_mix, db_in = _in_proj_bwd(dq, dkv, dglu, dgl, xs, dx1, g_mix_norm, w_in_t_full, exchange_2[4])

    gs = {"g_mix_norm": dg_mix, "b_in": db_in, "sinks": dsinks[:, 0].reshape(1, NQ),
          "conv_w": dconv_w, "conv_b": dconv_b, "ln_g": dln_g, "ln_b": dln_b,
          "b_conv_proj": db_cp, "g_ffn_norm": dg_ffn, "g_final": dg_final}
    blocks = list(_pack_small(gs, loss_part[0, 0]))
    tables = [lax.empty((8 * b.shape[0], b.shape[1]), b.dtype) for b in blocks]
    small = _chip_start("small_start", _small_ends, SMALL_PEERS, blocks, tables)

    state_2 = reduce_start("2", names_2, exchange_2, [grad_x, small[4]])
    share_1 = reduce_sum("1", names_1, state_1, [state_2[4]])
    blocks, tables = _chip_wait("small_wait", _small_ends, small[0], small[1], small[2], small[3],
                                [share_1[4]])
    me = (4 * lax.axis_index("x") + 2 * lax.axis_index("y") + lax.axis_index("c")).astype(jnp.int32)

    def view(a, name):
        if name == "conv_w":
            return a[0]
        if name == "g_final":
            return a.reshape(1, D)
        return a

    wmv = [(view(w[name], name), view(m[name], name), view(v[name], name)) for name in SMALL_NAMES]
    small_out, loss_row = _small_update(jnp.concatenate([s_idx, me.reshape(1)]), blocks[0], blocks[1],
                                        tables[0], tables[1], wmv)
    for name, (g, d, mm, vv) in zip(SMALL_NAMES, small_out):
        shape = w[name].shape
        out_g[name], out_d[name], out_m[name], out_v[name] = (
            g.reshape(shape), d.reshape(shape), mm.reshape(shape), vv.reshape(shape))

    reduce_finish("1", names_1, share_1, [loss_row])
    share_2 = reduce_sum("2", names_2, state_2, [out_d["w_conv_proj"]])
    reduce_finish("2", names_2, share_2, [])

    loss = loss_row[0, 0]
    return (loss, grad_x[None], *[out_g[k] for k in WEIGHT_ORDER], *[out_d[k] for k in WEIGHT_ORDER],
            *[out_m[k] for k in WEIGHT_ORDER], *[out_v[k] for k in WEIGHT_ORDER])
```

```python
import functools

import jax
import jax.numpy as jnp
from jax import lax
from jax.experimental import pallas as pl
from jax.experimental.pallas import tpu as pltpu

F32 = jnp.float32
BF16 = jnp.bfloat16

T = 2048
D = 1024
HD = 64
NQ = 8
NKV = 2
GROUP = NQ // NKV
BLK = 128
AW = NQ * HD
KVW = NKV * HD
C = 512
KW = 31
QKVW = AW + 2 * KVW
GLU_OFF = QKVW
GATE_OFF = GLU_OFF + 2 * C
INW = GATE_OFF + 2 * D
DFF = 2816
EPS = 1e-5
NEG = -1e30
SCALE = HD ** -0.5
HALO = 32
N_CHIPS = 4
FSH = 2 * DFF // N_CHIPS

ADAM_LR = 0.001
ADAM_B1 = 0.9
ADAM_B2 = 0.999
ADAM_EPS = 1e-08
ADAM_WD = 0.01
ADAM_STEP = 10

VMEM_LIMIT = 56 * 1024 * 1024
ROW_TM = 512
MESH = pl.DeviceIdType.MESH


def _params(*sem):
    return pltpu.CompilerParams(dimension_semantics=sem, vmem_limit_bytes=VMEM_LIMIT)


def _dot(a, b):
    return jnp.dot(a, b, preferred_element_type=F32)


def _dot_nt(a, b):
    return lax.dot_general(a, b, (((1,), (1,)), ((), ())), preferred_element_type=F32)


def _dot_tn(a, b):
    return lax.dot_general(a, b, (((0,), (0,)), ((), ())), preferred_element_type=F32)


def _sigmoid(v):
    return 1.0 / (1.0 + jnp.exp(-v))


def _rows(tm, n):
    return pl.BlockSpec((tm, n), lambda i: (i, 0))


def _whole(shape):
    return pl.BlockSpec(shape, lambda i: tuple(0 for _ in shape))


def _in_proj(x, g_mix, w_in_t, b_in):
    tm = ROW_TM

    def body(x_ref, g_ref, w_ref, b_ref, h_ref, qkv_ref, glu_ref, gl_ref):
        xv = x_ref[...]
        r = lax.rsqrt(jnp.mean(xv * xv, axis=-1, keepdims=True) + EPS)
        h = (xv * r * g_ref[...]).astype(BF16)
        h_ref[...] = h
        qkv_ref[...] = (_dot_nt(h, w_ref[0:GLU_OFF, :]) + b_ref[:, 0:GLU_OFF]).astype(BF16)
        glu_ref[...] = (_dot_nt(h, w_ref[GLU_OFF:GATE_OFF, :]) + b_ref[:, GLU_OFF:GATE_OFF]).astype(BF16)
        gl_ref[...] = (_dot_nt(h, w_ref[GATE_OFF:INW, :]) + b_ref[:, GATE_OFF:INW]).astype(BF16)

    return pl.pallas_call(
        body, name="in_proj", grid=(T // tm,),
        in_specs=[_rows(tm, D), _whole((1, D)), _whole((INW, D)), _whole((1, INW))],
        out_specs=[_rows(tm, D), _rows(tm, QKVW), _rows(tm, 2 * C), _rows(tm, 2 * D)],
        out_shape=[jax.ShapeDtypeStruct((T, D), BF16), jax.ShapeDtypeStruct((T, QKVW), BF16),
                   jax.ShapeDtypeStruct((T, 2 * C), BF16), jax.ShapeDtypeStruct((T, 2 * D), BF16)],
        compiler_params=_params("parallel"),
    )(x, g_mix, w_in_t, b_in)


GROWS = GROUP * BLK
BAND = 2 * BLK
ATT_SUB = 2


def _band(i):
    rb = pl.multiple_of(jnp.maximum(i - 1, 0) * BLK, BLK)
    row = lax.broadcasted_iota(jnp.int32, (GROWS, BAND), 0)
    kpos = rb + lax.broadcasted_iota(jnp.int32, (GROWS, BAND), 1)
    qpos = i * BLK + jnp.bitwise_and(row, BLK - 1)
    return rb, jnp.logical_and(kpos <= qpos, kpos > qpos - BLK)


def _sink_column(sink_ref, g):
    head = lax.shift_right_logical(lax.broadcasted_iota(jnp.int32, (GROWS, 1), 0), 7)
    col = jnp.full((GROWS, 1), sink_ref[0, g * GROUP], F32)
    for hh in range(1, GROUP):
        col = jnp.where(head == hh, sink_ref[0, g * GROUP + hh], col)
    return col


def _attn_fwd(qkv, sinks):
    def body(sink_ref, qkv_ref, o_ref, lse_ref, p_ref, s_ref):
        slots = [(sub, g) for sub in range(ATT_SUB) for g in range(NKV)]
        bands = [_band(pl.program_id(0) * ATT_SUB + sub) for sub in range(ATT_SUB)]
        for n, (sub, g) in enumerate(slots):
            rb = bands[sub][0]
            r0 = pl.multiple_of((pl.program_id(0) * ATT_SUB + sub) * BLK, BLK)
            kband = qkv_ref[pl.ds(rb, BAND), AW + g * HD:AW + (g + 1) * HD]
            for hh in range(GROUP):
                h = g * GROUP + hh
                s_ref[n, hh * BLK:(hh + 1) * BLK, :] = _dot_nt(
                    qkv_ref[pl.ds(r0, BLK), h * HD:(h + 1) * HD], kband)
        lses = []
        for n, (sub, g) in enumerate(slots):
            s = jnp.where(bands[sub][1], s_ref[n] * SCALE, NEG)
            sink = _sink_column(sink_ref, g)
            m = jnp.maximum(jnp.max(s, axis=-1, keepdims=True), sink)
            p = jnp.exp(s - m)
            den = jnp.sum(p, axis=-1, keepdims=True) + jnp.exp(sink - m)
            p_ref[n] = (p * (1.0 / den)).astype(BF16)
            lses.append(m + jnp.log(den))
        for n, (sub, g) in enumerate(slots):
            rb = bands[sub][0]
            rows = slice(sub * BLK, (sub + 1) * BLK)
            vband = qkv_ref[pl.ds(rb, BAND), AW + KVW + g * HD:AW + KVW + (g + 1) * HD]
            for hh in range(GROUP):
                h = g * GROUP + hh
                o_ref[rows, h * HD:(h + 1) * HD] = _dot(p_ref[n, hh * BLK:(hh + 1) * BLK, :], vband).astype(BF16)
                lse_ref[rows, h:h + 1] = lses[n][hh * BLK:(hh + 1) * BLK]

    nslot = ATT_SUB * NKV
    return pl.pallas_call(
        body, name="attn_fwd", grid=(T // (ATT_SUB * BLK),),
        in_specs=[pl.BlockSpec(memory_space=pltpu.SMEM), _whole((T, QKVW))],
        out_specs=[_rows(ATT_SUB * BLK, AW), _rows(ATT_SUB * BLK, NQ),
                   pl.BlockSpec((nslot, GROWS, BAND), lambda i: (i, 0, 0))],
        out_shape=[jax.ShapeDtypeStruct((T, AW), BF16), jax.ShapeDtypeStruct((T, NQ), F32),
                   jax.ShapeDtypeStruct((T // BLK * NKV, GROWS, BAND), BF16)],
        scratch_shapes=[pltpu.VMEM((nslot, GROWS, BAND), F32)],
        compiler_params=_params("parallel"),
    )(sinks, qkv)


CONV_TM = 256
CONV_SUB = 32


def _glu(ab):
    a = ab[:, 0:C].astype(F32)
    b = ab[:, C:2 * C].astype(F32)
    return a * _sigmoid(b)


SUBLANES = 8


def _shifted_copies(ref):
    rows = ref.shape[1] - SUBLANES
    for r in range(1, SUBLANES):
        ref[r, 0:rows, :] = ref[0, r:r + rows, :]


def _shifted_rows(ref, start, size):
    r = start % SUBLANES
    return ref[r, start - r:start - r + size, :]


def _conv_fwd(glu, conv_w, conv_b, ln_g, ln_b, dep):
    tm = CONV_TM

    def body(cur_ref, prev_ref, w_ref, cb_ref, g_ref, b_ref, dep_ref, u_ref, c_ref, zs_ref):
        i = pl.program_id(0)
        zprev = _glu(prev_ref[tm - HALO:tm, :])
        zs_ref[0, 0:HALO, :] = jnp.where(i > 0, zprev, 0.0)
        zs_ref[0, HALO:HALO + tm, :] = _glu(cur_ref[...])
        _shifted_copies(zs_ref)
        for s in range(tm // CONV_SUB):
            base = HALO + s * CONV_SUB - (KW - 1)
            acc = jnp.broadcast_to(cb_ref[...], (CONV_SUB, C))
            for j in range(KW):
                acc = acc + w_ref[j:j + 1, :] * _shifted_rows(zs_ref, base + j, CONV_SUB)
            rows = slice(s * CONV_SUB, (s + 1) * CONV_SUB)
            u_ref[rows, :] = acc
            mu = jnp.mean(acc, axis=-1, keepdims=True)
            xc = acc - mu
            var = jnp.mean(xc * xc, axis=-1, keepdims=True)
            y = xc * lax.rsqrt(var + EPS) * g_ref[...] + b_ref[...]
            c_ref[rows, :] = (y * _sigmoid(y)).astype(BF16)

    return pl.pallas_call(
        body, name="conv_fwd", grid=(T // tm,),
        in_specs=[_rows(tm, 2 * C),
                  pl.BlockSpec((tm, 2 * C), lambda i: (jnp.maximum(i - 1, 0), 0)),
                  _whole((KW, C)), _whole((1, C)), _whole((1, C)), _whole((1, C)), _whole((8, 128))],
        out_specs=[_rows(tm, C), _rows(tm, C)],
        out_shape=[jax.ShapeDtypeStruct((T, C), F32), jax.ShapeDtypeStruct((T, C), BF16)],
        scratch_shapes=[pltpu.VMEM((SUBLANES, HALO + tm, C), F32)],
        compiler_params=_params("parallel"),
    )(glu, glu, conv_w, conv_b, ln_g, ln_b, dep)


def _branch_outputs(o, cact, wap_ref, wcp_ref, bcp_ref):
    ya = jnp.concatenate([_dot(o, wap_ref[s]) for s in range(N_CHIPS)], axis=1)
    yc = jnp.concatenate([_dot(cact, wcp_ref[s]) for s in range(N_CHIPS)], axis=1) + bcp_ref[...]
    return ya, yc


def _mix_out(x, o, cact, gl, w_ap, w_cp, b_cp, w_out, dep, begun=None):
    tm = ROW_TM
    steps = T // tm // 2
    first = 0 if begun is None else steps
    rows = lambda n: pl.BlockSpec((tm, n), lambda i: (i + first, 0))
    extra = [] if begun is None else list(begun)

    def body(x_ref, o_ref, c_ref, gl_ref, wap_ref, wcp_ref, bcp_ref, wo_ref, dep_ref, *rest):
        ya_ref, yc_ref, mg_ref, x1_ref = rest[len(extra):]
        ya, yc = _branch_outputs(o_ref[...], c_ref[...], wap_ref, wcp_ref, bcp_ref)
        g0 = _sigmoid(gl_ref[:, 0:D].astype(F32))
        g1 = _sigmoid(gl_ref[:, D:2 * D].astype(F32))
        mg = (g0 * ya + g1 * yc).astype(BF16)
        ya_ref[...] = ya.astype(BF16)
        yc_ref[...] = yc.astype(BF16)
        mg_ref[...] = mg
        x1_ref[...] = x_ref[...] + _dot(mg, wo_ref[...])

    return pl.pallas_call(
        body, name="mix_out_first" if begun is None else "mix_out_second", grid=(steps,),
        in_specs=[rows(D), rows(AW), rows(C), rows(2 * D),
                  _whole((N_CHIPS, AW, D // N_CHIPS)), _whole((N_CHIPS, C, D // N_CHIPS)), _whole((1, D)),
                  _whole((D, D)), _whole((8, 128))] + [ANY_SPEC] * len(extra),
        out_specs=[rows(D), rows(D), rows(D), rows(D)],
        out_shape=[jax.ShapeDtypeStruct((T, D), BF16), jax.ShapeDtypeStruct((T, D), BF16),
                   jax.ShapeDtypeStruct((T, D), BF16), jax.ShapeDtypeStruct((T, D), F32)],
        input_output_aliases={9 + k: k for k in range(len(extra))},
        compiler_params=_params("parallel"),
    )(x, o, cact, gl, w_ap, w_cp, b_cp, w_out, dep, *extra)


def _swiglu_half(h, wg_ref, wu_ref, gu_ref, act_ref):
    gate = _dot(h, wg_ref[0])
    up = _dot(h, wu_ref[0])
    gu_ref[:, 0:FSH] = gate.astype(BF16)
    gu_ref[:, FSH:2 * FSH] = up.astype(BF16)
    act_ref[...] = (gate * _sigmoid(gate) * up).astype(BF16)


def _shard_spec(k):
    return pl.BlockSpec((1, D, FSH), lambda i: (k, 0, 0), pipeline_mode=pl.Buffered(1))


def _ffn_in_first(x1, g_ffn, w_fi, dep):
    tm = ROW_TM

    def body(x_ref, g_ref, wg_ref, wu_ref, dep_ref, h_ref, gu_ref, act_ref):
        xv = x_ref[...]
        r = lax.rsqrt(jnp.mean(xv * xv, axis=-1, keepdims=True) + EPS)
        h = (xv * r * g_ref[...]).astype(BF16)
        h_ref[...] = h
        _swiglu_half(h, wg_ref, wu_ref, gu_ref, act_ref)

    return pl.pallas_call(
        body, name="ffn_in_first", grid=(T // tm,),
        in_specs=[_rows(tm, D), _whole((1, D)), _shard_spec(0), _shard_spec(2), _whole((8, 128))],
        out_specs=[_rows(tm, D), pl.BlockSpec((tm, 2 * FSH), lambda i: (i, 0)),
                   pl.BlockSpec((tm, FSH), lambda i: (i, 0))],
        out_shape=[jax.ShapeDtypeStruct((T, D), BF16), jax.ShapeDtypeStruct((T, 2 * DFF), BF16),
                   jax.ShapeDtypeStruct((T, DFF), BF16)],
        compiler_params=_params("parallel"),
    )(x1, g_ffn, w_fi, w_fi, dep)


def _ffn_in_second(h2, w_fi, gu, act, dep):
    tm = ROW_TM

    def body(h_ref, wg_ref, wu_ref, gu_in, act_in, dep_ref, gu_ref, act_ref):
        _swiglu_half(h_ref[...], wg_ref, wu_ref, gu_ref, act_ref)

    return pl.pallas_call(
        body, name="ffn_in_second", grid=(T // tm,),
        in_specs=[_rows(tm, D), _shard_spec(1), _shard_spec(3), ANY_SPEC, ANY_SPEC, _whole((8, 128))],
        out_specs=[pl.BlockSpec((tm, 2 * FSH), lambda i: (i, 1)), pl.BlockSpec((tm, FSH), lambda i: (i, 1))],
        out_shape=[jax.ShapeDtypeStruct((T, 2 * DFF), BF16), jax.ShapeDtypeStruct((T, DFF), BF16)],
        input_output_aliases={3: 0, 4: 1},
        compiler_params=_params("parallel"),
    )(h2, w_fi, w_fi, gu, act, dep)


def _ffn_out_loss(x1, act, w_dn, g_final, target):
    tm = ROW_TM // 2

    def body(x_ref, a_ref, w_ref, g_ref, t_ref, dx_ref, dxb_ref, dg_ref, loss_ref):
        i = pl.program_id(0)
        x2 = x_ref[...] + _dot(a_ref[...], w_ref[...])
        r = lax.rsqrt(jnp.mean(x2 * x2, axis=-1, keepdims=True) + EPS)
        xh = x2 * r
        g = g_ref[...]
        err = xh * g - t_ref[...]
        dy = err * (1.0 / D)
        dyg = dy * g
        dx = r * (dyg - xh * jnp.mean(dyg * xh, axis=-1, keepdims=True))
        dx_ref[...] = dx
        dxb_ref[...] = dx.astype(BF16)
        part = 0.5 * jnp.sum(jnp.mean(err * err, axis=-1, keepdims=True), axis=0, keepdims=True)

        @pl.when(i == 0)
        def _():
            dg_ref[...] = jnp.zeros_like(dg_ref)
            loss_ref[...] = jnp.zeros_like(loss_ref)

        dg_ref[...] += jnp.sum(dy * xh, axis=0, keepdims=True)
        loss_ref[...] += jnp.broadcast_to(part, loss_ref.shape)

    return pl.pallas_call(
        body, name="ffn_out_loss", grid=(T // tm,),
        in_specs=[_rows(tm, D), _rows(tm, DFF), _whole((DFF, D)), _whole((1, D)), _rows(tm, D)],
        out_specs=[_rows(tm, D), _rows(tm, D), _whole((1, D)), _whole((1, 128))],
        out_shape=[jax.ShapeDtypeStruct((T, D), F32), jax.ShapeDtypeStruct((T, D), BF16),
                   jax.ShapeDtypeStruct((1, D), F32), jax.ShapeDtypeStruct((1, 128), F32)],
        compiler_params=_params("arbitrary"),
    )(x1, act, w_dn, g_final, target)


def _const(shape):
    return pl.BlockSpec(shape, lambda i: tuple(0 for _ in shape), pipeline_mode=pl.Buffered(1))


def _ffn_bwd(dx2, dx2b, gu, x1, g_ffn, w_dn_t, w_fi_t):
    tm = ROW_TM // 2

    def body(dx_ref, dxb_ref, gu_ref, x_ref, g_ref, wdn_ref, wfi_ref,
             dgu_ref, dx1_ref, dx1b_ref, dg_ref):
        i = pl.program_id(0)
        dxb = dxb_ref[...]
        dh = jnp.zeros((tm, D), F32)
        dacts = [_dot_nt(dxb, wdn_ref[k * FSH:(k + 1) * FSH, :]) for k in range(N_CHIPS // 2)]
        for k in range(N_CHIPS // 2):
            c0 = k * FSH
            dact = dacts[k]
            gate = gu_ref[:, 2 * c0:2 * c0 + FSH].astype(F32)
            up = gu_ref[:, 2 * c0 + FSH:2 * c0 + 2 * FSH].astype(F32)
            s = _sigmoid(gate)
            dup = (dact * gate * s).astype(BF16)
            dgate = (dact * up * s * (1.0 + gate * (1.0 - s))).astype(BF16)
            dgu_ref[:, c0:c0 + FSH] = dgate
            dgu_ref[:, DFF + c0:DFF + c0 + FSH] = dup
            dh = dh + _dot_nt(dgate, wfi_ref[k]) + _dot_nt(dup, wfi_ref[k + N_CHIPS // 2])
        xv = x_ref[...]
        r = lax.rsqrt(jnp.mean(xv * xv, axis=-1, keepdims=True) + EPS)
        xh = xv * r
        dhg = dh * g_ref[...]
        dx1 = dx_ref[...] + r * (dhg - xh * jnp.mean(dhg * xh, axis=-1, keepdims=True))
        dx1_ref[...] = dx1
        dx1b_ref[...] = dx1.astype(BF16)

        @pl.when(i == 0)
        def _():
            dg_ref[...] = jnp.zeros_like(dg_ref)

        dg_ref[...] += jnp.sum(dh * xh, axis=0, keepdims=True)

    return pl.pallas_call(
        body, name="ffn_bwd", grid=(T // tm,),
        in_specs=[_rows(tm, D), _rows(tm, D), _rows(tm, 2 * DFF), _rows(tm, D), _whole((1, D)),
                  _const((DFF, D)), _const((N_CHIPS, D, FSH))],
        out_specs=[_rows(tm, 2 * DFF), _rows(tm, D), _rows(tm, D), _whole((1, D))],
        out_shape=[jax.ShapeDtypeStruct((T, 2 * DFF), BF16), jax.ShapeDtypeStruct((T, D), F32),
                   jax.ShapeDtypeStruct((T, D), BF16), jax.ShapeDtypeStruct((1, D), F32)],
        compiler_params=_params("arbitrary"),
    )(dx2, dx2b, gu, x1, g_ffn, w_dn_t, w_fi_t)


def _mix_bwd(dx1b, gl, ya, yc, w_out, w_ap, w_cp, dep):
    tm = ROW_TM

    def body(dx_ref, gl_ref, ya_ref, yc_ref, wo_ref, wap_ref, wcp_ref, dep_ref,
             dya_ref, dyc_ref, dgl_ref, do_ref, dc_ref, db_ref):
        i = pl.program_id(0)
        dm = _dot_nt(dx_ref[...], wo_ref[...])
        ya, yc = ya_ref[...].astype(F32), yc_ref[...].astype(F32)
        g0 = _sigmoid(gl_ref[:, 0:D].astype(F32))
        g1 = _sigmoid(gl_ref[:, D:2 * D].astype(F32))
        dya = dm * g0
        dyc = dm * g1
        dgl_ref[:, 0:D] = (dya * ya * (1.0 - g0)).astype(BF16)
        dgl_ref[:, D:2 * D] = (dyc * yc * (1.0 - g1)).astype(BF16)
        dyab = dya.astype(BF16)
        dycb = dyc.astype(BF16)
        dya_ref[...] = dyab
        dyc_ref[...] = dycb
        sw = D // N_CHIPS
        do = jnp.zeros((tm, AW), F32)
        dcv = jnp.zeros((tm, C), F32)
        for s in range(N_CHIPS):
            do = do + _dot_nt(dyab[:, s * sw:(s + 1) * sw], wap_ref[s])
            dcv = dcv + _dot_nt(dycb[:, s * sw:(s + 1) * sw], wcp_ref[s])
        do_ref[...] = do.astype(BF16)
        dc_ref[...] = dcv.astype(BF16)

        @pl.when(i == 0)
        def _():
            db_ref[...] = jnp.zeros_like(db_ref)

        db_ref[...] += jnp.sum(dyc, axis=0, keepdims=True)

    return pl.pallas_call(
        body, name="mix_bwd", grid=(T // tm,),
        in_specs=[_rows(tm, D), _rows(tm, 2 * D), _rows(tm, D), _rows(tm, D),
                  _whole((D, D)), _whole((N_CHIPS, AW, D // N_CHIPS)), _whole((N_CHIPS, C, D // N_CHIPS)),
                  _whole((8, 128))],
        out_specs=[_rows(tm, D), _rows(tm, D), _rows(tm, 2 * D), _rows(tm, AW), _rows(tm, C),
                   _whole((1, D))],
        out_shape=[jax.ShapeDtypeStruct((T, D), BF16), jax.ShapeDtypeStruct((T, D), BF16),
                   jax.ShapeDtypeStruct((T, 2 * D), BF16), jax.ShapeDtypeStruct((T, AW), BF16),
                   jax.ShapeDtypeStruct((T, C), BF16), jax.ShapeDtypeStruct((1, D), F32)],
        compiler_params=_params("arbitrary"),
    )(dx1b, gl, ya, yc, w_out, w_ap, w_cp, dep)


def _conv_bwd(glu, u, dc, conv_w, ln_g, ln_b, dep):
    tm = CONV_TM
    nblk = T // tm

    def du_of(uv, dcv, g_ref, b_ref):
        mu = jnp.mean(uv, axis=-1, keepdims=True)
        xc = uv - mu
        var = jnp.mean(xc * xc, axis=-1, keepdims=True)
        rstd = lax.rsqrt(var + EPS)
        xh = xc * rstd
        y = xh * g_ref[...] + b_ref[...]
        sg = _sigmoid(y)
        dy = dcv * (sg * (1.0 + y * (1.0 - sg)))
        dxh = dy * g_ref[...]
        du = rstd * (dxh - jnp.mean(dxh, axis=-1, keepdims=True)
                     - xh * jnp.mean(dxh * xh, axis=-1, keepdims=True))
        return du, dy, xh

    def body(cur_ref, prev_ref, u_ref, un_ref, dc_ref, dcn_ref, w_ref, g_ref, b_ref, dep_ref,
             dglu_ref, dw_ref, dcb_ref, dg_ref, db_ref, zs_ref, dus_ref):
        i = pl.program_id(0)

        @pl.when(i == 0)
        def _():
            dw_ref[...] = jnp.zeros_like(dw_ref)
            dcb_ref[...] = jnp.zeros_like(dcb_ref)
            dg_ref[...] = jnp.zeros_like(dg_ref)
            db_ref[...] = jnp.zeros_like(db_ref)

        zprev = _glu(prev_ref[tm - HALO:tm, :])
        zs_ref[0, 0:HALO, :] = jnp.where(i > 0, zprev, 0.0)
        zs_ref[0, HALO:HALO + tm, :] = _glu(cur_ref[...])
        _shifted_copies(zs_ref)

        dun, _, _ = du_of(un_ref[0:HALO, :], dcn_ref[0:HALO, :].astype(F32), g_ref, b_ref)
        dus_ref[0, tm:tm + HALO, :] = jnp.where(i < nblk - 1, dun, 0.0)
        dg_acc = jnp.zeros((1, C), F32)
        db_acc = jnp.zeros((1, C), F32)
        dcb_acc = jnp.zeros((1, C), F32)
        for s in range(tm // CONV_SUB):
            rows = slice(s * CONV_SUB, (s + 1) * CONV_SUB)
            du, dy, xh = du_of(u_ref[rows, :], dc_ref[rows, :].astype(F32), g_ref, b_ref)
            dus_ref[0, rows, :] = du
            dg_acc = dg_acc + jnp.sum(dy * xh, axis=0, keepdims=True)
            db_acc = db_acc + jnp.sum(dy, axis=0, keepdims=True)
            dcb_acc = dcb_acc + jnp.sum(du, axis=0, keepdims=True)
        dg_ref[...] += dg_acc
        db_ref[...] += db_acc
        dcb_ref[...] += dcb_acc
        _shifted_copies(dus_ref)

        for j in range(KW):
            acc = jnp.zeros((CONV_SUB, C), F32)
            for s in range(tm // CONV_SUB):
                base = HALO + s * CONV_SUB - (KW - 1) + j
                acc = acc + dus_ref[0, s * CONV_SUB:(s + 1) * CONV_SUB, :] * _shifted_rows(zs_ref, base, CONV_SUB)
            dw_ref[j:j + 1, :] += jnp.sum(acc, axis=0, keepdims=True)

        for s in range(tm // CONV_SUB):
            rows = slice(s * CONV_SUB, (s + 1) * CONV_SUB)
            dz = jnp.zeros((CONV_SUB, C), F32)
            for j in range(KW):
                dz = dz + w_ref[j:j + 1, :] * _shifted_rows(dus_ref, s * CONV_SUB + (KW - 1) - j, CONV_SUB)
            a = cur_ref[rows, 0:C].astype(F32)
            sb = _sigmoid(cur_ref[rows, C:2 * C].astype(F32))
            dglu_ref[rows, 0:C] = (dz * sb).astype(BF16)
            dglu_ref[rows, C:2 * C] = (dz * a * sb * (1.0 - sb)).astype(BF16)

    nxt = lambda i: (jnp.minimum(i + 1, nblk - 1), 0)
    return pl.pallas_call(
        body, name="conv_bwd", grid=(nblk,),
        in_specs=[_rows(tm, 2 * C),
                  pl.BlockSpec((tm, 2 * C), lambda i: (jnp.maximum(i - 1, 0), 0)),
                  _rows(tm, C), pl.BlockSpec((tm, C), nxt),
                  _rows(tm, C), pl.BlockSpec((tm, C), nxt),
                  _whole((KW, C)), _whole((1, C)), _whole((1, C)), _whole((8, 128))],
        out_specs=[_rows(tm, 2 * C), _whole((KW, C)), _whole((1, C)), _whole((1, C)), _whole((1, C))],
        out_shape=[jax.ShapeDtypeStruct((T, 2 * C), BF16), jax.ShapeDtypeStruct((KW, C), F32),
                   jax.ShapeDtypeStruct((1, C), F32), jax.ShapeDtypeStruct((1, C), F32),
                   jax.ShapeDtypeStruct((1, C), F32)],
        scratch_shapes=[pltpu.VMEM((SUBLANES, HALO + tm, C), F32), pltpu.VMEM((SUBLANES, tm + HALO, C), F32)],
        compiler_params=_params("arbitrary"),
    )(glu, glu, u, u, dc, dc, conv_w, ln_g, ln_b, dep)


def _attn_bwd(qkv, do, lse, p, sinks, dep):
    nsub = 1

    def body(sink_ref, qkv_ref, do_ref, lse_ref, p_ref, dep_ref, dq_ref, dkv_ref, ds_ref,
             dp_ref, dsb_ref):
        i = pl.program_id(0)

        @pl.when(i == 0)
        def _():
            dkv_ref[...] = jnp.zeros_like(dkv_ref)
            ds_ref[...] = jnp.zeros_like(ds_ref)

        slots = [(sub, g) for sub in range(nsub) for g in range(NKV)]
        bands = [_band(i * nsub + sub) for sub in range(nsub)]
        r0s = [pl.multiple_of((i * nsub + sub) * BLK, BLK) for sub in range(nsub)]
        lses = []
        for n, (sub, g) in enumerate(slots):
            rb, blk = bands[sub][0], slice(sub * BLK, (sub + 1) * BLK)
            vband = qkv_ref[pl.ds(rb, BAND), AW + KVW + g * HD:AW + KVW + (g + 1) * HD]
            lse_parts = []
            for hh in range(GROUP):
                h = g * GROUP + hh
                dp_ref[n, hh * BLK:(hh + 1) * BLK, :] = _dot_nt(do_ref[blk, h * HD:(h + 1) * HD], vband)
                lse_parts.append(lse_ref[blk, h:h + 1])
            lses.append(jnp.concatenate(lse_parts, axis=0))
        dsinks = []
        for n, (sub, g) in enumerate(slots):
            pf, dpv = p_ref[n].astype(F32), dp_ref[n]
            dl = jnp.sum(pf * dpv, axis=-1, keepdims=True)
            dsb_ref[n] = (pf * (dpv - dl)).astype(BF16)
            dsinks.append(-(jnp.exp(_sink_column(sink_ref, g) - lses[n]) * dl))
        for n, (sub, g) in enumerate(slots):
            rb, blk = bands[sub][0], slice(sub * BLK, (sub + 1) * BLK)
            kband = qkv_ref[pl.ds(rb, BAND), AW + g * HD:AW + (g + 1) * HD]
            dk = jnp.zeros((BAND, HD), F32)
            dv = jnp.zeros((BAND, HD), F32)
            for hh in range(GROUP):
                h = g * GROUP + hh
                hcol = slice(h * HD, (h + 1) * HD)
                rows = slice(hh * BLK, (hh + 1) * BLK)
                dq_ref[blk, hcol] = (_dot(dsb_ref[n, rows, :], kband) * SCALE).astype(BF16)
                dk = dk + _dot_tn(dsb_ref[n, rows, :], qkv_ref[pl.ds(r0s[sub], BLK), hcol])
                dv = dv + _dot_tn(p_ref[n, rows, :], do_ref[blk, hcol])
                ds_ref[h:h + 1, :] += jnp.broadcast_to(
                    jnp.sum(dsinks[n][rows], axis=0, keepdims=True), (1, 128))
            dkv_ref[pl.ds(rb, BAND), g * HD:(g + 1) * HD] += dk * SCALE
            dkv_ref[pl.ds(rb, BAND), KVW + g * HD:KVW + (g + 1) * HD] += dv

    nslot, tq = nsub * NKV, nsub * BLK
    return pl.pallas_call(
        body, name="attn_bwd", grid=(T // tq,),
        in_specs=[pl.BlockSpec(memory_space=pltpu.SMEM), _whole((T, QKVW)),
                  _rows(tq, AW), _rows(tq, NQ),
                  pl.BlockSpec((nslot, GROWS, BAND), lambda i: (i, 0, 0)), _whole((8, 128))],
        out_specs=[_rows(tq, AW), _whole((T, 2 * KVW)), _whole((NQ, 128))],
        out_shape=[jax.ShapeDtypeStruct((T, AW), BF16), jax.ShapeDtypeStruct((T, 2 * KVW), F32),
                   jax.ShapeDtypeStruct((NQ, 128), F32)],
        scratch_shapes=[pltpu.VMEM((nslot, GROWS, BAND), F32), pltpu.VMEM((nslot, GROWS, BAND), BF16)],
        compiler_params=_params("arbitrary"),
    )(sinks, qkv, do, lse, p, dep)


PROJ_PARTS = [(0, AW), (AW, QKVW), (GLU_OFF, GATE_OFF), (GATE_OFF, INW)]


def _in_proj_bwd(dq, dkv, dglu, dgl, x, dx1, g_mix, w_in_t, dep):
    tm = ROW_TM

    def body(dq_ref, dkv_ref, dglu_ref, dgl_ref, x_ref, dx1_ref, g_ref, w_ref, dep_ref,
             gx_ref, dg_ref, db_ref):
        i = pl.program_id(0)

        @pl.when(i == 0)
        def _():
            dg_ref[...] = jnp.zeros_like(dg_ref)
            db_ref[...] = jnp.zeros_like(db_ref)

        dh = jnp.zeros((tm, D), F32)
        for part_ref, (lo, hi) in zip((dq_ref, dkv_ref, dglu_ref, dgl_ref), PROJ_PARTS):
            part = part_ref[...]
            dh = dh + _dot(part.astype(BF16), w_ref[lo:hi, :])
            db_ref[:, lo:hi] += jnp.sum(part.astype(F32), axis=0, keepdims=True)
        xv = x_ref[...]
        r = lax.rsqrt(jnp.mean(xv * xv, axis=-1, keepdims=True) + EPS)
        xh = xv * r
        dhg = dh * g_ref[...]
        gx_ref[...] = dx1_ref[...] + r * (dhg - xh * jnp.mean(dhg * xh, axis=-1, keepdims=True))
        dg_ref[...] += jnp.sum(dh * xh, axis=0, keepdims=True)

    return pl.pallas_call(
        body, name="in_proj_bwd", grid=(T // tm,),
        in_specs=[_rows(tm, AW), _rows(tm, 2 * KVW), _rows(tm, 2 * C), _rows(tm, 2 * D),
                  _rows(tm, D), _rows(tm, D), _whole((1, D)), _const((INW, D)), _whole((8, 128))],
        out_specs=[_rows(tm, D), _whole((1, D)), _whole((1, INW))],
        out_shape=[jax.ShapeDtypeStruct((T, D), F32), jax.ShapeDtypeStruct((1, D), F32),
                   jax.ShapeDtypeStruct((1, INW), F32)],
        compiler_params=_params("arbitrary"),
    )(dq, dkv, dglu, dgl, x, dx1, g_mix, w_in_t, dep)


def _grad_w_in_t(h, dq, dkv, dglu, dgl):
    tn, chunk = 512, 256

    def body(h_ref, dq_ref, dkv_ref, dglu_ref, dgl_ref, o_ref):
        hv = h_ref[...]
        for part_ref, (lo, hi) in zip((dq_ref, dkv_ref, dglu_ref, dgl_ref), PROJ_PARTS):
            for c0 in range(0, hi - lo, chunk):
                o_ref[lo + c0:lo + c0 + chunk, :] = _dot_tn(
                    part_ref[:, c0:c0 + chunk].astype(BF16), hv).astype(BF16)

    return pl.pallas_call(
        body, name="grad_w_in", grid=(D // tn,),
        in_specs=[pl.BlockSpec((T, tn), lambda j: (0, j)), _const((T, AW)), _const((T, 2 * KVW)),
                  _const((T, 2 * C)), _const((T, 2 * D))],
        out_specs=pl.BlockSpec((INW, tn), lambda j: (0, j)),
        out_shape=jax.ShapeDtypeStruct((INW, D), BF16),
        compiler_params=_params("parallel"),
    )(h, dq, dkv, dglu, dgl)


def _grad_w(a, b, name, tk, tn, col_sharded):
    k, n = a.shape[1], b.shape[1]

    single = n == tn
    sw = n // N_CHIPS

    def body(a_ref, b_ref, o_ref, at_ref):
        if single:
            res = _dot_tn(a_ref[...], b_ref[...]).astype(BF16)
            if col_sharded:
                for s in range(N_CHIPS):
                    o_ref[s] = res[:, s * sw:(s + 1) * sw]
            else:
                o_ref[...] = res
            return

        @pl.when(pl.program_id(1) == 0)
        def _():
            at_ref[...] = a_ref[...].T

        o_ref[...] = _dot(at_ref[...], b_ref[...]).astype(BF16)

    if col_sharded and single:
        shape = (N_CHIPS, k, sw)
        out_spec = pl.BlockSpec((N_CHIPS, tk, sw), lambda i, j: (0, i, 0))
    elif col_sharded:
        per = sw // tn
        shape = (N_CHIPS, k, sw)
        out_spec = pl.BlockSpec((None, tk, tn), lambda i, j: (j // per, i, j % per))
    else:
        shape = (1, k, n)
        out_spec = pl.BlockSpec((None, tk, tn), lambda i, j: (0, i, j))
    out = pl.pallas_call(
        body, name=name, grid=(k // tk, n // tn),
        in_specs=[pl.BlockSpec((T, tk), lambda i, j: (0, i)), pl.BlockSpec((T, tn), lambda i, j: (0, j))],
        out_specs=out_spec,
        out_shape=jax.ShapeDtypeStruct(shape, BF16),
        scratch_shapes=[pltpu.VMEM((tk, T), BF16)],
        compiler_params=_params("parallel", "arbitrary"),
    )(a, b)
    return out if col_sharded else out.reshape(N_CHIPS, k // N_CHIPS, n)


HBM_SPEC = pl.BlockSpec(memory_space=pltpu.HBM)


def _place():
    x, y, c = lax.axis_index("x"), lax.axis_index("y"), lax.axis_index("c")
    chips = [(1 - x, y), (x, 1 - y), (1 - x, 1 - y)]
    return x, y, c, chips


SEM_SPEC = pl.BlockSpec(memory_space=pltpu.SEMAPHORE)
ANY_SPEC = pl.BlockSpec(memory_space=pl.ANY)
VMEM_SPEC = pl.BlockSpec(memory_space=pltpu.VMEM)
EFFECT = pltpu.SideEffectType.DATAFLOW_SIDE_EFFECTING


def _gather_ends(src, land, x, y, c, chips):
    kh = src.shape[0] // 2
    s_me = 2 * x + y
    ends = [(src.at[pl.ds(c * kh, kh)], land.at[s_me, pl.ds(c * kh, kh)], (*chip, c)) for chip in chips]
    return ends + [(src, land.at[s_me], (x, y, 1 - c))]


def _reduce_ends(src, land, x, y, c, chips):
    return [(src.at[2 * chip[0] + chip[1]], land.at[j], (*chip, c)) for j, chip in enumerate(chips)]


def _chip_copies(ends, srcs, lands, send_sems, recv_sems, first=0):
    x, y, c, chips = _place()
    copies = []
    for src, land in zip(srcs, lands):
        peers = ends(src, land, x, y, c, chips)
        for s, d, to in peers:
            k = first * len(peers) + len(copies)
            copies.append(pltpu.make_async_remote_copy(
                src_ref=s, dst_ref=d, send_sem=send_sems.at[k], recv_sem=recv_sems.at[k],
                device_id=to, device_id_type=MESH))
    return copies


GATHER_PEERS, REDUCE_PEERS = 4, 3


def _handshake(shake):
    x, y, c, chips = _place()
    peers = ([(x, y, 1 - c)] if shake in ("pair", "both") else []) + (
        [(*chip, c) for chip in chips] if shake in ("chips", "both") else [])
    barrier = pltpu.get_barrier_semaphore()
    for peer in peers:
        pl.semaphore_signal(barrier, inc=1, device_id=peer, device_id_type=MESH)
    pl.semaphore_wait(barrier, len(peers))


def _chip_start(name, ends, peers, srcs, lands, shake=None):
    n = len(srcs)

    def body(*refs):
        if shake is not None:
            _handshake(shake[0])
        copies = _chip_copies(ends, refs[:n], refs[n:2 * n], refs[2 * n], refs[2 * n + 1])
        for cp in copies:
            cp.start()
        token = refs[-1]
        token[...] = jnp.zeros_like(token)

    hbm = lambda a: pltpu.HBM(a.shape, a.dtype)
    res = pl.pallas_call(
        body, name=name,
        out_shape=(pltpu.SemaphoreType.DMA((peers * n,)), pltpu.SemaphoreType.DMA((peers * n,)),
                   *[hbm(a) for a in srcs], *[hbm(a) for a in lands],
                   jax.ShapeDtypeStruct((8, 128), F32)),
        in_specs=[HBM_SPEC] * (2 * n),
        out_specs=(SEM_SPEC, SEM_SPEC, *[HBM_SPEC] * (2 * n), VMEM_SPEC),
        input_output_aliases={i: 2 + i for i in range(2 * n)},
        compiler_params=pltpu.CompilerParams(has_side_effects=EFFECT,
                                             collective_id=None if shake is None else shake[1]),
    )(*[pltpu.with_memory_space_constraint(a, pltpu.HBM) for a in (*srcs, *lands)])
    return res[0], res[1], list(res[2:2 + n]), list(res[2 + n:2 + 2 * n]), res[-1]


def _chip_wait(name, ends, send_sems, recv_sems, srcs, lands, after, first=0):
    n, na = len(srcs), len(after)

    def body(*refs):
        copies = _chip_copies(ends, refs[:n], refs[n:2 * n], refs[2 * n], refs[2 * n + 1], first)
        for cp in copies:
            cp.wait_send()
            cp.wait_recv()

    hbm = lambda a: pltpu.HBM(a.shape, a.dtype)
    res = pl.pallas_call(
        body, name=name,
        out_shape=tuple(hbm(a) for a in (*srcs, *lands)),
        in_specs=[HBM_SPEC] * (2 * n) + [SEM_SPEC, SEM_SPEC] + [ANY_SPEC] * na,
        out_specs=tuple([HBM_SPEC] * (2 * n)),
        input_output_aliases={i: i for i in range(2 * n)},
        compiler_params=pltpu.CompilerParams(has_side_effects=EFFECT),
    )(*srcs, *lands, send_sems, recv_sems, *after)
    return list(res[:n]), list(res[n:])


def _forward_copies(lands, send_sems, recv_sems):
    x, y, c, chips = _place()
    copies = []
    for land in lands:
        kh = land.shape[1] // 2
        for chip in chips:
            blk = land.at[2 * chip[0] + chip[1], pl.ds(c * kh, kh)]
            k = len(copies)
            copies.append(pltpu.make_async_remote_copy(
                src_ref=blk, dst_ref=blk, send_sem=send_sems.at[k], recv_sem=recv_sems.at[k],
                device_id=(x, y, 1 - c), device_id_type=MESH))
    return copies


def _gather_relay(name, send_sems, recv_sems, srcs, lands, after, first, shake):
    n, na = len(srcs), len(after)

    def body(*refs):
        _handshake(shake[0])
        land_refs = refs[n:2 * n]
        for cp in _chip_copies(_gather_ends, refs[:n], land_refs, refs[2 * n], refs[2 * n + 1], first):
            cp.wait_send()
            cp.wait_recv()
        out = refs[2 * n + 2 + na:]
        for cp in _forward_copies(land_refs, out[0], out[1]):
            cp.start()
        out[-1][...] = jnp.zeros_like(out[-1])

    hbm = lambda a: pltpu.HBM(a.shape, a.dtype)
    res = pl.pallas_call(
        body, name=name,
        out_shape=(pltpu.SemaphoreType.DMA((3 * n,)), pltpu.SemaphoreType.DMA((3 * n,)),
                   *[hbm(a) for a in lands], jax.ShapeDtypeStruct((8, 128), F32)),
        in_specs=[HBM_SPEC] * (2 * n) + [SEM_SPEC, SEM_SPEC] + [ANY_SPEC] * na,
        out_specs=(SEM_SPEC, SEM_SPEC, *[HBM_SPEC] * n, VMEM_SPEC),
        input_output_aliases={n + i: 2 + i for i in range(n)},
        compiler_params=pltpu.CompilerParams(has_side_effects=EFFECT, collective_id=shake[1]),
    )(*srcs, *lands, send_sems, recv_sems, *after)
    return res[0], res[1], list(res[2:2 + n]), res[-1]


def _forward_wait(name, send_sems, recv_sems, lands, after):
    n, na = len(lands), len(after)

    def body(*refs):
        for cp in _forward_copies(refs[:n], refs[n], refs[n + 1]):
            cp.wait_send()
            cp.wait_recv()

    hbm = lambda a: pltpu.HBM(a.shape, a.dtype)
    res = pl.pallas_call(
        body, name=name,
        out_shape=tuple(hbm(a) for a in lands),
        in_specs=[HBM_SPEC] * n + [SEM_SPEC, SEM_SPEC] + [ANY_SPEC] * na,
        out_specs=tuple([HBM_SPEC] * n),
        input_output_aliases={i: i for i in range(n)},
        compiler_params=pltpu.CompilerParams(has_side_effects=EFFECT),
    )(*lands, send_sems, recv_sems, *after)
    return list(res)


def _exchange_ends(src, land, x, y, c, chips):
    kh = src.shape[1] // 2
    return [(src.at[:, pl.ds((1 - c) * kh, kh)], land, (x, y, 1 - c))]


def _share_ends(src, land, x, y, c, chips):
    return [(src, land, (x, y, 1 - c))]


def _small_ends(src, land, x, y, c, chips):
    m = src.shape[0]
    rows = land.at[pl.ds((4 * x + 2 * y + c) * m, m)]
    peers = [(x, y, 1 - c)] + [(*chip, c) for chip in chips] + [(*chip, 1 - c) for chip in chips]
    return [(src, rows, to) for to in peers]


PAIR_PEERS, SMALL_PEERS = 1, 7
SHAKES = {"pair_1": ("pair", 0), "pair_2": ("pair", 1), "share_1": ("pair", 2), "share_2": ("pair", 3),
          "chip_1": ("chips", 4), "chip_2": ("chips", 5), "gather_b": ("both", 6),
          "relay_a": ("pair", 7), "relay_b1": ("pair", 8), "relay_b2": ("pair", 9), "relay_b3": ("pair", 10)}


def _row_tile(k):
    for t in (256, 240, 128, 176, 64, 32, 16):
        if k % t == 0:
            return t
    raise ValueError(k)


def _pair_sum(c_idx, g, got, name):
    _, k, n = g.shape
    kh = k // 2
    tm = _row_tile(kh)
    nb = kh // tm

    def body(c_ref, g_ref, r_ref, o_ref):
        o_ref[...] = (g_ref[...].astype(F32) + r_ref[...].astype(F32)).astype(BF16)

    return pl.pallas_call(
        body, name=name,
        grid_spec=pltpu.PrefetchScalarGridSpec(
            num_scalar_prefetch=1, grid=(nb,),
            in_specs=[pl.BlockSpec((N_CHIPS, tm, n), lambda i, c_ref: (0, c_ref[0] * nb + i, 0)),
                      pl.BlockSpec((N_CHIPS, tm, n), lambda i, c_ref: (0, i, 0))],
            out_specs=pl.BlockSpec((N_CHIPS, tm, n), lambda i, c_ref: (0, i, 0))),
        out_shape=jax.ShapeDtypeStruct((N_CHIPS, kh, n), BF16),
        compiler_params=_params("parallel"),
    )(c_idx, g, got)


def _chip_sum(s_idx, mine, got, name):
    _, kh, n = mine.shape
    tm = _row_tile(kh)

    def body(s_ref, m_ref, r_ref, o_ref):
        acc = m_ref[0].astype(F32)
        for j in range(3):
            acc = acc + r_ref[j].astype(F32)
        o_ref[...] = acc

    return pl.pallas_call(
        body, name=name,
        grid_spec=pltpu.PrefetchScalarGridSpec(
            num_scalar_prefetch=1, grid=(kh // tm,),
            in_specs=[pl.BlockSpec((1, tm, n), lambda i, s_ref: (s_ref[0], i, 0)),
                      pl.BlockSpec((3, tm, n), lambda i, s_ref: (0, i, 0))],
            out_specs=pl.BlockSpec((tm, n), lambda i, s_ref: (i, 0))),
        out_shape=jax.ShapeDtypeStruct((kh, n), F32),
        compiler_params=_params("parallel"),
    )(s_idx, mine, got)


def _adamw_math(w, g, m, v):
    m = ADAM_B1 * m + (1.0 - ADAM_B1) * g
    v = ADAM_B2 * v + (1.0 - ADAM_B2) * (g * g)
    m_hat = m / (1.0 - ADAM_B1 ** ADAM_STEP)
    v_hat = v / (1.0 - ADAM_B2 ** ADAM_STEP)
    delta = -ADAM_LR * (m_hat / (jnp.sqrt(v_hat) + ADAM_EPS) + ADAM_WD * w)
    return delta, m, v


def _adamw(c_idx, w, g_mine, g_other, m, v, name):
    k, n = w.shape
    tm = k // 4

    def body(c_ref, w_ref, gm_ref, go_ref, m_ref, v_ref, g_ref, d_ref, mo_ref, vo_ref):
        g = jnp.where(pl.program_id(0) == c_ref[0], gm_ref[...], go_ref[...])
        d, mm, vv = _adamw_math(w_ref[...], g, m_ref[...], v_ref[...])
        g_ref[...] = g
        d_ref[...] = d
        mo_ref[...] = mm
        vo_ref[...] = vv

    full = pl.BlockSpec((tm, n), lambda h, i, c_ref: (2 * h + i, 0))
    mine = pl.BlockSpec((tm, n), lambda h, i, c_ref: (jnp.where(h == c_ref[0], i, 0), 0))
    other = pl.BlockSpec((tm, n), lambda h, i, c_ref: (jnp.where(h == c_ref[0], 0, i), 0))
    shp = jax.ShapeDtypeStruct((k, n), F32)
    return pl.pallas_call(
        body, name=name,
        grid_spec=pltpu.PrefetchScalarGridSpec(
            num_scalar_prefetch=1, grid=(2, 2),
            in_specs=[full, mine, other, full, full], out_specs=[full] * 4),
        out_shape=[shp] * 4, compiler_params=_params("arbitrary", "arbitrary"),
    )(c_idx, w, g_mine, g_other, m, v)


VEC_SLOTS = {
    "g_mix_norm": (0, 0, D), "b_conv_proj": (0, D, D), "g_ffn_norm": (0, 2 * D, D),
    "g_final": (0, 3 * D, D), "b_in": (1, 0, INW), "conv_b": (2, 0, C), "ln_g": (2, C, C),
    "ln_b": (2, 2 * C, C), "sinks": (2, 3 * C, NQ), "loss": (2, 3 * C + 128, 1),
}
VEC_ROWS, VEC_COLS = 8, 4 * D
CW_ROWS = 32
SMALL_NAMES = ["g_mix_norm", "b_in", "sinks", "conv_w", "conv_b", "ln_g", "ln_b",
               "b_conv_proj", "g_ffn_norm", "g_final"]
CW_LANES = C // N_CHIPS


def _pack_small(gs, loss):
    row0 = jnp.concatenate([gs["g_mix_norm"], gs["b_conv_proj"], gs["g_ffn_norm"], gs["g_final"]], axis=1)
    row1 = jnp.pad(gs["b_in"], ((0, 0), (0, VEC_COLS - INW)))
    row2 = jnp.concatenate([gs["conv_b"], gs["ln_g"], gs["ln_b"],
                            jnp.pad(gs["sinks"], ((0, 0), (0, 128 - NQ))),
                            jnp.pad(loss.reshape(1, 1), ((0, 0), (0, VEC_COLS - 3 * C - 129)))], axis=1)
    vec = jnp.concatenate([row0, row1, row2, jnp.zeros((VEC_ROWS - 3, VEC_COLS), F32)], axis=0)
    cw = jnp.pad(gs["conv_w"], ((0, CW_ROWS - KW), (0, 0)))
    return vec, cw


def _small_update(idx, vec_own, cw_own, vec_all, cw_all, wmv):
    nsm = len(SMALL_NAMES)

    def body(s_ref, vown_ref, cown_ref, vec_ref, cw_ref, *refs):
        ins = refs[:3 * nsm]
        outs = refs[3 * nsm:7 * nsm]
        loss_ref = refs[7 * nsm]
        me = s_ref[1]

        def summed(own_ref, table_ref, rows_per_dev, r0, nrows, lane, width):
            acc = None
            for k in range(8):
                piece = jnp.where(me == k, own_ref[r0:r0 + nrows, lane:lane + width],
                                  table_ref[k * rows_per_dev + r0:k * rows_per_dev + r0 + nrows, lane:lane + width])
                acc = piece if acc is None else acc + piece
            return acc

        def total(slot):
            row, lane, width = slot
            return summed(vown_ref, vec_ref, VEC_ROWS, row, 1, lane, width)

        loss_ref[...] = jnp.broadcast_to(total(VEC_SLOTS["loss"]), loss_ref.shape)
        for p, name in enumerate(SMALL_NAMES):
            w_ref, m_ref, v_ref = ins[3 * p:3 * p + 3]
            g_ref, d_ref, mo_ref, vo_ref = outs[4 * p:4 * p + 4]
            if name == "conv_w":
                g = jnp.zeros((KW, CW_LANES), F32)
                for s in range(N_CHIPS):
                    cand = summed(cown_ref, cw_ref, CW_ROWS, 0, KW, s * CW_LANES, CW_LANES)
                    g = jnp.where(s_ref[0] == s, cand, g)
            else:
                g = total(VEC_SLOTS[name])
            d, mm, vv = _adamw_math(w_ref[...], g, m_ref[...], v_ref[...])
            g_ref[...] = g
            d_ref[...] = d
            mo_ref[...] = mm
            vo_ref[...] = vv

    vmem = pl.BlockSpec(memory_space=pltpu.VMEM)
    flat = [a for t in wmv for a in t]
    out_shape = []
    for w, _, _ in wmv:
        out_shape += [jax.ShapeDtypeStruct(w.shape, F32)] * 4
    out_shape.append(jax.ShapeDtypeStruct((1, 128), F32))
    res = pl.pallas_call(
        body, name="small_update",
        in_specs=[pl.BlockSpec(memory_space=pltpu.SMEM)] + [vmem] * (4 + len(flat)),
        out_specs=[vmem] * len(out_shape), out_shape=out_shape,
    )(idx, vec_own, cw_own, vec_all, cw_all, *flat)
    return [tuple(res[4 * p:4 * p + 4]) for p in range(nsm)], res[4 * nsm]


WEIGHT_ORDER = ["g_mix_norm", "w_in", "b_in", "sinks", "conv_w", "conv_b", "ln_g", "ln_b",
                "w_attn_proj", "w_conv_proj", "b_conv_proj", "w_out", "g_ffn_norm", "w_ffn_in",
                "w_ffn_down", "g_final"]


def kernel(x, g_mix_norm, w_in, b_in, sinks, conv_w, conv_b, ln_g, ln_b, w_attn_proj, w_conv_proj, b_conv_proj, w_out, g_ffn_norm, w_ffn_in, w_ffn_down, g_final, loss_target, m_g_mix_norm, m_w_in, m_b_in, m_sinks, m_conv_w, m_conv_b, m_ln_g, m_ln_b, m_w_attn_proj, m_w_conv_proj, m_b_conv_proj, m_w_out, m_g_ffn_norm, m_w_ffn_in, m_w_ffn_down, m_g_final, v_g_mix_norm, v_w_in, v_b_in, v_sinks, v_conv_w, v_conv_b, v_ln_g, v_ln_b, v_w_attn_proj, v_w_conv_proj, v_b_conv_proj, v_w_out, v_g_ffn_norm, v_w_ffn_in, v_w_ffn_down, v_g_final):
    w = dict(g_mix_norm=g_mix_norm, w_in=w_in, b_in=b_in, sinks=sinks, conv_w=conv_w, conv_b=conv_b,
             ln_g=ln_g, ln_b=ln_b, w_attn_proj=w_attn_proj, w_conv_proj=w_conv_proj,
             b_conv_proj=b_conv_proj, w_out=w_out, g_ffn_norm=g_ffn_norm, w_ffn_in=w_ffn_in,
             w_ffn_down=w_ffn_down, g_final=g_final)
    m = dict(g_mix_norm=m_g_mix_norm, w_in=m_w_in, b_in=m_b_in, sinks=m_sinks, conv_w=m_conv_w,
             conv_b=m_conv_b, ln_g=m_ln_g, ln_b=m_ln_b, w_attn_proj=m_w_attn_proj,
             w_conv_proj=m_w_conv_proj, b_conv_proj=m_b_conv_proj, w_out=m_w_out,
             g_ffn_norm=m_g_ffn_norm, w_ffn_in=m_w_ffn_in, w_ffn_down=m_w_ffn_down, g_final=m_g_final)
    v = dict(g_mix_norm=v_g_mix_norm, w_in=v_w_in, b_in=v_b_in, sinks=v_sinks, conv_w=v_conv_w,
             conv_b=v_conv_b, ln_g=v_ln_g, ln_b=v_ln_b, w_attn_proj=v_w_attn_proj,
             w_conv_proj=v_w_conv_proj, b_conv_proj=v_b_conv_proj, w_out=v_w_out,
             g_ffn_norm=v_g_ffn_norm, w_ffn_in=v_w_ffn_in, w_ffn_down=v_w_ffn_down, g_final=v_g_final)

    c_idx = lax.axis_index("c").astype(jnp.int32).reshape(1)
    s_idx = (2 * lax.axis_index("x") + lax.axis_index("y")).astype(jnp.int32).reshape(1)

    out_g, out_d, out_m, out_v = {}, {}, {}, {}

    def gather_start(tag, shards):
        lands = [lax.empty((N_CHIPS,) + s.shape, s.dtype) for s in shards]
        return _chip_start("gather_start_" + tag, _gather_ends, GATHER_PEERS, shards, lands,
                           SHAKES.get("gather_" + tag))

    def gather_relay(tag, state, after, first=0, count=None):
        send_sems, recv_sems, shards, lands, _ = state
        last = len(shards) if count is None else first + count
        return _gather_relay("gather_relay_" + tag, send_sems, recv_sems, shards[first:last],
                             lands[first:last], after, first, SHAKES["relay_" + tag])

    def gather_finish(tag, relay, after):
        return _forward_wait("forward_wait_" + tag, relay[0], relay[1], relay[2], after)

    names_b = ["w_attn_proj", "w_conv_proj", "w_out", "w_ffn_in", "w_ffn_down"]
    big = {name: (w[name][0], m[name][0], v[name][0]) for name in names_b}
    big["w_in"] = (w_in[0].T, m_w_in[0].T, v_w_in[0].T)
    state_a = gather_start("a", [big["w_in"][0].astype(BF16), jnp.pad(conv_w[0], ((0, CW_ROWS - KW), (0, 0)))])
    state_b = gather_start("b", [(big[name][0] + state_a[4][0, 0]).astype(BF16) for name in names_b])
    got_a = gather_finish("a", gather_relay("a", state_a, [state_b[4]]), [])
    w_in_t_full = got_a[0].reshape(INW, D)
    conv_w_full = got_a[1].transpose(1, 0, 2).reshape(CW_ROWS, C)[:KW]

    xs, target = x[0], loss_target[0]
    g_final2 = g_final.reshape(1, D)
    h, qkv, glu, gl = _in_proj(xs, g_mix_norm, w_in_t_full, b_in)
    o, lse, probs = _attn_fwd(qkv, sinks)
    relay_1 = gather_relay("b1", state_b, [o], 0, 3)
    u, cact = _conv_fwd(glu, conv_w_full, conv_b, ln_g, ln_b, relay_1[3])
    w_ap4, w_cp4, w_out4 = gather_finish("b1", relay_1, [cact])
    w_out_full = w_out4.reshape(D, D)
    mixed = _mix_out(xs, o, cact, gl, w_ap4, w_cp4, b_conv_proj, w_out_full, relay_1[3])
    relay_2 = gather_relay("b2", state_b, [mixed[3]], 3, 1)
    ya, yc, mg, x1 = _mix_out(xs, o, cact, gl, w_ap4, w_cp4, b_conv_proj, w_out_full, relay_2[3], mixed)
    w_fi4, = gather_finish("b2", relay_2, [x1])
    h2, gu, act = _ffn_in_first(x1, g_ffn_norm, w_fi4, relay_2[3])
    relay_3 = gather_relay("b3", state_b, [h2], 4, 1)
    gu, act = _ffn_in_second(h2, w_fi4, gu, act, relay_3[3])
    w_dn4, = gather_finish("b3", relay_3, [act])
    w_dn_full = w_dn4.reshape(DFF, D)
    dx2, dx2b, dg_final, loss_part = _ffn_out_loss(x1, act, w_dn_full, g_final2, target)

    def exchange_start(tag, grads):
        lands = [lax.empty((N_CHIPS, g.shape[1] // 2, g.shape[2]), g.dtype) for g in grads]
        return _chip_start("pair_start_" + tag, _exchange_ends, PAIR_PEERS, grads, lands, SHAKES["pair_" + tag])

    def reduce_start(tag, names, exchange, after):
        send_sems, recv_sems, grads, lands, _ = exchange
        grads, from_sibling = _chip_wait("pair_wait_" + tag, _exchange_ends, send_sems, recv_sems, grads, lands, after)
        pair = [_pair_sum(c_idx, g, r, "pair_sum_" + name) for name, g, r in zip(names, grads, from_sibling)]
        lands = [lax.empty((3,) + p.shape[1:], p.dtype) for p in pair]
        return _chip_start("chip_start_" + tag, _reduce_ends, REDUCE_PEERS, pair, lands, SHAKES["chip_" + tag])

    def reduce_sum(tag, names, state, after):
        send_sems, recv_sems, pair, lands, _ = state
        pair, lands = _chip_wait("chip_wait_" + tag, _reduce_ends, send_sems, recv_sems, pair, lands, after)
        mine = [_chip_sum(s_idx, p, r, "chip_sum_" + name) for name, p, r in zip(names, pair, lands)]
        others = [lax.empty(a.shape, a.dtype) for a in mine]
        return _chip_start("share_start_" + tag, _share_ends, PAIR_PEERS, mine, others, SHAKES["share_" + tag])

    def reduce_finish(tag, names, share, after):
        send_sems, recv_sems, mine, others, _ = share
        mine, others = _chip_wait("share_wait_" + tag, _share_ends, send_sems, recv_sems, mine, others, after)
        for name, g_mine, g_other in zip(names, mine, others):
            wv, mv, vv = big[name]
            res = _adamw(c_idx, wv, g_mine, g_other, mv, vv, "adamw_" + name)
            if name == "w_in":
                res = [a.T for a in res]
            out_g[name], out_d[name], out_m[name], out_v[name] = [a[None] for a in res]

    dgu, dx1, dx1b, dg_ffn = _ffn_bwd(dx2, dx2b, gu, x1, g_ffn_norm, w_dn_full, w_fi4)
    names_1 = ["w_ffn_in", "w_ffn_down", "w_out", "w_attn_proj", "w_conv_proj"]
    grads_1 = [_grad_w(h2, dgu, "grad_w_ffn_in", 512, FSH, True),
               _grad_w(act, dx2b, "grad_w_ffn_down", 256, D, False)]
    dya, dyc, dgl, do, dc, db_cp = _mix_bwd(dx1b, gl, ya, yc, w_out_full, w_ap4, w_cp4, relay_3[3])
    grads_1 += [_grad_w(mg, dx1b, "grad_w_out", 512, D, False),
                _grad_w(o, dya, "grad_w_attn_proj", 512, D, True),
                _grad_w(cact, dyc, "grad_w_conv_proj", 512, D, True)]
    exchange_1 = exchange_start("1", grads_1)
    dq, dkv, dsinks = _attn_bwd(qkv, do, lse, probs, sinks, exchange_1[4])
    state_1 = reduce_start("1", names_1, exchange_1, [dq])
    dglu, dconv_w, dconv_b, dln_g, dln_b = _conv_bwd(glu, u, dc, conv_w_full, ln_g, ln_b, state_1[4])
    names_2 = ["w_in"]
    gw_in_t = _grad_w_in_t(h, dq, dkv, dglu, dgl)
    exchange_2 = exchange_start("2", [gw_in_t.reshape(N_CHIPS, INW // N_CHIPS, D)])
    grad_x, dg_mix, db_in = _in_proj_bwd(dq, dkv, dglu, dgl, xs, dx1, g_mix_norm, w_in_t_full, exchange_2[4])

    gs = {"g_mix_norm": dg_mix, "b_in": db_in, "sinks": dsinks[:, 0].reshape(1, NQ),
          "conv_w": dconv_w, "conv_b": dconv_b, "ln_g": dln_g, "ln_b": dln_b,
          "b_conv_proj": db_cp, "g_ffn_norm": dg_ffn, "g_final": dg_final}
    blocks = list(_pack_small(gs, loss_part[0, 0]))
    tables = [lax.empty((8 * b.shape[0], b.shape[1]), b.dtype) for b in blocks]
    small = _chip_start("small_start", _small_ends, SMALL_PEERS, blocks, tables)

    state_2 = reduce_start("2", names_2, exchange_2, [grad_x, small[4]])
    share_1 = reduce_sum("1", names_1, state_1, [state_2[4]])
    blocks, tables = _chip_wait("small_wait", _small_ends, small[0], small[1], small[2], small[3],
                                [share_1[4]])
    me = (4 * lax.axis_index("x") + 2 * lax.axis_index("y") + lax.axis_index("c")).astype(jnp.int32)

    def view(a, name):
        if name == "conv_w":
            return a[0]
        if name == "g_final":
            return a.reshape(1, D)
        return a

    wmv = [(view(w[name], name), view(m[name], name), view(v[name], name)) for name in SMALL_NAMES]
    small_out, loss_row = _small_update(jnp.concatenate([s_idx, me.reshape(1)]), blocks[0], blocks[1],
                                        tables[0], tables[1], wmv)
    for name, (g, d, mm, vv) in zip(SMALL_NAMES, small_out):
        shape = w[name].shape
        out_g[name], out_d[name], out_m[name], out_v[name] = (
            g.reshape(shape), d.reshape(shape), mm.reshape(shape), vv.reshape(shape))

    reduce_finish("1", names_1, share_1, [loss_row])
    share_2 = reduce_sum("2", names_2, state_2, [out_d["w_conv_proj"]])
    reduce_finish("2", names_2, share_2, [])

    loss = loss_row[0, 0]
    return (loss, grad_x[None], *[out_g[k] for k in WEIGHT_ORDER], *[out_d[k] for k in WEIGHT_ORDER],
            *[out_m[k] for k in WEIGHT_ORDER], *[out_v[k] for k in WEIGHT_ORDER])
```

```python
import functools

import jax
import jax.numpy as jnp
from jax import lax
from jax.experimental import pallas as pl
from jax.experimental.pallas import tpu as pltpu

F32 = jnp.float32
BF16 = jnp.bfloat16

T = 2048
D = 1024
HD = 64
NQ = 8
NKV = 2
GROUP = NQ // NKV
BLK = 128
AW = NQ * HD
KVW = NKV * HD
C = 512
KW = 31
QKVW = AW + 2 * KVW
GLU_OFF = QKVW
GATE_OFF = GLU_OFF + 2 * C
INW = GATE_OFF + 2 * D
DFF = 2816
EPS = 1e-5
NEG = -1e30
SCALE = HD ** -0.5
HALO = 32
N_CHIPS = 4
FSH = 2 * DFF // N_CHIPS

ADAM_LR = 0.001
ADAM_B1 = 0.9
ADAM_B2 = 0.999
ADAM_EPS = 1e-08
ADAM_WD = 0.01
ADAM_STEP = 10

VMEM_LIMIT = 56 * 1024 * 1024
ROW_TM = 512
MESH = pl.DeviceIdType.MESH


def _params(*sem):
    return pltpu.CompilerParams(dimension_semantics=sem, vmem_limit_bytes=VMEM_LIMIT)


def _dot(a, b):
    return jnp.dot(a, b, preferred_element_type=F32)


def _dot_nt(a, b):
    return lax.dot_general(a, b, (((1,), (1,)), ((), ())), preferred_element_type=F32)


def _dot_tn(a, b):
    return lax.dot_general(a, b, (((0,), (0,)), ((), ())), preferred_element_type=F32)


def _sigmoid(v):
    return 1.0 / (1.0 + jnp.exp(-v))


def _rows(tm, n):
    return pl.BlockSpec((tm, n), lambda i: (i, 0))


def _whole(shape):
    return pl.BlockSpec(shape, lambda i: tuple(0 for _ in shape))


def _in_proj(x, g_mix, w_in_t, b_in):
    tm = ROW_TM

    def body(x_ref, g_ref, w_ref, b_ref, h_ref, qkv_ref, glu_ref, gl_ref):
        xv = x_ref[...]
        r = lax.rsqrt(jnp.mean(xv * xv, axis=-1, keepdims=True) + EPS)
        h = (xv * r * g_ref[...]).astype(BF16)
        h_ref[...] = h
        qkv_ref[...] = (_dot_nt(h, w_ref[0:GLU_OFF, :]) + b_ref[:, 0:GLU_OFF]).astype(BF16)
        glu_ref[...] = (_dot_nt(h, w_ref[GLU_OFF:GATE_OFF, :]) + b_ref[:, GLU_OFF:GATE_OFF]).astype(BF16)
        gl_ref[...] = (_dot_nt(h, w_ref[GATE_OFF:INW, :]) + b_ref[:, GATE_OFF:INW]).astype(BF16)

    return pl.pallas_call(
        body, name="in_proj", grid=(T // tm,),
        in_specs=[_rows(tm, D), _whole((1, D)), _whole((INW, D)), _whole((1, INW))],
        out_specs=[_rows(tm, D), _rows(tm, QKVW), _rows(tm, 2 * C), _rows(tm, 2 * D)],
        out_shape=[jax.ShapeDtypeStruct((T, D), BF16), jax.ShapeDtypeStruct((T, QKVW), BF16),
                   jax.ShapeDtypeStruct((T, 2 * C), BF16), jax.ShapeDtypeStruct((T, 2 * D), BF16)],
        compiler_params=_params("parallel"),
    )(x, g_mix, w_in_t, b_in)


GROWS = GROUP * BLK
BAND = 2 * BLK
ATT_SUB = 2


def _band(i):
    rb = pl.multiple_of(jnp.maximum(i - 1, 0) * BLK, BLK)
    row = lax.broadcasted_iota(jnp.int32, (GROWS, BAND), 0)
    kpos = rb + lax.broadcasted_iota(jnp.int32, (GROWS, BAND), 1)
    qpos = i * BLK + jnp.bitwise_and(row, BLK - 1)
    return rb, jnp.logical_and(kpos <= qpos, kpos > qpos - BLK)


def _sink_column(sink_ref, g):
    head = lax.shift_right_logical(lax.broadcasted_iota(jnp.int32, (GROWS, 1), 0), 7)
    col = jnp.full((GROWS, 1), sink_ref[0, g * GROUP], F32)
    for hh in range(1, GROUP):
        col = jnp.where(head == hh, sink_ref[0, g * GROUP + hh], col)
    return col


def _attn_fwd(qkv, sinks):
    def body(sink_ref, qkv_ref, o_ref, lse_ref, p_ref, s_ref):
        slots = [(sub, g) for sub in range(ATT_SUB) for g in range(NKV)]
        bands = [_band(pl.program_id(0) * ATT_SUB + sub) for sub in range(ATT_SUB)]
        for n, (sub, g) in enumerate(slots):
            rb = bands[sub][0]
            r0 = pl.multiple_of((pl.program_id(0) * ATT_SUB + sub) * BLK, BLK)
            kband = qkv_ref[pl.ds(rb, BAND), AW + g * HD:AW + (g + 1) * HD]
            for hh in range(GROUP):
                h = g * GROUP + hh
                s_ref[n, hh * BLK:(hh + 1) * BLK, :] = _dot_nt(
                    qkv_ref[pl.ds(r0, BLK), h * HD:(h + 1) * HD], kband)
        lses = []
        for n, (sub, g) in enumerate(slots):
            s = jnp.where(bands[sub][1], s_ref[n] * SCALE, NEG)
            sink = _sink_column(sink_ref, g)
            m = jnp.maximum(jnp.max(s, axis=-1, keepdims=True), sink)
            p = jnp.exp(s - m)
            den = jnp.sum(p, axis=-1, keepdims=True) + jnp.exp(sink - m)
            p_ref[n] = (p * (1.0 / den)).astype(BF16)
            lses.append(m + jnp.log(den))
        for n, (sub, g) in enumerate(slots):
            rb = bands[sub][0]
            rows = slice(sub * BLK, (sub + 1) * BLK)
            vband = qkv_ref[pl.ds(rb, BAND), AW + KVW + g * HD:AW + KVW + (g + 1) * HD]
            for hh in range(GROUP):
                h = g * GROUP + hh
                o_ref[rows, h * HD:(h + 1) * HD] = _dot(p_ref[n, hh * BLK:(hh + 1) * BLK, :], vband).astype(BF16)
                lse_ref[rows, h:h + 1] = lses[n][hh * BLK:(hh + 1) * BLK]

    nslot = ATT_SUB * NKV
    return pl.pallas_call(
        body, name="attn_fwd", grid=(T // (ATT_SUB * BLK),),
        in_specs=[pl.BlockSpec(memory_space=pltpu.SMEM), _whole((T, QKVW))],
        out_specs=[_rows(ATT_SUB * BLK, AW), _rows(ATT_SUB * BLK, NQ),
                   pl.BlockSpec((nslot, GROWS, BAND), lambda i: (i, 0, 0))],
        out_shape=[jax.ShapeDtypeStruct((T, AW), BF16), jax.ShapeDtypeStruct((T, NQ), F32),
                   jax.ShapeDtypeStruct((T // BLK * NKV, GROWS, BAND), BF16)],
        scratch_shapes=[pltpu.VMEM((nslot, GROWS, BAND), F32)],
        compiler_params=_params("parallel"),
    )(sinks, qkv)


CONV_TM = 256
CONV_SUB = 32


def _glu(ab):
    a = ab[:, 0:C].astype(F32)
    b = ab[:, C:2 * C].astype(F32)
    return a * _sigmoid(b)


SUBLANES = 8


def _shifted_copies(ref):
    rows = ref.shape[1] - SUBLANES
    for r in range(1, SUBLANES):
        ref[r, 0:rows, :] = ref[0, r:r + rows, :]


def _shifted_rows(ref, start, size):
    r = start % SUBLANES
    return ref[r, start - r:start - r + size, :]


def _conv_fwd(glu, conv_w, conv_b, ln_g, ln_b, dep):
    tm = CONV_TM

    def body(cur_ref, prev_ref, w_ref, cb_ref, g_ref, b_ref, dep_ref, u_ref, c_ref, zs_ref):
        i = pl.program_id(0)
        zprev = _glu(prev_ref[tm - HALO:tm, :])
        zs_ref[0, 0:HALO, :] = jnp.where(i > 0, zprev, 0.0)
        zs_ref[0, HALO:HALO + tm, :] = _glu(cur_ref[...])
        _shifted_copies(zs_ref)
        for s in range(tm // CONV_SUB):
            base = HALO + s * CONV_SUB - (KW - 1)
            acc = jnp.broadcast_to(cb_ref[...], (CONV_SUB, C))
            for j in range(KW):
                acc = acc + w_ref[j:j + 1, :] * _shifted_rows(zs_ref, base + j, CONV_SUB)
            rows = slice(s * CONV_SUB, (s + 1) * CONV_SUB)
            u_ref[rows, :] = acc
            mu = jnp.mean(acc, axis=-1, keepdims=True)
            xc = acc - mu
            var = jnp.mean(xc * xc, axis=-1, keepdims=True)
            y = xc * lax.rsqrt(var + EPS) * g_ref[...] + b_ref[...]
            c_ref[rows, :] = (y * _sigmoid(y)).astype(BF16)

    return pl.pallas_call(
        body, name="conv_fwd", grid=(T // tm,),
        in_specs=[_rows(tm, 2 * C),
                  pl.BlockSpec((tm, 2 * C), lambda i: (jnp.maximum(i - 1, 0), 0)),
                  _whole((KW, C)), _whole((1, C)), _whole((1, C)), _whole((1, C)), _whole((8, 128))],
        out_specs=[_rows(tm, C), _rows(tm, C)],
        out_shape=[jax.ShapeDtypeStruct((T, C), F32), jax.ShapeDtypeStruct((T, C), BF16)],
        scratch_shapes=[pltpu.VMEM((SUBLANES, HALO + tm, C), F32)],
        compiler_params=_params("parallel"),
    )(glu, glu, conv_w, conv_b, ln_g, ln_b, dep)


def _branch_outputs(o, cact, wap_ref, wcp_ref, bcp_ref):
    ya = jnp.concatenate([_dot(o, wap_ref[s]) for s in range(N_CHIPS)], axis=1)
    yc = jnp.concatenate([_dot(cact, wcp_ref[s]) for s in range(N_CHIPS)], axis=1) + bcp_ref[...]
    return ya, yc


def _mix_out(x, o, cact, gl, w_ap, w_cp, b_cp, w_out, dep, begun=None):
    tm = ROW_TM
    steps = T // tm // 2
    first = 0 if begun is None else steps
    rows = lambda n: pl.BlockSpec((tm, n), lambda i: (i + first, 0))
    extra = [] if begun is None else list(begun)

    def body(x_ref, o_ref, c_ref, gl_ref, wap_ref, wcp_ref, bcp_ref, wo_ref, dep_ref, *rest):
        ya_ref, yc_ref, mg_ref, x1_ref = rest[len(extra):]
        ya, yc = _branch_outputs(o_ref[...], c_ref[...], wap_ref, wcp_ref, bcp_ref)
        g0 = _sigmoid(gl_ref[:, 0:D].astype(F32))
        g1 = _sigmoid(gl_ref[:, D:2 * D].astype(F32))
        mg = (g0 * ya + g1 * yc).astype(BF16)
        ya_ref[...] = ya.astype(BF16)
        yc_ref[...] = yc.astype(BF16)
        mg_ref[...] = mg
        x1_ref[...] = x_ref[...] + _dot(mg, wo_ref[...])

    return pl.pallas_call(
        body, name="mix_out_first" if begun is None else "mix_out_second", grid=(steps,),
        in_specs=[rows(D), rows(AW), rows(C), rows(2 * D),
                  _whole((N_CHIPS, AW, D // N_CHIPS)), _whole((N_CHIPS, C, D // N_CHIPS)), _whole((1, D)),
                  _whole((D, D)), _whole((8, 128))] + [ANY_SPEC] * len(extra),
        out_specs=[rows(D), rows(D), rows(D), rows(D)],
        out_shape=[jax.ShapeDtypeStruct((T, D), BF16), jax.ShapeDtypeStruct((T, D), BF16),
                   jax.ShapeDtypeStruct((T, D), BF16), jax.ShapeDtypeStruct((T, D), F32)],
        input_output_aliases={9 + k: k for k in range(len(extra))},
        compiler_params=_params("parallel"),
    )(x, o, cact, gl, w_ap, w_cp, b_cp, w_out, dep, *extra)


def _swiglu_half(h, wg_ref, wu_ref, gu_ref, act_ref):
    gate = _dot(h, wg_ref[0])
    up = _dot(h, wu_ref[0])
    gu_ref[:, 0:FSH] = gate.astype(BF16)
    gu_ref[:, FSH:2 * FSH] = up.astype(BF16)
    act_ref[...] = (gate * _sigmoid(gate) * up).astype(BF16)


def _shard_spec(k):
    return pl.BlockSpec((1, D, FSH), lambda i: (k, 0, 0), pipeline_mode=pl.Buffered(1))


def _ffn_in_first(x1, g_ffn, w_fi, dep):
    tm = ROW_TM

    def body(x_ref, g_ref, wg_ref, wu_ref, dep_ref, h_ref, gu_ref, act_ref):
        xv = x_ref[...]
        r = lax.rsqrt(jnp.mean(xv * xv, axis=-1, keepdims=True) + EPS)
        h = (xv * r * g_ref[...]).astype(BF16)
        h_ref[...] = h
        _swiglu_half(h, wg_ref, wu_ref, gu_ref, act_ref)

    return pl.pallas_call(
        body, name="ffn_in_first", grid=(T // tm,),
        in_specs=[_rows(tm, D), _whole((1, D)), _shard_spec(0), _shard_spec(2), _whole((8, 128))],
        out_specs=[_rows(tm, D), pl.BlockSpec((tm, 2 * FSH), lambda i: (i, 0)),
                   pl.BlockSpec((tm, FSH), lambda i: (i, 0))],
        out_shape=[jax.ShapeDtypeStruct((T, D), BF16), jax.ShapeDtypeStruct((T, 2 * DFF), BF16),
                   jax.ShapeDtypeStruct((T, DFF), BF16)],
        compiler_params=_params("parallel"),
    )(x1, g_ffn, w_fi, w_fi, dep)


def _ffn_in_second(h2, w_fi, gu, act, dep):
    tm = ROW_TM

    def body(h_ref, wg_ref, wu_ref, gu_in, act_in, dep_ref, gu_ref, act_ref):
        _swiglu_half(h_ref[...], wg_ref, wu_ref, gu_ref, act_ref)

    return pl.pallas_call(
        body, name="ffn_in_second", grid=(T // tm,),
        in_specs=[_rows(tm, D), _shard_spec(1), _shard_spec(3), ANY_SPEC, ANY_SPEC, _whole((8, 128))],
        out_specs=[pl.BlockSpec((tm, 2 * FSH), lambda i: (i, 1)), pl.BlockSpec((tm, FSH), lambda i: (i, 1))],
        out_shape=[jax.ShapeDtypeStruct((T, 2 * DFF), BF16), jax.ShapeDtypeStruct((T, DFF), BF16)],
        input_output_aliases={3: 0, 4: 1},
        compiler_params=_params("parallel"),
    )(h2, w_fi, w_fi, gu, act, dep)


def _ffn_out_loss(x1, act, w_dn, g_final, target):
    tm = ROW_TM // 2

    def body(x_ref, a_ref, w_ref, g_ref, t_ref, dx_ref, dxb_ref, dg_ref, loss_ref):
        i = pl.program_id(0)
        x2 = x_ref[...] + _dot(a_ref[...], w_ref[...])
        r = lax.rsqrt(jnp.mean(x2 * x2, axis=-1, keepdims=True) + EPS)
        xh = x2 * r
        g = g_ref[...]
        err = xh * g - t_ref[...]
        dy = err * (1.0 / D)
        dyg = dy * g
        dx = r * (dyg - xh * jnp.mean(dyg * xh, axis=-1, keepdims=True))
        dx_ref[...] = dx
        dxb_ref[...] = dx.astype(BF16)
        part = 0.5 * jnp.sum(jnp.mean(err * err, axis=-1, keepdims=True), axis=0, keepdims=True)

        @pl.when(i == 0)
        def _():
            dg_ref[...] = jnp.zeros_like(dg_ref)
            loss_ref[...] = jnp.zeros_like(loss_ref)

        dg_ref[...] += jnp.sum(dy * xh, axis=0, keepdims=True)
        loss_ref[...] += jnp.broadcast_to(part, loss_ref.shape)

    return pl.pallas_call(
        body, name="ffn_out_loss", grid=(T // tm,),
        in_specs=[_rows(tm, D), _rows(tm, DFF), _whole((DFF, D)), _whole((1, D)), _rows(tm, D)],
        out_specs=[_rows(tm, D), _rows(tm, D), _whole((1, D)), _whole((1, 128))],
        out_shape=[jax.ShapeDtypeStruct((T, D), F32), jax.ShapeDtypeStruct((T, D), BF16),
                   jax.ShapeDtypeStruct((1, D), F32), jax.ShapeDtypeStruct((1, 128), F32)],
        compiler_params=_params("arbitrary"),
    )(x1, act, w_dn, g_final, target)


def _const(shape):
    return pl.BlockSpec(shape, lambda i: tuple(0 for _ in shape), pipeline_mode=pl.Buffered(1))


def _ffn_bwd(dx2, dx2b, gu, x1, g_ffn, w_dn_t, w_fi_t):
    tm = ROW_TM // 2

    def body(dx_ref, dxb_ref, gu_ref, x_ref, g_ref, wdn_ref, wfi_ref,
             dgu_ref, dx1_ref, dx1b_ref, dg_ref):
        i = pl.program_id(0)
        dxb = dxb_ref[...]
        dh = jnp.zeros((tm, D), F32)
        dacts = [_dot_nt(dxb, wdn_ref[k * FSH:(k + 1) * FSH, :]) for k in range(N_CHIPS // 2)]
        for k in range(N_CHIPS // 2):
            c0 = k * FSH
            dact = dacts[k]
            gate = gu_ref[:, 2 * c0:2 * c0 + FSH].astype(F32)
            up = gu_ref[:, 2 * c0 + FSH:2 * c0 + 2 * FSH].astype(F32)
            s = _sigmoid(gate)
            dup = (dact * gate * s).astype(BF16)
            dgate = (dact * up * s * (1.0 + gate * (1.0 - s))).astype(BF16)
            dgu_ref[:, c0:c0 + FSH] = dgate
            dgu_ref[:, DFF + c0:DFF + c0 + FSH] = dup
            dh = dh + _dot_nt(dgate, wfi_ref[k]) + _dot_nt(dup, wfi_ref[k + N_CHIPS // 2])
        xv = x_ref[...]
        r = lax.rsqrt(jnp.mean(xv * xv, axis=-1, keepdims=True) + EPS)
        xh = xv * r
        dhg = dh * g_ref[...]
        dx1 = dx_ref[...] + r * (dhg - xh * jnp.mean(dhg * xh, axis=-1, keepdims=True))
        dx1_ref[...] = dx1
        dx1b_ref[...] = dx1.astype(BF16)

        @pl.when(i == 0)
        def _():
            dg_ref[...] = jnp.zeros_like(dg_ref)

        dg_ref[...] += jnp.sum(dh * xh, axis=0, keepdims=True)

    return pl.pallas_call(
        body, name="ffn_bwd", grid=(T // tm,),
        in_specs=[_rows(tm, D), _rows(tm, D), _rows(tm, 2 * DFF), _rows(tm, D), _whole((1, D)),
                  _const((DFF, D)), _const((N_CHIPS, D, FSH))],
        out_specs=[_rows(tm, 2 * DFF), _rows(tm, D), _rows(tm, D), _whole((1, D))],
        out_shape=[jax.ShapeDtypeStruct((T, 2 * DFF), BF16), jax.ShapeDtypeStruct((T, D), F32),
                   jax.ShapeDtypeStruct((T, D), BF16), jax.ShapeDtypeStruct((1, D), F32)],
        compiler_params=_params("arbitrary"),
    )(dx2, dx2b, gu, x1, g_ffn, w_dn_t, w_fi_t)


def _mix_bwd(dx1b, gl, ya, yc, w_out, w_ap, w_cp, dep):
    tm = ROW_TM

    def body(dx_ref, gl_ref, ya_ref, yc_ref, wo_ref, wap_ref, wcp_ref, dep_ref,
             dya_ref, dyc_ref, dgl_ref, do_ref, dc_ref, db_ref):
        i = pl.program_id(0)
        dm = _dot_nt(dx_ref[...], wo_ref[...])
        ya, yc = ya_ref[...].astype(F32), yc_ref[...].astype(F32)
        g0 = _sigmoid(gl_ref[:, 0:D].astype(F32))
        g1 = _sigmoid(gl_ref[:, D:2 * D].astype(F32))
        dya = dm * g0
        dyc = dm * g1
        dgl_ref[:, 0:D] = (dya * ya * (1.0 - g0)).astype(BF16)
        dgl_ref[:, D:2 * D] = (dyc * yc * (1.0 - g1)).astype(BF16)
        dyab = dya.astype(BF16)
        dycb = dyc.astype(BF16)
        dya_ref[...] = dyab
        dyc_ref[...] = dycb
        sw = D // N_CHIPS
        do = jnp.zeros((tm, AW), F32)
        dcv = jnp.zeros((tm, C), F32)
        for s in range(N_CHIPS):
            do = do + _dot_nt(dyab[:, s * sw:(s + 1) * sw], wap_ref[s])
            dcv = dcv + _dot_nt(dycb[:, s * sw:(s + 1) * sw], wcp_ref[s])
        do_ref[...] = do.astype(BF16)
        dc_ref[...] = dcv.astype(BF16)

        @pl.when(i == 0)
        def _():
            db_ref[...] = jnp.zeros_like(db_ref)

        db_ref[...] += jnp.sum(dyc, axis=0, keepdims=True)

    return pl.pallas_call(
        body, name="mix_bwd", grid=(T // tm,),
        in_specs=[_rows(tm, D), _rows(tm, 2 * D), _rows(tm, D), _rows(tm, D),
                  _whole((D, D)), _whole((N_CHIPS, AW, D // N_CHIPS)), _whole((N_CHIPS, C, D // N_CHIPS)),
                  _whole((8, 128))],
        out_specs=[_rows(tm, D), _rows(tm, D), _rows(tm, 2 * D), _rows(tm, AW), _rows(tm, C),
                   _whole((1, D))],
        out_shape=[jax.ShapeDtypeStruct((T, D), BF16), jax.ShapeDtypeStruct((T, D), BF16),
                   jax.ShapeDtypeStruct((T, 2 * D), BF16), jax.ShapeDtypeStruct((T, AW), BF16),
                   jax.ShapeDtypeStruct((T, C), BF16), jax.ShapeDtypeStruct((1, D), F32)],
        compiler_params=_params("arbitrary"),
    )(dx1b, gl, ya, yc, w_out, w_ap, w_cp, dep)


def _conv_bwd(glu, u, dc, conv_w, ln_g, ln_b, dep):
    tm = CONV_TM
    nblk = T // tm

    def du_of(uv, dcv, g_ref, b_ref):
        mu = jnp.mean(uv, axis=-1, keepdims=True)
        xc = uv - mu
        var = jnp.mean(xc * xc, axis=-1, keepdims=True)
        rstd = lax.rsqrt(var + EPS)
        xh = xc * rstd
        y = xh * g_ref[...] + b_ref[...]
        sg = _sigmoid(y)
        dy = dcv * (sg * (1.0 + y * (1.0 - sg)))
        dxh = dy * g_ref[...]
        du = rstd * (dxh - jnp.mean(dxh, axis=-1, keepdims=True)
                     - xh * jnp.mean(dxh * xh, axis=-1, keepdims=True))
        return du, dy, xh

    def body(cur_ref, prev_ref, u_ref, un_ref, dc_ref, dcn_ref, w_ref, g_ref, b_ref, dep_ref,
             dglu_ref, dw_ref, dcb_ref, dg_ref, db_ref, zs_ref, dus_ref):
        i = pl.program_id(0)

        @pl.when(i == 0)
        def _():
            dw_ref[...] = jnp.zeros_like(dw_ref)
            dcb_ref[...] = jnp.zeros_like(dcb_ref)
            dg_ref[...] = jnp.zeros_like(dg_ref)
            db_ref[...] = jnp.zeros_like(db_ref)

        zprev = _glu(prev_ref[tm - HALO:tm, :])
        zs_ref[0, 0:HALO, :] = jnp.where(i > 0, zprev, 0.0)
        zs_ref[0, HALO:HALO + tm, :] = _glu(cur_ref[...])
        _shifted_copies(zs_ref)

        dun, _, _ = du_of(un_ref[0:HALO, :], dcn_ref[0:HALO, :].astype(F32), g_ref, b_ref)
        dus_ref[0, tm:tm + HALO, :] = jnp.where(i < nblk - 1, dun, 0.0)
        dg_acc = jnp.zeros((1, C), F32)
        db_acc = jnp.zeros((1, C), F32)
        dcb_acc = jnp.zeros((1, C), F32)
        for s in range(tm // CONV_SUB):
            rows = slice(s * CONV_SUB, (s + 1) * CONV_SUB)
            du, dy, xh = du_of(u_ref[rows, :], dc_ref[rows, :].astype(F32), g_ref, b_ref)
            dus_ref[0, rows, :] = du
            dg_acc = dg_acc + jnp.sum(dy * xh, axis=0, keepdims=True)
            db_acc = db_acc + jnp.sum(dy, axis=0, keepdims=True)
            dcb_acc = dcb_acc + jnp.sum(du, axis=0, keepdims=True)
        dg_ref[...] += dg_acc
        db_ref[...] += db_acc
        dcb_ref[...] += dcb_acc
        _shifted_copies(dus_ref)

        for j in range(KW):
            acc = jnp.zeros((CONV_SUB, C), F32)
            for s in range(tm // CONV_SUB):
                base = HALO + s * CONV_SUB - (KW - 1) + j
                acc = acc + dus_ref[0, s * CONV_SUB:(s + 1) * CONV_SUB, :] * _shifted_rows(zs_ref, base, CONV_SUB)
            dw_ref[j:j + 1, :] += jnp.sum(acc, axis=0, keepdims=True)

        for s in range(tm // CONV_SUB):
            rows = slice(s * CONV_SUB, (s + 1) * CONV_SUB)
            dz = jnp.zeros((CONV_SUB, C), F32)
            for j in range(KW):
                dz = dz + w_ref[j:j + 1, :] * _shifted_rows(dus_ref, s * CONV_SUB + (KW - 1) - j, CONV_SUB)
            a = cur_ref[rows, 0:C].astype(F32)
            sb = _sigmoid(cur_ref[rows, C:2 * C].astype(F32))
            dglu_ref[rows, 0:C] = (dz * sb).astype(BF16)
            dglu_ref[rows, C:2 * C] = (dz * a * sb * (1.0 - sb)).astype(BF16)

    nxt = lambda i: (jnp.minimum(i + 1, nblk - 1), 0)
    return pl.pallas_call(
        body, name="conv_bwd", grid=(nblk,),
        in_specs=[_rows(tm, 2 * C),
                  pl.BlockSpec((tm, 2 * C), lambda i: (jnp.maximum(i - 1, 0), 0)),
                  _rows(tm, C), pl.BlockSpec((tm, C), nxt),
                  _rows(tm, C), pl.BlockSpec((tm, C), nxt),
                  _whole((KW, C)), _whole((1, C)), _whole((1, C)), _whole((8, 128))],
        out_specs=[_rows(tm, 2 * C), _whole((KW, C)), _whole((1, C)), _whole((1, C)), _whole((1, C))],
        out_shape=[jax.ShapeDtypeStruct((T, 2 * C), BF16), jax.ShapeDtypeStruct((KW, C), F32),
                   jax.ShapeDtypeStruct((1, C), F32), jax.ShapeDtypeStruct((1, C), F32),
                   jax.ShapeDtypeStruct((1, C), F32)],
        scratch_shapes=[pltpu.VMEM((SUBLANES, HALO + tm, C), F32), pltpu.VMEM((SUBLANES, tm + HALO, C), F32)],
        compiler_params=_params("arbitrary"),
    )(glu, glu, u, u, dc, dc, conv_w, ln_g, ln_b, dep)


def _attn_bwd(qkv, do, lse, p, sinks, dep):
    nsub = 1

    def body(sink_ref, qkv_ref, do_ref, lse_ref, p_ref, dep_ref, dq_ref, dkv_ref, ds_ref,
             dp_ref, dsb_ref):
        i = pl.program_id(0)

        @pl.when(i == 0)
        def _():
            dkv_ref[...] = jnp.zeros_like(dkv_ref)
            ds_ref[...] = jnp.zeros_like(ds_ref)

        slots = [(sub, g) for sub in range(nsub) for g in range(NKV)]
        bands = [_band(i * nsub + sub) for sub in range(nsub)]
        r0s = [pl.multiple_of((i * nsub + sub) * BLK, BLK) for sub in range(nsub)]
        lses = []
        for n, (sub, g) in enumerate(slots):
            rb, blk = bands[sub][0], slice(sub * BLK, (sub + 1) * BLK)
            vband = qkv_ref[pl.ds(rb, BAND), AW + KVW + g * HD:AW + KVW + (g + 1) * HD]
            lse_parts = []
            for hh in range(GROUP):
                h = g * GROUP + hh
                dp_ref[n, hh * BLK:(hh + 1) * BLK, :] = _dot_nt(do_ref[blk, h * HD:(h + 1) * HD], vband)
                lse_parts.append(lse_ref[blk, h:h + 1])
            lses.append(jnp.concatenate(lse_parts, axis=0))
        dsinks = []
        for n, (sub, g) in enumerate(slots):
            pf, dpv = p_ref[n].astype(F32), dp_ref[n]
            dl = jnp.sum(pf * dpv, axis=-1, keepdims=True)
            dsb_ref[n] = (pf * (dpv - dl)).astype(BF16)
            dsinks.append(-(jnp.exp(_sink_column(sink_ref, g) - lses[n]) * dl))
        for n, (sub, g) in enumerate(slots):
            rb, blk = bands[sub][0], slice(sub * BLK, (sub + 1) * BLK)
            kband = qkv_ref[pl.ds(rb, BAND), AW + g * HD:AW + (g + 1) * HD]
            dk = jnp.zeros((BAND, HD), F32)
            dv = jnp.zeros((BAND, HD), F32)
            for hh in range(GROUP):
                h = g * GROUP + hh
                hcol = slice(h * HD, (h + 1) * HD)
                rows = slice(hh * BLK, (hh + 1) * BLK)
                dq_ref[blk, hcol] = (_dot(dsb_ref[n, rows, :], kband) * SCALE).astype(BF16)
                dk = dk + _dot_tn(dsb_ref[n, rows, :], qkv_ref[pl.ds(r0s[sub], BLK), hcol])
                dv = dv + _dot_tn(p_ref[n, rows, :], do_ref[blk, hcol])
                ds_ref[h:h + 1, :] += jnp.broadcast_to(
                    jnp.sum(dsinks[n][rows], axis=0, keepdims=True), (1, 128))
            dkv_ref[pl.ds(rb, BAND), g * HD:(g + 1) * HD] += dk * SCALE
            dkv_ref[pl.ds(rb, BAND), KVW + g * HD:KVW + (g + 1) * HD] += dv

    nslot, tq = nsub * NKV, nsub * BLK
    return pl.pallas_call(
        body, name="attn_bwd", grid=(T // tq,),
        in_specs=[pl.BlockSpec(memory_space=pltpu.SMEM), _whole((T, QKVW)),
                  _rows(tq, AW), _rows(tq, NQ),
                  pl.BlockSpec((nslot, GROWS, BAND), lambda i: (i, 0, 0)), _whole((8, 128))],
        out_specs=[_rows(tq, AW), _whole((T, 2 * KVW)), _whole((NQ, 128))],
        out_shape=[jax.ShapeDtypeStruct((T, AW), BF16), jax.ShapeDtypeStruct((T, 2 * KVW), F32),
                   jax.ShapeDtypeStruct((NQ, 128), F32)],
        scratch_shapes=[pltpu.VMEM((nslot, GROWS, BAND), F32), pltpu.VMEM((nslot, GROWS, BAND), BF16)],
        compiler_params=_params("arbitrary"),
    )(sinks, qkv, do, lse, p, dep)


PROJ_PARTS = [(0, AW), (AW, QKVW), (GLU_OFF, GATE_OFF), (GATE_OFF, INW)]


def _in_proj_bwd(dq, dkv, dglu, dgl, x, dx1, g_mix, w_in_t, dep):
    tm = ROW_TM

    def body(dq_ref, dkv_ref, dglu_ref, dgl_ref, x_ref, dx1_ref, g_ref, w_ref, dep_ref,
             gx_ref, dg_ref, db_ref):
        i = pl.program_id(0)

        @pl.when(i == 0)
        def _():
            dg_ref[...] = jnp.zeros_like(dg_ref)
            db_ref[...] = jnp.zeros_like(db_ref)

        dh = jnp.zeros((tm, D), F32)
        for part_ref, (lo, hi) in zip((dq_ref, dkv_ref, dglu_ref, dgl_ref), PROJ_PARTS):
            part = part_ref[...]
            dh = dh + _dot(part.astype(BF16), w_ref[lo:hi, :])
            db_ref[:, lo:hi] += jnp.sum(part.astype(F32), axis=0, keepdims=True)
        xv = x_ref[...]
        r = lax.rsqrt(jnp.mean(xv * xv, axis=-1, keepdims=True) + EPS)
        xh = xv * r
        dhg = dh * g_ref[...]
        gx_ref[...] = dx1_ref[...] + r * (dhg - xh * jnp.mean(dhg * xh, axis=-1, keepdims=True))
        dg_ref[...] += jnp.sum(dh * xh, axis=0, keepdims=True)

    return pl.pallas_call(
        body, name="in_proj_bwd", grid=(T // tm,),
        in_specs=[_rows(tm, AW), _rows(tm, 2 * KVW), _rows(tm, 2 * C), _rows(tm, 2 * D),
                  _rows(tm, D), _rows(tm, D), _whole((1, D)), _const((INW, D)), _whole((8, 128))],
        out_specs=[_rows(tm, D), _whole((1, D)), _whole((1, INW))],
        out_shape=[jax.ShapeDtypeStruct((T, D), F32), jax.ShapeDtypeStruct((1, D), F32),
                   jax.ShapeDtypeStruct((1, INW), F32)],
        compiler_params=_params("arbitrary"),
    )(dq, dkv, dglu, dgl, x, dx1, g_mix, w_in_t, dep)


def _grad_w_in_t(h, dq, dkv, dglu, dgl):
    tn, chunk = 512, 256

    def body(h_ref, dq_ref, dkv_ref, dglu_ref, dgl_ref, o_ref):
        hv = h_ref[...]
        for part_ref, (lo, hi) in zip((dq_ref, dkv_ref, dglu_ref, dgl_ref), PROJ_PARTS):
            for c0 in range(0, hi - lo, chunk):
                o_ref[lo + c0:lo + c0 + chunk, :] = _dot_tn(
                    part_ref[:, c0:c0 + chunk].astype(BF16), hv).astype(BF16)

    return pl.pallas_call(
        body, name="grad_w_in", grid=(D // tn,),
        in_specs=[pl.BlockSpec((T, tn), lambda j: (0, j)), _const((T, AW)), _const((T, 2 * KVW)),
                  _const((T, 2 * C)), _const((T, 2 * D))],
        out_specs=pl.BlockSpec((INW, tn), lambda j: (0, j)),
        out_shape=jax.ShapeDtypeStruct((INW, D), BF16),
        compiler_params=_params("parallel"),
    )(h, dq, dkv, dglu, dgl)


def _grad_w(a, b, name, tk, tn, col_sharded):
    k, n = a.shape[1], b.shape[1]

    single = n == tn
    sw = n // N_CHIPS

    def body(a_ref, b_ref, o_ref, at_ref):
        if single:
            res = _dot_tn(a_ref[...], b_ref[...]).astype(BF16)
            if col_sharded:
                for s in range(N_CHIPS):
                    o_ref[s] = res[:, s * sw:(s + 1) * sw]
            else:
                o_ref[...] = res
            return

        @pl.when(pl.program_id(1) == 0)
        def _():
            at_ref[...] = a_ref[...].T

        o_ref[...] = _dot(at_ref[...], b_ref[...]).astype(BF16)

    if col_sharded and single:
        shape = (N_CHIPS, k, sw)
        out_spec = pl.BlockSpec((N_CHIPS, tk, sw), lambda i, j: (0, i, 0))
    elif col_sharded:
        per = sw // tn
        shape = (N_CHIPS, k, sw)
        out_spec = pl.BlockSpec((None, tk, tn), lambda i, j: (j // per, i, j % per))
    else:
        shape = (1, k, n)
        out_spec = pl.BlockSpec((None, tk, tn), lambda i, j: (0, i, j))
    out = pl.pallas_call(
        body, name=name, grid=(k // tk, n // tn),
        in_specs=[pl.BlockSpec((T, tk), lambda i, j: (0, i)), pl.BlockSpec((T, tn), lambda i, j: (0, j))],
        out_specs=out_spec,
        out_shape=jax.ShapeDtypeStruct(shape, BF16),
        scratch_shapes=[pltpu.VMEM((tk, T), BF16)],
        compiler_params=_params("parallel", "arbitrary"),
    )(a, b)
    return out if col_sharded else out.reshape(N_CHIPS, k // N_CHIPS, n)


HBM_SPEC = pl.BlockSpec(memory_space=pltpu.HBM)


def _place():
    x, y, c = lax.axis_index("x"), lax.axis_index("y"), lax.axis_index("c")
    chips = [(1 - x, y), (x, 1 - y), (1 - x, 1 - y)]
    return x, y, c, chips


SEM_SPEC = pl.BlockSpec(memory_space=pltpu.SEMAPHORE)
ANY_SPEC = pl.BlockSpec(memory_space=pl.ANY)
VMEM_SPEC = pl.BlockSpec(memory_space=pltpu.VMEM)
EFFECT = pltpu.SideEffectType.DATAFLOW_SIDE_EFFECTING


def _gather_ends(src, land, x, y, c, chips):
    kh = src.shape[0] // 2
    s_me = 2 * x + y
    ends = [(src.at[pl.ds(c * kh, kh)], land.at[s_me, pl.ds(c * kh, kh)], (*chip, c)) for chip in chips]
    return ends + [(src, land.at[s_me], (x, y, 1 - c))]


def _reduce_ends(src, land, x, y, c, chips):
    return [(src.at[2 * chip[0] + chip[1]], land.at[j], (*chip, c)) for j, chip in enumerate(chips)]


def _chip_copies(ends, srcs, lands, send_sems, recv_sems, first=0):
    x, y, c, chips = _place()
    copies = []
    for src, land in zip(srcs, lands):
        peers = ends(src, land, x, y, c, chips)
        for s, d, to in peers:
            k = first * len(peers) + len(copies)
            copies.append(pltpu.make_async_remote_copy(
                src_ref=s, dst_ref=d, send_sem=send_sems.at[k], recv_sem=recv_sems.at[k],
                device_id=to, device_id_type=MESH))
    return copies


GATHER_PEERS, REDUCE_PEERS = 4, 3


def _handshake(shake):
    x, y, c, chips = _place()
    peers = ([(x, y, 1 - c)] if shake in ("pair", "both") else []) + (
        [(*chip, c) for chip in chips] if shake in ("chips", "both") else [])
    barrier = pltpu.get_barrier_semaphore()
    for peer in peers:
        pl.semaphore_signal(barrier, inc=1, device_id=peer, device_id_type=MESH)
    pl.semaphore_wait(barrier, len(peers))


def _chip_start(name, ends, peers, srcs, lands, shake=None):
    n = len(srcs)

    def body(*refs):
        if shake is not None:
            _handshake(shake[0])
        copies = _chip_copies(ends, refs[:n], refs[n:2 * n], refs[2 * n], refs[2 * n + 1])
        for cp in copies:
            cp.start()
        token = refs[-1]
        token[...] = jnp.zeros_like(token)

    hbm = lambda a: pltpu.HBM(a.shape, a.dtype)
    res = pl.pallas_call(
        body, name=name,
        out_shape=(pltpu.SemaphoreType.DMA((peers * n,)), pltpu.SemaphoreType.DMA((peers * n,)),
                   *[hbm(a) for a in srcs], *[hbm(a) for a in lands],
                   jax.ShapeDtypeStruct((8, 128), F32)),
        in_specs=[HBM_SPEC] * (2 * n),
        out_specs=(SEM_SPEC, SEM_SPEC, *[HBM_SPEC] * (2 * n), VMEM_SPEC),
        input_output_aliases={i: 2 + i for i in range(2 * n)},
        compiler_params=pltpu.CompilerParams(has_side_effects=EFFECT,
                                             collective_id=None if shake is None else shake[1]),
    )(*[pltpu.with_memory_space_constraint(a, pltpu.HBM) for a in (*srcs, *lands)])
    return res[0], res[1], list(res[2:2 + n]), list(res[2 + n:2 + 2 * n]), res[-1]


def _chip_wait(name, ends, send_sems, recv_sems, srcs, lands, after, first=0):
    n, na = len(srcs), len(after)

    def body(*refs):
        copies = _chip_copies(ends, refs[:n], refs[n:2 * n], refs[2 * n], refs[2 * n + 1], first)
        for cp in copies:
            cp.wait_send()
            cp.wait_recv()

    hbm = lambda a: pltpu.HBM(a.shape, a.dtype)
    res = pl.pallas_call(
        body, name=name,
        out_shape=tuple(hbm(a) for a in (*srcs, *lands)),
        in_specs=[HBM_SPEC] * (2 * n) + [SEM_SPEC, SEM_SPEC] + [ANY_SPEC] * na,
        out_specs=tuple([HBM_SPEC] * (2 * n)),
        input_output_aliases={i: i for i in range(2 * n)},
        compiler_params=pltpu.CompilerParams(has_side_effects=EFFECT),
    )(*srcs, *lands, send_sems, recv_sems, *after)
    return list(res[:n]), list(res[n:])


def _forward_copies(lands, send_sems, recv_sems):
    x, y, c, chips = _place()
    copies = []
    for land in lands:
        kh = land.shape[1] // 2
        for chip in chips:
            blk = land.at[2 * chip[0] + chip[1], pl.ds(c * kh, kh)]
            k = len(copies)
            copies.append(pltpu.make_async_remote_copy(
                src_ref=blk, dst_ref=blk, send_sem=send_sems.at[k], recv_sem=recv_sems.at[k],
                device_id=(x, y, 1 - c), device_id_type=MESH))
    return copies


def _gather_relay(name, send_sems, recv_sems, srcs, lands, after, first, shake):
    n, na = len(srcs), len(after)

    def body(*refs):
        _handshake(shake[0])
        land_refs = refs[n:2 * n]
        for cp in _chip_copies(_gather_ends, refs[:n], land_refs, refs[2 * n], refs[2 * n + 1], first):
            cp.wait_send()
            cp.wait_recv()
        out = refs[2 * n + 2 + na:]
        for cp in _forward_copies(land_refs, out[0], out[1]):
            cp.start()
        out[-1][...] = jnp.zeros_like(out[-1])

    hbm = lambda a: pltpu.HBM(a.shape, a.dtype)
    res = pl.pallas_call(
        body, name=name,
        out_shape=(pltpu.SemaphoreType.DMA((3 * n,)), pltpu.SemaphoreType.DMA((3 * n,)),
                   *[hbm(a) for a in lands], jax.ShapeDtypeStruct((8, 128), F32)),
        in_specs=[HBM_SPEC] * (2 * n) + [SEM_SPEC, SEM_SPEC] + [ANY_SPEC] * na,
        out_specs=(SEM_SPEC, SEM_SPEC, *[HBM_SPEC] * n, VMEM_SPEC),
        input_output_aliases={n + i: 2 + i for i in range(n)},
        compiler_params=pltpu.CompilerParams(has_side_effects=EFFECT, collective_id=shake[1]),
    )(*srcs, *lands, send_sems, recv_sems, *after)
    return res[0], res[1], list(res[2:2 + n]), res[-1]


def _forward_wait(name, send_sems, recv_sems, lands, after):
    n, na = len(lands), len(after)

    def body(*refs):
        for cp in _forward_copies(refs[:n], refs[n], refs[n + 1]):
            cp.wait_send()
            cp.wait_recv()

    hbm = lambda a: pltpu.HBM(a.shape, a.dtype)
    res = pl.pallas_call(
        body, name=name,
        out_shape=tuple(hbm(a) for a in lands),
        in_specs=[HBM_SPEC] * n + [SEM_SPEC, SEM_SPEC] + [ANY_SPEC] * na,
        out_specs=tuple([HBM_SPEC] * n),
        input_output_aliases={i: i for i in range(n)},
        compiler_params=pltpu.CompilerParams(has_side_effects=EFFECT),
    )(*lands, send_sems, recv_sems, *after)
    return list(res)


def _exchange_ends(src, land, x, y, c, chips):
    kh = src.shape[1] // 2
    return [(src.at[:, pl.ds((1 - c) * kh, kh)], land, (x, y, 1 - c))]


def _share_ends(src, land, x, y, c, chips):
    return [(src, land, (x, y, 1 - c))]


def _small_ends(src, land, x, y, c, chips):
    m = src.shape[0]
    rows = land.at[pl.ds((4 * x + 2 * y + c) * m, m)]
    peers = [(x, y, 1 - c)] + [(*chip, c) for chip in chips] + [(*chip, 1 - c) for chip in chips]
    return [(src, rows, to) for to in peers]


PAIR_PEERS, SMALL_PEERS = 1, 7
SHAKES = {"pair_1": ("pair", 0), "pair_2": ("pair", 1), "share_1": ("pair", 2), "share_2": ("pair", 3),
          "chip_1": ("chips", 4), "chip_2": ("chips", 5), "gather_b": ("both", 6),
          "relay_a": ("pair", 7), "relay_b1": ("pair", 8), "relay_b2": ("pair", 9), "relay_b3": ("pair", 10)}


def _row_tile(k):
    for t in (256, 240, 128, 176, 64, 32, 16):
        if k % t == 0:
            return t
    raise ValueError(k)


def _pair_sum(c_idx, g, got, name):
    _, k, n = g.shape
    kh = k // 2
    tm = _row_tile(kh)
    nb = kh // tm

    def body(c_ref, g_ref, r_ref, o_ref):
        o_ref[...] = (g_ref[...].astype(F32) + r_ref[...].astype(F32)).astype(BF16)

    return pl.pallas_call(
        body, name=name,
        grid_spec=pltpu.PrefetchScalarGridSpec(
            num_scalar_prefetch=1, grid=(nb,),
            in_specs=[pl.BlockSpec((N_CHIPS, tm, n), lambda i, c_ref: (0, c_ref[0] * nb + i, 0)),
                      pl.BlockSpec((N_CHIPS, tm, n), lambda i, c_ref: (0, i, 0))],
            out_specs=pl.BlockSpec((N_CHIPS, tm, n), lambda i, c_ref: (0, i, 0))),
        out_shape=jax.ShapeDtypeStruct((N_CHIPS, kh, n), BF16),
        compiler_params=_params("parallel"),
    )(c_idx, g, got)


def _chip_sum_small(s_idx, items, name):
    n = len(items)

    def body(s_ref, *refs):
        for k in range(n):
            m_ref, r_ref, o_ref = refs[2 * k], refs[2 * k + 1], refs[2 * n + k]
            acc = m_ref[0].astype(F32)
            for j in range(3):
                acc = acc + r_ref[j].astype(F32)
            o_ref[...] = acc

    in_specs, out_specs, out_shape, flat = [], [], [], []
    for mine, got in items:
        _, kh, w = mine.shape
        in_specs += [pl.BlockSpec((1, kh, w), lambda i, s_ref: (s_ref[0], 0, 0)),
                     pl.BlockSpec((3, kh, w), lambda i, s_ref: (0, 0, 0))]
        out_specs.append(pl.BlockSpec((kh, w), lambda i, s_ref: (0, 0)))
        out_shape.append(jax.ShapeDtypeStruct((kh, w), F32))
        flat += [mine, got]
    return pl.pallas_call(
        body, name=name,
        grid_spec=pltpu.PrefetchScalarGridSpec(num_scalar_prefetch=1, grid=(1,), in_specs=in_specs,
                                               out_specs=out_specs),
        out_shape=out_shape, compiler_params=_params("arbitrary"),
    )(s_idx, *flat)


def _chip_sum(s_idx, mine, got, name):
    _, kh, n = mine.shape
    tm = _row_tile(kh)

    def body(s_ref, m_ref, r_ref, o_ref):
        acc = m_ref[0].astype(F32)
        for j in range(3):
            acc = acc + r_ref[j].astype(F32)
        o_ref[...] = acc

    return pl.pallas_call(
        body, name=name,
        grid_spec=pltpu.PrefetchScalarGridSpec(
            num_scalar_prefetch=1, grid=(kh // tm,),
            in_specs=[pl.BlockSpec((1, tm, n), lambda i, s_ref: (s_ref[0], i, 0)),
                      pl.BlockSpec((3, tm, n), lambda i, s_ref: (0, i, 0))],
            out_specs=pl.BlockSpec((tm, n), lambda i, s_ref: (i, 0))),
        out_shape=jax.ShapeDtypeStruct((kh, n), F32),
        compiler_params=_params("parallel"),
    )(s_idx, mine, got)


def _adamw_math(w, g, m, v):
    m = ADAM_B1 * m + (1.0 - ADAM_B1) * g
    v = ADAM_B2 * v + (1.0 - ADAM_B2) * (g * g)
    m_hat = m / (1.0 - ADAM_B1 ** ADAM_STEP)
    v_hat = v / (1.0 - ADAM_B2 ** ADAM_STEP)
    delta = -ADAM_LR * (m_hat / (jnp.sqrt(v_hat) + ADAM_EPS) + ADAM_WD * w)
    return delta, m, v


def _adamw_small(c_idx, items, name):
    n = len(items)

    def body(c_ref, *refs):
        mine = pl.program_id(0) == c_ref[0]
        for k in range(n):
            w_ref, gm_ref, go_ref, m_ref, v_ref = refs[5 * k:5 * k + 5]
            g_ref, d_ref, mo_ref, vo_ref = refs[5 * n + 4 * k:5 * n + 4 * k + 4]
            g = jnp.where(mine, gm_ref[...], go_ref[...])
            d, mm, vv = _adamw_math(w_ref[...], g, m_ref[...], v_ref[...])
            g_ref[...] = g
            d_ref[...] = d
            mo_ref[...] = mm
            vo_ref[...] = vv

    in_specs, out_specs, out_shape, flat = [], [], [], []
    for w, g_mine, g_other, m, v in items:
        rows, cols = w.shape
        tm = rows // 4
        full = pl.BlockSpec((tm, cols), lambda h, i, c_ref: (2 * h + i, 0))
        own = pl.BlockSpec((tm, cols), lambda h, i, c_ref: (jnp.where(h == c_ref[0], i, 0), 0))
        other = pl.BlockSpec((tm, cols), lambda h, i, c_ref: (jnp.where(h == c_ref[0], 0, i), 0))
        in_specs += [full, own, other, full, full]
        out_specs += [full] * 4
        out_shape += [jax.ShapeDtypeStruct((rows, cols), F32)] * 4
        flat += [w, g_mine, g_other, m, v]
    res = pl.pallas_call(
        body, name=name,
        grid_spec=pltpu.PrefetchScalarGridSpec(num_scalar_prefetch=1, grid=(2, 2), in_specs=in_specs,
                                               out_specs=out_specs),
        out_shape=out_shape, compiler_params=_params("arbitrary", "arbitrary"),
    )(c_idx, *flat)
    return [tuple(res[4 * k:4 * k + 4]) for k in range(n)]


def _adamw(c_idx, w, g_mine, g_other, m, v, name):
    k, n = w.shape
    tm = k // 4

    def body(c_ref, w_ref, gm_ref, go_ref, m_ref, v_ref, g_ref, d_ref, mo_ref, vo_ref):
        g = jnp.where(pl.program_id(0) == c_ref[0], gm_ref[...], go_ref[...])
        d, mm, vv = _adamw_math(w_ref[...], g, m_ref[...], v_ref[...])
        g_ref[...] = g
        d_ref[...] = d
        mo_ref[...] = mm
        vo_ref[...] = vv

    full = pl.BlockSpec((tm, n), lambda h, i, c_ref: (2 * h + i, 0))
    mine = pl.BlockSpec((tm, n), lambda h, i, c_ref: (jnp.where(h == c_ref[0], i, 0), 0))
    other = pl.BlockSpec((tm, n), lambda h, i, c_ref: (jnp.where(h == c_ref[0], 0, i), 0))
    shp = jax.ShapeDtypeStruct((k, n), F32)
    return pl.pallas_call(
        body, name=name,
        grid_spec=pltpu.PrefetchScalarGridSpec(
            num_scalar_prefetch=1, grid=(2, 2),
            in_specs=[full, mine, other, full, full], out_specs=[full] * 4),
        out_shape=[shp] * 4, compiler_params=_params("arbitrary", "arbitrary"),
    )(c_idx, w, g_mine, g_other, m, v)


VEC_SLOTS = {
    "g_mix_norm": (0, 0, D), "b_conv_proj": (0, D, D), "g_ffn_norm": (0, 2 * D, D),
    "g_final": (0, 3 * D, D), "b_in": (1, 0, INW), "conv_b": (2, 0, C), "ln_g": (2, C, C),
    "ln_b": (2, 2 * C, C), "sinks": (2, 3 * C, NQ), "loss": (2, 3 * C + 128, 1),
}
VEC_ROWS, VEC_COLS = 8, 4 * D
CW_ROWS = 32
SMALL_NAMES = ["g_mix_norm", "b_in", "sinks", "conv_w", "conv_b", "ln_g", "ln_b",
               "b_conv_proj", "g_ffn_norm", "g_final"]
CW_LANES = C // N_CHIPS


def _pack_small(gs, loss):
    row0 = jnp.concatenate([gs["g_mix_norm"], gs["b_conv_proj"], gs["g_ffn_norm"], gs["g_final"]], axis=1)
    row1 = jnp.pad(gs["b_in"], ((0, 0), (0, VEC_COLS - INW)))
    row2 = jnp.concatenate([gs["conv_b"], gs["ln_g"], gs["ln_b"],
                            jnp.pad(gs["sinks"], ((0, 0), (0, 128 - NQ))),
                            jnp.pad(loss.reshape(1, 1), ((0, 0), (0, VEC_COLS - 3 * C - 129)))], axis=1)
    vec = jnp.concatenate([row0, row1, row2, jnp.zeros((VEC_ROWS - 3, VEC_COLS), F32)], axis=0)
    cw = jnp.pad(gs["conv_w"], ((0, CW_ROWS - KW), (0, 0)))
    return vec, cw


def _small_update(idx, vec_own, cw_own, vec_all, cw_all, wmv):
    nsm = len(SMALL_NAMES)

    def body(s_ref, vown_ref, cown_ref, vec_ref, cw_ref, *refs):
        ins = refs[:3 * nsm]
        outs = refs[3 * nsm:7 * nsm]
        loss_ref = refs[7 * nsm]
        me = s_ref[1]

        def summed(own_ref, table_ref, rows_per_dev, r0, nrows, lane, width):
            acc = None
            for k in range(8):
                piece = jnp.where(me == k, own_ref[r0:r0 + nrows, lane:lane + width],
                                  table_ref[k * rows_per_dev + r0:k * rows_per_dev + r0 + nrows, lane:lane + width])
                acc = piece if acc is None else acc + piece
            return acc

        def total(slot):
            row, lane, width = slot
            return summed(vown_ref, vec_ref, VEC_ROWS, row, 1, lane, width)

        loss_ref[...] = jnp.broadcast_to(total(VEC_SLOTS["loss"]), loss_ref.shape)
        for p, name in enumerate(SMALL_NAMES):
            w_ref, m_ref, v_ref = ins[3 * p:3 * p + 3]
            g_ref, d_ref, mo_ref, vo_ref = outs[4 * p:4 * p + 4]
            if name == "conv_w":
                g = jnp.zeros((KW, CW_LANES), F32)
                for s in range(N_CHIPS):
                    cand = summed(cown_ref, cw_ref, CW_ROWS, 0, KW, s * CW_LANES, CW_LANES)
                    g = jnp.where(s_ref[0] == s, cand, g)
            else:
                g = total(VEC_SLOTS[name])
            d, mm, vv = _adamw_math(w_ref[...], g, m_ref[...], v_ref[...])
            g_ref[...] = g
            d_ref[...] = d
            mo_ref[...] = mm
            vo_ref[...] = vv

    vmem = pl.BlockSpec(memory_space=pltpu.VMEM)
    flat = [a for t in wmv for a in t]
    out_shape = []
    for w, _, _ in wmv:
        out_shape += [jax.ShapeDtypeStruct(w.shape, F32)] * 4
    out_shape.append(jax.ShapeDtypeStruct((1, 128), F32))
    res = pl.pallas_call(
        body, name="small_update",
        in_specs=[pl.BlockSpec(memory_space=pltpu.SMEM)] + [vmem] * (4 + len(flat)),
        out_specs=[vmem] * len(out_shape), out_shape=out_shape,
    )(idx, vec_own, cw_own, vec_all, cw_all, *flat)
    return [tuple(res[4 * p:4 * p + 4]) for p in range(nsm)], res[4 * nsm]


SMALL_BIG = ("w_out", "w_attn_proj", "w_conv_proj")
WEIGHT_ORDER = ["g_mix_norm", "w_in", "b_in", "sinks", "conv_w", "conv_b", "ln_g", "ln_b",
                "w_attn_proj", "w_conv_proj", "b_conv_proj", "w_out", "g_ffn_norm", "w_ffn_in",
                "w_ffn_down", "g_final"]


def kernel(x, g_mix_norm, w_in, b_in, sinks, conv_w, conv_b, ln_g, ln_b, w_attn_proj, w_conv_proj, b_conv_proj, w_out, g_ffn_norm, w_ffn_in, w_ffn_down, g_final, loss_target, m_g_mix_norm, m_w_in, m_b_in, m_sinks, m_conv_w, m_conv_b, m_ln_g, m_ln_b, m_w_attn_proj, m_w_conv_proj, m_b_conv_proj, m_w_out, m_g_ffn_norm, m_w_ffn_in, m_w_ffn_down, m_g_final, v_g_mix_norm, v_w_in, v_b_in, v_sinks, v_conv_w, v_conv_b, v_ln_g, v_ln_b, v_w_attn_proj, v_w_conv_proj, v_b_conv_proj, v_w_out, v_g_ffn_norm, v_w_ffn_in, v_w_ffn_down, v_g_final):
    w = dict(g_mix_norm=g_mix_norm, w_in=w_in, b_in=b_in, sinks=sinks, conv_w=conv_w, conv_b=conv_b,
             ln_g=ln_g, ln_b=ln_b, w_attn_proj=w_attn_proj, w_conv_proj=w_conv_proj,
             b_conv_proj=b_conv_proj, w_out=w_out, g_ffn_norm=g_ffn_norm, w_ffn_in=w_ffn_in,
             w_ffn_down=w_ffn_down, g_final=g_final)
    m = dict(g_mix_norm=m_g_mix_norm, w_in=m_w_in, b_in=m_b_in, sinks=m_sinks, conv_w=m_conv_w,
             conv_b=m_conv_b, ln_g=m_ln_g, ln_b=m_ln_b, w_attn_proj=m_w_attn_proj,
             w_conv_proj=m_w_conv_proj, b_conv_proj=m_b_conv_proj, w_out=m_w_out,
             g_ffn_norm=m_g_ffn_norm, w_ffn_in=m_w_ffn_in, w_ffn_down=m_w_ffn_down, g_final=m_g_final)
    v = dict(g_mix_norm=v_g_mix_norm, w_in=v_w_in, b_in=v_b_in, sinks=v_sinks, conv_w=v_conv_w,
             conv_b=v_conv_b, ln_g=v_ln_g, ln_b=v_ln_b, w_attn_proj=v_w_attn_proj,
             w_conv_proj=v_w_conv_proj, b_conv_proj=v_b_conv_proj, w_out=v_w_out,
             g_ffn_norm=v_g_ffn_norm, w_ffn_in=v_w_ffn_in, w_ffn_down=v_w_ffn_down, g_final=v_g_final)

    c_idx = lax.axis_index("c").astype(jnp.int32).reshape(1)
    s_idx = (2 * lax.axis_index("x") + lax.axis_index("y")).astype(jnp.int32).reshape(1)

    out_g, out_d, out_m, out_v = {}, {}, {}, {}

    def gather_start(tag, shards):
        lands = [lax.empty((N_CHIPS,) + s.shape, s.dtype) for s in shards]
        return _chip_start("gather_start_" + tag, _gather_ends, GATHER_PEERS, shards, lands,
                           SHAKES.get("gather_" + tag))

    def gather_relay(tag, state, after, first=0, count=None):
        send_sems, recv_sems, shards, lands, _ = state
        last = len(shards) if count is None else first + count
        return _gather_relay("gather_relay_" + tag, send_sems, recv_sems, shards[first:last],
                             lands[first:last], after, first, SHAKES["relay_" + tag])

    def gather_finish(tag, relay, after):
        return _forward_wait("forward_wait_" + tag, relay[0], relay[1], relay[2], after)

    names_b = ["w_attn_proj", "w_conv_proj", "w_out", "w_ffn_in", "w_ffn_down"]
    big = {name: (w[name][0], m[name][0], v[name][0]) for name in names_b}
    big["w_in"] = (w_in[0].T, m_w_in[0].T, v_w_in[0].T)
    state_a = gather_start("a", [big["w_in"][0].astype(BF16), jnp.pad(conv_w[0], ((0, CW_ROWS - KW), (0, 0)))])
    state_b = gather_start("b", [(big[name][0] + state_a[4][0, 0]).astype(BF16) for name in names_b])
    got_a = gather_finish("a", gather_relay("a", state_a, [state_b[4]]), [])
    w_in_t_full = got_a[0].reshape(INW, D)
    conv_w_full = got_a[1].transpose(1, 0, 2).reshape(CW_ROWS, C)[:KW]

    xs, target = x[0], loss_target[0]
    g_final2 = g_final.reshape(1, D)
    h, qkv, glu, gl = _in_proj(xs, g_mix_norm, w_in_t_full, b_in)
    o, lse, probs = _attn_fwd(qkv, sinks)
    relay_1 = gather_relay("b1", state_b, [o], 0, 3)
    u, cact = _conv_fwd(glu, conv_w_full, conv_b, ln_g, ln_b, relay_1[3])
    w_ap4, w_cp4, w_out4 = gather_finish("b1", relay_1, [cact])
    w_out_full = w_out4.reshape(D, D)
    mixed = _mix_out(xs, o, cact, gl, w_ap4, w_cp4, b_conv_proj, w_out_full, relay_1[3])
    relay_2 = gather_relay("b2", state_b, [mixed[3]], 3, 1)
    ya, yc, mg, x1 = _mix_out(xs, o, cact, gl, w_ap4, w_cp4, b_conv_proj, w_out_full, relay_2[3], mixed)
    w_fi4, = gather_finish("b2", relay_2, [x1])
    h2, gu, act = _ffn_in_first(x1, g_ffn_norm, w_fi4, relay_2[3])
    relay_3 = gather_relay("b3", state_b, [h2], 4, 1)
    gu, act = _ffn_in_second(h2, w_fi4, gu, act, relay_3[3])
    w_dn4, = gather_finish("b3", relay_3, [act])
    w_dn_full = w_dn4.reshape(DFF, D)
    dx2, dx2b, dg_final, loss_part = _ffn_out_loss(x1, act, w_dn_full, g_final2, target)

    def exchange_start(tag, grads):
        lands = [lax.empty((N_CHIPS, g.shape[1] // 2, g.shape[2]), g.dtype) for g in grads]
        return _chip_start("pair_start_" + tag, _exchange_ends, PAIR_PEERS, grads, lands, SHAKES["pair_" + tag])

    def reduce_start(tag, names, exchange, after):
        send_sems, recv_sems, grads, lands, _ = exchange
        grads, from_sibling = _chip_wait("pair_wait_" + tag, _exchange_ends, send_sems, recv_sems, grads, lands, after)
        pair = [_pair_sum(c_idx, g, r, "pair_sum_" + name) for name, g, r in zip(names, grads, from_sibling)]
        lands = [lax.empty((3,) + p.shape[1:], p.dtype) for p in pair]
        return _chip_start("chip_start_" + tag, _reduce_ends, REDUCE_PEERS, pair, lands, SHAKES["chip_" + tag])

    def reduce_sum(tag, names, state, after):
        send_sems, recv_sems, pair, lands, _ = state
        pair, lands = _chip_wait("chip_wait_" + tag, _reduce_ends, send_sems, recv_sems, pair, lands, after)
        mine = {name: _chip_sum(s_idx, p, r, "chip_sum_" + name)
                for name, p, r in zip(names, pair, lands) if name not in SMALL_BIG}
        small = [(p, r) for name, p, r in zip(names, pair, lands) if name in SMALL_BIG]
        if small:
            sums = _chip_sum_small(s_idx, small, "chip_sum_small_" + tag)
            mine.update(zip([name for name in names if name in SMALL_BIG], sums))
        mine = [mine[name] for name in names]
        others = [lax.empty(a.shape, a.dtype) for a in mine]
        return _chip_start("share_start_" + tag, _share_ends, PAIR_PEERS, mine, others, SHAKES["share_" + tag])

    def reduce_finish(tag, names, share, after):
        send_sems, recv_sems, mine, others, _ = share
        mine, others = _chip_wait("share_wait_" + tag, _share_ends, send_sems, recv_sems, mine, others, after)
        results = {name: _adamw(c_idx, big[name][0], g_mine, g_other, big[name][1], big[name][2], "adamw_" + name)
                   for name, g_mine, g_other in zip(names, mine, others) if name not in SMALL_BIG}
        small = [name for name in names if name in SMALL_BIG]
        if small:
            halves = dict(zip(names, zip(mine, others)))
            items = [(big[name][0], *halves[name], big[name][1], big[name][2]) for name in small]
            results.update(zip(small, _adamw_small(c_idx, items, "adamw_small_" + tag)))
        for name in names:
            res = results[name]
            if name == "w_in":
                res = [a.T for a in res]
            out_g[name], out_d[name], out_m[name], out_v[name] = [a[None] for a in res]

    dgu, dx1, dx1b, dg_ffn = _ffn_bwd(dx2, dx2b, gu, x1, g_ffn_norm, w_dn_full, w_fi4)
    names_1 = ["w_ffn_in", "w_ffn_down", "w_out", "w_attn_proj", "w_conv_proj"]
    grads_1 = [_grad_w(h2, dgu, "grad_w_ffn_in", 512, FSH, True),
               _grad_w(act, dx2b, "grad_w_ffn_down", 256, D, False)]
    dya, dyc, dgl, do, dc, db_cp = _mix_bwd(dx1b, gl, ya, yc, w_out_full, w_ap4, w_cp4, relay_3[3])
    grads_1 += [_grad_w(mg, dx1b, "grad_w_out", 512, D, False),
                _grad_w(o, dya, "grad_w_attn_proj", 512, D, True),
                _grad_w(cact, dyc, "grad_w_conv_proj", 512, D, True)]
    exchange_1 = exchange_start("1", grads_1)
    dq, dkv, dsinks = _attn_bwd(qkv, do, lse, probs, sinks, exchange_1[4])
    state_1 = reduce_start("1", names_1, exchange_1, [dq])
    dglu, dconv_w, dconv_b, dln_g, dln_b = _conv_bwd(glu, u, dc, conv_w_full, ln_g, ln_b, state_1[4])
    names_2 = ["w_in"]
    gw_in_t = _grad_w_in_t(h, dq, dkv, dglu, dgl)
    exchange_2 = exchange_start("2", [gw_in_t.reshape(N_CHIPS, INW // N_CHIPS, D)])
    grad_x, dg_mix, db_in = _in_proj_bwd(dq, dkv, dglu, dgl, xs, dx1, g_mix_norm, w_in_t_full, exchange_2[4])

    gs = {"g_mix_norm": dg_mix, "b_in": db_in, "sinks": dsinks[:, 0].reshape(1, NQ),
          "conv_w": dconv_w, "conv_b": dconv_b, "ln_g": dln_g, "ln_b": dln_b,
          "b_conv_proj": db_cp, "g_ffn_norm": dg_ffn, "g_final": dg_final}
    blocks = list(_pack_small(gs, loss_part[0, 0]))
    tables = [lax.empty((8 * b.shape[0], b.shape[1]), b.dtype) for b in blocks]
    small = _chip_start("small_start", _small_ends, SMALL_PEERS, blocks, tables)

    state_2 = reduce_start("2", names_2, exchange_2, [grad_x, small[4]])
    share_1 = reduce_sum("1", names_1, state_1, [state_2[4]])
    blocks, tables = _chip_wait("small_wait", _small_ends, small[0], small[1], small[2], small[3],
                                [share_1[4]])
    me = (4 * lax.axis_index("x") + 2 * lax.axis_index("y") + lax.axis_index("c")).astype(jnp.int32)

    def view(a, name):
        if name == "conv_w":
            return a[0]
        if name == "g_final":
            return a.reshape(1, D)
        return a

    wmv = [(view(w[name], name), view(m[name], name), view(v[name], name)) for name in SMALL_NAMES]
    small_out, loss_row = _small_update(jnp.concatenate([s_idx, me.reshape(1)]), blocks[0], blocks[1],
                                        tables[0], tables[1], wmv)
    for name, (g, d, mm, vv) in zip(SMALL_NAMES, small_out):
        shape = w[name].shape
        out_g[name], out_d[name], out_m[name], out_v[name] = (
            g.reshape(shape), d.reshape(shape), mm.reshape(shape), vv.reshape(shape))

    reduce_finish("1", names_1, share_1, [loss_row])
    share_2 = reduce_sum("2", names_2, state_2, [out_d["w_conv_proj"]])
    reduce_finish("2", names_2, share_2, [])

    loss = loss_row[0, 0]
    return (loss, grad_x[None], *[out_g[k] for k in WEIGHT_ORDER], *[out_d[k] for k in WEIGHT_ORDER],
            *[out_m[k] for k in WEIGHT_ORDER], *[out_v[k] for k in WEIGHT_ORDER])
```

```python
import functools

import jax
import jax.numpy as jnp
from jax import lax
from jax.experimental import pallas as pl
from jax.experimental.pallas import tpu as pltpu

F32 = jnp.float32
BF16 = jnp.bfloat16

T = 2048
D = 1024
HD = 64
NQ = 8
NKV = 2
GROUP = NQ // NKV
BLK = 128
AW = NQ * HD
KVW = NKV * HD
C = 512
KW = 31
QKVW = AW + 2 * KVW
GLU_OFF = QKVW
GATE_OFF = GLU_OFF + 2 * C
INW = GATE_OFF + 2 * D
DFF = 2816
EPS = 1e-5
NEG = -1e30
SCALE = HD ** -0.5
HALO = 32
N_CHIPS = 4
FSH = 2 * DFF // N_CHIPS

ADAM_LR = 0.001
ADAM_B1 = 0.9
ADAM_B2 = 0.999
ADAM_EPS = 1e-08
ADAM_WD = 0.01
ADAM_STEP = 10

VMEM_LIMIT = 56 * 1024 * 1024
ROW_TM = 512
MESH = pl.DeviceIdType.MESH


def _params(*sem):
    return pltpu.CompilerParams(dimension_semantics=sem, vmem_limit_bytes=VMEM_LIMIT)


def _dot(a, b):
    return jnp.dot(a, b, preferred_element_type=F32)


def _dot_nt(a, b):
    return lax.dot_general(a, b, (((1,), (1,)), ((), ())), preferred_element_type=F32)


def _dot_tn(a, b):
    return lax.dot_general(a, b, (((0,), (0,)), ((), ())), preferred_element_type=F32)


def _sigmoid(v):
    return 1.0 / (1.0 + jnp.exp(-v))


def _rows(tm, n):
    return pl.BlockSpec((tm, n), lambda i: (i, 0))


def _whole(shape):
    return pl.BlockSpec(shape, lambda i: tuple(0 for _ in shape))


def _in_proj(x, g_mix, w_in_t, b_in):
    tm = ROW_TM

    def body(x_ref, g_ref, w_ref, b_ref, h_ref, qkv_ref, glu_ref, gl_ref):
        xv = x_ref[...]
        r = lax.rsqrt(jnp.mean(xv * xv, axis=-1, keepdims=True) + EPS)
        h = (xv * r * g_ref[...]).astype(BF16)
        h_ref[...] = h
        qkv_ref[...] = (_dot_nt(h, w_ref[0:GLU_OFF, :]) + b_ref[:, 0:GLU_OFF]).astype(BF16)
        glu_ref[...] = (_dot_nt(h, w_ref[GLU_OFF:GATE_OFF, :]) + b_ref[:, GLU_OFF:GATE_OFF]).astype(BF16)
        gl_ref[...] = (_dot_nt(h, w_ref[GATE_OFF:INW, :]) + b_ref[:, GATE_OFF:INW]).astype(BF16)

    return pl.pallas_call(
        body, name="in_proj", grid=(T // tm,),
        in_specs=[_rows(tm, D), _whole((1, D)), _whole((INW, D)), _whole((1, INW))],
        out_specs=[_rows(tm, D), _rows(tm, QKVW), _rows(tm, 2 * C), _rows(tm, 2 * D)],
        out_shape=[jax.ShapeDtypeStruct((T, D), BF16), jax.ShapeDtypeStruct((T, QKVW), BF16),
                   jax.ShapeDtypeStruct((T, 2 * C), BF16), jax.ShapeDtypeStruct((T, 2 * D), BF16)],
        compiler_params=_params("parallel"),
    )(x, g_mix, w_in_t, b_in)


GROWS = GROUP * BLK
BAND = 2 * BLK
ATT_SUB = 2


def _band(i):
    rb = pl.multiple_of(jnp.maximum(i - 1, 0) * BLK, BLK)
    row = lax.broadcasted_iota(jnp.int32, (GROWS, BAND), 0)
    kpos = rb + lax.broadcasted_iota(jnp.int32, (GROWS, BAND), 1)
    qpos = i * BLK + jnp.bitwise_and(row, BLK - 1)
    return rb, jnp.logical_and(kpos <= qpos, kpos > qpos - BLK)


def _sink_column(sink_ref, g):
    head = lax.shift_right_logical(lax.broadcasted_iota(jnp.int32, (GROWS, 1), 0), 7)
    col = jnp.full((GROWS, 1), sink_ref[0, g * GROUP], F32)
    for hh in range(1, GROUP):
        col = jnp.where(head == hh, sink_ref[0, g * GROUP + hh], col)
    return col


def _attn_fwd(qkv, sinks):
    def body(sink_ref, qkv_ref, o_ref, lse_ref, p_ref, s_ref):
        slots = [(sub, g) for sub in range(ATT_SUB) for g in range(NKV)]
        bands = [_band(pl.program_id(0) * ATT_SUB + sub) for sub in range(ATT_SUB)]
        for n, (sub, g) in enumerate(slots):
            rb = bands[sub][0]
            r0 = pl.multiple_of((pl.program_id(0) * ATT_SUB + sub) * BLK, BLK)
            kband = qkv_ref[pl.ds(rb, BAND), AW + g * HD:AW + (g + 1) * HD]
            for hh in range(GROUP):
                h = g * GROUP + hh
                s_ref[n, hh * BLK:(hh + 1) * BLK, :] = _dot_nt(
                    qkv_ref[pl.ds(r0, BLK), h * HD:(h + 1) * HD], kband)
        lses = []
        for n, (sub, g) in enumerate(slots):
            s = jnp.where(bands[sub][1], s_ref[n] * SCALE, NEG)
            sink = _sink_column(sink_ref, g)
            m = jnp.maximum(jnp.max(s, axis=-1, keepdims=True), sink)
            p = jnp.exp(s - m)
            den = jnp.sum(p, axis=-1, keepdims=True) + jnp.exp(sink - m)
            p_ref[n] = (p * (1.0 / den)).astype(BF16)
            lses.append(m + jnp.log(den))
        for n, (sub, g) in enumerate(slots):
            rb = bands[sub][0]
            rows = slice(sub * BLK, (sub + 1) * BLK)
            vband = qkv_ref[pl.ds(rb, BAND), AW + KVW + g * HD:AW + KVW + (g + 1) * HD]
            for hh in range(GROUP):
                h = g * GROUP + hh
                o_ref[rows, h * HD:(h + 1) * HD] = _dot(p_ref[n, hh * BLK:(hh + 1) * BLK, :], vband).astype(BF16)
                lse_ref[rows, h:h + 1] = lses[n][hh * BLK:(hh + 1) * BLK]

    nslot = ATT_SUB * NKV
    return pl.pallas_call(
        body, name="attn_fwd", grid=(T // (ATT_SUB * BLK),),
        in_specs=[pl.BlockSpec(memory_space=pltpu.SMEM), _whole((T, QKVW))],
        out_specs=[_rows(ATT_SUB * BLK, AW), _rows(ATT_SUB * BLK, NQ),
                   pl.BlockSpec((nslot, GROWS, BAND), lambda i: (i, 0, 0))],
        out_shape=[jax.ShapeDtypeStruct((T, AW), BF16), jax.ShapeDtypeStruct((T, NQ), F32),
                   jax.ShapeDtypeStruct((T // BLK * NKV, GROWS, BAND), BF16)],
        scratch_shapes=[pltpu.VMEM((nslot, GROWS, BAND), F32)],
        compiler_params=_params("parallel"),
    )(sinks, qkv)


CONV_TM = 256
CONV_SUB = 32


def _glu(ab):
    a = ab[:, 0:C].astype(F32)
    b = ab[:, C:2 * C].astype(F32)
    return a * _sigmoid(b)


SUBLANES = 8


def _shifted_copies(ref):
    rows = ref.shape[1] - SUBLANES
    for r in range(1, SUBLANES):
        ref[r, 0:rows, :] = ref[0, r:r + rows, :]


def _shifted_rows(ref, start, size):
    r = start % SUBLANES
    return ref[r, start - r:start - r + size, :]


def _conv_fwd(glu, conv_w, conv_b, ln_g, ln_b, dep):
    tm = CONV_TM

    def body(cur_ref, prev_ref, w_ref, cb_ref, g_ref, b_ref, dep_ref, u_ref, c_ref, zs_ref):
        i = pl.program_id(0)
        zprev = _glu(prev_ref[tm - HALO:tm, :])
        zs_ref[0, 0:HALO, :] = jnp.where(i > 0, zprev, 0.0)
        zs_ref[0, HALO:HALO + tm, :] = _glu(cur_ref[...])
        _shifted_copies(zs_ref)
        for s in range(tm // CONV_SUB):
            base = HALO + s * CONV_SUB - (KW - 1)
            acc = jnp.broadcast_to(cb_ref[...], (CONV_SUB, C))
            for j in range(KW):
                acc = acc + w_ref[j:j + 1, :] * _shifted_rows(zs_ref, base + j, CONV_SUB)
            rows = slice(s * CONV_SUB, (s + 1) * CONV_SUB)
            u_ref[rows, :] = acc
            mu = jnp.mean(acc, axis=-1, keepdims=True)
            xc = acc - mu
            var = jnp.mean(xc * xc, axis=-1, keepdims=True)
            y = xc * lax.rsqrt(var + EPS) * g_ref[...] + b_ref[...]
            c_ref[rows, :] = (y * _sigmoid(y)).astype(BF16)

    return pl.pallas_call(
        body, name="conv_fwd", grid=(T // tm,),
        in_specs=[_rows(tm, 2 * C),
                  pl.BlockSpec((tm, 2 * C), lambda i: (jnp.maximum(i - 1, 0), 0)),
                  _whole((KW, C)), _whole((1, C)), _whole((1, C)), _whole((1, C)), _whole((8, 128))],
        out_specs=[_rows(tm, C), _rows(tm, C)],
        out_shape=[jax.ShapeDtypeStruct((T, C), F32), jax.ShapeDtypeStruct((T, C), BF16)],
        scratch_shapes=[pltpu.VMEM((SUBLANES, HALO + tm, C), F32)],
        compiler_params=_params("parallel"),
    )(glu, glu, conv_w, conv_b, ln_g, ln_b, dep)


def _branch_outputs(o, cact, wap_ref, wcp_ref, bcp_ref):
    ya = jnp.concatenate([_dot(o, wap_ref[s]) for s in range(N_CHIPS)], axis=1)
    yc = jnp.concatenate([_dot(cact, wcp_ref[s]) for s in range(N_CHIPS)], axis=1) + bcp_ref[...]
    return ya, yc


def _mix_out(x, o, cact, gl, w_ap, w_cp, b_cp, w_out, dep, begun=None):
    tm = ROW_TM
    steps = T // tm // 2
    first = 0 if begun is None else steps
    rows = lambda n: pl.BlockSpec((tm, n), lambda i: (i + first, 0))
    extra = [] if begun is None else list(begun)

    def body(x_ref, o_ref, c_ref, gl_ref, wap_ref, wcp_ref, bcp_ref, wo_ref, dep_ref, *rest):
        ya_ref, yc_ref, mg_ref, x1_ref = rest[len(extra):]
        ya, yc = _branch_outputs(o_ref[...], c_ref[...], wap_ref, wcp_ref, bcp_ref)
        g0 = _sigmoid(gl_ref[:, 0:D].astype(F32))
        g1 = _sigmoid(gl_ref[:, D:2 * D].astype(F32))
        mg = (g0 * ya + g1 * yc).astype(BF16)
        ya_ref[...] = ya.astype(BF16)
        yc_ref[...] = yc.astype(BF16)
        mg_ref[...] = mg
        x1_ref[...] = x_ref[...] + _dot(mg, wo_ref[...])

    return pl.pallas_call(
        body, name="mix_out_first" if begun is None else "mix_out_second", grid=(steps,),
        in_specs=[rows(D), rows(AW), rows(C), rows(2 * D),
                  _whole((N_CHIPS, AW, D // N_CHIPS)), _whole((N_CHIPS, C, D // N_CHIPS)), _whole((1, D)),
                  _whole((D, D)), _whole((8, 128))] + [ANY_SPEC] * len(extra),
        out_specs=[rows(D), rows(D), rows(D), rows(D)],
        out_shape=[jax.ShapeDtypeStruct((T, D), BF16), jax.ShapeDtypeStruct((T, D), BF16),
                   jax.ShapeDtypeStruct((T, D), BF16), jax.ShapeDtypeStruct((T, D), F32)],
        input_output_aliases={9 + k: k for k in range(len(extra))},
        compiler_params=_params("parallel"),
    )(x, o, cact, gl, w_ap, w_cp, b_cp, w_out, dep, *extra)


def _swiglu_half(h, wg_ref, wu_ref, gu_ref, act_ref):
    gate = _dot(h, wg_ref[0])
    up = _dot(h, wu_ref[0])
    gu_ref[:, 0:FSH] = gate.astype(BF16)
    gu_ref[:, FSH:2 * FSH] = up.astype(BF16)
    act_ref[...] = (gate * _sigmoid(gate) * up).astype(BF16)


def _shard_spec(k):
    return pl.BlockSpec((1, D, FSH), lambda i: (k, 0, 0), pipeline_mode=pl.Buffered(1))


def _ffn_in_first(x1, g_ffn, w_fi, dep):
    tm = ROW_TM

    def body(x_ref, g_ref, wg_ref, wu_ref, dep_ref, h_ref, gu_ref, act_ref):
        xv = x_ref[...]
        r = lax.rsqrt(jnp.mean(xv * xv, axis=-1, keepdims=True) + EPS)
        h = (xv * r * g_ref[...]).astype(BF16)
        h_ref[...] = h
        _swiglu_half(h, wg_ref, wu_ref, gu_ref, act_ref)

    return pl.pallas_call(
        body, name="ffn_in_first", grid=(T // tm,),
        in_specs=[_rows(tm, D), _whole((1, D)), _shard_spec(0), _shard_spec(2), _whole((8, 128))],
        out_specs=[_rows(tm, D), pl.BlockSpec((tm, 2 * FSH), lambda i: (i, 0)),
                   pl.BlockSpec((tm, FSH), lambda i: (i, 0))],
        out_shape=[jax.ShapeDtypeStruct((T, D), BF16), jax.ShapeDtypeStruct((T, 2 * DFF), BF16),
                   jax.ShapeDtypeStruct((T, DFF), BF16)],
        compiler_params=_params("parallel"),
    )(x1, g_ffn, w_fi, w_fi, dep)


def _ffn_in_second(h2, w_fi, gu, act, dep):
    tm = ROW_TM

    def body(h_ref, wg_ref, wu_ref, gu_in, act_in, dep_ref, gu_ref, act_ref):
        _swiglu_half(h_ref[...], wg_ref, wu_ref, gu_ref, act_ref)

    return pl.pallas_call(
        body, name="ffn_in_second", grid=(T // tm,),
        in_specs=[_rows(tm, D), _shard_spec(1), _shard_spec(3), ANY_SPEC, ANY_SPEC, _whole((8, 128))],
        out_specs=[pl.BlockSpec((tm, 2 * FSH), lambda i: (i, 1)), pl.BlockSpec((tm, FSH), lambda i: (i, 1))],
        out_shape=[jax.ShapeDtypeStruct((T, 2 * DFF), BF16), jax.ShapeDtypeStruct((T, DFF), BF16)],
        input_output_aliases={3: 0, 4: 1},
        compiler_params=_params("parallel"),
    )(h2, w_fi, w_fi, gu, act, dep)


def _ffn_out_loss(x1, act, w_dn, g_final, target):
    tm = ROW_TM // 2

    def body(x_ref, a_ref, w_ref, g_ref, t_ref, dx_ref, dxb_ref, dg_ref, loss_ref):
        i = pl.program_id(0)
        x2 = x_ref[...] + _dot(a_ref[...], w_ref[...])
        r = lax.rsqrt(jnp.mean(x2 * x2, axis=-1, keepdims=True) + EPS)
        xh = x2 * r
        g = g_ref[...]
        err = xh * g - t_ref[...]
        dy = err * (1.0 / D)
        dyg = dy * g
        dx = r * (dyg - xh * jnp.mean(dyg * xh, axis=-1, keepdims=True))
        dx_ref[...] = dx
        dxb_ref[...] = dx.astype(BF16)
        part = 0.5 * jnp.sum(jnp.mean(err * err, axis=-1, keepdims=True), axis=0, keepdims=True)

        @pl.when(i == 0)
        def _():
            dg_ref[...] = jnp.zeros_like(dg_ref)
            loss_ref[...] = jnp.zeros_like(loss_ref)

        dg_ref[...] += jnp.sum(dy * xh, axis=0, keepdims=True)
        loss_ref[...] += jnp.broadcast_to(part, loss_ref.shape)

    return pl.pallas_call(
        body, name="ffn_out_loss", grid=(T // tm,),
        in_specs=[_rows(tm, D), _rows(tm, DFF), _whole((DFF, D)), _whole((1, D)), _rows(tm, D)],
        out_specs=[_rows(tm, D), _rows(tm, D), _whole((1, D)), _whole((1, 128))],
        out_shape=[jax.ShapeDtypeStruct((T, D), F32), jax.ShapeDtypeStruct((T, D), BF16),
                   jax.ShapeDtypeStruct((1, D), F32), jax.ShapeDtypeStruct((1, 128), F32)],
        compiler_params=_params("arbitrary"),
    )(x1, act, w_dn, g_final, target)


def _const(shape):
    return pl.BlockSpec(shape, lambda i: tuple(0 for _ in shape), pipeline_mode=pl.Buffered(1))


def _ffn_bwd(dx2, dx2b, gu, x1, g_ffn, w_dn_t, w_fi_t):
    tm = ROW_TM // 2

    def body(dx_ref, dxb_ref, gu_ref, x_ref, g_ref, wdn_ref, wfi_ref,
             dgu_ref, dx1_ref, dx1b_ref, dg_ref):
        i = pl.program_id(0)
        dxb = dxb_ref[...]
        dh = jnp.zeros((tm, D), F32)
        dacts = [_dot_nt(dxb, wdn_ref[k * FSH:(k + 1) * FSH, :]) for k in range(N_CHIPS // 2)]
        for k in range(N_CHIPS // 2):
            c0 = k * FSH
            dact = dacts[k]
            gate = gu_ref[:, 2 * c0:2 * c0 + FSH].astype(F32)
            up = gu_ref[:, 2 * c0 + FSH:2 * c0 + 2 * FSH].astype(F32)
            s = _sigmoid(gate)
            dup = (dact * gate * s).astype(BF16)
            dgate = (dact * up * s * (1.0 + gate * (1.0 - s))).astype(BF16)
            dgu_ref[:, c0:c0 + FSH] = dgate
            dgu_ref[:, DFF + c0:DFF + c0 + FSH] = dup
            dh = dh + _dot_nt(dgate, wfi_ref[k]) + _dot_nt(dup, wfi_ref[k + N_CHIPS // 2])
        xv = x_ref[...]
        r = lax.rsqrt(jnp.mean(xv * xv, axis=-1, keepdims=True) + EPS)
        xh = xv * r
        dhg = dh * g_ref[...]
        dx1 = dx_ref[...] + r * (dhg - xh * jnp.mean(dhg * xh, axis=-1, keepdims=True))
        dx1_ref[...] = dx1
        dx1b_ref[...] = dx1.astype(BF16)

        @pl.when(i == 0)
        def _():
            dg_ref[...] = jnp.zeros_like(dg_ref)

        dg_ref[...] += jnp.sum(dh * xh, axis=0, keepdims=True)

    return pl.pallas_call(
        body, name="ffn_bwd", grid=(T // tm,),
        in_specs=[_rows(tm, D), _rows(tm, D), _rows(tm, 2 * DFF), _rows(tm, D), _whole((1, D)),
                  _const((DFF, D)), _const((N_CHIPS, D, FSH))],
        out_specs=[_rows(tm, 2 * DFF), _rows(tm, D), _rows(tm, D), _whole((1, D))],
        out_shape=[jax.ShapeDtypeStruct((T, 2 * DFF), BF16), jax.ShapeDtypeStruct((T, D), F32),
                   jax.ShapeDtypeStruct((T, D), BF16), jax.ShapeDtypeStruct((1, D), F32)],
        compiler_params=_params("arbitrary"),
    )(dx2, dx2b, gu, x1, g_ffn, w_dn_t, w_fi_t)


def _mix_bwd(dx1b, gl, ya, yc, w_out, w_ap, w_cp, dep):
    tm = ROW_TM

    def body(dx_ref, gl_ref, ya_ref, yc_ref, wo_ref, wap_ref, wcp_ref, dep_ref,
             dya_ref, dyc_ref, dgl_ref, do_ref, dc_ref, db_ref):
        i = pl.program_id(0)
        dm = _dot_nt(dx_ref[...], wo_ref[...])
        ya, yc = ya_ref[...].astype(F32), yc_ref[...].astype(F32)
        g0 = _sigmoid(gl_ref[:, 0:D].astype(F32))
        g1 = _sigmoid(gl_ref[:, D:2 * D].astype(F32))
        dya = dm * g0
        dyc = dm * g1
        dgl_ref[:, 0:D] = (dya * ya * (1.0 - g0)).astype(BF16)
        dgl_ref[:, D:2 * D] = (dyc * yc * (1.0 - g1)).astype(BF16)
        dyab = dya.astype(BF16)
        dycb = dyc.astype(BF16)
        dya_ref[...] = dyab
        dyc_ref[...] = dycb
        sw = D // N_CHIPS
        do = jnp.zeros((tm, AW), F32)
        dcv = jnp.zeros((tm, C), F32)
        for s in range(N_CHIPS):
            do = do + _dot_nt(dyab[:, s * sw:(s + 1) * sw], wap_ref[s])
            dcv = dcv + _dot_nt(dycb[:, s * sw:(s + 1) * sw], wcp_ref[s])
        do_ref[...] = do.astype(BF16)
        dc_ref[...] = dcv.astype(BF16)

        @pl.when(i == 0)
        def _():
            db_ref[...] = jnp.zeros_like(db_ref)

        db_ref[...] += jnp.sum(dyc, axis=0, keepdims=True)

    return pl.pallas_call(
        body, name="mix_bwd", grid=(T // tm,),
        in_specs=[_rows(tm, D), _rows(tm, 2 * D), _rows(tm, D), _rows(tm, D),
                  _whole((D, D)), _whole((N_CHIPS, AW, D // N_CHIPS)), _whole((N_CHIPS, C, D // N_CHIPS)),
                  _whole((8, 128))],
        out_specs=[_rows(tm, D), _rows(tm, D), _rows(tm, 2 * D), _rows(tm, AW), _rows(tm, C),
                   _whole((1, D))],
        out_shape=[jax.ShapeDtypeStruct((T, D), BF16), jax.ShapeDtypeStruct((T, D), BF16),
                   jax.ShapeDtypeStruct((T, 2 * D), BF16), jax.ShapeDtypeStruct((T, AW), BF16),
                   jax.ShapeDtypeStruct((T, C), BF16), jax.ShapeDtypeStruct((1, D), F32)],
        compiler_params=_params("arbitrary"),
    )(dx1b, gl, ya, yc, w_out, w_ap, w_cp, dep)


def _conv_bwd(glu, u, dc, conv_w, ln_g, ln_b, dep):
    tm = CONV_TM
    nblk = T // tm

    def du_of(uv, dcv, g_ref, b_ref):
        mu = jnp.mean(uv, axis=-1, keepdims=True)
        xc = uv - mu
        var = jnp.mean(xc * xc, axis=-1, keepdims=True)
        rstd = lax.rsqrt(var + EPS)
        xh = xc * rstd
        y = xh * g_ref[...] + b_ref[...]
        sg = _sigmoid(y)
        dy = dcv * (sg * (1.0 + y * (1.0 - sg)))
        dxh = dy * g_ref[...]
        du = rstd * (dxh - jnp.mean(dxh, axis=-1, keepdims=True)
                     - xh * jnp.mean(dxh * xh, axis=-1, keepdims=True))
        return du, dy, xh

    def body(cur_ref, prev_ref, u_ref, un_ref, dc_ref, dcn_ref, w_ref, g_ref, b_ref, dep_ref,
             dglu_ref, dw_ref, dcb_ref, dg_ref, db_ref, zs_ref, dus_ref):
        i = pl.program_id(0)

        @pl.when(i == 0)
        def _():
            dw_ref[...] = jnp.zeros_like(dw_ref)
            dcb_ref[...] = jnp.zeros_like(dcb_ref)
            dg_ref[...] = jnp.zeros_like(dg_ref)
            db_ref[...] = jnp.zeros_like(db_ref)

        zprev = _glu(prev_ref[tm - HALO:tm, :])
        zs_ref[0, 0:HALO, :] = jnp.where(i > 0, zprev, 0.0)
        zs_ref[0, HALO:HALO + tm, :] = _glu(cur_ref[...])
        _shifted_copies(zs_ref)

        dun, _, _ = du_of(un_ref[0:HALO, :], dcn_ref[0:HALO, :].astype(F32), g_ref, b_ref)
        dus_ref[0, tm:tm + HALO, :] = jnp.where(i < nblk - 1, dun, 0.0)
        dg_acc = jnp.zeros((1, C), F32)
        db_acc = jnp.zeros((1, C), F32)
        dcb_acc = jnp.zeros((1, C), F32)
        for s in range(tm // CONV_SUB):
            rows = slice(s * CONV_SUB, (s + 1) * CONV_SUB)
            du, dy, xh = du_of(u_ref[rows, :], dc_ref[rows, :].astype(F32), g_ref, b_ref)
            dus_ref[0, rows, :] = du
            dg_acc = dg_acc + jnp.sum(dy * xh, axis=0, keepdims=True)
            db_acc = db_acc + jnp.sum(dy, axis=0, keepdims=True)
            dcb_acc = dcb_acc + jnp.sum(du, axis=0, keepdims=True)
        dg_ref[...] += dg_acc
        db_ref[...] += db_acc
        dcb_ref[...] += dcb_acc
        _shifted_copies(dus_ref)

        for j in range(KW):
            acc = jnp.zeros((CONV_SUB, C), F32)
            for s in range(tm // CONV_SUB):
                base = HALO + s * CONV_SUB - (KW - 1) + j
                acc = acc + dus_ref[0, s * CONV_SUB:(s + 1) * CONV_SUB, :] * _shifted_rows(zs_ref, base, CONV_SUB)
            dw_ref[j:j + 1, :] += jnp.sum(acc, axis=0, keepdims=True)

        for s in range(tm // CONV_SUB):
            rows = slice(s * CONV_SUB, (s + 1) * CONV_SUB)
            dz = jnp.zeros((CONV_SUB, C), F32)
            for j in range(KW):
                dz = dz + w_ref[j:j + 1, :] * _shifted_rows(dus_ref, s * CONV_SUB + (KW - 1) - j, CONV_SUB)
            a = cur_ref[rows, 0:C].astype(F32)
            sb = _sigmoid(cur_ref[rows, C:2 * C].astype(F32))
            dglu_ref[rows, 0:C] = (dz * sb).astype(BF16)
            dglu_ref[rows, C:2 * C] = (dz * a * sb * (1.0 - sb)).astype(BF16)

    nxt = lambda i: (jnp.minimum(i + 1, nblk - 1), 0)
    return pl.pallas_call(
        body, name="conv_bwd", grid=(nblk,),
        in_specs=[_rows(tm, 2 * C),
                  pl.BlockSpec((tm, 2 * C), lambda i: (jnp.maximum(i - 1, 0), 0)),
                  _rows(tm, C), pl.BlockSpec((tm, C), nxt),
                  _rows(tm, C), pl.BlockSpec((tm, C), nxt),
                  _whole((KW, C)), _whole((1, C)), _whole((1, C)), _whole((8, 128))],
        out_specs=[_rows(tm, 2 * C), _whole((KW, C)), _whole((1, C)), _whole((1, C)), _whole((1, C))],
        out_shape=[jax.ShapeDtypeStruct((T, 2 * C), BF16), jax.ShapeDtypeStruct((KW, C), F32),
                   jax.ShapeDtypeStruct((1, C), F32), jax.ShapeDtypeStruct((1, C), F32),
                   jax.ShapeDtypeStruct((1, C), F32)],
        scratch_shapes=[pltpu.VMEM((SUBLANES, HALO + tm, C), F32), pltpu.VMEM((SUBLANES, tm + HALO, C), F32)],
        compiler_params=_params("arbitrary"),
    )(glu, glu, u, u, dc, dc, conv_w, ln_g, ln_b, dep)


def _attn_bwd(qkv, do, lse, p, sinks, dep):
    nsub = 1

    def body(sink_ref, qkv_ref, do_ref, lse_ref, p_ref, dep_ref, dq_ref, dkv_ref, ds_ref,
             dp_ref, dsb_ref):
        i = pl.program_id(0)

        @pl.when(i == 0)
        def _():
            dkv_ref[...] = jnp.zeros_like(dkv_ref)
            ds_ref[...] = jnp.zeros_like(ds_ref)

        slots = [(sub, g) for sub in range(nsub) for g in range(NKV)]
        bands = [_band(i * nsub + sub) for sub in range(nsub)]
        r0s = [pl.multiple_of((i * nsub + sub) * BLK, BLK) for sub in range(nsub)]
        lses = []
        for n, (sub, g) in enumerate(slots):
            rb, blk = bands[sub][0], slice(sub * BLK, (sub + 1) * BLK)
            vband = qkv_ref[pl.ds(rb, BAND), AW + KVW + g * HD:AW + KVW + (g + 1) * HD]
            lse_parts = []
            for hh in range(GROUP):
                h = g * GROUP + hh
                dp_ref[n, hh * BLK:(hh + 1) * BLK, :] = _dot_nt(do_ref[blk, h * HD:(h + 1) * HD], vband)
                lse_parts.append(lse_ref[blk, h:h + 1])
            lses.append(jnp.concatenate(lse_parts, axis=0))
        dsinks = []
        for n, (sub, g) in enumerate(slots):
            pf, dpv = p_ref[n].astype(F32), dp_ref[n]
            dl = jnp.sum(pf * dpv, axis=-1, keepdims=True)
            dsb_ref[n] = (pf * (dpv - dl)).astype(BF16)
            dsinks.append(-(jnp.exp(_sink_column(sink_ref, g) - lses[n]) * dl))
        for n, (sub, g) in enumerate(slots):
            rb, blk = bands[sub][0], slice(sub * BLK, (sub + 1) * BLK)
            kband = qkv_ref[pl.ds(rb, BAND), AW + g * HD:AW + (g + 1) * HD]
            dk = jnp.zeros((BAND, HD), F32)
            dv = jnp.zeros((BAND, HD), F32)
            for hh in range(GROUP):
                h = g * GROUP + hh
                hcol = slice(h * HD, (h + 1) * HD)
                rows = slice(hh * BLK, (hh + 1) * BLK)
                dq_ref[blk, hcol] = (_dot(dsb_ref[n, rows, :], kband) * SCALE).astype(BF16)
                dk = dk + _dot_tn(dsb_ref[n, rows, :], qkv_ref[pl.ds(r0s[sub], BLK), hcol])
                dv = dv + _dot_tn(p_ref[n, rows, :], do_ref[blk, hcol])
                ds_ref[h:h + 1, :] += jnp.broadcast_to(
                    jnp.sum(dsinks[n][rows], axis=0, keepdims=True), (1, 128))
            dkv_ref[pl.ds(rb, BAND), g * HD:(g + 1) * HD] += dk * SCALE
            dkv_ref[pl.ds(rb, BAND), KVW + g * HD:KVW + (g + 1) * HD] += dv

    nslot, tq = nsub * NKV, nsub * BLK
    return pl.pallas_call(
        body, name="attn_bwd", grid=(T // tq,),
        in_specs=[pl.BlockSpec(memory_space=pltpu.SMEM), _whole((T, QKVW)),
                  _rows(tq, AW), _rows(tq, NQ),
                  pl.BlockSpec((nslot, GROWS, BAND), lambda i: (i, 0, 0)), _whole((8, 128))],
        out_specs=[_rows(tq, AW), _whole((T, 2 * KVW)), _whole((NQ, 128))],
        out_shape=[jax.ShapeDtypeStruct((T, AW), BF16), jax.ShapeDtypeStruct((T, 2 * KVW), F32),
                   jax.ShapeDtypeStruct((NQ, 128), F32)],
        scratch_shapes=[pltpu.VMEM((nslot, GROWS, BAND), F32), pltpu.VMEM((nslot, GROWS, BAND), BF16)],
        compiler_params=_params("arbitrary"),
    )(sinks, qkv, do, lse, p, dep)


PROJ_PARTS = [(0, AW), (AW, QKVW), (GLU_OFF, GATE_OFF), (GATE_OFF, INW)]


def _in_proj_bwd(dq, dkv, dglu, dgl, x, dx1, g_mix, w_in_t, dep):
    tm = ROW_TM

    def body(dq_ref, dkv_ref, dglu_ref, dgl_ref, x_ref, dx1_ref, g_ref, w_ref, dep_ref,
             gx_ref, dg_ref, db_ref):
        i = pl.program_id(0)

        @pl.when(i == 0)
        def _():
            dg_ref[...] = jnp.zeros_like(dg_ref)
            db_ref[...] = jnp.zeros_like(db_ref)

        dh = jnp.zeros((tm, D), F32)
        for part_ref, (lo, hi) in zip((dq_ref, dkv_ref, dglu_ref, dgl_ref), PROJ_PARTS):
            part = part_ref[...]
            dh = dh + _dot(part.astype(BF16), w_ref[lo:hi, :])
            db_ref[:, lo:hi] += jnp.sum(part.astype(F32), axis=0, keepdims=True)
        xv = x_ref[...]
        r = lax.rsqrt(jnp.mean(xv * xv, axis=-1, keepdims=True) + EPS)
        xh = xv * r
        dhg = dh * g_ref[...]
        gx_ref[...] = dx1_ref[...] + r * (dhg - xh * jnp.mean(dhg * xh, axis=-1, keepdims=True))
        dg_ref[...] += jnp.sum(dh * xh, axis=0, keepdims=True)

    return pl.pallas_call(
        body, name="in_proj_bwd", grid=(T // tm,),
        in_specs=[_rows(tm, AW), _rows(tm, 2 * KVW), _rows(tm, 2 * C), _rows(tm, 2 * D),
                  _rows(tm, D), _rows(tm, D), _whole((1, D)), _const((INW, D)), _whole((8, 128))],
        out_specs=[_rows(tm, D), _whole((1, D)), _whole((1, INW))],
        out_shape=[jax.ShapeDtypeStruct((T, D), F32), jax.ShapeDtypeStruct((1, D), F32),
                   jax.ShapeDtypeStruct((1, INW), F32)],
        compiler_params=_params("arbitrary"),
    )(dq, dkv, dglu, dgl, x, dx1, g_mix, w_in_t, dep)


def _grad_w_in_t(h, dq, dkv, dglu, dgl):
    tn, chunk = 512, 256

    def body(h_ref, dq_ref, dkv_ref, dglu_ref, dgl_ref, o_ref):
        hv = h_ref[...]
        for part_ref, (lo, hi) in zip((dq_ref, dkv_ref, dglu_ref, dgl_ref), PROJ_PARTS):
            for c0 in range(0, hi - lo, chunk):
                o_ref[lo + c0:lo + c0 + chunk, :] = _dot_tn(
                    part_ref[:, c0:c0 + chunk].astype(BF16), hv).astype(BF16)

    return pl.pallas_call(
        body, name="grad_w_in", grid=(D // tn,),
        in_specs=[pl.BlockSpec((T, tn), lambda j: (0, j)), _const((T, AW)), _const((T, 2 * KVW)),
                  _const((T, 2 * C)), _const((T, 2 * D))],
        out_specs=pl.BlockSpec((INW, tn), lambda j: (0, j)),
        out_shape=jax.ShapeDtypeStruct((INW, D), BF16),
        compiler_params=_params("parallel"),
    )(h, dq, dkv, dglu, dgl)


def _grad_w(a, b, name, tk, tn, col_sharded):
    k, n = a.shape[1], b.shape[1]

    single = n == tn
    sw = n // N_CHIPS

    def body(a_ref, b_ref, o_ref, at_ref):
        if single:
            res = _dot_tn(a_ref[...], b_ref[...]).astype(BF16)
            if col_sharded:
                for s in range(N_CHIPS):
                    o_ref[s] = res[:, s * sw:(s + 1) * sw]
            else:
                o_ref[...] = res
            return

        @pl.when(pl.program_id(1) == 0)
        def _():
            at_ref[...] = a_ref[...].T

        o_ref[...] = _dot(at_ref[...], b_ref[...]).astype(BF16)

    if col_sharded and single:
        shape = (N_CHIPS, k, sw)
        out_spec = pl.BlockSpec((N_CHIPS, tk, sw), lambda i, j: (0, i, 0))
    elif col_sharded:
        per = sw // tn
        shape = (N_CHIPS, k, sw)
        out_spec = pl.BlockSpec((None, tk, tn), lambda i, j: (j // per, i, j % per))
    else:
        shape = (1, k, n)
        out_spec = pl.BlockSpec((None, tk, tn), lambda i, j: (0, i, j))
    out = pl.pallas_call(
        body, name=name, grid=(k // tk, n // tn),
        in_specs=[pl.BlockSpec((T, tk), lambda i, j: (0, i)), pl.BlockSpec((T, tn), lambda i, j: (0, j))],
        out_specs=out_spec,
        out_shape=jax.ShapeDtypeStruct(shape, BF16),
        scratch_shapes=[pltpu.VMEM((tk, T), BF16)],
        compiler_params=_params("parallel", "arbitrary"),
    )(a, b)
    return out if col_sharded else out.reshape(N_CHIPS, k // N_CHIPS, n)


HBM_SPEC = pl.BlockSpec(memory_space=pltpu.HBM)


def _place():
    x, y, c = lax.axis_index("x"), lax.axis_index("y"), lax.axis_index("c")
    chips = [(1 - x, y), (x, 1 - y), (1 - x, 1 - y)]
    return x, y, c, chips


SEM_SPEC = pl.BlockSpec(memory_space=pltpu.SEMAPHORE)
ANY_SPEC = pl.BlockSpec(memory_space=pl.ANY)
VMEM_SPEC = pl.BlockSpec(memory_space=pltpu.VMEM)
EFFECT = pltpu.SideEffectType.DATAFLOW_SIDE_EFFECTING


def _gather_ends(src, land, x, y, c, chips):
    kh = src.shape[0] // 2
    s_me = 2 * x + y
    ends = [(src.at[pl.ds(c * kh, kh)], land.at[s_me, pl.ds(c * kh, kh)], (*chip, c)) for chip in chips]
    return ends + [(src, land.at[s_me], (x, y, 1 - c))]


def _reduce_ends(src, land, x, y, c, chips):
    return [(src.at[2 * chip[0] + chip[1]], land.at[j], (*chip, c)) for j, chip in enumerate(chips)]


def _chip_copies(ends, srcs, lands, send_sems, recv_sems, first=0):
    x, y, c, chips = _place()
    copies = []
    for src, land in zip(srcs, lands):
        peers = ends(src, land, x, y, c, chips)
        for s, d, to in peers:
            k = first * len(peers) + len(copies)
            copies.append(pltpu.make_async_remote_copy(
                src_ref=s, dst_ref=d, send_sem=send_sems.at[k], recv_sem=recv_sems.at[k],
                device_id=to, device_id_type=MESH))
    return copies


GATHER_PEERS, REDUCE_PEERS = 4, 3


def _handshake(shake):
    x, y, c, chips = _place()
    peers = ([(x, y, 1 - c)] if shake in ("pair", "both") else []) + (
        [(*chip, c) for chip in chips] if shake in ("chips", "both") else [])
    barrier = pltpu.get_barrier_semaphore()
    for peer in peers:
        pl.semaphore_signal(barrier, inc=1, device_id=peer, device_id_type=MESH)
    pl.semaphore_wait(barrier, len(peers))


def _chip_start(name, ends, peers, srcs, lands, shake=None):
    n = len(srcs)

    def body(*refs):
        if shake is not None:
            _handshake(shake[0])
        copies = _chip_copies(ends, refs[:n], refs[n:2 * n], refs[2 * n], refs[2 * n + 1])
        for cp in copies:
            cp.start()
        token = refs[-1]
        token[...] = jnp.zeros_like(token)

    hbm = lambda a: pltpu.HBM(a.shape, a.dtype)
    res = pl.pallas_call(
        body, name=name,
        out_shape=(pltpu.SemaphoreType.DMA((peers * n,)), pltpu.SemaphoreType.DMA((peers * n,)),
                   *[hbm(a) for a in srcs], *[hbm(a) for a in lands],
                   jax.ShapeDtypeStruct((8, 128), F32)),
        in_specs=[HBM_SPEC] * (2 * n),
        out_specs=(SEM_SPEC, SEM_SPEC, *[HBM_SPEC] * (2 * n), VMEM_SPEC),
        input_output_aliases={i: 2 + i for i in range(2 * n)},
        compiler_params=pltpu.CompilerParams(has_side_effects=EFFECT,
                                             collective_id=None if shake is None else shake[1]),
    )(*[pltpu.with_memory_space_constraint(a, pltpu.HBM) for a in (*srcs, *lands)])
    return res[0], res[1], list(res[2:2 + n]), list(res[2 + n:2 + 2 * n]), res[-1]


def _chip_wait(name, ends, send_sems, recv_sems, srcs, lands, after, first=0):
    n, na = len(srcs), len(after)

    def body(*refs):
        copies = _chip_copies(ends, refs[:n], refs[n:2 * n], refs[2 * n], refs[2 * n + 1], first)
        for cp in copies:
            cp.wait_send()
            cp.wait_recv()

    hbm = lambda a: pltpu.HBM(a.shape, a.dtype)
    res = pl.pallas_call(
        body, name=name,
        out_shape=tuple(hbm(a) for a in (*srcs, *lands)),
        in_specs=[HBM_SPEC] * (2 * n) + [SEM_SPEC, SEM_SPEC] + [ANY_SPEC] * na,
        out_specs=tuple([HBM_SPEC] * (2 * n)),
        input_output_aliases={i: i for i in range(2 * n)},
        compiler_params=pltpu.CompilerParams(has_side_effects=EFFECT),
    )(*srcs, *lands, send_sems, recv_sems, *after)
    return list(res[:n]), list(res[n:])


def _forward_copies(lands, send_sems, recv_sems):
    x, y, c, chips = _place()
    copies = []
    for land in lands:
        kh = land.shape[1] // 2
        for chip in chips:
            blk = land.at[2 * chip[0] + chip[1], pl.ds(c * kh, kh)]
            k = len(copies)
            copies.append(pltpu.make_async_remote_copy(
                src_ref=blk, dst_ref=blk, send_sem=send_sems.at[k], recv_sem=recv_sems.at[k],
                device_id=(x, y, 1 - c), device_id_type=MESH))
    return copies


def _gather_relay(name, send_sems, recv_sems, srcs, lands, after, first, shake):
    n, na = len(srcs), len(after)

    def body(*refs):
        _handshake(shake[0])
        land_refs = refs[n:2 * n]
        for cp in _chip_copies(_gather_ends, refs[:n], land_refs, refs[2 * n], refs[2 * n + 1], first):
            cp.wait_send()
            cp.wait_recv()
        out = refs[2 * n + 2 + na:]
        for cp in _forward_copies(land_refs, out[0], out[1]):
            cp.start()
        out[-1][...] = jnp.zeros_like(out[-1])

    hbm = lambda a: pltpu.HBM(a.shape, a.dtype)
    res = pl.pallas_call(
        body, name=name,
        out_shape=(pltpu.SemaphoreType.DMA((3 * n,)), pltpu.SemaphoreType.DMA((3 * n,)),
                   *[hbm(a) for a in lands], jax.ShapeDtypeStruct((8, 128), F32)),
        in_specs=[HBM_SPEC] * (2 * n) + [SEM_SPEC, SEM_SPEC] + [ANY_SPEC] * na,
        out_specs=(SEM_SPEC, SEM_SPEC, *[HBM_SPEC] * n, VMEM_SPEC),
        input_output_aliases={n + i: 2 + i for i in range(n)},
        compiler_params=pltpu.CompilerParams(has_side_effects=EFFECT, collective_id=shake[1]),
    )(*srcs, *lands, send_sems, recv_sems, *after)
    return res[0], res[1], list(res[2:2 + n]), res[-1]


def _forward_wait(name, send_sems, recv_sems, lands, after):
    n, na = len(lands), len(after)

    def body(*refs):
        for cp in _forward_copies(refs[:n], refs[n], refs[n + 1]):
            cp.wait_send()
            cp.wait_recv()

    hbm = lambda a: pltpu.HBM(a.shape, a.dtype)
    res = pl.pallas_call(
        body, name=name,
        out_shape=tuple(hbm(a) for a in lands),
        in_specs=[HBM_SPEC] * n + [SEM_SPEC, SEM_SPEC] + [ANY_SPEC] * na,
        out_specs=tuple([HBM_SPEC] * n),
        input_output_aliases={i: i for i in range(n)},
        compiler_params=pltpu.CompilerParams(has_side_effects=EFFECT),
    )(*lands, send_sems, recv_sems, *after)
    return list(res)


def _exchange_ends(src, land, x, y, c, chips):
    kh = src.shape[1] // 2
    return [(src.at[:, pl.ds((1 - c) * kh, kh)], land, (x, y, 1 - c))]


def _share_ends(src, land, x, y, c, chips):
    return [(src, land, (x, y, 1 - c))]


def _small_ends(src, land, x, y, c, chips):
    m = src.shape[0]
    rows = land.at[pl.ds((4 * x + 2 * y + c) * m, m)]
    peers = [(x, y, 1 - c)] + [(*chip, c) for chip in chips] + [(*chip, 1 - c) for chip in chips]
    return [(src, rows, to) for to in peers]


PAIR_PEERS, SMALL_PEERS = 1, 7
SHAKES = {"pair_1": ("pair", 0), "pair_2": ("pair", 1), "share_1": ("pair", 2), "share_2": ("pair", 3),
          "chip_1": ("chips", 4), "chip_2": ("chips", 5), "gather_b": ("both", 6),
          "relay_a": ("pair", 7), "relay_b1": ("pair", 8), "relay_b2": ("pair", 9), "relay_b3": ("pair", 10)}


def _row_tile(k):
    for t in (256, 240, 128, 176, 64, 32, 16):
        if k % t == 0:
            return t
    raise ValueError(k)


def _pair_sum_small(c_idx, items, name):
    n = len(items)

    def body(c_ref, *refs):
        for k in range(n):
            g_ref, r_ref, o_ref = refs[2 * k], refs[2 * k + 1], refs[2 * n + k]
            o_ref[...] = (g_ref[...].astype(F32) + r_ref[...].astype(F32)).astype(BF16)

    in_specs, out_specs, out_shape, flat = [], [], [], []
    for g, got in items:
        _, kh, w = got.shape
        in_specs += [pl.BlockSpec((N_CHIPS, kh, w), lambda i, c_ref: (0, c_ref[0], 0)),
                     pl.BlockSpec((N_CHIPS, kh, w), lambda i, c_ref: (0, 0, 0))]
        out_specs.append(pl.BlockSpec((N_CHIPS, kh, w), lambda i, c_ref: (0, 0, 0)))
        out_shape.append(jax.ShapeDtypeStruct((N_CHIPS, kh, w), BF16))
        flat += [g, got]
    return pl.pallas_call(
        body, name=name,
        grid_spec=pltpu.PrefetchScalarGridSpec(num_scalar_prefetch=1, grid=(1,), in_specs=in_specs,
                                               out_specs=out_specs),
        out_shape=out_shape, compiler_params=_params("arbitrary"),
    )(c_idx, *flat)


def _pair_sum(c_idx, g, got, name):
    _, k, n = g.shape
    kh = k // 2
    tm = _row_tile(kh)
    nb = kh // tm

    def body(c_ref, g_ref, r_ref, o_ref):
        o_ref[...] = (g_ref[...].astype(F32) + r_ref[...].astype(F32)).astype(BF16)

    return pl.pallas_call(
        body, name=name,
        grid_spec=pltpu.PrefetchScalarGridSpec(
            num_scalar_prefetch=1, grid=(nb,),
            in_specs=[pl.BlockSpec((N_CHIPS, tm, n), lambda i, c_ref: (0, c_ref[0] * nb + i, 0)),
                      pl.BlockSpec((N_CHIPS, tm, n), lambda i, c_ref: (0, i, 0))],
            out_specs=pl.BlockSpec((N_CHIPS, tm, n), lambda i, c_ref: (0, i, 0))),
        out_shape=jax.ShapeDtypeStruct((N_CHIPS, kh, n), BF16),
        compiler_params=_params("parallel"),
    )(c_idx, g, got)


def _chip_sum_small(s_idx, items, name):
    n = len(items)

    def body(s_ref, *refs):
        for k in range(n):
            m_ref, r_ref, o_ref = refs[2 * k], refs[2 * k + 1], refs[2 * n + k]
            acc = m_ref[0].astype(F32)
            for j in range(3):
                acc = acc + r_ref[j].astype(F32)
            o_ref[...] = acc

    in_specs, out_specs, out_shape, flat = [], [], [], []
    for mine, got in items:
        _, kh, w = mine.shape
        in_specs += [pl.BlockSpec((1, kh, w), lambda i, s_ref: (s_ref[0], 0, 0)),
                     pl.BlockSpec((3, kh, w), lambda i, s_ref: (0, 0, 0))]
        out_specs.append(pl.BlockSpec((kh, w), lambda i, s_ref: (0, 0)))
        out_shape.append(jax.ShapeDtypeStruct((kh, w), F32))
        flat += [mine, got]
    return pl.pallas_call(
        body, name=name,
        grid_spec=pltpu.PrefetchScalarGridSpec(num_scalar_prefetch=1, grid=(1,), in_specs=in_specs,
                                               out_specs=out_specs),
        out_shape=out_shape, compiler_params=_params("arbitrary"),
    )(s_idx, *flat)


def _chip_sum(s_idx, mine, got, name):
    _, kh, n = mine.shape
    tm = _row_tile(kh)

    def body(s_ref, m_ref, r_ref, o_ref):
        acc = m_ref[0].astype(F32)
        for j in range(3):
            acc = acc + r_ref[j].astype(F32)
        o_ref[...] = acc

    return pl.pallas_call(
        body, name=name,
        grid_spec=pltpu.PrefetchScalarGridSpec(
            num_scalar_prefetch=1, grid=(kh // tm,),
            in_specs=[pl.BlockSpec((1, tm, n), lambda i, s_ref: (s_ref[0], i, 0)),
                      pl.BlockSpec((3, tm, n), lambda i, s_ref: (0, i, 0))],
            out_specs=pl.BlockSpec((tm, n), lambda i, s_ref: (i, 0))),
        out_shape=jax.ShapeDtypeStruct((kh, n), F32),
        compiler_params=_params("parallel"),
    )(s_idx, mine, got)


def _adamw_math(w, g, m, v):
    m = ADAM_B1 * m + (1.0 - ADAM_B1) * g
    v = ADAM_B2 * v + (1.0 - ADAM_B2) * (g * g)
    m_hat = m / (1.0 - ADAM_B1 ** ADAM_STEP)
    v_hat = v / (1.0 - ADAM_B2 ** ADAM_STEP)
    delta = -ADAM_LR * (m_hat / (jnp.sqrt(v_hat) + ADAM_EPS) + ADAM_WD * w)
    return delta, m, v


def _adamw_small(c_idx, items, name):
    n = len(items)

    def body(c_ref, *refs):
        mine = pl.program_id(0) == c_ref[0]
        for k in range(n):
            w_ref, gm_ref, go_ref, m_ref, v_ref = refs[5 * k:5 * k + 5]
            g_ref, d_ref, mo_ref, vo_ref = refs[5 * n + 4 * k:5 * n + 4 * k + 4]
            g = jnp.where(mine, gm_ref[...], go_ref[...])
            d, mm, vv = _adamw_math(w_ref[...], g, m_ref[...], v_ref[...])
            g_ref[...] = g
            d_ref[...] = d
            mo_ref[...] = mm
            vo_ref[...] = vv

    in_specs, out_specs, out_shape, flat = [], [], [], []
    for w, g_mine, g_other, m, v in items:
        rows, cols = w.shape
        tm = rows // 4
        full = pl.BlockSpec((tm, cols), lambda h, i, c_ref: (2 * h + i, 0))
        own = pl.BlockSpec((tm, cols), lambda h, i, c_ref: (jnp.where(h == c_ref[0], i, 0), 0))
        other = pl.BlockSpec((tm, cols), lambda h, i, c_ref: (jnp.where(h == c_ref[0], 0, i), 0))
        in_specs += [full, own, other, full, full]
        out_specs += [full] * 4
        out_shape += [jax.ShapeDtypeStruct((rows, cols), F32)] * 4
        flat += [w, g_mine, g_other, m, v]
    res = pl.pallas_call(
        body, name=name,
        grid_spec=pltpu.PrefetchScalarGridSpec(num_scalar_prefetch=1, grid=(2, 2), in_specs=in_specs,
                                               out_specs=out_specs),
        out_shape=out_shape, compiler_params=_params("arbitrary", "arbitrary"),
    )(c_idx, *flat)
    return [tuple(res[4 * k:4 * k + 4]) for k in range(n)]


def _adamw(c_idx, w, g_mine, g_other, m, v, name):
    k, n = w.shape
    tm = k // 4

    def body(c_ref, w_ref, gm_ref, go_ref, m_ref, v_ref, g_ref, d_ref, mo_ref, vo_ref):
        g = jnp.where(pl.program_id(0) == c_ref[0], gm_ref[...], go_ref[...])
        d, mm, vv = _adamw_math(w_ref[...], g, m_ref[...], v_ref[...])
        g_ref[...] = g
        d_ref[...] = d
        mo_ref[...] = mm
        vo_ref[...] = vv

    full = pl.BlockSpec((tm, n), lambda h, i, c_ref: (2 * h + i, 0))
    mine = pl.BlockSpec((tm, n), lambda h, i, c_ref: (jnp.where(h == c_ref[0], i, 0), 0))
    other = pl.BlockSpec((tm, n), lambda h, i, c_ref: (jnp.where(h == c_ref[0], 0, i), 0))
    shp = jax.ShapeDtypeStruct((k, n), F32)
    return pl.pallas_call(
        body, name=name,
        grid_spec=pltpu.PrefetchScalarGridSpec(
            num_scalar_prefetch=1, grid=(2, 2),
            in_specs=[full, mine, other, full, full], out_specs=[full] * 4),
        out_shape=[shp] * 4, compiler_params=_params("arbitrary", "arbitrary"),
    )(c_idx, w, g_mine, g_other, m, v)


VEC_SLOTS = {
    "g_mix_norm": (0, 0, D), "b_conv_proj": (0, D, D), "g_ffn_norm": (0, 2 * D, D),
    "g_final": (0, 3 * D, D), "b_in": (1, 0, INW), "conv_b": (2, 0, C), "ln_g": (2, C, C),
    "ln_b": (2, 2 * C, C), "sinks": (2, 3 * C, NQ), "loss": (2, 3 * C + 128, 1),
}
VEC_ROWS, VEC_COLS = 8, 4 * D
CW_ROWS = 32
SMALL_NAMES = ["g_mix_norm", "b_in", "sinks", "conv_w", "conv_b", "ln_g", "ln_b",
               "b_conv_proj", "g_ffn_norm", "g_final"]
CW_LANES = C // N_CHIPS


def _pack_small(gs, loss):
    row0 = jnp.concatenate([gs["g_mix_norm"], gs["b_conv_proj"], gs["g_ffn_norm"], gs["g_final"]], axis=1)
    row1 = jnp.pad(gs["b_in"], ((0, 0), (0, VEC_COLS - INW)))
    row2 = jnp.concatenate([gs["conv_b"], gs["ln_g"], gs["ln_b"],
                            jnp.pad(gs["sinks"], ((0, 0), (0, 128 - NQ))),
                            jnp.pad(loss.reshape(1, 1), ((0, 0), (0, VEC_COLS - 3 * C - 129)))], axis=1)
    vec = jnp.concatenate([row0, row1, row2, jnp.zeros((VEC_ROWS - 3, VEC_COLS), F32)], axis=0)
    cw = jnp.pad(gs["conv_w"], ((0, CW_ROWS - KW), (0, 0)))
    return vec, cw


def _small_update(idx, vec_own, cw_own, vec_all, cw_all, wmv):
    nsm = len(SMALL_NAMES)

    def body(s_ref, vown_ref, cown_ref, vec_ref, cw_ref, *refs):
        ins = refs[:3 * nsm]
        outs = refs[3 * nsm:7 * nsm]
        loss_ref = refs[7 * nsm]
        me = s_ref[1]

        def summed(own_ref, table_ref, rows_per_dev, r0, nrows, lane, width):
            acc = None
            for k in range(8):
                piece = jnp.where(me == k, own_ref[r0:r0 + nrows, lane:lane + width],
                                  table_ref[k * rows_per_dev + r0:k * rows_per_dev + r0 + nrows, lane:lane + width])
                acc = piece if acc is None else acc + piece
            return acc

        def total(slot):
            row, lane, width = slot
            return summed(vown_ref, vec_ref, VEC_ROWS, row, 1, lane, width)

        loss_ref[...] = jnp.broadcast_to(total(VEC_SLOTS["loss"]), loss_ref.shape)
        for p, name in enumerate(SMALL_NAMES):
            w_ref, m_ref, v_ref = ins[3 * p:3 * p + 3]
            g_ref, d_ref, mo_ref, vo_ref = outs[4 * p:4 * p + 4]
            if name == "conv_w":
                g = jnp.zeros((KW, CW_LANES), F32)
                for s in range(N_CHIPS):
                    cand = summed(cown_ref, cw_ref, CW_ROWS, 0, KW, s * CW_LANES, CW_LANES)
                    g = jnp.where(s_ref[0] == s, cand, g)
            else:
                g = total(VEC_SLOTS[name])
            d, mm, vv = _adamw_math(w_ref[...], g, m_ref[...], v_ref[...])
            g_ref[...] = g
            d_ref[...] = d
            mo_ref[...] = mm
            vo_ref[...] = vv

    vmem = pl.BlockSpec(memory_space=pltpu.VMEM)
    flat = [a for t in wmv for a in t]
    out_shape = []
    for w, _, _ in wmv:
        out_shape += [jax.ShapeDtypeStruct(w.shape, F32)] * 4
    out_shape.append(jax.ShapeDtypeStruct((1, 128), F32))
    res = pl.pallas_call(
        body, name="small_update",
        in_specs=[pl.BlockSpec(memory_space=pltpu.SMEM)] + [vmem] * (4 + len(flat)),
        out_specs=[vmem] * len(out_shape), out_shape=out_shape,
    )(idx, vec_own, cw_own, vec_all, cw_all, *flat)
    return [tuple(res[4 * p:4 * p + 4]) for p in range(nsm)], res[4 * nsm]


SMALL_BIG = ("w_out", "w_attn_proj", "w_conv_proj")
WEIGHT_ORDER = ["g_mix_norm", "w_in", "b_in", "sinks", "conv_w", "conv_b", "ln_g", "ln_b",
                "w_attn_proj", "w_conv_proj", "b_conv_proj", "w_out", "g_ffn_norm", "w_ffn_in",
                "w_ffn_down", "g_final"]


def kernel(x, g_mix_norm, w_in, b_in, sinks, conv_w, conv_b, ln_g, ln_b, w_attn_proj, w_conv_proj, b_conv_proj, w_out, g_ffn_norm, w_ffn_in, w_ffn_down, g_final, loss_target, m_g_mix_norm, m_w_in, m_b_in, m_sinks, m_conv_w, m_conv_b, m_ln_g, m_ln_b, m_w_attn_proj, m_w_conv_proj, m_b_conv_proj, m_w_out, m_g_ffn_norm, m_w_ffn_in, m_w_ffn_down, m_g_final, v_g_mix_norm, v_w_in, v_b_in, v_sinks, v_conv_w, v_conv_b, v_ln_g, v_ln_b, v_w_attn_proj, v_w_conv_proj, v_b_conv_proj, v_w_out, v_g_ffn_norm, v_w_ffn_in, v_w_ffn_down, v_g_final):
    w = dict(g_mix_norm=g_mix_norm, w_in=w_in, b_in=b_in, sinks=sinks, conv_w=conv_w, conv_b=conv_b,
             ln_g=ln_g, ln_b=ln_b, w_attn_proj=w_attn_proj, w_conv_proj=w_conv_proj,
             b_conv_proj=b_conv_proj, w_out=w_out, g_ffn_norm=g_ffn_norm, w_ffn_in=w_ffn_in,
             w_ffn_down=w_ffn_down, g_final=g_final)
    m = dict(g_mix_norm=m_g_mix_norm, w_in=m_w_in, b_in=m_b_in, sinks=m_sinks, conv_w=m_conv_w,
             conv_b=m_conv_b, ln_g=m_ln_g, ln_b=m_ln_b, w_attn_proj=m_w_attn_proj,
             w_conv_proj=m_w_conv_proj, b_conv_proj=m_b_conv_proj, w_out=m_w_out,
             g_ffn_norm=m_g_ffn_norm, w_ffn_in=m_w_ffn_in, w_ffn_down=m_w_ffn_down, g_final=m_g_final)
    v = dict(g_mix_norm=v_g_mix_norm, w_in=v_w_in, b_in=v_b_in, sinks=v_sinks, conv_w=v_conv_w,
             conv_b=v_conv_b, ln_g=v_ln_g, ln_b=v_ln_b, w_attn_proj=v_w_attn_proj,
             w_conv_proj=v_w_conv_proj, b_conv_proj=v_b_conv_proj, w_out=v_w_out,
             g_ffn_norm=v_g_ffn_norm, w_ffn_in=v_w_ffn_in, w_ffn_down=v_w_ffn_down, g_final=v_g_final)

    c_idx = lax.axis_index("c").astype(jnp.int32).reshape(1)
    s_idx = (2 * lax.axis_index("x") + lax.axis_index("y")).astype(jnp.int32).reshape(1)

    out_g, out_d, out_m, out_v = {}, {}, {}, {}

    def gather_start(tag, shards):
        lands = [lax.empty((N_CHIPS,) + s.shape, s.dtype) for s in shards]
        return _chip_start("gather_start_" + tag, _gather_ends, GATHER_PEERS, shards, lands,
                           SHAKES.get("gather_" + tag))

    def gather_relay(tag, state, after, first=0, count=None):
        send_sems, recv_sems, shards, lands, _ = state
        last = len(shards) if count is None else first + count
        return _gather_relay("gather_relay_" + tag, send_sems, recv_sems, shards[first:last],
                             lands[first:last], after, first, SHAKES["relay_" + tag])

    def gather_finish(tag, relay, after):
        return _forward_wait("forward_wait_" + tag, relay[0], relay[1], relay[2], after)

    names_b = ["w_attn_proj", "w_conv_proj", "w_out", "w_ffn_in", "w_ffn_down"]
    big = {name: (w[name][0], m[name][0], v[name][0]) for name in names_b}
    big["w_in"] = (w_in[0].T, m_w_in[0].T, v_w_in[0].T)
    state_a = gather_start("a", [big["w_in"][0].astype(BF16), jnp.pad(conv_w[0], ((0, CW_ROWS - KW), (0, 0)))])
    state_b = gather_start("b", [(big[name][0] + state_a[4][0, 0]).astype(BF16) for name in names_b])
    got_a = gather_finish("a", gather_relay("a", state_a, [state_b[4]]), [])
    w_in_t_full = got_a[0].reshape(INW, D)
    conv_w_full = got_a[1].transpose(1, 0, 2).reshape(CW_ROWS, C)[:KW]

    xs, target = x[0], loss_target[0]
    g_final2 = g_final.reshape(1, D)
    h, qkv, glu, gl = _in_proj(xs, g_mix_norm, w_in_t_full, b_in)
    o, lse, probs = _attn_fwd(qkv, sinks)
    relay_1 = gather_relay("b1", state_b, [o], 0, 3)
    u, cact = _conv_fwd(glu, conv_w_full, conv_b, ln_g, ln_b, relay_1[3])
    w_ap4, w_cp4, w_out4 = gather_finish("b1", relay_1, [cact])
    w_out_full = w_out4.reshape(D, D)
    mixed = _mix_out(xs, o, cact, gl, w_ap4, w_cp4, b_conv_proj, w_out_full, relay_1[3])
    relay_2 = gather_relay("b2", state_b, [mixed[3]], 3, 1)
    ya, yc, mg, x1 = _mix_out(xs, o, cact, gl, w_ap4, w_cp4, b_conv_proj, w_out_full, relay_2[3], mixed)
    w_fi4, = gather_finish("b2", relay_2, [x1])
    h2, gu, act = _ffn_in_first(x1, g_ffn_norm, w_fi4, relay_2[3])
    relay_3 = gather_relay("b3", state_b, [h2], 4, 1)
    gu, act = _ffn_in_second(h2, w_fi4, gu, act, relay_3[3])
    w_dn4, = gather_finish("b3", relay_3, [act])
    w_dn_full = w_dn4.reshape(DFF, D)
    dx2, dx2b, dg_final, loss_part = _ffn_out_loss(x1, act, w_dn_full, g_final2, target)

    def exchange_start(tag, grads):
        lands = [lax.empty((N_CHIPS, g.shape[1] // 2, g.shape[2]), g.dtype) for g in grads]
        return _chip_start("pair_start_" + tag, _exchange_ends, PAIR_PEERS, grads, lands, SHAKES["pair_" + tag])

    def reduce_start(tag, names, exchange, after):
        send_sems, recv_sems, grads, lands, _ = exchange
        grads, from_sibling = _chip_wait("pair_wait_" + tag, _exchange_ends, send_sems, recv_sems, grads, lands, after)
        pair = {name: _pair_sum(c_idx, g, r, "pair_sum_" + name)
                for name, g, r in zip(names, grads, from_sibling) if name not in SMALL_BIG}
        small = [(g, r) for name, g, r in zip(names, grads, from_sibling) if name in SMALL_BIG]
        if small:
            sums = _pair_sum_small(c_idx, small, "pair_sum_small_" + tag)
            pair.update(zip([name for name in names if name in SMALL_BIG], sums))
        pair = [pair[name] for name in names]
        lands = [lax.empty((3,) + p.shape[1:], p.dtype) for p in pair]
        return _chip_start("chip_start_" + tag, _reduce_ends, REDUCE_PEERS, pair, lands, SHAKES["chip_" + tag])

    def reduce_sum(tag, names, state, after):
        send_sems, recv_sems, pair, lands, _ = state
        pair, lands = _chip_wait("chip_wait_" + tag, _reduce_ends, send_sems, recv_sems, pair, lands, after)
        mine = {name: _chip_sum(s_idx, p, r, "chip_sum_" + name)
                for name, p, r in zip(names, pair, lands) if name not in SMALL_BIG}
        small = [(p, r) for name, p, r in zip(names, pair, lands) if name in SMALL_BIG]
        if small:
            sums = _chip_sum_small(s_idx, small, "chip_sum_small_" + tag)
            mine.update(zip([name for name in names if name in SMALL_BIG], sums))
        mine = [mine[name] for name in names]
        others = [lax.empty(a.shape, a.dtype) for a in mine]
        return _chip_start("share_start_" + tag, _share_ends, PAIR_PEERS, mine, others, SHAKES["share_" + tag])

    def reduce_finish(tag, names, share, after):
        send_sems, recv_sems, mine, others, _ = share
        mine, others = _chip_wait("share_wait_" + tag, _share_ends, send_sems, recv_sems, mine, others, after)
        results = {name: _adamw(c_idx, big[name][0], g_mine, g_other, big[name][1], big[name][2], "adamw_" + name)
                   for name, g_mine, g_other in zip(names, mine, others) if name not in SMALL_BIG}
        small = [name for name in names if name in SMALL_BIG]
        if small:
            halves = dict(zip(names, zip(mine, others)))
            items = [(big[name][0], *halves[name], big[name][1], big[name][2]) for name in small]
            results.update(zip(small, _adamw_small(c_idx, items, "adamw_small_" + tag)))
        for name in names:
            res = results[name]
            if name == "w_in":
                res = [a.T for a in res]
            out_g[name], out_d[name], out_m[name], out_v[name] = [a[None] for a in res]

    dgu, dx1, dx1b, dg_ffn = _ffn_bwd(dx2, dx2b, gu, x1, g_ffn_norm, w_dn_full, w_fi4)
    names_1 = ["w_ffn_in", "w_ffn_down", "w_out", "w_attn_proj", "w_conv_proj"]
    grads_1 = [_grad_w(h2, dgu, "grad_w_ffn_in", 512, FSH, True),
               _grad_w(act, dx2b, "grad_w_ffn_down", 256, D, False)]
    dya, dyc, dgl, do, dc, db_cp = _mix_bwd(dx1b, gl, ya, yc, w_out_full, w_ap4, w_cp4, relay_3[3])
    grads_1 += [_grad_w(mg, dx1b, "grad_w_out", 512, D, False),
                _grad_w(o, dya, "grad_w_attn_proj", 512, D, True),
                _grad_w(cact, dyc, "grad_w_conv_proj", 512, D, True)]
    exchange_1 = exchange_start("1", grads_1)
    dq, dkv, dsinks = _attn_bwd(qkv, do, lse, probs, sinks, exchange_1[4])
    state_1 = reduce_start("1", names_1, exchange_1, [dq])
    dglu, dconv_w, dconv_b, dln_g, dln_b = _conv_bwd(glu, u, dc, conv_w_full, ln_g, ln_b, state_1[4])
    names_2 = ["w_in"]
    gw_in_t = _grad_w_in_t(h, dq, dkv, dglu, dgl)
    exchange_2 = exchange_start("2", [gw_in_t.reshape(N_CHIPS, INW // N_CHIPS, D)])
    grad_x, dg_mix, db_in = _in_proj_bwd(dq, dkv, dglu, dgl, xs, dx1, g_mix_norm, w_in_t_full, exchange_2[4])

    gs = {"g_mix_norm": dg_mix, "b_in": db_in, "sinks": dsinks[:, 0].reshape(1, NQ),
          "conv_w": dconv_w, "conv_b": dconv_b, "ln_g": dln_g, "ln_b": dln_b,
          "b_conv_proj": db_cp, "g_ffn_norm": dg_ffn, "g_final": dg_final}
    blocks = list(_pack_small(gs, loss_part[0, 0]))
    tables = [lax.empty((8 * b.shape[0], b.shape[1]), b.dtype) for b in blocks]
    small = _chip_start("small_start", _small_ends, SMALL_PEERS, blocks, tables)

    state_2 = reduce_start("2", names_2, exchange_2, [grad_x, small[4]])
    share_1 = reduce_sum("1", names_1, state_1, [state_2[4]])
    blocks, tables = _chip_wait("small_wait", _small_ends, small[0], small[1], small[2], small[3],
                                [share_1[4]])
    me = (4 * lax.axis_index("x") + 2 * lax.axis_index("y") + lax.axis_index("c")).astype(jnp.int32)

    def view(a, name):
        if name == "conv_w":
            return a[0]
        if name == "g_final":
            return a.reshape(1, D)
        return a

    wmv = [(view(w[name], name), view(m[name], name), view(v[name], name)) for name in SMALL_NAMES]
    small_out, loss_row = _small_update(jnp.concatenate([s_idx, me.reshape(1)]), blocks[0], blocks[1],
                                        tables[0], tables[1], wmv)
    for name, (g, d, mm, vv) in zip(SMALL_NAMES, small_out):
        shape = w[name].shape
        out_g[name], out_d[name], out_m[name], out_v[name] = (
            g.reshape(shape), d.reshape(shape), mm.reshape(shape), vv.reshape(shape))

    reduce_finish("1", names_1, share_1, [loss_row])
    share_2 = reduce_sum("2", names_2, state_2, [out_d["w_conv_proj"]])
    reduce_finish("2", names_2, share_2, [])

    loss = loss_row[0, 0]
    return (loss, grad_x[None], *[out_g[k] for k in WEIGHT_ORDER], *[out_d[k] for k in WEIGHT_ORDER],
            *[out_m[k] for k in WEIGHT_ORDER], *[out_v[k] for k in WEIGHT_ORDER])
```

```python
import functools

import jax
import jax.numpy as jnp
from jax import lax
from jax.experimental import pallas as pl
from jax.experimental.pallas import tpu as pltpu

F32 = jnp.float32
BF16 = jnp.bfloat16

T = 2048
D = 1024
HD = 64
NQ = 8
NKV = 2
GROUP = NQ // NKV
BLK = 128
AW = NQ * HD
KVW = NKV * HD
C = 512
KW = 31
QKVW = AW + 2 * KVW
GLU_OFF = QKVW
GATE_OFF = GLU_OFF + 2 * C
INW = GATE_OFF + 2 * D
DFF = 2816
EPS = 1e-5
NEG = -1e30
SCALE = HD ** -0.5
HALO = 32
N_CHIPS = 4
FSH = 2 * DFF // N_CHIPS

ADAM_LR = 0.001
ADAM_B1 = 0.9
ADAM_B2 = 0.999
ADAM_EPS = 1e-08
ADAM_WD = 0.01
ADAM_STEP = 10

VMEM_LIMIT = 56 * 1024 * 1024
ROW_TM = 512
MESH = pl.DeviceIdType.MESH


def _params(*sem):
    return pltpu.CompilerParams(dimension_semantics=sem, vmem_limit_bytes=VMEM_LIMIT)


def _dot(a, b):
    return jnp.dot(a, b, preferred_element_type=F32)


def _dot_nt(a, b):
    return lax.dot_general(a, b, (((1,), (1,)), ((), ())), preferred_element_type=F32)


def _dot_tn(a, b):
    return lax.dot_general(a, b, (((0,), (0,)), ((), ())), preferred_element_type=F32)


def _sigmoid(v):
    return 1.0 / (1.0 + jnp.exp(-v))


def _rows(tm, n):
    return pl.BlockSpec((tm, n), lambda i: (i, 0))


def _whole(shape):
    return pl.BlockSpec(shape, lambda i: tuple(0 for _ in shape))


def _in_proj(x, g_mix, w_in_t, b_in):
    tm = ROW_TM

    def body(x_ref, g_ref, w_ref, b_ref, h_ref, qkv_ref, glu_ref, gl_ref):
        xv = x_ref[...]
        r = lax.rsqrt(jnp.mean(xv * xv, axis=-1, keepdims=True) + EPS)
        h = (xv * r * g_ref[...]).astype(BF16)
        h_ref[...] = h
        qkv_ref[...] = (_dot_nt(h, w_ref[0:GLU_OFF, :]) + b_ref[:, 0:GLU_OFF]).astype(BF16)
        glu_ref[...] = (_dot_nt(h, w_ref[GLU_OFF:GATE_OFF, :]) + b_ref[:, GLU_OFF:GATE_OFF]).astype(BF16)
        gl_ref[...] = (_dot_nt(h, w_ref[GATE_OFF:INW, :]) + b_ref[:, GATE_OFF:INW]).astype(BF16)

    return pl.pallas_call(
        body, name="in_proj", grid=(T // tm,),
        in_specs=[_rows(tm, D), _whole((1, D)), _whole((INW, D)), _whole((1, INW))],
        out_specs=[_rows(tm, D), _rows(tm, QKVW), _rows(tm, 2 * C), _rows(tm, 2 * D)],
        out_shape=[jax.ShapeDtypeStruct((T, D), BF16), jax.ShapeDtypeStruct((T, QKVW), BF16),
                   jax.ShapeDtypeStruct((T, 2 * C), BF16), jax.ShapeDtypeStruct((T, 2 * D), BF16)],
        compiler_params=_params("parallel"),
    )(x, g_mix, w_in_t, b_in)


GROWS = GROUP * BLK
BAND = 2 * BLK
ATT_SUB = 2


def _band(i):
    rb = pl.multiple_of(jnp.maximum(i - 1, 0) * BLK, BLK)
    row = lax.broadcasted_iota(jnp.int32, (GROWS, BAND), 0)
    kpos = rb + lax.broadcasted_iota(jnp.int32, (GROWS, BAND), 1)
    qpos = i * BLK + jnp.bitwise_and(row, BLK - 1)
    return rb, jnp.logical_and(kpos <= qpos, kpos > qpos - BLK)


def _sink_column(sink_ref, g):
    head = lax.shift_right_logical(lax.broadcasted_iota(jnp.int32, (GROWS, 1), 0), 7)
    col = jnp.full((GROWS, 1), sink_ref[0, g * GROUP], F32)
    for hh in range(1, GROUP):
        col = jnp.where(head == hh, sink_ref[0, g * GROUP + hh], col)
    return col


def _attn_fwd(qkv, sinks):
    def body(sink_ref, qkv_ref, o_ref, lse_ref, p_ref, s_ref):
        slots = [(sub, g) for sub in range(ATT_SUB) for g in range(NKV)]
        bands = [_band(pl.program_id(0) * ATT_SUB + sub) for sub in range(ATT_SUB)]
        for n, (sub, g) in enumerate(slots):
            rb = bands[sub][0]
            r0 = pl.multiple_of((pl.program_id(0) * ATT_SUB + sub) * BLK, BLK)
            kband = qkv_ref[pl.ds(rb, BAND), AW + g * HD:AW + (g + 1) * HD]
            for hh in range(GROUP):
                h = g * GROUP + hh
                s_ref[n, hh * BLK:(hh + 1) * BLK, :] = _dot_nt(
                    qkv_ref[pl.ds(r0, BLK), h * HD:(h + 1) * HD], kband)
        lses = []
        for n, (sub, g) in enumerate(slots):
            s = jnp.where(bands[sub][1], s_ref[n] * SCALE, NEG)
            sink = _sink_column(sink_ref, g)
            m = jnp.maximum(jnp.max(s, axis=-1, keepdims=True), sink)
            p = jnp.exp(s - m)
            den = jnp.sum(p, axis=-1, keepdims=True) + jnp.exp(sink - m)
            p_ref[n] = (p * (1.0 / den)).astype(BF16)
            lses.append(m + jnp.log(den))
        for n, (sub, g) in enumerate(slots):
            rb = bands[sub][0]
            rows = slice(sub * BLK, (sub + 1) * BLK)
            vband = qkv_ref[pl.ds(rb, BAND), AW + KVW + g * HD:AW + KVW + (g + 1) * HD]
            for hh in range(GROUP):
                h = g * GROUP + hh
                o_ref[rows, h * HD:(h + 1) * HD] = _dot(p_ref[n, hh * BLK:(hh + 1) * BLK, :], vband).astype(BF16)
                lse_ref[rows, h:h + 1] = lses[n][hh * BLK:(hh + 1) * BLK]

    nslot = ATT_SUB * NKV
    return pl.pallas_call(
        body, name="attn_fwd", grid=(T // (ATT_SUB * BLK),),
        in_specs=[pl.BlockSpec(memory_space=pltpu.SMEM), _whole((T, QKVW))],
        out_specs=[_rows(ATT_SUB * BLK, AW), _rows(ATT_SUB * BLK, NQ),
                   pl.BlockSpec((nslot, GROWS, BAND), lambda i: (i, 0, 0))],
        out_shape=[jax.ShapeDtypeStruct((T, AW), BF16), jax.ShapeDtypeStruct((T, NQ), F32),
                   jax.ShapeDtypeStruct((T // BLK * NKV, GROWS, BAND), BF16)],
        scratch_shapes=[pltpu.VMEM((nslot, GROWS, BAND), F32)],
        compiler_params=_params("parallel"),
    )(sinks, qkv)


CONV_TM = 256
CONV_SUB = 32


def _glu(ab):
    a = ab[:, 0:C].astype(F32)
    b = ab[:, C:2 * C].astype(F32)
    return a * _sigmoid(b)


SUBLANES = 8


def _shifted_copies(ref):
    rows = ref.shape[1] - SUBLANES
    for r in range(1, SUBLANES):
        ref[r, 0:rows, :] = ref[0, r:r + rows, :]


def _shifted_rows(ref, start, size):
    r = start % SUBLANES
    return ref[r, start - r:start - r + size, :]


def _conv_fwd(glu, conv_w, conv_b, ln_g, ln_b, dep):
    tm = CONV_TM

    def body(cur_ref, prev_ref, w_ref, cb_ref, g_ref, b_ref, dep_ref, u_ref, c_ref, zs_ref):
        i = pl.program_id(0)
        zprev = _glu(prev_ref[tm - HALO:tm, :])
        zs_ref[0, 0:HALO, :] = jnp.where(i > 0, zprev, 0.0)
        zs_ref[0, HALO:HALO + tm, :] = _glu(cur_ref[...])
        _shifted_copies(zs_ref)
        for s in range(tm // CONV_SUB):
            base = HALO + s * CONV_SUB - (KW - 1)
            acc = jnp.broadcast_to(cb_ref[...], (CONV_SUB, C))
            for j in range(KW):
                acc = acc + w_ref[j:j + 1, :] * _shifted_rows(zs_ref, base + j, CONV_SUB)
            rows = slice(s * CONV_SUB, (s + 1) * CONV_SUB)
            u_ref[rows, :] = acc
            mu = jnp.mean(acc, axis=-1, keepdims=True)
            xc = acc - mu
            var = jnp.mean(xc * xc, axis=-1, keepdims=True)
            y = xc * lax.rsqrt(var + EPS) * g_ref[...] + b_ref[...]
            c_ref[rows, :] = (y * _sigmoid(y)).astype(BF16)

    return pl.pallas_call(
        body, name="conv_fwd", grid=(T // tm,),
        in_specs=[_rows(tm, 2 * C),
                  pl.BlockSpec((tm, 2 * C), lambda i: (jnp.maximum(i - 1, 0), 0)),
                  _whole((KW, C)), _whole((1, C)), _whole((1, C)), _whole((1, C)), _whole((8, 128))],
        out_specs=[_rows(tm, C), _rows(tm, C)],
        out_shape=[jax.ShapeDtypeStruct((T, C), F32), jax.ShapeDtypeStruct((T, C), BF16)],
        scratch_shapes=[pltpu.VMEM((SUBLANES, HALO + tm, C), F32)],
        compiler_params=_params("parallel"),
    )(glu, glu, conv_w, conv_b, ln_g, ln_b, dep)


def _branch_outputs(o, cact, wap_ref, wcp_ref, bcp_ref):
    ya = jnp.concatenate([_dot(o, wap_ref[s]) for s in range(N_CHIPS)], axis=1)
    yc = jnp.concatenate([_dot(cact, wcp_ref[s]) for s in range(N_CHIPS)], axis=1) + bcp_ref[...]
    return ya, yc


def _mix_out(x, o, cact, gl, w_ap, w_cp, b_cp, w_out, dep, begun=None):
    tm = ROW_TM
    steps = T // tm // 2
    first = 0 if begun is None else steps
    rows = lambda n: pl.BlockSpec((tm, n), lambda i: (i + first, 0))
    extra = [] if begun is None else list(begun)

    def body(x_ref, o_ref, c_ref, gl_ref, wap_ref, wcp_ref, bcp_ref, wo_ref, dep_ref, *rest):
        ya_ref, yc_ref, mg_ref, x1_ref = rest[len(extra):]
        ya, yc = _branch_outputs(o_ref[...], c_ref[...], wap_ref, wcp_ref, bcp_ref)
        g0 = _sigmoid(gl_ref[:, 0:D].astype(F32))
        g1 = _sigmoid(gl_ref[:, D:2 * D].astype(F32))
        mg = (g0 * ya + g1 * yc).astype(BF16)
        ya_ref[...] = ya.astype(BF16)
        yc_ref[...] = yc.astype(BF16)
        mg_ref[...] = mg
        x1_ref[...] = x_ref[...] + _dot(mg, wo_ref[...])

    return pl.pallas_call(
        body, name="mix_out_first" if begun is None else "mix_out_second", grid=(steps,),
        in_specs=[rows(D), rows(AW), rows(C), rows(2 * D),
                  _whole((N_CHIPS, AW, D // N_CHIPS)), _whole((N_CHIPS, C, D // N_CHIPS)), _whole((1, D)),
                  _whole((D, D)), _whole((8, 128))] + [ANY_SPEC] * len(extra),
        out_specs=[rows(D), rows(D), rows(D), rows(D)],
        out_shape=[jax.ShapeDtypeStruct((T, D), BF16), jax.ShapeDtypeStruct((T, D), BF16),
                   jax.ShapeDtypeStruct((T, D), BF16), jax.ShapeDtypeStruct((T, D), F32)],
        input_output_aliases={9 + k: k for k in range(len(extra))},
        compiler_params=_params("parallel"),
    )(x, o, cact, gl, w_ap, w_cp, b_cp, w_out, dep, *extra)


def _swiglu_half(h, wg_ref, wu_ref, gu_ref, act_ref):
    gate = _dot(h, wg_ref[0])
    up = _dot(h, wu_ref[0])
    gu_ref[:, 0:FSH] = gate.astype(BF16)
    gu_ref[:, FSH:2 * FSH] = up.astype(BF16)
    act_ref[...] = (gate * _sigmoid(gate) * up).astype(BF16)


def _shard_spec(k):
    return pl.BlockSpec((1, D, FSH), lambda i: (k, 0, 0), pipeline_mode=pl.Buffered(1))


def _ffn_in_first(x1, g_ffn, w_fi, dep):
    tm = ROW_TM

    def body(x_ref, g_ref, wg_ref, wu_ref, dep_ref, h_ref, gu_ref, act_ref):
        xv = x_ref[...]
        r = lax.rsqrt(jnp.mean(xv * xv, axis=-1, keepdims=True) + EPS)
        h = (xv * r * g_ref[...]).astype(BF16)
        h_ref[...] = h
        _swiglu_half(h, wg_ref, wu_ref, gu_ref, act_ref)

    return pl.pallas_call(
        body, name="ffn_in_first", grid=(T // tm,),
        in_specs=[_rows(tm, D), _whole((1, D)), _shard_spec(0), _shard_spec(2), _whole((8, 128))],
        out_specs=[_rows(tm, D), pl.BlockSpec((tm, 2 * FSH), lambda i: (i, 0)),
                   pl.BlockSpec((tm, FSH), lambda i: (i, 0))],
        out_shape=[jax.ShapeDtypeStruct((T, D), BF16), jax.ShapeDtypeStruct((T, 2 * DFF), BF16),
                   jax.ShapeDtypeStruct((T, DFF), BF16)],
        compiler_params=_params("parallel"),
    )(x1, g_ffn, w_fi, w_fi, dep)


def _ffn_in_second(h2, w_fi, gu, act, dep):
    tm = ROW_TM

    def body(h_ref, wg_ref, wu_ref, gu_in, act_in, dep_ref, gu_ref, act_ref):
        _swiglu_half(h_ref[...], wg_ref, wu_ref, gu_ref, act_ref)

    return pl.pallas_call(
        body, name="ffn_in_second", grid=(T // tm,),
        in_specs=[_rows(tm, D), _shard_spec(1), _shard_spec(3), ANY_SPEC, ANY_SPEC, _whole((8, 128))],
        out_specs=[pl.BlockSpec((tm, 2 * FSH), lambda i: (i, 1)), pl.BlockSpec((tm, FSH), lambda i: (i, 1))],
        out_shape=[jax.ShapeDtypeStruct((T, 2 * DFF), BF16), jax.ShapeDtypeStruct((T, DFF), BF16)],
        input_output_aliases={3: 0, 4: 1},
        compiler_params=_params("parallel"),
    )(h2, w_fi, w_fi, gu, act, dep)


def _ffn_out_loss(x1, act, w_dn, g_final, target):
    tm = ROW_TM // 2

    def body(x_ref, a_ref, w_ref, g_ref, t_ref, dx_ref, dxb_ref, dg_ref, loss_ref):
        i = pl.program_id(0)
        x2 = x_ref[...] + _dot(a_ref[...], w_ref[...])
        r = lax.rsqrt(jnp.mean(x2 * x2, axis=-1, keepdims=True) + EPS)
        xh = x2 * r
        g = g_ref[...]
        err = xh * g - t_ref[...]
        dy = err * (1.0 / D)
        dyg = dy * g
        dx = r * (dyg - xh * jnp.mean(dyg * xh, axis=-1, keepdims=True))
        dx_ref[...] = dx
        dxb_ref[...] = dx.astype(BF16)
        part = 0.5 * jnp.sum(jnp.mean(err * err, axis=-1, keepdims=True), axis=0, keepdims=True)

        @pl.when(i == 0)
        def _():
            dg_ref[...] = jnp.zeros_like(dg_ref)
            loss_ref[...] = jnp.zeros_like(loss_ref)

        dg_ref[...] += jnp.sum(dy * xh, axis=0, keepdims=True)
        loss_ref[...] += jnp.broadcast_to(part, loss_ref.shape)

    return pl.pallas_call(
        body, name="ffn_out_loss", grid=(T // tm,),
        in_specs=[_rows(tm, D), _rows(tm, DFF), _whole((DFF, D)), _whole((1, D)), _rows(tm, D)],
        out_specs=[_rows(tm, D), _rows(tm, D), _whole((1, D)), _whole((1, 128))],
        out_shape=[jax.ShapeDtypeStruct((T, D), F32), jax.ShapeDtypeStruct((T, D), BF16),
                   jax.ShapeDtypeStruct((1, D), F32), jax.ShapeDtypeStruct((1, 128), F32)],
        compiler_params=_params("arbitrary"),
    )(x1, act, w_dn, g_final, target)


def _const(shape):
    return pl.BlockSpec(shape, lambda i: tuple(0 for _ in shape), pipeline_mode=pl.Buffered(1))


def _ffn_bwd(dx2, dx2b, gu, x1, g_ffn, w_dn_t, w_fi_t):
    tm = ROW_TM // 2

    def body(dx_ref, dxb_ref, gu_ref, x_ref, g_ref, wdn_ref, wfi_ref,
             dgu_ref, dx1_ref, dx1b_ref, dg_ref):
        i = pl.program_id(0)
        dxb = dxb_ref[...]
        dh = jnp.zeros((tm, D), F32)
        dacts = [_dot_nt(dxb, wdn_ref[k * FSH:(k + 1) * FSH, :]) for k in range(N_CHIPS // 2)]
        for k in range(N_CHIPS // 2):
            c0 = k * FSH
            dact = dacts[k]
            gate = gu_ref[:, 2 * c0:2 * c0 + FSH].astype(F32)
            up = gu_ref[:, 2 * c0 + FSH:2 * c0 + 2 * FSH].astype(F32)
            s = _sigmoid(gate)
            dup = (dact * gate * s).astype(BF16)
            dgate = (dact * up * s * (1.0 + gate * (1.0 - s))).astype(BF16)
            dgu_ref[:, c0:c0 + FSH] = dgate
            dgu_ref[:, DFF + c0:DFF + c0 + FSH] = dup
            dh = dh + _dot_nt(dgate, wfi_ref[k]) + _dot_nt(dup, wfi_ref[k + N_CHIPS // 2])
        xv = x_ref[...]
        r = lax.rsqrt(jnp.mean(xv * xv, axis=-1, keepdims=True) + EPS)
        xh = xv * r
        dhg = dh * g_ref[...]
        dx1 = dx_ref[...] + r * (dhg - xh * jnp.mean(dhg * xh, axis=-1, keepdims=True))
        dx1_ref[...] = dx1
        dx1b_ref[...] = dx1.astype(BF16)

        @pl.when(i == 0)
        def _():
            dg_ref[...] = jnp.zeros_like(dg_ref)

        dg_ref[...] += jnp.sum(dh * xh, axis=0, keepdims=True)

    return pl.pallas_call(
        body, name="ffn_bwd", grid=(T // tm,),
        in_specs=[_rows(tm, D), _rows(tm, D), _rows(tm, 2 * DFF), _rows(tm, D), _whole((1, D)),
                  _const((DFF, D)), _const((N_CHIPS, D, FSH))],
        out_specs=[_rows(tm, 2 * DFF), _rows(tm, D), _rows(tm, D), _whole((1, D))],
        out_shape=[jax.ShapeDtypeStruct((T, 2 * DFF), BF16), jax.ShapeDtypeStruct((T, D), F32),
                   jax.ShapeDtypeStruct((T, D), BF16), jax.ShapeDtypeStruct((1, D), F32)],
        compiler_params=_params("arbitrary"),
    )(dx2, dx2b, gu, x1, g_ffn, w_dn_t, w_fi_t)


def _mix_bwd(dx1b, gl, ya, yc, w_out, w_ap, w_cp, dep):
    tm = ROW_TM

    def body(dx_ref, gl_ref, ya_ref, yc_ref, wo_ref, wap_ref, wcp_ref, dep_ref,
             dya_ref, dyc_ref, dgl_ref, do_ref, dc_ref, db_ref):
        i = pl.program_id(0)
        dm = _dot_nt(dx_ref[...], wo_ref[...])
        ya, yc = ya_ref[...].astype(F32), yc_ref[...].astype(F32)
        g0 = _sigmoid(gl_ref[:, 0:D].astype(F32))
        g1 = _sigmoid(gl_ref[:, D:2 * D].astype(F32))
        dya = dm * g0
        dyc = dm * g1
        dgl_ref[:, 0:D] = (dya * ya * (1.0 - g0)).astype(BF16)
        dgl_ref[:, D:2 * D] = (dyc * yc * (1.0 - g1)).astype(BF16)
        dyab = dya.astype(BF16)
        dycb = dyc.astype(BF16)
        dya_ref[...] = dyab
        dyc_ref[...] = dycb
        sw = D // N_CHIPS
        do = jnp.zeros((tm, AW), F32)
        dcv = jnp.zeros((tm, C), F32)
        for s in range(N_CHIPS):
            do = do + _dot_nt(dyab[:, s * sw:(s + 1) * sw], wap_ref[s])
            dcv = dcv + _dot_nt(dycb[:, s * sw:(s + 1) * sw], wcp_ref[s])
        do_ref[...] = do.astype(BF16)
        dc_ref[...] = dcv.astype(BF16)

        @pl.when(i == 0)
        def _():
            db_ref[...] = jnp.zeros_like(db_ref)

        db_ref[...] += jnp.sum(dyc, axis=0, keepdims=True)

    return pl.pallas_call(
        body, name="mix_bwd", grid=(T // tm,),
        in_specs=[_rows(tm, D), _rows(tm, 2 * D), _rows(tm, D), _rows(tm, D),
                  _whole((D, D)), _whole((N_CHIPS, AW, D // N_CHIPS)), _whole((N_CHIPS, C, D // N_CHIPS)),
                  _whole((8, 128))],
        out_specs=[_rows(tm, D), _rows(tm, D), _rows(tm, 2 * D), _rows(tm, AW), _rows(tm, C),
                   _whole((1, D))],
        out_shape=[jax.ShapeDtypeStruct((T, D), BF16), jax.ShapeDtypeStruct((T, D), BF16),
                   jax.ShapeDtypeStruct((T, 2 * D), BF16), jax.ShapeDtypeStruct((T, AW), BF16),
                   jax.ShapeDtypeStruct((T, C), BF16), jax.ShapeDtypeStruct((1, D), F32)],
        compiler_params=_params("arbitrary"),
    )(dx1b, gl, ya, yc, w_out, w_ap, w_cp, dep)


def _conv_bwd(glu, u, dc, conv_w, ln_g, ln_b, dep):
    tm = CONV_TM
    nblk = T // tm

    def du_of(uv, dcv, g_ref, b_ref):
        mu = jnp.mean(uv, axis=-1, keepdims=True)
        xc = uv - mu
        var = jnp.mean(xc * xc, axis=-1, keepdims=True)
        rstd = lax.rsqrt(var + EPS)
        xh = xc * rstd
        y = xh * g_ref[...] + b_ref[...]
        sg = _sigmoid(y)
        dy = dcv * (sg * (1.0 + y * (1.0 - sg)))
        dxh = dy * g_ref[...]
        du = rstd * (dxh - jnp.mean(dxh, axis=-1, keepdims=True)
                     - xh * jnp.mean(dxh * xh, axis=-1, keepdims=True))
        return du, dy, xh

    def body(cur_ref, prev_ref, u_ref, un_ref, dc_ref, dcn_ref, w_ref, g_ref, b_ref, dep_ref,
             dglu_ref, dw_ref, dcb_ref, dg_ref, db_ref, zs_ref, dus_ref):
        i = pl.program_id(0)

        @pl.when(i == 0)
        def _():
            dw_ref[...] = jnp.zeros_like(dw_ref)
            dcb_ref[...] = jnp.zeros_like(dcb_ref)
            dg_ref[...] = jnp.zeros_like(dg_ref)
            db_ref[...] = jnp.zeros_like(db_ref)

        zprev = _glu(prev_ref[tm - HALO:tm, :])
        zs_ref[0, 0:HALO, :] = jnp.where(i > 0, zprev, 0.0)
        zs_ref[0, HALO:HALO + tm, :] = _glu(cur_ref[...])
        _shifted_copies(zs_ref)

        dun, _, _ = du_of(un_ref[0:HALO, :], dcn_ref[0:HALO, :].astype(F32), g_ref, b_ref)
        dus_ref[0, tm:tm + HALO, :] = jnp.where(i < nblk - 1, dun, 0.0)
        dg_acc = jnp.zeros((1, C), F32)
        db_acc = jnp.zeros((1, C), F32)
        dcb_acc = jnp.zeros((1, C), F32)
        for s in range(tm // CONV_SUB):
            rows = slice(s * CONV_SUB, (s + 1) * CONV_SUB)
            du, dy, xh = du_of(u_ref[rows, :], dc_ref[rows, :].astype(F32), g_ref, b_ref)
            dus_ref[0, rows, :] = du
            dg_acc = dg_acc + jnp.sum(dy * xh, axis=0, keepdims=True)
            db_acc = db_acc + jnp.sum(dy, axis=0, keepdims=True)
            dcb_acc = dcb_acc + jnp.sum(du, axis=0, keepdims=True)
        dg_ref[...] += dg_acc
        db_ref[...] += db_acc
        dcb_ref[...] += dcb_acc
        _shifted_copies(dus_ref)

        for j in range(KW):
            acc = jnp.zeros((CONV_SUB, C), F32)
            for s in range(tm // CONV_SUB):
                base = HALO + s * CONV_SUB - (KW - 1) + j
                acc = acc + dus_ref[0, s * CONV_SUB:(s + 1) * CONV_SUB, :] * _shifted_rows(zs_ref, base, CONV_SUB)
            dw_ref[j:j + 1, :] += jnp.sum(acc, axis=0, keepdims=True)

        for s in range(tm // CONV_SUB):
            rows = slice(s * CONV_SUB, (s + 1) * CONV_SUB)
            dz = jnp.zeros((CONV_SUB, C), F32)
            for j in range(KW):
                dz = dz + w_ref[j:j + 1, :] * _shifted_rows(dus_ref, s * CONV_SUB + (KW - 1) - j, CONV_SUB)
            a = cur_ref[rows, 0:C].astype(F32)
            sb = _sigmoid(cur_ref[rows, C:2 * C].astype(F32))
            dglu_ref[rows, 0:C] = (dz * sb).astype(BF16)
            dglu_ref[rows, C:2 * C] = (dz * a * sb * (1.0 - sb)).astype(BF16)

    nxt = lambda i: (jnp.minimum(i + 1, nblk - 1), 0)
    return pl.pallas_call(
        body, name="conv_bwd", grid=(nblk,),
        in_specs=[_rows(tm, 2 * C),
                  pl.BlockSpec((tm, 2 * C), lambda i: (jnp.maximum(i - 1, 0), 0)),
                  _rows(tm, C), pl.BlockSpec((tm, C), nxt),
                  _rows(tm, C), pl.BlockSpec((tm, C), nxt),
                  _whole((KW, C)), _whole((1, C)), _whole((1, C)), _whole((8, 128))],
        out_specs=[_rows(tm, 2 * C), _whole((KW, C)), _whole((1, C)), _whole((1, C)), _whole((1, C))],
        out_shape=[jax.ShapeDtypeStruct((T, 2 * C), BF16), jax.ShapeDtypeStruct((KW, C), F32),
                   jax.ShapeDtypeStruct((1, C), F32), jax.ShapeDtypeStruct((1, C), F32),
                   jax.ShapeDtypeStruct((1, C), F32)],
        scratch_shapes=[pltpu.VMEM((SUBLANES, HALO + tm, C), F32), pltpu.VMEM((SUBLANES, tm + HALO, C), F32)],
        compiler_params=_params("arbitrary"),
    )(glu, glu, u, u, dc, dc, conv_w, ln_g, ln_b, dep)


def _attn_bwd(qkv, do, lse, p, sinks, *deps):
    nsub = 1

    def body(sink_ref, qkv_ref, do_ref, lse_ref, p_ref, *rest):
        dq_ref, dkv_ref, ds_ref, dp_ref, dsb_ref = rest[len(deps):]
        i = pl.program_id(0)

        @pl.when(i == 0)
        def _():
            dkv_ref[...] = jnp.zeros_like(dkv_ref)
            ds_ref[...] = jnp.zeros_like(ds_ref)

        slots = [(sub, g) for sub in range(nsub) for g in range(NKV)]
        bands = [_band(i * nsub + sub) for sub in range(nsub)]
        r0s = [pl.multiple_of((i * nsub + sub) * BLK, BLK) for sub in range(nsub)]
        lses = []
        for n, (sub, g) in enumerate(slots):
            rb, blk = bands[sub][0], slice(sub * BLK, (sub + 1) * BLK)
            vband = qkv_ref[pl.ds(rb, BAND), AW + KVW + g * HD:AW + KVW + (g + 1) * HD]
            lse_parts = []
            for hh in range(GROUP):
                h = g * GROUP + hh
                dp_ref[n, hh * BLK:(hh + 1) * BLK, :] = _dot_nt(do_ref[blk, h * HD:(h + 1) * HD], vband)
                lse_parts.append(lse_ref[blk, h:h + 1])
            lses.append(jnp.concatenate(lse_parts, axis=0))
        dsinks = []
        for n, (sub, g) in enumerate(slots):
            pf, dpv = p_ref[n].astype(F32), dp_ref[n]
            dl = jnp.sum(pf * dpv, axis=-1, keepdims=True)
            dsb_ref[n] = (pf * (dpv - dl)).astype(BF16)
            dsinks.append(-(jnp.exp(_sink_column(sink_ref, g) - lses[n]) * dl))
        for n, (sub, g) in enumerate(slots):
            rb, blk = bands[sub][0], slice(sub * BLK, (sub + 1) * BLK)
            kband = qkv_ref[pl.ds(rb, BAND), AW + g * HD:AW + (g + 1) * HD]
            dk = jnp.zeros((BAND, HD), F32)
            dv = jnp.zeros((BAND, HD), F32)
            for hh in range(GROUP):
                h = g * GROUP + hh
                hcol = slice(h * HD, (h + 1) * HD)
                rows = slice(hh * BLK, (hh + 1) * BLK)
                dq_ref[blk, hcol] = (_dot(dsb_ref[n, rows, :], kband) * SCALE).astype(BF16)
                dk = dk + _dot_tn(dsb_ref[n, rows, :], qkv_ref[pl.ds(r0s[sub], BLK), hcol])
                dv = dv + _dot_tn(p_ref[n, rows, :], do_ref[blk, hcol])
                ds_ref[h:h + 1, :] += jnp.broadcast_to(
                    jnp.sum(dsinks[n][rows], axis=0, keepdims=True), (1, 128))
            dkv_ref[pl.ds(rb, BAND), g * HD:(g + 1) * HD] += dk * SCALE
            dkv_ref[pl.ds(rb, BAND), KVW + g * HD:KVW + (g + 1) * HD] += dv

    nslot, tq = nsub * NKV, nsub * BLK
    return pl.pallas_call(
        body, name="attn_bwd", grid=(T // tq,),
        in_specs=[pl.BlockSpec(memory_space=pltpu.SMEM), _whole((T, QKVW)),
                  _rows(tq, AW), _rows(tq, NQ),
                  pl.BlockSpec((nslot, GROWS, BAND), lambda i: (i, 0, 0))] + [_whole((8, 128))] * len(deps),
        out_specs=[_rows(tq, AW), _whole((T, 2 * KVW)), _whole((NQ, 128))],
        out_shape=[jax.ShapeDtypeStruct((T, AW), BF16), jax.ShapeDtypeStruct((T, 2 * KVW), F32),
                   jax.ShapeDtypeStruct((NQ, 128), F32)],
        scratch_shapes=[pltpu.VMEM((nslot, GROWS, BAND), F32), pltpu.VMEM((nslot, GROWS, BAND), BF16)],
        compiler_params=_params("arbitrary"),
    )(sinks, qkv, do, lse, p, *deps)


PROJ_PARTS = [(0, AW), (AW, QKVW), (GLU_OFF, GATE_OFF), (GATE_OFF, INW)]


def _in_proj_bwd(dq, dkv, dglu, dgl, x, dx1, g_mix, w_in_t, dep):
    tm = ROW_TM

    def body(dq_ref, dkv_ref, dglu_ref, dgl_ref, x_ref, dx1_ref, g_ref, w_ref, dep_ref,
             gx_ref, dg_ref, db_ref):
        i = pl.program_id(0)

        @pl.when(i == 0)
        def _():
            dg_ref[...] = jnp.zeros_like(dg_ref)
            db_ref[...] = jnp.zeros_like(db_ref)

        dh = jnp.zeros((tm, D), F32)
        for part_ref, (lo, hi) in zip((dq_ref, dkv_ref, dglu_ref, dgl_ref), PROJ_PARTS):
            part = part_ref[...]
            dh = dh + _dot(part.astype(BF16), w_ref[lo:hi, :])
            db_ref[:, lo:hi] += jnp.sum(part.astype(F32), axis=0, keepdims=True)
        xv = x_ref[...]
        r = lax.rsqrt(jnp.mean(xv * xv, axis=-1, keepdims=True) + EPS)
        xh = xv * r
        dhg = dh * g_ref[...]
        gx_ref[...] = dx1_ref[...] + r * (dhg - xh * jnp.mean(dhg * xh, axis=-1, keepdims=True))
        dg_ref[...] += jnp.sum(dh * xh, axis=0, keepdims=True)

    return pl.pallas_call(
        body, name="in_proj_bwd", grid=(T // tm,),
        in_specs=[_rows(tm, AW), _rows(tm, 2 * KVW), _rows(tm, 2 * C), _rows(tm, 2 * D),
                  _rows(tm, D), _rows(tm, D), _whole((1, D)), _const((INW, D)), _whole((8, 128))],
        out_specs=[_rows(tm, D), _whole((1, D)), _whole((1, INW))],
        out_shape=[jax.ShapeDtypeStruct((T, D), F32), jax.ShapeDtypeStruct((1, D), F32),
                   jax.ShapeDtypeStruct((1, INW), F32)],
        compiler_params=_params("arbitrary"),
    )(dq, dkv, dglu, dgl, x, dx1, g_mix, w_in_t, dep)


def _grad_w_in_t(h, dq, dkv, dglu, dgl):
    tn, chunk = 512, 256

    def body(h_ref, dq_ref, dkv_ref, dglu_ref, dgl_ref, o_ref):
        hv = h_ref[...]
        for part_ref, (lo, hi) in zip((dq_ref, dkv_ref, dglu_ref, dgl_ref), PROJ_PARTS):
            for c0 in range(0, hi - lo, chunk):
                o_ref[lo + c0:lo + c0 + chunk, :] = _dot_tn(
                    part_ref[:, c0:c0 + chunk].astype(BF16), hv).astype(BF16)

    return pl.pallas_call(
        body, name="grad_w_in", grid=(D // tn,),
        in_specs=[pl.BlockSpec((T, tn), lambda j: (0, j)), _const((T, AW)), _const((T, 2 * KVW)),
                  _const((T, 2 * C)), _const((T, 2 * D))],
        out_specs=pl.BlockSpec((INW, tn), lambda j: (0, j)),
        out_shape=jax.ShapeDtypeStruct((INW, D), BF16),
        compiler_params=_params("parallel"),
    )(h, dq, dkv, dglu, dgl)


def _grad_w(a, b, name, tk, tn, col_sharded, dep=None):
    k, n = a.shape[1], b.shape[1]

    single = n == tn
    sw = n // N_CHIPS
    deps = [] if dep is None else [dep]

    def body(a_ref, b_ref, *rest):
        o_ref, at_ref = rest[len(deps):]
        if single:
            res = _dot_tn(a_ref[...], b_ref[...]).astype(BF16)
            if col_sharded:
                for s in range(N_CHIPS):
                    o_ref[s] = res[:, s * sw:(s + 1) * sw]
            else:
                o_ref[...] = res
            return

        @pl.when(pl.program_id(1) == 0)
        def _():
            at_ref[...] = a_ref[...].T

        o_ref[...] = _dot(at_ref[...], b_ref[...]).astype(BF16)

    if col_sharded and single:
        shape = (N_CHIPS, k, sw)
        out_spec = pl.BlockSpec((N_CHIPS, tk, sw), lambda i, j: (0, i, 0))
    elif col_sharded:
        per = sw // tn
        shape = (N_CHIPS, k, sw)
        out_spec = pl.BlockSpec((None, tk, tn), lambda i, j: (j // per, i, j % per))
    else:
        shape = (1, k, n)
        out_spec = pl.BlockSpec((None, tk, tn), lambda i, j: (0, i, j))
    out = pl.pallas_call(
        body, name=name, grid=(k // tk, n // tn),
        in_specs=[pl.BlockSpec((T, tk), lambda i, j: (0, i)), pl.BlockSpec((T, tn), lambda i, j: (0, j))]
        + [ANY_SPEC] * len(deps),
        out_specs=out_spec,
        out_shape=jax.ShapeDtypeStruct(shape, BF16),
        scratch_shapes=[pltpu.VMEM((tk, T), BF16)],
        compiler_params=_params("parallel", "arbitrary"),
    )(a, b, *deps)
    return out if col_sharded else out.reshape(N_CHIPS, k // N_CHIPS, n)


HBM_SPEC = pl.BlockSpec(memory_space=pltpu.HBM)


def _place():
    x, y, c = lax.axis_index("x"), lax.axis_index("y"), lax.axis_index("c")
    chips = [(1 - x, y), (x, 1 - y), (1 - x, 1 - y)]
    return x, y, c, chips


SEM_SPEC = pl.BlockSpec(memory_space=pltpu.SEMAPHORE)
ANY_SPEC = pl.BlockSpec(memory_space=pl.ANY)
VMEM_SPEC = pl.BlockSpec(memory_space=pltpu.VMEM)
EFFECT = pltpu.SideEffectType.DATAFLOW_SIDE_EFFECTING


def _gather_ends(src, land, x, y, c, chips):
    kh = src.shape[0] // 2
    s_me = 2 * x + y
    ends = [(src.at[pl.ds(c * kh, kh)], land.at[s_me, pl.ds(c * kh, kh)], (*chip, c)) for chip in chips]
    return ends + [(src, land.at[s_me], (x, y, 1 - c))]


def _reduce_ends(src, land, x, y, c, chips):
    return [(src.at[2 * chip[0] + chip[1]], land.at[j], (*chip, c)) for j, chip in enumerate(chips)]


def _chip_copies(ends, srcs, lands, send_sems, recv_sems, first=0):
    x, y, c, chips = _place()
    copies = []
    for src, land in zip(srcs, lands):
        peers = ends(src, land, x, y, c, chips)
        for s, d, to in peers:
            k = first * len(peers) + len(copies)
            copies.append(pltpu.make_async_remote_copy(
                src_ref=s, dst_ref=d, send_sem=send_sems.at[k], recv_sem=recv_sems.at[k],
                device_id=to, device_id_type=MESH))
    return copies


GATHER_PEERS, REDUCE_PEERS = 4, 3


def _handshake(shake):
    x, y, c, chips = _place()
    peers = ([(x, y, 1 - c)] if shake in ("pair", "both") else []) + (
        [(*chip, c) for chip in chips] if shake in ("chips", "both") else [])
    barrier = pltpu.get_barrier_semaphore()
    for peer in peers:
        pl.semaphore_signal(barrier, inc=1, device_id=peer, device_id_type=MESH)
    pl.semaphore_wait(barrier, len(peers))


def _chip_start(name, ends, peers, srcs, lands, shake=None):
    n = len(srcs)

    def body(*refs):
        if shake is not None:
            _handshake(shake[0])
        copies = _chip_copies(ends, refs[:n], refs[n:2 * n], refs[2 * n], refs[2 * n + 1])
        for cp in copies:
            cp.start()
        token = refs[-1]
        token[...] = jnp.zeros_like(token)

    hbm = lambda a: pltpu.HBM(a.shape, a.dtype)
    res = pl.pallas_call(
        body, name=name,
        out_shape=(pltpu.SemaphoreType.DMA((peers * n,)), pltpu.SemaphoreType.DMA((peers * n,)),
                   *[hbm(a) for a in srcs], *[hbm(a) for a in lands],
                   jax.ShapeDtypeStruct((8, 128), F32)),
        in_specs=[HBM_SPEC] * (2 * n),
        out_specs=(SEM_SPEC, SEM_SPEC, *[HBM_SPEC] * (2 * n), VMEM_SPEC),
        input_output_aliases={i: 2 + i for i in range(2 * n)},
        compiler_params=pltpu.CompilerParams(has_side_effects=EFFECT,
                                             collective_id=None if shake is None else shake[1]),
    )(*[pltpu.with_memory_space_constraint(a, pltpu.HBM) for a in (*srcs, *lands)])
    return res[0], res[1], list(res[2:2 + n]), list(res[2 + n:2 + 2 * n]), res[-1]


def _chip_wait(name, ends, send_sems, recv_sems, srcs, lands, after, first=0):
    n, na = len(srcs), len(after)

    def body(*refs):
        copies = _chip_copies(ends, refs[:n], refs[n:2 * n], refs[2 * n], refs[2 * n + 1], first)
        for cp in copies:
            cp.wait_send()
            cp.wait_recv()

    hbm = lambda a: pltpu.HBM(a.shape, a.dtype)
    res = pl.pallas_call(
        body, name=name,
        out_shape=tuple(hbm(a) for a in (*srcs, *lands)),
        in_specs=[HBM_SPEC] * (2 * n) + [SEM_SPEC, SEM_SPEC] + [ANY_SPEC] * na,
        out_specs=tuple([HBM_SPEC] * (2 * n)),
        input_output_aliases={i: i for i in range(2 * n)},
        compiler_params=pltpu.CompilerParams(has_side_effects=EFFECT),
    )(*srcs, *lands, send_sems, recv_sems, *after)
    return list(res[:n]), list(res[n:])


def _forward_copies(lands, send_sems, recv_sems):
    x, y, c, chips = _place()
    copies = []
    for land in lands:
        kh = land.shape[1] // 2
        for chip in chips:
            blk = land.at[2 * chip[0] + chip[1], pl.ds(c * kh, kh)]
            k = len(copies)
            copies.append(pltpu.make_async_remote_copy(
                src_ref=blk, dst_ref=blk, send_sem=send_sems.at[k], recv_sem=recv_sems.at[k],
                device_id=(x, y, 1 - c), device_id_type=MESH))
    return copies


def _gather_relay(name, send_sems, recv_sems, srcs, lands, after, first, shake):
    n, na = len(srcs), len(after)

    def body(*refs):
        _handshake(shake[0])
        land_refs = refs[n:2 * n]
        for cp in _chip_copies(_gather_ends, refs[:n], land_refs, refs[2 * n], refs[2 * n + 1], first):
            cp.wait_send()
            cp.wait_recv()
        out = refs[2 * n + 2 + na:]
        for cp in _forward_copies(land_refs, out[0], out[1]):
            cp.start()
        out[-1][...] = jnp.zeros_like(out[-1])

    hbm = lambda a: pltpu.HBM(a.shape, a.dtype)
    res = pl.pallas_call(
        body, name=name,
        out_shape=(pltpu.SemaphoreType.DMA((3 * n,)), pltpu.SemaphoreType.DMA((3 * n,)),
                   *[hbm(a) for a in lands], jax.ShapeDtypeStruct((8, 128), F32)),
        in_specs=[HBM_SPEC] * (2 * n) + [SEM_SPEC, SEM_SPEC] + [ANY_SPEC] * na,
        out_specs=(SEM_SPEC, SEM_SPEC, *[HBM_SPEC] * n, VMEM_SPEC),
        input_output_aliases={n + i: 2 + i for i in range(n)},
        compiler_params=pltpu.CompilerParams(has_side_effects=EFFECT, collective_id=shake[1]),
    )(*srcs, *lands, send_sems, recv_sems, *after)
    return res[0], res[1], list(res[2:2 + n]), res[-1]


def _forward_wait(name, send_sems, recv_sems, lands, after):
    n, na = len(lands), len(after)

    def body(*refs):
        for cp in _forward_copies(refs[:n], refs[n], refs[n + 1]):
            cp.wait_send()
            cp.wait_recv()

    hbm = lambda a: pltpu.HBM(a.shape, a.dtype)
    res = pl.pallas_call(
        body, name=name,
        out_shape=tuple(hbm(a) for a in lands),
        in_specs=[HBM_SPEC] * n + [SEM_SPEC, SEM_SPEC] + [ANY_SPEC] * na,
        out_specs=tuple([HBM_SPEC] * n),
        input_output_aliases={i: i for i in range(n)},
        compiler_params=pltpu.CompilerParams(has_side_effects=EFFECT),
    )(*lands, send_sems, recv_sems, *after)
    return list(res)


def _exchange_ends(src, land, x, y, c, chips):
    kh = src.shape[1] // 2
    return [(src.at[:, pl.ds((1 - c) * kh, kh)], land, (x, y, 1 - c))]


def _share_ends(src, land, x, y, c, chips):
    return [(src, land, (x, y, 1 - c))]


def _small_ends(src, land, x, y, c, chips):
    m = src.shape[0]
    rows = land.at[pl.ds((4 * x + 2 * y + c) * m, m)]
    peers = [(x, y, 1 - c)] + [(*chip, c) for chip in chips] + [(*chip, 1 - c) for chip in chips]
    return [(src, rows, to) for to in peers]


PAIR_PEERS, SMALL_PEERS = 1, 7
SHAKES = {"pair_1": ("pair", 0), "pair_2": ("pair", 1), "share_1": ("pair", 2), "share_2": ("pair", 3),
          "chip_1": ("chips", 4), "chip_2": ("chips", 5), "gather_b": ("both", 6),
          "relay_a": ("pair", 7), "relay_b1": ("pair", 8), "relay_b2": ("pair", 9), "relay_b3": ("pair", 10),
          "pair_0": ("pair", 11), "chip_0": ("chips", 12), "share_0": ("pair", 13)}


def _row_tile(k):
    for t in (256, 240, 128, 176, 64, 32, 16):
        if k % t == 0:
            return t
    raise ValueError(k)


def _pair_sum_small(c_idx, items, name):
    n = len(items)

    def body(c_ref, *refs):
        for k in range(n):
            g_ref, r_ref, o_ref = refs[2 * k], refs[2 * k + 1], refs[2 * n + k]
            o_ref[...] = (g_ref[...].astype(F32) + r_ref[...].astype(F32)).astype(BF16)

    in_specs, out_specs, out_shape, flat = [], [], [], []
    for g, got in items:
        _, kh, w = got.shape
        in_specs += [pl.BlockSpec((N_CHIPS, kh, w), lambda i, c_ref: (0, c_ref[0], 0)),
                     pl.BlockSpec((N_CHIPS, kh, w), lambda i, c_ref: (0, 0, 0))]
        out_specs.append(pl.BlockSpec((N_CHIPS, kh, w), lambda i, c_ref: (0, 0, 0)))
        out_shape.append(jax.ShapeDtypeStruct((N_CHIPS, kh, w), BF16))
        flat += [g, got]
    return pl.pallas_call(
        body, name=name,
        grid_spec=pltpu.PrefetchScalarGridSpec(num_scalar_prefetch=1, grid=(1,), in_specs=in_specs,
                                               out_specs=out_specs),
        out_shape=out_shape, compiler_params=_params("arbitrary"),
    )(c_idx, *flat)


def _pair_sum(c_idx, g, got, name):
    _, k, n = g.shape
    kh = k // 2
    tm = _row_tile(kh)
    nb = kh // tm

    def body(c_ref, g_ref, r_ref, o_ref):
        o_ref[...] = (g_ref[...].astype(F32) + r_ref[...].astype(F32)).astype(BF16)

    return pl.pallas_call(
        body, name=name,
        grid_spec=pltpu.PrefetchScalarGridSpec(
            num_scalar_prefetch=1, grid=(nb,),
            in_specs=[pl.BlockSpec((N_CHIPS, tm, n), lambda i, c_ref: (0, c_ref[0] * nb + i, 0)),
                      pl.BlockSpec((N_CHIPS, tm, n), lambda i, c_ref: (0, i, 0))],
            out_specs=pl.BlockSpec((N_CHIPS, tm, n), lambda i, c_ref: (0, i, 0))),
        out_shape=jax.ShapeDtypeStruct((N_CHIPS, kh, n), BF16),
        compiler_params=_params("parallel"),
    )(c_idx, g, got)


def _chip_sum_small(s_idx, items, name):
    n = len(items)

    def body(s_ref, *refs):
        for k in range(n):
            m_ref, r_ref, o_ref = refs[2 * k], refs[2 * k + 1], refs[2 * n + k]
            acc = m_ref[0].astype(F32)
            for j in range(3):
                acc = acc + r_ref[j].astype(F32)
            o_ref[...] = acc

    in_specs, out_specs, out_shape, flat = [], [], [], []
    for mine, got in items:
        _, kh, w = mine.shape
        in_specs += [pl.BlockSpec((1, kh, w), lambda i, s_ref: (s_ref[0], 0, 0)),
                     pl.BlockSpec((3, kh, w), lambda i, s_ref: (0, 0, 0))]
        out_specs.append(pl.BlockSpec((kh, w), lambda i, s_ref: (0, 0)))
        out_shape.append(jax.ShapeDtypeStruct((kh, w), F32))
        flat += [mine, got]
    return pl.pallas_call(
        body, name=name,
        grid_spec=pltpu.PrefetchScalarGridSpec(num_scalar_prefetch=1, grid=(1,), in_specs=in_specs,
                                               out_specs=out_specs),
        out_shape=out_shape, compiler_params=_params("arbitrary"),
    )(s_idx, *flat)


def _chip_sum(s_idx, mine, got, name):
    _, kh, n = mine.shape
    tm = _row_tile(kh)

    def body(s_ref, m_ref, r_ref, o_ref):
        acc = m_ref[0].astype(F32)
        for j in range(3):
            acc = acc + r_ref[j].astype(F32)
        o_ref[...] = acc

    return pl.pallas_call(
        body, name=name,
        grid_spec=pltpu.PrefetchScalarGridSpec(
            num_scalar_prefetch=1, grid=(kh // tm,),
            in_specs=[pl.BlockSpec((1, tm, n), lambda i, s_ref: (s_ref[0], i, 0)),
                      pl.BlockSpec((3, tm, n), lambda i, s_ref: (0, i, 0))],
            out_specs=pl.BlockSpec((tm, n), lambda i, s_ref: (i, 0))),
        out_shape=jax.ShapeDtypeStruct((kh, n), F32),
        compiler_params=_params("parallel"),
    )(s_idx, mine, got)


ADAMW_TILES = 2


def _adamw_math(w, g, m, v):
    m = ADAM_B1 * m + (1.0 - ADAM_B1) * g
    v = ADAM_B2 * v + (1.0 - ADAM_B2) * (g * g)
    m_hat = m / (1.0 - ADAM_B1 ** ADAM_STEP)
    v_hat = v / (1.0 - ADAM_B2 ** ADAM_STEP)
    delta = -ADAM_LR * (m_hat / (jnp.sqrt(v_hat) + ADAM_EPS) + ADAM_WD * w)
    return delta, m, v


def _adamw_small(c_idx, items, name):
    n = len(items)

    def body(c_ref, *refs):
        mine = pl.program_id(0) == c_ref[0]
        for k in range(n):
            w_ref, gm_ref, go_ref, m_ref, v_ref = refs[5 * k:5 * k + 5]
            g_ref, d_ref, mo_ref, vo_ref = refs[5 * n + 4 * k:5 * n + 4 * k + 4]
            g = jnp.where(mine, gm_ref[...], go_ref[...])
            d, mm, vv = _adamw_math(w_ref[...], g, m_ref[...], v_ref[...])
            g_ref[...] = g
            d_ref[...] = d
            mo_ref[...] = mm
            vo_ref[...] = vv

    in_specs, out_specs, out_shape, flat = [], [], [], []
    for w, g_mine, g_other, m, v in items:
        rows, cols = w.shape
        tm = rows // (2 * ADAMW_TILES)
        full = pl.BlockSpec((tm, cols), lambda h, i, c_ref: (ADAMW_TILES * h + i, 0))
        own = pl.BlockSpec((tm, cols), lambda h, i, c_ref: (jnp.where(h == c_ref[0], i, 0), 0))
        other = pl.BlockSpec((tm, cols), lambda h, i, c_ref: (jnp.where(h == c_ref[0], 0, i), 0))
        in_specs += [full, own, other, full, full]
        out_specs += [full] * 4
        out_shape += [jax.ShapeDtypeStruct((rows, cols), F32)] * 4
        flat += [w, g_mine, g_other, m, v]
    res = pl.pallas_call(
        body, name=name,
        grid_spec=pltpu.PrefetchScalarGridSpec(num_scalar_prefetch=1, grid=(2, ADAMW_TILES), in_specs=in_specs,
                                               out_specs=out_specs),
        out_shape=out_shape, compiler_params=_params("arbitrary", "arbitrary"),
    )(c_idx, *flat)
    return [tuple(res[4 * k:4 * k + 4]) for k in range(n)]


def _adamw(c_idx, w, g_mine, g_other, m, v, name):
    k, n = w.shape
    nt = ADAMW_TILES
    tm = k // (2 * nt)

    def body(c_ref, w_ref, gm_ref, go_ref, m_ref, v_ref, g_ref, d_ref, mo_ref, vo_ref):
        g = jnp.where(pl.program_id(0) == c_ref[0], gm_ref[...], go_ref[...])
        d, mm, vv = _adamw_math(w_ref[...], g, m_ref[...], v_ref[...])
        g_ref[...] = g
        d_ref[...] = d
        mo_ref[...] = mm
        vo_ref[...] = vv

    full = pl.BlockSpec((tm, n), lambda h, i, c_ref: (nt * h + i, 0))
    mine = pl.BlockSpec((tm, n), lambda h, i, c_ref: (jnp.where(h == c_ref[0], i, 0), 0))
    other = pl.BlockSpec((tm, n), lambda h, i, c_ref: (jnp.where(h == c_ref[0], 0, i), 0))
    shp = jax.ShapeDtypeStruct((k, n), F32)
    return pl.pallas_call(
        body, name=name,
        grid_spec=pltpu.PrefetchScalarGridSpec(
            num_scalar_prefetch=1, grid=(2, nt),
            in_specs=[full, mine, other, full, full], out_specs=[full] * 4),
        out_shape=[shp] * 4, compiler_params=_params("arbitrary", "arbitrary"),
    )(c_idx, w, g_mine, g_other, m, v)


VEC_SLOTS = {
    "g_mix_norm": (0, 0, D), "b_conv_proj": (0, D, D), "g_ffn_norm": (0, 2 * D, D),
    "g_final": (0, 3 * D, D), "b_in": (1, 0, INW), "conv_b": (2, 0, C), "ln_g": (2, C, C),
    "ln_b": (2, 2 * C, C), "sinks": (2, 3 * C, NQ), "loss": (2, 3 * C + 128, 1),
}
VEC_ROWS, VEC_COLS = 8, 4 * D
CW_ROWS = 32
SMALL_NAMES = ["g_mix_norm", "b_in", "sinks", "conv_w", "conv_b", "ln_g", "ln_b",
               "b_conv_proj", "g_ffn_norm", "g_final"]
CW_LANES = C // N_CHIPS


def _pack_small(gs, loss):
    row0 = jnp.concatenate([gs["g_mix_norm"], gs["b_conv_proj"], gs["g_ffn_norm"], gs["g_final"]], axis=1)
    row1 = jnp.pad(gs["b_in"], ((0, 0), (0, VEC_COLS - INW)))
    row2 = jnp.concatenate([gs["conv_b"], gs["ln_g"], gs["ln_b"],
                            jnp.pad(gs["sinks"], ((0, 0), (0, 128 - NQ))),
                            jnp.pad(loss.reshape(1, 1), ((0, 0), (0, VEC_COLS - 3 * C - 129)))], axis=1)
    vec = jnp.concatenate([row0, row1, row2, jnp.zeros((VEC_ROWS - 3, VEC_COLS), F32)], axis=0)
    cw = jnp.pad(gs["conv_w"], ((0, CW_ROWS - KW), (0, 0)))
    return vec, cw


def _small_update(idx, vec_own, cw_own, vec_all, cw_all, wmv):
    nsm = len(SMALL_NAMES)

    def body(s_ref, vown_ref, cown_ref, vec_ref, cw_ref, *refs):
        ins = refs[:3 * nsm]
        outs = refs[3 * nsm:7 * nsm]
        loss_ref = refs[7 * nsm]
        me = s_ref[1]

        def summed(own_ref, table_ref, rows_per_dev, r0, nrows, lane, width):
            acc = None
            for k in range(8):
                piece = jnp.where(me == k, own_ref[r0:r0 + nrows, lane:lane + width],
                                  table_ref[k * rows_per_dev + r0:k * rows_per_dev + r0 + nrows, lane:lane + width])
                acc = piece if acc is None else acc + piece
            return acc

        def total(slot):
            row, lane, width = slot
            return summed(vown_ref, vec_ref, VEC_ROWS, row, 1, lane, width)

        loss_ref[...] = jnp.broadcast_to(total(VEC_SLOTS["loss"]), loss_ref.shape)
        for p, name in enumerate(SMALL_NAMES):
            w_ref, m_ref, v_ref = ins[3 * p:3 * p + 3]
            g_ref, d_ref, mo_ref, vo_ref = outs[4 * p:4 * p + 4]
            if name == "conv_w":
                g = jnp.zeros((KW, CW_LANES), F32)
                for s in range(N_CHIPS):
                    cand = summed(cown_ref, cw_ref, CW_ROWS, 0, KW, s * CW_LANES, CW_LANES)
                    g = jnp.where(s_ref[0] == s, cand, g)
            else:
                g = total(VEC_SLOTS[name])
            d, mm, vv = _adamw_math(w_ref[...], g, m_ref[...], v_ref[...])
            g_ref[...] = g
            d_ref[...] = d
            mo_ref[...] = mm
            vo_ref[...] = vv

    vmem = pl.BlockSpec(memory_space=pltpu.VMEM)
    flat = [a for t in wmv for a in t]
    out_shape = []
    for w, _, _ in wmv:
        out_shape += [jax.ShapeDtypeStruct(w.shape, F32)] * 4
    out_shape.append(jax.ShapeDtypeStruct((1, 128), F32))
    res = pl.pallas_call(
        body, name="small_update",
        in_specs=[pl.BlockSpec(memory_space=pltpu.SMEM)] + [vmem] * (4 + len(flat)),
        out_specs=[vmem] * len(out_shape), out_shape=out_shape,
    )(idx, vec_own, cw_own, vec_all, cw_all, *flat)
    return [tuple(res[4 * p:4 * p + 4]) for p in range(nsm)], res[4 * nsm]


SMALL_BIG = ("w_out", "w_attn_proj", "w_conv_proj")
WEIGHT_ORDER = ["g_mix_norm", "w_in", "b_in", "sinks", "conv_w", "conv_b", "ln_g", "ln_b",
                "w_attn_proj", "w_conv_proj", "b_conv_proj", "w_out", "g_ffn_norm", "w_ffn_in",
                "w_ffn_down", "g_final"]


def kernel(x, g_mix_norm, w_in, b_in, sinks, conv_w, conv_b, ln_g, ln_b, w_attn_proj, w_conv_proj, b_conv_proj, w_out, g_ffn_norm, w_ffn_in, w_ffn_down, g_final, loss_target, m_g_mix_norm, m_w_in, m_b_in, m_sinks, m_conv_w, m_conv_b, m_ln_g, m_ln_b, m_w_attn_proj, m_w_conv_proj, m_b_conv_proj, m_w_out, m_g_ffn_norm, m_w_ffn_in, m_w_ffn_down, m_g_final, v_g_mix_norm, v_w_in, v_b_in, v_sinks, v_conv_w, v_conv_b, v_ln_g, v_ln_b, v_w_attn_proj, v_w_conv_proj, v_b_conv_proj, v_w_out, v_g_ffn_norm, v_w_ffn_in, v_w_ffn_down, v_g_final):
    w = dict(g_mix_norm=g_mix_norm, w_in=w_in, b_in=b_in, sinks=sinks, conv_w=conv_w, conv_b=conv_b,
             ln_g=ln_g, ln_b=ln_b, w_attn_proj=w_attn_proj, w_conv_proj=w_conv_proj,
             b_conv_proj=b_conv_proj, w_out=w_out, g_ffn_norm=g_ffn_norm, w_ffn_in=w_ffn_in,
             w_ffn_down=w_ffn_down, g_final=g_final)
    m = dict(g_mix_norm=m_g_mix_norm, w_in=m_w_in, b_in=m_b_in, sinks=m_sinks, conv_w=m_conv_w,
             conv_b=m_conv_b, ln_g=m_ln_g, ln_b=m_ln_b, w_attn_proj=m_w_attn_proj,
             w_conv_proj=m_w_conv_proj, b_conv_proj=m_b_conv_proj, w_out=m_w_out,
             g_ffn_norm=m_g_ffn_norm, w_ffn_in=m_w_ffn_in, w_ffn_down=m_w_ffn_down, g_final=m_g_final)
    v = dict(g_mix_norm=v_g_mix_norm, w_in=v_w_in, b_in=v_b_in, sinks=v_sinks, conv_w=v_conv_w,
             conv_b=v_conv_b, ln_g=v_ln_g, ln_b=v_ln_b, w_attn_proj=v_w_attn_proj,
             w_conv_proj=v_w_conv_proj, b_conv_proj=v_b_conv_proj, w_out=v_w_out,
             g_ffn_norm=v_g_ffn_norm, w_ffn_in=v_w_ffn_in, w_ffn_down=v_w_ffn_down, g_final=v_g_final)

    c_idx = lax.axis_index("c").astype(jnp.int32).reshape(1)
    s_idx = (2 * lax.axis_index("x") + lax.axis_index("y")).astype(jnp.int32).reshape(1)

    out_g, out_d, out_m, out_v = {}, {}, {}, {}

    def gather_start(tag, shards):
        lands = [lax.empty((N_CHIPS,) + s.shape, s.dtype) for s in shards]
        return _chip_start("gather_start_" + tag, _gather_ends, GATHER_PEERS, shards, lands,
                           SHAKES.get("gather_" + tag))

    def gather_relay(tag, state, after, first=0, count=None):
        send_sems, recv_sems, shards, lands, _ = state
        last = len(shards) if count is None else first + count
        return _gather_relay("gather_relay_" + tag, send_sems, recv_sems, shards[first:last],
                             lands[first:last], after, first, SHAKES["relay_" + tag])

    def gather_finish(tag, relay, after):
        return _forward_wait("forward_wait_" + tag, relay[0], relay[1], relay[2], after)

    names_b = ["w_attn_proj", "w_conv_proj", "w_out", "w_ffn_in", "w_ffn_down"]
    big = {name: (w[name][0], m[name][0], v[name][0]) for name in names_b}
    big["w_in"] = (w_in[0].T, m_w_in[0].T, v_w_in[0].T)
    state_a = gather_start("a", [big["w_in"][0].astype(BF16), jnp.pad(conv_w[0], ((0, CW_ROWS - KW), (0, 0)))])
    state_b = gather_start("b", [(big[name][0] + state_a[4][0, 0]).astype(BF16) for name in names_b])
    got_a = gather_finish("a", gather_relay("a", state_a, [state_b[4]]), [])
    w_in_t_full = got_a[0].reshape(INW, D)
    conv_w_full = got_a[1].transpose(1, 0, 2).reshape(CW_ROWS, C)[:KW]

    xs, target = x[0], loss_target[0]
    g_final2 = g_final.reshape(1, D)
    h, qkv, glu, gl = _in_proj(xs, g_mix_norm, w_in_t_full, b_in)
    o, lse, probs = _attn_fwd(qkv, sinks)
    relay_1 = gather_relay("b1", state_b, [o], 0, 3)
    u, cact = _conv_fwd(glu, conv_w_full, conv_b, ln_g, ln_b, relay_1[3])
    w_ap4, w_cp4, w_out4 = gather_finish("b1", relay_1, [cact])
    w_out_full = w_out4.reshape(D, D)
    mixed = _mix_out(xs, o, cact, gl, w_ap4, w_cp4, b_conv_proj, w_out_full, relay_1[3])
    relay_2 = gather_relay("b2", state_b, [mixed[3]], 3, 1)
    ya, yc, mg, x1 = _mix_out(xs, o, cact, gl, w_ap4, w_cp4, b_conv_proj, w_out_full, relay_2[3], mixed)
    w_fi4, = gather_finish("b2", relay_2, [x1])
    h2, gu, act = _ffn_in_first(x1, g_ffn_norm, w_fi4, relay_2[3])
    relay_3 = gather_relay("b3", state_b, [h2], 4, 1)
    gu, act = _ffn_in_second(h2, w_fi4, gu, act, relay_3[3])
    w_dn4, = gather_finish("b3", relay_3, [act])
    w_dn_full = w_dn4.reshape(DFF, D)
    dx2, dx2b, dg_final, loss_part = _ffn_out_loss(x1, act, w_dn_full, g_final2, target)

    def exchange_start(tag, grads):
        lands = [lax.empty((N_CHIPS, g.shape[1] // 2, g.shape[2]), g.dtype) for g in grads]
        return _chip_start("pair_start_" + tag, _exchange_ends, PAIR_PEERS, grads, lands, SHAKES["pair_" + tag])

    def reduce_start(tag, names, exchange, after):
        send_sems, recv_sems, grads, lands, _ = exchange
        grads, from_sibling = _chip_wait("pair_wait_" + tag, _exchange_ends, send_sems, recv_sems, grads, lands, after)
        pair = {name: _pair_sum(c_idx, g, r, "pair_sum_" + name)
                for name, g, r in zip(names, grads, from_sibling) if name not in SMALL_BIG}
        small = [(g, r) for name, g, r in zip(names, grads, from_sibling) if name in SMALL_BIG]
        if small:
            sums = _pair_sum_small(c_idx, small, "pair_sum_small_" + tag)
            pair.update(zip([name for name in names if name in SMALL_BIG], sums))
        pair = [pair[name] for name in names]
        lands = [lax.empty((3,) + p.shape[1:], p.dtype) for p in pair]
        return _chip_start("chip_start_" + tag, _reduce_ends, REDUCE_PEERS, pair, lands, SHAKES["chip_" + tag])

    def reduce_sum(tag, names, state, after):
        send_sems, recv_sems, pair, lands, _ = state
        pair, lands = _chip_wait("chip_wait_" + tag, _reduce_ends, send_sems, recv_sems, pair, lands, after)
        mine = {name: _chip_sum(s_idx, p, r, "chip_sum_" + name)
                for name, p, r in zip(names, pair, lands) if name not in SMALL_BIG}
        small = [(p, r) for name, p, r in zip(names, pair, lands) if name in SMALL_BIG]
        if small:
            sums = _chip_sum_small(s_idx, small, "chip_sum_small_" + tag)
            mine.update(zip([name for name in names if name in SMALL_BIG], sums))
        mine = [mine[name] for name in names]
        others = [lax.empty(a.shape, a.dtype) for a in mine]
        return _chip_start("share_start_" + tag, _share_ends, PAIR_PEERS, mine, others, SHAKES["share_" + tag])

    def reduce_finish(tag, names, share, after):
        send_sems, recv_sems, mine, others, _ = share
        mine, others = _chip_wait("share_wait_" + tag, _share_ends, send_sems, recv_sems, mine, others, after)
        results = {name: _adamw(c_idx, big[name][0], g_mine, g_other, big[name][1], big[name][2], "adamw_" + name)
                   for name, g_mine, g_other in zip(names, mine, others) if name not in SMALL_BIG}
        small = [name for name in names if name in SMALL_BIG]
        if small:
            halves = dict(zip(names, zip(mine, others)))
            items = [(big[name][0], *halves[name], big[name][1], big[name][2]) for name in small]
            results.update(zip(small, _adamw_small(c_idx, items, "adamw_small_" + tag)))
        for name in names:
            res = results[name]
            if name == "w_in":
                res = [a.T for a in res]
            out_g[name], out_d[name], out_m[name], out_v[name] = [a[None] for a in res]

    dgu, dx1, dx1b, dg_ffn = _ffn_bwd(dx2, dx2b, gu, x1, g_ffn_norm, w_dn_full, w_fi4)
    names_0 = ["w_ffn_in", "w_ffn_down"]
    dya, dyc, dgl, do, dc, db_cp = _mix_bwd(dx1b, gl, ya, yc, w_out_full, w_ap4, w_cp4, relay_3[3])
    grads_0 = [_grad_w(h2, dgu, "grad_w_ffn_in", 512, FSH, True, dya),
               _grad_w(act, dx2b, "grad_w_ffn_down", 256, D, False)]
    exchange_0 = exchange_start("0", grads_0)
    names_1 = ["w_out", "w_attn_proj", "w_conv_proj"]
    grads_1 = [_grad_w(mg, dx1b, "grad_w_out", 512, D, False, exchange_0[4]),
               _grad_w(o, dya, "grad_w_attn_proj", 512, D, True, exchange_0[4]),
               _grad_w(cact, dyc, "grad_w_conv_proj", 512, D, True, exchange_0[4])]
    state_0 = reduce_start("0", names_0, exchange_0, grads_1)
    exchange_1 = exchange_start("1", grads_1)
    dq, dkv, dsinks = _attn_bwd(qkv, do, lse, probs, sinks, exchange_1[4], state_0[4])
    state_1 = reduce_start("1", names_1, exchange_1, [dq])
    dglu, dconv_w, dconv_b, dln_g, dln_b = _conv_bwd(glu, u, dc, conv_w_full, ln_g, ln_b, state_1[4])
    names_2 = ["w_in"]
    gw_in_t = _grad_w_in_t(h, dq, dkv, dglu, dgl)
    exchange_2 = exchange_start("2", [gw_in_t.reshape(N_CHIPS, INW // N_CHIPS, D)])
    grad_x, dg_mix, db_in = _in_proj_bwd(dq, dkv, dglu, dgl, xs, dx1, g_mix_norm, w_in_t_full, exchange_2[4])

    gs = {"g_mix_norm": dg_mix, "b_in": db_in, "sinks": dsinks[:, 0].reshape(1, NQ),
          "conv_w": dconv_w, "conv_b": dconv_b, "ln_g": dln_g, "ln_b": dln_b,
          "b_conv_proj": db_cp, "g_ffn_norm": dg_ffn, "g_final": dg_final}
    blocks = list(_pack_small(gs, loss_part[0, 0]))
    tables = [lax.empty((8 * b.shape[0], b.shape[1]), b.dtype) for b in blocks]
    small = _chip_start("small_start", _small_ends, SMALL_PEERS, blocks, tables)

    state_2 = reduce_start("2", names_2, exchange_2, [grad_x, small[4]])
    share_0 = reduce_sum("0", names_0, state_0, [state_2[4]])
    share_1 = reduce_sum("1", names_1, state_1, [share_0[4]])
    blocks, tables = _chip_wait("small_wait", _small_ends, small[0], small[1], small[2], small[3],
                                [share_1[4]])
    me = (4 * lax.axis_index("x") + 2 * lax.axis_index("y") + lax.axis_index("c")).astype(jnp.int32)

    def view(a, name):
        if name == "conv_w":
            return a[0]
        if name == "g_final":
            return a.reshape(1, D)
        return a

    wmv = [(view(w[name], name), view(m[name], name), view(v[name], name)) for name in SMALL_NAMES]
    small_out, loss_row = _small_update(jnp.concatenate([s_idx, me.reshape(1)]), blocks[0], blocks[1],
                                        tables[0], tables[1], wmv)
    for name, (g, d, mm, vv) in zip(SMALL_NAMES, small_out):
        shape = w[name].shape
        out_g[name], out_d[name], out_m[name], out_v[name] = (
            g.reshape(shape), d.reshape(shape), mm.reshape(shape), vv.reshape(shape))

    reduce_finish("0", names_0, share_0, [loss_row])
    reduce_finish("1", names_1, share_1, [out_d["w_ffn_down"]])
    share_2 = reduce_sum("2", names_2, state_2, [out_d["w_conv_proj"]])
    reduce_finish("2", names_2, share_2, [])

    loss = loss_row[0, 0]
    return (loss, grad_x[None], *[out_g[k] for k in WEIGHT_ORDER], *[out_d[k] for k in WEIGHT_ORDER],
            *[out_m[k] for k in WEIGHT_ORDER], *[out_v[k] for k in WEIGHT_ORDER])
```

```python
import functools

import jax
import jax.numpy as jnp
from jax import lax
from jax.experimental import pallas as pl
from jax.experimental.pallas import tpu as pltpu

F32 = jnp.float32
BF16 = jnp.bfloat16

T = 2048
D = 1024
HD = 64
NQ = 8
NKV = 2
GROUP = NQ // NKV
BLK = 128
AW = NQ * HD
KVW = NKV * HD
C = 512
KW = 31
QKVW = AW + 2 * KVW
GLU_OFF = QKVW
GATE_OFF = GLU_OFF + 2 * C
INW = GATE_OFF + 2 * D
DFF = 2816
EPS = 1e-5
NEG = -1e30
SCALE = HD ** -0.5
HALO = 32
N_CHIPS = 4
FSH = 2 * DFF // N_CHIPS

ADAM_LR = 0.001
ADAM_B1 = 0.9
ADAM_B2 = 0.999
ADAM_EPS = 1e-08
ADAM_WD = 0.01
ADAM_STEP = 10

VMEM_LIMIT = 56 * 1024 * 1024
ROW_TM = 512
MESH = pl.DeviceIdType.MESH


def _params(*sem):
    return pltpu.CompilerParams(dimension_semantics=sem, vmem_limit_bytes=VMEM_LIMIT)


def _dot(a, b):
    return jnp.dot(a, b, preferred_element_type=F32)


def _dot_nt(a, b):
    return lax.dot_general(a, b, (((1,), (1,)), ((), ())), preferred_element_type=F32)


def _dot_tn(a, b):
    return lax.dot_general(a, b, (((0,), (0,)), ((), ())), preferred_element_type=F32)


def _sigmoid(v):
    return 1.0 / (1.0 + jnp.exp(-v))


def _rows(tm, n):
    return pl.BlockSpec((tm, n), lambda i: (i, 0))


def _whole(shape):
    return pl.BlockSpec(shape, lambda i: tuple(0 for _ in shape))


def _in_proj(x, g_mix, w_in_t, b_in):
    tm = ROW_TM

    def body(x_ref, g_ref, w_ref, b_ref, h_ref, qkv_ref, glu_ref, gl_ref):
        xv = x_ref[...]
        r = lax.rsqrt(jnp.mean(xv * xv, axis=-1, keepdims=True) + EPS)
        h = (xv * r * g_ref[...]).astype(BF16)
        h_ref[...] = h
        qkv_ref[...] = (_dot_nt(h, w_ref[0:GLU_OFF, :]) + b_ref[:, 0:GLU_OFF]).astype(BF16)
        glu_ref[...] = (_dot_nt(h, w_ref[GLU_OFF:GATE_OFF, :]) + b_ref[:, GLU_OFF:GATE_OFF]).astype(BF16)
        gl_ref[...] = (_dot_nt(h, w_ref[GATE_OFF:INW, :]) + b_ref[:, GATE_OFF:INW]).astype(BF16)

    return pl.pallas_call(
        body, name="in_proj", grid=(T // tm,),
        in_specs=[_rows(tm, D), _whole((1, D)), _whole((INW, D)), _whole((1, INW))],
        out_specs=[_rows(tm, D), _rows(tm, QKVW), _rows(tm, 2 * C), _rows(tm, 2 * D)],
        out_shape=[jax.ShapeDtypeStruct((T, D), BF16), jax.ShapeDtypeStruct((T, QKVW), BF16),
                   jax.ShapeDtypeStruct((T, 2 * C), BF16), jax.ShapeDtypeStruct((T, 2 * D), BF16)],
        compiler_params=_params("parallel"),
    )(x, g_mix, w_in_t, b_in)


GROWS = GROUP * BLK
BAND = 2 * BLK
ATT_SUB = 2


def _band(i):
    rb = pl.multiple_of(jnp.maximum(i - 1, 0) * BLK, BLK)
    row = lax.broadcasted_iota(jnp.int32, (GROWS, BAND), 0)
    kpos = rb + lax.broadcasted_iota(jnp.int32, (GROWS, BAND), 1)
    qpos = i * BLK + jnp.bitwise_and(row, BLK - 1)
    return rb, jnp.logical_and(kpos <= qpos, kpos > qpos - BLK)


def _sink_column(sink_ref, g):
    head = lax.shift_right_logical(lax.broadcasted_iota(jnp.int32, (GROWS, 1), 0), 7)
    col = jnp.full((GROWS, 1), sink_ref[0, g * GROUP], F32)
    for hh in range(1, GROUP):
        col = jnp.where(head == hh, sink_ref[0, g * GROUP + hh], col)
    return col


def _attn_fwd(qkv, sinks):
    def body(sink_ref, qkv_ref, o_ref, lse_ref, p_ref, s_ref):
        slots = [(sub, g) for sub in range(ATT_SUB) for g in range(NKV)]
        bands = [_band(pl.program_id(0) * ATT_SUB + sub) for sub in range(ATT_SUB)]
        for n, (sub, g) in enumerate(slots):
            rb = bands[sub][0]
            r0 = pl.multiple_of((pl.program_id(0) * ATT_SUB + sub) * BLK, BLK)
            kband = qkv_ref[pl.ds(rb, BAND), AW + g * HD:AW + (g + 1) * HD]
            for hh in range(GROUP):
                h = g * GROUP + hh
                s_ref[n, hh * BLK:(hh + 1) * BLK, :] = _dot_nt(
                    qkv_ref[pl.ds(r0, BLK), h * HD:(h + 1) * HD], kband)
        lses = []
        for n, (sub, g) in enumerate(slots):
            s = jnp.where(bands[sub][1], s_ref[n] * SCALE, NEG)
            sink = _sink_column(sink_ref, g)
            m = jnp.maximum(jnp.max(s, axis=-1, keepdims=True), sink)
            p = jnp.exp(s - m)
            den = jnp.sum(p, axis=-1, keepdims=True) + jnp.exp(sink - m)
            p_ref[n] = (p * (1.0 / den)).astype(BF16)
            lses.append(m + jnp.log(den))
        for n, (sub, g) in enumerate(slots):
            rb = bands[sub][0]
            rows = slice(sub * BLK, (sub + 1) * BLK)
            vband = qkv_ref[pl.ds(rb, BAND), AW + KVW + g * HD:AW + KVW + (g + 1) * HD]
            for hh in range(GROUP):
                h = g * GROUP + hh
                o_ref[rows, h * HD:(h + 1) * HD] = _dot(p_ref[n, hh * BLK:(hh + 1) * BLK, :], vband).astype(BF16)
                lse_ref[rows, h:h + 1] = lses[n][hh * BLK:(hh + 1) * BLK]

    nslot = ATT_SUB * NKV
    return pl.pallas_call(
        body, name="attn_fwd", grid=(T // (ATT_SUB * BLK),),
        in_specs=[pl.BlockSpec(memory_space=pltpu.SMEM), _whole((T, QKVW))],
        out_specs=[_rows(ATT_SUB * BLK, AW), _rows(ATT_SUB * BLK, NQ),
                   pl.BlockSpec((nslot, GROWS, BAND), lambda i: (i, 0, 0))],
        out_shape=[jax.ShapeDtypeStruct((T, AW), BF16), jax.ShapeDtypeStruct((T, NQ), F32),
                   jax.ShapeDtypeStruct((T // BLK * NKV, GROWS, BAND), BF16)],
        scratch_shapes=[pltpu.VMEM((nslot, GROWS, BAND), F32)],
        compiler_params=_params("parallel"),
    )(sinks, qkv)


CONV_TM = 256
CONV_SUB = 32


def _glu(ab):
    a = ab[:, 0:C].astype(F32)
    b = ab[:, C:2 * C].astype(F32)
    return a * _sigmoid(b)


SUBLANES = 8


def _shifted_copies(ref):
    rows = ref.shape[1] - SUBLANES
    for r in range(1, SUBLANES):
        ref[r, 0:rows, :] = ref[0, r:r + rows, :]


def _shifted_rows(ref, start, size):
    r = start % SUBLANES
    return ref[r, start - r:start - r + size, :]


def _conv_fwd(glu, conv_w, conv_b, ln_g, ln_b, dep):
    tm = CONV_TM

    def body(cur_ref, prev_ref, w_ref, cb_ref, g_ref, b_ref, dep_ref, u_ref, c_ref, zs_ref):
        i = pl.program_id(0)
        zprev = _glu(prev_ref[tm - HALO:tm, :])
        zs_ref[0, 0:HALO, :] = jnp.where(i > 0, zprev, 0.0)
        zs_ref[0, HALO:HALO + tm, :] = _glu(cur_ref[...])
        _shifted_copies(zs_ref)
        for s in range(tm // CONV_SUB):
            base = HALO + s * CONV_SUB - (KW - 1)
            acc = jnp.broadcast_to(cb_ref[...], (CONV_SUB, C))
            for j in range(KW):
                acc = acc + w_ref[j:j + 1, :] * _shifted_rows(zs_ref, base + j, CONV_SUB)
            rows = slice(s * CONV_SUB, (s + 1) * CONV_SUB)
            u_ref[rows, :] = acc
            mu = jnp.mean(acc, axis=-1, keepdims=True)
            xc = acc - mu
            var = jnp.mean(xc * xc, axis=-1, keepdims=True)
            y = xc * lax.rsqrt(var + EPS) * g_ref[...] + b_ref[...]
            c_ref[rows, :] = (y * _sigmoid(y)).astype(BF16)

    return pl.pallas_call(
        body, name="conv_fwd", grid=(T // tm,),
        in_specs=[_rows(tm, 2 * C),
                  pl.BlockSpec((tm, 2 * C), lambda i: (jnp.maximum(i - 1, 0), 0)),
                  _whole((KW, C)), _whole((1, C)), _whole((1, C)), _whole((1, C)), _whole((8, 128))],
        out_specs=[_rows(tm, C), _rows(tm, C)],
        out_shape=[jax.ShapeDtypeStruct((T, C), F32), jax.ShapeDtypeStruct((T, C), BF16)],
        scratch_shapes=[pltpu.VMEM((SUBLANES, HALO + tm, C), F32)],
        compiler_params=_params("parallel"),
    )(glu, glu, conv_w, conv_b, ln_g, ln_b, dep)


def _branch_outputs(o, cact, wap_ref, wcp_ref, bcp_ref):
    ya = jnp.concatenate([_dot(o, wap_ref[s]) for s in range(N_CHIPS)], axis=1)
    yc = jnp.concatenate([_dot(cact, wcp_ref[s]) for s in range(N_CHIPS)], axis=1) + bcp_ref[...]
    return ya, yc


def _mix_out(x, o, cact, gl, w_ap, w_cp, b_cp, w_out, dep, begun=None):
    tm = ROW_TM
    steps = T // tm // 2
    first = 0 if begun is None else steps
    rows = lambda n: pl.BlockSpec((tm, n), lambda i: (i + first, 0))
    extra = [] if begun is None else list(begun)

    def body(x_ref, o_ref, c_ref, gl_ref, wap_ref, wcp_ref, bcp_ref, wo_ref, dep_ref, *rest):
        ya_ref, yc_ref, mg_ref, x1_ref = rest[len(extra):]
        ya, yc = _branch_outputs(o_ref[...], c_ref[...], wap_ref, wcp_ref, bcp_ref)
        g0 = _sigmoid(gl_ref[:, 0:D].astype(F32))
        g1 = _sigmoid(gl_ref[:, D:2 * D].astype(F32))
        mg = (g0 * ya + g1 * yc).astype(BF16)
        ya_ref[...] = ya.astype(BF16)
        yc_ref[...] = yc.astype(BF16)
        mg_ref[...] = mg
        x1_ref[...] = x_ref[...] + _dot(mg, wo_ref[...])

    return pl.pallas_call(
        body, name="mix_out_first" if begun is None else "mix_out_second", grid=(steps,),
        in_specs=[rows(D), rows(AW), rows(C), rows(2 * D),
                  _whole((N_CHIPS, AW, D // N_CHIPS)), _whole((N_CHIPS, C, D // N_CHIPS)), _whole((1, D)),
                  _whole((D, D)), _whole((8, 128))] + [ANY_SPEC] * len(extra),
        out_specs=[rows(D), rows(D), rows(D), rows(D)],
        out_shape=[jax.ShapeDtypeStruct((T, D), BF16), jax.ShapeDtypeStruct((T, D), BF16),
                   jax.ShapeDtypeStruct((T, D), BF16), jax.ShapeDtypeStruct((T, D), F32)],
        input_output_aliases={9 + k: k for k in range(len(extra))},
        compiler_params=_params("parallel"),
    )(x, o, cact, gl, w_ap, w_cp, b_cp, w_out, dep, *extra)


def _swiglu_half(h, wg_ref, wu_ref, gu_ref, act_ref):
    gate = _dot(h, wg_ref[0])
    up = _dot(h, wu_ref[0])
    gu_ref[:, 0:FSH] = gate.astype(BF16)
    gu_ref[:, FSH:2 * FSH] = up.astype(BF16)
    act_ref[...] = (gate * _sigmoid(gate) * up).astype(BF16)


def _shard_spec(k):
    return pl.BlockSpec((1, D, FSH), lambda i: (k, 0, 0), pipeline_mode=pl.Buffered(1))


def _ffn_in_first(x1, g_ffn, w_fi, dep):
    tm = ROW_TM

    def body(x_ref, g_ref, wg_ref, wu_ref, dep_ref, h_ref, gu_ref, act_ref):
        xv = x_ref[...]
        r = lax.rsqrt(jnp.mean(xv * xv, axis=-1, keepdims=True) + EPS)
        h = (xv * r * g_ref[...]).astype(BF16)
        h_ref[...] = h
        _swiglu_half(h, wg_ref, wu_ref, gu_ref, act_ref)

    return pl.pallas_call(
        body, name="ffn_in_first", grid=(T // tm,),
        in_specs=[_rows(tm, D), _whole((1, D)), _shard_spec(0), _shard_spec(2), _whole((8, 128))],
        out_specs=[_rows(tm, D), pl.BlockSpec((tm, 2 * FSH), lambda i: (i, 0)),
                   pl.BlockSpec((tm, FSH), lambda i: (i, 0))],
        out_shape=[jax.ShapeDtypeStruct((T, D), BF16), jax.ShapeDtypeStruct((T, 2 * DFF), BF16),
                   jax.ShapeDtypeStruct((T, DFF), BF16)],
        compiler_params=_params("parallel"),
    )(x1, g_ffn, w_fi, w_fi, dep)


def _ffn_in_second(h2, w_fi, gu, act, dep):
    tm = ROW_TM

    def body(h_ref, wg_ref, wu_ref, gu_in, act_in, dep_ref, gu_ref, act_ref):
        _swiglu_half(h_ref[...], wg_ref, wu_ref, gu_ref, act_ref)

    return pl.pallas_call(
        body, name="ffn_in_second", grid=(T // tm,),
        in_specs=[_rows(tm, D), _shard_spec(1), _shard_spec(3), ANY_SPEC, ANY_SPEC, _whole((8, 128))],
        out_specs=[pl.BlockSpec((tm, 2 * FSH), lambda i: (i, 1)), pl.BlockSpec((tm, FSH), lambda i: (i, 1))],
        out_shape=[jax.ShapeDtypeStruct((T, 2 * DFF), BF16), jax.ShapeDtypeStruct((T, DFF), BF16)],
        input_output_aliases={3: 0, 4: 1},
        compiler_params=_params("parallel"),
    )(h2, w_fi, w_fi, gu, act, dep)


def _ffn_out_loss(x1, act, w_dn, g_final, target):
    tm = ROW_TM // 2

    def body(x_ref, a_ref, w_ref, g_ref, t_ref, dx_ref, dxb_ref, dg_ref, loss_ref):
        i = pl.program_id(0)
        x2 = x_ref[...] + _dot(a_ref[...], w_ref[...])
        r = lax.rsqrt(jnp.mean(x2 * x2, axis=-1, keepdims=True) + EPS)
        xh = x2 * r
        g = g_ref[...]
        err = xh * g - t_ref[...]
        dy = err * (1.0 / D)
        dyg = dy * g
        dx = r * (dyg - xh * jnp.mean(dyg * xh, axis=-1, keepdims=True))
        dx_ref[...] = dx
        dxb_ref[...] = dx.astype(BF16)
        part = 0.5 * jnp.sum(jnp.mean(err * err, axis=-1, keepdims=True), axis=0, keepdims=True)

        @pl.when(i == 0)
        def _():
            dg_ref[...] = jnp.zeros_like(dg_ref)
            loss_ref[...] = jnp.zeros_like(loss_ref)

        dg_ref[...] += jnp.sum(dy * xh, axis=0, keepdims=True)
        loss_ref[...] += jnp.broadcast_to(part, loss_ref.shape)

    return pl.pallas_call(
        body, name="ffn_out_loss", grid=(T // tm,),
        in_specs=[_rows(tm, D), _rows(tm, DFF), _whole((DFF, D)), _whole((1, D)), _rows(tm, D)],
        out_specs=[_rows(tm, D), _rows(tm, D), _whole((1, D)), _whole((1, 128))],
        out_shape=[jax.ShapeDtypeStruct((T, D), F32), jax.ShapeDtypeStruct((T, D), BF16),
                   jax.ShapeDtypeStruct((1, D), F32), jax.ShapeDtypeStruct((1, 128), F32)],
        compiler_params=_params("arbitrary"),
    )(x1, act, w_dn, g_final, target)


def _const(shape):
    return pl.BlockSpec(shape, lambda i: tuple(0 for _ in shape), pipeline_mode=pl.Buffered(1))


def _ffn_bwd(dx2, dx2b, gu, x1, g_ffn, w_dn_t, w_fi_t):
    tm = ROW_TM // 2

    def body(dx_ref, dxb_ref, gu_ref, x_ref, g_ref, wdn_ref, wfi_ref,
             dgu_ref, dx1_ref, dx1b_ref, dg_ref):
        i = pl.program_id(0)
        dxb = dxb_ref[...]
        dh = jnp.zeros((tm, D), F32)
        dacts = [_dot_nt(dxb, wdn_ref[k * FSH:(k + 1) * FSH, :]) for k in range(N_CHIPS // 2)]
        for k in range(N_CHIPS // 2):
            c0 = k * FSH
            dact = dacts[k]
            gate = gu_ref[:, 2 * c0:2 * c0 + FSH].astype(F32)
            up = gu_ref[:, 2 * c0 + FSH:2 * c0 + 2 * FSH].astype(F32)
            s = _sigmoid(gate)
            dup = (dact * gate * s).astype(BF16)
            dgate = (dact * up * s * (1.0 + gate * (1.0 - s))).astype(BF16)
            dgu_ref[:, c0:c0 + FSH] = dgate
            dgu_ref[:, DFF + c0:DFF + c0 + FSH] = dup
            dh = dh + _dot_nt(dgate, wfi_ref[k]) + _dot_nt(dup, wfi_ref[k + N_CHIPS // 2])
        xv = x_ref[...]
        r = lax.rsqrt(jnp.mean(xv * xv, axis=-1, keepdims=True) + EPS)
        xh = xv * r
        dhg = dh * g_ref[...]
        dx1 = dx_ref[...] + r * (dhg - xh * jnp.mean(dhg * xh, axis=-1, keepdims=True))
        dx1_ref[...] = dx1
        dx1b_ref[...] = dx1.astype(BF16)

        @pl.when(i == 0)
        def _():
            dg_ref[...] = jnp.zeros_like(dg_ref)

        dg_ref[...] += jnp.sum(dh * xh, axis=0, keepdims=True)

    return pl.pallas_call(
        body, name="ffn_bwd", grid=(T // tm,),
        in_specs=[_rows(tm, D), _rows(tm, D), _rows(tm, 2 * DFF), _rows(tm, D), _whole((1, D)),
                  _const((DFF, D)), _const((N_CHIPS, D, FSH))],
        out_specs=[_rows(tm, 2 * DFF), _rows(tm, D), _rows(tm, D), _whole((1, D))],
        out_shape=[jax.ShapeDtypeStruct((T, 2 * DFF), BF16), jax.ShapeDtypeStruct((T, D), F32),
                   jax.ShapeDtypeStruct((T, D), BF16), jax.ShapeDtypeStruct((1, D), F32)],
        compiler_params=_params("arbitrary"),
    )(dx2, dx2b, gu, x1, g_ffn, w_dn_t, w_fi_t)


def _mix_bwd(dx1b, gl, ya, yc, w_out, w_ap, w_cp, dep):
    tm = ROW_TM

    def body(dx_ref, gl_ref, ya_ref, yc_ref, wo_ref, wap_ref, wcp_ref, dep_ref,
             dya_ref, dyc_ref, dgl_ref, do_ref, dc_ref, db_ref):
        i = pl.program_id(0)
        dm = _dot_nt(dx_ref[...], wo_ref[...])
        ya, yc = ya_ref[...].astype(F32), yc_ref[...].astype(F32)
        g0 = _sigmoid(gl_ref[:, 0:D].astype(F32))
        g1 = _sigmoid(gl_ref[:, D:2 * D].astype(F32))
        dya = dm * g0
        dyc = dm * g1
        dgl_ref[:, 0:D] = (dya * ya * (1.0 - g0)).astype(BF16)
        dgl_ref[:, D:2 * D] = (dyc * yc * (1.0 - g1)).astype(BF16)
        dyab = dya.astype(BF16)
        dycb = dyc.astype(BF16)
        dya_ref[...] = dyab
        dyc_ref[...] = dycb
        sw = D // N_CHIPS
        do = jnp.zeros((tm, AW), F32)
        dcv = jnp.zeros((tm, C), F32)
        for s in range(N_CHIPS):
            do = do + _dot_nt(dyab[:, s * sw:(s + 1) * sw], wap_ref[s])
            dcv = dcv + _dot_nt(dycb[:, s * sw:(s + 1) * sw], wcp_ref[s])
        do_ref[...] = do.astype(BF16)
        dc_ref[...] = dcv.astype(BF16)

        @pl.when(i == 0)
        def _():
            db_ref[...] = jnp.zeros_like(db_ref)

        db_ref[...] += jnp.sum(dyc, axis=0, keepdims=True)

    return pl.pallas_call(
        body, name="mix_bwd", grid=(T // tm,),
        in_specs=[_rows(tm, D), _rows(tm, 2 * D), _rows(tm, D), _rows(tm, D),
                  _whole((D, D)), _whole((N_CHIPS, AW, D // N_CHIPS)), _whole((N_CHIPS, C, D // N_CHIPS)),
                  _whole((8, 128))],
        out_specs=[_rows(tm, D), _rows(tm, D), _rows(tm, 2 * D), _rows(tm, AW), _rows(tm, C),
                   _whole((1, D))],
        out_shape=[jax.ShapeDtypeStruct((T, D), BF16), jax.ShapeDtypeStruct((T, D), BF16),
                   jax.ShapeDtypeStruct((T, 2 * D), BF16), jax.ShapeDtypeStruct((T, AW), BF16),
                   jax.ShapeDtypeStruct((T, C), BF16), jax.ShapeDtypeStruct((1, D), F32)],
        compiler_params=_params("arbitrary"),
    )(dx1b, gl, ya, yc, w_out, w_ap, w_cp, dep)


def _conv_bwd(glu, u, dc, conv_w, ln_g, ln_b, dep):
    tm = CONV_TM
    nblk = T // tm

    def du_of(uv, dcv, g_ref, b_ref):
        mu = jnp.mean(uv, axis=-1, keepdims=True)
        xc = uv - mu
        var = jnp.mean(xc * xc, axis=-1, keepdims=True)
        rstd = lax.rsqrt(var + EPS)
        xh = xc * rstd
        y = xh * g_ref[...] + b_ref[...]
        sg = _sigmoid(y)
        dy = dcv * (sg * (1.0 + y * (1.0 - sg)))
        dxh = dy * g_ref[...]
        du = rstd * (dxh - jnp.mean(dxh, axis=-1, keepdims=True)
                     - xh * jnp.mean(dxh * xh, axis=-1, keepdims=True))
        return du, dy, xh

    def body(cur_ref, prev_ref, u_ref, un_ref, dc_ref, dcn_ref, w_ref, g_ref, b_ref, dep_ref,
             dglu_ref, dw_ref, dcb_ref, dg_ref, db_ref, zs_ref, dus_ref):
        i = pl.program_id(0)

        @pl.when(i == 0)
        def _():
            dw_ref[...] = jnp.zeros_like(dw_ref)
            dcb_ref[...] = jnp.zeros_like(dcb_ref)
            dg_ref[...] = jnp.zeros_like(dg_ref)
            db_ref[...] = jnp.zeros_like(db_ref)

        zprev = _glu(prev_ref[tm - HALO:tm, :])
        zs_ref[0, 0:HALO, :] = jnp.where(i > 0, zprev, 0.0)
        zs_ref[0, HALO:HALO + tm, :] = _glu(cur_ref[...])
        _shifted_copies(zs_ref)

        dun, _, _ = du_of(un_ref[0:HALO, :], dcn_ref[0:HALO, :].astype(F32), g_ref, b_ref)
        dus_ref[0, tm:tm + HALO, :] = jnp.where(i < nblk - 1, dun, 0.0)
        dg_acc = jnp.zeros((1, C), F32)
        db_acc = jnp.zeros((1, C), F32)
        dcb_acc = jnp.zeros((1, C), F32)
        for s in range(tm // CONV_SUB):
            rows = slice(s * CONV_SUB, (s + 1) * CONV_SUB)
            du, dy, xh = du_of(u_ref[rows, :], dc_ref[rows, :].astype(F32), g_ref, b_ref)
            dus_ref[0, rows, :] = du
            dg_acc = dg_acc + jnp.sum(dy * xh, axis=0, keepdims=True)
            db_acc = db_acc + jnp.sum(dy, axis=0, keepdims=True)
            dcb_acc = dcb_acc + jnp.sum(du, axis=0, keepdims=True)
        dg_ref[...] += dg_acc
        db_ref[...] += db_acc
        dcb_ref[...] += dcb_acc
        _shifted_copies(dus_ref)

        for j in range(KW):
            acc = jnp.zeros((CONV_SUB, C), F32)
            for s in range(tm // CONV_SUB):
                base = HALO + s * CONV_SUB - (KW - 1) + j
                acc = acc + dus_ref[0, s * CONV_SUB:(s + 1) * CONV_SUB, :] * _shifted_rows(zs_ref, base, CONV_SUB)
            dw_ref[j:j + 1, :] += jnp.sum(acc, axis=0, keepdims=True)

        for s in range(tm // CONV_SUB):
            rows = slice(s * CONV_SUB, (s + 1) * CONV_SUB)
            dz = jnp.zeros((CONV_SUB, C), F32)
            for j in range(KW):
                dz = dz + w_ref[j:j + 1, :] * _shifted_rows(dus_ref, s * CONV_SUB + (KW - 1) - j, CONV_SUB)
            a = cur_ref[rows, 0:C].astype(F32)
            sb = _sigmoid(cur_ref[rows, C:2 * C].astype(F32))
            dglu_ref[rows, 0:C] = (dz * sb).astype(BF16)
            dglu_ref[rows, C:2 * C] = (dz * a * sb * (1.0 - sb)).astype(BF16)

    nxt = lambda i: (jnp.minimum(i + 1, nblk - 1), 0)
    return pl.pallas_call(
        body, name="conv_bwd", grid=(nblk,),
        in_specs=[_rows(tm, 2 * C),
                  pl.BlockSpec((tm, 2 * C), lambda i: (jnp.maximum(i - 1, 0), 0)),
                  _rows(tm, C), pl.BlockSpec((tm, C), nxt),
                  _rows(tm, C), pl.BlockSpec((tm, C), nxt),
                  _whole((KW, C)), _whole((1, C)), _whole((1, C)), _whole((8, 128))],
        out_specs=[_rows(tm, 2 * C), _whole((KW, C)), _whole((1, C)), _whole((1, C)), _whole((1, C))],
        out_shape=[jax.ShapeDtypeStruct((T, 2 * C), BF16), jax.ShapeDtypeStruct((KW, C), F32),
                   jax.ShapeDtypeStruct((1, C), F32), jax.ShapeDtypeStruct((1, C), F32),
                   jax.ShapeDtypeStruct((1, C), F32)],
        scratch_shapes=[pltpu.VMEM((SUBLANES, HALO + tm, C), F32), pltpu.VMEM((SUBLANES, tm + HALO, C), F32)],
        compiler_params=_params("arbitrary"),
    )(glu, glu, u, u, dc, dc, conv_w, ln_g, ln_b, dep)


def _attn_bwd(qkv, do, lse, p, sinks, *deps):
    nsub = 1

    def body(sink_ref, qkv_ref, do_ref, lse_ref, p_ref, *rest):
        dq_ref, dkv_ref, ds_ref, dp_ref, dsb_ref = rest[len(deps):]
        i = pl.program_id(0)

        @pl.when(i == 0)
        def _():
            dkv_ref[...] = jnp.zeros_like(dkv_ref)
            ds_ref[...] = jnp.zeros_like(ds_ref)

        slots = [(sub, g) for sub in range(nsub) for g in range(NKV)]
        bands = [_band(i * nsub + sub) for sub in range(nsub)]
        r0s = [pl.multiple_of((i * nsub + sub) * BLK, BLK) for sub in range(nsub)]
        lses = []
        for n, (sub, g) in enumerate(slots):
            rb, blk = bands[sub][0], slice(sub * BLK, (sub + 1) * BLK)
            vband = qkv_ref[pl.ds(rb, BAND), AW + KVW + g * HD:AW + KVW + (g + 1) * HD]
            lse_parts = []
            for hh in range(GROUP):
                h = g * GROUP + hh
                dp_ref[n, hh * BLK:(hh + 1) * BLK, :] = _dot_nt(do_ref[blk, h * HD:(h + 1) * HD], vband)
                lse_parts.append(lse_ref[blk, h:h + 1])
            lses.append(jnp.concatenate(lse_parts, axis=0))
        dsinks = []
        for n, (sub, g) in enumerate(slots):
            pf, dpv = p_ref[n].astype(F32), dp_ref[n]
            dl = jnp.sum(pf * dpv, axis=-1, keepdims=True)
            dsb_ref[n] = (pf * (dpv - dl)).astype(BF16)
            dsinks.append(-(jnp.exp(_sink_column(sink_ref, g) - lses[n]) * dl))
        for n, (sub, g) in enumerate(slots):
            rb, blk = bands[sub][0], slice(sub * BLK, (sub + 1) * BLK)
            kband = qkv_ref[pl.ds(rb, BAND), AW + g * HD:AW + (g + 1) * HD]
            dk = jnp.zeros((BAND, HD), F32)
            dv = jnp.zeros((BAND, HD), F32)
            for hh in range(GROUP):
                h = g * GROUP + hh
                hcol = slice(h * HD, (h + 1) * HD)
                rows = slice(hh * BLK, (hh + 1) * BLK)
                dq_ref[blk, hcol] = (_dot(dsb_ref[n, rows, :], kband) * SCALE).astype(BF16)
                dk = dk + _dot_tn(dsb_ref[n, rows, :], qkv_ref[pl.ds(r0s[sub], BLK), hcol])
                dv = dv + _dot_tn(p_ref[n, rows, :], do_ref[blk, hcol])
                ds_ref[h:h + 1, :] += jnp.broadcast_to(
                    jnp.sum(dsinks[n][rows], axis=0, keepdims=True), (1, 128))
            dkv_ref[pl.ds(rb, BAND), g * HD:(g + 1) * HD] += dk * SCALE
            dkv_ref[pl.ds(rb, BAND), KVW + g * HD:KVW + (g + 1) * HD] += dv

    nslot, tq = nsub * NKV, nsub * BLK
    return pl.pallas_call(
        body, name="attn_bwd", grid=(T // tq,),
        in_specs=[pl.BlockSpec(memory_space=pltpu.SMEM), _whole((T, QKVW)),
                  _rows(tq, AW), _rows(tq, NQ),
                  pl.BlockSpec((nslot, GROWS, BAND), lambda i: (i, 0, 0))] + [_whole((8, 128))] * len(deps),
        out_specs=[_rows(tq, AW), _whole((T, 2 * KVW)), _whole((NQ, 128))],
        out_shape=[jax.ShapeDtypeStruct((T, AW), BF16), jax.ShapeDtypeStruct((T, 2 * KVW), F32),
                   jax.ShapeDtypeStruct((NQ, 128), F32)],
        scratch_shapes=[pltpu.VMEM((nslot, GROWS, BAND), F32), pltpu.VMEM((nslot, GROWS, BAND), BF16)],
        compiler_params=_params("arbitrary"),
    )(sinks, qkv, do, lse, p, *deps)


PROJ_PARTS = [(0, AW), (AW, QKVW), (GLU_OFF, GATE_OFF), (GATE_OFF, INW)]


def _in_proj_bwd(dq, dkv, dglu, dgl, x, dx1, g_mix, w_in_t, dep):
    tm = ROW_TM

    def body(dq_ref, dkv_ref, dglu_ref, dgl_ref, x_ref, dx1_ref, g_ref, w_ref, dep_ref,
             gx_ref, dg_ref, db_ref):
        i = pl.program_id(0)

        @pl.when(i == 0)
        def _():
            dg_ref[...] = jnp.zeros_like(dg_ref)
            db_ref[...] = jnp.zeros_like(db_ref)

        dh = jnp.zeros((tm, D), F32)
        for part_ref, (lo, hi) in zip((dq_ref, dkv_ref, dglu_ref, dgl_ref), PROJ_PARTS):
            part = part_ref[...]
            dh = dh + _dot(part.astype(BF16), w_ref[lo:hi, :])
            db_ref[:, lo:hi] += jnp.sum(part.astype(F32), axis=0, keepdims=True)
        xv = x_ref[...]
        r = lax.rsqrt(jnp.mean(xv * xv, axis=-1, keepdims=True) + EPS)
        xh = xv * r
        dhg = dh * g_ref[...]
        gx_ref[...] = dx1_ref[...] + r * (dhg - xh * jnp.mean(dhg * xh, axis=-1, keepdims=True))
        dg_ref[...] += jnp.sum(dh * xh, axis=0, keepdims=True)

    return pl.pallas_call(
        body, name="in_proj_bwd", grid=(T // tm,),
        in_specs=[_rows(tm, AW), _rows(tm, 2 * KVW), _rows(tm, 2 * C), _rows(tm, 2 * D),
                  _rows(tm, D), _rows(tm, D), _whole((1, D)), _const((INW, D)), _whole((8, 128))],
        out_specs=[_rows(tm, D), _whole((1, D)), _whole((1, INW))],
        out_shape=[jax.ShapeDtypeStruct((T, D), F32), jax.ShapeDtypeStruct((1, D), F32),
                   jax.ShapeDtypeStruct((1, INW), F32)],
        compiler_params=_params("arbitrary"),
    )(dq, dkv, dglu, dgl, x, dx1, g_mix, w_in_t, dep)


def _grad_w_in_t(h, dq, dkv, dglu, dgl):
    tn, chunk = 512, 256

    def body(h_ref, dq_ref, dkv_ref, dglu_ref, dgl_ref, o_ref):
        hv = h_ref[...]
        for part_ref, (lo, hi) in zip((dq_ref, dkv_ref, dglu_ref, dgl_ref), PROJ_PARTS):
            for c0 in range(0, hi - lo, chunk):
                o_ref[lo + c0:lo + c0 + chunk, :] = _dot_tn(
                    part_ref[:, c0:c0 + chunk].astype(BF16), hv).astype(BF16)

    return pl.pallas_call(
        body, name="grad_w_in", grid=(D // tn,),
        in_specs=[pl.BlockSpec((T, tn), lambda j: (0, j)), _const((T, AW)), _const((T, 2 * KVW)),
                  _const((T, 2 * C)), _const((T, 2 * D))],
        out_specs=pl.BlockSpec((INW, tn), lambda j: (0, j)),
        out_shape=jax.ShapeDtypeStruct((INW, D), BF16),
        compiler_params=_params("parallel"),
    )(h, dq, dkv, dglu, dgl)


def _grad_w(a, b, name, tk, tn, col_sharded, dep=None):
    k, n = a.shape[1], b.shape[1]

    single = n == tn
    sw = n // N_CHIPS
    deps = [] if dep is None else [dep]

    def body(a_ref, b_ref, *rest):
        o_ref, at_ref = rest[len(deps):]
        if single:
            res = _dot_tn(a_ref[...], b_ref[...]).astype(BF16)
            if col_sharded:
                for s in range(N_CHIPS):
                    o_ref[s] = res[:, s * sw:(s + 1) * sw]
            else:
                o_ref[...] = res
            return

        @pl.when(pl.program_id(1) == 0)
        def _():
            at_ref[...] = a_ref[...].T

        o_ref[...] = _dot(at_ref[...], b_ref[...]).astype(BF16)

    if col_sharded and single:
        shape = (N_CHIPS, k, sw)
        out_spec = pl.BlockSpec((N_CHIPS, tk, sw), lambda i, j: (0, i, 0))
    elif col_sharded:
        per = sw // tn
        shape = (N_CHIPS, k, sw)
        out_spec = pl.BlockSpec((None, tk, tn), lambda i, j: (j // per, i, j % per))
    else:
        shape = (1, k, n)
        out_spec = pl.BlockSpec((None, tk, tn), lambda i, j: (0, i, j))
    out = pl.pallas_call(
        body, name=name, grid=(k // tk, n // tn),
        in_specs=[pl.BlockSpec((T, tk), lambda i, j: (0, i)), pl.BlockSpec((T, tn), lambda i, j: (0, j))]
        + [ANY_SPEC] * len(deps),
        out_specs=out_spec,
        out_shape=jax.ShapeDtypeStruct(shape, BF16),
        scratch_shapes=[pltpu.VMEM((tk, T), BF16)],
        compiler_params=_params("parallel", "arbitrary"),
    )(a, b, *deps)
    return out if col_sharded else out.reshape(N_CHIPS, k // N_CHIPS, n)


HBM_SPEC = pl.BlockSpec(memory_space=pltpu.HBM)


def _place():
    x, y, c = lax.axis_index("x"), lax.axis_index("y"), lax.axis_index("c")
    chips = [(1 - x, y), (x, 1 - y), (1 - x, 1 - y)]
    return x, y, c, chips


SEM_SPEC = pl.BlockSpec(memory_space=pltpu.SEMAPHORE)
ANY_SPEC = pl.BlockSpec(memory_space=pl.ANY)
VMEM_SPEC = pl.BlockSpec(memory_space=pltpu.VMEM)
EFFECT = pltpu.SideEffectType.DATAFLOW_SIDE_EFFECTING


def _gather_ends(src, land, x, y, c, chips):
    kh = src.shape[0] // 2
    s_me = 2 * x + y
    ends = [(src.at[pl.ds(c * kh, kh)], land.at[s_me, pl.ds(c * kh, kh)], (*chip, c)) for chip in chips]
    return ends + [(src, land.at[s_me], (x, y, 1 - c))]


def _reduce_ends(src, land, x, y, c, chips):
    return [(src.at[2 * chip[0] + chip[1]], land.at[j], (*chip, c)) for j, chip in enumerate(chips)]


def _chip_copies(ends, srcs, lands, send_sems, recv_sems, first=0):
    x, y, c, chips = _place()
    copies = []
    for src, land in zip(srcs, lands):
        peers = ends(src, land, x, y, c, chips)
        for s, d, to in peers:
            k = first * len(peers) + len(copies)
            copies.append(pltpu.make_async_remote_copy(
                src_ref=s, dst_ref=d, send_sem=send_sems.at[k], recv_sem=recv_sems.at[k],
                device_id=to, device_id_type=MESH))
    return copies


GATHER_PEERS, REDUCE_PEERS = 4, 3


def _handshake(shake):
    x, y, c, chips = _place()
    peers = ([(x, y, 1 - c)] if shake in ("pair", "both") else []) + (
        [(*chip, c) for chip in chips] if shake in ("chips", "both") else [])
    barrier = pltpu.get_barrier_semaphore()
    for peer in peers:
        pl.semaphore_signal(barrier, inc=1, device_id=peer, device_id_type=MESH)
    pl.semaphore_wait(barrier, len(peers))


def _chip_start(name, ends, peers, srcs, lands, shake=None):
    n = len(srcs)

    def body(*refs):
        if shake is not None:
            _handshake(shake[0])
        copies = _chip_copies(ends, refs[:n], refs[n:2 * n], refs[2 * n], refs[2 * n + 1])
        for cp in copies:
            cp.start()
        token = refs[-1]
        token[...] = jnp.zeros_like(token)

    hbm = lambda a: pltpu.HBM(a.shape, a.dtype)
    res = pl.pallas_call(
        body, name=name,
        out_shape=(pltpu.SemaphoreType.DMA((peers * n,)), pltpu.SemaphoreType.DMA((peers * n,)),
                   *[hbm(a) for a in srcs], *[hbm(a) for a in lands],
                   jax.ShapeDtypeStruct((8, 128), F32)),
        in_specs=[HBM_SPEC] * (2 * n),
        out_specs=(SEM_SPEC, SEM_SPEC, *[HBM_SPEC] * (2 * n), VMEM_SPEC),
        input_output_aliases={i: 2 + i for i in range(2 * n)},
        compiler_params=pltpu.CompilerParams(has_side_effects=EFFECT,
                                             collective_id=None if shake is None else shake[1]),
    )(*[pltpu.with_memory_space_constraint(a, pltpu.HBM) for a in (*srcs, *lands)])
    return res[0], res[1], list(res[2:2 + n]), list(res[2 + n:2 + 2 * n]), res[-1]


def _chip_wait(name, ends, send_sems, recv_sems, srcs, lands, after, first=0):
    n, na = len(srcs), len(after)

    def body(*refs):
        copies = _chip_copies(ends, refs[:n], refs[n:2 * n], refs[2 * n], refs[2 * n + 1], first)
        for cp in copies:
            cp.wait_send()
            cp.wait_recv()

    hbm = lambda a: pltpu.HBM(a.shape, a.dtype)
    res = pl.pallas_call(
        body, name=name,
        out_shape=tuple(hbm(a) for a in (*srcs, *lands)),
        in_specs=[HBM_SPEC] * (2 * n) + [SEM_SPEC, SEM_SPEC] + [ANY_SPEC] * na,
        out_specs=tuple([HBM_SPEC] * (2 * n)),
        input_output_aliases={i: i for i in range(2 * n)},
        compiler_params=pltpu.CompilerParams(has_side_effects=EFFECT),
    )(*srcs, *lands, send_sems, recv_sems, *after)
    return list(res[:n]), list(res[n:])


def _forward_copies(lands, send_sems, recv_sems):
    x, y, c, chips = _place()
    copies = []
    for land in lands:
        kh = land.shape[1] // 2
        for chip in chips:
            blk = land.at[2 * chip[0] + chip[1], pl.ds(c * kh, kh)]
            k = len(copies)
            copies.append(pltpu.make_async_remote_copy(
                src_ref=blk, dst_ref=blk, send_sem=send_sems.at[k], recv_sem=recv_sems.at[k],
                device_id=(x, y, 1 - c), device_id_type=MESH))
    return copies


def _gather_relay(name, send_sems, recv_sems, srcs, lands, after, first, shake):
    n, na = len(srcs), len(after)

    def body(*refs):
        _handshake(shake[0])
        land_refs = refs[n:2 * n]
        for cp in _chip_copies(_gather_ends, refs[:n], land_refs, refs[2 * n], refs[2 * n + 1], first):
            cp.wait_send()
            cp.wait_recv()
        out = refs[2 * n + 2 + na:]
        for cp in _forward_copies(land_refs, out[0], out[1]):
            cp.start()
        out[-1][...] = jnp.zeros_like(out[-1])

    hbm = lambda a: pltpu.HBM(a.shape, a.dtype)
    res = pl.pallas_call(
        body, name=name,
        out_shape=(pltpu.SemaphoreType.DMA((3 * n,)), pltpu.SemaphoreType.DMA((3 * n,)),
                   *[hbm(a) for a in lands], jax.ShapeDtypeStruct((8, 128), F32)),
        in_specs=[HBM_SPEC] * (2 * n) + [SEM_SPEC, SEM_SPEC] + [ANY_SPEC] * na,
        out_specs=(SEM_SPEC, SEM_SPEC, *[HBM_SPEC] * n, VMEM_SPEC),
        input_output_aliases={n + i: 2 + i for i in range(n)},
        compiler_params=pltpu.CompilerParams(has_side_effects=EFFECT, collective_id=shake[1]),
    )(*srcs, *lands, send_sems, recv_sems, *after)
    return res[0], res[1], list(res[2:2 + n]), res[-1]


def _forward_wait(name, send_sems, recv_sems, lands, after):
    n, na = len(lands), len(after)

    def body(*refs):
        for cp in _forward_copies(refs[:n], refs[n], refs[n + 1]):
            cp.wait_send()
            cp.wait_recv()

    hbm = lambda a: pltpu.HBM(a.shape, a.dtype)
    res = pl.pallas_call(
        body, name=name,
        out_shape=tuple(hbm(a) for a in lands),
        in_specs=[HBM_SPEC] * n + [SEM_SPEC, SEM_SPEC] + [ANY_SPEC] * na,
        out_specs=tuple([HBM_SPEC] * n),
        input_output_aliases={i: i for i in range(n)},
        compiler_params=pltpu.CompilerParams(has_side_effects=EFFECT),
    )(*lands, send_sems, recv_sems, *after)
    return list(res)


def _exchange_ends(src, land, x, y, c, chips):
    kh = src.shape[1] // 2
    return [(src.at[:, pl.ds((1 - c) * kh, kh)], land, (x, y, 1 - c))]


def _share_ends(src, land, x, y, c, chips):
    return [(src, land, (x, y, 1 - c))]


def _small_ends(src, land, x, y, c, chips):
    m = src.shape[0]
    rows = land.at[pl.ds((4 * x + 2 * y + c) * m, m)]
    peers = [(x, y, 1 - c)] + [(*chip, c) for chip in chips] + [(*chip, 1 - c) for chip in chips]
    return [(src, rows, to) for to in peers]


PAIR_PEERS, SMALL_PEERS = 1, 7
SHAKES = {"pair_1": ("pair", 0), "pair_2": ("pair", 1), "share_1": ("pair", 2), "share_2": ("pair", 3),
          "chip_1": ("chips", 4), "chip_2": ("chips", 5), "gather_b": ("both", 6),
          "relay_a": ("pair", 7), "relay_b1": ("pair", 8), "relay_b2": ("pair", 9), "relay_b3": ("pair", 10),
          "pair_0": ("pair", 11), "chip_0": ("chips", 12), "share_0": ("pair", 13)}


def _row_tile(k):
    for t in (256, 240, 128, 176, 64, 32, 16):
        if k % t == 0:
            return t
    raise ValueError(k)


def _pair_sum_small(c_idx, items, name):
    n = len(items)

    def body(c_ref, *refs):
        for k in range(n):
            g_ref, r_ref, o_ref = refs[2 * k], refs[2 * k + 1], refs[2 * n + k]
            o_ref[...] = (g_ref[...].astype(F32) + r_ref[...].astype(F32)).astype(BF16)

    in_specs, out_specs, out_shape, flat = [], [], [], []
    for g, got in items:
        _, kh, w = got.shape
        in_specs += [pl.BlockSpec((N_CHIPS, kh, w), lambda i, c_ref: (0, c_ref[0], 0)),
                     pl.BlockSpec((N_CHIPS, kh, w), lambda i, c_ref: (0, 0, 0))]
        out_specs.append(pl.BlockSpec((N_CHIPS, kh, w), lambda i, c_ref: (0, 0, 0)))
        out_shape.append(jax.ShapeDtypeStruct((N_CHIPS, kh, w), BF16))
        flat += [g, got]
    return pl.pallas_call(
        body, name=name,
        grid_spec=pltpu.PrefetchScalarGridSpec(num_scalar_prefetch=1, grid=(1,), in_specs=in_specs,
                                               out_specs=out_specs),
        out_shape=out_shape, compiler_params=_params("arbitrary"),
    )(c_idx, *flat)


def _pair_sum(c_idx, g, got, name):
    _, k, n = g.shape
    kh = k // 2
    tm = _row_tile(kh)
    nb = kh // tm

    def body(c_ref, g_ref, r_ref, o_ref):
        o_ref[...] = (g_ref[...].astype(F32) + r_ref[...].astype(F32)).astype(BF16)

    return pl.pallas_call(
        body, name=name,
        grid_spec=pltpu.PrefetchScalarGridSpec(
            num_scalar_prefetch=1, grid=(nb,),
            in_specs=[pl.BlockSpec((N_CHIPS, tm, n), lambda i, c_ref: (0, c_ref[0] * nb + i, 0)),
                      pl.BlockSpec((N_CHIPS, tm, n), lambda i, c_ref: (0, i, 0))],
            out_specs=pl.BlockSpec((N_CHIPS, tm, n), lambda i, c_ref: (0, i, 0))),
        out_shape=jax.ShapeDtypeStruct((N_CHIPS, kh, n), BF16),
        compiler_params=_params("parallel"),
    )(c_idx, g, got)


def _chip_sum_small(s_idx, items, name):
    n = len(items)

    def body(s_ref, *refs):
        for k in range(n):
            m_ref, r_ref, o_ref = refs[2 * k], refs[2 * k + 1], refs[2 * n + k]
            acc = m_ref[0].astype(F32)
            for j in range(3):
                acc = acc + r_ref[j].astype(F32)
            o_ref[...] = acc

    in_specs, out_specs, out_shape, flat = [], [], [], []
    for mine, got in items:
        _, kh, w = mine.shape
        in_specs += [pl.BlockSpec((1, kh, w), lambda i, s_ref: (s_ref[0], 0, 0)),
                     pl.BlockSpec((3, kh, w), lambda i, s_ref: (0, 0, 0))]
        out_specs.append(pl.BlockSpec((kh, w), lambda i, s_ref: (0, 0)))
        out_shape.append(jax.ShapeDtypeStruct((kh, w), F32))
        flat += [mine, got]
    return pl.pallas_call(
        body, name=name,
        grid_spec=pltpu.PrefetchScalarGridSpec(num_scalar_prefetch=1, grid=(1,), in_specs=in_specs,
                                               out_specs=out_specs),
        out_shape=out_shape, compiler_params=_params("arbitrary"),
    )(s_idx, *flat)


def _chip_sum(s_idx, mine, got, name):
    _, kh, n = mine.shape
    tm = _row_tile(kh)

    def body(s_ref, m_ref, r_ref, o_ref):
        acc = m_ref[0].astype(F32)
        for j in range(3):
            acc = acc + r_ref[j].astype(F32)
        o_ref[...] = acc

    return pl.pallas_call(
        body, name=name,
        grid_spec=pltpu.PrefetchScalarGridSpec(
            num_scalar_prefetch=1, grid=(kh // tm,),
            in_specs=[pl.BlockSpec((1, tm, n), lambda i, s_ref: (s_ref[0], i, 0)),
                      pl.BlockSpec((3, tm, n), lambda i, s_ref: (0, i, 0))],
            out_specs=pl.BlockSpec((tm, n), lambda i, s_ref: (i, 0))),
        out_shape=jax.ShapeDtypeStruct((kh, n), F32),
        compiler_params=_params("parallel"),
    )(s_idx, mine, got)


ADAMW_TILES = 2


def _adamw_math(w, g, m, v):
    m = ADAM_B1 * m + (1.0 - ADAM_B1) * g
    v = ADAM_B2 * v + (1.0 - ADAM_B2) * (g * g)
    m_hat = m / (1.0 - ADAM_B1 ** ADAM_STEP)
    v_hat = v / (1.0 - ADAM_B2 ** ADAM_STEP)
    delta = -ADAM_LR * (m_hat / (jnp.sqrt(v_hat) + ADAM_EPS) + ADAM_WD * w)
    return delta, m, v


def _adamw_small(c_idx, items, name):
    n = len(items)

    def body(c_ref, *refs):
        mine = pl.program_id(0) == c_ref[0]
        for k in range(n):
            w_ref, gm_ref, go_ref, m_ref, v_ref = refs[5 * k:5 * k + 5]
            g_ref, d_ref, mo_ref, vo_ref = refs[5 * n + 4 * k:5 * n + 4 * k + 4]
            g = jnp.where(mine, gm_ref[...], go_ref[...])
            d, mm, vv = _adamw_math(w_ref[...], g, m_ref[...], v_ref[...])
            g_ref[...] = g
            d_ref[...] = d
            mo_ref[...] = mm
            vo_ref[...] = vv

    in_specs, out_specs, out_shape, flat = [], [], [], []
    for w, g_mine, g_other, m, v in items:
        rows, cols = w.shape
        tm = rows // (2 * ADAMW_TILES)
        full = pl.BlockSpec((tm, cols), lambda h, i, c_ref: (ADAMW_TILES * h + i, 0))
        own = pl.BlockSpec((tm, cols), lambda h, i, c_ref: (jnp.where(h == c_ref[0], i, 0), 0))
        other = pl.BlockSpec((tm, cols), lambda h, i, c_ref: (jnp.where(h == c_ref[0], 0, i), 0))
        in_specs += [full, own, other, full, full]
        out_specs += [full] * 4
        out_shape += [jax.ShapeDtypeStruct((rows, cols), F32)] * 4
        flat += [w, g_mine, g_other, m, v]
    res = pl.pallas_call(
        body, name=name,
        grid_spec=pltpu.PrefetchScalarGridSpec(num_scalar_prefetch=1, grid=(2, ADAMW_TILES), in_specs=in_specs,
                                               out_specs=out_specs),
        out_shape=out_shape, compiler_params=_params("arbitrary", "arbitrary"),
    )(c_idx, *flat)
    return [tuple(res[4 * k:4 * k + 4]) for k in range(n)]


def _adamw(c_idx, w, g_mine, g_other, m, v, name):
    k, n = w.shape
    nt = ADAMW_TILES
    tm = k // (2 * nt)

    def body(c_ref, w_ref, gm_ref, go_ref, m_ref, v_ref, g_ref, d_ref, mo_ref, vo_ref):
        g = jnp.where(pl.program_id(0) == c_ref[0], gm_ref[...], go_ref[...])
        d, mm, vv = _adamw_math(w_ref[...], g, m_ref[...], v_ref[...])
        g_ref[...] = g
        d_ref[...] = d
        mo_ref[...] = mm
        vo_ref[...] = vv

    full = pl.BlockSpec((tm, n), lambda h, i, c_ref: (nt * h + i, 0))
    mine = pl.BlockSpec((tm, n), lambda h, i, c_ref: (jnp.where(h == c_ref[0], i, 0), 0))
    other = pl.BlockSpec((tm, n), lambda h, i, c_ref: (jnp.where(h == c_ref[0], 0, i), 0))
    shp = jax.ShapeDtypeStruct((k, n), F32)
    return pl.pallas_call(
        body, name=name,
        grid_spec=pltpu.PrefetchScalarGridSpec(
            num_scalar_prefetch=1, grid=(2, nt),
            in_specs=[full, mine, other, full, full], out_specs=[full] * 4),
        out_shape=[shp] * 4, compiler_params=_params("arbitrary", "arbitrary"),
    )(c_idx, w, g_mine, g_other, m, v)


VEC_SLOTS = {
    "g_mix_norm": (0, 0, D), "b_conv_proj": (0, D, D), "g_ffn_norm": (0, 2 * D, D),
    "g_final": (0, 3 * D, D), "b_in": (1, 0, INW), "conv_b": (2, 0, C), "ln_g": (2, C, C),
    "ln_b": (2, 2 * C, C), "sinks": (2, 3 * C, NQ), "loss": (2, 3 * C + 128, 1),
}
VEC_ROWS, VEC_COLS = 8, 4 * D
CW_ROWS = 32
SMALL_NAMES = ["g_mix_norm", "b_in", "sinks", "conv_w", "conv_b", "ln_g", "ln_b",
               "b_conv_proj", "g_ffn_norm", "g_final"]
CW_LANES = C // N_CHIPS


def _pack_small(gs, loss):
    row0 = jnp.concatenate([gs["g_mix_norm"], gs["b_conv_proj"], gs["g_ffn_norm"], gs["g_final"]], axis=1)
    row1 = jnp.pad(gs["b_in"], ((0, 0), (0, VEC_COLS - INW)))
    row2 = jnp.concatenate([gs["conv_b"], gs["ln_g"], gs["ln_b"],
                            jnp.pad(gs["sinks"], ((0, 0), (0, 128 - NQ))),
                            jnp.pad(loss.reshape(1, 1), ((0, 0), (0, VEC_COLS - 3 * C - 129)))], axis=1)
    vec = jnp.concatenate([row0, row1, row2, jnp.zeros((VEC_ROWS - 3, VEC_COLS), F32)], axis=0)
    cw = jnp.pad(gs["conv_w"], ((0, CW_ROWS - KW), (0, 0)))
    return vec, cw


def _small_update(idx, vec_own, cw_own, vec_all, cw_all, wmv):
    nsm = len(SMALL_NAMES)

    def body(s_ref, vown_ref, cown_ref, vec_ref, cw_ref, *refs):
        ins = refs[:3 * nsm]
        outs = refs[3 * nsm:7 * nsm]
        loss_ref = refs[7 * nsm]
        me = s_ref[1]

        def summed(own_ref, table_ref, rows_per_dev, r0, nrows, lane, width):
            acc = None
            for k in range(8):
                piece = jnp.where(me == k, own_ref[r0:r0 + nrows, lane:lane + width],
                                  table_ref[k * rows_per_dev + r0:k * rows_per_dev + r0 + nrows, lane:lane + width])
                acc = piece if acc is None else acc + piece
            return acc

        def total(slot):
            row, lane, width = slot
            return summed(vown_ref, vec_ref, VEC_ROWS, row, 1, lane, width)

        loss_ref[...] = jnp.broadcast_to(total(VEC_SLOTS["loss"]), loss_ref.shape)
        for p, name in enumerate(SMALL_NAMES):
            w_ref, m_ref, v_ref = ins[3 * p:3 * p + 3]
            g_ref, d_ref, mo_ref, vo_ref = outs[4 * p:4 * p + 4]
            if name == "conv_w":
                g = jnp.zeros((KW, CW_LANES), F32)
                for s in range(N_CHIPS):
                    cand = summed(cown_ref, cw_ref, CW_ROWS, 0, KW, s * CW_LANES, CW_LANES)
                    g = jnp.where(s_ref[0] == s, cand, g)
            else:
                g = total(VEC_SLOTS[name])
            d, mm, vv = _adamw_math(w_ref[...], g, m_ref[...], v_ref[...])
            g_ref[...] = g
            d_ref[...] = d
            mo_ref[...] = mm
            vo_ref[...] = vv

    vmem = pl.BlockSpec(memory_space=pltpu.VMEM)
    flat = [a for t in wmv for a in t]
    out_shape = []
    for w, _, _ in wmv:
        out_shape += [jax.ShapeDtypeStruct(w.shape, F32)] * 4
    out_shape.append(jax.ShapeDtypeStruct((1, 128), F32))
    res = pl.pallas_call(
        body, name="small_update",
        in_specs=[pl.BlockSpec(memory_space=pltpu.SMEM)] + [vmem] * (4 + len(flat)),
        out_specs=[vmem] * len(out_shape), out_shape=out_shape,
    )(idx, vec_own, cw_own, vec_all, cw_all, *flat)
    return [tuple(res[4 * p:4 * p + 4]) for p in range(nsm)], res[4 * nsm]


SMALL_BIG = ("w_out", "w_attn_proj", "w_conv_proj")
WEIGHT_ORDER = ["g_mix_norm", "w_in", "b_in", "sinks", "conv_w", "conv_b", "ln_g", "ln_b",
                "w_attn_proj", "w_conv_proj", "b_conv_proj", "w_out", "g_ffn_norm", "w_ffn_in",
                "w_ffn_down", "g_final"]


def kernel(x, g_mix_norm, w_in, b_in, sinks, conv_w, conv_b, ln_g, ln_b, w_attn_proj, w_conv_proj, b_conv_proj, w_out, g_ffn_norm, w_ffn_in, w_ffn_down, g_final, loss_target, m_g_mix_norm, m_w_in, m_b_in, m_sinks, m_conv_w, m_conv_b, m_ln_g, m_ln_b, m_w_attn_proj, m_w_conv_proj, m_b_conv_proj, m_w_out, m_g_ffn_norm, m_w_ffn_in, m_w_ffn_down, m_g_final, v_g_mix_norm, v_w_in, v_b_in, v_sinks, v_conv_w, v_conv_b, v_ln_g, v_ln_b, v_w_attn_proj, v_w_conv_proj, v_b_conv_proj, v_w_out, v_g_ffn_norm, v_w_ffn_in, v_w_ffn_down, v_g_final):
    w = dict(g_mix_norm=g_mix_norm, w_in=w_in, b_in=b_in, sinks=sinks, conv_w=conv_w, conv_b=conv_b,
             ln_g=ln_g, ln_b=ln_b, w_attn_proj=w_attn_proj, w_conv_proj=w_conv_proj,
             b_conv_proj=b_conv_proj, w_out=w_out, g_ffn_norm=g_ffn_norm, w_ffn_in=w_ffn_in,
             w_ffn_down=w_ffn_down, g_final=g_final)
    m = dict(g_mix_norm=m_g_mix_norm, w_in=m_w_in, b_in=m_b_in, sinks=m_sinks, conv_w=m_conv_w,
             conv_b=m_conv_b, ln_g=m_ln_g, ln_b=m_ln_b, w_attn_proj=m_w_attn_proj,
             w_conv_proj=m_w_conv_proj, b_conv_proj=m_b_conv_proj, w_out=m_w_out,
             g_ffn_norm=m_g_ffn_norm, w_ffn_in=m_w_ffn_in, w_ffn_down=m_w_ffn_down, g_final=m_g_final)
    v = dict(g_mix_norm=v_g_mix_norm, w_in=v_w_in, b_in=v_b_in, sinks=v_sinks, conv_w=v_conv_w,
             conv_b=v_conv_b, ln_g=v_ln_g, ln_b=v_ln_b, w_attn_proj=v_w_attn_proj,
             w_conv_proj=v_w_conv_proj, b_conv_proj=v_b_conv_proj, w_out=v_w_out,
             g_ffn_norm=v_g_ffn_norm, w_ffn_in=v_w_ffn_in, w_ffn_down=v_w_ffn_down, g_final=v_g_final)

    c_idx = lax.axis_index("c").astype(jnp.int32).reshape(1)
    s_idx = (2 * lax.axis_index("x") + lax.axis_index("y")).astype(jnp.int32).reshape(1)

    out_g, out_d, out_m, out_v = {}, {}, {}, {}

    def gather_start(tag, shards):
        lands = [lax.empty((N_CHIPS,) + s.shape, s.dtype) for s in shards]
        return _chip_start("gather_start_" + tag, _gather_ends, GATHER_PEERS, shards, lands,
                           SHAKES.get("gather_" + tag))

    def gather_relay(tag, state, after, first=0, count=None):
        send_sems, recv_sems, shards, lands, _ = state
        last = len(shards) if count is None else first + count
        return _gather_relay("gather_relay_" + tag, send_sems, recv_sems, shards[first:last],
                             lands[first:last], after, first, SHAKES["relay_" + tag])

    def gather_finish(tag, relay, after):
        return _forward_wait("forward_wait_" + tag, relay[0], relay[1], relay[2], after)

    names_b = ["w_attn_proj", "w_conv_proj", "w_out", "w_ffn_in", "w_ffn_down"]
    big = {name: (w[name][0], m[name][0], v[name][0]) for name in names_b}
    big["w_in"] = (w_in[0].T, m_w_in[0].T, v_w_in[0].T)
    state_a = gather_start("a", [big["w_in"][0].astype(BF16), jnp.pad(conv_w[0], ((0, CW_ROWS - KW), (0, 0)))])
    state_b = gather_start("b", [(big[name][0] + state_a[4][0, 0]).astype(BF16) for name in names_b])
    got_a = gather_finish("a", gather_relay("a", state_a, [state_b[4]]), [])
    w_in_t_full = got_a[0].reshape(INW, D)
    conv_w_full = got_a[1].transpose(1, 0, 2).reshape(CW_ROWS, C)[:KW]

    xs, target = x[0], loss_target[0]
    g_final2 = g_final.reshape(1, D)
    h, qkv, glu, gl = _in_proj(xs, g_mix_norm, w_in_t_full, b_in)
    o, lse, probs = _attn_fwd(qkv, sinks)
    relay_1 = gather_relay("b1", state_b, [o], 0, 3)
    u, cact = _conv_fwd(glu, conv_w_full, conv_b, ln_g, ln_b, relay_1[3])
    w_ap4, w_cp4, w_out4 = gather_finish("b1", relay_1, [cact])
    w_out_full = w_out4.reshape(D, D)
    mixed = _mix_out(xs, o, cact, gl, w_ap4, w_cp4, b_conv_proj, w_out_full, relay_1[3])
    relay_2 = gather_relay("b2", state_b, [mixed[3]], 3, 1)
    ya, yc, mg, x1 = _mix_out(xs, o, cact, gl, w_ap4, w_cp4, b_conv_proj, w_out_full, relay_2[3], mixed)
    w_fi4, = gather_finish("b2", relay_2, [x1])
    h2, gu, act = _ffn_in_first(x1, g_ffn_norm, w_fi4, relay_2[3])
    relay_3 = gather_relay("b3", state_b, [h2], 4, 1)
    gu, act = _ffn_in_second(h2, w_fi4, gu, act, relay_3[3])
    w_dn4, = gather_finish("b3", relay_3, [act])
    w_dn_full = w_dn4.reshape(DFF, D)
    dx2, dx2b, dg_final, loss_part = _ffn_out_loss(x1, act, w_dn_full, g_final2, target)

    def exchange_start(tag, grads):
        lands = [lax.empty((N_CHIPS, g.shape[1] // 2, g.shape[2]), g.dtype) for g in grads]
        return _chip_start("pair_start_" + tag, _exchange_ends, PAIR_PEERS, grads, lands, SHAKES["pair_" + tag])

    def reduce_start(tag, names, exchange, after):
        send_sems, recv_sems, grads, lands, _ = exchange
        grads, from_sibling = _chip_wait("pair_wait_" + tag, _exchange_ends, send_sems, recv_sems, grads, lands, after)
        pair = {name: _pair_sum(c_idx, g, r, "pair_sum_" + name)
                for name, g, r in zip(names, grads, from_sibling) if name not in SMALL_BIG}
        small = [(g, r) for name, g, r in zip(names, grads, from_sibling) if name in SMALL_BIG]
        if small:
            sums = _pair_sum_small(c_idx, small, "pair_sum_small_" + tag)
            pair.update(zip([name for name in names if name in SMALL_BIG], sums))
        pair = [pair[name] for name in names]
        lands = [lax.empty((3,) + p.shape[1:], p.dtype) for p in pair]
        return _chip_start("chip_start_" + tag, _reduce_ends, REDUCE_PEERS, pair, lands, SHAKES["chip_" + tag])

    def reduce_sum(tag, names, state, after):
        send_sems, recv_sems, pair, lands, _ = state
        pair, lands = _chip_wait("chip_wait_" + tag, _reduce_ends, send_sems, recv_sems, pair, lands, after)
        mine = {name: _chip_sum(s_idx, p, r, "chip_sum_" + name)
                for name, p, r in zip(names, pair, lands) if name not in SMALL_BIG}
        small = [(p, r) for name, p, r in zip(names, pair, lands) if name in SMALL_BIG]
        if small:
            sums = _chip_sum_small(s_idx, small, "chip_sum_small_" + tag)
            mine.update(zip([name for name in names if name in SMALL_BIG], sums))
        mine = [mine[name] for name in names]
        others = [lax.empty(a.shape, a.dtype) for a in mine]
        return _chip_start("share_start_" + tag, _share_ends, PAIR_PEERS, mine, others, SHAKES["share_" + tag])

    def reduce_finish(tag, names, share, after):
        send_sems, recv_sems, mine, others, _ = share
        mine, others = _chip_wait("share_wait_" + tag, _share_ends, send_sems, recv_sems, mine, others, after)
        results = {name: _adamw(c_idx, big[name][0], g_mine, g_other, big[name][1], big[name][2], "adamw_" + name)
                   for name, g_mine, g_other in zip(names, mine, others) if name not in SMALL_BIG}
        small = [name for name in names if name in SMALL_BIG]
        if small:
            halves = dict(zip(names, zip(mine, others)))
            items = [(big[name][0], *halves[name], big[name][1], big[name][2]) for name in small]
            results.update(zip(small, _adamw_small(c_idx, items, "adamw_small_" + tag)))
        for name in names:
            res = results[name]
            if name == "w_in":
                res = [a.T for a in res]
            out_g[name], out_d[name], out_m[name], out_v[name] = [a[None] for a in res]

    dgu, dx1, dx1b, dg_ffn = _ffn_bwd(dx2, dx2b, gu, x1, g_ffn_norm, w_dn_full, w_fi4)
    names_0 = ["w_ffn_in", "w_ffn_down"]
    dya, dyc, dgl, do, dc, db_cp = _mix_bwd(dx1b, gl, ya, yc, w_out_full, w_ap4, w_cp4, relay_3[3])
    grads_0 = [_grad_w(h2, dgu, "grad_w_ffn_in", 512, FSH, True, dya),
               _grad_w(act, dx2b, "grad_w_ffn_down", 256, D, False)]
    exchange_0 = exchange_start("0", grads_0)
    names_1 = ["w_out", "w_attn_proj", "w_conv_proj"]
    grads_1 = [_grad_w(mg, dx1b, "grad_w_out", 512, D, False, exchange_0[4]),
               _grad_w(o, dya, "grad_w_attn_proj", 512, D, True, exchange_0[4]),
               _grad_w(cact, dyc, "grad_w_conv_proj", 512, D, True, exchange_0[4])]
    state_0 = reduce_start("0", names_0, exchange_0, grads_1)
    exchange_1 = exchange_start("1", grads_1)
    dq, dkv, dsinks = _attn_bwd(qkv, do, lse, probs, sinks, exchange_1[4], state_0[4])
    state_1 = reduce_start("1", names_1, exchange_1, [dq])
    dglu, dconv_w, dconv_b, dln_g, dln_b = _conv_bwd(glu, u, dc, conv_w_full, ln_g, ln_b, state_1[4])
    names_2 = ["w_in"]
    gw_in_t = _grad_w_in_t(h, dq, dkv, dglu, dgl)
    exchange_2 = exchange_start("2", [gw_in_t.reshape(N_CHIPS, INW // N_CHIPS, D)])
    share_0 = reduce_sum("0", names_0, state_0, [exchange_2[4]])
    state_2 = reduce_start("2", names_2, exchange_2, [share_0[4]])
    grad_x, dg_mix, db_in = _in_proj_bwd(dq, dkv, dglu, dgl, xs, dx1, g_mix_norm, w_in_t_full, state_2[4])

    gs = {"g_mix_norm": dg_mix, "b_in": db_in, "sinks": dsinks[:, 0].reshape(1, NQ),
          "conv_w": dconv_w, "conv_b": dconv_b, "ln_g": dln_g, "ln_b": dln_b,
          "b_conv_proj": db_cp, "g_ffn_norm": dg_ffn, "g_final": dg_final}
    blocks = list(_pack_small(gs, loss_part[0, 0]))
    tables = [lax.empty((8 * b.shape[0], b.shape[1]), b.dtype) for b in blocks]
    small = _chip_start("small_start", _small_ends, SMALL_PEERS, blocks, tables)

    share_1 = reduce_sum("1", names_1, state_1, [small[4]])
    reduce_finish("0", names_0, share_0, [share_1[4]])
    blocks, tables = _chip_wait("small_wait", _small_ends, small[0], small[1], small[2], small[3],
                                [out_d["w_ffn_down"]])
    me = (4 * lax.axis_index("x") + 2 * lax.axis_index("y") + lax.axis_index("c")).astype(jnp.int32)

    def view(a, name):
        if name == "conv_w":
            return a[0]
        if name == "g_final":
            return a.reshape(1, D)
        return a

    wmv = [(view(w[name], name), view(m[name], name), view(v[name], name)) for name in SMALL_NAMES]
    small_out, loss_row = _small_update(jnp.concatenate([s_idx, me.reshape(1)]), blocks[0], blocks[1],
                                        tables[0], tables[1], wmv)
    for name, (g, d, mm, vv) in zip(SMALL_NAMES, small_out):
        shape = w[name].shape
        out_g[name], out_d[name], out_m[name], out_v[name] = (
            g.reshape(shape), d.reshape(shape), mm.reshape(shape), vv.reshape(shape))

    share_2 = reduce_sum("2", names_2, state_2, [loss_row])
    reduce_finish("1", names_1, share_1, [share_2[4]])
    reduce_finish("2", names_2, share_2, [out_d["w_conv_proj"]])

    loss = loss_row[0, 0]
    return (loss, grad_x[None], *[out_g[k] for k in WEIGHT_ORDER], *[out_d[k] for k in WEIGHT_ORDER],
            *[out_m[k] for k in WEIGHT_ORDER], *[out_v[k] for k in WEIGHT_ORDER])
```
